```python
import jax, jax.numpy as jnp
from jax import lax
import numpy as np

D_MODEL = 1024
BATCH = 32
SEQ = 256
DEPTH = 2
DEC_BATCH = 2
DEC_SEQ = 1024
PAST_LEN = 512

GRID_W = 64
N_EVEN = (DEPTH + 1) // 2
N_ODD = DEPTH // 2
D_FF = 4 * D_MODEL
EPS = 1e-6
D_CONF = D_MODEL // 2
CONF_WIDTH = 31
D_SHORT = D_MODEL // 2
SHORT_WIDTH = 3
EVEN_SPLITS = (D_CONF, D_CONF, D_SHORT, D_SHORT, D_SHORT)
HEAD_DIM = 64
N_Q_HEADS = (D_MODEL // 2) // HEAD_DIM
N_KV_HEADS = 2
Q_PER_KV = N_Q_HEADS // N_KV_HEADS
D_ATT = N_Q_HEADS * HEAD_DIM
D_KV = N_KV_HEADS * HEAD_DIM
WINDOW = 128
BLOCK = 128
ROPE_BASE = 10000.0
N_M_HEADS = (D_MODEL // 2) // HEAD_DIM
D_MLSTM = N_M_HEADS * HEAD_DIM
N_GATES = 4
CHUNK = 128
FORGET_BIAS = 3.0
ODD_SPLITS = (D_ATT, D_KV, D_KV, D_MLSTM, D_MLSTM, D_MLSTM, D_MLSTM, N_GATES * N_M_HEADS)
D_IN_EVEN = sum(EVEN_SPLITS)
D_IN_ODD = sum(ODD_SPLITS)
ATT_SCALE = HEAD_DIM ** -0.5

kernel_name = 'hybrid_diffusion_prefix_step'


def split_cols(x, sizes):
    return jnp.split(x, [int(s) for s in np.cumsum(sizes)[:-1]], axis=-1)


def rmsnorm(x, g):
    xf = x.astype(jnp.float32)
    y = xf * lax.rsqrt(jnp.mean(xf * xf, axis=-1, keepdims=True) + EPS)
    return (y * g.astype(jnp.float32)).astype(x.dtype)


def layernorm(x, g, b):
    xf = x.astype(jnp.float32)
    mu = jnp.mean(xf, axis=-1, keepdims=True)
    var = jnp.mean(jnp.square(xf - mu), axis=-1, keepdims=True)
    y = (xf - mu) * lax.rsqrt(var + EPS) * g.astype(jnp.float32) + b.astype(jnp.float32)
    return y.astype(x.dtype)


def depthwise_conv(x, w):
    k = w.shape[0]
    return lax.conv_general_dilated(x, w[:, None, :].astype(x.dtype), (1,), [(k // 2, k // 2)],
                                    dimension_numbers=('NWC', 'WIO', 'NWC'),
                                    feature_group_count=x.shape[-1])


def modulation(cond, mod_w, mod_b):
    return [m[:, None, :] for m in jnp.split(jax.nn.silu(cond) @ mod_w + mod_b, 6, axis=-1)]


def modulate_in(x, g, shift, scale):
    return rmsnorm(x, g) * (1 + scale) + shift


def gated_out(x, h, g, gate):
    return x + gate * rmsnorm(h, g)


def sq_relu_mlp(h, w1, w2):
    return jnp.square(jax.nn.relu(h @ w1)) @ w2


def even_mixer(h, w_in, conv_a_w, conv_a_b, ln_g, ln_b, conv_b_w, w_out):
    a_val, a_gate, b_gate, c_gate, b_x = split_cols(h @ w_in, EVEN_SPLITS)
    a = a_val * jax.nn.sigmoid(a_gate)
    a = depthwise_conv(a, conv_a_w) + conv_a_b
    a = jax.nn.silu(layernorm(a, ln_g, ln_b))
    s = b_gate * depthwise_conv(c_gate * b_x, conv_b_w)
    return jnp.concatenate([a, s], axis=-1) @ w_out


def axial_rope_tables(t):
    rows = t // GRID_W
    row = jnp.broadcast_to(jnp.arange(rows)[:, None], (rows, GRID_W)).reshape(t).astype(jnp.float32)
    col = jnp.broadcast_to(jnp.arange(GRID_W)[None, :], (rows, GRID_W)).reshape(t).astype(jnp.float32)
    n_freq = HEAD_DIM // 4
    inv_freq = ROPE_BASE ** (-jnp.arange(n_freq, dtype=jnp.float32) / n_freq)
    ang = jnp.concatenate([row[:, None] * inv_freq, col[:, None] * inv_freq], axis=-1)
    return jnp.cos(ang), jnp.sin(ang)


def apply_rope(x, cos, sin):
    half = HEAD_DIM // 2
    xf = x.astype(jnp.float32)
    x1, x2 = xf[..., :half], xf[..., half:]
    c, s = cos[:, None, :], sin[:, None, :]
    return jnp.concatenate([x1 * c - x2 * s, x1 * s + x2 * c], axis=-1).astype(x.dtype)


def sink_column(sink, shape):
    s = sink.reshape(N_KV_HEADS, Q_PER_KV).astype(jnp.float32)[None, :, :, None, None]
    return jnp.broadcast_to(s, shape[:-1] + (1,))


def context_attention(q, k, v, sink):
    b, l = q.shape[:2]
    nb = l // BLOCK
    qb = q.reshape(b, nb, BLOCK, N_KV_HEADS, Q_PER_KV, HEAD_DIM).transpose(1, 0, 2, 3, 4, 5)

    def one_block(qblk):
        s = jnp.einsum('bqhgd,bhkd->bhgqk', qblk, k).astype(jnp.float32) * ATT_SCALE
        p = jax.nn.softmax(jnp.concatenate([s, sink_column(sink, s.shape)], axis=-1), axis=-1)
        return jnp.einsum('bhgqk,bhkd->bqhgd', p[..., :-1].astype(v.dtype), v)

    out = lax.map(one_block, qb)
    return out.transpose(1, 0, 2, 3, 4, 5).reshape(b, l, D_ATT)


def latent_attention(q, k, v, k_ctx, v_ctx, sink):
    b, t = q.shape[:2]
    nb = t // BLOCK
    lc = k_ctx.shape[2]
    qb = q.reshape(b, nb, BLOCK, N_KV_HEADS, Q_PER_KV, HEAD_DIM).transpose(1, 0, 2, 3, 4, 5)
    pad = ((0, 0), (BLOCK, BLOCK), (0, 0), (0, 0))

    def band(a):
        ap = jnp.pad(a, pad).reshape(b, nb + 2, BLOCK, N_KV_HEADS, HEAD_DIM)
        return jnp.concatenate([ap[:, :-2], ap[:, 1:-1], ap[:, 2:]], axis=2).transpose(1, 0, 2, 3, 4)

    qi = jnp.arange(BLOCK)
    kj = jnp.arange(3 * BLOCK)
    band_ok = jnp.abs(kj[None, :] - BLOCK - qi[:, None]) <= WINDOW
    key_pos = (jnp.arange(nb)[:, None] - 1) * BLOCK + kj[None, :]
    mask = band_ok[None] & ((key_pos >= 0) & (key_pos < t))[:, None, :]

    def one_block(args):
        qblk, kblk, vblk, mblk = args
        s_ctx = jnp.einsum('bqhgd,bhkd->bhgqk', qblk, k_ctx).astype(jnp.float32) * ATT_SCALE
        s_lat = jnp.einsum('bqhgd,bkhd->bhgqk', qblk, kblk).astype(jnp.float32) * ATT_SCALE
        s_lat = jnp.where(mblk, s_lat, -jnp.inf)
        logits = jnp.concatenate([s_ctx, s_lat, sink_column(sink, s_lat.shape)], axis=-1)
        p = jax.nn.softmax(logits, axis=-1).astype(v.dtype)
        return (jnp.einsum('bhgqk,bhkd->bqhgd', p[..., :lc], v_ctx)
                + jnp.einsum('bhgqk,bkhd->bqhgd', p[..., lc:lc + 3 * BLOCK], vblk))

    out = lax.map(one_block, (qb, band(k), band(v), mask))
    return out.transpose(1, 0, 2, 3, 4, 5).reshape(b, t, D_ATT)


def mlstm_scan(q, k, v, ig, lf, c0, n0, m0):
    bsz, nh, t, dh = q.shape
    nc = t // CHUNK
    causal = jnp.tril(jnp.ones((CHUNK, CHUNK), dtype=bool))

    def chunks(a):
        return jnp.moveaxis(a.reshape((bsz, nh, nc, CHUNK) + a.shape[3:]), 2, 0)

    def step(carry, xs):
        c_prev, n_prev, m_prev = carry
        qc, kc, vc, igc, lfc = xs
        bcum = jnp.cumsum(lfc, axis=-1)
        log_d = jnp.where(causal, bcum[..., :, None] - bcum[..., None, :] + igc[..., None, :], -jnp.inf)
        log_inter = bcum + m_prev[..., None]
        m_t = jnp.maximum(log_inter, jnp.max(log_d, axis=-1))
        dmat = jnp.exp(log_d - m_t[..., None])
        w_inter = jnp.exp(log_inter - m_t)
        s = jnp.einsum('bhtd,bhsd->bhts', qc, kc) * dmat
        numer = (jnp.einsum('bhts,bhse->bhte', s, vc)
                 + w_inter[..., None] * jnp.einsum('bhtd,bhde->bhte', qc, c_prev))
        denom = jnp.sum(s, axis=-1) + w_inter * jnp.einsum('bhtd,bhd->bht', qc, n_prev)
        h = numer / jnp.maximum(jnp.abs(denom), jnp.exp(-m_t))[..., None]
        b_last = bcum[..., -1]
        log_e = b_last[..., None] - bcum + igc
        m_new = jnp.maximum(b_last + m_prev, jnp.max(log_e, axis=-1))
        e = jnp.exp(log_e - m_new[..., None])
        decay = jnp.exp(b_last + m_prev - m_new)
        c_new = decay[..., None, None] * c_prev + jnp.einsum('bhs,bhsd,bhse->bhde', e, kc, vc)
        n_new = decay[..., None] * n_prev + jnp.einsum('bhs,bhsd->bhd', e, kc)
        return (c_new, n_new, m_new), h

    init = (c0.astype(jnp.float32), n0.astype(jnp.float32), m0.astype(jnp.float32))
    (c, n, m), h = lax.scan(step, init, tuple(chunks(a) for a in (q, k, v, ig, lf)))
    return jnp.moveaxis(h, 0, 2).reshape(bsz, nh, t, dh), (c, n, m)


def mlstm_bidir(qm, km, vm, g, gate_b, init_fwd, init_bwd):
    g = g + gate_b.astype(jnp.float32)[None, :, :, None]
    ig_f, lf_f = g[:, 0], jax.nn.log_sigmoid(g[:, 1])
    ig_b, lf_b = g[:, 2], jax.nn.log_sigmoid(g[:, 3])
    h_f, st_f = mlstm_scan(qm, km, vm, ig_f, lf_f, *init_fwd)
    flip = lambda a: jnp.flip(a, axis=2)
    h_b, st_b = mlstm_scan(flip(qm), flip(km), flip(vm), flip(ig_b), flip(lf_b), *init_bwd)
    return h_f + flip(h_b), st_f, st_b


def mlstm_readout(h, om, hnorm_g, dtype):
    bsz, nh, t, dh = h.shape
    h = h.transpose(0, 2, 1, 3)
    h = h * lax.rsqrt(jnp.mean(h * h, axis=-1, keepdims=True) + EPS) * hnorm_g.astype(jnp.float32).reshape(nh, dh)
    return (h.reshape(bsz, t, D_MLSTM) * jax.nn.sigmoid(om.astype(jnp.float32))).astype(dtype)


def odd_project(h, w_in):
    bsz, t = h.shape[:2]
    qa, ka, va, qm, km, vm, om, g = split_cols(h @ w_in, ODD_SPLITS)
    heads = lambda a: a.reshape(bsz, t, N_M_HEADS, HEAD_DIM).transpose(0, 2, 1, 3).astype(jnp.float32)
    g = g.reshape(bsz, t, N_GATES, N_M_HEADS).transpose(0, 2, 3, 1).astype(jnp.float32)
    return (qa.reshape(bsz, t, N_Q_HEADS, HEAD_DIM), ka.reshape(bsz, t, N_KV_HEADS, HEAD_DIM),
            va.reshape(bsz, t, N_KV_HEADS, HEAD_DIM), heads(qm), heads(km) * HEAD_DIM ** -0.5, heads(vm), om, g)


def odd_mixer_context(h, w_in, sink, gate_b, hnorm_g, w_out):
    bsz = h.shape[0]
    qa, ka, va, qm, km, vm, om, g = odd_project(h, w_in)
    k_ctx = ka.transpose(0, 2, 1, 3)
    v_ctx = va.transpose(0, 2, 1, 3)
    att = context_attention(qa, k_ctx, v_ctx, sink)
    zero = (jnp.zeros((bsz, N_M_HEADS, HEAD_DIM, HEAD_DIM), jnp.float32),
            jnp.zeros((bsz, N_M_HEADS, HEAD_DIM), jnp.float32),
            jnp.zeros((bsz, N_M_HEADS), jnp.float32))
    hm, st_f, st_b = mlstm_bidir(qm, km, vm, g, gate_b, zero, zero)
    y = jnp.concatenate([att, mlstm_readout(hm, om, hnorm_g, h.dtype)], axis=-1) @ w_out
    return y, (k_ctx, v_ctx, st_f[0], st_f[1], st_f[2], st_b[0], st_b[1], st_b[2])


def odd_mixer_latent(h, k_ctx, v_ctx, st_f, st_b, w_in, sink, gate_b, hnorm_g, w_out):
    qa, ka, va, qm, km, vm, om, g = odd_project(h, w_in)
    cos, sin = axial_rope_tables(h.shape[1])
    att = latent_attention(apply_rope(qa, cos, sin), apply_rope(ka, cos, sin), va,
                           k_ctx.astype(h.dtype), v_ctx.astype(h.dtype), sink)
    hm, _, _ = mlstm_bidir(qm, km, vm, g, gate_b, st_f, st_b)
    return jnp.concatenate([att, mlstm_readout(hm, om, hnorm_g, h.dtype)], axis=-1) @ w_out


def setup_inputs(seed: int = 0) -> dict:
    key = jax.random.key(seed)
    ks = list(jax.random.split(key, 40))
    cnt = [0]

    def nrm(shape, s=1.0):
        cnt[0] += 1
        return s * jax.random.normal(ks[cnt[0] - 1], shape, jnp.float32)

    return {
        'x_prompt': nrm((BATCH, SEQ, D_MODEL)),
        'x_sample': nrm((DEC_BATCH, DEC_SEQ, D_MODEL)),
        'c': nrm((DEC_BATCH, D_MODEL)),
        'cache_k': nrm((DEC_BATCH, N_ODD, N_KV_HEADS, PAST_LEN, HEAD_DIM)),
        'cache_v': nrm((DEC_BATCH, N_ODD, N_KV_HEADS, PAST_LEN, HEAD_DIM)),
        'state_c_fwd': nrm((DEC_BATCH, N_ODD, N_M_HEADS, HEAD_DIM, HEAD_DIM), 0.5),
        'state_n_fwd': nrm((DEC_BATCH, N_ODD, N_M_HEADS, HEAD_DIM), 0.5),
        'state_m_fwd': nrm((DEC_BATCH, N_ODD, N_M_HEADS)),
        'state_c_bwd': nrm((DEC_BATCH, N_ODD, N_M_HEADS, HEAD_DIM, HEAD_DIM), 0.5),
        'state_n_bwd': nrm((DEC_BATCH, N_ODD, N_M_HEADS, HEAD_DIM), 0.5),
        'state_m_bwd': nrm((DEC_BATCH, N_ODD, N_M_HEADS)),
        'c_ctx': nrm((D_MODEL,)),
        'mod_w': nrm((DEPTH, D_MODEL, 6 * D_MODEL), 0.5 * D_MODEL ** -0.5),
        'mod_b': nrm((DEPTH, 6 * D_MODEL), 0.02),
        'norm_g': 1.0 + nrm((DEPTH, 4, D_MODEL), 0.05),
        'mlp_w1': nrm((DEPTH, D_MODEL, D_FF), D_MODEL ** -0.5),
        'mlp_w2': nrm((DEPTH, D_FF, D_MODEL), D_FF ** -0.5),
        'even_in_w': nrm((N_EVEN, D_MODEL, D_IN_EVEN), D_MODEL ** -0.5),
        'conv_a_w': nrm((N_EVEN, CONF_WIDTH, D_CONF), CONF_WIDTH ** -0.5),
        'conv_a_b': nrm((N_EVEN, D_CONF), 0.02),
        'ln_a_g': 1.0 + nrm((N_EVEN, D_CONF), 0.05),
        'ln_a_b': nrm((N_EVEN, D_CONF), 0.02),
        'conv_b_w': nrm((N_EVEN, SHORT_WIDTH, D_SHORT), SHORT_WIDTH ** -0.5),
        'even_out_w': nrm((N_EVEN, D_CONF + D_SHORT, D_MODEL), (D_CONF + D_SHORT) ** -0.5),
        'odd_in_w': nrm((N_ODD, D_MODEL, D_IN_ODD), D_MODEL ** -0.5),
        'attn_sink': nrm((N_ODD, N_Q_HEADS)),
        'gate_b': nrm((N_ODD, N_GATES, N_M_HEADS), 0.1)
                  + jnp.array([0.0, FORGET_BIAS, 0.0, FORGET_BIAS], jnp.float32)[None, :, None],
        'hnorm_g': 1.0 + nrm((N_ODD, D_MLSTM), 0.05),
        'odd_out_w': nrm((N_ODD, D_ATT + D_MLSTM, D_MODEL), (D_ATT + D_MLSTM) ** -0.5),
    }


def reference(x_prompt, x_sample, c, cache_k, cache_v, state_c_fwd, state_n_fwd, state_m_fwd,
              state_c_bwd, state_n_bwd, state_m_bwd, c_ctx, mod_w, mod_b, norm_g, mlp_w1, mlp_w2,
              even_in_w, conv_a_w, conv_a_b, ln_a_g, ln_a_b, conv_b_w, even_out_w,
              odd_in_w, attn_sink, gate_b, hnorm_g, odd_out_w):
    yp, ys = x_prompt, x_sample
    new = [[] for _ in range(8)]
    for layer in range(DEPTH):
        j = layer // 2
        g4 = norm_g[layer]
        mp_ = modulation(c_ctx[None, :], mod_w[layer], mod_b[layer])
        ms_ = modulation(c, mod_w[layer], mod_b[layer])
        hp = modulate_in(yp, g4[0], mp_[0], mp_[1])
        hs = modulate_in(ys, g4[0], ms_[0], ms_[1])
        if layer % 2 == 0:
            ep = (even_in_w[j], conv_a_w[j], conv_a_b[j], ln_a_g[j], ln_a_b[j], conv_b_w[j], even_out_w[j])
            op_p = even_mixer(hp, *ep)
            op_s = even_mixer(hs, *ep)
        else:
            od = (odd_in_w[j], attn_sink[j], gate_b[j], hnorm_g[j], odd_out_w[j])
            op_p, st = odd_mixer_context(hp, *od)
            for lst, a in zip(new, st):
                lst.append(a)
            op_s = odd_mixer_latent(hs, cache_k[:, j], cache_v[:, j],
                                    (state_c_fwd[:, j], state_n_fwd[:, j], state_m_fwd[:, j]),
                                    (state_c_bwd[:, j], state_n_bwd[:, j], state_m_bwd[:, j]), *od)
        yp = gated_out(yp, op_p, g4[1], mp_[2])
        ys = gated_out(ys, op_s, g4[1], ms_[2])
        hp = modulate_in(yp, g4[2], mp_[3], mp_[4])
        hs = modulate_in(ys, g4[2], ms_[3], ms_[4])
        yp = gated_out(yp, sq_relu_mlp(hp, mlp_w1[layer], mlp_w2[layer]), g4[3], mp_[5])
        ys = gated_out(ys, sq_relu_mlp(hs, mlp_w1[layer], mlp_w2[layer]), g4[3], ms_[5])
    s = [jnp.stack(lst, axis=1) for lst in new]
    return (yp, ys, s[0], s[1], s[2], s[3], s[4], s[5], s[6], s[7])
```

```python
import functools

import jax
import jax.numpy as jnp
from jax import lax
from jax.experimental import pallas as pl
from jax.experimental.pallas import tpu as pltpu

F32 = jnp.float32
BF16 = jnp.bfloat16

D_MODEL = 1024
D_FF = 4 * D_MODEL
EPS = 1e-6
D_HALF = D_MODEL // 2
CONF_WIDTH = 31
CONF_HALO = 16
HEAD_DIM = 64
N_HEADS = 8
N_PAIRS = N_HEADS // 2
N_KV = 2
LANES = 128
CHUNK = 128
WINDOW = 128
GRID_W = 64
ROPE_BASE = 10000.0
ATT_SCALE = HEAD_DIM ** -0.5
ROW_CHUNK = 256
COND_ROWS = 8
VMEM_LIMIT = 56 * 1024 * 1024


def _dot(a, b):
    return jnp.dot(a, b, preferred_element_type=F32)


def _dot_nt(a, b):
    return lax.dot_general(a, b, (((1,), (1,)), ((), ())), preferred_element_type=F32)


def _rms(x, g):
    return x * lax.rsqrt(jnp.mean(x * x, axis=-1, keepdims=True) + EPS) * g


def _norm_mod(x, g, shift, scale):
    return _rms(x, g) * (1.0 + scale) + shift


def _params(n_grid):
    return pltpu.CompilerParams(dimension_semantics=("arbitrary",) * n_grid, vmem_limit_bytes=VMEM_LIMIT)


def _const_spec(shape):
    zeros = (0,) * len(shape)
    return pl.BlockSpec(shape, lambda *_: zeros, pipeline_mode=pl.Buffered(1))


def _mod_kernel(cond_ref, w_ref, b_ref, o_ref):
    s = jax.nn.silu(cond_ref[...]).astype(BF16)
    o_ref[...] = _dot(s, w_ref[...].astype(BF16)) + b_ref[...]


def _modulation(cond, mod_w, mod_b):
    depth = mod_w.shape[0]
    n_out = mod_w.shape[2]
    tn = D_MODEL
    out = pl.pallas_call(
        _mod_kernel,
        grid=(depth, n_out // tn),
        in_specs=[
            pl.BlockSpec((COND_ROWS, D_MODEL), lambda l, j: (0, 0)),
            pl.BlockSpec((None, D_MODEL, tn), lambda l, j: (l, 0, j)),
            pl.BlockSpec((None, 1, tn), lambda l, j: (l, 0, j)),
        ],
        out_specs=pl.BlockSpec((None, COND_ROWS, tn), lambda l, j: (l, 0, j)),
        out_shape=jax.ShapeDtypeStruct((depth, COND_ROWS, n_out), F32),
        compiler_params=_params(2),
        name="modulation",
    )(cond, mod_w, mod_b.reshape(depth, 1, n_out))
    return out.reshape(depth, COND_ROWS, 6, D_MODEL)


def _even_kernel(xp_ref, xc_ref, xn_ref, mod_ref, ng_ref, win_ref, caw_ref, cab_ref, lng_ref, lnb_ref,
                 cbw_ref, wout_ref, o_ref, apad, cpad, bgs, zs, *, n_chunks):
    c = pl.program_id(1)
    rows = ROW_CHUNK + 2 * CONF_HALO
    shift1, scale1, gate1 = mod_ref[0:1, :], mod_ref[1:2, :], mod_ref[2:3, :]
    xc = xc_ref[...]
    xh = jnp.concatenate([xp_ref[...], xc, xn_ref[...]], axis=0)
    h = _norm_mod(xh, ng_ref[0:1, :], shift1, scale1).astype(BF16)
    ri = lax.broadcasted_iota(jnp.int32, (rows, D_HALF), 0)
    lo = jnp.where(c == 0, CONF_HALO, 0)
    hi = jnp.where(c == n_chunks - 1, CONF_HALO + ROW_CHUNK, rows)
    inside = (ri >= lo) & (ri < hi)
    a = _dot(h, win_ref[:, 0:D_HALF]) * jax.nn.sigmoid(_dot(h, win_ref[:, D_HALF:2 * D_HALF]))
    apad[...] = jnp.where(inside, a, 0.0)
    cx = _dot(h, win_ref[:, 3 * D_HALF:4 * D_HALF]) * _dot(h, win_ref[:, 4 * D_HALF:5 * D_HALF])
    cpad[...] = jnp.where(inside, cx, 0.0)
    bgs[...] = _dot(h[CONF_HALO:CONF_HALO + ROW_CHUNK], win_ref[:, 2 * D_HALF:3 * D_HALF])

    sub = 32
    for j in range(ROW_CHUNK // sub):
        r0 = j * sub
        acc = caw_ref[0:1, :] * apad[r0 + 1:r0 + 1 + sub, :]
        for k in range(1, CONF_WIDTH):
            acc = acc + caw_ref[k:k + 1, :] * apad[r0 + 1 + k:r0 + 1 + k + sub, :]
        acc = acc + cab_ref[...]
        mu = jnp.mean(acc, axis=-1, keepdims=True)
        dlt = acc - mu
        var = jnp.mean(dlt * dlt, axis=-1, keepdims=True)
        a_out = jax.nn.silu(dlt * lax.rsqrt(var + EPS) * lng_ref[...] + lnb_ref[...])
        zs[r0:r0 + sub, 0:D_HALF] = a_out.astype(BF16)
        base = r0 + CONF_HALO - 1
        sc = (cbw_ref[0:1, :] * cpad[base:base + sub, :]
              + cbw_ref[1:2, :] * cpad[base + 1:base + 1 + sub, :]
              + cbw_ref[2:3, :] * cpad[base + 2:base + 2 + sub, :])
        zs[r0:r0 + sub, D_HALF:D_MODEL] = (bgs[r0:r0 + sub, :] * sc).astype(BF16)

    o = _dot(zs[...], wout_ref[...])
    o_ref[...] = xc + gate1 * _rms(o, ng_ref[1:2, :])


def _even_layer(x, modv, cond_base, cond_stride, ng, win, caw, cab, lng, lnb, cbw, wout):
    b, t, _ = x.shape
    n_chunks = t // ROW_CHUNK
    hpc = ROW_CHUNK // CONF_HALO
    n_halo_blocks = t // CONF_HALO
    rows = ROW_CHUNK + 2 * CONF_HALO
    kern = functools.partial(_even_kernel, n_chunks=n_chunks)
    return pl.pallas_call(
        kern,
        grid=(b, n_chunks),
        in_specs=[
            pl.BlockSpec((None, CONF_HALO, D_MODEL), lambda i, c: (i, jnp.maximum(c * hpc - 1, 0), 0)),
            pl.BlockSpec((None, ROW_CHUNK, D_MODEL), lambda i, c: (i, c, 0)),
            pl.BlockSpec((None, CONF_HALO, D_MODEL),
                         lambda i, c: (i, jnp.minimum((c + 1) * hpc, n_halo_blocks - 1), 0)),
            pl.BlockSpec((None, 6, D_MODEL), lambda i, c: (cond_base + cond_stride * i, 0, 0)),
            _const_spec((4, D_MODEL)),
            _const_spec((D_MODEL, 5 * D_HALF)),
            _const_spec((CONF_WIDTH, D_HALF)),
            _const_spec((1, D_HALF)),
            _const_spec((1, D_HALF)),
            _const_spec((1, D_HALF)),
            _const_spec((3, D_HALF)),
            _const_spec((D_MODEL, D_MODEL)),
        ],
        out_specs=pl.BlockSpec((None, ROW_CHUNK, D_MODEL), lambda i, c: (i, c, 0)),
        out_shape=jax.ShapeDtypeStruct(x.shape, F32),
        scratch_shapes=[
            pltpu.VMEM((rows, D_HALF), F32),
            pltpu.VMEM((rows, D_HALF), F32),
            pltpu.VMEM((ROW_CHUNK, D_HALF), F32),
            pltpu.VMEM((ROW_CHUNK, D_MODEL), BF16),
        ],
        compiler_params=_params(2),
        name="even_mixer",
    )(x, x, x, modv, ng, win, caw, cab, lng, lnb, cbw, wout)


def _mlp_kernel(x_ref, mod_ref, ng_ref, w1_ref, w2_ref, o_ref):
    x = x_ref[...]
    h = _norm_mod(x, ng_ref[2:3, :], mod_ref[3:4, :], mod_ref[4:5, :]).astype(BF16)
    acc = None
    for c in range(D_FF // D_MODEL):
        cols = slice(c * D_MODEL, (c + 1) * D_MODEL)
        hid = jnp.square(jnp.maximum(_dot(h, w1_ref[:, cols]), 0.0)).astype(BF16)
        part = _dot(hid, w2_ref[cols, :])
        acc = part if acc is None else acc + part
    o_ref[...] = x + mod_ref[5:6, :] * _rms(acc, ng_ref[3:4, :])


def _mlp_layer(x, modv, cond_base, cond_stride, ng, w1, w2):
    b, t, _ = x.shape
    tm = 512
    assert (b * t) % tm == 0 and (cond_stride == 0 or t % tm == 0)
    tiles_per_seq = max(t // tm, 1)
    x2 = x.reshape(b * t, D_MODEL)
    out = pl.pallas_call(
        _mlp_kernel,
        grid=(b * t // tm,),
        in_specs=[
            pl.BlockSpec((tm, D_MODEL), lambda i: (i, 0)),
            pl.BlockSpec((None, 6, D_MODEL), lambda i: (cond_base + cond_stride * (i // tiles_per_seq), 0, 0)),
            _const_spec((4, D_MODEL)),
            _const_spec((D_MODEL, D_FF)),
            _const_spec((D_FF, D_MODEL)),
        ],
        out_specs=pl.BlockSpec((tm, D_MODEL), lambda i: (i, 0)),
        out_shape=jax.ShapeDtypeStruct(x2.shape, F32),
        compiler_params=_params(1),
        name="mlp",
    )(x2, modv, ng, w1, w2)
    return out.reshape(b, t, D_MODEL)


def _log_sigmoid(x):
    return jnp.minimum(x, 0.0) - jnp.log(1.0 + jnp.exp(-jnp.abs(x)))


def _rot_half(x, first_half):
    return jnp.where(first_half, pltpu.roll(x, 96, axis=1), pltpu.roll(x, 32, axis=1))


def _odd_kernel(*refs, t, lc, latent):
    if latent:
        (x_ref, mod_ref, ng_ref, win_ref, wg_ref, gb_ref, sink_ref, hn_ref, wout_ref,
         cos_ref, sin_ref, kc_ref, vc_ref, cin_ref, nin_ref, min_ref,
         o_ref,
         qa_s, qb_s, kk_s, vv_s, qma_s, qmb_s, qmf_s, km_s, kmb_s, vm_s, om_s, g_s, z_s, hf_s, hb_s,
         c_s, n_s, m_s) = refs
    else:
        (x_ref, mod_ref, ng_ref, win_ref, wg_ref, gb_ref, sink_ref, hn_ref, wout_ref,
         o_ref, ko_ref, vo_ref, co_ref, no_ref, mo_ref,
         qa_s, qb_s, kk_s, vv_s, qma_s, qmb_s, qmf_s, km_s, kmb_s, vm_s, om_s, g_s, z_s, hf_s, hb_s,
         c_s, n_s, m_s) = refs

    n_blocks = t // CHUNK
    pad = CHUNK if latent else 0
    shift1, scale1, gate1 = mod_ref[0:1, :], mod_ref[1:2, :], mod_ref[2:3, :]
    lane = lax.broadcasted_iota(jnp.int32, (1, LANES), 1)
    left = lane < HEAD_DIM
    first_half = (lane % HEAD_DIM) < (HEAD_DIM // 2)

    if latent:
        zeros_pad = jnp.zeros((CHUNK, LANES), BF16)
        for kv in range(N_KV):
            kk_s[kv, 0:CHUNK, :] = zeros_pad
            kk_s[kv, CHUNK + t:2 * CHUNK + t, :] = zeros_pad
            vv_s[kv, 0:CHUNK, :] = zeros_pad
            vv_s[kv, CHUNK + t:2 * CHUNK + t, :] = zeros_pad

    def project(rc, carry):
        r0 = pl.multiple_of(rc * ROW_CHUNK, ROW_CHUNK)
        rows = pl.ds(r0, ROW_CHUNK)
        krows = pl.ds(r0 + pad, ROW_CHUNK)
        h = _norm_mod(x_ref[rows, :], ng_ref[0:1, :], shift1, scale1).astype(BF16)
        q = _dot(h, win_ref[:, 0:D_HALF])
        kv2 = _dot(h, win_ref[:, D_HALF:D_HALF + 2 * LANES])
        ka, va = kv2[:, 0:LANES], kv2[:, LANES:2 * LANES]
        if latent:
            cs, sn = cos_ref[rows, :], sin_ref[rows, :]
            ka = ka * cs + _rot_half(ka, first_half) * sn
        else:
            ko_ref[rows, :] = ka
            vo_ref[rows, :] = va
        kr = pltpu.roll(ka, HEAD_DIM, axis=1)
        vr = pltpu.roll(va, HEAD_DIM, axis=1)
        kk_s[0, krows, :] = jnp.where(left, ka, kr).astype(BF16)
        kk_s[1, krows, :] = jnp.where(left, kr, ka).astype(BF16)
        vv_s[0, krows, :] = jnp.where(left, va, vr).astype(BF16)
        vv_s[1, krows, :] = jnp.where(left, vr, va).astype(BF16)
        for p in range(N_PAIRS):
            cols = slice(p * LANES, (p + 1) * LANES)
            qp = q[:, cols]
            if latent:
                qp = qp * cs + _rot_half(qp, first_half) * sn
            qa_s[rows, cols] = jnp.where(left, qp, 0.0).astype(BF16)
            qb_s[rows, cols] = jnp.where(left, 0.0, qp).astype(BF16)
        base = D_HALF + 2 * LANES
        qm = _dot(h, win_ref[:, base:base + D_HALF])
        qmf_s[rows, :] = qm
        for p in range(N_PAIRS):
            cols = slice(p * LANES, (p + 1) * LANES)
            qma_s[rows, cols] = jnp.where(left, qm[:, cols], 0.0).astype(BF16)
            qmb_s[rows, cols] = jnp.where(left, 0.0, qm[:, cols]).astype(BF16)
        km = _dot(h, win_ref[:, base + D_HALF:base + 2 * D_HALF]) * (HEAD_DIM ** -0.5)
        km_s[rows, :] = km
        kmb_s[rows, :] = km.astype(BF16)
        vm_s[rows, :] = _dot(h, win_ref[:, base + 2 * D_HALF:base + 3 * D_HALF]).astype(BF16)
        om_s[rows, :] = _dot(h, win_ref[:, base + 3 * D_HALF:base + 4 * D_HALF])
        g_s[rows, :] = _dot(h, wg_ref[...]) + gb_ref[...]
        return carry

    lax.fori_loop(0, t // ROW_CHUNK, project, 0)

    qi = lax.broadcasted_iota(jnp.int32, (CHUNK, 3 * CHUNK), 0)
    kj = lax.broadcasted_iota(jnp.int32, (CHUNK, 3 * CHUNK), 1)
    band_ok = jnp.abs(kj - CHUNK - qi) <= WINDOW

    def attend(i, carry):
        r0 = pl.multiple_of(i * CHUNK, CHUNK)
        rows = pl.ds(r0, CHUNK)
        if latent:
            key_pos = kj + (i - 1) * CHUNK
            mask = band_ok & (key_pos >= 0) & (key_pos < t)
        for p in range(N_PAIRS):
            kv = p // (N_PAIRS // N_KV)
            cols = slice(p * LANES, (p + 1) * LANES)
            if latent:
                k_ctx = kc_ref[kv].astype(BF16)
                v_ctx = vc_ref[kv].astype(BF16)
                k_win = kk_s[kv, pl.ds(r0, 3 * CHUNK), :]
                v_win = vv_s[kv, pl.ds(r0, 3 * CHUNK), :]
            else:
                k_ctx = kk_s[kv]
                v_ctx = vv_s[kv]
            outs = []
            for j, q_s in enumerate((qa_s, qb_s)):
                hd = 2 * p + j
                qh = q_s[rows, cols]
                sink = sink_ref[hd:hd + 1, 0:1]
                s_ctx = _dot_nt(qh, k_ctx) * ATT_SCALE
                mx = jnp.maximum(jnp.max(s_ctx, axis=-1, keepdims=True), sink)
                if latent:
                    s_lat = jnp.where(mask, _dot_nt(qh, k_win) * ATT_SCALE, -jnp.inf)
                    mx = jnp.maximum(mx, jnp.max(s_lat, axis=-1, keepdims=True))
                p_ctx = jnp.exp(s_ctx - mx)
                den = jnp.sum(p_ctx, axis=-1, keepdims=True) + jnp.exp(sink - mx)
                num = _dot(p_ctx.astype(BF16), v_ctx)
                if latent:
                    p_lat = jnp.exp(s_lat - mx)
                    den = den + jnp.sum(p_lat, axis=-1, keepdims=True)
                    num = num + _dot(p_lat.astype(BF16), v_win)
                outs.append(num / den)
            z_s[rows, cols] = jnp.where(left, outs[0], outs[1]).astype(BF16)
        return carry

    lax.fori_loop(0, n_blocks, attend, 0)

    if latent:
        c_s[...] = cin_ref[...]
        n_s[...] = nin_ref[...]
        m_s[...] = min_ref[...]
    else:
        c_s[...] = jnp.zeros(c_s.shape, F32)
        n_s[...] = jnp.zeros(n_s.shape, F32)
        m_s[...] = jnp.zeros(m_s.shape, F32)

    ti = lax.broadcasted_iota(jnp.int32, (CHUNK, CHUNK), 0)
    si = lax.broadcasted_iota(jnp.int32, (CHUNK, CHUNK), 1)
    top = ti < HEAD_DIM
    same_head = top == (si < HEAD_DIM)
    see = (si <= ti, si >= ti)
    csum = ((ti <= si).astype(F32), (ti >= si).astype(F32))
    last = (CHUNK - 1, 0)

    def mlstm(i, carry):
        offs = (pl.multiple_of(i * CHUNK, CHUNK), pl.multiple_of((n_blocks - 1 - i) * CHUNK, CHUNK))
        for d in range(2):
            rows = pl.ds(offs[d], CHUNK)
            gt = jnp.transpose(g_s[rows, :])
            ig = gt[8 * d:8 * d + 8, :]
            lf = _log_sigmoid(gt[16 + 8 * d:24 + 8 * d, :])
            bcum = jnp.dot(lf, csum[d], precision=lax.Precision.HIGHEST, preferred_element_type=F32)
            a_all = ig - bcum
            h_s = (hf_s, hb_s)[d]
            for p in range(N_PAIRS):
                cols = slice(p * LANES, (p + 1) * LANES)
                kp = kmb_s[rows, cols]
                vp = vm_s[rows, cols]
                qf = qmf_s[rows, cols]
                c_prev = c_s[d * N_PAIRS + p]
                n_prev = n_s[d * N_PAIRS + p:d * N_PAIRS + p + 1, :]
                q_c = _dot(qf.astype(BF16), c_prev.astype(BF16))
                outs, e_rows, decays = [], [], []
                for j, q_s in enumerate((qma_s, qmb_s)):
                    hd = 2 * p + j
                    mrow = d * N_HEADS + hd
                    mine = left if j == 0 else jnp.logical_not(left)
                    a_row = a_all[hd:hd + 1, :]
                    lf_row = lf[hd:hd + 1, :]
                    m_prev = m_s[mrow:mrow + 1, 0:1]
                    zmat = jnp.where(see[d], a_row, -jnp.inf)
                    m_run = jnp.maximum(jnp.max(zmat, axis=-1, keepdims=True), m_prev)
                    dmat = jnp.exp(zmat - m_run)
                    b_col = jnp.sum(jnp.where(see[d], lf_row, 0.0), axis=-1, keepdims=True)
                    w_int = jnp.exp(m_prev - m_run)
                    s_mat = _dot_nt(q_s[rows, cols], kp) * dmat
                    num = _dot(s_mat.astype(BF16), vp) + w_int * q_c
                    q_n = jnp.sum(jnp.where(mine, qf * n_prev, 0.0), axis=-1, keepdims=True)
                    den = jnp.sum(s_mat, axis=-1, keepdims=True) + w_int * q_n
                    outs.append(num / jnp.maximum(jnp.abs(den), jnp.exp(-(b_col + m_run))))
                    m_last = m_run[last[d]:last[d] + 1, :]
                    b_last = b_col[last[d]:last[d] + 1, :]
                    e_rows.append(jnp.exp(a_row - m_last))
                    decays.append(jnp.exp(m_prev - m_last))
                    m_s[mrow:mrow + 1, :] = jnp.broadcast_to(b_last + m_last, (1, LANES))
                h_s[rows, cols] = jnp.where(left, outs[0], outs[1])
                k_t = jnp.transpose(km_s[rows, cols])
                k_te = (k_t * jnp.where(top, e_rows[0], e_rows[1])).astype(BF16)
                upd = _dot(k_te, vp)
                decay = jnp.where(left, decays[0], decays[1])
                c_s[d * N_PAIRS + p] = decay * c_prev + jnp.where(same_head, upd, 0.0)
                e2 = jnp.concatenate([e_rows[0], e_rows[1], jnp.zeros((6, CHUNK), F32)], axis=0).astype(BF16)
                n_k = _dot(e2, kp)
                n_s[d * N_PAIRS + p:d * N_PAIRS + p + 1, :] = (
                    decay * n_prev + jnp.where(left, n_k[0:1, :], n_k[1:2, :]))
        return carry

    lax.fori_loop(0, n_blocks, mlstm, 0)

    if not latent:
        co_ref[...] = c_s[...]
        no_ref[...] = n_s[...]
        mo_ref[...] = m_s[...]

    def finish(rc, carry):
        r0 = pl.multiple_of(rc * ROW_CHUNK, ROW_CHUNK)
        rows = pl.ds(r0, ROW_CHUNK)
        for p in range(N_PAIRS):
            cols = slice(p * LANES, (p + 1) * LANES)
            hm = hf_s[rows, cols] + hb_s[rows, cols]
            sq = hm * hm
            ms_a = jnp.sum(jnp.where(left, sq, 0.0), axis=-1, keepdims=True)
            ms_b = jnp.sum(jnp.where(left, 0.0, sq), axis=-1, keepdims=True)
            ms = jnp.where(left, ms_a, ms_b) * (1.0 / HEAD_DIM)
            y = hm * lax.rsqrt(ms + EPS) * hn_ref[:, cols] * jax.nn.sigmoid(om_s[rows, cols])
            z_s[rows, D_HALF + p * LANES:D_HALF + (p + 1) * LANES] = y.astype(BF16)
        o = _dot(z_s[rows, :], wout_ref[...])
        o_ref[rows, :] = x_ref[rows, :] + gate1 * _rms(o, ng_ref[1:2, :])
        return carry

    lax.fori_loop(0, t // ROW_CHUNK, finish, 0)


def _odd_scratch(t, latent):
    pad = 2 * CHUNK if latent else 0
    return [
        pltpu.VMEM((t, D_HALF), BF16),
        pltpu.VMEM((t, D_HALF), BF16),
        pltpu.VMEM((N_KV, t + pad, LANES), BF16),
        pltpu.VMEM((N_KV, t + pad, LANES), BF16),
        pltpu.VMEM((t, D_HALF), BF16),
        pltpu.VMEM((t, D_HALF), BF16),
        pltpu.VMEM((t, D_HALF), F32),
        pltpu.VMEM((t, D_HALF), F32),
        pltpu.VMEM((t, D_HALF), BF16),
        pltpu.VMEM((t, D_HALF), BF16),
        pltpu.VMEM((t, D_HALF), F32),
        pltpu.VMEM((t, LANES), F32),
        pltpu.VMEM((t, D_MODEL), BF16),
        pltpu.VMEM((t, D_HALF), F32),
        pltpu.VMEM((t, D_HALF), F32),
        pltpu.VMEM((2 * N_PAIRS, LANES, LANES), F32),
        pltpu.VMEM((2 * N_PAIRS, LANES), F32),
        pltpu.VMEM((2 * N_HEADS, LANES), F32),
    ]


def _odd_common_specs(t, cond_base, cond_stride):
    d_main = D_HALF + 2 * LANES + 4 * D_HALF
    return [
        pl.BlockSpec((None, t, D_MODEL), lambda i: (i, 0, 0)),
        pl.BlockSpec((None, 6, D_MODEL), lambda i: (cond_base + cond_stride * i, 0, 0)),
        _const_spec((4, D_MODEL)),
        _const_spec((D_MODEL, d_main)),
        _const_spec((D_MODEL, LANES)),
        _const_spec((1, LANES)),
        _const_spec((N_HEADS, LANES)),
        _const_spec((1, D_HALF)),
        _const_spec((D_MODEL, D_MODEL)),
    ]


def _odd_context(x, modv, ng, w_main, w_gate, gate_bias, sink_b, hnorm, wout):
    b, t, _ = x.shape
    kern = functools.partial(_odd_kernel, t=t, lc=t, latent=False)
    per_seq = lambda shape: pl.BlockSpec((None,) + shape, lambda i: (i,) + (0,) * len(shape))
    return pl.pallas_call(
        kern,
        grid=(b,),
        in_specs=_odd_common_specs(t, 0, 0),
        out_specs=[per_seq((t, D_MODEL)), per_seq((t, LANES)), per_seq((t, LANES)),
                   per_seq((2 * N_PAIRS, LANES, LANES)), per_seq((2 * N_PAIRS, LANES)),
                   per_seq((2 * N_HEADS, LANES))],
        out_shape=[jax.ShapeDtypeStruct((b, t, D_MODEL), F32),
                   jax.ShapeDtypeStruct((b, t, LANES), F32),
                   jax.ShapeDtypeStruct((b, t, LANES), F32),
                   jax.ShapeDtypeStruct((b, 2 * N_PAIRS, LANES, LANES), F32),
                   jax.ShapeDtypeStruct((b, 2 * N_PAIRS, LANES), F32),
                   jax.ShapeDtypeStruct((b, 2 * N_HEADS, LANES), F32)],
        scratch_shapes=_odd_scratch(t, False),
        compiler_params=_params(1),
        name="odd_mixer_context",
    )(x, modv, ng, w_main, w_gate, gate_bias, sink_b, hnorm, wout)


def _odd_latent(x, modv, ng, w_main, w_gate, gate_bias, sink_b, hnorm, wout, cos_t, sin_t, kc, vc, c_in, n_in, m_in):
    b, t, _ = x.shape
    lc = kc.shape[2]
    kern = functools.partial(_odd_kernel, t=t, lc=lc, latent=True)
    per_seq = lambda shape: pl.BlockSpec((None,) + shape, lambda i: (i,) + (0,) * len(shape))
    return pl.pallas_call(
        kern,
        grid=(b,),
        in_specs=_odd_common_specs(t, 1, 1) + [
            _const_spec((t, LANES)), _const_spec((t, LANES)),
            per_seq((N_KV, lc, LANES)), per_seq((N_KV, lc, LANES)),
            per_seq((2 * N_PAIRS, LANES, LANES)), per_seq((2 * N_PAIRS, LANES)), per_seq((2 * N_HEADS, LANES)),
        ],
        out_specs=per_seq((t, D_MODEL)),
        out_shape=jax.ShapeDtypeStruct((b, t, D_MODEL), F32),
        scratch_shapes=_odd_scratch(t, True),
        compiler_params=_params(1),
        name="odd_mixer_latent",
    )(x, modv, ng, w_main, w_gate, gate_bias, sink_b, hnorm, wout, cos_t, sin_t, kc, vc, c_in, n_in, m_in)


def _rope_tables(t):
    rows = t // GRID_W
    row = jnp.broadcast_to(jnp.arange(rows)[:, None], (rows, GRID_W)).reshape(t).astype(F32)
    col = jnp.broadcast_to(jnp.arange(GRID_W)[None, :], (rows, GRID_W)).reshape(t).astype(F32)
    n_freq = HEAD_DIM // 4
    inv_freq = ROPE_BASE ** (-jnp.arange(n_freq, dtype=F32) / n_freq)
    ang = jnp.concatenate([row[:, None] * inv_freq, col[:, None] * inv_freq], axis=-1)
    cos, sin = jnp.cos(ang), jnp.sin(ang)
    cos_l = jnp.tile(cos, (1, LANES // cos.shape[1]))
    sin_l = jnp.tile(jnp.concatenate([-sin, sin], axis=-1), (1, LANES // HEAD_DIM))
    return cos_l, sin_l


def _pair_blockdiag(c):
    b = c.shape[0]
    c = c.reshape(b, N_PAIRS, 2, HEAD_DIM, HEAD_DIM)
    z = jnp.zeros_like(c[:, :, 0])
    top = jnp.concatenate([c[:, :, 0], z], axis=-1)
    bot = jnp.concatenate([z, c[:, :, 1]], axis=-1)
    return jnp.concatenate([top, bot], axis=-2)


def _pair_diag_blocks(cbd):
    b = cbd.shape[0]
    a = cbd[:, :, :HEAD_DIM, :HEAD_DIM]
    d = cbd[:, :, HEAD_DIM:, HEAD_DIM:]
    return jnp.stack([a, d], axis=2).reshape(b, N_HEADS, HEAD_DIM, HEAD_DIM)


def _lane_bcast(v):
    return jnp.broadcast_to(v[..., None], v.shape + (LANES,))


def kernel(x_prompt, x_sample, c, cache_k, cache_v, state_c_fwd, state_n_fwd, state_m_fwd, state_c_bwd, state_n_bwd, state_m_bwd, c_ctx, mod_w, mod_b, norm_g, mlp_w1, mlp_w2, even_in_w, conv_a_w, conv_a_b, ln_a_g, ln_a_b, conv_b_w, even_out_w, odd_in_w, attn_sink, gate_b, hnorm_g, odd_out_w):
    n_dec = x_sample.shape[0]
    n_ctx = x_prompt.shape[0]
    cond = jnp.concatenate([c_ctx[None, :], c, jnp.zeros((COND_ROWS - 1 - n_dec, D_MODEL), F32)], axis=0)
    modv = _modulation(cond, mod_w, mod_b)

    yp, ys = x_prompt, x_sample

    ev = (even_in_w[0].astype(BF16), conv_a_w[0], conv_a_b[0][None, :], ln_a_g[0][None, :], ln_a_b[0][None, :],
          conv_b_w[0], even_out_w[0].astype(BF16))
    yp = _even_layer(yp, modv[0], 0, 0, norm_g[0], *ev)
    ys = _even_layer(ys, modv[0], 1, 1, norm_g[0], *ev)
    w1, w2 = mlp_w1[0].astype(BF16), mlp_w2[0].astype(BF16)
    yp = _mlp_layer(yp, modv[0], 0, 0, norm_g[0], w1, w2)
    ys = _mlp_layer(ys, modv[0], 1, 1, norm_g[0], w1, w2)

    w_in = odd_in_w[0]
    d_main = D_HALF + 2 * LANES + 4 * D_HALF
    w_main = w_in[:, :d_main].astype(BF16)
    order = jnp.array([0, 2, 1, 3])
    wg = w_in[:, d_main:].reshape(D_MODEL, 4, N_HEADS)[:, order, :].reshape(D_MODEL, 4 * N_HEADS)
    w_gate = jnp.pad(wg, ((0, 0), (0, LANES - 4 * N_HEADS))).astype(BF16)
    gate_bias = jnp.pad(gate_b[0][order, :].reshape(1, 4 * N_HEADS), ((0, 0), (0, LANES - 4 * N_HEADS)))
    sink_b = _lane_bcast(attn_sink[0])
    hnorm = hnorm_g[0][None, :]
    wout = odd_out_w[0].astype(BF16)
    odd = (norm_g[1], w_main, w_gate, gate_bias, sink_b, hnorm, wout)

    op, k_new, v_new, c_new, n_new, m_new = _odd_context(yp, modv[1], *odd)

    t_dec = x_sample.shape[1]
    cos_t, sin_t = _rope_tables(t_dec)
    kc = jnp.concatenate([cache_k[:, 0], cache_k[:, 0]], axis=-1)
    vc = jnp.concatenate([cache_v[:, 0], cache_v[:, 0]], axis=-1)
    c_in = jnp.concatenate([_pair_blockdiag(state_c_fwd[:, 0]), _pair_blockdiag(state_c_bwd[:, 0])], axis=1)
    n_in = jnp.concatenate([state_n_fwd[:, 0].reshape(n_dec, N_PAIRS, LANES),
                            state_n_bwd[:, 0].reshape(n_dec, N_PAIRS, LANES)], axis=1)
    m_in = _lane_bcast(jnp.concatenate([state_m_fwd[:, 0], state_m_bwd[:, 0]], axis=1))
    os_ = _odd_latent(ys, modv[1], *odd, cos_t, sin_t, kc, vc, c_in, n_in, m_in)

    w1, w2 = mlp_w1[1].astype(BF16), mlp_w2[1].astype(BF16)
    yp = _mlp_layer(op, modv[1], 0, 0, norm_g[1], w1, w2)
    ys = _mlp_layer(os_, modv[1], 1, 1, norm_g[1], w1, w2)

    t_ctx = x_prompt.shape[1]
    heads_first = lambda a: a.reshape(n_ctx, t_ctx, N_KV, HEAD_DIM).transpose(0, 2, 1, 3)[:, None]
    new_k = heads_first(k_new)
    new_v = heads_first(v_new)
    c_f = _pair_diag_blocks(c_new[:, :N_PAIRS])[:, None]
    c_b = _pair_diag_blocks(c_new[:, N_PAIRS:])[:, None]
    n_f = n_new[:, :N_PAIRS].reshape(n_ctx, N_HEADS, HEAD_DIM)[:, None]
    n_b = n_new[:, N_PAIRS:].reshape(n_ctx, N_HEADS, HEAD_DIM)[:, None]
    m_f = m_new[:, :N_HEADS, 0][:, None]
    m_b = m_new[:, N_HEADS:, 0][:, None]
    return (yp, ys, new_k, new_v, c_f, n_f, m_f, c_b, n_b, m_b)
```

```python
import functools

import jax
import jax.numpy as jnp
from jax import lax
from jax.experimental import pallas as pl
from jax.experimental.pallas import tpu as pltpu

F32 = jnp.float32
BF16 = jnp.bfloat16

D_MODEL = 1024
D_FF = 4 * D_MODEL
EPS = 1e-6
D_HALF = D_MODEL // 2
CONF_WIDTH = 31
CONF_HALO = 16
HEAD_DIM = 64
N_HEADS = 8
N_PAIRS = N_HEADS // 2
N_KV = 2
LANES = 128
CHUNK = 128
WINDOW = 128
GRID_W = 64
ROPE_BASE = 10000.0
ATT_SCALE = HEAD_DIM ** -0.5
ROW_CHUNK = 256
COND_ROWS = 8
VMEM_LIMIT = 56 * 1024 * 1024


def _dot(a, b):
    return jnp.dot(a, b, preferred_element_type=F32)


def _dot_nt(a, b):
    return lax.dot_general(a, b, (((1,), (1,)), ((), ())), preferred_element_type=F32)


def _split3(x):
    hi = x.astype(BF16)
    r1 = x - hi.astype(F32)
    mid = r1.astype(BF16)
    lo = (r1 - mid.astype(F32)).astype(BF16)
    return hi, mid, lo


def _dot_exact_lhs(x, b01):
    hi, mid, lo = _split3(x)
    return _dot(hi, b01) + _dot(mid, b01) + _dot(lo, b01)


def _dot_exact_rhs(a01, x):
    hi, mid, lo = _split3(x)
    return _dot(a01, hi) + _dot(a01, mid) + _dot(a01, lo)


def _rms(x, g):
    return x * lax.rsqrt(jnp.mean(x * x, axis=-1, keepdims=True) + EPS) * g


def _norm_mod(x, g, shift, scale):
    return _rms(x, g) * (1.0 + scale) + shift


def _params(n_grid):
    return pltpu.CompilerParams(dimension_semantics=("arbitrary",) * n_grid, vmem_limit_bytes=VMEM_LIMIT)


def _const_spec(shape):
    zeros = (0,) * len(shape)
    return pl.BlockSpec(shape, lambda *_: zeros, pipeline_mode=pl.Buffered(1))


def _mod_kernel(cond_ref, w_ref, b_ref, o_ref):
    s = jax.nn.silu(cond_ref[...]).astype(BF16)
    o_ref[...] = _dot(s, w_ref[...].astype(BF16)) + b_ref[...]


def _modulation(cond, mod_w, mod_b):
    depth = mod_w.shape[0]
    n_out = mod_w.shape[2]
    tn = D_MODEL
    out = pl.pallas_call(
        _mod_kernel,
        grid=(depth, n_out // tn),
        in_specs=[
            pl.BlockSpec((COND_ROWS, D_MODEL), lambda l, j: (0, 0)),
            pl.BlockSpec((None, D_MODEL, tn), lambda l, j: (l, 0, j)),
            pl.BlockSpec((None, 1, tn), lambda l, j: (l, 0, j)),
        ],
        out_specs=pl.BlockSpec((None, COND_ROWS, tn), lambda l, j: (l, 0, j)),
        out_shape=jax.ShapeDtypeStruct((depth, COND_ROWS, n_out), F32),
        compiler_params=_params(2),
        name="modulation",
    )(cond, mod_w, mod_b.reshape(depth, 1, n_out))
    return out.reshape(depth, COND_ROWS, 6, D_MODEL)


def _even_kernel(xp_ref, xc_ref, xn_ref, mod_ref, ng_ref, win_ref, caw_ref, cab_ref, lng_ref, lnb_ref,
                 cbw_ref, wout_ref, o_ref, apad, cpad, bgs, zs, *, n_chunks):
    c = pl.program_id(1)
    rows = ROW_CHUNK + 2 * CONF_HALO
    shift1, scale1, gate1 = mod_ref[0:1, :], mod_ref[1:2, :], mod_ref[2:3, :]
    xc = xc_ref[...]
    xh = jnp.concatenate([xp_ref[...], xc, xn_ref[...]], axis=0)
    h = _norm_mod(xh, ng_ref[0:1, :], shift1, scale1).astype(BF16)
    ri = lax.broadcasted_iota(jnp.int32, (rows, D_HALF), 0)
    lo = jnp.where(c == 0, CONF_HALO, 0)
    hi = jnp.where(c == n_chunks - 1, CONF_HALO + ROW_CHUNK, rows)
    inside = (ri >= lo) & (ri < hi)
    a = _dot(h, win_ref[:, 0:D_HALF]) * jax.nn.sigmoid(_dot(h, win_ref[:, D_HALF:2 * D_HALF]))
    apad[...] = jnp.where(inside, a, 0.0)
    cx = _dot(h, win_ref[:, 3 * D_HALF:4 * D_HALF]) * _dot(h, win_ref[:, 4 * D_HALF:5 * D_HALF])
    cpad[...] = jnp.where(inside, cx, 0.0)
    bgs[...] = _dot(h[CONF_HALO:CONF_HALO + ROW_CHUNK], win_ref[:, 2 * D_HALF:3 * D_HALF])

    sub = 32
    for j in range(ROW_CHUNK // sub):
        r0 = j * sub
        acc = caw_ref[0:1, :] * apad[r0 + 1:r0 + 1 + sub, :]
        for k in range(1, CONF_WIDTH):
            acc = acc + caw_ref[k:k + 1, :] * apad[r0 + 1 + k:r0 + 1 + k + sub, :]
        acc = acc + cab_ref[...]
        mu = jnp.mean(acc, axis=-1, keepdims=True)
        dlt = acc - mu
        var = jnp.mean(dlt * dlt, axis=-1, keepdims=True)
        a_out = jax.nn.silu(dlt * lax.rsqrt(var + EPS) * lng_ref[...] + lnb_ref[...])
        zs[r0:r0 + sub, 0:D_HALF] = a_out.astype(BF16)
        base = r0 + CONF_HALO - 1
        sc = (cbw_ref[0:1, :] * cpad[base:base + sub, :]
              + cbw_ref[1:2, :] * cpad[base + 1:base + 1 + sub, :]
              + cbw_ref[2:3, :] * cpad[base + 2:base + 2 + sub, :])
        zs[r0:r0 + sub, D_HALF:D_MODEL] = (bgs[r0:r0 + sub, :] * sc).astype(BF16)

    o = _dot(zs[...], wout_ref[...])
    o_ref[...] = xc + gate1 * _rms(o, ng_ref[1:2, :])


def _even_layer(x, modv, cond_base, cond_stride, ng, win, caw, cab, lng, lnb, cbw, wout):
    b, t, _ = x.shape
    n_chunks = t // ROW_CHUNK
    hpc = ROW_CHUNK // CONF_HALO
    n_halo_blocks = t // CONF_HALO
    rows = ROW_CHUNK + 2 * CONF_HALO
    kern = functools.partial(_even_kernel, n_chunks=n_chunks)
    return pl.pallas_call(
        kern,
        grid=(b, n_chunks),
        in_specs=[
            pl.BlockSpec((None, CONF_HALO, D_MODEL), lambda i, c: (i, jnp.maximum(c * hpc - 1, 0), 0)),
            pl.BlockSpec((None, ROW_CHUNK, D_MODEL), lambda i, c: (i, c, 0)),
            pl.BlockSpec((None, CONF_HALO, D_MODEL),
                         lambda i, c: (i, jnp.minimum((c + 1) * hpc, n_halo_blocks - 1), 0)),
            pl.BlockSpec((None, 6, D_MODEL), lambda i, c: (cond_base + cond_stride * i, 0, 0)),
            _const_spec((4, D_MODEL)),
            _const_spec((D_MODEL, 5 * D_HALF)),
            _const_spec((CONF_WIDTH, D_HALF)),
            _const_spec((1, D_HALF)),
            _const_spec((1, D_HALF)),
            _const_spec((1, D_HALF)),
            _const_spec((3, D_HALF)),
            _const_spec((D_MODEL, D_MODEL)),
        ],
        out_specs=pl.BlockSpec((None, ROW_CHUNK, D_MODEL), lambda i, c: (i, c, 0)),
        out_shape=jax.ShapeDtypeStruct(x.shape, F32),
        scratch_shapes=[
            pltpu.VMEM((rows, D_HALF), F32),
            pltpu.VMEM((rows, D_HALF), F32),
            pltpu.VMEM((ROW_CHUNK, D_HALF), F32),
            pltpu.VMEM((ROW_CHUNK, D_MODEL), BF16),
        ],
        compiler_params=_params(2),
        name="even_mixer",
    )(x, x, x, modv, ng, win, caw, cab, lng, lnb, cbw, wout)


def _mlp_kernel(x_ref, mod_ref, ng_ref, w1_ref, w2_ref, o_ref):
    x = x_ref[...]
    h = _norm_mod(x, ng_ref[2:3, :], mod_ref[3:4, :], mod_ref[4:5, :]).astype(BF16)
    acc = None
    for c in range(D_FF // D_MODEL):
        cols = slice(c * D_MODEL, (c + 1) * D_MODEL)
        hid = jnp.square(jnp.maximum(_dot(h, w1_ref[:, cols]), 0.0)).astype(BF16)
        part = _dot(hid, w2_ref[cols, :])
        acc = part if acc is None else acc + part
    o_ref[...] = x + mod_ref[5:6, :] * _rms(acc, ng_ref[3:4, :])


def _mlp_layer(x, modv, cond_base, cond_stride, ng, w1, w2):
    b, t, _ = x.shape
    tm = 512
    assert (b * t) % tm == 0 and (cond_stride == 0 or t % tm == 0)
    tiles_per_seq = max(t // tm, 1)
    x2 = x.reshape(b * t, D_MODEL)
    out = pl.pallas_call(
        _mlp_kernel,
        grid=(b * t // tm,),
        in_specs=[
            pl.BlockSpec((tm, D_MODEL), lambda i: (i, 0)),
            pl.BlockSpec((None, 6, D_MODEL), lambda i: (cond_base + cond_stride * (i // tiles_per_seq), 0, 0)),
            _const_spec((4, D_MODEL)),
            _const_spec((D_MODEL, D_FF)),
            _const_spec((D_FF, D_MODEL)),
        ],
        out_specs=pl.BlockSpec((tm, D_MODEL), lambda i: (i, 0)),
        out_shape=jax.ShapeDtypeStruct(x2.shape, F32),
        compiler_params=_params(1),
        name="mlp",
    )(x2, modv, ng, w1, w2)
    return out.reshape(b, t, D_MODEL)


def _log_sigmoid(x):
    return jnp.minimum(x, 0.0) - jnp.log(1.0 + jnp.exp(-jnp.abs(x)))


def _rot_half(x, first_half):
    return jnp.where(first_half, pltpu.roll(x, 96, axis=1), pltpu.roll(x, 32, axis=1))


def _odd_kernel(*refs, t, lc, latent):
    if latent:
        (x_ref, mod_ref, ng_ref, win_ref, wg_ref, gb_ref, sink_ref, hn_ref, wout_ref, sel_ref,
         cos_ref, sin_ref, kc_ref, vc_ref, cin_ref, nin_ref, min_ref,
         o_ref,
         qa_s, qb_s, kk_s, vvt_s, qma_s, qmb_s, qm_s, km_s, vmt_s, vmtf_s, om_s, g_s, z_s, hft_s, hbt_s,
         ab_s, c_s, n_s, m_s, kc_s, vct_s) = refs
    else:
        (x_ref, mod_ref, ng_ref, win_ref, wg_ref, gb_ref, sink_ref, hn_ref, wout_ref, sel_ref,
         o_ref, ko_ref, vo_ref, co_ref, no_ref, mo_ref,
         qa_s, qb_s, kk_s, vvt_s, qma_s, qmb_s, qm_s, km_s, vmt_s, vmtf_s, om_s, g_s, z_s, hft_s, hbt_s,
         ab_s, c_s, n_s, m_s) = refs

    n_blocks = t // CHUNK
    pad = CHUNK if latent else 0
    shift1, scale1, gate1 = mod_ref[0:1, :], mod_ref[1:2, :], mod_ref[2:3, :]
    lane = lax.broadcasted_iota(jnp.int32, (1, LANES), 1)
    left = lane < HEAD_DIM
    first_half = (lane % HEAD_DIM) < (HEAD_DIM // 2)
    ti = lax.broadcasted_iota(jnp.int32, (CHUNK, CHUNK), 0)
    si = lax.broadcasted_iota(jnp.int32, (CHUNK, CHUNK), 1)
    top = ti < HEAD_DIM
    same_head = top == (si < HEAD_DIM)

    if latent:
        for kv in range(N_KV):
            kk_s[kv, 0:CHUNK, :] = jnp.zeros((CHUNK, LANES), BF16)
            kk_s[kv, CHUNK + t:2 * CHUNK + t, :] = jnp.zeros((CHUNK, LANES), BF16)
            vvt_s[kv, :, 0:CHUNK] = jnp.zeros((LANES, CHUNK), BF16)
            vvt_s[kv, :, CHUNK + t:2 * CHUNK + t] = jnp.zeros((LANES, CHUNK), BF16)
            kc_s[kv] = kc_ref[kv].astype(BF16)
            vct_s[kv] = jnp.transpose(vc_ref[kv]).astype(BF16)

    def project(rc, carry):
        r0 = pl.multiple_of(rc * ROW_CHUNK, ROW_CHUNK)
        rows = pl.ds(r0, ROW_CHUNK)
        krows = pl.ds(r0 + pad, ROW_CHUNK)
        h = _norm_mod(x_ref[rows, :], ng_ref[0:1, :], shift1, scale1).astype(BF16)
        q = _dot(h, win_ref[:, 0:D_HALF])
        kv2 = _dot(h, win_ref[:, D_HALF:D_HALF + 2 * LANES])
        ka, va = kv2[:, 0:LANES], kv2[:, LANES:2 * LANES]
        if latent:
            cs, sn = cos_ref[rows, :], sin_ref[rows, :]
            ka = ka * cs + _rot_half(ka, first_half) * sn
        else:
            ko_ref[rows, :] = ka
            vo_ref[rows, :] = va
        kr = pltpu.roll(ka, HEAD_DIM, axis=1)
        vr = pltpu.roll(va, HEAD_DIM, axis=1)
        kk_s[0, krows, :] = jnp.where(left, ka, kr).astype(BF16)
        kk_s[1, krows, :] = jnp.where(left, kr, ka).astype(BF16)
        vvt_s[0, :, krows] = jnp.transpose(jnp.where(left, va, vr)).astype(BF16)
        vvt_s[1, :, krows] = jnp.transpose(jnp.where(left, vr, va)).astype(BF16)
        base = D_HALF + 2 * LANES
        qm = _dot(h, win_ref[:, base:base + D_HALF])
        qm_s[rows, :] = qm.astype(BF16)
        vm = _dot(h, win_ref[:, base + 2 * D_HALF:base + 3 * D_HALF])
        for p in range(N_PAIRS):
            cols = slice(p * LANES, (p + 1) * LANES)
            qp = q[:, cols]
            if latent:
                qp = qp * cs + _rot_half(qp, first_half) * sn
            qa_s[rows, cols] = jnp.where(left, qp, 0.0).astype(BF16)
            qb_s[rows, cols] = jnp.where(left, 0.0, qp).astype(BF16)
            qma_s[rows, cols] = jnp.where(left, qm[:, cols], 0.0).astype(BF16)
            qmb_s[rows, cols] = jnp.where(left, 0.0, qm[:, cols]).astype(BF16)
            vt = jnp.transpose(vm[:, cols])
            vmtf_s[cols, rows] = vt
            vmt_s[cols, rows] = vt.astype(BF16)
        km = _dot(h, win_ref[:, base + D_HALF:base + 2 * D_HALF]) * (HEAD_DIM ** -0.5)
        km_s[rows, :] = km.astype(BF16)
        om_s[rows, :] = _dot(h, win_ref[:, base + 3 * D_HALF:base + 4 * D_HALF])
        g_s[rows, :] = _dot(h, wg_ref[...]) + gb_ref[...]
        return carry

    lax.fori_loop(0, t // ROW_CHUNK, project, 0)

    kj = lax.broadcasted_iota(jnp.int32, (3 * CHUNK, CHUNK), 0)
    qi = lax.broadcasted_iota(jnp.int32, (3 * CHUNK, CHUNK), 1)
    band_ok = jnp.abs(kj - CHUNK - qi) <= WINDOW

    def attend(i, carry):
        r0 = pl.multiple_of(i * CHUNK, CHUNK)
        rows = pl.ds(r0, CHUNK)
        if latent:
            key_pos = kj + (i - 1) * CHUNK
            mask = band_ok & (key_pos >= 0) & (key_pos < t)
            win = pl.ds(r0, 3 * CHUNK)
        for p in range(N_PAIRS):
            kv = p // (N_PAIRS // N_KV)
            cols = slice(p * LANES, (p + 1) * LANES)
            if latent:
                k_ctx, vt_ctx = kc_s[kv], vct_s[kv]
                k_win, vt_win = kk_s[kv, win, :], vvt_s[kv, :, win]
            else:
                k_ctx, vt_ctx = kk_s[kv], vvt_s[kv]
            outs = []
            for j, q_s in enumerate((qa_s, qb_s)):
                hd = 2 * p + j
                qh = q_s[rows, cols]
                sink = sink_ref[hd:hd + 1, 0:1]
                s_ctx = _dot_nt(k_ctx, qh) * ATT_SCALE
                mx = jnp.maximum(jnp.max(s_ctx, axis=0, keepdims=True), sink)
                if latent:
                    s_lat = jnp.where(mask, _dot_nt(k_win, qh) * ATT_SCALE, -jnp.inf)
                    mx = jnp.maximum(mx, jnp.max(s_lat, axis=0, keepdims=True))
                p_ctx = jnp.exp(s_ctx - mx)
                den = jnp.sum(p_ctx, axis=0, keepdims=True) + jnp.exp(sink - mx)
                num = _dot(vt_ctx, p_ctx.astype(BF16))
                if latent:
                    p_lat = jnp.exp(s_lat - mx)
                    den = den + jnp.sum(p_lat, axis=0, keepdims=True)
                    num = num + _dot(vt_win, p_lat.astype(BF16))
                outs.append(num / den)
            z_s[rows, cols] = jnp.transpose(jnp.where(top, outs[0], outs[1])).astype(BF16)
        return carry

    lax.fori_loop(0, n_blocks, attend, 0)

    if latent:
        c_s[...] = cin_ref[...]
        n_s[...] = nin_ref[...]
        m_s[...] = min_ref[...]
    else:
        c_s[...] = jnp.zeros(c_s.shape, F32)
        n_s[...] = jnp.zeros(n_s.shape, F32)
        m_s[...] = jnp.zeros(m_s.shape, F32)

    see = (ti <= si, ti >= si)
    tri = tuple(m.astype(F32).astype(BF16) for m in see)
    tri_t = (tri[1], tri[0])
    gate_col = lax.broadcasted_iota(jnp.int32, (CHUNK, LANES), 1)
    last = (CHUNK - 1, 0)

    def mlstm(i, carry):
        offs = (pl.multiple_of(i * CHUNK, CHUNK), pl.multiple_of((n_blocks - 1 - i) * CHUNK, CHUNK))
        for d in range(2):
            rows = pl.ds(offs[d], CHUNK)
            g = g_s[rows, :]
            bcum_c = _dot_exact_rhs(tri_t[d], _log_sigmoid(g))
            ab_s[d] = _dot_exact_lhs(jnp.where(gate_col < 2 * N_HEADS, g, -bcum_c), sel_ref[d])
            gt = jnp.transpose(g)
            ig = gt[8 * d:8 * d + 8, :]
            lf = _log_sigmoid(gt[2 * N_HEADS:4 * N_HEADS, :])
            bcum = _dot_exact_lhs(lf, tri[d])[8 * d:8 * d + 8, :]
            a_all = ig - bcum
            ht_s = (hft_s, hbt_s)[d]
            for p in range(N_PAIRS):
                cols = slice(p * LANES, (p + 1) * LANES)
                kp = km_s[rows, cols]
                vt = vmt_s[cols, rows]
                ct_prev = c_s[d * N_PAIRS + p]
                n_prev = n_s[d * N_PAIRS + p:d * N_PAIRS + p + 1, :]
                n8 = jnp.broadcast_to(n_prev, (8, LANES)).astype(BF16)
                qc_t = _dot_nt(ct_prev.astype(BF16), qm_s[rows, cols])
                outs, e_rows, decays = [], [], []
                for j, q_s in enumerate((qma_s, qmb_s)):
                    hd = 2 * p + j
                    mrow = d * N_HEADS + hd
                    qh = q_s[rows, cols]
                    a_row = a_all[hd:hd + 1, :]
                    b_row = bcum[hd:hd + 1, :]
                    m_prev = m_s[mrow:mrow + 1, 0:1]
                    z_t = jnp.where(see[d], ab_s[d, :, hd * LANES:(hd + 1) * LANES], -jnp.inf)
                    m_run = jnp.maximum(jnp.max(z_t, axis=0, keepdims=True), m_prev)
                    w_int = jnp.exp(m_prev - m_run)
                    s_t = _dot_nt(kp, qh) * jnp.exp(z_t - m_run)
                    num = _dot(vt, s_t.astype(BF16)) + w_int * qc_t
                    q_n = _dot_nt(n8, qh)[0:1, :]
                    den = jnp.sum(s_t, axis=0, keepdims=True) + w_int * q_n
                    outs.append(num / jnp.maximum(jnp.abs(den), jnp.exp(-(b_row + m_run))))
                    m_last = m_run[:, last[d]:last[d] + 1]
                    e_rows.append(jnp.exp(a_row - m_last))
                    decays.append(jnp.exp(m_prev - m_last))
                    m_s[mrow:mrow + 1, :] = jnp.broadcast_to(b_row[:, last[d]:last[d] + 1] + m_last, (1, LANES))
                ht_s[cols, rows] = jnp.where(top, outs[0], outs[1])
                vt_e = (vmtf_s[cols, rows] * jnp.where(top, e_rows[0], e_rows[1])).astype(BF16)
                upd = _dot(vt_e, kp)
                c_s[d * N_PAIRS + p] = (jnp.where(top, decays[0], decays[1]) * ct_prev
                                        + jnp.where(same_head, upd, 0.0))
                e2 = jnp.concatenate([e_rows[0], e_rows[1], jnp.zeros((6, CHUNK), F32)], axis=0).astype(BF16)
                n_k = _dot(e2, kp)
                n_s[d * N_PAIRS + p:d * N_PAIRS + p + 1, :] = (
                    jnp.where(left, decays[0], decays[1]) * n_prev + jnp.where(left, n_k[0:1, :], n_k[1:2, :]))
        return carry

    lax.fori_loop(0, n_blocks, mlstm, 0)

    if not latent:
        co_ref[...] = c_s[...]
        no_ref[...] = n_s[...]
        mo_ref[...] = m_s[...]

    top_w = lax.broadcasted_iota(jnp.int32, (LANES, ROW_CHUNK), 0) < HEAD_DIM

    def finish(rc, carry):
        r0 = pl.multiple_of(rc * ROW_CHUNK, ROW_CHUNK)
        rows = pl.ds(r0, ROW_CHUNK)
        for p in range(N_PAIRS):
            cols = slice(p * LANES, (p + 1) * LANES)
            hm = hft_s[cols, rows] + hbt_s[cols, rows]
            sq = hm * hm
            ms_a = jnp.sum(sq[0:HEAD_DIM], axis=0, keepdims=True)
            ms_b = jnp.sum(sq[HEAD_DIM:LANES], axis=0, keepdims=True)
            ms = jnp.where(top_w, ms_a, ms_b) * (1.0 / HEAD_DIM)
            y = jnp.transpose(hm * lax.rsqrt(ms + EPS)) * hn_ref[:, cols] * jax.nn.sigmoid(om_s[rows, cols])
            z_s[rows, D_HALF + p * LANES:D_HALF + (p + 1) * LANES] = y.astype(BF16)
        o = _dot(z_s[rows, :], wout_ref[...])
        o_ref[rows, :] = x_ref[rows, :] + gate1 * _rms(o, ng_ref[1:2, :])
        return carry

    lax.fori_loop(0, t // ROW_CHUNK, finish, 0)


def _odd_scratch(t, lc, latent):
    pad = 2 * CHUNK if latent else 0
    shapes = [
        pltpu.VMEM((t, D_HALF), BF16),
        pltpu.VMEM((t, D_HALF), BF16),
        pltpu.VMEM((N_KV, t + pad, LANES), BF16),
        pltpu.VMEM((N_KV, LANES, t + pad), BF16),
        pltpu.VMEM((t, D_HALF), BF16),
        pltpu.VMEM((t, D_HALF), BF16),
        pltpu.VMEM((t, D_HALF), BF16),
        pltpu.VMEM((t, D_HALF), BF16),
        pltpu.VMEM((D_HALF, t), BF16),
        pltpu.VMEM((D_HALF, t), F32),
        pltpu.VMEM((t, D_HALF), F32),
        pltpu.VMEM((t, LANES), F32),
        pltpu.VMEM((t, D_MODEL), BF16),
        pltpu.VMEM((D_HALF, t), F32),
        pltpu.VMEM((D_HALF, t), F32),
        pltpu.VMEM((2, CHUNK, N_HEADS * LANES), F32),
        pltpu.VMEM((2 * N_PAIRS, LANES, LANES), F32),
        pltpu.VMEM((2 * N_PAIRS, LANES), F32),
        pltpu.VMEM((2 * N_HEADS, LANES), F32),
    ]
    if latent:
        shapes += [pltpu.VMEM((N_KV, lc, LANES), BF16), pltpu.VMEM((N_KV, LANES, lc), BF16)]
    return shapes


def _odd_common_specs(t, cond_base, cond_stride):
    d_main = D_HALF + 2 * LANES + 4 * D_HALF
    return [
        pl.BlockSpec((None, t, D_MODEL), lambda i: (i, 0, 0)),
        pl.BlockSpec((None, 6, D_MODEL), lambda i: (cond_base + cond_stride * i, 0, 0)),
        _const_spec((4, D_MODEL)),
        _const_spec((D_MODEL, d_main)),
        _const_spec((D_MODEL, LANES)),
        _const_spec((1, LANES)),
        _const_spec((N_HEADS, LANES)),
        _const_spec((1, D_HALF)),
        _const_spec((D_MODEL, D_MODEL)),
        _const_spec((2, LANES, N_HEADS * LANES)),
    ]


def _per_seq(shape):
    return pl.BlockSpec((None,) + shape, lambda i: (i,) + (0,) * len(shape))


def _odd_context(x, modv, ng, w_main, w_gate, gate_bias, sink_b, hnorm, wout, sel):
    b, t, _ = x.shape
    kern = functools.partial(_odd_kernel, t=t, lc=t, latent=False)
    return pl.pallas_call(
        kern,
        grid=(b,),
        in_specs=_odd_common_specs(t, 0, 0),
        out_specs=[_per_seq((t, D_MODEL)), _per_seq((t, LANES)), _per_seq((t, LANES)),
                   _per_seq((2 * N_PAIRS, LANES, LANES)), _per_seq((2 * N_PAIRS, LANES)),
                   _per_seq((2 * N_HEADS, LANES))],
        out_shape=[jax.ShapeDtypeStruct((b, t, D_MODEL), F32),
                   jax.ShapeDtypeStruct((b, t, LANES), F32),
                   jax.ShapeDtypeStruct((b, t, LANES), F32),
                   jax.ShapeDtypeStruct((b, 2 * N_PAIRS, LANES, LANES), F32),
                   jax.ShapeDtypeStruct((b, 2 * N_PAIRS, LANES), F32),
                   jax.ShapeDtypeStruct((b, 2 * N_HEADS, LANES), F32)],
        scratch_shapes=_odd_scratch(t, t, False),
        compiler_params=_params(1),
        name="odd_mixer_context",
    )(x, modv, ng, w_main, w_gate, gate_bias, sink_b, hnorm, wout, sel)


def _odd_latent(x, modv, ng, w_main, w_gate, gate_bias, sink_b, hnorm, wout, sel, cos_t, sin_t, kc, vc,
                c_in, n_in, m_in):
    b, t, _ = x.shape
    lc = kc.shape[2]
    kern = functools.partial(_odd_kernel, t=t, lc=lc, latent=True)
    return pl.pallas_call(
        kern,
        grid=(b,),
        in_specs=_odd_common_specs(t, 1, 1) + [
            _const_spec((t, LANES)), _const_spec((t, LANES)),
            _per_seq((N_KV, lc, LANES)), _per_seq((N_KV, lc, LANES)),
            _per_seq((2 * N_PAIRS, LANES, LANES)), _per_seq((2 * N_PAIRS, LANES)), _per_seq((2 * N_HEADS, LANES)),
        ],
        out_specs=_per_seq((t, D_MODEL)),
        out_shape=jax.ShapeDtypeStruct((b, t, D_MODEL), F32),
        scratch_shapes=_odd_scratch(t, lc, True),
        compiler_params=_params(1),
        name="odd_mixer_latent",
    )(x, modv, ng, w_main, w_gate, gate_bias, sink_b, hnorm, wout, sel, cos_t, sin_t, kc, vc, c_in, n_in, m_in)


def _gate_select():
    c = jnp.arange(LANES)[:, None]
    h = jnp.arange(N_HEADS * LANES)[None, :] // LANES
    return jnp.stack([((c == N_HEADS * d + h) | (c == 2 * N_HEADS + N_HEADS * d + h)).astype(BF16)
                      for d in range(2)])


def _rope_tables(t):
    rows = t // GRID_W
    row = jnp.broadcast_to(jnp.arange(rows)[:, None], (rows, GRID_W)).reshape(t).astype(F32)
    col = jnp.broadcast_to(jnp.arange(GRID_W)[None, :], (rows, GRID_W)).reshape(t).astype(F32)
    n_freq = HEAD_DIM // 4
    inv_freq = ROPE_BASE ** (-jnp.arange(n_freq, dtype=F32) / n_freq)
    ang = jnp.concatenate([row[:, None] * inv_freq, col[:, None] * inv_freq], axis=-1)
    cos, sin = jnp.cos(ang), jnp.sin(ang)
    cos_l = jnp.tile(cos, (1, LANES // cos.shape[1]))
    sin_l = jnp.tile(jnp.concatenate([-sin, sin], axis=-1), (1, LANES // HEAD_DIM))
    return cos_l, sin_l


def _pair_blockdiag(c):
    b = c.shape[0]
    c = c.reshape(b, N_PAIRS, 2, HEAD_DIM, HEAD_DIM)
    z = jnp.zeros_like(c[:, :, 0])
    top = jnp.concatenate([c[:, :, 0], z], axis=-1)
    bot = jnp.concatenate([z, c[:, :, 1]], axis=-1)
    return jnp.concatenate([top, bot], axis=-2)


def _pair_diag_blocks(cbd):
    b = cbd.shape[0]
    a = cbd[:, :, :HEAD_DIM, :HEAD_DIM]
    d = cbd[:, :, HEAD_DIM:, HEAD_DIM:]
    return jnp.stack([a, d], axis=2).reshape(b, N_HEADS, HEAD_DIM, HEAD_DIM)


def _lane_bcast(v):
    return jnp.broadcast_to(v[..., None], v.shape + (LANES,))


def kernel(x_prompt, x_sample, c, cache_k, cache_v, state_c_fwd, state_n_fwd, state_m_fwd, state_c_bwd, state_n_bwd, state_m_bwd, c_ctx, mod_w, mod_b, norm_g, mlp_w1, mlp_w2, even_in_w, conv_a_w, conv_a_b, ln_a_g, ln_a_b, conv_b_w, even_out_w, odd_in_w, attn_sink, gate_b, hnorm_g, odd_out_w):
    n_dec = x_sample.shape[0]
    n_ctx = x_prompt.shape[0]
    cond = jnp.concatenate([c_ctx[None, :], c, jnp.zeros((COND_ROWS - 1 - n_dec, D_MODEL), F32)], axis=0)
    modv = _modulation(cond, mod_w, mod_b)

    yp, ys = x_prompt, x_sample

    ev = (even_in_w[0].astype(BF16), conv_a_w[0], conv_a_b[0][None, :], ln_a_g[0][None, :], ln_a_b[0][None, :],
          conv_b_w[0], even_out_w[0].astype(BF16))
    yp = _even_layer(yp, modv[0], 0, 0, norm_g[0], *ev)
    ys = _even_layer(ys, modv[0], 1, 1, norm_g[0], *ev)
    w1, w2 = mlp_w1[0].astype(BF16), mlp_w2[0].astype(BF16)
    yp = _mlp_layer(yp, modv[0], 0, 0, norm_g[0], w1, w2)
    ys = _mlp_layer(ys, modv[0], 1, 1, norm_g[0], w1, w2)

    w_in = odd_in_w[0]
    d_main = D_HALF + 2 * LANES + 4 * D_HALF
    w_main = w_in[:, :d_main].astype(BF16)
    order = jnp.array([0, 2, 1, 3])
    wg = w_in[:, d_main:].reshape(D_MODEL, 4, N_HEADS)[:, order, :].reshape(D_MODEL, 4 * N_HEADS)
    w_gate = jnp.pad(wg, ((0, 0), (0, LANES - 4 * N_HEADS))).astype(BF16)
    gate_bias = jnp.pad(gate_b[0][order, :].reshape(1, 4 * N_HEADS), ((0, 0), (0, LANES - 4 * N_HEADS)))
    sink_b = _lane_bcast(attn_sink[0])
    hnorm = hnorm_g[0][None, :]
    wout = odd_out_w[0].astype(BF16)
    odd = (norm_g[1], w_main, w_gate, gate_bias, sink_b, hnorm, wout, _gate_select())

    op, k_new, v_new, c_new, n_new, m_new = _odd_context(yp, modv[1], *odd)

    t_dec = x_sample.shape[1]
    cos_t, sin_t = _rope_tables(t_dec)
    kc = jnp.concatenate([cache_k[:, 0], cache_k[:, 0]], axis=-1)
    vc = jnp.concatenate([cache_v[:, 0], cache_v[:, 0]], axis=-1)
    c_in = jnp.concatenate([_pair_blockdiag(jnp.swapaxes(state_c_fwd[:, 0], -1, -2)),
                            _pair_blockdiag(jnp.swapaxes(state_c_bwd[:, 0], -1, -2))], axis=1)
    n_in = jnp.concatenate([state_n_fwd[:, 0].reshape(n_dec, N_PAIRS, LANES),
                            state_n_bwd[:, 0].reshape(n_dec, N_PAIRS, LANES)], axis=1)
    m_in = _lane_bcast(jnp.concatenate([state_m_fwd[:, 0], state_m_bwd[:, 0]], axis=1))
    os_ = _odd_latent(ys, modv[1], *odd, cos_t, sin_t, kc, vc, c_in, n_in, m_in)

    w1, w2 = mlp_w1[1].astype(BF16), mlp_w2[1].astype(BF16)
    yp = _mlp_layer(op, modv[1], 0, 0, norm_g[1], w1, w2)
    ys = _mlp_layer(os_, modv[1], 1, 1, norm_g[1], w1, w2)

    t_ctx = x_prompt.shape[1]
    heads_first = lambda a: a.reshape(n_ctx, t_ctx, N_KV, HEAD_DIM).transpose(0, 2, 1, 3)[:, None]
    new_k = heads_first(k_new)
    new_v = heads_first(v_new)
    c_f = jnp.swapaxes(_pair_diag_blocks(c_new[:, :N_PAIRS]), -1, -2)[:, None]
    c_b = jnp.swapaxes(_pair_diag_blocks(c_new[:, N_PAIRS:]), -1, -2)[:, None]
    n_f = n_new[:, :N_PAIRS].reshape(n_ctx, N_HEADS, HEAD_DIM)[:, None]
    n_b = n_new[:, N_PAIRS:].reshape(n_ctx, N_HEADS, HEAD_DIM)[:, None]
    m_f = m_new[:, :N_HEADS, 0][:, None]
    m_b = m_new[:, N_HEADS:, 0][:, None]
    return (yp, ys, new_k, new_v, c_f, n_f, m_f, c_b, n_b, m_b)
```

```python
import functools

import jax
import jax.numpy as jnp
from jax import lax
from jax.experimental import pallas as pl
from jax.experimental.pallas import tpu as pltpu

F32 = jnp.float32
BF16 = jnp.bfloat16

D_MODEL = 1024
D_FF = 4 * D_MODEL
EPS = 1e-6
D_HALF = D_MODEL // 2
CONF_WIDTH = 31
CONF_HALO = 16
HEAD_DIM = 64
N_HEADS = 8
N_PAIRS = N_HEADS // 2
N_KV = 2
LANES = 128
CHUNK = 128
WINDOW = 128
GRID_W = 64
ROPE_BASE = 10000.0
ATT_SCALE = HEAD_DIM ** -0.5
ROW_CHUNK = 256
COND_ROWS = 8
VMEM_LIMIT = 56 * 1024 * 1024


def _dot(a, b):
    return jnp.dot(a, b, preferred_element_type=F32)


def _dot_nt(a, b):
    return lax.dot_general(a, b, (((1,), (1,)), ((), ())), preferred_element_type=F32)


def _split3(x):
    hi = x.astype(BF16)
    r1 = x - hi.astype(F32)
    mid = r1.astype(BF16)
    lo = (r1 - mid.astype(F32)).astype(BF16)
    return hi, mid, lo


def _dot_exact_lhs(x, b01):
    hi, mid, lo = _split3(x)
    return _dot(hi, b01) + _dot(mid, b01) + _dot(lo, b01)


def _rms(x, g):
    return x * lax.rsqrt(jnp.mean(x * x, axis=-1, keepdims=True) + EPS) * g


def _norm_mod(x, g, shift, scale):
    return _rms(x, g) * (1.0 + scale) + shift


def _params(n_grid):
    return pltpu.CompilerParams(dimension_semantics=("arbitrary",) * n_grid, vmem_limit_bytes=VMEM_LIMIT)


def _const_spec(shape):
    zeros = (0,) * len(shape)
    return pl.BlockSpec(shape, lambda *_: zeros, pipeline_mode=pl.Buffered(1))


def _mod_kernel(cond_ref, w_ref, b_ref, o_ref):
    s = jax.nn.silu(cond_ref[...]).astype(BF16)
    o_ref[...] = _dot(s, w_ref[...].astype(BF16)) + b_ref[...]


def _modulation(cond, mod_w, mod_b):
    depth = mod_w.shape[0]
    n_out = mod_w.shape[2]
    tn = D_MODEL
    out = pl.pallas_call(
        _mod_kernel,
        grid=(depth, n_out // tn),
        in_specs=[
            pl.BlockSpec((COND_ROWS, D_MODEL), lambda l, j: (0, 0)),
            pl.BlockSpec((None, D_MODEL, tn), lambda l, j: (l, 0, j)),
            pl.BlockSpec((None, 1, tn), lambda l, j: (l, 0, j)),
        ],
        out_specs=pl.BlockSpec((None, COND_ROWS, tn), lambda l, j: (l, 0, j)),
        out_shape=jax.ShapeDtypeStruct((depth, COND_ROWS, n_out), F32),
        compiler_params=_params(2),
        name="modulation",
    )(cond, mod_w, mod_b.reshape(depth, 1, n_out))
    return out.reshape(depth, COND_ROWS, 6, D_MODEL)


def _even_kernel(xp_ref, xc_ref, xn_ref, mod_ref, ng_ref, win_ref, caw_ref, cab_ref, lng_ref, lnb_ref,
                 cbw_ref, wout_ref, o_ref, apad, cpad, bgs, zs, *, n_chunks):
    c = pl.program_id(1)
    rows = ROW_CHUNK + 2 * CONF_HALO
    shift1, scale1, gate1 = mod_ref[0:1, :], mod_ref[1:2, :], mod_ref[2:3, :]
    xc = xc_ref[...]
    xh = jnp.concatenate([xp_ref[...], xc, xn_ref[...]], axis=0)
    h = _norm_mod(xh, ng_ref[0:1, :], shift1, scale1).astype(BF16)
    ri = lax.broadcasted_iota(jnp.int32, (rows, D_HALF), 0)
    lo = jnp.where(c == 0, CONF_HALO, 0)
    hi = jnp.where(c == n_chunks - 1, CONF_HALO + ROW_CHUNK, rows)
    inside = (ri >= lo) & (ri < hi)
    a = _dot(h, win_ref[:, 0:D_HALF]) * jax.nn.sigmoid(_dot(h, win_ref[:, D_HALF:2 * D_HALF]))
    apad[...] = jnp.where(inside, a, 0.0)
    cx = _dot(h, win_ref[:, 3 * D_HALF:4 * D_HALF]) * _dot(h, win_ref[:, 4 * D_HALF:5 * D_HALF])
    cpad[...] = jnp.where(inside, cx, 0.0)
    bgs[...] = _dot(h[CONF_HALO:CONF_HALO + ROW_CHUNK], win_ref[:, 2 * D_HALF:3 * D_HALF])

    sub = 32
    for j in range(ROW_CHUNK // sub):
        r0 = j * sub
        acc = caw_ref[0:1, :] * apad[r0 + 1:r0 + 1 + sub, :]
        for k in range(1, CONF_WIDTH):
            acc = acc + caw_ref[k:k + 1, :] * apad[r0 + 1 + k:r0 + 1 + k + sub, :]
        acc = acc + cab_ref[...]
        mu = jnp.mean(acc, axis=-1, keepdims=True)
        dlt = acc - mu
        var = jnp.mean(dlt * dlt, axis=-1, keepdims=True)
        a_out = jax.nn.silu(dlt * lax.rsqrt(var + EPS) * lng_ref[...] + lnb_ref[...])
        zs[r0:r0 + sub, 0:D_HALF] = a_out.astype(BF16)
        base = r0 + CONF_HALO - 1
        sc = (cbw_ref[0:1, :] * cpad[base:base + sub, :]
              + cbw_ref[1:2, :] * cpad[base + 1:base + 1 + sub, :]
              + cbw_ref[2:3, :] * cpad[base + 2:base + 2 + sub, :])
        zs[r0:r0 + sub, D_HALF:D_MODEL] = (bgs[r0:r0 + sub, :] * sc).astype(BF16)

    o = _dot(zs[...], wout_ref[...])
    o_ref[...] = xc + gate1 * _rms(o, ng_ref[1:2, :])


def _even_layer(x, modv, cond_base, cond_stride, ng, win, caw, cab, lng, lnb, cbw, wout):
    b, t, _ = x.shape
    n_chunks = t // ROW_CHUNK
    hpc = ROW_CHUNK // CONF_HALO
    n_halo_blocks = t // CONF_HALO
    rows = ROW_CHUNK + 2 * CONF_HALO
    kern = functools.partial(_even_kernel, n_chunks=n_chunks)
    return pl.pallas_call(
        kern,
        grid=(b, n_chunks),
        in_specs=[
            pl.BlockSpec((None, CONF_HALO, D_MODEL), lambda i, c: (i, jnp.maximum(c * hpc - 1, 0), 0)),
            pl.BlockSpec((None, ROW_CHUNK, D_MODEL), lambda i, c: (i, c, 0)),
            pl.BlockSpec((None, CONF_HALO, D_MODEL),
                         lambda i, c: (i, jnp.minimum((c + 1) * hpc, n_halo_blocks - 1), 0)),
            pl.BlockSpec((None, 6, D_MODEL), lambda i, c: (cond_base + cond_stride * i, 0, 0)),
            _const_spec((4, D_MODEL)),
            _const_spec((D_MODEL, 5 * D_HALF)),
            _const_spec((CONF_WIDTH, D_HALF)),
            _const_spec((1, D_HALF)),
            _const_spec((1, D_HALF)),
            _const_spec((1, D_HALF)),
            _const_spec((3, D_HALF)),
            _const_spec((D_MODEL, D_MODEL)),
        ],
        out_specs=pl.BlockSpec((None, ROW_CHUNK, D_MODEL), lambda i, c: (i, c, 0)),
        out_shape=jax.ShapeDtypeStruct(x.shape, F32),
        scratch_shapes=[
            pltpu.VMEM((rows, D_HALF), F32),
            pltpu.VMEM((rows, D_HALF), F32),
            pltpu.VMEM((ROW_CHUNK, D_HALF), F32),
            pltpu.VMEM((ROW_CHUNK, D_MODEL), BF16),
        ],
        compiler_params=_params(2),
        name="even_mixer",
    )(x, x, x, modv, ng, win, caw, cab, lng, lnb, cbw, wout)


def _mlp_kernel(x_ref, mod_ref, ng_ref, w1_ref, w2_ref, o_ref):
    x = x_ref[...]
    h = _norm_mod(x, ng_ref[2:3, :], mod_ref[3:4, :], mod_ref[4:5, :]).astype(BF16)
    acc = None
    for c in range(D_FF // D_MODEL):
        cols = slice(c * D_MODEL, (c + 1) * D_MODEL)
        hid = jnp.square(jnp.maximum(_dot(h, w1_ref[:, cols]), 0.0)).astype(BF16)
        part = _dot(hid, w2_ref[cols, :])
        acc = part if acc is None else acc + part
    o_ref[...] = x + mod_ref[5:6, :] * _rms(acc, ng_ref[3:4, :])


def _mlp_layer(x, modv, cond_base, cond_stride, ng, w1, w2):
    b, t, _ = x.shape
    tm = 512
    assert (b * t) % tm == 0 and (cond_stride == 0 or t % tm == 0)
    tiles_per_seq = max(t // tm, 1)
    x2 = x.reshape(b * t, D_MODEL)
    out = pl.pallas_call(
        _mlp_kernel,
        grid=(b * t // tm,),
        in_specs=[
            pl.BlockSpec((tm, D_MODEL), lambda i: (i, 0)),
            pl.BlockSpec((None, 6, D_MODEL), lambda i: (cond_base + cond_stride * (i // tiles_per_seq), 0, 0)),
            _const_spec((4, D_MODEL)),
            _const_spec((D_MODEL, D_FF)),
            _const_spec((D_FF, D_MODEL)),
        ],
        out_specs=pl.BlockSpec((tm, D_MODEL), lambda i: (i, 0)),
        out_shape=jax.ShapeDtypeStruct(x2.shape, F32),
        compiler_params=_params(1),
        name="mlp",
    )(x2, modv, ng, w1, w2)
    return out.reshape(b, t, D_MODEL)


def _log_sigmoid(x):
    return jnp.minimum(x, 0.0) - jnp.log(1.0 + jnp.exp(-jnp.abs(x)))


def _rot_half(x, first_half):
    return jnp.where(first_half, pltpu.roll(x, 96, axis=1), pltpu.roll(x, 32, axis=1))


def _values_and_ones(v):
    vt = jnp.transpose(v)
    row = lax.broadcasted_iota(jnp.int32, vt.shape, 0)
    return jnp.where(row < HEAD_DIM, vt, 1.0)


def _odd_kernel(*refs, t, lc, latent):
    if latent:
        (x_ref, mod_ref, ng_ref, win_ref, wg_ref, gb_ref, sink_ref, hn_ref, wout_ref,
         cos_ref, sin_ref, kc_ref, vc_ref, cin_ref, nin_ref, min_ref,
         o_ref,
         qa_s, qb_s, kk_s, vvt_s, qma_s, qmb_s, qm_s, km_s, vmt_s, vmtf_s, om_s, g_s, z_s, hft_s, hbt_s,
         sr_s, st_s, qc_s, c_s, n_s, m_s, kc_s, vct_s) = refs
    else:
        (x_ref, mod_ref, ng_ref, win_ref, wg_ref, gb_ref, sink_ref, hn_ref, wout_ref,
         o_ref, ko_ref, vo_ref, co_ref, no_ref, mo_ref,
         qa_s, qb_s, kk_s, vvt_s, qma_s, qmb_s, qm_s, km_s, vmt_s, vmtf_s, om_s, g_s, z_s, hft_s, hbt_s,
         sr_s, st_s, qc_s, c_s, n_s, m_s) = refs

    n_blocks = t // CHUNK
    pad = CHUNK if latent else 0
    shift1, scale1, gate1 = mod_ref[0:1, :], mod_ref[1:2, :], mod_ref[2:3, :]
    lane = lax.broadcasted_iota(jnp.int32, (1, LANES), 1)
    left = lane < HEAD_DIM
    first_half = (lane % HEAD_DIM) < (HEAD_DIM // 2)
    ti = lax.broadcasted_iota(jnp.int32, (CHUNK, CHUNK), 0)
    si = lax.broadcasted_iota(jnp.int32, (CHUNK, CHUNK), 1)
    top = ti < HEAD_DIM
    same_head = top == (si < HEAD_DIM)

    if latent:
        for kv in range(N_KV):
            kk_s[kv, 0:CHUNK, :] = jnp.zeros((CHUNK, LANES), BF16)
            kk_s[kv, CHUNK + t:2 * CHUNK + t, :] = jnp.zeros((CHUNK, LANES), BF16)
            vvt_s[kv, :, 0:CHUNK] = jnp.zeros((LANES, CHUNK), BF16)
            vvt_s[kv, :, CHUNK + t:2 * CHUNK + t] = jnp.zeros((LANES, CHUNK), BF16)
            kc_s[kv] = kc_ref[kv].astype(BF16)
            vct_s[kv] = _values_and_ones(vc_ref[kv]).astype(BF16)

    def project(rc, carry):
        r0 = pl.multiple_of(rc * ROW_CHUNK, ROW_CHUNK)
        rows = pl.ds(r0, ROW_CHUNK)
        krows = pl.ds(r0 + pad, ROW_CHUNK)
        h = _norm_mod(x_ref[rows, :], ng_ref[0:1, :], shift1, scale1).astype(BF16)
        q = _dot(h, win_ref[:, 0:D_HALF])
        kv2 = _dot(h, win_ref[:, D_HALF:D_HALF + 2 * LANES])
        ka, va = kv2[:, 0:LANES], kv2[:, LANES:2 * LANES]
        if latent:
            cs, sn = cos_ref[rows, :], sin_ref[rows, :]
            ka = ka * cs + _rot_half(ka, first_half) * sn
        else:
            ko_ref[rows, :] = ka
            vo_ref[rows, :] = va
        kr = pltpu.roll(ka, HEAD_DIM, axis=1)
        vr = pltpu.roll(va, HEAD_DIM, axis=1)
        kk_s[0, krows, :] = jnp.where(left, ka, kr).astype(BF16)
        kk_s[1, krows, :] = jnp.where(left, kr, ka).astype(BF16)
        vvt_s[0, :, krows] = _values_and_ones(va).astype(BF16)
        vvt_s[1, :, krows] = _values_and_ones(vr).astype(BF16)
        base = D_HALF + 2 * LANES
        qm = _dot(h, win_ref[:, base:base + D_HALF])
        qm_s[rows, :] = qm.astype(BF16)
        vm = _dot(h, win_ref[:, base + 2 * D_HALF:base + 3 * D_HALF])
        for p in range(N_PAIRS):
            cols = slice(p * LANES, (p + 1) * LANES)
            qp = q[:, cols]
            if latent:
                qp = qp * cs + _rot_half(qp, first_half) * sn
            qp = qp * ATT_SCALE
            qa_s[rows, cols] = jnp.where(left, qp, 0.0).astype(BF16)
            qb_s[rows, cols] = jnp.where(left, 0.0, qp).astype(BF16)
            qma_s[rows, cols] = jnp.where(left, qm[:, cols], 0.0).astype(BF16)
            qmb_s[rows, cols] = jnp.where(left, 0.0, qm[:, cols]).astype(BF16)
            vt = jnp.transpose(vm[:, cols])
            vmtf_s[cols, rows] = vt
            vmt_s[cols, rows] = vt.astype(BF16)
        km = _dot(h, win_ref[:, base + D_HALF:base + 2 * D_HALF]) * (HEAD_DIM ** -0.5)
        km_s[rows, :] = km.astype(BF16)
        om_s[rows, :] = _dot(h, win_ref[:, base + 3 * D_HALF:base + 4 * D_HALF])
        g_s[rows, :] = _dot(h, wg_ref[...]) + gb_ref[...]
        return carry

    lax.fori_loop(0, t // ROW_CHUNK, project, 0)

    kj = lax.broadcasted_iota(jnp.int32, (3 * CHUNK, 2 * CHUNK), 0)
    qi = lax.broadcasted_iota(jnp.int32, (3 * CHUNK, 2 * CHUNK), 1) % CHUNK
    band_ok = jnp.abs(kj - CHUNK - qi) <= WINDOW
    head_a = lax.broadcasted_iota(jnp.int32, (1, 2 * CHUNK), 1) < CHUNK

    n_keys = lc + 3 * CHUNK if latent else lc
    group = 2 if latent else N_PAIRS

    def attention(s_s, p_s):
        def attend(i, carry):
            r0 = pl.multiple_of(i * CHUNK, CHUNK)
            rows = pl.ds(r0, CHUNK)
            if latent:
                key_pos = kj + (i - 1) * CHUNK
                mask = band_ok & (key_pos >= 0) & (key_pos < t)
                win = pl.ds(r0, 3 * CHUNK)
            for g0 in range(0, N_PAIRS, group):
                for gi in range(group):
                    p = g0 + gi
                    kv = p // (N_PAIRS // N_KV)
                    cols = slice(p * LANES, (p + 1) * LANES)
                    q2 = jnp.concatenate([qa_s[rows, cols], qb_s[rows, cols]], axis=0)
                    if latent:
                        s_s[gi, 0:lc, :] = _dot_nt(kc_s[kv], q2)
                        s_s[gi, lc:n_keys, :] = jnp.where(mask, _dot_nt(kk_s[kv, win, :], q2), -jnp.inf)
                    else:
                        s_s[gi] = _dot_nt(kk_s[kv], q2)
                maxes = []
                for gi in range(group):
                    p = g0 + gi
                    sink = jnp.where(head_a, sink_ref[2 * p:2 * p + 1, 0:1], sink_ref[2 * p + 1:2 * p + 2, 0:1])
                    mx = jnp.maximum(jnp.max(s_s[gi], axis=0, keepdims=True), sink)
                    p_s[gi] = jnp.exp(s_s[gi] - mx).astype(BF16)
                    maxes.append((sink, mx))
                for gi in range(group):
                    p = g0 + gi
                    kv = p // (N_PAIRS // N_KV)
                    cols = slice(p * LANES, (p + 1) * LANES)
                    if latent:
                        num = (_dot(vct_s[kv], p_s[gi, 0:lc, :]) + _dot(vvt_s[kv, :, win], p_s[gi, lc:n_keys, :]))
                    else:
                        num = _dot(vvt_s[kv], p_s[gi])
                    sink, mx = maxes[gi]
                    den = num[HEAD_DIM:HEAD_DIM + 1, :] + jnp.exp(sink - mx)
                    out = num[0:HEAD_DIM, :] * (1.0 / den)
                    pair = jnp.concatenate([out[:, 0:CHUNK], out[:, CHUNK:2 * CHUNK]], axis=0)
                    z_s[rows, cols] = jnp.transpose(pair).astype(BF16)
            return carry

        lax.fori_loop(0, n_blocks, attend, 0)

    pl.run_scoped(attention, pltpu.VMEM((group, n_keys, 2 * CHUNK), F32),
                  pltpu.VMEM((group, n_keys, 2 * CHUNK), BF16))

    if latent:
        c_s[...] = cin_ref[...]
        n_s[...] = nin_ref[...]
        m_s[...] = min_ref[...]
    else:
        c_s[...] = jnp.zeros(c_s.shape, F32)
        n_s[...] = jnp.zeros(n_s.shape, F32)
        m_s[...] = jnp.zeros(m_s.shape, F32)

    see = (ti <= si, ti >= si)
    tri = tuple(m.astype(F32).astype(BF16) for m in see)
    last = (CHUNK - 1, 0)

    def mlstm(i, carry):
        offs = (pl.multiple_of(i * CHUNK, CHUNK), pl.multiple_of((n_blocks - 1 - i) * CHUNK, CHUNK))
        half = (slice(0, LANES), slice(LANES, 2 * LANES))
        a_rows, b_rows = [], []
        for d in range(2):
            gt = jnp.transpose(g_s[pl.ds(offs[d], CHUNK), :])
            lf = _log_sigmoid(gt[2 * N_HEADS:4 * N_HEADS, :])
            bcum = _dot_exact_lhs(lf, tri[d])[8 * d:8 * d + 8, :]
            a_rows.append(gt[8 * d:8 * d + 8, :] - bcum)
            b_rows.append(bcum)
        q_ns = []
        for d in range(2):
            rows = pl.ds(offs[d], CHUNK)
            for p in range(N_PAIRS):
                u = d * N_PAIRS + p
                cols = slice(p * LANES, (p + 1) * LANES)
                q2 = jnp.concatenate([qma_s[rows, cols], qmb_s[rows, cols]], axis=0)
                sr_s[u] = _dot_nt(km_s[rows, cols], q2)
                qc_s[u] = _dot_nt(c_s[u].astype(BF16), qm_s[rows, cols])
                n8 = jnp.broadcast_to(n_s[u:u + 1, :], (8, LANES)).astype(BF16)
                q_ns.append(_dot_nt(n8, q2)[0:1, :])
        stats = {}
        for d in range(2):
            for hd in range(N_HEADS):
                u, j = d * N_PAIRS + hd // 2, hd % 2
                mrow = d * N_HEADS + hd
                a_row = a_rows[d][hd:hd + 1, :]
                b_row = b_rows[d][hd:hd + 1, :]
                m_prev = m_s[mrow:mrow + 1, 0:1]
                a_col = jnp.transpose(jnp.broadcast_to(a_row, (CHUNK, CHUNK)))
                z_t = jnp.where(see[d], a_col, -jnp.inf)
                m_run = jnp.maximum(jnp.max(z_t, axis=0, keepdims=True), m_prev)
                s_t = sr_s[u, :, half[j]] * jnp.exp(z_t - m_run)
                st_s[u, :, half[j]] = s_t.astype(BF16)
                w_int = jnp.exp(m_prev - m_run)
                den = jnp.sum(s_t, axis=0, keepdims=True) + w_int * q_ns[u][:, half[j]]
                inv = 1.0 / jnp.maximum(jnp.abs(den), jnp.exp(-(b_row + m_run)))
                m_last = m_run[:, last[d]:last[d] + 1]
                stats[(u, j)] = (w_int, inv, jnp.exp(a_row - m_last), jnp.exp(m_prev - m_last))
                m_s[mrow:mrow + 1, :] = jnp.broadcast_to(b_row[:, last[d]:last[d] + 1] + m_last, (1, LANES))
        for d in range(2):
            rows = pl.ds(offs[d], CHUNK)
            ht_s = (hft_s, hbt_s)[d]
            for p in range(N_PAIRS):
                u = d * N_PAIRS + p
                cols = slice(p * LANES, (p + 1) * LANES)
                (w_a, inv_a, e_a, dec_a), (w_b, inv_b, e_b, dec_b) = stats[(u, 0)], stats[(u, 1)]
                kp = km_s[rows, cols]
                num2 = _dot(vmt_s[cols, rows], st_s[u])
                num = jnp.where(top, num2[:, half[0]], num2[:, half[1]])
                ht_s[cols, rows] = (num + jnp.where(top, w_a, w_b) * qc_s[u]) * jnp.where(top, inv_a, inv_b)
                vt_e = (vmtf_s[cols, rows] * jnp.where(top, e_a, e_b)).astype(BF16)
                c_s[u] = jnp.where(top, dec_a, dec_b) * c_s[u] + jnp.where(same_head, _dot(vt_e, kp), 0.0)
                e2 = jnp.concatenate([e_a, e_b, jnp.zeros((6, CHUNK), F32)], axis=0).astype(BF16)
                n_k = _dot(e2, kp)
                n_s[u:u + 1, :] = (jnp.where(left, dec_a, dec_b) * n_s[u:u + 1, :]
                                   + jnp.where(left, n_k[0:1, :], n_k[1:2, :]))
        return carry

    lax.fori_loop(0, n_blocks, mlstm, 0)

    if not latent:
        co_ref[...] = c_s[...]
        no_ref[...] = n_s[...]
        mo_ref[...] = m_s[...]

    top_w = lax.broadcasted_iota(jnp.int32, (LANES, ROW_CHUNK), 0) < HEAD_DIM

    def finish(rc, carry):
        r0 = pl.multiple_of(rc * ROW_CHUNK, ROW_CHUNK)
        rows = pl.ds(r0, ROW_CHUNK)
        for p in range(N_PAIRS):
            cols = slice(p * LANES, (p + 1) * LANES)
            hm = hft_s[cols, rows] + hbt_s[cols, rows]
            sq = hm * hm
            ms_a = jnp.sum(sq[0:HEAD_DIM], axis=0, keepdims=True)
            ms_b = jnp.sum(sq[HEAD_DIM:LANES], axis=0, keepdims=True)
            ms = jnp.where(top_w, ms_a, ms_b) * (1.0 / HEAD_DIM)
            y = jnp.transpose(hm * lax.rsqrt(ms + EPS)) * hn_ref[:, cols] * jax.nn.sigmoid(om_s[rows, cols])
            z_s[rows, D_HALF + p * LANES:D_HALF + (p + 1) * LANES] = y.astype(BF16)
        o = _dot(z_s[rows, :], wout_ref[...])
        o_ref[rows, :] = x_ref[rows, :] + gate1 * _rms(o, ng_ref[1:2, :])
        return carry

    lax.fori_loop(0, t // ROW_CHUNK, finish, 0)


def _odd_scratch(t, lc, latent):
    pad = 2 * CHUNK if latent else 0
    shapes = [
        pltpu.VMEM((t, D_HALF), BF16),
        pltpu.VMEM((t, D_HALF), BF16),
        pltpu.VMEM((N_KV, t + pad, LANES), BF16),
        pltpu.VMEM((N_KV, LANES, t + pad), BF16),
        pltpu.VMEM((t, D_HALF), BF16),
        pltpu.VMEM((t, D_HALF), BF16),
        pltpu.VMEM((t, D_HALF), BF16),
        pltpu.VMEM((t, D_HALF), BF16),
        pltpu.VMEM((D_HALF, t), BF16),
        pltpu.VMEM((D_HALF, t), F32),
        pltpu.VMEM((t, D_HALF), F32),
        pltpu.VMEM((t, LANES), F32),
        pltpu.VMEM((t, D_MODEL), BF16),
        pltpu.VMEM((D_HALF, t), F32),
        pltpu.VMEM((D_HALF, t), F32),
        pltpu.VMEM((2 * N_PAIRS, CHUNK, 2 * LANES), F32),
        pltpu.VMEM((2 * N_PAIRS, CHUNK, 2 * LANES), BF16),
        pltpu.VMEM((2 * N_PAIRS, LANES, CHUNK), F32),
        pltpu.VMEM((2 * N_PAIRS, LANES, LANES), F32),
        pltpu.VMEM((2 * N_PAIRS, LANES), F32),
        pltpu.VMEM((2 * N_HEADS, LANES), F32),
    ]
    if latent:
        shapes += [pltpu.VMEM((N_KV, lc, LANES), BF16), pltpu.VMEM((N_KV, LANES, lc), BF16)]
    return shapes


def _odd_common_specs(t, cond_base, cond_stride):
    d_main = D_HALF + 2 * LANES + 4 * D_HALF
    return [
        pl.BlockSpec((None, t, D_MODEL), lambda i: (i, 0, 0)),
        pl.BlockSpec((None, 6, D_MODEL), lambda i: (cond_base + cond_stride * i, 0, 0)),
        _const_spec((4, D_MODEL)),
        _const_spec((D_MODEL, d_main)),
        _const_spec((D_MODEL, LANES)),
        _const_spec((1, LANES)),
        _const_spec((N_HEADS, LANES)),
        _const_spec((1, D_HALF)),
        _const_spec((D_MODEL, D_MODEL)),
    ]


def _per_seq(shape):
    return pl.BlockSpec((None,) + shape, lambda i: (i,) + (0,) * len(shape))


def _odd_context(x, modv, ng, w_main, w_gate, gate_bias, sink_b, hnorm, wout):
    b, t, _ = x.shape
    kern = functools.partial(_odd_kernel, t=t, lc=t, latent=False)
    return pl.pallas_call(
        kern,
        grid=(b,),
        in_specs=_odd_common_specs(t, 0, 0),
        out_specs=[_per_seq((t, D_MODEL)), _per_seq((t, LANES)), _per_seq((t, LANES)),
                   _per_seq((2 * N_PAIRS, LANES, LANES)), _per_seq((2 * N_PAIRS, LANES)),
                   _per_seq((2 * N_HEADS, LANES))],
        out_shape=[jax.ShapeDtypeStruct((b, t, D_MODEL), F32),
                   jax.ShapeDtypeStruct((b, t, LANES), F32),
                   jax.ShapeDtypeStruct((b, t, LANES), F32),
                   jax.ShapeDtypeStruct((b, 2 * N_PAIRS, LANES, LANES), F32),
                   jax.ShapeDtypeStruct((b, 2 * N_PAIRS, LANES), F32),
                   jax.ShapeDtypeStruct((b, 2 * N_HEADS, LANES), F32)],
        scratch_shapes=_odd_scratch(t, t, False),
        compiler_params=_params(1),
        name="odd_mixer_context",
    )(x, modv, ng, w_main, w_gate, gate_bias, sink_b, hnorm, wout)


def _odd_latent(x, modv, ng, w_main, w_gate, gate_bias, sink_b, hnorm, wout, cos_t, sin_t, kc, vc,
                c_in, n_in, m_in):
    b, t, _ = x.shape
    lc = kc.shape[2]
    kern = functools.partial(_odd_kernel, t=t, lc=lc, latent=True)
    return pl.pallas_call(
        kern,
        grid=(b,),
        in_specs=_odd_common_specs(t, 1, 1) + [
            _const_spec((t, LANES)), _const_spec((t, LANES)),
            _per_seq((N_KV, lc, LANES)), _per_seq((N_KV, lc, LANES)),
            _per_seq((2 * N_PAIRS, LANES, LANES)), _per_seq((2 * N_PAIRS, LANES)), _per_seq((2 * N_HEADS, LANES)),
        ],
        out_specs=_per_seq((t, D_MODEL)),
        out_shape=jax.ShapeDtypeStruct((b, t, D_MODEL), F32),
        scratch_shapes=_odd_scratch(t, lc, True),
        compiler_params=_params(1),
        name="odd_mixer_latent",
    )(x, modv, ng, w_main, w_gate, gate_bias, sink_b, hnorm, wout, cos_t, sin_t, kc, vc, c_in, n_in, m_in)


def _rope_tables(t):
    rows = t // GRID_W
    row = jnp.broadcast_to(jnp.arange(rows)[:, None], (rows, GRID_W)).reshape(t).astype(F32)
    col = jnp.broadcast_to(jnp.arange(GRID_W)[None, :], (rows, GRID_W)).reshape(t).astype(F32)
    n_freq = HEAD_DIM // 4
    inv_freq = ROPE_BASE ** (-jnp.arange(n_freq, dtype=F32) / n_freq)
    ang = jnp.concatenate([row[:, None] * inv_freq, col[:, None] * inv_freq], axis=-1)
    cos, sin = jnp.cos(ang), jnp.sin(ang)
    cos_l = jnp.tile(cos, (1, LANES // cos.shape[1]))
    sin_l = jnp.tile(jnp.concatenate([-sin, sin], axis=-1), (1, LANES // HEAD_DIM))
    return cos_l, sin_l


def _pair_blockdiag(c):
    b = c.shape[0]
    c = c.reshape(b, N_PAIRS, 2, HEAD_DIM, HEAD_DIM)
    z = jnp.zeros_like(c[:, :, 0])
    top = jnp.concatenate([c[:, :, 0], z], axis=-1)
    bot = jnp.concatenate([z, c[:, :, 1]], axis=-1)
    return jnp.concatenate([top, bot], axis=-2)


def _pair_diag_blocks(cbd):
    b = cbd.shape[0]
    a = cbd[:, :, :HEAD_DIM, :HEAD_DIM]
    d = cbd[:, :, HEAD_DIM:, HEAD_DIM:]
    return jnp.stack([a, d], axis=2).reshape(b, N_HEADS, HEAD_DIM, HEAD_DIM)


def _lane_bcast(v):
    return jnp.broadcast_to(v[..., None], v.shape + (LANES,))


def kernel(x_prompt, x_sample, c, cache_k, cache_v, state_c_fwd, state_n_fwd, state_m_fwd, state_c_bwd, state_n_bwd, state_m_bwd, c_ctx, mod_w, mod_b, norm_g, mlp_w1, mlp_w2, even_in_w, conv_a_w, conv_a_b, ln_a_g, ln_a_b, conv_b_w, even_out_w, odd_in_w, attn_sink, gate_b, hnorm_g, odd_out_w):
    n_dec = x_sample.shape[0]
    n_ctx = x_prompt.shape[0]
    cond = jnp.concatenate([c_ctx[None, :], c, jnp.zeros((COND_ROWS - 1 - n_dec, D_MODEL), F32)], axis=0)
    modv = _modulation(cond, mod_w, mod_b)

    yp, ys = x_prompt, x_sample

    ev = (even_in_w[0].astype(BF16), conv_a_w[0], conv_a_b[0][None, :], ln_a_g[0][None, :], ln_a_b[0][None, :],
          conv_b_w[0], even_out_w[0].astype(BF16))
    yp = _even_layer(yp, modv[0], 0, 0, norm_g[0], *ev)
    ys = _even_layer(ys, modv[0], 1, 1, norm_g[0], *ev)
    w1, w2 = mlp_w1[0].astype(BF16), mlp_w2[0].astype(BF16)
    yp = _mlp_layer(yp, modv[0], 0, 0, norm_g[0], w1, w2)
    ys = _mlp_layer(ys, modv[0], 1, 1, norm_g[0], w1, w2)

    w_in = odd_in_w[0]
    d_main = D_HALF + 2 * LANES + 4 * D_HALF
    w_main = w_in[:, :d_main].astype(BF16)
    order = jnp.array([0, 2, 1, 3])
    wg = w_in[:, d_main:].reshape(D_MODEL, 4, N_HEADS)[:, order, :].reshape(D_MODEL, 4 * N_HEADS)
    w_gate = jnp.pad(wg, ((0, 0), (0, LANES - 4 * N_HEADS))).astype(BF16)
    gate_bias = jnp.pad(gate_b[0][order, :].reshape(1, 4 * N_HEADS), ((0, 0), (0, LANES - 4 * N_HEADS)))
    sink_b = _lane_bcast(attn_sink[0])
    hnorm = hnorm_g[0][None, :]
    wout = odd_out_w[0].astype(BF16)
    odd = (norm_g[1], w_main, w_gate, gate_bias, sink_b, hnorm, wout)

    op, k_new, v_new, c_new, n_new, m_new = _odd_context(yp, modv[1], *odd)

    t_dec = x_sample.shape[1]
    cos_t, sin_t = _rope_tables(t_dec)
    kc = jnp.concatenate([cache_k[:, 0], cache_k[:, 0]], axis=-1)
    vc = jnp.concatenate([cache_v[:, 0], cache_v[:, 0]], axis=-1)
    c_in = jnp.concatenate([_pair_blockdiag(jnp.swapaxes(state_c_fwd[:, 0], -1, -2)),
                            _pair_blockdiag(jnp.swapaxes(state_c_bwd[:, 0], -1, -2))], axis=1)
    n_in = jnp.concatenate([state_n_fwd[:, 0].reshape(n_dec, N_PAIRS, LANES),
                            state_n_bwd[:, 0].reshape(n_dec, N_PAIRS, LANES)], axis=1)
    m_in = _lane_bcast(jnp.concatenate([state_m_fwd[:, 0], state_m_bwd[:, 0]], axis=1))
    os_ = _odd_latent(ys, modv[1], *odd, cos_t, sin_t, kc, vc, c_in, n_in, m_in)

    w1, w2 = mlp_w1[1].astype(BF16), mlp_w2[1].astype(BF16)
    yp = _mlp_layer(op, modv[1], 0, 0, norm_g[1], w1, w2)
    ys = _mlp_layer(os_, modv[1], 1, 1, norm_g[1], w1, w2)

    t_ctx = x_prompt.shape[1]
    heads_first = lambda a: a.reshape(n_ctx, t_ctx, N_KV, HEAD_DIM).transpose(0, 2, 1, 3)[:, None]
    new_k = heads_first(k_new)
    new_v = heads_first(v_new)
    c_f = jnp.swapaxes(_pair_diag_blocks(c_new[:, :N_PAIRS]), -1, -2)[:, None]
    c_b = jnp.swapaxes(_pair_diag_blocks(c_new[:, N_PAIRS:]), -1, -2)[:, None]
    n_f = n_new[:, :N_PAIRS].reshape(n_ctx, N_HEADS, HEAD_DIM)[:, None]
    n_b = n_new[:, N_PAIRS:].reshape(n_ctx, N_HEADS, HEAD_DIM)[:, None]
    m_f = m_new[:, :N_HEADS, 0][:, None]
    m_b = m_new[:, N_HEADS:, 0][:, None]
    return (yp, ys, new_k, new_v, c_f, n_f, m_f, c_b, n_b, m_b)
```

```python
import functools

import jax
import jax.numpy as jnp
from jax import lax
from jax.experimental import pallas as pl
from jax.experimental.pallas import tpu as pltpu

F32 = jnp.float32
BF16 = jnp.bfloat16

D_MODEL = 1024
D_FF = 4 * D_MODEL
EPS = 1e-6
D_HALF = D_MODEL // 2
CONF_WIDTH = 31
CONF_HALO = 16
HEAD_DIM = 64
N_HEADS = 8
N_PAIRS = N_HEADS // 2
N_KV = 2
LANES = 128
CHUNK = 128
WINDOW = 128
GRID_W = 64
ROPE_BASE = 10000.0
ATT_SCALE = HEAD_DIM ** -0.5
ROW_CHUNK = 256
COND_ROWS = 8
VMEM_LIMIT = 56 * 1024 * 1024


def _dot(a, b):
    return jnp.dot(a, b, preferred_element_type=F32)


def _dot_nt(a, b):
    return lax.dot_general(a, b, (((1,), (1,)), ((), ())), preferred_element_type=F32)


def _split3(x):
    hi = x.astype(BF16)
    r1 = x - hi.astype(F32)
    mid = r1.astype(BF16)
    lo = (r1 - mid.astype(F32)).astype(BF16)
    return hi, mid, lo


def _dot_exact_lhs(x, b01):
    hi, mid, lo = _split3(x)
    return _dot(hi, b01) + _dot(mid, b01) + _dot(lo, b01)


def _rms(x, g):
    return x * lax.rsqrt(jnp.mean(x * x, axis=-1, keepdims=True) + EPS) * g


def _norm_mod(x, g, shift, scale):
    return _rms(x, g) * (1.0 + scale) + shift


def _params(n_grid):
    return pltpu.CompilerParams(dimension_semantics=("arbitrary",) * n_grid, vmem_limit_bytes=VMEM_LIMIT)


def _const_spec(shape):
    zeros = (0,) * len(shape)
    return pl.BlockSpec(shape, lambda *_: zeros, pipeline_mode=pl.Buffered(1))


def _mod_kernel(cond_ref, w_ref, b_ref, o_ref):
    s = jax.nn.silu(cond_ref[...]).astype(BF16)
    o_ref[...] = _dot(s, w_ref[...].astype(BF16)) + b_ref[...]


def _modulation(cond, mod_w, mod_b):
    depth = mod_w.shape[0]
    n_out = mod_w.shape[2]
    tn = D_MODEL
    out = pl.pallas_call(
        _mod_kernel,
        grid=(depth, n_out // tn),
        in_specs=[
            pl.BlockSpec((COND_ROWS, D_MODEL), lambda l, j: (0, 0)),
            pl.BlockSpec((None, D_MODEL, tn), lambda l, j: (l, 0, j)),
            pl.BlockSpec((None, 1, tn), lambda l, j: (l, 0, j)),
        ],
        out_specs=pl.BlockSpec((None, COND_ROWS, tn), lambda l, j: (l, 0, j)),
        out_shape=jax.ShapeDtypeStruct((depth, COND_ROWS, n_out), F32),
        compiler_params=_params(2),
        name="modulation",
    )(cond, mod_w, mod_b.reshape(depth, 1, n_out))
    return out.reshape(depth, COND_ROWS, 6, D_MODEL)


def _even_kernel(xp_ref, xc_ref, xn_ref, mod_ref, ng_ref, win_ref, caw_ref, cab_ref, lng_ref, lnb_ref,
                 cbw_ref, wout_ref, o_ref, apad, cpad, bgs, zs, *, n_chunks):
    c = pl.program_id(1)
    rows = ROW_CHUNK + 2 * CONF_HALO
    shift1, scale1, gate1 = mod_ref[0:1, :], mod_ref[1:2, :], mod_ref[2:3, :]
    xc = xc_ref[...]
    xh = jnp.concatenate([xp_ref[...], xc, xn_ref[...]], axis=0)
    h = _norm_mod(xh, ng_ref[0:1, :], shift1, scale1).astype(BF16)
    ri = lax.broadcasted_iota(jnp.int32, (rows, D_HALF), 0)
    lo = jnp.where(c == 0, CONF_HALO, 0)
    hi = jnp.where(c == n_chunks - 1, CONF_HALO + ROW_CHUNK, rows)
    inside = (ri >= lo) & (ri < hi)
    a = _dot(h, win_ref[:, 0:D_HALF]) * jax.nn.sigmoid(_dot(h, win_ref[:, D_HALF:2 * D_HALF]))
    apad[...] = jnp.where(inside, a, 0.0)
    cx = _dot(h, win_ref[:, 3 * D_HALF:4 * D_HALF]) * _dot(h, win_ref[:, 4 * D_HALF:5 * D_HALF])
    cpad[...] = jnp.where(inside, cx, 0.0)
    bgs[...] = _dot(h[CONF_HALO:CONF_HALO + ROW_CHUNK], win_ref[:, 2 * D_HALF:3 * D_HALF])

    sub = 64
    tile = 8
    for j in range(ROW_CHUNK // sub):
        r0 = j * sub
        groups = []
        for cg in range(D_HALF // LANES):
            cols = slice(cg * LANES, (cg + 1) * LANES)
            acc = None
            for r in range(tile):
                part = None
                for m in range(-(-(CONF_WIDTH + 1) // tile)):
                    o = tile * m + r
                    if 1 <= o <= CONF_WIDTH:
                        term = caw_ref[o - 1:o, cols] * apad[r0 + tile * m:r0 + tile * m + sub + tile, cols]
                        part = term if part is None else part + term
                shifted = part[r:r + sub, :]
                acc = shifted if acc is None else acc + shifted
            groups.append(acc)
        acc = jnp.concatenate(groups, axis=1) + cab_ref[...]
        mu = jnp.mean(acc, axis=-1, keepdims=True)
        dlt = acc - mu
        var = jnp.mean(dlt * dlt, axis=-1, keepdims=True)
        a_out = jax.nn.silu(dlt * lax.rsqrt(var + EPS) * lng_ref[...] + lnb_ref[...])
        zs[r0:r0 + sub, 0:D_HALF] = a_out.astype(BF16)
        base = r0 + CONF_HALO - 1
        sc = (cbw_ref[0:1, :] * cpad[base:base + sub, :]
              + cbw_ref[1:2, :] * cpad[base + 1:base + 1 + sub, :]
              + cbw_ref[2:3, :] * cpad[base + 2:base + 2 + sub, :])
        zs[r0:r0 + sub, D_HALF:D_MODEL] = (bgs[r0:r0 + sub, :] * sc).astype(BF16)

    o = _dot(zs[...], wout_ref[...])
    o_ref[...] = xc + gate1 * _rms(o, ng_ref[1:2, :])


def _even_layer(x, modv, cond_base, cond_stride, ng, win, caw, cab, lng, lnb, cbw, wout):
    b, t, _ = x.shape
    n_chunks = t // ROW_CHUNK
    hpc = ROW_CHUNK // CONF_HALO
    n_halo_blocks = t // CONF_HALO
    rows = ROW_CHUNK + 2 * CONF_HALO
    kern = functools.partial(_even_kernel, n_chunks=n_chunks)
    return pl.pallas_call(
        kern,
        grid=(b, n_chunks),
        in_specs=[
            pl.BlockSpec((None, CONF_HALO, D_MODEL), lambda i, c: (i, jnp.maximum(c * hpc - 1, 0), 0)),
            pl.BlockSpec((None, ROW_CHUNK, D_MODEL), lambda i, c: (i, c, 0)),
            pl.BlockSpec((None, CONF_HALO, D_MODEL),
                         lambda i, c: (i, jnp.minimum((c + 1) * hpc, n_halo_blocks - 1), 0)),
            pl.BlockSpec((None, 6, D_MODEL), lambda i, c: (cond_base + cond_stride * i, 0, 0)),
            _const_spec((4, D_MODEL)),
            _const_spec((D_MODEL, 5 * D_HALF)),
            _const_spec((CONF_WIDTH, D_HALF)),
            _const_spec((1, D_HALF)),
            _const_spec((1, D_HALF)),
            _const_spec((1, D_HALF)),
            _const_spec((3, D_HALF)),
            _const_spec((D_MODEL, D_MODEL)),
        ],
        out_specs=pl.BlockSpec((None, ROW_CHUNK, D_MODEL), lambda i, c: (i, c, 0)),
        out_shape=jax.ShapeDtypeStruct(x.shape, F32),
        scratch_shapes=[
            pltpu.VMEM((rows, D_HALF), F32),
            pltpu.VMEM((rows, D_HALF), F32),
            pltpu.VMEM((ROW_CHUNK, D_HALF), F32),
            pltpu.VMEM((ROW_CHUNK, D_MODEL), BF16),
        ],
        compiler_params=_params(2),
        name="even_mixer",
    )(x, x, x, modv, ng, win, caw, cab, lng, lnb, cbw, wout)


def _mlp_kernel(x_ref, mod_ref, ng_ref, w1_ref, w2_ref, o_ref):
    x = x_ref[...]
    h = _norm_mod(x, ng_ref[2:3, :], mod_ref[3:4, :], mod_ref[4:5, :]).astype(BF16)
    acc = None
    for c in range(D_FF // D_MODEL):
        cols = slice(c * D_MODEL, (c + 1) * D_MODEL)
        hid = jnp.square(jnp.maximum(_dot(h, w1_ref[:, cols]), 0.0)).astype(BF16)
        part = _dot(hid, w2_ref[cols, :])
        acc = part if acc is None else acc + part
    o_ref[...] = x + mod_ref[5:6, :] * _rms(acc, ng_ref[3:4, :])


def _mlp_layer(x, modv, cond_base, cond_stride, ng, w1, w2):
    b, t, _ = x.shape
    tm = 512
    assert (b * t) % tm == 0 and (cond_stride == 0 or t % tm == 0)
    tiles_per_seq = max(t // tm, 1)
    x2 = x.reshape(b * t, D_MODEL)
    out = pl.pallas_call(
        _mlp_kernel,
        grid=(b * t // tm,),
        in_specs=[
            pl.BlockSpec((tm, D_MODEL), lambda i: (i, 0)),
            pl.BlockSpec((None, 6, D_MODEL), lambda i: (cond_base + cond_stride * (i // tiles_per_seq), 0, 0)),
            _const_spec((4, D_MODEL)),
            _const_spec((D_MODEL, D_FF)),
            _const_spec((D_FF, D_MODEL)),
        ],
        out_specs=pl.BlockSpec((tm, D_MODEL), lambda i: (i, 0)),
        out_shape=jax.ShapeDtypeStruct(x2.shape, F32),
        compiler_params=_params(1),
        name="mlp",
    )(x2, modv, ng, w1, w2)
    return out.reshape(b, t, D_MODEL)


def _log_sigmoid(x):
    return jnp.minimum(x, 0.0) - jnp.log(1.0 + jnp.exp(-jnp.abs(x)))


def _rot_half(x, first_half):
    return jnp.where(first_half, pltpu.roll(x, 96, axis=1), pltpu.roll(x, 32, axis=1))


def _values_and_ones(v):
    vt = jnp.transpose(v)
    row = lax.broadcasted_iota(jnp.int32, vt.shape, 0)
    return jnp.where(row < HEAD_DIM, vt, 1.0)


def _odd_kernel(*refs, t, lc, latent):
    if latent:
        (x_ref, mod_ref, ng_ref, win_ref, wg_ref, gb_ref, sink_ref, hn_ref, wout_ref,
         cos_ref, sin_ref, kc_ref, vc_ref, cin_ref, nin_ref, min_ref,
         o_ref,
         qa_s, qb_s, kk_s, vvt_s, qma_s, qmb_s, qm_s, km_s, vmt_s, vmtf_s, om_s, g_s, z_s, hft_s, hbt_s,
         sr_s, st_s, qc_s, c_s, n_s, m_s, kc_s, vct_s) = refs
    else:
        (x_ref, mod_ref, ng_ref, win_ref, wg_ref, gb_ref, sink_ref, hn_ref, wout_ref,
         o_ref, ko_ref, vo_ref, cf_ref, cb_ref, no_ref, mo_ref,
         qa_s, qb_s, kk_s, vvt_s, qma_s, qmb_s, qm_s, km_s, vmt_s, vmtf_s, om_s, g_s, z_s, hft_s, hbt_s,
         sr_s, st_s, qc_s, c_s, n_s, m_s) = refs

    n_blocks = t // CHUNK
    pad = CHUNK if latent else 0
    shift1, scale1, gate1 = mod_ref[0:1, :], mod_ref[1:2, :], mod_ref[2:3, :]
    lane = lax.broadcasted_iota(jnp.int32, (1, LANES), 1)
    left = lane < HEAD_DIM
    first_half = (lane % HEAD_DIM) < (HEAD_DIM // 2)
    ti = lax.broadcasted_iota(jnp.int32, (CHUNK, CHUNK), 0)
    si = lax.broadcasted_iota(jnp.int32, (CHUNK, CHUNK), 1)
    top = ti < HEAD_DIM
    same_head = top == (si < HEAD_DIM)

    if latent:
        for kv in range(N_KV):
            kk_s[kv, 0:CHUNK, :] = jnp.zeros((CHUNK, LANES), BF16)
            kk_s[kv, CHUNK + t:2 * CHUNK + t, :] = jnp.zeros((CHUNK, LANES), BF16)
            vvt_s[kv, :, 0:CHUNK] = jnp.zeros((LANES, CHUNK), BF16)
            vvt_s[kv, :, CHUNK + t:2 * CHUNK + t] = jnp.zeros((LANES, CHUNK), BF16)
            kc_s[kv] = kc_ref[kv].astype(BF16)
            vct_s[kv] = _values_and_ones(vc_ref[kv]).astype(BF16)

    def project(rc, carry):
        r0 = pl.multiple_of(rc * ROW_CHUNK, ROW_CHUNK)
        rows = pl.ds(r0, ROW_CHUNK)
        krows = pl.ds(r0 + pad, ROW_CHUNK)
        h = _norm_mod(x_ref[rows, :], ng_ref[0:1, :], shift1, scale1).astype(BF16)
        q = _dot(h, win_ref[:, 0:D_HALF])
        kv2 = _dot(h, win_ref[:, D_HALF:D_HALF + 2 * LANES])
        ka, va = kv2[:, 0:LANES], kv2[:, LANES:2 * LANES]
        if latent:
            cs, sn = cos_ref[rows, :], sin_ref[rows, :]
            ka = ka * cs + _rot_half(ka, first_half) * sn
        kr = pltpu.roll(ka, HEAD_DIM, axis=1)
        vr = pltpu.roll(va, HEAD_DIM, axis=1)
        if not latent:
            ko_ref[0, 0, rows, :] = ka[:, 0:HEAD_DIM]
            ko_ref[0, 1, rows, :] = kr[:, 0:HEAD_DIM]
            vo_ref[0, 0, rows, :] = va[:, 0:HEAD_DIM]
            vo_ref[0, 1, rows, :] = vr[:, 0:HEAD_DIM]
        kk_s[0, krows, :] = jnp.where(left, ka, kr).astype(BF16)
        kk_s[1, krows, :] = jnp.where(left, kr, ka).astype(BF16)
        vvt_s[0, :, krows] = _values_and_ones(va).astype(BF16)
        vvt_s[1, :, krows] = _values_and_ones(vr).astype(BF16)
        base = D_HALF + 2 * LANES
        qm = _dot(h, win_ref[:, base:base + D_HALF])
        qm_s[rows, :] = qm.astype(BF16)
        vm = _dot(h, win_ref[:, base + 2 * D_HALF:base + 3 * D_HALF])
        for p in range(N_PAIRS):
            cols = slice(p * LANES, (p + 1) * LANES)
            qp = q[:, cols]
            if latent:
                qp = qp * cs + _rot_half(qp, first_half) * sn
            qp = qp * ATT_SCALE
            qa_s[rows, cols] = jnp.where(left, qp, 0.0).astype(BF16)
            qb_s[rows, cols] = jnp.where(left, 0.0, qp).astype(BF16)
            qma_s[rows, cols] = jnp.where(left, qm[:, cols], 0.0).astype(BF16)
            qmb_s[rows, cols] = jnp.where(left, 0.0, qm[:, cols]).astype(BF16)
            vt = jnp.transpose(vm[:, cols])
            vmtf_s[cols, rows] = vt
            vmt_s[cols, rows] = vt.astype(BF16)
        km = _dot(h, win_ref[:, base + D_HALF:base + 2 * D_HALF]) * (HEAD_DIM ** -0.5)
        km_s[rows, :] = km.astype(BF16)
        om_s[rows, :] = _dot(h, win_ref[:, base + 3 * D_HALF:base + 4 * D_HALF])
        g_s[rows, :] = _dot(h, wg_ref[...]) + gb_ref[...]
        return carry

    lax.fori_loop(0, t // ROW_CHUNK, project, 0)

    kj = lax.broadcasted_iota(jnp.int32, (3 * CHUNK, 2 * CHUNK), 0)
    qi = lax.broadcasted_iota(jnp.int32, (3 * CHUNK, 2 * CHUNK), 1) % CHUNK
    band_ok = jnp.abs(kj - CHUNK - qi) <= WINDOW
    head_a = lax.broadcasted_iota(jnp.int32, (1, 2 * CHUNK), 1) < CHUNK

    n_keys = lc + 3 * CHUNK if latent else lc
    group = 2 if latent else N_PAIRS

    def attention(s_s, p_s):
        def attend(i, carry):
            r0 = pl.multiple_of(i * CHUNK, CHUNK)
            rows = pl.ds(r0, CHUNK)
            if latent:
                key_pos = kj + (i - 1) * CHUNK
                mask = band_ok & (key_pos >= 0) & (key_pos < t)
                win = pl.ds(r0, 3 * CHUNK)
            for g0 in range(0, N_PAIRS, group):
                for gi in range(group):
                    p = g0 + gi
                    kv = p // (N_PAIRS // N_KV)
                    cols = slice(p * LANES, (p + 1) * LANES)
                    q2 = jnp.concatenate([qa_s[rows, cols], qb_s[rows, cols]], axis=0)
                    if latent:
                        s_s[gi, 0:lc, :] = _dot_nt(kc_s[kv], q2)
                        s_s[gi, lc:n_keys, :] = jnp.where(mask, _dot_nt(kk_s[kv, win, :], q2), -jnp.inf)
                    else:
                        s_s[gi] = _dot_nt(kk_s[kv], q2)
                maxes = []
                for gi in range(group):
                    p = g0 + gi
                    sink = jnp.where(head_a, sink_ref[2 * p:2 * p + 1, 0:1], sink_ref[2 * p + 1:2 * p + 2, 0:1])
                    mx = jnp.maximum(jnp.max(s_s[gi], axis=0, keepdims=True), sink)
                    p_s[gi] = jnp.exp(s_s[gi] - mx).astype(BF16)
                    maxes.append((sink, mx))
                for gi in range(group):
                    p = g0 + gi
                    kv = p // (N_PAIRS // N_KV)
                    cols = slice(p * LANES, (p + 1) * LANES)
                    if latent:
                        num = (_dot(vct_s[kv], p_s[gi, 0:lc, :]) + _dot(vvt_s[kv, :, win], p_s[gi, lc:n_keys, :]))
                    else:
                        num = _dot(vvt_s[kv], p_s[gi])
                    sink, mx = maxes[gi]
                    den = num[HEAD_DIM:HEAD_DIM + 1, :] + jnp.exp(sink - mx)
                    out = num[0:HEAD_DIM, :] * (1.0 / den)
                    pair = jnp.concatenate([out[:, 0:CHUNK], out[:, CHUNK:2 * CHUNK]], axis=0)
                    z_s[rows, cols] = jnp.transpose(pair).astype(BF16)
            return carry

        lax.fori_loop(0, n_blocks, attend, 0)

    pl.run_scoped(attention, pltpu.VMEM((group, n_keys, 2 * CHUNK), F32),
                  pltpu.VMEM((group, n_keys, 2 * CHUNK), BF16))

    if latent:
        c_s[...] = cin_ref[...]
        n_s[...] = nin_ref[...]
        m_s[...] = min_ref[...]
    else:
        c_s[...] = jnp.zeros(c_s.shape, F32)
        n_s[...] = jnp.zeros(n_s.shape, F32)
        m_s[...] = jnp.zeros(m_s.shape, F32)

    see = (ti <= si, ti >= si)
    tri = tuple(m.astype(F32).astype(BF16) for m in see)
    last = (CHUNK - 1, 0)

    def mlstm(i, carry):
        offs = (pl.multiple_of(i * CHUNK, CHUNK), pl.multiple_of((n_blocks - 1 - i) * CHUNK, CHUNK))
        half = (slice(0, LANES), slice(LANES, 2 * LANES))
        a_rows, b_rows = [], []
        for d in range(2):
            gt = jnp.transpose(g_s[pl.ds(offs[d], CHUNK), :])
            lf = _log_sigmoid(gt[2 * N_HEADS:4 * N_HEADS, :])
            bcum = _dot_exact_lhs(lf, tri[d])[8 * d:8 * d + 8, :]
            a_rows.append(gt[8 * d:8 * d + 8, :] - bcum)
            b_rows.append(bcum)
        q_ns = []
        for d in range(2):
            rows = pl.ds(offs[d], CHUNK)
            for p in range(N_PAIRS):
                u = d * N_PAIRS + p
                cols = slice(p * LANES, (p + 1) * LANES)
                q2 = jnp.concatenate([qma_s[rows, cols], qmb_s[rows, cols]], axis=0)
                sr_s[u] = _dot_nt(km_s[rows, cols], q2)
                qc_s[u] = _dot_nt(c_s[u].astype(BF16), qm_s[rows, cols])
                n8 = jnp.broadcast_to(n_s[u:u + 1, :], (8, LANES)).astype(BF16)
                q_ns.append(_dot_nt(n8, q2)[0:1, :])
        stats = {}
        for d in range(2):
            for hd in range(N_HEADS):
                u, j = d * N_PAIRS + hd // 2, hd % 2
                mrow = d * N_HEADS + hd
                a_row = a_rows[d][hd:hd + 1, :]
                b_row = b_rows[d][hd:hd + 1, :]
                m_prev = m_s[mrow:mrow + 1, 0:1]
                a_col = jnp.transpose(jnp.broadcast_to(a_row, (CHUNK, CHUNK)))
                z_t = jnp.where(see[d], a_col, -jnp.inf)
                m_run = jnp.maximum(jnp.max(z_t, axis=0, keepdims=True), m_prev)
                s_t = sr_s[u, :, half[j]] * jnp.exp(z_t - m_run)
                st_s[u, :, half[j]] = s_t.astype(BF16)
                w_int = jnp.exp(m_prev - m_run)
                den = jnp.sum(s_t, axis=0, keepdims=True) + w_int * q_ns[u][:, half[j]]
                inv = 1.0 / jnp.maximum(jnp.abs(den), jnp.exp(-(b_row + m_run)))
                m_last = m_run[:, last[d]:last[d] + 1]
                stats[(u, j)] = (w_int, inv, jnp.exp(a_row - m_last), jnp.exp(m_prev - m_last))
                m_s[mrow:mrow + 1, :] = jnp.broadcast_to(b_row[:, last[d]:last[d] + 1] + m_last, (1, LANES))
        for d in range(2):
            rows = pl.ds(offs[d], CHUNK)
            ht_s = (hft_s, hbt_s)[d]
            for p in range(N_PAIRS):
                u = d * N_PAIRS + p
                cols = slice(p * LANES, (p + 1) * LANES)
                (w_a, inv_a, e_a, dec_a), (w_b, inv_b, e_b, dec_b) = stats[(u, 0)], stats[(u, 1)]
                kp = km_s[rows, cols]
                num2 = _dot(vmt_s[cols, rows], st_s[u])
                num = jnp.where(top, num2[:, half[0]], num2[:, half[1]])
                ht_s[cols, rows] = (num + jnp.where(top, w_a, w_b) * qc_s[u]) * jnp.where(top, inv_a, inv_b)
                vt_e = (vmtf_s[cols, rows] * jnp.where(top, e_a, e_b)).astype(BF16)
                c_s[u] = jnp.where(top, dec_a, dec_b) * c_s[u] + jnp.where(same_head, _dot(vt_e, kp), 0.0)
                e2 = jnp.concatenate([e_a, e_b, jnp.zeros((6, CHUNK), F32)], axis=0).astype(BF16)
                n_k = _dot(e2, kp)
                n_s[u:u + 1, :] = (jnp.where(left, dec_a, dec_b) * n_s[u:u + 1, :]
                                   + jnp.where(left, n_k[0:1, :], n_k[1:2, :]))
        return carry

    lax.fori_loop(0, n_blocks, mlstm, 0)

    if not latent:
        for d, c_ref in enumerate((cf_ref, cb_ref)):
            for p in range(N_PAIRS):
                c_pair = jnp.transpose(c_s[d * N_PAIRS + p])
                c_ref[0, 2 * p] = c_pair[0:HEAD_DIM, 0:HEAD_DIM]
                c_ref[0, 2 * p + 1] = c_pair[HEAD_DIM:LANES, HEAD_DIM:LANES]
        no_ref[...] = n_s[...]
        mo_ref[...] = m_s[...]

    top_w = lax.broadcasted_iota(jnp.int32, (LANES, ROW_CHUNK), 0) < HEAD_DIM

    def finish(rc, carry):
        r0 = pl.multiple_of(rc * ROW_CHUNK, ROW_CHUNK)
        rows = pl.ds(r0, ROW_CHUNK)
        for p in range(N_PAIRS):
            cols = slice(p * LANES, (p + 1) * LANES)
            hm = hft_s[cols, rows] + hbt_s[cols, rows]
            sq = hm * hm
            ms_a = jnp.sum(sq[0:HEAD_DIM], axis=0, keepdims=True)
            ms_b = jnp.sum(sq[HEAD_DIM:LANES], axis=0, keepdims=True)
            ms = jnp.where(top_w, ms_a, ms_b) * (1.0 / HEAD_DIM)
            y = jnp.transpose(hm * lax.rsqrt(ms + EPS)) * hn_ref[:, cols] * jax.nn.sigmoid(om_s[rows, cols])
            z_s[rows, D_HALF + p * LANES:D_HALF + (p + 1) * LANES] = y.astype(BF16)
        o = _dot(z_s[rows, :], wout_ref[...])
        o_ref[rows, :] = x_ref[rows, :] + gate1 * _rms(o, ng_ref[1:2, :])
        return carry

    lax.fori_loop(0, t // ROW_CHUNK, finish, 0)


def _odd_scratch(t, lc, latent):
    pad = 2 * CHUNK if latent else 0
    shapes = [
        pltpu.VMEM((t, D_HALF), BF16),
        pltpu.VMEM((t, D_HALF), BF16),
        pltpu.VMEM((N_KV, t + pad, LANES), BF16),
        pltpu.VMEM((N_KV, LANES, t + pad), BF16),
        pltpu.VMEM((t, D_HALF), BF16),
        pltpu.VMEM((t, D_HALF), BF16),
        pltpu.VMEM((t, D_HALF), BF16),
        pltpu.VMEM((t, D_HALF), BF16),
        pltpu.VMEM((D_HALF, t), BF16),
        pltpu.VMEM((D_HALF, t), F32),
        pltpu.VMEM((t, D_HALF), F32),
        pltpu.VMEM((t, LANES), F32),
        pltpu.VMEM((t, D_MODEL), BF16),
        pltpu.VMEM((D_HALF, t), F32),
        pltpu.VMEM((D_HALF, t), F32),
        pltpu.VMEM((2 * N_PAIRS, CHUNK, 2 * LANES), F32),
        pltpu.VMEM((2 * N_PAIRS, CHUNK, 2 * LANES), BF16),
        pltpu.VMEM((2 * N_PAIRS, LANES, CHUNK), F32),
        pltpu.VMEM((2 * N_PAIRS, LANES, LANES), F32),
        pltpu.VMEM((2 * N_PAIRS, LANES), F32),
        pltpu.VMEM((2 * N_HEADS, LANES), F32),
    ]
    if latent:
        shapes += [pltpu.VMEM((N_KV, lc, LANES), BF16), pltpu.VMEM((N_KV, LANES, lc), BF16)]
    return shapes


def _odd_common_specs(t, cond_base, cond_stride):
    d_main = D_HALF + 2 * LANES + 4 * D_HALF
    return [
        pl.BlockSpec((None, t, D_MODEL), lambda i: (i, 0, 0)),
        pl.BlockSpec((None, 6, D_MODEL), lambda i: (cond_base + cond_stride * i, 0, 0)),
        _const_spec((4, D_MODEL)),
        _const_spec((D_MODEL, d_main)),
        _const_spec((D_MODEL, LANES)),
        _const_spec((1, LANES)),
        _const_spec((N_HEADS, LANES)),
        _const_spec((1, D_HALF)),
        _const_spec((D_MODEL, D_MODEL)),
    ]


def _per_seq(shape):
    return pl.BlockSpec((None,) + shape, lambda i: (i,) + (0,) * len(shape))


def _odd_context(x, modv, ng, w_main, w_gate, gate_bias, sink_b, hnorm, wout):
    b, t, _ = x.shape
    kern = functools.partial(_odd_kernel, t=t, lc=t, latent=False)
    return pl.pallas_call(
        kern,
        grid=(b,),
        in_specs=_odd_common_specs(t, 0, 0),
        out_specs=[_per_seq((t, D_MODEL)),
                   _per_seq((1, N_KV, t, HEAD_DIM)), _per_seq((1, N_KV, t, HEAD_DIM)),
                   _per_seq((1, N_HEADS, HEAD_DIM, HEAD_DIM)), _per_seq((1, N_HEADS, HEAD_DIM, HEAD_DIM)),
                   _per_seq((2 * N_PAIRS, LANES)), _per_seq((2 * N_HEADS, LANES))],
        out_shape=[jax.ShapeDtypeStruct((b, t, D_MODEL), F32),
                   jax.ShapeDtypeStruct((b, 1, N_KV, t, HEAD_DIM), F32),
                   jax.ShapeDtypeStruct((b, 1, N_KV, t, HEAD_DIM), F32),
                   jax.ShapeDtypeStruct((b, 1, N_HEADS, HEAD_DIM, HEAD_DIM), F32),
                   jax.ShapeDtypeStruct((b, 1, N_HEADS, HEAD_DIM, HEAD_DIM), F32),
                   jax.ShapeDtypeStruct((b, 2 * N_PAIRS, LANES), F32),
                   jax.ShapeDtypeStruct((b, 2 * N_HEADS, LANES), F32)],
        scratch_shapes=_odd_scratch(t, t, False),
        compiler_params=_params(1),
        name="odd_mixer_context",
    )(x, modv, ng, w_main, w_gate, gate_bias, sink_b, hnorm, wout)


def _odd_latent(x, modv, ng, w_main, w_gate, gate_bias, sink_b, hnorm, wout, cos_t, sin_t, kc, vc,
                c_in, n_in, m_in):
    b, t, _ = x.shape
    lc = kc.shape[2]
    kern = functools.partial(_odd_kernel, t=t, lc=lc, latent=True)
    return pl.pallas_call(
        kern,
        grid=(b,),
        in_specs=_odd_common_specs(t, 1, 1) + [
            _const_spec((t, LANES)), _const_spec((t, LANES)),
            _per_seq((N_KV, lc, LANES)), _per_seq((N_KV, lc, LANES)),
            _per_seq((2 * N_PAIRS, LANES, LANES)), _per_seq((2 * N_PAIRS, LANES)), _per_seq((2 * N_HEADS, LANES)),
        ],
        out_specs=_per_seq((t, D_MODEL)),
        out_shape=jax.ShapeDtypeStruct((b, t, D_MODEL), F32),
        scratch_shapes=_odd_scratch(t, lc, True),
        compiler_params=_params(1),
        name="odd_mixer_latent",
    )(x, modv, ng, w_main, w_gate, gate_bias, sink_b, hnorm, wout, cos_t, sin_t, kc, vc, c_in, n_in, m_in)


def _rope_tables(t):
    rows = t // GRID_W
    row = jnp.broadcast_to(jnp.arange(rows)[:, None], (rows, GRID_W)).reshape(t).astype(F32)
    col = jnp.broadcast_to(jnp.arange(GRID_W)[None, :], (rows, GRID_W)).reshape(t).astype(F32)
    n_freq = HEAD_DIM // 4
    inv_freq = ROPE_BASE ** (-jnp.arange(n_freq, dtype=F32) / n_freq)
    ang = jnp.concatenate([row[:, None] * inv_freq, col[:, None] * inv_freq], axis=-1)
    cos, sin = jnp.cos(ang), jnp.sin(ang)
    cos_l = jnp.tile(cos, (1, LANES // cos.shape[1]))
    sin_l = jnp.tile(jnp.concatenate([-sin, sin], axis=-1), (1, LANES // HEAD_DIM))
    return cos_l, sin_l


def _pair_blockdiag(c):
    b = c.shape[0]
    c = c.reshape(b, N_PAIRS, 2, HEAD_DIM, HEAD_DIM)
    z = jnp.zeros_like(c[:, :, 0])
    top = jnp.concatenate([c[:, :, 0], z], axis=-1)
    bot = jnp.concatenate([z, c[:, :, 1]], axis=-1)
    return jnp.concatenate([top, bot], axis=-2)


def _lane_bcast(v):
    return jnp.broadcast_to(v[..., None], v.shape + (LANES,))


def kernel(x_prompt, x_sample, c, cache_k, cache_v, state_c_fwd, state_n_fwd, state_m_fwd, state_c_bwd, state_n_bwd, state_m_bwd, c_ctx, mod_w, mod_b, norm_g, mlp_w1, mlp_w2, even_in_w, conv_a_w, conv_a_b, ln_a_g, ln_a_b, conv_b_w, even_out_w, odd_in_w, attn_sink, gate_b, hnorm_g, odd_out_w):
    n_dec = x_sample.shape[0]
    n_ctx = x_prompt.shape[0]
    cond = jnp.concatenate([c_ctx[None, :], c, jnp.zeros((COND_ROWS - 1 - n_dec, D_MODEL), F32)], axis=0)
    modv = _modulation(cond, mod_w, mod_b)

    yp, ys = x_prompt, x_sample

    ev = (even_in_w[0].astype(BF16), conv_a_w[0], conv_a_b[0][None, :], ln_a_g[0][None, :], ln_a_b[0][None, :],
          conv_b_w[0], even_out_w[0].astype(BF16))
    yp = _even_layer(yp, modv[0], 0, 0, norm_g[0], *ev)
    ys = _even_layer(ys, modv[0], 1, 1, norm_g[0], *ev)
    w1, w2 = mlp_w1[0].astype(BF16), mlp_w2[0].astype(BF16)
    yp = _mlp_layer(yp, modv[0], 0, 0, norm_g[0], w1, w2)
    ys = _mlp_layer(ys, modv[0], 1, 1, norm_g[0], w1, w2)

    w_in = odd_in_w[0]
    d_main = D_HALF + 2 * LANES + 4 * D_HALF
    w_main = w_in[:, :d_main].astype(BF16)
    order = jnp.array([0, 2, 1, 3])
    wg = w_in[:, d_main:].reshape(D_MODEL, 4, N_HEADS)[:, order, :].reshape(D_MODEL, 4 * N_HEADS)
    w_gate = jnp.pad(wg, ((0, 0), (0, LANES - 4 * N_HEADS))).astype(BF16)
    gate_bias = jnp.pad(gate_b[0][order, :].reshape(1, 4 * N_HEADS), ((0, 0), (0, LANES - 4 * N_HEADS)))
    sink_b = _lane_bcast(attn_sink[0])
    hnorm = hnorm_g[0][None, :]
    wout = odd_out_w[0].astype(BF16)
    odd = (norm_g[1], w_main, w_gate, gate_bias, sink_b, hnorm, wout)

    op, new_k, new_v, c_f, c_b, n_new, m_new = _odd_context(yp, modv[1], *odd)

    t_dec = x_sample.shape[1]
    cos_t, sin_t = _rope_tables(t_dec)
    kc = jnp.concatenate([cache_k[:, 0], cache_k[:, 0]], axis=-1)
    vc = jnp.concatenate([cache_v[:, 0], cache_v[:, 0]], axis=-1)
    c_in = jnp.concatenate([_pair_blockdiag(jnp.swapaxes(state_c_fwd[:, 0], -1, -2)),
                            _pair_blockdiag(jnp.swapaxes(state_c_bwd[:, 0], -1, -2))], axis=1)
    n_in = jnp.concatenate([state_n_fwd[:, 0].reshape(n_dec, N_PAIRS, LANES),
                            state_n_bwd[:, 0].reshape(n_dec, N_PAIRS, LANES)], axis=1)
    m_in = _lane_bcast(jnp.concatenate([state_m_fwd[:, 0], state_m_bwd[:, 0]], axis=1))
    os_ = _odd_latent(ys, modv[1], *odd, cos_t, sin_t, kc, vc, c_in, n_in, m_in)

    w1, w2 = mlp_w1[1].astype(BF16), mlp_w2[1].astype(BF16)
    yp = _mlp_layer(op, modv[1], 0, 0, norm_g[1], w1, w2)
    ys = _mlp_layer(os_, modv[1], 1, 1, norm_g[1], w1, w2)

    n_f = n_new[:, :N_PAIRS].reshape(n_ctx, N_HEADS, HEAD_DIM)[:, None]
    n_b = n_new[:, N_PAIRS:].reshape(n_ctx, N_HEADS, HEAD_DIM)[:, None]
    m_f = m_new[:, :N_HEADS, 0][:, None]
    m_b = m_new[:, N_HEADS:, 0][:, None]
    return (yp, ys, new_k, new_v, c_f, n_f, m_f, c_b, n_b, m_b)
```

```python
import functools

import jax
import jax.numpy as jnp
from jax import lax
from jax.experimental import pallas as pl
from jax.experimental.pallas import tpu as pltpu

F32 = jnp.float32
BF16 = jnp.bfloat16

D_MODEL = 1024
D_FF = 4 * D_MODEL
EPS = 1e-6
D_HALF = D_MODEL // 2
CONF_WIDTH = 31
CONF_HALO = 16
HEAD_DIM = 64
N_HEADS = 8
N_PAIRS = N_HEADS // 2
N_KV = 2
LANES = 128
CHUNK = 128
WINDOW = 128
GRID_W = 64
ROPE_BASE = 10000.0
ATT_SCALE = HEAD_DIM ** -0.5
D_IN_ODD = D_HALF + 2 * N_KV * HEAD_DIM + 4 * D_HALF + 4 * N_HEADS
ROW_CHUNK = 256
COND_ROWS = 8
VMEM_LIMIT = 56 * 1024 * 1024


def _dot(a, b):
    return jnp.dot(a, b, preferred_element_type=F32)


def _dot_nt(a, b):
    return lax.dot_general(a, b, (((1,), (1,)), ((), ())), preferred_element_type=F32)


def _split3(x):
    hi = x.astype(BF16)
    r1 = x - hi.astype(F32)
    mid = r1.astype(BF16)
    lo = (r1 - mid.astype(F32)).astype(BF16)
    return hi, mid, lo


def _dot_exact_lhs(x, b01):
    hi, mid, lo = _split3(x)
    return _dot(hi, b01) + _dot(mid, b01) + _dot(lo, b01)


def _rms(x, g):
    return x * lax.rsqrt(jnp.mean(x * x, axis=-1, keepdims=True) + EPS) * g


def _norm_mod(x, g, shift, scale):
    return _rms(x, g) * (1.0 + scale) + shift


def _params(n_grid):
    return pltpu.CompilerParams(dimension_semantics=("arbitrary",) * n_grid, vmem_limit_bytes=VMEM_LIMIT)


def _const_spec(shape):
    zeros = (0,) * len(shape)
    return pl.BlockSpec(shape, lambda *_: zeros, pipeline_mode=pl.Buffered(1))


def _layer_spec(shape, layer):
    index = (layer,) + (0,) * len(shape)
    return pl.BlockSpec((None,) + shape, lambda *_: index, pipeline_mode=pl.Buffered(1))


def _mod_spec(layer, cond_of):
    return pl.BlockSpec((None, None, 6, D_MODEL), lambda *idx: (layer, cond_of(*idx), 0, 0))


def _mod_kernel(cond_ref, w_ref, b_ref, o_ref):
    s = jax.nn.silu(cond_ref[...]).astype(BF16)
    o_ref[...] = _dot(s, w_ref[...].astype(BF16)) + b_ref[...]


def _modulation(cond, mod_w, mod_b):
    depth = mod_w.shape[0]
    n_out = mod_w.shape[2]
    tn = D_MODEL
    out = pl.pallas_call(
        _mod_kernel,
        grid=(depth, n_out // tn),
        in_specs=[
            pl.BlockSpec((COND_ROWS, D_MODEL), lambda l, j: (0, 0)),
            pl.BlockSpec((None, D_MODEL, tn), lambda l, j: (l, 0, j)),
            pl.BlockSpec((None, 1, tn), lambda l, j: (l, 0, j)),
        ],
        out_specs=pl.BlockSpec((None, COND_ROWS, tn), lambda l, j: (l, 0, j)),
        out_shape=jax.ShapeDtypeStruct((depth, COND_ROWS, n_out), F32),
        compiler_params=_params(2),
        name="modulation",
    )(cond, mod_w, mod_b.reshape(depth, 1, n_out))
    return out.reshape(depth, COND_ROWS, 6, D_MODEL)


def _even_kernel(xp_ref, xc_ref, xn_ref, mod_ref, ng_ref, win_ref, caw_ref, cab_ref, lng_ref, lnb_ref,
                 cbw_ref, wout_ref, o_ref, apad, cpad, bgs, zs, *, n_chunks):
    c = pl.program_id(1)
    rows = ROW_CHUNK + 2 * CONF_HALO
    shift1, scale1, gate1 = mod_ref[0:1, :], mod_ref[1:2, :], mod_ref[2:3, :]
    xc = xc_ref[...]
    xh = jnp.concatenate([xp_ref[...], xc, xn_ref[...]], axis=0)
    h = _norm_mod(xh, ng_ref[0:1, :], shift1, scale1).astype(BF16)
    ri = lax.broadcasted_iota(jnp.int32, (rows, D_HALF), 0)
    lo = jnp.where(c == 0, CONF_HALO, 0)
    hi = jnp.where(c == n_chunks - 1, CONF_HALO + ROW_CHUNK, rows)
    inside = (ri >= lo) & (ri < hi)
    a = _dot(h, win_ref[:, 0:D_HALF]) * jax.nn.sigmoid(_dot(h, win_ref[:, D_HALF:2 * D_HALF]))
    apad[...] = jnp.where(inside, a, 0.0)
    cx = _dot(h, win_ref[:, 3 * D_HALF:4 * D_HALF]) * _dot(h, win_ref[:, 4 * D_HALF:5 * D_HALF])
    cpad[...] = jnp.where(inside, cx, 0.0)
    bgs[...] = _dot(h[CONF_HALO:CONF_HALO + ROW_CHUNK], win_ref[:, 2 * D_HALF:3 * D_HALF])

    sub = 64
    tile = 8
    for j in range(ROW_CHUNK // sub):
        r0 = j * sub
        groups = []
        for cg in range(D_HALF // LANES):
            cols = slice(cg * LANES, (cg + 1) * LANES)
            acc = None
            for r in range(tile):
                part = None
                for m in range(-(-(CONF_WIDTH + 1) // tile)):
                    o = tile * m + r
                    if 1 <= o <= CONF_WIDTH:
                        term = caw_ref[o - 1:o, cols] * apad[r0 + tile * m:r0 + tile * m + sub + tile, cols]
                        part = term if part is None else part + term
                shifted = part[r:r + sub, :]
                acc = shifted if acc is None else acc + shifted
            groups.append(acc)
        acc = jnp.concatenate(groups, axis=1) + cab_ref[...]
        mu = jnp.mean(acc, axis=-1, keepdims=True)
        dlt = acc - mu
        var = jnp.mean(dlt * dlt, axis=-1, keepdims=True)
        a_out = jax.nn.silu(dlt * lax.rsqrt(var + EPS) * lng_ref[...] + lnb_ref[...])
        zs[r0:r0 + sub, 0:D_HALF] = a_out.astype(BF16)
        base = r0 + CONF_HALO - 1
        sc = (cbw_ref[0:1, :] * cpad[base:base + sub, :]
              + cbw_ref[1:2, :] * cpad[base + 1:base + 1 + sub, :]
              + cbw_ref[2:3, :] * cpad[base + 2:base + 2 + sub, :])
        zs[r0:r0 + sub, D_HALF:D_MODEL] = (bgs[r0:r0 + sub, :] * sc).astype(BF16)

    o = _dot(zs[...], wout_ref[...])
    o_ref[...] = xc + gate1 * _rms(o, ng_ref[1:2, :])


def _even_layer(x, modv, layer, cond_base, cond_stride, ng, win, caw, cab, lng, lnb, cbw, wout):
    j = layer // 2
    b, t, _ = x.shape
    n_chunks = t // ROW_CHUNK
    hpc = ROW_CHUNK // CONF_HALO
    n_halo_blocks = t // CONF_HALO
    rows = ROW_CHUNK + 2 * CONF_HALO
    kern = functools.partial(_even_kernel, n_chunks=n_chunks)
    return pl.pallas_call(
        kern,
        grid=(b, n_chunks),
        in_specs=[
            pl.BlockSpec((None, CONF_HALO, D_MODEL), lambda i, c: (i, jnp.maximum(c * hpc - 1, 0), 0)),
            pl.BlockSpec((None, ROW_CHUNK, D_MODEL), lambda i, c: (i, c, 0)),
            pl.BlockSpec((None, CONF_HALO, D_MODEL),
                         lambda i, c: (i, jnp.minimum((c + 1) * hpc, n_halo_blocks - 1), 0)),
            _mod_spec(layer, lambda i, c: cond_base + cond_stride * i),
            _layer_spec((4, D_MODEL), layer),
            _layer_spec((D_MODEL, 5 * D_HALF), j),
            _layer_spec((CONF_WIDTH, D_HALF), j),
            _layer_spec((1, D_HALF), j),
            _layer_spec((1, D_HALF), j),
            _layer_spec((1, D_HALF), j),
            _layer_spec((3, D_HALF), j),
            _layer_spec((D_MODEL, D_MODEL), j),
        ],
        out_specs=pl.BlockSpec((None, ROW_CHUNK, D_MODEL), lambda i, c: (i, c, 0)),
        out_shape=jax.ShapeDtypeStruct(x.shape, F32),
        scratch_shapes=[
            pltpu.VMEM((rows, D_HALF), F32),
            pltpu.VMEM((rows, D_HALF), F32),
            pltpu.VMEM((ROW_CHUNK, D_HALF), F32),
            pltpu.VMEM((ROW_CHUNK, D_MODEL), BF16),
        ],
        compiler_params=_params(2),
        name="even_mixer",
    )(x, x, x, modv, ng, win, caw, cab, lng, lnb, cbw, wout)


def _mlp_kernel(x_ref, mod_ref, ng_ref, w1_ref, w2_ref, o_ref):
    x = x_ref[...]
    h = _norm_mod(x, ng_ref[2:3, :], mod_ref[3:4, :], mod_ref[4:5, :]).astype(BF16)
    acc = None
    for c in range(D_FF // D_MODEL):
        cols = slice(c * D_MODEL, (c + 1) * D_MODEL)
        hid = jnp.square(jnp.maximum(_dot(h, w1_ref[:, cols]), 0.0)).astype(BF16)
        part = _dot(hid, w2_ref[cols, :])
        acc = part if acc is None else acc + part
    o_ref[...] = x + mod_ref[5:6, :] * _rms(acc, ng_ref[3:4, :])


def _mlp_layer(x, modv, layer, cond_base, cond_stride, ng, w1, w2):
    b, t, _ = x.shape
    tm = 512
    assert (b * t) % tm == 0 and (cond_stride == 0 or t % tm == 0)
    tiles_per_seq = max(t // tm, 1)
    x2 = x.reshape(b * t, D_MODEL)
    out = pl.pallas_call(
        _mlp_kernel,
        grid=(b * t // tm,),
        in_specs=[
            pl.BlockSpec((tm, D_MODEL), lambda i: (i, 0)),
            _mod_spec(layer, lambda i: cond_base + cond_stride * (i // tiles_per_seq)),
            _layer_spec((4, D_MODEL), layer),
            _layer_spec((D_MODEL, D_FF), layer),
            _layer_spec((D_FF, D_MODEL), layer),
        ],
        out_specs=pl.BlockSpec((tm, D_MODEL), lambda i: (i, 0)),
        out_shape=jax.ShapeDtypeStruct(x2.shape, F32),
        compiler_params=_params(1),
        name="mlp",
    )(x2, modv, ng, w1, w2)
    return out.reshape(b, t, D_MODEL)


def _log_sigmoid(x):
    return jnp.minimum(x, 0.0) - jnp.log(1.0 + jnp.exp(-jnp.abs(x)))


def _rot_half(x, first_half):
    return jnp.where(first_half, pltpu.roll(x, 96, axis=1), pltpu.roll(x, 32, axis=1))


def _values_and_ones(v):
    vt = jnp.transpose(v)
    row = lax.broadcasted_iota(jnp.int32, vt.shape, 0)
    return jnp.where(row < HEAD_DIM, vt, 1.0)


def _odd_kernel(*refs, t, lc, latent):
    if latent:
        (x_ref, mod_ref, ng_ref, win_ref, wg_ref, gb_ref, sink_ref, hn_ref, wout_ref,
         cos_ref, sin_ref, kc_ref, vc_ref, cin_ref, nin_ref, min_ref,
         o_ref,
         qa_s, qb_s, kk_s, vvt_s, qma_s, qmb_s, qm_s, km_s, vmt_s, vmtf_s, om_s, g_s, z_s, hft_s, hbt_s,
         sr_s, st_s, qc_s, c_s, n_s, m_s, kc_s, vct_s) = refs
    else:
        (x_ref, mod_ref, ng_ref, win_ref, wg_ref, gb_ref, sink_ref, hn_ref, wout_ref,
         o_ref, ko_ref, vo_ref, cf_ref, cb_ref, no_ref, mo_ref,
         qa_s, qb_s, kk_s, vvt_s, qma_s, qmb_s, qm_s, km_s, vmt_s, vmtf_s, om_s, g_s, z_s, hft_s, hbt_s,
         sr_s, st_s, qc_s, c_s, n_s, m_s) = refs

    n_blocks = t // CHUNK
    pad = CHUNK if latent else 0
    shift1, scale1, gate1 = mod_ref[0:1, :], mod_ref[1:2, :], mod_ref[2:3, :]
    lane = lax.broadcasted_iota(jnp.int32, (1, LANES), 1)
    left = lane < HEAD_DIM
    first_half = (lane % HEAD_DIM) < (HEAD_DIM // 2)
    ti = lax.broadcasted_iota(jnp.int32, (CHUNK, CHUNK), 0)
    si = lax.broadcasted_iota(jnp.int32, (CHUNK, CHUNK), 1)
    top = ti < HEAD_DIM
    same_head = top == (si < HEAD_DIM)

    if latent:
        for kv in range(N_KV):
            kk_s[kv, 0:CHUNK, :] = jnp.zeros((CHUNK, LANES), BF16)
            kk_s[kv, CHUNK + t:2 * CHUNK + t, :] = jnp.zeros((CHUNK, LANES), BF16)
            vvt_s[kv, :, 0:CHUNK] = jnp.zeros((LANES, CHUNK), BF16)
            vvt_s[kv, :, CHUNK + t:2 * CHUNK + t] = jnp.zeros((LANES, CHUNK), BF16)
            kc_s[kv] = kc_ref[kv].astype(BF16)
            vct_s[kv] = _values_and_ones(vc_ref[kv]).astype(BF16)

    def project(rc, carry):
        r0 = pl.multiple_of(rc * ROW_CHUNK, ROW_CHUNK)
        rows = pl.ds(r0, ROW_CHUNK)
        krows = pl.ds(r0 + pad, ROW_CHUNK)
        h = _norm_mod(x_ref[rows, :], ng_ref[0:1, :], shift1, scale1).astype(BF16)
        q = _dot(h, win_ref[:, 0:D_HALF])
        kv2 = _dot(h, win_ref[:, D_HALF:D_HALF + 2 * LANES])
        ka, va = kv2[:, 0:LANES], kv2[:, LANES:2 * LANES]
        if latent:
            cs, sn = cos_ref[rows, :], sin_ref[rows, :]
            ka = ka * cs + _rot_half(ka, first_half) * sn
        kr = pltpu.roll(ka, HEAD_DIM, axis=1)
        vr = pltpu.roll(va, HEAD_DIM, axis=1)
        if not latent:
            ka_t, va_t = jnp.transpose(ka), jnp.transpose(va)
            for kv in range(N_KV):
                ko_ref[0, kv, :, rows] = ka_t[kv * HEAD_DIM:(kv + 1) * HEAD_DIM, :]
                vo_ref[0, kv, :, rows] = va_t[kv * HEAD_DIM:(kv + 1) * HEAD_DIM, :]
        kk_s[0, krows, :] = jnp.where(left, ka, kr).astype(BF16)
        kk_s[1, krows, :] = jnp.where(left, kr, ka).astype(BF16)
        vvt_s[0, :, krows] = _values_and_ones(va).astype(BF16)
        vvt_s[1, :, krows] = _values_and_ones(vr).astype(BF16)
        base = D_HALF + 2 * LANES
        qm = _dot(h, win_ref[:, base:base + D_HALF])
        qm_s[rows, :] = qm.astype(BF16)
        vm = _dot(h, win_ref[:, base + 2 * D_HALF:base + 3 * D_HALF])
        for p in range(N_PAIRS):
            cols = slice(p * LANES, (p + 1) * LANES)
            qp = q[:, cols]
            if latent:
                qp = qp * cs + _rot_half(qp, first_half) * sn
            qp = qp * ATT_SCALE
            qa_s[rows, cols] = jnp.where(left, qp, 0.0).astype(BF16)
            qb_s[rows, cols] = jnp.where(left, 0.0, qp).astype(BF16)
            qma_s[rows, cols] = jnp.where(left, qm[:, cols], 0.0).astype(BF16)
            qmb_s[rows, cols] = jnp.where(left, 0.0, qm[:, cols]).astype(BF16)
            vt = jnp.transpose(vm[:, cols])
            vmtf_s[cols, rows] = vt
            vmt_s[cols, rows] = vt.astype(BF16)
        km = _dot(h, win_ref[:, base + D_HALF:base + 2 * D_HALF]) * (HEAD_DIM ** -0.5)
        km_s[rows, :] = km.astype(BF16)
        om_s[rows, :] = _dot(h, win_ref[:, base + 3 * D_HALF:base + 4 * D_HALF])
        g_s[rows, :] = _dot(h, wg_ref[...]) + gb_ref[...]
        return carry

    lax.fori_loop(0, t // ROW_CHUNK, project, 0)

    kj = lax.broadcasted_iota(jnp.int32, (3 * CHUNK, 2 * CHUNK), 0)
    qi = lax.broadcasted_iota(jnp.int32, (3 * CHUNK, 2 * CHUNK), 1) % CHUNK
    band_ok = jnp.abs(kj - CHUNK - qi) <= WINDOW
    head_a = lax.broadcasted_iota(jnp.int32, (1, 2 * CHUNK), 1) < CHUNK

    n_keys = lc + 3 * CHUNK if latent else lc
    group = 2 if latent else N_PAIRS

    def attention(s_s, p_s):
        def attend(i, carry):
            r0 = pl.multiple_of(i * CHUNK, CHUNK)
            rows = pl.ds(r0, CHUNK)
            if latent:
                key_pos = kj + (i - 1) * CHUNK
                mask = band_ok & (key_pos >= 0) & (key_pos < t)
                win = pl.ds(r0, 3 * CHUNK)
            for g0 in range(0, N_PAIRS, group):
                for gi in range(group):
                    p = g0 + gi
                    kv = p // (N_PAIRS // N_KV)
                    cols = slice(p * LANES, (p + 1) * LANES)
                    q2 = jnp.concatenate([qa_s[rows, cols], qb_s[rows, cols]], axis=0)
                    if latent:
                        s_s[gi, 0:lc, :] = _dot_nt(kc_s[kv], q2)
                        s_s[gi, lc:n_keys, :] = jnp.where(mask, _dot_nt(kk_s[kv, win, :], q2), -jnp.inf)
                    else:
                        s_s[gi] = _dot_nt(kk_s[kv], q2)
                maxes = []
                for gi in range(group):
                    p = g0 + gi
                    sink = jnp.where(head_a, sink_ref[2 * p:2 * p + 1, 0:1], sink_ref[2 * p + 1:2 * p + 2, 0:1])
                    mx = jnp.maximum(jnp.max(s_s[gi], axis=0, keepdims=True), sink)
                    p_s[gi] = jnp.exp(s_s[gi] - mx).astype(BF16)
                    maxes.append((sink, mx))
                for gi in range(group):
                    p = g0 + gi
                    kv = p // (N_PAIRS // N_KV)
                    cols = slice(p * LANES, (p + 1) * LANES)
                    if latent:
                        num = (_dot(vct_s[kv], p_s[gi, 0:lc, :]) + _dot(vvt_s[kv, :, win], p_s[gi, lc:n_keys, :]))
                    else:
                        num = _dot(vvt_s[kv], p_s[gi])
                    sink, mx = maxes[gi]
                    den = num[HEAD_DIM:HEAD_DIM + 1, :] + jnp.exp(sink - mx)
                    out = num[0:HEAD_DIM, :] * (1.0 / den)
                    pair = jnp.concatenate([out[:, 0:CHUNK], out[:, CHUNK:2 * CHUNK]], axis=0)
                    z_s[rows, cols] = jnp.transpose(pair).astype(BF16)
            return carry

        return attend

    if latent:
        c_s[...] = cin_ref[...]
        n_s[...] = nin_ref[...]
        m_s[...] = min_ref[...]
    else:
        c_s[...] = jnp.zeros(c_s.shape, F32)
        n_s[...] = jnp.zeros(n_s.shape, F32)
        m_s[...] = jnp.zeros(m_s.shape, F32)

    see = (ti <= si, ti >= si)
    tri = tuple(m.astype(F32).astype(BF16) for m in see)
    last = (CHUNK - 1, 0)

    def mlstm(i, carry):
        offs = (pl.multiple_of(i * CHUNK, CHUNK), pl.multiple_of((n_blocks - 1 - i) * CHUNK, CHUNK))
        half = (slice(0, LANES), slice(LANES, 2 * LANES))
        a_rows, b_rows = [], []
        for d in range(2):
            gt = jnp.transpose(g_s[pl.ds(offs[d], CHUNK), :])
            lf = _log_sigmoid(gt[2 * N_HEADS:4 * N_HEADS, :])
            bcum = _dot_exact_lhs(lf, tri[d])[8 * d:8 * d + 8, :]
            a_rows.append(gt[8 * d:8 * d + 8, :] - bcum)
            b_rows.append(bcum)
        q_ns = []
        for d in range(2):
            rows = pl.ds(offs[d], CHUNK)
            for p in range(N_PAIRS):
                u = d * N_PAIRS + p
                cols = slice(p * LANES, (p + 1) * LANES)
                q2 = jnp.concatenate([qma_s[rows, cols], qmb_s[rows, cols]], axis=0)
                sr_s[u] = _dot_nt(km_s[rows, cols], q2)
                qc_s[u] = _dot_nt(c_s[u].astype(BF16), qm_s[rows, cols])
                n8 = jnp.broadcast_to(n_s[u:u + 1, :], (8, LANES)).astype(BF16)
                q_ns.append(_dot_nt(n8, q2)[0:1, :])
        stats = {}
        for d in range(2):
            for hd in range(N_HEADS):
                u, j = d * N_PAIRS + hd // 2, hd % 2
                mrow = d * N_HEADS + hd
                a_row = a_rows[d][hd:hd + 1, :]
                b_row = b_rows[d][hd:hd + 1, :]
                m_prev = m_s[mrow:mrow + 1, 0:1]
                a_col = jnp.transpose(jnp.broadcast_to(a_row, (CHUNK, CHUNK)))
                z_t = jnp.where(see[d], a_col, -jnp.inf)
                m_run = jnp.maximum(jnp.max(z_t, axis=0, keepdims=True), m_prev)
                s_t = sr_s[u, :, half[j]] * jnp.exp(z_t - m_run)
                st_s[u, :, half[j]] = s_t.astype(BF16)
                w_int = jnp.exp(m_prev - m_run)
                den = jnp.sum(s_t, axis=0, keepdims=True) + w_int * q_ns[u][:, half[j]]
                inv = 1.0 / jnp.maximum(jnp.abs(den), jnp.exp(-(b_row + m_run)))
                m_last = m_run[:, last[d]:last[d] + 1]
                stats[(u, j)] = (w_int, inv, jnp.exp(a_row - m_last), jnp.exp(m_prev - m_last))
                m_s[mrow:mrow + 1, :] = jnp.broadcast_to(b_row[:, last[d]:last[d] + 1] + m_last, (1, LANES))
        for d in range(2):
            rows = pl.ds(offs[d], CHUNK)
            ht_s = (hft_s, hbt_s)[d]
            for p in range(N_PAIRS):
                u = d * N_PAIRS + p
                cols = slice(p * LANES, (p + 1) * LANES)
                (w_a, inv_a, e_a, dec_a), (w_b, inv_b, e_b, dec_b) = stats[(u, 0)], stats[(u, 1)]
                kp = km_s[rows, cols]
                num2 = _dot(vmt_s[cols, rows], st_s[u])
                num = jnp.where(top, num2[:, half[0]], num2[:, half[1]])
                ht_s[cols, rows] = (num + jnp.where(top, w_a, w_b) * qc_s[u]) * jnp.where(top, inv_a, inv_b)
                vt_e = (vmtf_s[cols, rows] * jnp.where(top, e_a, e_b)).astype(BF16)
                c_s[u] = jnp.where(top, dec_a, dec_b) * c_s[u] + jnp.where(same_head, _dot(vt_e, kp), 0.0)
                e2 = jnp.concatenate([e_a, e_b, jnp.zeros((6, CHUNK), F32)], axis=0).astype(BF16)
                n_k = _dot(e2, kp)
                n_s[u:u + 1, :] = (jnp.where(left, dec_a, dec_b) * n_s[u:u + 1, :]
                                   + jnp.where(left, n_k[0:1, :], n_k[1:2, :]))
        return carry

    def mixers(s_s, p_s):
        attend = attention(s_s, p_s)

        def both(i, carry):
            attend(i, carry)
            return mlstm(i, carry)

        lax.fori_loop(0, n_blocks, both, 0, unroll=2)

    pl.run_scoped(mixers, pltpu.VMEM((group, n_keys, 2 * CHUNK), F32),
                  pltpu.VMEM((group, n_keys, 2 * CHUNK), BF16))

    if not latent:
        for d, c_ref in enumerate((cf_ref, cb_ref)):
            for p in range(N_PAIRS):
                c_pair = jnp.transpose(c_s[d * N_PAIRS + p])
                c_ref[0, 2 * p] = c_pair[0:HEAD_DIM, 0:HEAD_DIM]
                c_ref[0, 2 * p + 1] = c_pair[HEAD_DIM:LANES, HEAD_DIM:LANES]
        no_ref[...] = n_s[...]
        mo_ref[...] = m_s[...]

    top_w = lax.broadcasted_iota(jnp.int32, (LANES, ROW_CHUNK), 0) < HEAD_DIM

    def finish(rc, carry):
        r0 = pl.multiple_of(rc * ROW_CHUNK, ROW_CHUNK)
        rows = pl.ds(r0, ROW_CHUNK)
        for p in range(N_PAIRS):
            cols = slice(p * LANES, (p + 1) * LANES)
            hm = hft_s[cols, rows] + hbt_s[cols, rows]
            sq = hm * hm
            ms_a = jnp.sum(sq[0:HEAD_DIM], axis=0, keepdims=True)
            ms_b = jnp.sum(sq[HEAD_DIM:LANES], axis=0, keepdims=True)
            ms = jnp.where(top_w, ms_a, ms_b) * (1.0 / HEAD_DIM)
            y = jnp.transpose(hm * lax.rsqrt(ms + EPS)) * hn_ref[:, cols] * jax.nn.sigmoid(om_s[rows, cols])
            z_s[rows, D_HALF + p * LANES:D_HALF + (p + 1) * LANES] = y.astype(BF16)
        o = _dot(z_s[rows, :], wout_ref[...])
        o_ref[rows, :] = x_ref[rows, :] + gate1 * _rms(o, ng_ref[1:2, :])
        return carry

    lax.fori_loop(0, t // ROW_CHUNK, finish, 0)


def _odd_scratch(t, lc, latent):
    pad = 2 * CHUNK if latent else 0
    shapes = [
        pltpu.VMEM((t, D_HALF), BF16),
        pltpu.VMEM((t, D_HALF), BF16),
        pltpu.VMEM((N_KV, t + pad, LANES), BF16),
        pltpu.VMEM((N_KV, LANES, t + pad), BF16),
        pltpu.VMEM((t, D_HALF), BF16),
        pltpu.VMEM((t, D_HALF), BF16),
        pltpu.VMEM((t, D_HALF), BF16),
        pltpu.VMEM((t, D_HALF), BF16),
        pltpu.VMEM((D_HALF, t), BF16),
        pltpu.VMEM((D_HALF, t), F32),
        pltpu.VMEM((t, D_HALF), F32),
        pltpu.VMEM((t, LANES), F32),
        pltpu.VMEM((t, D_MODEL), BF16),
        pltpu.VMEM((D_HALF, t), F32),
        pltpu.VMEM((D_HALF, t), F32),
        pltpu.VMEM((2 * N_PAIRS, CHUNK, 2 * LANES), F32),
        pltpu.VMEM((2 * N_PAIRS, CHUNK, 2 * LANES), BF16),
        pltpu.VMEM((2 * N_PAIRS, LANES, CHUNK), F32),
        pltpu.VMEM((2 * N_PAIRS, LANES, LANES), F32),
        pltpu.VMEM((2 * N_PAIRS, LANES), F32),
        pltpu.VMEM((2 * N_HEADS, LANES), F32),
    ]
    if latent:
        shapes += [pltpu.VMEM((N_KV, lc, LANES), BF16), pltpu.VMEM((N_KV, LANES, lc), BF16)]
    return shapes


def _odd_common_specs(t, layer, cond_base, cond_stride):
    j = layer // 2
    return [
        pl.BlockSpec((None, t, D_MODEL), lambda i: (i, 0, 0)),
        _mod_spec(layer, lambda i: cond_base + cond_stride * i),
        _layer_spec((4, D_MODEL), layer),
        _layer_spec((D_MODEL, D_IN_ODD), j),
        _const_spec((D_MODEL, LANES)),
        _const_spec((1, LANES)),
        _const_spec((N_HEADS, LANES)),
        _layer_spec((1, D_HALF), j),
        _layer_spec((D_MODEL, D_MODEL), j),
    ]


def _per_seq(shape):
    return pl.BlockSpec((None,) + shape, lambda i: (i,) + (0,) * len(shape))


def _odd_context(x, modv, layer, ng, w_main, w_gate, gate_bias, sink_b, hnorm, wout):
    b, t, _ = x.shape
    kern = functools.partial(_odd_kernel, t=t, lc=t, latent=False)
    return pl.pallas_call(
        kern,
        grid=(b,),
        in_specs=_odd_common_specs(t, layer, 0, 0),
        out_specs=[_per_seq((t, D_MODEL)),
                   _per_seq((1, N_KV, HEAD_DIM, t)), _per_seq((1, N_KV, HEAD_DIM, t)),
                   _per_seq((1, N_HEADS, HEAD_DIM, HEAD_DIM)), _per_seq((1, N_HEADS, HEAD_DIM, HEAD_DIM)),
                   _per_seq((2 * N_PAIRS, LANES)), _per_seq((2 * N_HEADS, LANES))],
        out_shape=[jax.ShapeDtypeStruct((b, t, D_MODEL), F32),
                   jax.ShapeDtypeStruct((b, 1, N_KV, HEAD_DIM, t), F32),
                   jax.ShapeDtypeStruct((b, 1, N_KV, HEAD_DIM, t), F32),
                   jax.ShapeDtypeStruct((b, 1, N_HEADS, HEAD_DIM, HEAD_DIM), F32),
                   jax.ShapeDtypeStruct((b, 1, N_HEADS, HEAD_DIM, HEAD_DIM), F32),
                   jax.ShapeDtypeStruct((b, 2 * N_PAIRS, LANES), F32),
                   jax.ShapeDtypeStruct((b, 2 * N_HEADS, LANES), F32)],
        scratch_shapes=_odd_scratch(t, t, False),
        compiler_params=_params(1),
        name="odd_mixer_context",
    )(x, modv, ng, w_main, w_gate, gate_bias, sink_b, hnorm, wout)


def _odd_latent(x, modv, layer, ng, w_main, w_gate, gate_bias, sink_b, hnorm, wout, cos_t, sin_t, kc, vc,
                c_in, n_in, m_in):
    b, t, _ = x.shape
    lc = kc.shape[2]
    kern = functools.partial(_odd_kernel, t=t, lc=lc, latent=True)
    return pl.pallas_call(
        kern,
        grid=(b,),
        in_specs=_odd_common_specs(t, layer, 1, 1) + [
            _const_spec((t, LANES)), _const_spec((t, LANES)),
            _per_seq((N_KV, lc, LANES)), _per_seq((N_KV, lc, LANES)),
            _per_seq((2 * N_PAIRS, LANES, LANES)), _per_seq((2 * N_PAIRS, LANES)), _per_seq((2 * N_HEADS, LANES)),
        ],
        out_specs=_per_seq((t, D_MODEL)),
        out_shape=jax.ShapeDtypeStruct((b, t, D_MODEL), F32),
        scratch_shapes=_odd_scratch(t, lc, True),
        compiler_params=_params(1),
        name="odd_mixer_latent",
    )(x, modv, ng, w_main, w_gate, gate_bias, sink_b, hnorm, wout, cos_t, sin_t, kc, vc, c_in, n_in, m_in)


def _rope_tables(t):
    rows = t // GRID_W
    row = jnp.broadcast_to(jnp.arange(rows)[:, None], (rows, GRID_W)).reshape(t).astype(F32)
    col = jnp.broadcast_to(jnp.arange(GRID_W)[None, :], (rows, GRID_W)).reshape(t).astype(F32)
    n_freq = HEAD_DIM // 4
    inv_freq = ROPE_BASE ** (-jnp.arange(n_freq, dtype=F32) / n_freq)
    ang = jnp.concatenate([row[:, None] * inv_freq, col[:, None] * inv_freq], axis=-1)
    cos, sin = jnp.cos(ang), jnp.sin(ang)
    cos_l = jnp.tile(cos, (1, LANES // cos.shape[1]))
    sin_l = jnp.tile(jnp.concatenate([-sin, sin], axis=-1), (1, LANES // HEAD_DIM))
    return cos_l, sin_l


def _pair_blockdiag(c):
    b = c.shape[0]
    c = c.reshape(b, N_PAIRS, 2, HEAD_DIM, HEAD_DIM)
    z = jnp.zeros_like(c[:, :, 0])
    top = jnp.concatenate([c[:, :, 0], z], axis=-1)
    bot = jnp.concatenate([z, c[:, :, 1]], axis=-1)
    return jnp.concatenate([top, bot], axis=-2)


def _lane_bcast(v):
    return jnp.broadcast_to(v[..., None], v.shape + (LANES,))


def kernel(x_prompt, x_sample, c, cache_k, cache_v, state_c_fwd, state_n_fwd, state_m_fwd, state_c_bwd, state_n_bwd, state_m_bwd, c_ctx, mod_w, mod_b, norm_g, mlp_w1, mlp_w2, even_in_w, conv_a_w, conv_a_b, ln_a_g, ln_a_b, conv_b_w, even_out_w, odd_in_w, attn_sink, gate_b, hnorm_g, odd_out_w):
    n_dec = x_sample.shape[0]
    n_ctx = x_prompt.shape[0]
    cond = jnp.concatenate([c_ctx[None, :], c, jnp.zeros((COND_ROWS - 1 - n_dec, D_MODEL), F32)], axis=0)
    modv = _modulation(cond, mod_w, mod_b)

    yp, ys = x_prompt, x_sample
    w1, w2 = mlp_w1.astype(BF16), mlp_w2.astype(BF16)

    ev = (norm_g, even_in_w.astype(BF16), conv_a_w, conv_a_b[:, None, :], ln_a_g[:, None, :], ln_a_b[:, None, :],
          conv_b_w, even_out_w.astype(BF16))
    yp = _even_layer(yp, modv, 0, 0, 0, *ev)
    ys = _even_layer(ys, modv, 0, 1, 1, *ev)
    yp = _mlp_layer(yp, modv, 0, 0, 0, norm_g, w1, w2)
    ys = _mlp_layer(ys, modv, 0, 1, 1, norm_g, w1, w2)

    order = jnp.array([0, 2, 1, 3])
    d_main = D_IN_ODD - 4 * N_HEADS
    wg = odd_in_w[0][:, d_main:].reshape(D_MODEL, 4, N_HEADS)[:, order, :].reshape(D_MODEL, 4 * N_HEADS)
    w_gate = jnp.pad(wg, ((0, 0), (0, LANES - 4 * N_HEADS))).astype(BF16)
    gate_bias = jnp.pad(gate_b[0][order, :].reshape(1, 4 * N_HEADS), ((0, 0), (0, LANES - 4 * N_HEADS)))
    sink_b = _lane_bcast(attn_sink[0])
    odd = (1, norm_g, odd_in_w.astype(BF16), w_gate, gate_bias, sink_b, hnorm_g[:, None, :], odd_out_w.astype(BF16))

    op, k_t, v_t, c_f, c_b, n_new, m_new = _odd_context(yp, modv, *odd)
    new_k, new_v = jnp.swapaxes(k_t, -1, -2), jnp.swapaxes(v_t, -1, -2)

    t_dec = x_sample.shape[1]
    cos_t, sin_t = _rope_tables(t_dec)
    kc = jnp.concatenate([cache_k[:, 0], cache_k[:, 0]], axis=-1)
    vc = jnp.concatenate([cache_v[:, 0], cache_v[:, 0]], axis=-1)
    c_in = jnp.concatenate([_pair_blockdiag(jnp.swapaxes(state_c_fwd[:, 0], -1, -2)),
                            _pair_blockdiag(jnp.swapaxes(state_c_bwd[:, 0], -1, -2))], axis=1)
    n_in = jnp.concatenate([state_n_fwd[:, 0].reshape(n_dec, N_PAIRS, LANES),
                            state_n_bwd[:, 0].reshape(n_dec, N_PAIRS, LANES)], axis=1)
    m_in = _lane_bcast(jnp.concatenate([state_m_fwd[:, 0], state_m_bwd[:, 0]], axis=1))
    os_ = _odd_latent(ys, modv, *odd, cos_t, sin_t, kc, vc, c_in, n_in, m_in)

    yp = _mlp_layer(op, modv, 1, 0, 0, norm_g, w1, w2)
    ys = _mlp_layer(os_, modv, 1, 1, 1, norm_g, w1, w2)

    n_f = n_new[:, :N_PAIRS].reshape(n_ctx, N_HEADS, HEAD_DIM)[:, None]
    n_b = n_new[:, N_PAIRS:].reshape(n_ctx, N_HEADS, HEAD_DIM)[:, None]
    m_f = m_new[:, :N_HEADS, 0][:, None]
    m_b = m_new[:, N_HEADS:, 0][:, None]
    return (yp, ys, new_k, new_v, c_f, n_f, m_f, c_b, n_b, m_b)
```

```python
import functools

import jax
import jax.numpy as jnp
from jax import lax
from jax.experimental import pallas as pl
from jax.experimental.pallas import tpu as pltpu

F32 = jnp.float32
BF16 = jnp.bfloat16

D_MODEL = 1024
D_FF = 4 * D_MODEL
EPS = 1e-6
D_HALF = D_MODEL // 2
CONF_WIDTH = 31
CONF_HALO = 16
HEAD_DIM = 64
N_HEADS = 8
N_PAIRS = N_HEADS // 2
N_KV = 2
LANES = 128
CHUNK = 128
WINDOW = 128
GRID_W = 64
ROPE_BASE = 10000.0
ATT_SCALE = HEAD_DIM ** -0.5
D_IN_ODD = D_HALF + 2 * N_KV * HEAD_DIM + 4 * D_HALF + 4 * N_HEADS
ROW_CHUNK = 256
MLP_SUB_ROWS = 512
ODD_CTX_SEQS = 2
ODD_STAGGER = 6
EVEN_CHUNKS = 2
EVEN_STAGGER = 3
COND_ROWS = 8
VMEM_LIMIT = 56 * 1024 * 1024


def _dot(a, b):
    return jnp.dot(a, b, preferred_element_type=F32)


def _dot_nt(a, b):
    return lax.dot_general(a, b, (((1,), (1,)), ((), ())), preferred_element_type=F32)


def _split3(x):
    hi = x.astype(BF16)
    r1 = x - hi.astype(F32)
    mid = r1.astype(BF16)
    lo = (r1 - mid.astype(F32)).astype(BF16)
    return hi, mid, lo


def _dot_exact_lhs(x, b01):
    hi, mid, lo = _split3(x)
    return _dot(hi, b01) + _dot(mid, b01) + _dot(lo, b01)


def _rms(x, g):
    return x * lax.rsqrt(jnp.mean(x * x, axis=-1, keepdims=True) + EPS) * g


def _norm_mod(x, g, shift, scale):
    return _rms(x, g) * (1.0 + scale) + shift


def _params(n_grid):
    return pltpu.CompilerParams(dimension_semantics=("arbitrary",) * n_grid, vmem_limit_bytes=VMEM_LIMIT)


def _const_spec(shape):
    zeros = (0,) * len(shape)
    return pl.BlockSpec(shape, lambda *_: zeros, pipeline_mode=pl.Buffered(1))


def _layer_spec(shape, layer):
    index = (layer,) + (0,) * len(shape)
    return pl.BlockSpec((None,) + shape, lambda *_: index, pipeline_mode=pl.Buffered(1))


def _mod_spec(layer, cond_of):
    return pl.BlockSpec((None, None, 6, D_MODEL), lambda *idx: (layer, cond_of(*idx), 0, 0))


def _mod_kernel(cond_ref, w_ref, b_ref, o_ref):
    s = jax.nn.silu(cond_ref[...]).astype(BF16)
    o_ref[...] = _dot(s, w_ref[...].astype(BF16)) + b_ref[...]


def _modulation(cond, mod_w, mod_b):
    depth = mod_w.shape[0]
    n_out = mod_w.shape[2]
    tn = D_MODEL
    out = pl.pallas_call(
        _mod_kernel,
        grid=(depth, n_out // tn),
        in_specs=[
            pl.BlockSpec((COND_ROWS, D_MODEL), lambda l, j: (0, 0)),
            pl.BlockSpec((None, D_MODEL, tn), lambda l, j: (l, 0, j)),
            pl.BlockSpec((None, 1, tn), lambda l, j: (l, 0, j)),
        ],
        out_specs=pl.BlockSpec((None, COND_ROWS, tn), lambda l, j: (l, 0, j)),
        out_shape=jax.ShapeDtypeStruct((depth, COND_ROWS, n_out), F32),
        compiler_params=_params(2),
        name="modulation",
    )(cond, mod_w, mod_b.reshape(depth, 1, n_out))
    return out.reshape(depth, COND_ROWS, 6, D_MODEL)


def _run_staggered(programs, stagger):
    programs = list(programs)
    live, rounds = [], 0
    while programs or live:
        if programs and rounds % stagger == 0:
            live.append(programs.pop(0))
        for g in list(live):
            try:
                next(g)
            except StopIteration:
                live.remove(g)
        rounds += 1


def _even_kernel(*refs, n_chunks, nsub):
    halos, (xc_ref, mod_ref, ng_ref, win_ref, caw_ref, cab_ref, lng_ref, lnb_ref, cbw_ref, wout_ref, o_ref,
            apad, cpad, bgs, zs) = refs[:2 * nsub], refs[2 * nsub:]
    shared = (mod_ref, ng_ref, win_ref, caw_ref, cab_ref, lng_ref, lnb_ref, cbw_ref, wout_ref)
    _run_staggered(
        (_even_chunk(pl.program_id(0) * nsub + k, halos[2 * k], xc_ref.at[k], halos[2 * k + 1], *shared,
                     o_ref.at[k], apad.at[k], cpad.at[k], bgs.at[k], zs.at[k], n_chunks=n_chunks)
         for k in range(nsub)), EVEN_STAGGER)


def _even_chunk(g, xp_ref, xc_ref, xn_ref, mod_ref, ng_ref, win_ref, caw_ref, cab_ref, lng_ref, lnb_ref,
                cbw_ref, wout_ref, o_ref, apad, cpad, bgs, zs, *, n_chunks):
    c = g % n_chunks if n_chunks > 1 else 0
    rows = ROW_CHUNK + 2 * CONF_HALO
    shift1, scale1, gate1 = mod_ref[0:1, :], mod_ref[1:2, :], mod_ref[2:3, :]
    xh = jnp.concatenate([xp_ref[...], xc_ref[...], xn_ref[...]], axis=0)
    h = _norm_mod(xh, ng_ref[0:1, :], shift1, scale1).astype(BF16)
    ri = lax.broadcasted_iota(jnp.int32, (rows, D_HALF), 0)
    lo = jnp.where(c == 0, CONF_HALO, 0)
    hi = jnp.where(c == n_chunks - 1, CONF_HALO + ROW_CHUNK, rows)
    inside = (ri >= lo) & (ri < hi)
    a = _dot(h, win_ref[:, 0:D_HALF]) * jax.nn.sigmoid(_dot(h, win_ref[:, D_HALF:2 * D_HALF]))
    apad[...] = jnp.where(inside, a, 0.0)
    yield
    cx = _dot(h, win_ref[:, 3 * D_HALF:4 * D_HALF]) * _dot(h, win_ref[:, 4 * D_HALF:5 * D_HALF])
    cpad[...] = jnp.where(inside, cx, 0.0)
    yield
    bgs[...] = _dot(h[CONF_HALO:CONF_HALO + ROW_CHUNK], win_ref[:, 2 * D_HALF:3 * D_HALF])
    yield

    sub = 64
    tile = 8
    for j in range(ROW_CHUNK // sub):
        r0 = j * sub
        groups = []
        for cg in range(D_HALF // LANES):
            cols = slice(cg * LANES, (cg + 1) * LANES)
            acc = None
            for r in range(tile):
                part = None
                for m in range(-(-(CONF_WIDTH + 1) // tile)):
                    o = tile * m + r
                    if 1 <= o <= CONF_WIDTH:
                        term = caw_ref[o - 1:o, cols] * apad[r0 + tile * m:r0 + tile * m + sub + tile, cols]
                        part = term if part is None else part + term
                shifted = part[r:r + sub, :]
                acc = shifted if acc is None else acc + shifted
            groups.append(acc)
        acc = jnp.concatenate(groups, axis=1) + cab_ref[...]
        mu = jnp.mean(acc, axis=-1, keepdims=True)
        dlt = acc - mu
        var = jnp.mean(dlt * dlt, axis=-1, keepdims=True)
        a_out = jax.nn.silu(dlt * lax.rsqrt(var + EPS) * lng_ref[...] + lnb_ref[...])
        zs[r0:r0 + sub, 0:D_HALF] = a_out.astype(BF16)
        base = r0 + CONF_HALO - 1
        sc = (cbw_ref[0:1, :] * cpad[base:base + sub, :]
              + cbw_ref[1:2, :] * cpad[base + 1:base + 1 + sub, :]
              + cbw_ref[2:3, :] * cpad[base + 2:base + 2 + sub, :])
        zs[r0:r0 + sub, D_HALF:D_MODEL] = (bgs[r0:r0 + sub, :] * sc).astype(BF16)
        yield

    o = _dot(zs[...], wout_ref[...])
    o_ref[...] = xc_ref[...] + gate1 * _rms(o, ng_ref[1:2, :])
    yield


def _even_layer(x, modv, layer, cond_base, cond_stride, ng, win, caw, cab, lng, lnb, cbw, wout):
    j = layer // 2
    b, t, _ = x.shape
    n_chunks = t // ROW_CHUNK
    nsub = EVEN_CHUNKS
    assert (b * n_chunks) % nsub == 0 and (cond_stride == 0 or n_chunks % nsub == 0)
    hpc = ROW_CHUNK // CONF_HALO
    n_halo_blocks = b * t // CONF_HALO
    rows = ROW_CHUNK + 2 * CONF_HALO
    halo_specs = []
    for k in range(nsub):
        halo_specs += [
            pl.BlockSpec((None, CONF_HALO, D_MODEL),
                         lambda i, k=k: (jnp.maximum((i * nsub + k) * hpc - 1, 0), 0, 0)),
            pl.BlockSpec((None, CONF_HALO, D_MODEL),
                         lambda i, k=k: (jnp.minimum((i * nsub + k + 1) * hpc, n_halo_blocks - 1), 0, 0)),
        ]
    x_halo = x.reshape(n_halo_blocks, CONF_HALO, D_MODEL)
    x_chunks = x.reshape(b * n_chunks, ROW_CHUNK, D_MODEL)
    kern = functools.partial(_even_kernel, n_chunks=n_chunks, nsub=nsub)
    out = pl.pallas_call(
        kern,
        grid=(b * n_chunks // nsub,),
        in_specs=halo_specs + [
            pl.BlockSpec((nsub, ROW_CHUNK, D_MODEL), lambda i: (i, 0, 0)),
            _mod_spec(layer, lambda i: cond_base + cond_stride * ((i * nsub) // n_chunks)),
            _layer_spec((4, D_MODEL), layer),
            _layer_spec((D_MODEL, 5 * D_HALF), j),
            _layer_spec((CONF_WIDTH, D_HALF), j),
            _layer_spec((1, D_HALF), j),
            _layer_spec((1, D_HALF), j),
            _layer_spec((1, D_HALF), j),
            _layer_spec((3, D_HALF), j),
            _layer_spec((D_MODEL, D_MODEL), j),
        ],
        out_specs=pl.BlockSpec((nsub, ROW_CHUNK, D_MODEL), lambda i: (i, 0, 0)),
        out_shape=jax.ShapeDtypeStruct(x_chunks.shape, F32),
        scratch_shapes=[
            pltpu.VMEM((nsub, rows, D_HALF), F32),
            pltpu.VMEM((nsub, rows, D_HALF), F32),
            pltpu.VMEM((nsub, ROW_CHUNK, D_HALF), F32),
            pltpu.VMEM((nsub, ROW_CHUNK, D_MODEL), BF16),
        ],
        compiler_params=_params(1),
        name="even_mixer",
    )(*([x_halo] * (2 * nsub)), x_chunks, modv, ng, win, caw, cab, lng, lnb, cbw, wout)
    return out.reshape(b, t, D_MODEL)


def _mlp_kernel(x_ref, mod_ref, ng_ref, w1_ref, w2_ref, o_ref):
    for r0 in range(0, x_ref.shape[0], MLP_SUB_ROWS):
        rows = slice(r0, r0 + MLP_SUB_ROWS)
        x = x_ref[rows, :]
        h = _norm_mod(x, ng_ref[2:3, :], mod_ref[3:4, :], mod_ref[4:5, :]).astype(BF16)
        acc = None
        for c in range(D_FF // D_MODEL):
            cols = slice(c * D_MODEL, (c + 1) * D_MODEL)
            hid = jnp.square(jnp.maximum(_dot(h, w1_ref[:, cols]), 0.0)).astype(BF16)
            part = _dot(hid, w2_ref[cols, :])
            acc = part if acc is None else acc + part
        o_ref[rows, :] = x + mod_ref[5:6, :] * _rms(acc, ng_ref[3:4, :])


def _mlp_layer(x, modv, layer, cond_base, cond_stride, ng, w1, w2):
    b, t, _ = x.shape
    tm = 2 * MLP_SUB_ROWS
    assert (b * t) % tm == 0 and (cond_stride == 0 or t % tm == 0)
    tiles_per_seq = max(t // tm, 1)
    x2 = x.reshape(b * t, D_MODEL)
    out = pl.pallas_call(
        _mlp_kernel,
        grid=(b * t // tm,),
        in_specs=[
            pl.BlockSpec((tm, D_MODEL), lambda i: (i, 0)),
            _mod_spec(layer, lambda i: cond_base + cond_stride * (i // tiles_per_seq)),
            _layer_spec((4, D_MODEL), layer),
            _layer_spec((D_MODEL, D_FF), layer),
            _layer_spec((D_FF, D_MODEL), layer),
        ],
        out_specs=pl.BlockSpec((tm, D_MODEL), lambda i: (i, 0)),
        out_shape=jax.ShapeDtypeStruct(x2.shape, F32),
        compiler_params=_params(1),
        name="mlp",
    )(x2, modv, ng, w1, w2)
    return out.reshape(b, t, D_MODEL)


def _log_sigmoid(x):
    return jnp.minimum(x, 0.0) - jnp.log(1.0 + jnp.exp(-jnp.abs(x)))


def _rot_half(x, first_half):
    return jnp.where(first_half, pltpu.roll(x, 96, axis=1), pltpu.roll(x, 32, axis=1))


def _values_and_ones(v):
    vt = jnp.transpose(v)
    row = lax.broadcasted_iota(jnp.int32, vt.shape, 0)
    return jnp.where(row < HEAD_DIM, vt, 1.0)


def _aligned(i, m):
    return i * m if isinstance(i, int) else pl.multiple_of(i * m, m)


def _drain(pieces):
    for _ in pieces:
        pass


def _each(body, n, static, unroll=1):
    if static:
        for i in range(n):
            yield from body(i)
    else:
        lax.fori_loop(0, n, lambda i, c: (_drain(body(i)), c)[1], 0, unroll=unroll)


def _alternate(*programs):
    live = list(programs)
    while live:
        for g in list(live):
            try:
                next(g)
            except StopIteration:
                live.remove(g)
            else:
                yield


def _odd_kernel(*refs, t, lc, latent, nseq):
    shared = set(range(1, 11 if latent else 9))
    _run_staggered((_odd_seq(*[r if k in shared else r.at[sq] for k, r in enumerate(refs)],
                             t=t, lc=lc, latent=latent) for sq in range(nseq)), ODD_STAGGER)


def _odd_seq(*refs, t, lc, latent):
    if latent:
        (x_ref, mod_ref, ng_ref, win_ref, wg_ref, gb_ref, sink_ref, hn_ref, wout_ref,
         cos_ref, sin_ref, kc_ref, vc_ref, cin_ref, nin_ref, min_ref,
         o_ref,
         qa_s, qb_s, kk_s, vvt_s, qma_s, qmb_s, qm_s, km_s, vmt_s, vmtf_s, om_s, g_s, z_s, hft_s, hbt_s,
         sr_s, st_s, qc_s, c_s, n_s, m_s, s_s, p_s, kc_s, vct_s) = refs
    else:
        (x_ref, mod_ref, ng_ref, win_ref, wg_ref, gb_ref, sink_ref, hn_ref, wout_ref,
         o_ref, ko_ref, vo_ref, cf_ref, cb_ref, no_ref, mo_ref,
         qa_s, qb_s, kk_s, vvt_s, qma_s, qmb_s, qm_s, km_s, vmt_s, vmtf_s, om_s, g_s, z_s, hft_s, hbt_s,
         sr_s, st_s, qc_s, c_s, n_s, m_s, s_s, p_s) = refs

    static = not latent
    n_blocks = t // CHUNK
    pad = CHUNK if latent else 0
    shift1, scale1, gate1 = mod_ref[0:1, :], mod_ref[1:2, :], mod_ref[2:3, :]
    lane = lax.broadcasted_iota(jnp.int32, (1, LANES), 1)
    left = lane < HEAD_DIM
    first_half = (lane % HEAD_DIM) < (HEAD_DIM // 2)
    ti = lax.broadcasted_iota(jnp.int32, (CHUNK, CHUNK), 0)
    si = lax.broadcasted_iota(jnp.int32, (CHUNK, CHUNK), 1)
    top = ti < HEAD_DIM
    same_head = top == (si < HEAD_DIM)

    if latent:
        for kv in range(N_KV):
            kk_s[kv, 0:CHUNK, :] = jnp.zeros((CHUNK, LANES), BF16)
            kk_s[kv, CHUNK + t:2 * CHUNK + t, :] = jnp.zeros((CHUNK, LANES), BF16)
            vvt_s[kv, :, 0:CHUNK] = jnp.zeros((LANES, CHUNK), BF16)
            vvt_s[kv, :, CHUNK + t:2 * CHUNK + t] = jnp.zeros((LANES, CHUNK), BF16)
            kc_s[kv] = kc_ref[kv].astype(BF16)
            vct_s[kv] = _values_and_ones(vc_ref[kv]).astype(BF16)

    def project(rc):
        r0 = _aligned(rc, ROW_CHUNK)
        rows = pl.ds(r0, ROW_CHUNK)
        krows = pl.ds(r0 + pad, ROW_CHUNK)
        h = _norm_mod(x_ref[rows, :], ng_ref[0:1, :], shift1, scale1).astype(BF16)
        if latent:
            cs, sn = cos_ref[rows, :], sin_ref[rows, :]
        q = _dot(h, win_ref[:, 0:D_HALF])
        for p in range(N_PAIRS):
            cols = slice(p * LANES, (p + 1) * LANES)
            qp = q[:, cols]
            if latent:
                qp = qp * cs + _rot_half(qp, first_half) * sn
            qp = qp * ATT_SCALE
            qa_s[rows, cols] = jnp.where(left, qp, 0.0).astype(BF16)
            qb_s[rows, cols] = jnp.where(left, 0.0, qp).astype(BF16)
        yield
        kv2 = _dot(h, win_ref[:, D_HALF:D_HALF + 2 * LANES])
        ka, va = kv2[:, 0:LANES], kv2[:, LANES:2 * LANES]
        if latent:
            ka = ka * cs + _rot_half(ka, first_half) * sn
        kr = pltpu.roll(ka, HEAD_DIM, axis=1)
        vr = pltpu.roll(va, HEAD_DIM, axis=1)
        if not latent:
            ka_t, va_t = jnp.transpose(ka), jnp.transpose(va)
            for kv in range(N_KV):
                ko_ref[0, kv, :, rows] = ka_t[kv * HEAD_DIM:(kv + 1) * HEAD_DIM, :]
                vo_ref[0, kv, :, rows] = va_t[kv * HEAD_DIM:(kv + 1) * HEAD_DIM, :]
        kk_s[0, krows, :] = jnp.where(left, ka, kr).astype(BF16)
        kk_s[1, krows, :] = jnp.where(left, kr, ka).astype(BF16)
        vvt_s[0, :, krows] = _values_and_ones(va).astype(BF16)
        vvt_s[1, :, krows] = _values_and_ones(vr).astype(BF16)
        yield
        base = D_HALF + 2 * LANES
        qm = _dot(h, win_ref[:, base:base + D_HALF])
        qm_s[rows, :] = qm.astype(BF16)
        for p in range(N_PAIRS):
            cols = slice(p * LANES, (p + 1) * LANES)
            qma_s[rows, cols] = jnp.where(left, qm[:, cols], 0.0).astype(BF16)
            qmb_s[rows, cols] = jnp.where(left, 0.0, qm[:, cols]).astype(BF16)
        yield
        vm = _dot(h, win_ref[:, base + 2 * D_HALF:base + 3 * D_HALF])
        for p in range(N_PAIRS):
            cols = slice(p * LANES, (p + 1) * LANES)
            vt = jnp.transpose(vm[:, cols])
            vmtf_s[cols, rows] = vt
            vmt_s[cols, rows] = vt.astype(BF16)
        yield
        km = _dot(h, win_ref[:, base + D_HALF:base + 2 * D_HALF]) * (HEAD_DIM ** -0.5)
        km_s[rows, :] = km.astype(BF16)
        yield
        om_s[rows, :] = _dot(h, win_ref[:, base + 3 * D_HALF:base + 4 * D_HALF])
        g_s[rows, :] = _dot(h, wg_ref[...]) + gb_ref[...]
        yield

    yield from _each(project, t // ROW_CHUNK, static)

    kj = lax.broadcasted_iota(jnp.int32, (3 * CHUNK, 2 * CHUNK), 0)
    qi = lax.broadcasted_iota(jnp.int32, (3 * CHUNK, 2 * CHUNK), 1) % CHUNK
    band_ok = jnp.abs(kj - CHUNK - qi) <= WINDOW
    head_a = lax.broadcasted_iota(jnp.int32, (1, 2 * CHUNK), 1) < CHUNK

    n_keys = _att_keys(lc, latent)
    group = _att_group(latent)

    def attention():
        def attend(i):
            r0 = _aligned(i, CHUNK)
            rows = pl.ds(r0, CHUNK)
            if latent:
                key_pos = kj + (i - 1) * CHUNK
                mask = band_ok & (key_pos >= 0) & (key_pos < t)
                win = pl.ds(r0, 3 * CHUNK)
            for g0 in range(0, N_PAIRS, group):
                for gi in range(group):
                    p = g0 + gi
                    kv = p // (N_PAIRS // N_KV)
                    cols = slice(p * LANES, (p + 1) * LANES)
                    q2 = jnp.concatenate([qa_s[rows, cols], qb_s[rows, cols]], axis=0)
                    if latent:
                        s_s[gi, 0:lc, :] = _dot_nt(kc_s[kv], q2)
                        s_s[gi, lc:n_keys, :] = jnp.where(mask, _dot_nt(kk_s[kv, win, :], q2), -jnp.inf)
                    else:
                        s_s[gi] = _dot_nt(kk_s[kv], q2)
                yield
                maxes = []
                for gi in range(group):
                    p = g0 + gi
                    sink = jnp.where(head_a, sink_ref[2 * p:2 * p + 1, 0:1], sink_ref[2 * p + 1:2 * p + 2, 0:1])
                    mx = jnp.maximum(jnp.max(s_s[gi], axis=0, keepdims=True), sink)
                    p_s[gi] = jnp.exp(s_s[gi] - mx).astype(BF16)
                    maxes.append((sink, mx))
                yield
                for gi in range(group):
                    p = g0 + gi
                    kv = p // (N_PAIRS // N_KV)
                    cols = slice(p * LANES, (p + 1) * LANES)
                    if latent:
                        num = (_dot(vct_s[kv], p_s[gi, 0:lc, :]) + _dot(vvt_s[kv, :, win], p_s[gi, lc:n_keys, :]))
                    else:
                        num = _dot(vvt_s[kv], p_s[gi])
                    sink, mx = maxes[gi]
                    den = num[HEAD_DIM:HEAD_DIM + 1, :] + jnp.exp(sink - mx)
                    out = num[0:HEAD_DIM, :] * (1.0 / den)
                    pair = jnp.concatenate([out[:, 0:CHUNK], out[:, CHUNK:2 * CHUNK]], axis=0)
                    z_s[rows, cols] = jnp.transpose(pair).astype(BF16)
                yield

        return attend

    if latent:
        c_s[...] = cin_ref[...]
        n_s[...] = nin_ref[...]
        m_s[...] = min_ref[...]
    else:
        c_s[...] = jnp.zeros(c_s.shape, F32)
        n_s[...] = jnp.zeros(n_s.shape, F32)
        m_s[...] = jnp.zeros(m_s.shape, F32)

    see = (ti <= si, ti >= si)
    tri = tuple(m.astype(F32).astype(BF16) for m in see)
    last = (CHUNK - 1, 0)

    def mlstm(i):
        offs = (_aligned(i, CHUNK), _aligned(n_blocks - 1 - i, CHUNK))
        half = (slice(0, LANES), slice(LANES, 2 * LANES))
        a_rows, b_rows = [], []
        for d in range(2):
            gt = jnp.transpose(g_s[pl.ds(offs[d], CHUNK), :])
            lf = _log_sigmoid(gt[2 * N_HEADS:4 * N_HEADS, :])
            bcum = _dot_exact_lhs(lf, tri[d])[8 * d:8 * d + 8, :]
            a_rows.append(gt[8 * d:8 * d + 8, :] - bcum)
            b_rows.append(bcum)
        yield
        q_ns = []
        for d in range(2):
            rows = pl.ds(offs[d], CHUNK)
            for p in range(N_PAIRS):
                u = d * N_PAIRS + p
                cols = slice(p * LANES, (p + 1) * LANES)
                q2 = jnp.concatenate([qma_s[rows, cols], qmb_s[rows, cols]], axis=0)
                sr_s[u] = _dot_nt(km_s[rows, cols], q2)
                qc_s[u] = _dot_nt(c_s[u].astype(BF16), qm_s[rows, cols])
                n8 = jnp.broadcast_to(n_s[u:u + 1, :], (8, LANES)).astype(BF16)
                q_ns.append(_dot_nt(n8, q2)[0:1, :])
            yield
        stats = {}
        for d in range(2):
            for hd in range(N_HEADS):
                u, j = d * N_PAIRS + hd // 2, hd % 2
                mrow = d * N_HEADS + hd
                a_row = a_rows[d][hd:hd + 1, :]
                b_row = b_rows[d][hd:hd + 1, :]
                m_prev = m_s[mrow:mrow + 1, 0:1]
                a_col = jnp.transpose(jnp.broadcast_to(a_row, (CHUNK, CHUNK)))
                z_t = jnp.where(see[d], a_col, -jnp.inf)
                m_run = jnp.maximum(jnp.max(z_t, axis=0, keepdims=True), m_prev)
                s_t = sr_s[u, :, half[j]] * jnp.exp(z_t - m_run)
                st_s[u, :, half[j]] = s_t.astype(BF16)
                w_int = jnp.exp(m_prev - m_run)
                den = jnp.sum(s_t, axis=0, keepdims=True) + w_int * q_ns[u][:, half[j]]
                inv = 1.0 / jnp.maximum(jnp.abs(den), jnp.exp(-(b_row + m_run)))
                m_last = m_run[:, last[d]:last[d] + 1]
                stats[(u, j)] = (w_int, inv, jnp.exp(a_row - m_last), jnp.exp(m_prev - m_last))
                m_s[mrow:mrow + 1, :] = jnp.broadcast_to(b_row[:, last[d]:last[d] + 1] + m_last, (1, LANES))
            yield
        for d in range(2):
            rows = pl.ds(offs[d], CHUNK)
            ht_s = (hft_s, hbt_s)[d]
            for p in range(N_PAIRS):
                u = d * N_PAIRS + p
                cols = slice(p * LANES, (p + 1) * LANES)
                (w_a, inv_a, e_a, dec_a), (w_b, inv_b, e_b, dec_b) = stats[(u, 0)], stats[(u, 1)]
                kp = km_s[rows, cols]
                num2 = _dot(vmt_s[cols, rows], st_s[u])
                num = jnp.where(top, num2[:, half[0]], num2[:, half[1]])
                ht_s[cols, rows] = (num + jnp.where(top, w_a, w_b) * qc_s[u]) * jnp.where(top, inv_a, inv_b)
                vt_e = (vmtf_s[cols, rows] * jnp.where(top, e_a, e_b)).astype(BF16)
                c_s[u] = jnp.where(top, dec_a, dec_b) * c_s[u] + jnp.where(same_head, _dot(vt_e, kp), 0.0)
                e2 = jnp.concatenate([e_a, e_b, jnp.zeros((6, CHUNK), F32)], axis=0).astype(BF16)
                n_k = _dot(e2, kp)
                n_s[u:u + 1, :] = (jnp.where(left, dec_a, dec_b) * n_s[u:u + 1, :]
                                   + jnp.where(left, n_k[0:1, :], n_k[1:2, :]))
            yield

    attend = attention()
    yield from _each(lambda i: _alternate(attend(i), mlstm(i)), n_blocks, static, unroll=2)

    if not latent:
        for d, c_ref in enumerate((cf_ref, cb_ref)):
            for p in range(N_PAIRS):
                c_pair = jnp.transpose(c_s[d * N_PAIRS + p])
                c_ref[0, 2 * p] = c_pair[0:HEAD_DIM, 0:HEAD_DIM]
                c_ref[0, 2 * p + 1] = c_pair[HEAD_DIM:LANES, HEAD_DIM:LANES]
        no_ref[...] = n_s[...]
        mo_ref[...] = m_s[...]
        yield

    top_w = lax.broadcasted_iota(jnp.int32, (LANES, ROW_CHUNK), 0) < HEAD_DIM

    def finish(rc):
        r0 = _aligned(rc, ROW_CHUNK)
        rows = pl.ds(r0, ROW_CHUNK)
        for p in range(N_PAIRS):
            cols = slice(p * LANES, (p + 1) * LANES)
            hm = hft_s[cols, rows] + hbt_s[cols, rows]
            sq = hm * hm
            ms_a = jnp.sum(sq[0:HEAD_DIM], axis=0, keepdims=True)
            ms_b = jnp.sum(sq[HEAD_DIM:LANES], axis=0, keepdims=True)
            ms = jnp.where(top_w, ms_a, ms_b) * (1.0 / HEAD_DIM)
            y = jnp.transpose(hm * lax.rsqrt(ms + EPS)) * hn_ref[:, cols] * jax.nn.sigmoid(om_s[rows, cols])
            z_s[rows, D_HALF + p * LANES:D_HALF + (p + 1) * LANES] = y.astype(BF16)
            if p % 2:
                yield
        o = _dot(z_s[rows, :], wout_ref[...])
        o_ref[rows, :] = x_ref[rows, :] + gate1 * _rms(o, ng_ref[1:2, :])
        yield

    yield from _each(finish, t // ROW_CHUNK, static)


def _att_keys(lc, latent):
    return lc + 3 * CHUNK if latent else lc


def _att_group(latent):
    return 2 if latent else N_PAIRS


def _odd_scratch(t, lc, latent, nseq):
    pad = 2 * CHUNK if latent else 0
    att = (_att_group(latent), _att_keys(lc, latent), 2 * CHUNK)
    shapes = [
        pltpu.VMEM((t, D_HALF), BF16),
        pltpu.VMEM((t, D_HALF), BF16),
        pltpu.VMEM((N_KV, t + pad, LANES), BF16),
        pltpu.VMEM((N_KV, LANES, t + pad), BF16),
        pltpu.VMEM((t, D_HALF), BF16),
        pltpu.VMEM((t, D_HALF), BF16),
        pltpu.VMEM((t, D_HALF), BF16),
        pltpu.VMEM((t, D_HALF), BF16),
        pltpu.VMEM((D_HALF, t), BF16),
        pltpu.VMEM((D_HALF, t), F32),
        pltpu.VMEM((t, D_HALF), F32),
        pltpu.VMEM((t, LANES), F32),
        pltpu.VMEM((t, D_MODEL), BF16),
        pltpu.VMEM((D_HALF, t), F32),
        pltpu.VMEM((D_HALF, t), F32),
        pltpu.VMEM((2 * N_PAIRS, CHUNK, 2 * LANES), F32),
        pltpu.VMEM((2 * N_PAIRS, CHUNK, 2 * LANES), BF16),
        pltpu.VMEM((2 * N_PAIRS, LANES, CHUNK), F32),
        pltpu.VMEM((2 * N_PAIRS, LANES, LANES), F32),
        pltpu.VMEM((2 * N_PAIRS, LANES), F32),
        pltpu.VMEM((2 * N_HEADS, LANES), F32),
        pltpu.VMEM(att, F32),
        pltpu.VMEM(att, BF16),
    ]
    if latent:
        shapes += [pltpu.VMEM((N_KV, lc, LANES), BF16), pltpu.VMEM((N_KV, LANES, lc), BF16)]
    return [pltpu.VMEM((nseq,) + tuple(sh.shape), sh.dtype) for sh in shapes]


def _odd_common_specs(t, layer, cond_base, cond_stride, nseq):
    assert cond_stride == 0 or nseq == 1
    j = layer // 2
    return [
        _per_seq((t, D_MODEL), nseq),
        _mod_spec(layer, lambda i: cond_base + cond_stride * i),
        _layer_spec((4, D_MODEL), layer),
        _layer_spec((D_MODEL, D_IN_ODD), j),
        _const_spec((D_MODEL, LANES)),
        _const_spec((1, LANES)),
        _const_spec((N_HEADS, LANES)),
        _layer_spec((1, D_HALF), j),
        _layer_spec((D_MODEL, D_MODEL), j),
    ]


def _per_seq(shape, nseq):
    return pl.BlockSpec((nseq,) + shape, lambda i: (i,) + (0,) * len(shape))


def _odd_context(x, modv, layer, ng, w_main, w_gate, gate_bias, sink_b, hnorm, wout):
    b, t, _ = x.shape
    nseq = ODD_CTX_SEQS
    assert b % nseq == 0
    kern = functools.partial(_odd_kernel, t=t, lc=t, latent=False, nseq=nseq)
    per_seq = functools.partial(_per_seq, nseq=nseq)
    return pl.pallas_call(
        kern,
        grid=(b // nseq,),
        in_specs=_odd_common_specs(t, layer, 0, 0, nseq),
        out_specs=[per_seq((t, D_MODEL)),
                   per_seq((1, N_KV, HEAD_DIM, t)), per_seq((1, N_KV, HEAD_DIM, t)),
                   per_seq((1, N_HEADS, HEAD_DIM, HEAD_DIM)), per_seq((1, N_HEADS, HEAD_DIM, HEAD_DIM)),
                   per_seq((2 * N_PAIRS, LANES)), per_seq((2 * N_HEADS, LANES))],
        out_shape=[jax.ShapeDtypeStruct((b, t, D_MODEL), F32),
                   jax.ShapeDtypeStruct((b, 1, N_KV, HEAD_DIM, t), F32),
                   jax.ShapeDtypeStruct((b, 1, N_KV, HEAD_DIM, t), F32),
                   jax.ShapeDtypeStruct((b, 1, N_HEADS, HEAD_DIM, HEAD_DIM), F32),
                   jax.ShapeDtypeStruct((b, 1, N_HEADS, HEAD_DIM, HEAD_DIM), F32),
                   jax.ShapeDtypeStruct((b, 2 * N_PAIRS, LANES), F32),
                   jax.ShapeDtypeStruct((b, 2 * N_HEADS, LANES), F32)],
        scratch_shapes=_odd_scratch(t, t, False, nseq),
        compiler_params=_params(1),
        name="odd_mixer_context",
    )(x, modv, ng, w_main, w_gate, gate_bias, sink_b, hnorm, wout)


def _odd_latent(x, modv, layer, ng, w_main, w_gate, gate_bias, sink_b, hnorm, wout, cos_t, sin_t, kc, vc,
                c_in, n_in, m_in):
    b, t, _ = x.shape
    lc = kc.shape[2]
    kern = functools.partial(_odd_kernel, t=t, lc=lc, latent=True, nseq=1)
    per_seq = functools.partial(_per_seq, nseq=1)
    return pl.pallas_call(
        kern,
        grid=(b,),
        in_specs=_odd_common_specs(t, layer, 1, 1, 1) + [
            _const_spec((t, LANES)), _const_spec((t, LANES)),
            per_seq((N_KV, lc, LANES)), per_seq((N_KV, lc, LANES)),
            per_seq((2 * N_PAIRS, LANES, LANES)), per_seq((2 * N_PAIRS, LANES)), per_seq((2 * N_HEADS, LANES)),
        ],
        out_specs=per_seq((t, D_MODEL)),
        out_shape=jax.ShapeDtypeStruct((b, t, D_MODEL), F32),
        scratch_shapes=_odd_scratch(t, lc, True, 1),
        compiler_params=_params(1),
        name="odd_mixer_latent",
    )(x, modv, ng, w_main, w_gate, gate_bias, sink_b, hnorm, wout, cos_t, sin_t, kc, vc, c_in, n_in, m_in)


def _rope_tables(t):
    rows = t // GRID_W
    row = jnp.broadcast_to(jnp.arange(rows)[:, None], (rows, GRID_W)).reshape(t).astype(F32)
    col = jnp.broadcast_to(jnp.arange(GRID_W)[None, :], (rows, GRID_W)).reshape(t).astype(F32)
    n_freq = HEAD_DIM // 4
    inv_freq = ROPE_BASE ** (-jnp.arange(n_freq, dtype=F32) / n_freq)
    ang = jnp.concatenate([row[:, None] * inv_freq, col[:, None] * inv_freq], axis=-1)
    cos, sin = jnp.cos(ang), jnp.sin(ang)
    cos_l = jnp.tile(cos, (1, LANES // cos.shape[1]))
    sin_l = jnp.tile(jnp.concatenate([-sin, sin], axis=-1), (1, LANES // HEAD_DIM))
    return cos_l, sin_l


def _pair_blockdiag(c):
    b = c.shape[0]
    c = c.reshape(b, N_PAIRS, 2, HEAD_DIM, HEAD_DIM)
    z = jnp.zeros_like(c[:, :, 0])
    top = jnp.concatenate([c[:, :, 0], z], axis=-1)
    bot = jnp.concatenate([z, c[:, :, 1]], axis=-1)
    return jnp.concatenate([top, bot], axis=-2)


def _lane_bcast(v):
    return jnp.broadcast_to(v[..., None], v.shape + (LANES,))


def kernel(x_prompt, x_sample, c, cache_k, cache_v, state_c_fwd, state_n_fwd, state_m_fwd, state_c_bwd, state_n_bwd, state_m_bwd, c_ctx, mod_w, mod_b, norm_g, mlp_w1, mlp_w2, even_in_w, conv_a_w, conv_a_b, ln_a_g, ln_a_b, conv_b_w, even_out_w, odd_in_w, attn_sink, gate_b, hnorm_g, odd_out_w):
    n_dec = x_sample.shape[0]
    n_ctx = x_prompt.shape[0]
    cond = jnp.concatenate([c_ctx[None, :], c, jnp.zeros((COND_ROWS - 1 - n_dec, D_MODEL), F32)], axis=0)
    modv = _modulation(cond, mod_w, mod_b)

    yp, ys = x_prompt, x_sample
    w1, w2 = mlp_w1.astype(BF16), mlp_w2.astype(BF16)

    ev = (norm_g, even_in_w.astype(BF16), conv_a_w, conv_a_b[:, None, :], ln_a_g[:, None, :], ln_a_b[:, None, :],
          conv_b_w, even_out_w.astype(BF16))
    yp = _even_layer(yp, modv, 0, 0, 0, *ev)
    ys = _even_layer(ys, modv, 0, 1, 1, *ev)
    yp = _mlp_layer(yp, modv, 0, 0, 0, norm_g, w1, w2)
    ys = _mlp_layer(ys, modv, 0, 1, 1, norm_g, w1, w2)

    order = jnp.array([0, 2, 1, 3])
    d_main = D_IN_ODD - 4 * N_HEADS
    wg = odd_in_w[0][:, d_main:].reshape(D_MODEL, 4, N_HEADS)[:, order, :].reshape(D_MODEL, 4 * N_HEADS)
    w_gate = jnp.pad(wg, ((0, 0), (0, LANES - 4 * N_HEADS))).astype(BF16)
    gate_bias = jnp.pad(gate_b[0][order, :].reshape(1, 4 * N_HEADS), ((0, 0), (0, LANES - 4 * N_HEADS)))
    sink_b = _lane_bcast(attn_sink[0])
    odd = (1, norm_g, odd_in_w.astype(BF16), w_gate, gate_bias, sink_b, hnorm_g[:, None, :], odd_out_w.astype(BF16))

    op, k_t, v_t, c_f, c_b, n_new, m_new = _odd_context(yp, modv, *odd)
    new_k, new_v = jnp.swapaxes(k_t, -1, -2), jnp.swapaxes(v_t, -1, -2)

    t_dec = x_sample.shape[1]
    cos_t, sin_t = _rope_tables(t_dec)
    kc = jnp.concatenate([cache_k[:, 0], cache_k[:, 0]], axis=-1)
    vc = jnp.concatenate([cache_v[:, 0], cache_v[:, 0]], axis=-1)
    c_in = jnp.concatenate([_pair_blockdiag(jnp.swapaxes(state_c_fwd[:, 0], -1, -2)),
                            _pair_blockdiag(jnp.swapaxes(state_c_bwd[:, 0], -1, -2))], axis=1)
    n_in = jnp.concatenate([state_n_fwd[:, 0].reshape(n_dec, N_PAIRS, LANES),
                            state_n_bwd[:, 0].reshape(n_dec, N_PAIRS, LANES)], axis=1)
    m_in = _lane_bcast(jnp.concatenate([state_m_fwd[:, 0], state_m_bwd[:, 0]], axis=1))
    os_ = _odd_latent(ys, modv, *odd, cos_t, sin_t, kc, vc, c_in, n_in, m_in)

    yp = _mlp_layer(op, modv, 1, 0, 0, norm_g, w1, w2)
    ys = _mlp_layer(os_, modv, 1, 1, 1, norm_g, w1, w2)

    n_f = n_new[:, :N_PAIRS].reshape(n_ctx, N_HEADS, HEAD_DIM)[:, None]
    n_b = n_new[:, N_PAIRS:].reshape(n_ctx, N_HEADS, HEAD_DIM)[:, None]
    m_f = m_new[:, :N_HEADS, 0][:, None]
    m_b = m_new[:, N_HEADS:, 0][:, None]
    return (yp, ys, new_k, new_v, c_f, n_f, m_f, c_b, n_b, m_b)
```

```python
import functools

import jax
import jax.numpy as jnp
from jax import lax
from jax.experimental import pallas as pl
from jax.experimental.pallas import tpu as pltpu

F32 = jnp.float32
BF16 = jnp.bfloat16

D_MODEL = 1024
D_FF = 4 * D_MODEL
EPS = 1e-6
D_HALF = D_MODEL // 2
CONF_WIDTH = 31
CONF_HALO = 16
HEAD_DIM = 64
N_HEADS = 8
N_PAIRS = N_HEADS // 2
N_KV = 2
LANES = 128
CHUNK = 128
WINDOW = 128
GRID_W = 64
ROPE_BASE = 10000.0
ATT_SCALE = HEAD_DIM ** -0.5
D_IN_ODD = D_HALF + 2 * N_KV * HEAD_DIM + 4 * D_HALF + 4 * N_HEADS
ROW_CHUNK = 256
MLP_ROWS = 512
ODD_CTX_SEQS = 2
ODD_STAGGER = 6
EVEN_CHUNKS = 2
EVEN_STAGGER = 3
COND_ROWS = 8
VMEM_LIMIT = 56 * 1024 * 1024


def _dot(a, b):
    return jnp.dot(a, b, preferred_element_type=F32)


def _dot_nt(a, b):
    return lax.dot_general(a, b, (((1,), (1,)), ((), ())), preferred_element_type=F32)


def _split3(x):
    hi = x.astype(BF16)
    r1 = x - hi.astype(F32)
    mid = r1.astype(BF16)
    lo = (r1 - mid.astype(F32)).astype(BF16)
    return hi, mid, lo


def _dot_exact_lhs(x, b01):
    hi, mid, lo = _split3(x)
    return _dot(hi, b01) + _dot(mid, b01) + _dot(lo, b01)


def _rms(x, g):
    return x * lax.rsqrt(jnp.mean(x * x, axis=-1, keepdims=True) + EPS) * g


def _norm_mod(x, g, shift, scale):
    return _rms(x, g) * (1.0 + scale) + shift


def _params(n_grid):
    return pltpu.CompilerParams(dimension_semantics=("arbitrary",) * n_grid, vmem_limit_bytes=VMEM_LIMIT)


def _const_spec(shape):
    zeros = (0,) * len(shape)
    return pl.BlockSpec(shape, lambda *_: zeros, pipeline_mode=pl.Buffered(1))


def _layer_spec(shape, layer):
    index = (layer,) + (0,) * len(shape)
    return pl.BlockSpec((None,) + shape, lambda *_: index, pipeline_mode=pl.Buffered(1))


def _mod_spec(layer, cond_of):
    return pl.BlockSpec((None, None, 6, D_MODEL), lambda *idx: (layer, cond_of(*idx), 0, 0))


def _mod_kernel(cond_ref, w_ref, b_ref, o_ref):
    s = jax.nn.silu(cond_ref[...]).astype(BF16)
    o_ref[...] = _dot(s, w_ref[...].astype(BF16)) + b_ref[...]


def _modulation(cond, mod_w, mod_b):
    depth = mod_w.shape[0]
    n_out = mod_w.shape[2]
    tn = D_MODEL
    out = pl.pallas_call(
        _mod_kernel,
        grid=(depth, n_out // tn),
        in_specs=[
            pl.BlockSpec((COND_ROWS, D_MODEL), lambda l, j: (0, 0)),
            pl.BlockSpec((None, D_MODEL, tn), lambda l, j: (l, 0, j)),
            pl.BlockSpec((None, 1, tn), lambda l, j: (l, 0, j)),
        ],
        out_specs=pl.BlockSpec((None, COND_ROWS, tn), lambda l, j: (l, 0, j)),
        out_shape=jax.ShapeDtypeStruct((depth, COND_ROWS, n_out), F32),
        compiler_params=_params(2),
        name="modulation",
    )(cond, mod_w, mod_b.reshape(depth, 1, n_out))
    return out.reshape(depth, COND_ROWS, 6, D_MODEL)


def _run_staggered(programs, stagger):
    programs = list(programs)
    live, rounds = [], 0
    while programs or live:
        if programs and rounds % stagger == 0:
            live.append(programs.pop(0))
        for g in list(live):
            try:
                next(g)
            except StopIteration:
                live.remove(g)
        rounds += 1


def _even_kernel(*refs, n_chunks, nsub):
    halos, (xc_ref, mod_ref, ng_ref, win_ref, caw_ref, cab_ref, lng_ref, lnb_ref, cbw_ref, wout_ref, o_ref,
            apad, cpad, bgs, zs) = refs[:2 * nsub], refs[2 * nsub:]
    shared = (mod_ref, ng_ref, win_ref, caw_ref, cab_ref, lng_ref, lnb_ref, cbw_ref, wout_ref)
    _run_staggered(
        (_even_chunk(pl.program_id(0) * nsub + k, halos[2 * k], xc_ref.at[k], halos[2 * k + 1], *shared,
                     o_ref.at[k], apad.at[k], cpad.at[k], bgs.at[k], zs.at[k], n_chunks=n_chunks)
         for k in range(nsub)), EVEN_STAGGER)


def _even_chunk(g, xp_ref, xc_ref, xn_ref, mod_ref, ng_ref, win_ref, caw_ref, cab_ref, lng_ref, lnb_ref,
                cbw_ref, wout_ref, o_ref, apad, cpad, bgs, zs, *, n_chunks):
    c = g % n_chunks if n_chunks > 1 else 0
    rows = ROW_CHUNK + 2 * CONF_HALO
    shift1, scale1, gate1 = mod_ref[0:1, :], mod_ref[1:2, :], mod_ref[2:3, :]
    xh = jnp.concatenate([xp_ref[...], xc_ref[...], xn_ref[...]], axis=0)
    h = _norm_mod(xh, ng_ref[0:1, :], shift1, scale1).astype(BF16)
    ri = lax.broadcasted_iota(jnp.int32, (rows, D_HALF), 0)
    lo = jnp.where(c == 0, CONF_HALO, 0)
    hi = jnp.where(c == n_chunks - 1, CONF_HALO + ROW_CHUNK, rows)
    inside = (ri >= lo) & (ri < hi)
    a = _dot(h, win_ref[:, 0:D_HALF]) * jax.nn.sigmoid(_dot(h, win_ref[:, D_HALF:2 * D_HALF]))
    apad[...] = jnp.where(inside, a, 0.0)
    yield
    cx = _dot(h, win_ref[:, 3 * D_HALF:4 * D_HALF]) * _dot(h, win_ref[:, 4 * D_HALF:5 * D_HALF])
    cpad[...] = jnp.where(inside, cx, 0.0)
    yield
    bgs[...] = _dot(h[CONF_HALO:CONF_HALO + ROW_CHUNK], win_ref[:, 2 * D_HALF:3 * D_HALF])
    yield

    sub = 64
    tile = 8
    for j in range(ROW_CHUNK // sub):
        r0 = j * sub
        groups = []
        for cg in range(D_HALF // LANES):
            cols = slice(cg * LANES, (cg + 1) * LANES)
            acc = None
            for r in range(tile):
                part = None
                for m in range(-(-(CONF_WIDTH + 1) // tile)):
                    o = tile * m + r
                    if 1 <= o <= CONF_WIDTH:
                        term = caw_ref[o - 1:o, cols] * apad[r0 + tile * m:r0 + tile * m + sub + tile, cols]
                        part = term if part is None else part + term
                shifted = part[r:r + sub, :]
                acc = shifted if acc is None else acc + shifted
            groups.append(acc)
        acc = jnp.concatenate(groups, axis=1) + cab_ref[...]
        mu = jnp.mean(acc, axis=-1, keepdims=True)
        dlt = acc - mu
        var = jnp.mean(dlt * dlt, axis=-1, keepdims=True)
        a_out = jax.nn.silu(dlt * lax.rsqrt(var + EPS) * lng_ref[...] + lnb_ref[...])
        zs[r0:r0 + sub, 0:D_HALF] = a_out.astype(BF16)
        base = r0 + CONF_HALO - 1
        sc = (cbw_ref[0:1, :] * cpad[base:base + sub, :]
              + cbw_ref[1:2, :] * cpad[base + 1:base + 1 + sub, :]
              + cbw_ref[2:3, :] * cpad[base + 2:base + 2 + sub, :])
        zs[r0:r0 + sub, D_HALF:D_MODEL] = (bgs[r0:r0 + sub, :] * sc).astype(BF16)
        yield

    o = _dot(zs[...], wout_ref[...])
    o_ref[...] = xc_ref[...] + gate1 * _rms(o, ng_ref[1:2, :])
    yield


def _even_layer(x, modv, layer, cond_base, cond_stride, ng, win, caw, cab, lng, lnb, cbw, wout):
    j = layer // 2
    b, t, _ = x.shape
    n_chunks = t // ROW_CHUNK
    nsub = EVEN_CHUNKS
    assert (b * n_chunks) % nsub == 0 and (cond_stride == 0 or n_chunks % nsub == 0)
    hpc = ROW_CHUNK // CONF_HALO
    n_halo_blocks = b * t // CONF_HALO
    rows = ROW_CHUNK + 2 * CONF_HALO
    halo_specs = []
    for k in range(nsub):
        halo_specs += [
            pl.BlockSpec((None, CONF_HALO, D_MODEL),
                         lambda i, k=k: (jnp.maximum((i * nsub + k) * hpc - 1, 0), 0, 0)),
            pl.BlockSpec((None, CONF_HALO, D_MODEL),
                         lambda i, k=k: (jnp.minimum((i * nsub + k + 1) * hpc, n_halo_blocks - 1), 0, 0)),
        ]
    x_halo = x.reshape(n_halo_blocks, CONF_HALO, D_MODEL)
    x_chunks = x.reshape(b * n_chunks, ROW_CHUNK, D_MODEL)
    kern = functools.partial(_even_kernel, n_chunks=n_chunks, nsub=nsub)
    out = pl.pallas_call(
        kern,
        grid=(b * n_chunks // nsub,),
        in_specs=halo_specs + [
            pl.BlockSpec((nsub, ROW_CHUNK, D_MODEL), lambda i: (i, 0, 0)),
            _mod_spec(layer, lambda i: cond_base + cond_stride * ((i * nsub) // n_chunks)),
            _layer_spec((4, D_MODEL), layer),
            _layer_spec((D_MODEL, 5 * D_HALF), j),
            _layer_spec((CONF_WIDTH, D_HALF), j),
            _layer_spec((1, D_HALF), j),
            _layer_spec((1, D_HALF), j),
            _layer_spec((1, D_HALF), j),
            _layer_spec((3, D_HALF), j),
            _layer_spec((D_MODEL, D_MODEL), j),
        ],
        out_specs=pl.BlockSpec((nsub, ROW_CHUNK, D_MODEL), lambda i: (i, 0, 0)),
        out_shape=jax.ShapeDtypeStruct(x_chunks.shape, F32),
        scratch_shapes=[
            pltpu.VMEM((nsub, rows, D_HALF), F32),
            pltpu.VMEM((nsub, rows, D_HALF), F32),
            pltpu.VMEM((nsub, ROW_CHUNK, D_HALF), F32),
            pltpu.VMEM((nsub, ROW_CHUNK, D_MODEL), BF16),
        ],
        compiler_params=_params(1),
        name="even_mixer",
    )(*([x_halo] * (2 * nsub)), x_chunks, modv, ng, win, caw, cab, lng, lnb, cbw, wout)
    return out.reshape(b, t, D_MODEL)


def _mlp_kernel(x_ref, mod_ref, ng_ref, w1_hbm, w2_hbm, o_ref, w1_s, w2_s, stage, sem, *, layer):
    n_blocks = D_FF // D_MODEL

    def block_copy(k):
        c, slot = k // 2, k % 2
        span = pl.ds(c * D_MODEL, D_MODEL)
        src = w1_hbm.at[layer, :, span] if k % 2 == 0 else w2_hbm.at[layer, span, :]
        return pltpu.make_async_copy(src, stage.at[slot], sem.at[slot])

    def fetch(k, dst):
        block_copy(k).wait()
        dst[...] = stage[k % 2].astype(BF16)
        if k + 2 < 2 * n_blocks:
            block_copy(k + 2).start()

    def tile(load):
        x = x_ref[...]
        h = _norm_mod(x, ng_ref[2:3, :], mod_ref[3:4, :], mod_ref[4:5, :]).astype(BF16)
        acc = None
        for c in range(n_blocks):
            cols = slice(c * D_MODEL, (c + 1) * D_MODEL)
            if load:
                fetch(2 * c, w1_s.at[:, cols])
            hid = jnp.square(jnp.maximum(_dot(h, w1_s[:, cols]), 0.0)).astype(BF16)
            if load:
                fetch(2 * c + 1, w2_s.at[cols, :])
            part = _dot(hid, w2_s[cols, :])
            acc = part if acc is None else acc + part
        o_ref[...] = x + mod_ref[5:6, :] * _rms(acc, ng_ref[3:4, :])

    first = pl.program_id(0) == 0

    @pl.when(first)
    def _():
        block_copy(0).start()
        block_copy(1).start()
        tile(load=True)

    @pl.when(jnp.logical_not(first))
    def _():
        tile(load=False)


def _mlp_layer(x, modv, layer, cond_base, cond_stride, ng, w1, w2):
    b, t, _ = x.shape
    tm = MLP_ROWS
    assert (b * t) % tm == 0 and (cond_stride == 0 or t % tm == 0)
    tiles_per_seq = max(t // tm, 1)
    x2 = x.reshape(b * t, D_MODEL)
    out = pl.pallas_call(
        functools.partial(_mlp_kernel, layer=layer),
        grid=(b * t // tm,),
        in_specs=[
            pl.BlockSpec((tm, D_MODEL), lambda i: (i, 0)),
            _mod_spec(layer, lambda i: cond_base + cond_stride * (i // tiles_per_seq)),
            _layer_spec((4, D_MODEL), layer),
            pl.BlockSpec(memory_space=pl.ANY),
            pl.BlockSpec(memory_space=pl.ANY),
        ],
        out_specs=pl.BlockSpec((tm, D_MODEL), lambda i: (i, 0)),
        out_shape=jax.ShapeDtypeStruct(x2.shape, F32),
        scratch_shapes=[
            pltpu.VMEM((D_MODEL, D_FF), BF16),
            pltpu.VMEM((D_FF, D_MODEL), BF16),
            pltpu.VMEM((2, D_MODEL, D_MODEL), F32),
            pltpu.SemaphoreType.DMA((2,)),
        ],
        compiler_params=_params(1),
        name="mlp",
    )(x2, modv, ng, w1, w2)
    return out.reshape(b, t, D_MODEL)


def _log_sigmoid(x):
    return jnp.minimum(x, 0.0) - jnp.log(1.0 + jnp.exp(-jnp.abs(x)))


def _rot_half(x, first_half):
    return jnp.where(first_half, pltpu.roll(x, 96, axis=1), pltpu.roll(x, 32, axis=1))


def _values_and_ones(v):
    vt = jnp.transpose(v)
    row = lax.broadcasted_iota(jnp.int32, vt.shape, 0)
    return jnp.where(row < HEAD_DIM, vt, 1.0)


def _aligned(i, m):
    return i * m if isinstance(i, int) else pl.multiple_of(i * m, m)


def _drain(pieces):
    for _ in pieces:
        pass


def _each(body, n, static, unroll=1):
    if static:
        for i in range(n):
            yield from body(i)
    else:
        lax.fori_loop(0, n, lambda i, c: (_drain(body(i)), c)[1], 0, unroll=unroll)


def _alternate(*programs):
    live = list(programs)
    while live:
        for g in list(live):
            try:
                next(g)
            except StopIteration:
                live.remove(g)
            else:
                yield


def _odd_kernel(*refs, t, lc, latent, nseq):
    shared = set(range(1, 11 if latent else 9))
    _run_staggered((_odd_seq(*[r if k in shared else r.at[sq] for k, r in enumerate(refs)],
                             t=t, lc=lc, latent=latent) for sq in range(nseq)), ODD_STAGGER)


def _odd_seq(*refs, t, lc, latent):
    if latent:
        (x_ref, mod_ref, ng_ref, win_ref, wg_ref, gb_ref, sink_ref, hn_ref, wout_ref,
         cos_ref, sin_ref, kc_ref, vc_ref, cin_ref, nin_ref, min_ref,
         o_ref,
         qa_s, qb_s, kk_s, vvt_s, qma_s, qmb_s, qm_s, km_s, vmt_s, vmtf_s, om_s, g_s, z_s, hft_s, hbt_s,
         sr_s, st_s, qc_s, c_s, n_s, m_s, s_s, p_s, kc_s, vct_s) = refs
    else:
        (x_ref, mod_ref, ng_ref, win_ref, wg_ref, gb_ref, sink_ref, hn_ref, wout_ref,
         o_ref, ko_ref, vo_ref, cf_ref, cb_ref, no_ref, mo_ref,
         qa_s, qb_s, kk_s, vvt_s, qma_s, qmb_s, qm_s, km_s, vmt_s, vmtf_s, om_s, g_s, z_s, hft_s, hbt_s,
         sr_s, st_s, qc_s, c_s, n_s, m_s, s_s, p_s) = refs

    static = not latent
    n_blocks = t // CHUNK
    pad = CHUNK if latent else 0
    shift1, scale1, gate1 = mod_ref[0:1, :], mod_ref[1:2, :], mod_ref[2:3, :]
    lane = lax.broadcasted_iota(jnp.int32, (1, LANES), 1)
    left = lane < HEAD_DIM
    first_half = (lane % HEAD_DIM) < (HEAD_DIM // 2)
    ti = lax.broadcasted_iota(jnp.int32, (CHUNK, CHUNK), 0)
    si = lax.broadcasted_iota(jnp.int32, (CHUNK, CHUNK), 1)
    top = ti < HEAD_DIM
    same_head = top == (si < HEAD_DIM)

    if latent:
        for kv in range(N_KV):
            kk_s[kv, 0:CHUNK, :] = jnp.zeros((CHUNK, LANES), BF16)
            kk_s[kv, CHUNK + t:2 * CHUNK + t, :] = jnp.zeros((CHUNK, LANES), BF16)
            vvt_s[kv, :, 0:CHUNK] = jnp.zeros((LANES, CHUNK), BF16)
            vvt_s[kv, :, CHUNK + t:2 * CHUNK + t] = jnp.zeros((LANES, CHUNK), BF16)
            kc_s[kv] = kc_ref[kv].astype(BF16)
            vct_s[kv] = _values_and_ones(vc_ref[kv]).astype(BF16)

    def project(rc):
        r0 = _aligned(rc, ROW_CHUNK)
        rows = pl.ds(r0, ROW_CHUNK)
        krows = pl.ds(r0 + pad, ROW_CHUNK)
        h = _norm_mod(x_ref[rows, :], ng_ref[0:1, :], shift1, scale1).astype(BF16)
        if latent:
            cs, sn = cos_ref[rows, :], sin_ref[rows, :]
        q = _dot(h, win_ref[:, 0:D_HALF])
        for p in range(N_PAIRS):
            cols = slice(p * LANES, (p + 1) * LANES)
            qp = q[:, cols]
            if latent:
                qp = qp * cs + _rot_half(qp, first_half) * sn
            qp = qp * ATT_SCALE
            qa_s[rows, cols] = jnp.where(left, qp, 0.0).astype(BF16)
            qb_s[rows, cols] = jnp.where(left, 0.0, qp).astype(BF16)
        yield
        kv2 = _dot(h, win_ref[:, D_HALF:D_HALF + 2 * LANES])
        ka, va = kv2[:, 0:LANES], kv2[:, LANES:2 * LANES]
        if latent:
            ka = ka * cs + _rot_half(ka, first_half) * sn
        kr = pltpu.roll(ka, HEAD_DIM, axis=1)
        vr = pltpu.roll(va, HEAD_DIM, axis=1)
        if not latent:
            ka_t, va_t = jnp.transpose(ka), jnp.transpose(va)
            for kv in range(N_KV):
                ko_ref[0, kv, :, rows] = ka_t[kv * HEAD_DIM:(kv + 1) * HEAD_DIM, :]
                vo_ref[0, kv, :, rows] = va_t[kv * HEAD_DIM:(kv + 1) * HEAD_DIM, :]
        kk_s[0, krows, :] = jnp.where(left, ka, kr).astype(BF16)
        kk_s[1, krows, :] = jnp.where(left, kr, ka).astype(BF16)
        vvt_s[0, :, krows] = _values_and_ones(va).astype(BF16)
        vvt_s[1, :, krows] = _values_and_ones(vr).astype(BF16)
        yield
        base = D_HALF + 2 * LANES
        qm = _dot(h, win_ref[:, base:base + D_HALF])
        qm_s[rows, :] = qm.astype(BF16)
        for p in range(N_PAIRS):
            cols = slice(p * LANES, (p + 1) * LANES)
            qma_s[rows, cols] = jnp.where(left, qm[:, cols], 0.0).astype(BF16)
            qmb_s[rows, cols] = jnp.where(left, 0.0, qm[:, cols]).astype(BF16)
        yield
        vm = _dot(h, win_ref[:, base + 2 * D_HALF:base + 3 * D_HALF])
        for p in range(N_PAIRS):
            cols = slice(p * LANES, (p + 1) * LANES)
            vt = jnp.transpose(vm[:, cols])
            vmtf_s[cols, rows] = vt
            vmt_s[cols, rows] = vt.astype(BF16)
        yield
        km = _dot(h, win_ref[:, base + D_HALF:base + 2 * D_HALF]) * (HEAD_DIM ** -0.5)
        km_s[rows, :] = km.astype(BF16)
        yield
        om_s[rows, :] = _dot(h, win_ref[:, base + 3 * D_HALF:base + 4 * D_HALF])
        g_s[rows, :] = _dot(h, wg_ref[...]) + gb_ref[...]
        yield

    yield from _each(project, t // ROW_CHUNK, static)

    kj = lax.broadcasted_iota(jnp.int32, (3 * CHUNK, 2 * CHUNK), 0)
    qi = lax.broadcasted_iota(jnp.int32, (3 * CHUNK, 2 * CHUNK), 1) % CHUNK
    band_ok = jnp.abs(kj - CHUNK - qi) <= WINDOW
    head_a = lax.broadcasted_iota(jnp.int32, (1, 2 * CHUNK), 1) < CHUNK

    n_keys = _att_keys(lc, latent)
    group = _att_group(latent)

    def attention():
        def attend(i):
            r0 = _aligned(i, CHUNK)
            rows = pl.ds(r0, CHUNK)
            if latent:
                key_pos = kj + (i - 1) * CHUNK
                mask = band_ok & (key_pos >= 0) & (key_pos < t)
                win = pl.ds(r0, 3 * CHUNK)
            for g0 in range(0, N_PAIRS, group):
                for gi in range(group):
                    p = g0 + gi
                    kv = p // (N_PAIRS // N_KV)
                    cols = slice(p * LANES, (p + 1) * LANES)
                    q2 = jnp.concatenate([qa_s[rows, cols], qb_s[rows, cols]], axis=0)
                    if latent:
                        s_s[gi, 0:lc, :] = _dot_nt(kc_s[kv], q2)
                        s_s[gi, lc:n_keys, :] = jnp.where(mask, _dot_nt(kk_s[kv, win, :], q2), -jnp.inf)
                    else:
                        s_s[gi] = _dot_nt(kk_s[kv], q2)
                yield
                maxes = []
                for gi in range(group):
                    p = g0 + gi
                    sink = jnp.where(head_a, sink_ref[2 * p:2 * p + 1, 0:1], sink_ref[2 * p + 1:2 * p + 2, 0:1])
                    mx = jnp.maximum(jnp.max(s_s[gi], axis=0, keepdims=True), sink)
                    p_s[gi] = jnp.exp(s_s[gi] - mx).astype(BF16)
                    maxes.append((sink, mx))
                yield
                for gi in range(group):
                    p = g0 + gi
                    kv = p // (N_PAIRS // N_KV)
                    cols = slice(p * LANES, (p + 1) * LANES)
                    if latent:
                        num = (_dot(vct_s[kv], p_s[gi, 0:lc, :]) + _dot(vvt_s[kv, :, win], p_s[gi, lc:n_keys, :]))
                    else:
                        num = _dot(vvt_s[kv], p_s[gi])
                    sink, mx = maxes[gi]
                    den = num[HEAD_DIM:HEAD_DIM + 1, :] + jnp.exp(sink - mx)
                    out = num[0:HEAD_DIM, :] * (1.0 / den)
                    pair = jnp.concatenate([out[:, 0:CHUNK], out[:, CHUNK:2 * CHUNK]], axis=0)
                    z_s[rows, cols] = jnp.transpose(pair).astype(BF16)
                yield

        return attend

    if latent:
        c_s[...] = cin_ref[...]
        n_s[...] = nin_ref[...]
        m_s[...] = min_ref[...]
    else:
        c_s[...] = jnp.zeros(c_s.shape, F32)
        n_s[...] = jnp.zeros(n_s.shape, F32)
        m_s[...] = jnp.zeros(m_s.shape, F32)

    see = (ti <= si, ti >= si)
    tri = tuple(m.astype(F32).astype(BF16) for m in see)
    last = (CHUNK - 1, 0)

    def mlstm(i):
        offs = (_aligned(i, CHUNK), _aligned(n_blocks - 1 - i, CHUNK))
        half = (slice(0, LANES), slice(LANES, 2 * LANES))
        a_rows, b_rows = [], []
        for d in range(2):
            gt = jnp.transpose(g_s[pl.ds(offs[d], CHUNK), :])
            lf = _log_sigmoid(gt[2 * N_HEADS:4 * N_HEADS, :])
            bcum = _dot_exact_lhs(lf, tri[d])[8 * d:8 * d + 8, :]
            a_rows.append(gt[8 * d:8 * d + 8, :] - bcum)
            b_rows.append(bcum)
        yield
        q_ns = []
        for d in range(2):
            rows = pl.ds(offs[d], CHUNK)
            for p in range(N_PAIRS):
                u = d * N_PAIRS + p
                cols = slice(p * LANES, (p + 1) * LANES)
                q2 = jnp.concatenate([qma_s[rows, cols], qmb_s[rows, cols]], axis=0)
                sr_s[u] = _dot_nt(km_s[rows, cols], q2)
                qc_s[u] = _dot_nt(c_s[u].astype(BF16), qm_s[rows, cols])
                n8 = jnp.broadcast_to(n_s[u:u + 1, :], (8, LANES)).astype(BF16)
                q_ns.append(_dot_nt(n8, q2)[0:1, :])
            yield
        stats = {}
        for d in range(2):
            for hd in range(N_HEADS):
                u, j = d * N_PAIRS + hd // 2, hd % 2
                mrow = d * N_HEADS + hd
                a_row = a_rows[d][hd:hd + 1, :]
                b_row = b_rows[d][hd:hd + 1, :]
                m_prev = m_s[mrow:mrow + 1, 0:1]
                a_col = jnp.transpose(jnp.broadcast_to(a_row, (CHUNK, CHUNK)))
                z_t = jnp.where(see[d], a_col, -jnp.inf)
                m_run = jnp.maximum(jnp.max(z_t, axis=0, keepdims=True), m_prev)
                s_t = sr_s[u, :, half[j]] * jnp.exp(z_t - m_run)
                st_s[u, :, half[j]] = s_t.astype(BF16)
                w_int = jnp.exp(m_prev - m_run)
                den = jnp.sum(s_t, axis=0, keepdims=True) + w_int * q_ns[u][:, half[j]]
                inv = 1.0 / jnp.maximum(jnp.abs(den), jnp.exp(-(b_row + m_run)))
                m_last = m_run[:, last[d]:last[d] + 1]
                stats[(u, j)] = (w_int, inv, jnp.exp(a_row - m_last), jnp.exp(m_prev - m_last))
                m_s[mrow:mrow + 1, :] = jnp.broadcast_to(b_row[:, last[d]:last[d] + 1] + m_last, (1, LANES))
            yield
        for d in range(2):
            rows = pl.ds(offs[d], CHUNK)
            ht_s = (hft_s, hbt_s)[d]
            for p in range(N_PAIRS):
                u = d * N_PAIRS + p
                cols = slice(p * LANES, (p + 1) * LANES)
                (w_a, inv_a, e_a, dec_a), (w_b, inv_b, e_b, dec_b) = stats[(u, 0)], stats[(u, 1)]
                kp = km_s[rows, cols]
                num2 = _dot(vmt_s[cols, rows], st_s[u])
                num = jnp.where(top, num2[:, half[0]], num2[:, half[1]])
                ht_s[cols, rows] = (num + jnp.where(top, w_a, w_b) * qc_s[u]) * jnp.where(top, inv_a, inv_b)
                vt_e = (vmtf_s[cols, rows] * jnp.where(top, e_a, e_b)).astype(BF16)
                c_s[u] = jnp.where(top, dec_a, dec_b) * c_s[u] + jnp.where(same_head, _dot(vt_e, kp), 0.0)
                e2 = jnp.concatenate([e_a, e_b, jnp.zeros((6, CHUNK), F32)], axis=0).astype(BF16)
                n_k = _dot(e2, kp)
                n_s[u:u + 1, :] = (jnp.where(left, dec_a, dec_b) * n_s[u:u + 1, :]
                                   + jnp.where(left, n_k[0:1, :], n_k[1:2, :]))
            yield

    attend = attention()
    yield from _each(lambda i: _alternate(attend(i), mlstm(i)), n_blocks, static, unroll=2)

    if not latent:
        for d, c_ref in enumerate((cf_ref, cb_ref)):
            for p in range(N_PAIRS):
                c_pair = jnp.transpose(c_s[d * N_PAIRS + p])
                c_ref[0, 2 * p] = c_pair[0:HEAD_DIM, 0:HEAD_DIM]
                c_ref[0, 2 * p + 1] = c_pair[HEAD_DIM:LANES, HEAD_DIM:LANES]
        no_ref[...] = n_s[...]
        mo_ref[...] = m_s[...]
        yield

    top_w = lax.broadcasted_iota(jnp.int32, (LANES, ROW_CHUNK), 0) < HEAD_DIM

    def finish(rc):
        r0 = _aligned(rc, ROW_CHUNK)
        rows = pl.ds(r0, ROW_CHUNK)
        for p in range(N_PAIRS):
            cols = slice(p * LANES, (p + 1) * LANES)
            hm = hft_s[cols, rows] + hbt_s[cols, rows]
            sq = hm * hm
            ms_a = jnp.sum(sq[0:HEAD_DIM], axis=0, keepdims=True)
            ms_b = jnp.sum(sq[HEAD_DIM:LANES], axis=0, keepdims=True)
            ms = jnp.where(top_w, ms_a, ms_b) * (1.0 / HEAD_DIM)
            y = jnp.transpose(hm * lax.rsqrt(ms + EPS)) * hn_ref[:, cols] * jax.nn.sigmoid(om_s[rows, cols])
            z_s[rows, D_HALF + p * LANES:D_HALF + (p + 1) * LANES] = y.astype(BF16)
            if p % 2:
                yield
        o = _dot(z_s[rows, :], wout_ref[...])
        o_ref[rows, :] = x_ref[rows, :] + gate1 * _rms(o, ng_ref[1:2, :])
        yield

    yield from _each(finish, t // ROW_CHUNK, static)


def _att_keys(lc, latent):
    return lc + 3 * CHUNK if latent else lc


def _att_group(latent):
    return 2 if latent else N_PAIRS


def _odd_scratch(t, lc, latent, nseq):
    pad = 2 * CHUNK if latent else 0
    att = (_att_group(latent), _att_keys(lc, latent), 2 * CHUNK)
    shapes = [
        pltpu.VMEM((t, D_HALF), BF16),
        pltpu.VMEM((t, D_HALF), BF16),
        pltpu.VMEM((N_KV, t + pad, LANES), BF16),
        pltpu.VMEM((N_KV, LANES, t + pad), BF16),
        pltpu.VMEM((t, D_HALF), BF16),
        pltpu.VMEM((t, D_HALF), BF16),
        pltpu.VMEM((t, D_HALF), BF16),
        pltpu.VMEM((t, D_HALF), BF16),
        pltpu.VMEM((D_HALF, t), BF16),
        pltpu.VMEM((D_HALF, t), F32),
        pltpu.VMEM((t, D_HALF), F32),
        pltpu.VMEM((t, LANES), F32),
        pltpu.VMEM((t, D_MODEL), BF16),
        pltpu.VMEM((D_HALF, t), F32),
        pltpu.VMEM((D_HALF, t), F32),
        pltpu.VMEM((2 * N_PAIRS, CHUNK, 2 * LANES), F32),
        pltpu.VMEM((2 * N_PAIRS, CHUNK, 2 * LANES), BF16),
        pltpu.VMEM((2 * N_PAIRS, LANES, CHUNK), F32),
        pltpu.VMEM((2 * N_PAIRS, LANES, LANES), F32),
        pltpu.VMEM((2 * N_PAIRS, LANES), F32),
        pltpu.VMEM((2 * N_HEADS, LANES), F32),
        pltpu.VMEM(att, F32),
        pltpu.VMEM(att, BF16),
    ]
    if latent:
        shapes += [pltpu.VMEM((N_KV, lc, LANES), BF16), pltpu.VMEM((N_KV, LANES, lc), BF16)]
    return [pltpu.VMEM((nseq,) + tuple(sh.shape), sh.dtype) for sh in shapes]


def _odd_common_specs(t, layer, cond_base, cond_stride, nseq):
    assert cond_stride == 0 or nseq == 1
    j = layer // 2
    return [
        _per_seq((t, D_MODEL), nseq),
        _mod_spec(layer, lambda i: cond_base + cond_stride * i),
        _layer_spec((4, D_MODEL), layer),
        _layer_spec((D_MODEL, D_IN_ODD), j),
        _const_spec((D_MODEL, LANES)),
        _const_spec((1, LANES)),
        _const_spec((N_HEADS, LANES)),
        _layer_spec((1, D_HALF), j),
        _layer_spec((D_MODEL, D_MODEL), j),
    ]


def _per_seq(shape, nseq):
    return pl.BlockSpec((nseq,) + shape, lambda i: (i,) + (0,) * len(shape))


def _odd_context(x, modv, layer, ng, w_main, w_gate, gate_bias, sink_b, hnorm, wout):
    b, t, _ = x.shape
    nseq = ODD_CTX_SEQS
    assert b % nseq == 0
    kern = functools.partial(_odd_kernel, t=t, lc=t, latent=False, nseq=nseq)
    per_seq = functools.partial(_per_seq, nseq=nseq)
    return pl.pallas_call(
        kern,
        grid=(b // nseq,),
        in_specs=_odd_common_specs(t, layer, 0, 0, nseq),
        out_specs=[per_seq((t, D_MODEL)),
                   per_seq((1, N_KV, HEAD_DIM, t)), per_seq((1, N_KV, HEAD_DIM, t)),
                   per_seq((1, N_HEADS, HEAD_DIM, HEAD_DIM)), per_seq((1, N_HEADS, HEAD_DIM, HEAD_DIM)),
                   per_seq((2 * N_PAIRS, LANES)), per_seq((2 * N_HEADS, LANES))],
        out_shape=[jax.ShapeDtypeStruct((b, t, D_MODEL), F32),
                   jax.ShapeDtypeStruct((b, 1, N_KV, HEAD_DIM, t), F32),
                   jax.ShapeDtypeStruct((b, 1, N_KV, HEAD_DIM, t), F32),
                   jax.ShapeDtypeStruct((b, 1, N_HEADS, HEAD_DIM, HEAD_DIM), F32),
                   jax.ShapeDtypeStruct((b, 1, N_HEADS, HEAD_DIM, HEAD_DIM), F32),
                   jax.ShapeDtypeStruct((b, 2 * N_PAIRS, LANES), F32),
                   jax.ShapeDtypeStruct((b, 2 * N_HEADS, LANES), F32)],
        scratch_shapes=_odd_scratch(t, t, False, nseq),
        compiler_params=_params(1),
        name="odd_mixer_context",
    )(x, modv, ng, w_main, w_gate, gate_bias, sink_b, hnorm, wout)


def _odd_latent(x, modv, layer, ng, w_main, w_gate, gate_bias, sink_b, hnorm, wout, cos_t, sin_t, kc, vc,
                c_in, n_in, m_in):
    b, t, _ = x.shape
    lc = kc.shape[2]
    kern = functools.partial(_odd_kernel, t=t, lc=lc, latent=True, nseq=1)
    per_seq = functools.partial(_per_seq, nseq=1)
    return pl.pallas_call(
        kern,
        grid=(b,),
        in_specs=_odd_common_specs(t, layer, 1, 1, 1) + [
            _const_spec((t, LANES)), _const_spec((t, LANES)),
            per_seq((N_KV, lc, LANES)), per_seq((N_KV, lc, LANES)),
            per_seq((2 * N_PAIRS, LANES, LANES)), per_seq((2 * N_PAIRS, LANES)), per_seq((2 * N_HEADS, LANES)),
        ],
        out_specs=per_seq((t, D_MODEL)),
        out_shape=jax.ShapeDtypeStruct((b, t, D_MODEL), F32),
        scratch_shapes=_odd_scratch(t, lc, True, 1),
        compiler_params=_params(1),
        name="odd_mixer_latent",
    )(x, modv, ng, w_main, w_gate, gate_bias, sink_b, hnorm, wout, cos_t, sin_t, kc, vc, c_in, n_in, m_in)


def _rope_tables(t):
    rows = t // GRID_W
    row = jnp.broadcast_to(jnp.arange(rows)[:, None], (rows, GRID_W)).reshape(t).astype(F32)
    col = jnp.broadcast_to(jnp.arange(GRID_W)[None, :], (rows, GRID_W)).reshape(t).astype(F32)
    n_freq = HEAD_DIM // 4
    inv_freq = ROPE_BASE ** (-jnp.arange(n_freq, dtype=F32) / n_freq)
    ang = jnp.concatenate([row[:, None] * inv_freq, col[:, None] * inv_freq], axis=-1)
    cos, sin = jnp.cos(ang), jnp.sin(ang)
    cos_l = jnp.tile(cos, (1, LANES // cos.shape[1]))
    sin_l = jnp.tile(jnp.concatenate([-sin, sin], axis=-1), (1, LANES // HEAD_DIM))
    return cos_l, sin_l


def _pair_blockdiag(c):
    b = c.shape[0]
    c = c.reshape(b, N_PAIRS, 2, HEAD_DIM, HEAD_DIM)
    z = jnp.zeros_like(c[:, :, 0])
    top = jnp.concatenate([c[:, :, 0], z], axis=-1)
    bot = jnp.concatenate([z, c[:, :, 1]], axis=-1)
    return jnp.concatenate([top, bot], axis=-2)


def _lane_bcast(v):
    return jnp.broadcast_to(v[..., None], v.shape + (LANES,))


def kernel(x_prompt, x_sample, c, cache_k, cache_v, state_c_fwd, state_n_fwd, state_m_fwd, state_c_bwd, state_n_bwd, state_m_bwd, c_ctx, mod_w, mod_b, norm_g, mlp_w1, mlp_w2, even_in_w, conv_a_w, conv_a_b, ln_a_g, ln_a_b, conv_b_w, even_out_w, odd_in_w, attn_sink, gate_b, hnorm_g, odd_out_w):
    n_dec = x_sample.shape[0]
    n_ctx = x_prompt.shape[0]
    cond = jnp.concatenate([c_ctx[None, :], c, jnp.zeros((COND_ROWS - 1 - n_dec, D_MODEL), F32)], axis=0)
    modv = _modulation(cond, mod_w, mod_b)

    yp, ys = x_prompt, x_sample
    w1, w2 = mlp_w1, mlp_w2

    ev = (norm_g, even_in_w.astype(BF16), conv_a_w, conv_a_b[:, None, :], ln_a_g[:, None, :], ln_a_b[:, None, :],
          conv_b_w, even_out_w.astype(BF16))
    yp = _even_layer(yp, modv, 0, 0, 0, *ev)
    ys = _even_layer(ys, modv, 0, 1, 1, *ev)
    yp = _mlp_layer(yp, modv, 0, 0, 0, norm_g, w1, w2)
    ys = _mlp_layer(ys, modv, 0, 1, 1, norm_g, w1, w2)

    order = jnp.array([0, 2, 1, 3])
    d_main = D_IN_ODD - 4 * N_HEADS
    wg = odd_in_w[0][:, d_main:].reshape(D_MODEL, 4, N_HEADS)[:, order, :].reshape(D_MODEL, 4 * N_HEADS)
    w_gate = jnp.pad(wg, ((0, 0), (0, LANES - 4 * N_HEADS))).astype(BF16)
    gate_bias = jnp.pad(gate_b[0][order, :].reshape(1, 4 * N_HEADS), ((0, 0), (0, LANES - 4 * N_HEADS)))
    sink_b = _lane_bcast(attn_sink[0])
    odd = (1, norm_g, odd_in_w.astype(BF16), w_gate, gate_bias, sink_b, hnorm_g[:, None, :], odd_out_w.astype(BF16))

    op, k_t, v_t, c_f, c_b, n_new, m_new = _odd_context(yp, modv, *odd)
    new_k, new_v = jnp.swapaxes(k_t, -1, -2), jnp.swapaxes(v_t, -1, -2)

    t_dec = x_sample.shape[1]
    cos_t, sin_t = _rope_tables(t_dec)
    kc = jnp.concatenate([cache_k[:, 0], cache_k[:, 0]], axis=-1)
    vc = jnp.concatenate([cache_v[:, 0], cache_v[:, 0]], axis=-1)
    c_in = jnp.concatenate([_pair_blockdiag(jnp.swapaxes(state_c_fwd[:, 0], -1, -2)),
                            _pair_blockdiag(jnp.swapaxes(state_c_bwd[:, 0], -1, -2))], axis=1)
    n_in = jnp.concatenate([state_n_fwd[:, 0].reshape(n_dec, N_PAIRS, LANES),
                            state_n_bwd[:, 0].reshape(n_dec, N_PAIRS, LANES)], axis=1)
    m_in = _lane_bcast(jnp.concatenate([state_m_fwd[:, 0], state_m_bwd[:, 0]], axis=1))
    os_ = _odd_latent(ys, modv, *odd, cos_t, sin_t, kc, vc, c_in, n_in, m_in)

    yp = _mlp_layer(op, modv, 1, 0, 0, norm_g, w1, w2)
    ys = _mlp_layer(os_, modv, 1, 1, 1, norm_g, w1, w2)

    n_f = n_new[:, :N_PAIRS].reshape(n_ctx, N_HEADS, HEAD_DIM)[:, None]
    n_b = n_new[:, N_PAIRS:].reshape(n_ctx, N_HEADS, HEAD_DIM)[:, None]
    m_f = m_new[:, :N_HEADS, 0][:, None]
    m_b = m_new[:, N_HEADS:, 0][:, None]
    return (yp, ys, new_k, new_v, c_f, n_f, m_f, c_b, n_b, m_b)
```

```python
import functools

import jax
import jax.numpy as jnp
from jax import lax
from jax.experimental import pallas as pl
from jax.experimental.pallas import tpu as pltpu

F32 = jnp.float32
BF16 = jnp.bfloat16

D_MODEL = 1024
D_FF = 4 * D_MODEL
EPS = 1e-6
D_HALF = D_MODEL // 2
CONF_WIDTH = 31
CONF_HALO = 16
HEAD_DIM = 64
N_HEADS = 8
N_PAIRS = N_HEADS // 2
N_KV = 2
LANES = 128
CHUNK = 128
WINDOW = 128
GRID_W = 64
ROPE_BASE = 10000.0
ATT_SCALE = HEAD_DIM ** -0.5
D_IN_ODD = D_HALF + 2 * N_KV * HEAD_DIM + 4 * D_HALF + 4 * N_HEADS
ROW_CHUNK = 256
MLP_ROWS = 512
ODD_CTX_SEQS = 2
ODD_STAGGER = 6
EVEN_CONV_ROWS = 64
EVEN_CHUNKS = 2
EVEN_STAGGER = 3
COND_ROWS = 8
VMEM_LIMIT = 56 * 1024 * 1024


def _dot(a, b):
    return jnp.dot(a, b, preferred_element_type=F32)


def _dot_nt(a, b):
    return lax.dot_general(a, b, (((1,), (1,)), ((), ())), preferred_element_type=F32)


def _split3(x):
    hi = x.astype(BF16)
    r1 = x - hi.astype(F32)
    mid = r1.astype(BF16)
    lo = (r1 - mid.astype(F32)).astype(BF16)
    return hi, mid, lo


def _dot_exact_lhs(x, b01):
    hi, mid, lo = _split3(x)
    return _dot(hi, b01) + _dot(mid, b01) + _dot(lo, b01)


def _rms(x, g):
    return x * lax.rsqrt(jnp.mean(x * x, axis=-1, keepdims=True) + EPS) * g


def _norm_mod(x, g, shift, scale):
    return _rms(x, g) * (1.0 + scale) + shift


def _params(n_grid):
    return pltpu.CompilerParams(dimension_semantics=("arbitrary",) * n_grid, vmem_limit_bytes=VMEM_LIMIT)


def _const_spec(shape):
    zeros = (0,) * len(shape)
    return pl.BlockSpec(shape, lambda *_: zeros, pipeline_mode=pl.Buffered(1))


def _layer_spec(shape, layer):
    index = (layer,) + (0,) * len(shape)
    return pl.BlockSpec((None,) + shape, lambda *_: index, pipeline_mode=pl.Buffered(1))


def _mod_spec(layer, cond_of):
    return pl.BlockSpec((None, None, 6, D_MODEL), lambda *idx: (layer, cond_of(*idx), 0, 0))


def _mod_kernel(cond_ref, w_ref, b_ref, o_ref):
    s = jax.nn.silu(cond_ref[...]).astype(BF16)
    o_ref[...] = _dot(s, w_ref[...].astype(BF16)) + b_ref[...]


def _modulation(cond, mod_w, mod_b):
    depth = mod_w.shape[0]
    n_out = mod_w.shape[2]
    tn = 2 * D_MODEL
    out = pl.pallas_call(
        _mod_kernel,
        grid=(depth, n_out // tn),
        in_specs=[
            pl.BlockSpec((COND_ROWS, D_MODEL), lambda l, j: (0, 0)),
            pl.BlockSpec((None, D_MODEL, tn), lambda l, j: (l, 0, j)),
            pl.BlockSpec((None, 1, tn), lambda l, j: (l, 0, j)),
        ],
        out_specs=pl.BlockSpec((None, COND_ROWS, tn), lambda l, j: (l, 0, j)),
        out_shape=jax.ShapeDtypeStruct((depth, COND_ROWS, n_out), F32),
        compiler_params=_params(2),
        name="modulation",
    )(cond, mod_w, mod_b.reshape(depth, 1, n_out))
    return out.reshape(depth, COND_ROWS, 6, D_MODEL)


def _stage_bf16(pairs, stage, sem):
    width = stage.shape[2]
    blocks = [(src.at[:, pl.ds(c0, width)], dst.at[:, pl.ds(c0, width)])
              for src, dst in pairs for c0 in range(0, dst.shape[1], width)]
    copies = [pltpu.make_async_copy(src, stage.at[k % 2], sem.at[k % 2]) for k, (src, _) in enumerate(blocks)]
    for cp in copies[:2]:
        cp.start()
    for k, (_, dst) in enumerate(blocks):
        copies[k].wait()
        dst[...] = stage[k % 2].astype(BF16)
        if k + 2 < len(blocks):
            copies[k + 2].start()


def _weight_scratch(shapes, width):
    return ([pltpu.VMEM(sh, BF16) for sh in shapes]
            + [pltpu.VMEM((2, D_MODEL, width), F32), pltpu.SemaphoreType.DMA((2,))])


def _run_staggered(programs, stagger):
    programs = list(programs)
    live, rounds = [], 0
    while programs or live:
        if programs and rounds % stagger == 0:
            live.append(programs.pop(0))
        for g in list(live):
            try:
                next(g)
            except StopIteration:
                live.remove(g)
        rounds += 1


def _even_kernel(*refs, n_chunks, nsub, j):
    halos, (xc_ref, mod_ref, ng_ref, win_hbm, caw_ref, cab_ref, lng_ref, lnb_ref, cbw_ref, wout_hbm, o_ref,
            apad, cpad, bgs, zs, win_s, wout_s, stage, sem) = refs[:2 * nsub], refs[2 * nsub:]

    @pl.when(pl.program_id(0) == 0)
    def _():
        _stage_bf16([(win_hbm.at[j], win_s), (wout_hbm.at[j], wout_s)], stage, sem)

    shared = (mod_ref, ng_ref, win_s, caw_ref, cab_ref, lng_ref, lnb_ref, cbw_ref, wout_s)
    _run_staggered(
        (_even_chunk(pl.program_id(0) * nsub + k, halos[2 * k], xc_ref.at[k], halos[2 * k + 1], *shared,
                     o_ref.at[k], apad.at[k], cpad.at[k], bgs.at[k], zs.at[k], n_chunks=n_chunks)
         for k in range(nsub)), EVEN_STAGGER)


def _even_chunk(g, xp_ref, xc_ref, xn_ref, mod_ref, ng_ref, win_ref, caw_ref, cab_ref, lng_ref, lnb_ref,
                cbw_ref, wout_ref, o_ref, apad, cpad, bgs, zs, *, n_chunks):
    c = g % n_chunks if n_chunks > 1 else 0
    rows = ROW_CHUNK + 2 * CONF_HALO
    shift1, scale1, gate1 = mod_ref[0:1, :], mod_ref[1:2, :], mod_ref[2:3, :]
    own = slice(CONF_HALO, CONF_HALO + ROW_CHUNK)
    if n_chunks == 1:
        h = _norm_mod(xc_ref[...], ng_ref[0:1, :], shift1, scale1).astype(BF16)
        h_own = h
        for pad_ref in (apad, cpad):
            pad_ref[0:CONF_HALO, :] = jnp.zeros((CONF_HALO, D_HALF), F32)
            pad_ref[CONF_HALO + ROW_CHUNK:rows, :] = jnp.zeros((CONF_HALO, D_HALF), F32)
        keep = lambda v: v
        span = own
    else:
        xh = jnp.concatenate([xp_ref[...], xc_ref[...], xn_ref[...]], axis=0)
        h = _norm_mod(xh, ng_ref[0:1, :], shift1, scale1).astype(BF16)
        h_own = h[own]
        ri = lax.broadcasted_iota(jnp.int32, (rows, D_HALF), 0)
        lo = jnp.where(c == 0, CONF_HALO, 0)
        hi = jnp.where(c == n_chunks - 1, CONF_HALO + ROW_CHUNK, rows)
        inside = (ri >= lo) & (ri < hi)
        keep = lambda v: jnp.where(inside, v, 0.0)
        span = slice(0, rows)
    a = _dot(h, win_ref[:, 0:D_HALF]) * jax.nn.sigmoid(_dot(h, win_ref[:, D_HALF:2 * D_HALF]))
    apad[span, :] = keep(a)
    yield
    cx = _dot(h, win_ref[:, 3 * D_HALF:4 * D_HALF]) * _dot(h, win_ref[:, 4 * D_HALF:5 * D_HALF])
    cpad[span, :] = keep(cx)
    yield
    bgs[...] = _dot(h_own, win_ref[:, 2 * D_HALF:3 * D_HALF])
    yield

    sub = EVEN_CONV_ROWS
    tile = 8
    for j in range(ROW_CHUNK // sub):
        r0 = j * sub
        groups = []
        for cg in range(D_HALF // LANES):
            cols = slice(cg * LANES, (cg + 1) * LANES)
            acc = None
            for r in range(tile):
                part = None
                for m in range(-(-(CONF_WIDTH + 1) // tile)):
                    o = tile * m + r
                    if 1 <= o <= CONF_WIDTH:
                        term = caw_ref[o - 1:o, cols] * apad[r0 + tile * m:r0 + tile * m + sub + tile, cols]
                        part = term if part is None else part + term
                shifted = part[r:r + sub, :]
                acc = shifted if acc is None else acc + shifted
            groups.append(acc)
        acc = jnp.concatenate(groups, axis=1) + cab_ref[...]
        mu = jnp.mean(acc, axis=-1, keepdims=True)
        dlt = acc - mu
        var = jnp.mean(dlt * dlt, axis=-1, keepdims=True)
        a_out = jax.nn.silu(dlt * lax.rsqrt(var + EPS) * lng_ref[...] + lnb_ref[...])
        zs[r0:r0 + sub, 0:D_HALF] = a_out.astype(BF16)
        base = r0 + CONF_HALO - 1
        sc = (cbw_ref[0:1, :] * cpad[base:base + sub, :]
              + cbw_ref[1:2, :] * cpad[base + 1:base + 1 + sub, :]
              + cbw_ref[2:3, :] * cpad[base + 2:base + 2 + sub, :])
        zs[r0:r0 + sub, D_HALF:D_MODEL] = (bgs[r0:r0 + sub, :] * sc).astype(BF16)
        yield

    o = _dot(zs[...], wout_ref[...])
    o_ref[...] = xc_ref[...] + gate1 * _rms(o, ng_ref[1:2, :])
    yield


def _even_layer(x, modv, layer, cond_base, cond_stride, ng, win, caw, cab, lng, lnb, cbw, wout):
    j = layer // 2
    b, t, _ = x.shape
    n_chunks = t // ROW_CHUNK
    nsub = EVEN_CHUNKS
    assert (b * n_chunks) % nsub == 0 and (cond_stride == 0 or n_chunks % nsub == 0)
    hpc = ROW_CHUNK // CONF_HALO
    n_halo_blocks = b * t // CONF_HALO
    rows = ROW_CHUNK + 2 * CONF_HALO
    halo_specs = []
    for k in range(nsub):
        halo_specs += [
            pl.BlockSpec((None, CONF_HALO, D_MODEL),
                         lambda i, k=k: (jnp.maximum((i * nsub + k) * hpc - 1, 0), 0, 0)),
            pl.BlockSpec((None, CONF_HALO, D_MODEL),
                         lambda i, k=k: (jnp.minimum((i * nsub + k + 1) * hpc, n_halo_blocks - 1), 0, 0)),
        ]
    x_halo = x.reshape(n_halo_blocks, CONF_HALO, D_MODEL)
    x_chunks = x.reshape(b * n_chunks, ROW_CHUNK, D_MODEL)
    kern = functools.partial(_even_kernel, n_chunks=n_chunks, nsub=nsub, j=j)
    out = pl.pallas_call(
        kern,
        grid=(b * n_chunks // nsub,),
        in_specs=halo_specs + [
            pl.BlockSpec((nsub, ROW_CHUNK, D_MODEL), lambda i: (i, 0, 0)),
            _mod_spec(layer, lambda i: cond_base + cond_stride * ((i * nsub) // n_chunks)),
            _layer_spec((4, D_MODEL), layer),
            pl.BlockSpec(memory_space=pl.ANY),
            _layer_spec((CONF_WIDTH, D_HALF), j),
            _layer_spec((1, D_HALF), j),
            _layer_spec((1, D_HALF), j),
            _layer_spec((1, D_HALF), j),
            _layer_spec((3, D_HALF), j),
            pl.BlockSpec(memory_space=pl.ANY),
        ],
        out_specs=pl.BlockSpec((nsub, ROW_CHUNK, D_MODEL), lambda i: (i, 0, 0)),
        out_shape=jax.ShapeDtypeStruct(x_chunks.shape, F32),
        scratch_shapes=[
            pltpu.VMEM((nsub, rows, D_HALF), F32),
            pltpu.VMEM((nsub, rows, D_HALF), F32),
            pltpu.VMEM((nsub, ROW_CHUNK, D_HALF), F32),
            pltpu.VMEM((nsub, ROW_CHUNK, D_MODEL), BF16),
        ] + _weight_scratch([(D_MODEL, 5 * D_HALF), (D_MODEL, D_MODEL)], D_HALF),
        compiler_params=_params(1),
        name="even_mixer",
    )(*([x_halo] * (2 * nsub)), x_chunks, modv, ng, win, caw, cab, lng, lnb, cbw, wout)
    return out.reshape(b, t, D_MODEL)


def _mlp_kernel(x_ref, mod_ref, ng_ref, w1_hbm, w2_hbm, o_ref, w1_s, w2_s, stage, sem, *, layer):
    n_blocks = D_FF // D_MODEL

    def block_copy(k):
        c, slot = k // 2, k % 2
        span = pl.ds(c * D_MODEL, D_MODEL)
        src = w1_hbm.at[layer, :, span] if k % 2 == 0 else w2_hbm.at[layer, span, :]
        return pltpu.make_async_copy(src, stage.at[slot], sem.at[slot])

    def fetch(k, dst):
        block_copy(k).wait()
        dst[...] = stage[k % 2].astype(BF16)
        if k + 2 < 2 * n_blocks:
            block_copy(k + 2).start()

    def tile(load):
        x = x_ref[...]
        h = _norm_mod(x, ng_ref[2:3, :], mod_ref[3:4, :], mod_ref[4:5, :]).astype(BF16)
        acc = None
        for c in range(n_blocks):
            cols = slice(c * D_MODEL, (c + 1) * D_MODEL)
            if load:
                fetch(2 * c, w1_s.at[:, cols])
            hid = jnp.square(jnp.maximum(_dot(h, w1_s[:, cols]), 0.0)).astype(BF16)
            if load:
                fetch(2 * c + 1, w2_s.at[cols, :])
            part = _dot(hid, w2_s[cols, :])
            acc = part if acc is None else acc + part
        o_ref[...] = x + mod_ref[5:6, :] * _rms(acc, ng_ref[3:4, :])

    first = pl.program_id(0) == 0

    @pl.when(first)
    def _():
        block_copy(0).start()
        block_copy(1).start()
        tile(load=True)

    @pl.when(jnp.logical_not(first))
    def _():
        tile(load=False)


def _mlp_layer(x, modv, layer, cond_base, cond_stride, ng, w1, w2):
    b, t, _ = x.shape
    tm = MLP_ROWS
    assert (b * t) % tm == 0 and (cond_stride == 0 or t % tm == 0)
    tiles_per_seq = max(t // tm, 1)
    x2 = x.reshape(b * t, D_MODEL)
    out = pl.pallas_call(
        functools.partial(_mlp_kernel, layer=layer),
        grid=(b * t // tm,),
        in_specs=[
            pl.BlockSpec((tm, D_MODEL), lambda i: (i, 0)),
            _mod_spec(layer, lambda i: cond_base + cond_stride * (i // tiles_per_seq)),
            _layer_spec((4, D_MODEL), layer),
            pl.BlockSpec(memory_space=pl.ANY),
            pl.BlockSpec(memory_space=pl.ANY),
        ],
        out_specs=pl.BlockSpec((tm, D_MODEL), lambda i: (i, 0)),
        out_shape=jax.ShapeDtypeStruct(x2.shape, F32),
        scratch_shapes=[
            pltpu.VMEM((D_MODEL, D_FF), BF16),
            pltpu.VMEM((D_FF, D_MODEL), BF16),
            pltpu.VMEM((2, D_MODEL, D_MODEL), F32),
            pltpu.SemaphoreType.DMA((2,)),
        ],
        compiler_params=_params(1),
        name="mlp",
    )(x2, modv, ng, w1, w2)
    return out.reshape(b, t, D_MODEL)


def _log_sigmoid(x):
    return jnp.minimum(x, 0.0) - jnp.log(1.0 + jnp.exp(-jnp.abs(x)))


def _rot_half(x, first_half):
    return jnp.where(first_half, pltpu.roll(x, 96, axis=1), pltpu.roll(x, 32, axis=1))


def _values_and_ones(v):
    vt = jnp.transpose(v)
    row = lax.broadcasted_iota(jnp.int32, vt.shape, 0)
    return jnp.where(row < HEAD_DIM, vt, 1.0)


def _aligned(i, m):
    return i * m if isinstance(i, int) else pl.multiple_of(i * m, m)


def _drain(pieces):
    for _ in pieces:
        pass


def _each(body, n, static, unroll=1):
    if static:
        for i in range(n):
            yield from body(i)
    else:
        lax.fori_loop(0, n, lambda i, c: (_drain(body(i)), c)[1], 0, unroll=unroll)


def _alternate(*programs):
    live = list(programs)
    while live:
        for g in list(live):
            try:
                next(g)
            except StopIteration:
                live.remove(g)
            else:
                yield


def _odd_kernel(*refs, t, lc, latent, nseq, j):
    *refs, win_s, wout_s, stage, sem = refs

    @pl.when(pl.program_id(0) == 0)
    def _():
        _stage_bf16([(refs[3].at[j], win_s), (refs[8].at[j], wout_s)], stage, sem)

    refs[3], refs[8] = win_s, wout_s
    shared = set(range(1, 11 if latent else 9))
    _run_staggered((_odd_seq(*[r if k in shared else r.at[sq] for k, r in enumerate(refs)],
                             t=t, lc=lc, latent=latent) for sq in range(nseq)), ODD_STAGGER)


def _odd_seq(*refs, t, lc, latent):
    if latent:
        (x_ref, mod_ref, ng_ref, win_ref, wg_ref, gb_ref, sink_ref, hn_ref, wout_ref,
         cos_ref, sin_ref, kc_ref, vc_ref, cin_ref, nin_ref, min_ref,
         o_ref,
         qa_s, qb_s, kk_s, vvt_s, qma_s, qmb_s, qm_s, km_s, vmt_s, vmtf_s, om_s, g_s, z_s, hft_s, hbt_s,
         sr_s, st_s, qc_s, c_s, n_s, m_s, s_s, p_s, kc_s, vct_s) = refs
    else:
        (x_ref, mod_ref, ng_ref, win_ref, wg_ref, gb_ref, sink_ref, hn_ref, wout_ref,
         o_ref, ko_ref, vo_ref, cf_ref, cb_ref, no_ref, mo_ref,
         qa_s, qb_s, kk_s, vvt_s, qma_s, qmb_s, qm_s, km_s, vmt_s, vmtf_s, om_s, g_s, z_s, hft_s, hbt_s,
         sr_s, st_s, qc_s, c_s, n_s, m_s, s_s, p_s) = refs

    static = not latent
    n_blocks = t // CHUNK
    pad = CHUNK if latent else 0
    shift1, scale1, gate1 = mod_ref[0:1, :], mod_ref[1:2, :], mod_ref[2:3, :]
    lane = lax.broadcasted_iota(jnp.int32, (1, LANES), 1)
    left = lane < HEAD_DIM
    first_half = (lane % HEAD_DIM) < (HEAD_DIM // 2)
    ti = lax.broadcasted_iota(jnp.int32, (CHUNK, CHUNK), 0)
    si = lax.broadcasted_iota(jnp.int32, (CHUNK, CHUNK), 1)
    top = ti < HEAD_DIM
    same_head = top == (si < HEAD_DIM)

    if latent:
        for kv in range(N_KV):
            kk_s[kv, 0:CHUNK, :] = jnp.zeros((CHUNK, LANES), BF16)
            kk_s[kv, CHUNK + t:2 * CHUNK + t, :] = jnp.zeros((CHUNK, LANES), BF16)
            vvt_s[kv, :, 0:CHUNK] = jnp.zeros((LANES, CHUNK), BF16)
            vvt_s[kv, :, CHUNK + t:2 * CHUNK + t] = jnp.zeros((LANES, CHUNK), BF16)
            kc_s[kv] = kc_ref[kv].astype(BF16)
            vct_s[kv] = _values_and_ones(vc_ref[kv]).astype(BF16)

    def project(rc):
        r0 = _aligned(rc, ROW_CHUNK)
        rows = pl.ds(r0, ROW_CHUNK)
        krows = pl.ds(r0 + pad, ROW_CHUNK)
        h = _norm_mod(x_ref[rows, :], ng_ref[0:1, :], shift1, scale1).astype(BF16)
        if latent:
            cs, sn = cos_ref[rows, :], sin_ref[rows, :]
        q = _dot(h, win_ref[:, 0:D_HALF])
        for p in range(N_PAIRS):
            cols = slice(p * LANES, (p + 1) * LANES)
            qp = q[:, cols]
            if latent:
                qp = qp * cs + _rot_half(qp, first_half) * sn
            qp = qp * ATT_SCALE
            qa_s[rows, cols] = jnp.where(left, qp, 0.0).astype(BF16)
            qb_s[rows, cols] = jnp.where(left, 0.0, qp).astype(BF16)
        yield
        kv2 = _dot(h, win_ref[:, D_HALF:D_HALF + 2 * LANES])
        ka, va = kv2[:, 0:LANES], kv2[:, LANES:2 * LANES]
        if latent:
            ka = ka * cs + _rot_half(ka, first_half) * sn
        kr = pltpu.roll(ka, HEAD_DIM, axis=1)
        vr = pltpu.roll(va, HEAD_DIM, axis=1)
        if not latent:
            ka_t, va_t = jnp.transpose(ka), jnp.transpose(va)
            for kv in range(N_KV):
                ko_ref[0, kv, :, rows] = ka_t[kv * HEAD_DIM:(kv + 1) * HEAD_DIM, :]
                vo_ref[0, kv, :, rows] = va_t[kv * HEAD_DIM:(kv + 1) * HEAD_DIM, :]
        kk_s[0, krows, :] = jnp.where(left, ka, kr).astype(BF16)
        kk_s[1, krows, :] = jnp.where(left, kr, ka).astype(BF16)
        vvt_s[0, :, krows] = _values_and_ones(va).astype(BF16)
        vvt_s[1, :, krows] = _values_and_ones(vr).astype(BF16)
        yield
        base = D_HALF + 2 * LANES
        qm = _dot(h, win_ref[:, base:base + D_HALF])
        qm_s[rows, :] = qm.astype(BF16)
        for p in range(N_PAIRS):
            cols = slice(p * LANES, (p + 1) * LANES)
            qma_s[rows, cols] = jnp.where(left, qm[:, cols], 0.0).astype(BF16)
            qmb_s[rows, cols] = jnp.where(left, 0.0, qm[:, cols]).astype(BF16)
        yield
        vm = _dot(h, win_ref[:, base + 2 * D_HALF:base + 3 * D_HALF])
        for p in range(N_PAIRS):
            cols = slice(p * LANES, (p + 1) * LANES)
            vt = jnp.transpose(vm[:, cols])
            vmtf_s[cols, rows] = vt
            vmt_s[cols, rows] = vt.astype(BF16)
        yield
        km = _dot(h, win_ref[:, base + D_HALF:base + 2 * D_HALF]) * (HEAD_DIM ** -0.5)
        km_s[rows, :] = km.astype(BF16)
        yield
        om_s[rows, :] = _dot(h, win_ref[:, base + 3 * D_HALF:base + 4 * D_HALF])
        g_s[rows, :] = _dot(h, wg_ref[...]) + gb_ref[...]
        yield

    yield from _each(project, t // ROW_CHUNK, static)

    kj = lax.broadcasted_iota(jnp.int32, (3 * CHUNK, 2 * CHUNK), 0)
    qi = lax.broadcasted_iota(jnp.int32, (3 * CHUNK, 2 * CHUNK), 1) % CHUNK
    band_ok = jnp.abs(kj - CHUNK - qi) <= WINDOW
    head_a = lax.broadcasted_iota(jnp.int32, (1, 2 * CHUNK), 1) < CHUNK

    n_keys = _att_keys(lc, latent)
    group = _att_group(latent)

    def attention():
        def attend(i):
            r0 = _aligned(i, CHUNK)
            rows = pl.ds(r0, CHUNK)
            if latent:
                key_pos = kj + (i - 1) * CHUNK
                mask = band_ok & (key_pos >= 0) & (key_pos < t)
                win = pl.ds(r0, 3 * CHUNK)
            for g0 in range(0, N_PAIRS, group):
                for gi in range(group):
                    p = g0 + gi
                    kv = p // (N_PAIRS // N_KV)
                    cols = slice(p * LANES, (p + 1) * LANES)
                    q2 = jnp.concatenate([qa_s[rows, cols], qb_s[rows, cols]], axis=0)
                    if latent:
                        s_s[gi, 0:lc, :] = _dot_nt(kc_s[kv], q2)
                        s_s[gi, lc:n_keys, :] = jnp.where(mask, _dot_nt(kk_s[kv, win, :], q2), -jnp.inf)
                    else:
                        s_s[gi] = _dot_nt(kk_s[kv], q2)
                yield
                maxes = []
                for gi in range(group):
                    p = g0 + gi
                    sink = jnp.where(head_a, sink_ref[2 * p:2 * p + 1, 0:1], sink_ref[2 * p + 1:2 * p + 2, 0:1])
                    mx = jnp.maximum(jnp.max(s_s[gi], axis=0, keepdims=True), sink)
                    p_s[gi] = jnp.exp(s_s[gi] - mx).astype(BF16)
                    maxes.append((sink, mx))
                yield
                for gi in range(group):
                    p = g0 + gi
                    kv = p // (N_PAIRS // N_KV)
                    cols = slice(p * LANES, (p + 1) * LANES)
                    if latent:
                        num = (_dot(vct_s[kv], p_s[gi, 0:lc, :]) + _dot(vvt_s[kv, :, win], p_s[gi, lc:n_keys, :]))
                    else:
                        num = _dot(vvt_s[kv], p_s[gi])
                    sink, mx = maxes[gi]
                    den = num[HEAD_DIM:HEAD_DIM + 1, :] + jnp.exp(sink - mx)
                    out = num[0:HEAD_DIM, :] * (1.0 / den)
                    pair = jnp.concatenate([out[:, 0:CHUNK], out[:, CHUNK:2 * CHUNK]], axis=0)
                    z_s[rows, cols] = jnp.transpose(pair).astype(BF16)
                yield

        return attend

    if latent:
        c_s[...] = cin_ref[...]
        n_s[...] = nin_ref[...]
        m_s[...] = min_ref[...]
    else:
        c_s[...] = jnp.zeros(c_s.shape, F32)
        n_s[...] = jnp.zeros(n_s.shape, F32)
        m_s[...] = jnp.zeros(m_s.shape, F32)

    see = (ti <= si, ti >= si)
    tri = tuple(m.astype(F32).astype(BF16) for m in see)
    last = (CHUNK - 1, 0)

    def mlstm(i):
        offs = (_aligned(i, CHUNK), _aligned(n_blocks - 1 - i, CHUNK))
        half = (slice(0, LANES), slice(LANES, 2 * LANES))
        a_rows, b_rows = [], []
        for d in range(2):
            gt = jnp.transpose(g_s[pl.ds(offs[d], CHUNK), :])
            lf = _log_sigmoid(gt[2 * N_HEADS:4 * N_HEADS, :])
            bcum = _dot_exact_lhs(lf, tri[d])[8 * d:8 * d + 8, :]
            a_rows.append(gt[8 * d:8 * d + 8, :] - bcum)
            b_rows.append(bcum)
        yield
        q_ns = []
        for d in range(2):
            rows = pl.ds(offs[d], CHUNK)
            for p in range(N_PAIRS):
                u = d * N_PAIRS + p
                cols = slice(p * LANES, (p + 1) * LANES)
                q2 = jnp.concatenate([qma_s[rows, cols], qmb_s[rows, cols]], axis=0)
                sr_s[u] = _dot_nt(km_s[rows, cols], q2)
                qc_s[u] = _dot_nt(c_s[u].astype(BF16), qm_s[rows, cols])
                n8 = jnp.broadcast_to(n_s[u:u + 1, :], (8, LANES)).astype(BF16)
                q_ns.append(_dot_nt(n8, q2)[0:1, :])
            yield
        stats = {}
        for d in range(2):
            for hd in range(N_HEADS):
                u, j = d * N_PAIRS + hd // 2, hd % 2
                mrow = d * N_HEADS + hd
                a_row = a_rows[d][hd:hd + 1, :]
                b_row = b_rows[d][hd:hd + 1, :]
                m_prev = m_s[mrow:mrow + 1, 0:1]
                a_col = jnp.transpose(jnp.broadcast_to(a_row, (CHUNK, CHUNK)))
                z_t = jnp.where(see[d], a_col, -jnp.inf)
                m_run = jnp.maximum(jnp.max(z_t, axis=0, keepdims=True), m_prev)
                s_t = sr_s[u, :, half[j]] * jnp.exp(z_t - m_run)
                st_s[u, :, half[j]] = s_t.astype(BF16)
                w_int = jnp.exp(m_prev - m_run)
                den = jnp.sum(s_t, axis=0, keepdims=True) + w_int * q_ns[u][:, half[j]]
                inv = 1.0 / jnp.maximum(jnp.abs(den), jnp.exp(-(b_row + m_run)))
                m_last = m_run[:, last[d]:last[d] + 1]
                stats[(u, j)] = (w_int, inv, jnp.exp(a_row - m_last), jnp.exp(m_prev - m_last))
                m_s[mrow:mrow + 1, :] = jnp.broadcast_to(b_row[:, last[d]:last[d] + 1] + m_last, (1, LANES))
            yield
        for d in range(2):
            rows = pl.ds(offs[d], CHUNK)
            ht_s = (hft_s, hbt_s)[d]
            for p in range(N_PAIRS):
                u = d * N_PAIRS + p
                cols = slice(p * LANES, (p + 1) * LANES)
                (w_a, inv_a, e_a, dec_a), (w_b, inv_b, e_b, dec_b) = stats[(u, 0)], stats[(u, 1)]
                kp = km_s[rows, cols]
                num2 = _dot(vmt_s[cols, rows], st_s[u])
                num = jnp.where(top, num2[:, half[0]], num2[:, half[1]])
                ht_s[cols, rows] = (num + jnp.where(top, w_a, w_b) * qc_s[u]) * jnp.where(top, inv_a, inv_b)
                vt_e = (vmtf_s[cols, rows] * jnp.where(top, e_a, e_b)).astype(BF16)
                c_s[u] = jnp.where(top, dec_a, dec_b) * c_s[u] + jnp.where(same_head, _dot(vt_e, kp), 0.0)
                e2 = jnp.concatenate([e_a, e_b, jnp.zeros((6, CHUNK), F32)], axis=0).astype(BF16)
                n_k = _dot(e2, kp)
                n_s[u:u + 1, :] = (jnp.where(left, dec_a, dec_b) * n_s[u:u + 1, :]
                                   + jnp.where(left, n_k[0:1, :], n_k[1:2, :]))
            yield

    attend = attention()
    yield from _each(lambda i: _alternate(attend(i), mlstm(i)), n_blocks, static, unroll=2)

    if not latent:
        for d, c_ref in enumerate((cf_ref, cb_ref)):
            for p in range(N_PAIRS):
                c_pair = jnp.transpose(c_s[d * N_PAIRS + p])
                c_ref[0, 2 * p] = c_pair[0:HEAD_DIM, 0:HEAD_DIM]
                c_ref[0, 2 * p + 1] = c_pair[HEAD_DIM:LANES, HEAD_DIM:LANES]
        no_ref[...] = n_s[...]
        mo_ref[...] = m_s[...]
        yield

    top_w = lax.broadcasted_iota(jnp.int32, (LANES, ROW_CHUNK), 0) < HEAD_DIM

    def finish(rc):
        r0 = _aligned(rc, ROW_CHUNK)
        rows = pl.ds(r0, ROW_CHUNK)
        for p in range(N_PAIRS):
            cols = slice(p * LANES, (p + 1) * LANES)
            hm = hft_s[cols, rows] + hbt_s[cols, rows]
            sq = hm * hm
            ms_a = jnp.sum(sq[0:HEAD_DIM], axis=0, keepdims=True)
            ms_b = jnp.sum(sq[HEAD_DIM:LANES], axis=0, keepdims=True)
            ms = jnp.where(top_w, ms_a, ms_b) * (1.0 / HEAD_DIM)
            y = jnp.transpose(hm * lax.rsqrt(ms + EPS)) * hn_ref[:, cols] * jax.nn.sigmoid(om_s[rows, cols])
            z_s[rows, D_HALF + p * LANES:D_HALF + (p + 1) * LANES] = y.astype(BF16)
            if p % 2:
                yield
        o = _dot(z_s[rows, :], wout_ref[...])
        o_ref[rows, :] = x_ref[rows, :] + gate1 * _rms(o, ng_ref[1:2, :])
        yield

    yield from _each(finish, t // ROW_CHUNK, static)


def _att_keys(lc, latent):
    return lc + 3 * CHUNK if latent else lc


def _att_group(latent):
    return 2 if latent else N_PAIRS


def _odd_scratch(t, lc, latent, nseq):
    pad = 2 * CHUNK if latent else 0
    att = (_att_group(latent), _att_keys(lc, latent), 2 * CHUNK)
    shapes = [
        pltpu.VMEM((t, D_HALF), BF16),
        pltpu.VMEM((t, D_HALF), BF16),
        pltpu.VMEM((N_KV, t + pad, LANES), BF16),
        pltpu.VMEM((N_KV, LANES, t + pad), BF16),
        pltpu.VMEM((t, D_HALF), BF16),
        pltpu.VMEM((t, D_HALF), BF16),
        pltpu.VMEM((t, D_HALF), BF16),
        pltpu.VMEM((t, D_HALF), BF16),
        pltpu.VMEM((D_HALF, t), BF16),
        pltpu.VMEM((D_HALF, t), F32),
        pltpu.VMEM((t, D_HALF), F32),
        pltpu.VMEM((t, LANES), F32),
        pltpu.VMEM((t, D_MODEL), BF16),
        pltpu.VMEM((D_HALF, t), F32),
        pltpu.VMEM((D_HALF, t), F32),
        pltpu.VMEM((2 * N_PAIRS, CHUNK, 2 * LANES), F32),
        pltpu.VMEM((2 * N_PAIRS, CHUNK, 2 * LANES), BF16),
        pltpu.VMEM((2 * N_PAIRS, LANES, CHUNK), F32),
        pltpu.VMEM((2 * N_PAIRS, LANES, LANES), F32),
        pltpu.VMEM((2 * N_PAIRS, LANES), F32),
        pltpu.VMEM((2 * N_HEADS, LANES), F32),
        pltpu.VMEM(att, F32),
        pltpu.VMEM(att, BF16),
    ]
    if latent:
        shapes += [pltpu.VMEM((N_KV, lc, LANES), BF16), pltpu.VMEM((N_KV, LANES, lc), BF16)]
    return [pltpu.VMEM((nseq,) + tuple(sh.shape), sh.dtype) for sh in shapes]


def _odd_weight_scratch():
    return _weight_scratch([(D_MODEL, D_IN_ODD - 4 * N_HEADS), (D_MODEL, D_MODEL)], 2 * LANES)


def _odd_common_specs(t, layer, cond_base, cond_stride, nseq):
    assert cond_stride == 0 or nseq == 1
    j = layer // 2
    return [
        _per_seq((t, D_MODEL), nseq),
        _mod_spec(layer, lambda i: cond_base + cond_stride * i),
        _layer_spec((4, D_MODEL), layer),
        pl.BlockSpec(memory_space=pl.ANY),
        _const_spec((D_MODEL, LANES)),
        _const_spec((1, LANES)),
        _const_spec((N_HEADS, LANES)),
        _layer_spec((1, D_HALF), j),
        pl.BlockSpec(memory_space=pl.ANY),
    ]


def _per_seq(shape, nseq):
    return pl.BlockSpec((nseq,) + shape, lambda i: (i,) + (0,) * len(shape))


def _odd_context(x, modv, layer, ng, w_main, w_gate, gate_bias, sink_b, hnorm, wout):
    b, t, _ = x.shape
    nseq = ODD_CTX_SEQS
    assert b % nseq == 0
    kern = functools.partial(_odd_kernel, t=t, lc=t, latent=False, nseq=nseq, j=layer // 2)
    per_seq = functools.partial(_per_seq, nseq=nseq)
    return pl.pallas_call(
        kern,
        grid=(b // nseq,),
        in_specs=_odd_common_specs(t, layer, 0, 0, nseq),
        out_specs=[per_seq((t, D_MODEL)),
                   per_seq((1, N_KV, HEAD_DIM, t)), per_seq((1, N_KV, HEAD_DIM, t)),
                   per_seq((1, N_HEADS, HEAD_DIM, HEAD_DIM)), per_seq((1, N_HEADS, HEAD_DIM, HEAD_DIM)),
                   per_seq((2 * N_PAIRS, LANES)), per_seq((2 * N_HEADS, LANES))],
        out_shape=[jax.ShapeDtypeStruct((b, t, D_MODEL), F32),
                   jax.ShapeDtypeStruct((b, 1, N_KV, HEAD_DIM, t), F32),
                   jax.ShapeDtypeStruct((b, 1, N_KV, HEAD_DIM, t), F32),
                   jax.ShapeDtypeStruct((b, 1, N_HEADS, HEAD_DIM, HEAD_DIM), F32),
                   jax.ShapeDtypeStruct((b, 1, N_HEADS, HEAD_DIM, HEAD_DIM), F32),
                   jax.ShapeDtypeStruct((b, 2 * N_PAIRS, LANES), F32),
                   jax.ShapeDtypeStruct((b, 2 * N_HEADS, LANES), F32)],
        scratch_shapes=_odd_scratch(t, t, False, nseq) + _odd_weight_scratch(),
        compiler_params=_params(1),
        name="odd_mixer_context",
    )(x, modv, ng, w_main, w_gate, gate_bias, sink_b, hnorm, wout)


def _odd_latent(x, modv, layer, ng, w_main, w_gate, gate_bias, sink_b, hnorm, wout, cos_t, sin_t, kc, vc,
                c_in, n_in, m_in):
    b, t, _ = x.shape
    lc = kc.shape[2]
    kern = functools.partial(_odd_kernel, t=t, lc=lc, latent=True, nseq=1, j=layer // 2)
    per_seq = functools.partial(_per_seq, nseq=1)
    return pl.pallas_call(
        kern,
        grid=(b,),
        in_specs=_odd_common_specs(t, layer, 1, 1, 1) + [
            _const_spec((t, LANES)), _const_spec((t, LANES)),
            per_seq((N_KV, lc, LANES)), per_seq((N_KV, lc, LANES)),
            per_seq((2 * N_PAIRS, LANES, LANES)), per_seq((2 * N_PAIRS, LANES)), per_seq((2 * N_HEADS, LANES)),
        ],
        out_specs=per_seq((t, D_MODEL)),
        out_shape=jax.ShapeDtypeStruct((b, t, D_MODEL), F32),
        scratch_shapes=_odd_scratch(t, lc, True, 1) + _odd_weight_scratch(),
        compiler_params=_params(1),
        name="odd_mixer_latent",
    )(x, modv, ng, w_main, w_gate, gate_bias, sink_b, hnorm, wout, cos_t, sin_t, kc, vc, c_in, n_in, m_in)


def _rope_tables(t):
    rows = t // GRID_W
    row = jnp.broadcast_to(jnp.arange(rows)[:, None], (rows, GRID_W)).reshape(t).astype(F32)
    col = jnp.broadcast_to(jnp.arange(GRID_W)[None, :], (rows, GRID_W)).reshape(t).astype(F32)
    n_freq = HEAD_DIM // 4
    inv_freq = ROPE_BASE ** (-jnp.arange(n_freq, dtype=F32) / n_freq)
    ang = jnp.concatenate([row[:, None] * inv_freq, col[:, None] * inv_freq], axis=-1)
    cos, sin = jnp.cos(ang), jnp.sin(ang)
    cos_l = jnp.tile(cos, (1, LANES // cos.shape[1]))
    sin_l = jnp.tile(jnp.concatenate([-sin, sin], axis=-1), (1, LANES // HEAD_DIM))
    return cos_l, sin_l


def _pair_blockdiag(c):
    b = c.shape[0]
    c = c.reshape(b, N_PAIRS, 2, HEAD_DIM, HEAD_DIM)
    z = jnp.zeros_like(c[:, :, 0])
    top = jnp.concatenate([c[:, :, 0], z], axis=-1)
    bot = jnp.concatenate([z, c[:, :, 1]], axis=-1)
    return jnp.concatenate([top, bot], axis=-2)


def _lane_bcast(v):
    return jnp.broadcast_to(v[..., None], v.shape + (LANES,))


def kernel(x_prompt, x_sample, c, cache_k, cache_v, state_c_fwd, state_n_fwd, state_m_fwd, state_c_bwd, state_n_bwd, state_m_bwd, c_ctx, mod_w, mod_b, norm_g, mlp_w1, mlp_w2, even_in_w, conv_a_w, conv_a_b, ln_a_g, ln_a_b, conv_b_w, even_out_w, odd_in_w, attn_sink, gate_b, hnorm_g, odd_out_w):
    n_dec = x_sample.shape[0]
    n_ctx = x_prompt.shape[0]
    cond = jnp.concatenate([c_ctx[None, :], c, jnp.zeros((COND_ROWS - 1 - n_dec, D_MODEL), F32)], axis=0)
    modv = _modulation(cond, mod_w, mod_b)

    yp, ys = x_prompt, x_sample
    w1, w2 = mlp_w1, mlp_w2

    ev = (norm_g, even_in_w, conv_a_w, conv_a_b[:, None, :], ln_a_g[:, None, :], ln_a_b[:, None, :],
          conv_b_w, even_out_w)
    yp = _even_layer(yp, modv, 0, 0, 0, *ev)
    ys = _even_layer(ys, modv, 0, 1, 1, *ev)
    yp = _mlp_layer(yp, modv, 0, 0, 0, norm_g, w1, w2)
    ys = _mlp_layer(ys, modv, 0, 1, 1, norm_g, w1, w2)

    order = jnp.array([0, 2, 1, 3])
    d_main = D_IN_ODD - 4 * N_HEADS
    wg = odd_in_w[0][:, d_main:].reshape(D_MODEL, 4, N_HEADS)[:, order, :].reshape(D_MODEL, 4 * N_HEADS)
    w_gate = jnp.pad(wg, ((0, 0), (0, LANES - 4 * N_HEADS))).astype(BF16)
    gate_bias = jnp.pad(gate_b[0][order, :].reshape(1, 4 * N_HEADS), ((0, 0), (0, LANES - 4 * N_HEADS)))
    sink_b = _lane_bcast(attn_sink[0])
    odd = (1, norm_g, odd_in_w, w_gate, gate_bias, sink_b, hnorm_g[:, None, :], odd_out_w)

    op, k_t, v_t, c_f, c_b, n_new, m_new = _odd_context(yp, modv, *odd)
    new_k, new_v = jnp.swapaxes(k_t, -1, -2), jnp.swapaxes(v_t, -1, -2)

    t_dec = x_sample.shape[1]
    cos_t, sin_t = _rope_tables(t_dec)
    kc = jnp.concatenate([cache_k[:, 0], cache_k[:, 0]], axis=-1)
    vc = jnp.concatenate([cache_v[:, 0], cache_v[:, 0]], axis=-1)
    c_in = jnp.concatenate([_pair_blockdiag(jnp.swapaxes(state_c_fwd[:, 0], -1, -2)),
                            _pair_blockdiag(jnp.swapaxes(state_c_bwd[:, 0], -1, -2))], axis=1)
    n_in = jnp.concatenate([state_n_fwd[:, 0].reshape(n_dec, N_PAIRS, LANES),
                            state_n_bwd[:, 0].reshape(n_dec, N_PAIRS, LANES)], axis=1)
    m_in = _lane_bcast(jnp.concatenate([state_m_fwd[:, 0], state_m_bwd[:, 0]], axis=1))
    os_ = _odd_latent(ys, modv, *odd, cos_t, sin_t, kc, vc, c_in, n_in, m_in)

    yp = _mlp_layer(op, modv, 1, 0, 0, norm_g, w1, w2)
    ys = _mlp_layer(os_, modv, 1, 1, 1, norm_g, w1, w2)

    n_f = n_new[:, :N_PAIRS].reshape(n_ctx, N_HEADS, HEAD_DIM)[:, None]
    n_b = n_new[:, N_PAIRS:].reshape(n_ctx, N_HEADS, HEAD_DIM)[:, None]
    m_f = m_new[:, :N_HEADS, 0][:, None]
    m_b = m_new[:, N_HEADS:, 0][:, None]
    return (yp, ys, new_k, new_v, c_f, n_f, m_f, c_b, n_b, m_b)
```

```python
import functools

import jax
import jax.numpy as jnp
from jax import lax
from jax.experimental import pallas as pl
from jax.experimental.pallas import tpu as pltpu

F32 = jnp.float32
BF16 = jnp.bfloat16

D_MODEL = 1024
D_FF = 4 * D_MODEL
EPS = 1e-6
D_HALF = D_MODEL // 2
CONF_WIDTH = 31
CONF_HALO = 16
HEAD_DIM = 64
N_HEADS = 8
N_PAIRS = N_HEADS // 2
N_KV = 2
LANES = 128
CHUNK = 128
WINDOW = 128
GRID_W = 64
ROPE_BASE = 10000.0
ATT_SCALE = HEAD_DIM ** -0.5
D_IN_ODD = D_HALF + 2 * N_KV * HEAD_DIM + 4 * D_HALF + 4 * N_HEADS
ROW_CHUNK = 256
MLP_ROWS = 512
ODD_CTX_SEQS = 2
ODD_STAGGER = 6
EVEN_CONV_ROWS = 64
EVEN_CHUNKS = 2
EVEN_STAGGER = 3
COND_ROWS = 8
VMEM_LIMIT = 56 * 1024 * 1024


def _dot(a, b):
    return jnp.dot(a, b, preferred_element_type=F32)


def _dot_nt(a, b):
    return lax.dot_general(a, b, (((1,), (1,)), ((), ())), preferred_element_type=F32)


def _split3(x):
    hi = x.astype(BF16)
    r1 = x - hi.astype(F32)
    mid = r1.astype(BF16)
    lo = (r1 - mid.astype(F32)).astype(BF16)
    return hi, mid, lo


def _dot_exact_lhs(x, b01):
    hi, mid, lo = _split3(x)
    return _dot(hi, b01) + _dot(mid, b01) + _dot(lo, b01)


def _rms(x, g):
    return x * lax.rsqrt(jnp.mean(x * x, axis=-1, keepdims=True) + EPS) * g


def _norm_mod(x, g, shift, scale):
    return _rms(x, g) * (1.0 + scale) + shift


def _params(n_grid):
    return pltpu.CompilerParams(dimension_semantics=("arbitrary",) * n_grid, vmem_limit_bytes=VMEM_LIMIT)


def _const_spec(shape):
    zeros = (0,) * len(shape)
    return pl.BlockSpec(shape, lambda *_: zeros, pipeline_mode=pl.Buffered(1))


def _layer_spec(shape, layer):
    index = (layer,) + (0,) * len(shape)
    return pl.BlockSpec((None,) + shape, lambda *_: index, pipeline_mode=pl.Buffered(1))


def _mod_spec(layer, cond_of):
    return pl.BlockSpec((None, None, 6, D_MODEL), lambda *idx: (layer, cond_of(*idx), 0, 0))


def _mod_kernel(cond_ref, w_ref, b_ref, o_ref):
    s = jax.nn.silu(cond_ref[...]).astype(BF16)
    o_ref[...] = _dot(s, w_ref[...].astype(BF16)) + b_ref[...]


def _modulation(cond, mod_w, mod_b):
    depth = mod_w.shape[0]
    n_out = mod_w.shape[2]
    tn = 2 * D_MODEL
    out = pl.pallas_call(
        _mod_kernel,
        grid=(depth, n_out // tn),
        in_specs=[
            pl.BlockSpec((COND_ROWS, D_MODEL), lambda l, j: (0, 0)),
            pl.BlockSpec((None, D_MODEL, tn), lambda l, j: (l, 0, j)),
            pl.BlockSpec((None, 1, tn), lambda l, j: (l, 0, j)),
        ],
        out_specs=pl.BlockSpec((None, COND_ROWS, tn), lambda l, j: (l, 0, j)),
        out_shape=jax.ShapeDtypeStruct((depth, COND_ROWS, n_out), F32),
        compiler_params=_params(2),
        name="modulation",
    )(cond, mod_w, mod_b.reshape(depth, 1, n_out))
    return out.reshape(depth, COND_ROWS, 6, D_MODEL)


def _run_staggered(programs, stagger):
    programs = list(programs)
    live, rounds = [], 0
    while programs or live:
        if programs and rounds % stagger == 0:
            live.append(programs.pop(0))
        for g in list(live):
            try:
                next(g)
            except StopIteration:
                live.remove(g)
        rounds += 1


def _even_kernel(*refs, n_chunks, nsub):
    halos, (xc_ref, mod_ref, ng_ref, win_ref, caw_ref, cab_ref, lng_ref, lnb_ref, cbw_ref, wout_ref, o_ref,
            apad, cpad, bgs, zs) = refs[:2 * nsub], refs[2 * nsub:]
    shared = (mod_ref, ng_ref, win_ref, caw_ref, cab_ref, lng_ref, lnb_ref, cbw_ref, wout_ref)
    _run_staggered(
        (_even_chunk(pl.program_id(0) * nsub + k, halos[2 * k], xc_ref.at[k], halos[2 * k + 1], *shared,
                     o_ref.at[k], apad.at[k], cpad.at[k], bgs.at[k], zs.at[k], n_chunks=n_chunks)
         for k in range(nsub)), EVEN_STAGGER)


def _even_chunk(g, xp_ref, xc_ref, xn_ref, mod_ref, ng_ref, win_ref, caw_ref, cab_ref, lng_ref, lnb_ref,
                cbw_ref, wout_ref, o_ref, apad, cpad, bgs, zs, *, n_chunks):
    c = g % n_chunks if n_chunks > 1 else 0
    rows = ROW_CHUNK + 2 * CONF_HALO
    shift1, scale1, gate1 = mod_ref[0:1, :], mod_ref[1:2, :], mod_ref[2:3, :]
    own = slice(CONF_HALO, CONF_HALO + ROW_CHUNK)
    if n_chunks == 1:
        h = _norm_mod(xc_ref[...], ng_ref[0:1, :], shift1, scale1).astype(BF16)
        h_own = h
        for pad_ref in (apad, cpad):
            pad_ref[0:CONF_HALO, :] = jnp.zeros((CONF_HALO, D_HALF), F32)
            pad_ref[CONF_HALO + ROW_CHUNK:rows, :] = jnp.zeros((CONF_HALO, D_HALF), F32)
        keep = lambda v: v
        span = own
    else:
        xh = jnp.concatenate([xp_ref[...], xc_ref[...], xn_ref[...]], axis=0)
        h = _norm_mod(xh, ng_ref[0:1, :], shift1, scale1).astype(BF16)
        h_own = h[own]
        ri = lax.broadcasted_iota(jnp.int32, (rows, D_HALF), 0)
        lo = jnp.where(c == 0, CONF_HALO, 0)
        hi = jnp.where(c == n_chunks - 1, CONF_HALO + ROW_CHUNK, rows)
        inside = (ri >= lo) & (ri < hi)
        keep = lambda v: jnp.where(inside, v, 0.0)
        span = slice(0, rows)
    a = _dot(h, win_ref[:, 0:D_HALF]) * jax.nn.sigmoid(_dot(h, win_ref[:, D_HALF:2 * D_HALF]))
    apad[span, :] = keep(a)
    yield
    cx = _dot(h, win_ref[:, 3 * D_HALF:4 * D_HALF]) * _dot(h, win_ref[:, 4 * D_HALF:5 * D_HALF])
    cpad[span, :] = keep(cx)
    yield
    bgs[...] = _dot(h_own, win_ref[:, 2 * D_HALF:3 * D_HALF])
    yield

    sub = EVEN_CONV_ROWS
    tile = 8
    for j in range(ROW_CHUNK // sub):
        r0 = j * sub
        groups = []
        for cg in range(D_HALF // LANES):
            cols = slice(cg * LANES, (cg + 1) * LANES)
            acc = None
            for r in range(tile):
                part = None
                for m in range(-(-(CONF_WIDTH + 1) // tile)):
                    o = tile * m + r
                    if 1 <= o <= CONF_WIDTH:
                        term = caw_ref[o - 1:o, cols] * apad[r0 + tile * m:r0 + tile * m + sub + tile, cols]
                        part = term if part is None else part + term
                shifted = part[r:r + sub, :]
                acc = shifted if acc is None else acc + shifted
            groups.append(acc)
        acc = jnp.concatenate(groups, axis=1) + cab_ref[...]
        mu = jnp.mean(acc, axis=-1, keepdims=True)
        dlt = acc - mu
        var = jnp.mean(dlt * dlt, axis=-1, keepdims=True)
        a_out = jax.nn.silu(dlt * lax.rsqrt(var + EPS) * lng_ref[...] + lnb_ref[...])
        zs[r0:r0 + sub, 0:D_HALF] = a_out.astype(BF16)
        base = r0 + CONF_HALO - 1
        sc = (cbw_ref[0:1, :] * cpad[base:base + sub, :]
              + cbw_ref[1:2, :] * cpad[base + 1:base + 1 + sub, :]
              + cbw_ref[2:3, :] * cpad[base + 2:base + 2 + sub, :])
        zs[r0:r0 + sub, D_HALF:D_MODEL] = (bgs[r0:r0 + sub, :] * sc).astype(BF16)
        yield

    o = _dot(zs[...], wout_ref[...])
    o_ref[...] = xc_ref[...] + gate1 * _rms(o, ng_ref[1:2, :])
    yield


def _even_layer(x, modv, layer, cond_base, cond_stride, ng, win, caw, cab, lng, lnb, cbw, wout):
    j = layer // 2
    b, t, _ = x.shape
    n_chunks = t // ROW_CHUNK
    nsub = EVEN_CHUNKS
    assert (b * n_chunks) % nsub == 0 and (cond_stride == 0 or n_chunks % nsub == 0)
    hpc = ROW_CHUNK // CONF_HALO
    n_halo_blocks = b * t // CONF_HALO
    rows = ROW_CHUNK + 2 * CONF_HALO
    halo_specs = []
    for k in range(nsub):
        halo_specs += [
            pl.BlockSpec((None, CONF_HALO, D_MODEL),
                         lambda i, k=k: (jnp.maximum((i * nsub + k) * hpc - 1, 0), 0, 0)),
            pl.BlockSpec((None, CONF_HALO, D_MODEL),
                         lambda i, k=k: (jnp.minimum((i * nsub + k + 1) * hpc, n_halo_blocks - 1), 0, 0)),
        ]
    x_halo = x.reshape(n_halo_blocks, CONF_HALO, D_MODEL)
    x_chunks = x.reshape(b * n_chunks, ROW_CHUNK, D_MODEL)
    kern = functools.partial(_even_kernel, n_chunks=n_chunks, nsub=nsub)
    out = pl.pallas_call(
        kern,
        grid=(b * n_chunks // nsub,),
        in_specs=halo_specs + [
            pl.BlockSpec((nsub, ROW_CHUNK, D_MODEL), lambda i: (i, 0, 0)),
            _mod_spec(layer, lambda i: cond_base + cond_stride * ((i * nsub) // n_chunks)),
            _layer_spec((4, D_MODEL), layer),
            _layer_spec((D_MODEL, 5 * D_HALF), j),
            _layer_spec((CONF_WIDTH, D_HALF), j),
            _layer_spec((1, D_HALF), j),
            _layer_spec((1, D_HALF), j),
            _layer_spec((1, D_HALF), j),
            _layer_spec((3, D_HALF), j),
            _layer_spec((D_MODEL, D_MODEL), j),
        ],
        out_specs=pl.BlockSpec((nsub, ROW_CHUNK, D_MODEL), lambda i: (i, 0, 0)),
        out_shape=jax.ShapeDtypeStruct(x_chunks.shape, F32),
        scratch_shapes=[
            pltpu.VMEM((nsub, rows, D_HALF), F32),
            pltpu.VMEM((nsub, rows, D_HALF), F32),
            pltpu.VMEM((nsub, ROW_CHUNK, D_HALF), F32),
            pltpu.VMEM((nsub, ROW_CHUNK, D_MODEL), BF16),
        ],
        compiler_params=_params(1),
        name="even_mixer",
    )(*([x_halo] * (2 * nsub)), x_chunks, modv, ng, win, caw, cab, lng, lnb, cbw, wout)
    return out.reshape(b, t, D_MODEL)


def _mlp_kernel(x_ref, mod_ref, ng_ref, w1_hbm, w2_hbm, o_ref, w1_s, w2_s, stage, sem, *, layer):
    n_blocks = D_FF // D_MODEL

    def block_copy(k):
        c, slot = k // 2, k % 2
        span = pl.ds(c * D_MODEL, D_MODEL)
        src = w1_hbm.at[layer, :, span] if k % 2 == 0 else w2_hbm.at[layer, span, :]
        return pltpu.make_async_copy(src, stage.at[slot], sem.at[slot])

    def fetch(k, dst):
        block_copy(k).wait()
        dst[...] = stage[k % 2].astype(BF16)
        if k + 2 < 2 * n_blocks:
            block_copy(k + 2).start()

    def tile(load):
        x = x_ref[...]
        h = _norm_mod(x, ng_ref[2:3, :], mod_ref[3:4, :], mod_ref[4:5, :]).astype(BF16)
        acc = None
        for c in range(n_blocks):
            cols = slice(c * D_MODEL, (c + 1) * D_MODEL)
            if load:
                fetch(2 * c, w1_s.at[:, cols])
            hid = jnp.square(jnp.maximum(_dot(h, w1_s[:, cols]), 0.0)).astype(BF16)
            if load:
                fetch(2 * c + 1, w2_s.at[cols, :])
            part = _dot(hid, w2_s[cols, :])
            acc = part if acc is None else acc + part
        o_ref[...] = x + mod_ref[5:6, :] * _rms(acc, ng_ref[3:4, :])

    first = pl.program_id(0) == 0

    @pl.when(first)
    def _():
        block_copy(0).start()
        block_copy(1).start()
        tile(load=True)

    @pl.when(jnp.logical_not(first))
    def _():
        tile(load=False)


def _mlp_layer(x, modv, layer, cond_base, cond_stride, ng, w1, w2):
    b, t, _ = x.shape
    tm = MLP_ROWS
    assert (b * t) % tm == 0 and (cond_stride == 0 or t % tm == 0)
    tiles_per_seq = max(t // tm, 1)
    x2 = x.reshape(b * t, D_MODEL)
    out = pl.pallas_call(
        functools.partial(_mlp_kernel, layer=layer),
        grid=(b * t // tm,),
        in_specs=[
            pl.BlockSpec((tm, D_MODEL), lambda i: (i, 0)),
            _mod_spec(layer, lambda i: cond_base + cond_stride * (i // tiles_per_seq)),
            _layer_spec((4, D_MODEL), layer),
            pl.BlockSpec(memory_space=pl.ANY),
            pl.BlockSpec(memory_space=pl.ANY),
        ],
        out_specs=pl.BlockSpec((tm, D_MODEL), lambda i: (i, 0)),
        out_shape=jax.ShapeDtypeStruct(x2.shape, F32),
        scratch_shapes=[
            pltpu.VMEM((D_MODEL, D_FF), BF16),
            pltpu.VMEM((D_FF, D_MODEL), BF16),
            pltpu.VMEM((2, D_MODEL, D_MODEL), F32),
            pltpu.SemaphoreType.DMA((2,)),
        ],
        compiler_params=_params(1),
        name="mlp",
    )(x2, modv, ng, w1, w2)
    return out.reshape(b, t, D_MODEL)


def _log_sigmoid(x):
    return jnp.minimum(x, 0.0) - jnp.log(1.0 + jnp.exp(-jnp.abs(x)))


def _rot_half(x, first_half):
    return jnp.where(first_half, pltpu.roll(x, 96, axis=1), pltpu.roll(x, 32, axis=1))


def _values_and_ones(v):
    vt = jnp.transpose(v)
    row = lax.broadcasted_iota(jnp.int32, vt.shape, 0)
    return jnp.where(row < HEAD_DIM, vt, 1.0)


def _aligned(i, m):
    return i * m if isinstance(i, int) else pl.multiple_of(i * m, m)


def _drain(pieces):
    for _ in pieces:
        pass


def _each(body, n, static, unroll=1):
    if static:
        for i in range(n):
            yield from body(i)
    else:
        lax.fori_loop(0, n, lambda i, c: (_drain(body(i)), c)[1], 0, unroll=unroll)


def _alternate(*programs):
    live = list(programs)
    while live:
        for g in list(live):
            try:
                next(g)
            except StopIteration:
                live.remove(g)
            else:
                yield


def _odd_kernel(*refs, t, lc, latent, nseq):
    shared = set(range(1, 11 if latent else 9))
    _run_staggered((_odd_seq(*[r if k in shared else r.at[sq] for k, r in enumerate(refs)],
                             t=t, lc=lc, latent=latent) for sq in range(nseq)), ODD_STAGGER)


def _odd_seq(*refs, t, lc, latent):
    if latent:
        (x_ref, mod_ref, ng_ref, win_ref, wg_ref, gb_ref, sink_ref, hn_ref, wout_ref,
         cos_ref, sin_ref, kc_ref, vc_ref, cin_ref, nin_ref, min_ref,
         o_ref,
         qa_s, qb_s, kk_s, vvt_s, qma_s, qmb_s, qm_s, km_s, vmt_s, vmtf_s, om_s, g_s, z_s, hft_s, hbt_s,
         sr_s, st_s, qc_s, c_s, n_s, m_s, s_s, p_s, kc_s, vct_s) = refs
    else:
        (x_ref, mod_ref, ng_ref, win_ref, wg_ref, gb_ref, sink_ref, hn_ref, wout_ref,
         o_ref, ko_ref, vo_ref, cf_ref, cb_ref, no_ref, mo_ref,
         qa_s, qb_s, kk_s, vvt_s, qma_s, qmb_s, qm_s, km_s, vmt_s, vmtf_s, om_s, g_s, z_s, hft_s, hbt_s,
         sr_s, st_s, qc_s, c_s, n_s, m_s, s_s, p_s) = refs

    static = not latent
    n_blocks = t // CHUNK
    pad = CHUNK if latent else 0
    shift1, scale1, gate1 = mod_ref[0:1, :], mod_ref[1:2, :], mod_ref[2:3, :]
    lane = lax.broadcasted_iota(jnp.int32, (1, LANES), 1)
    left = lane < HEAD_DIM
    first_half = (lane % HEAD_DIM) < (HEAD_DIM // 2)
    ti = lax.broadcasted_iota(jnp.int32, (CHUNK, CHUNK), 0)
    si = lax.broadcasted_iota(jnp.int32, (CHUNK, CHUNK), 1)
    top = ti < HEAD_DIM
    same_head = top == (si < HEAD_DIM)

    if latent:
        for kv in range(N_KV):
            kk_s[kv, 0:CHUNK, :] = jnp.zeros((CHUNK, LANES), BF16)
            kk_s[kv, CHUNK + t:2 * CHUNK + t, :] = jnp.zeros((CHUNK, LANES), BF16)
            vvt_s[kv, :, 0:CHUNK] = jnp.zeros((LANES, CHUNK), BF16)
            vvt_s[kv, :, CHUNK + t:2 * CHUNK + t] = jnp.zeros((LANES, CHUNK), BF16)
            kc_s[kv] = kc_ref[kv].astype(BF16)
            vct_s[kv] = _values_and_ones(vc_ref[kv]).astype(BF16)

    def project(rc):
        r0 = _aligned(rc, ROW_CHUNK)
        rows = pl.ds(r0, ROW_CHUNK)
        krows = pl.ds(r0 + pad, ROW_CHUNK)
        h = _norm_mod(x_ref[rows, :], ng_ref[0:1, :], shift1, scale1).astype(BF16)
        if latent:
            cs, sn = cos_ref[rows, :], sin_ref[rows, :]
        q = _dot(h, win_ref[:, 0:D_HALF])
        for p in range(N_PAIRS):
            cols = slice(p * LANES, (p + 1) * LANES)
            qp = q[:, cols]
            if latent:
                qp = qp * cs + _rot_half(qp, first_half) * sn
            qp = qp * ATT_SCALE
            qa_s[rows, cols] = jnp.where(left, qp, 0.0).astype(BF16)
            qb_s[rows, cols] = jnp.where(left, 0.0, qp).astype(BF16)
        yield
        kv2 = _dot(h, win_ref[:, D_HALF:D_HALF + 2 * LANES])
        ka, va = kv2[:, 0:LANES], kv2[:, LANES:2 * LANES]
        if latent:
            ka = ka * cs + _rot_half(ka, first_half) * sn
        kr = pltpu.roll(ka, HEAD_DIM, axis=1)
        vr = pltpu.roll(va, HEAD_DIM, axis=1)
        if not latent:
            ka_t, va_t = jnp.transpose(ka), jnp.transpose(va)
            for kv in range(N_KV):
                ko_ref[0, kv, :, rows] = ka_t[kv * HEAD_DIM:(kv + 1) * HEAD_DIM, :]
                vo_ref[0, kv, :, rows] = va_t[kv * HEAD_DIM:(kv + 1) * HEAD_DIM, :]
        kk_s[0, krows, :] = jnp.where(left, ka, kr).astype(BF16)
        kk_s[1, krows, :] = jnp.where(left, kr, ka).astype(BF16)
        vvt_s[0, :, krows] = _values_and_ones(va).astype(BF16)
        vvt_s[1, :, krows] = _values_and_ones(vr).astype(BF16)
        yield
        base = D_HALF + 2 * LANES
        qm = _dot(h, win_ref[:, base:base + D_HALF])
        qm_s[rows, :] = qm.astype(BF16)
        for p in range(N_PAIRS):
            cols = slice(p * LANES, (p + 1) * LANES)
            qma_s[rows, cols] = jnp.where(left, qm[:, cols], 0.0).astype(BF16)
            qmb_s[rows, cols] = jnp.where(left, 0.0, qm[:, cols]).astype(BF16)
        yield
        vm = _dot(h, win_ref[:, base + 2 * D_HALF:base + 3 * D_HALF])
        for p in range(N_PAIRS):
            cols = slice(p * LANES, (p + 1) * LANES)
            vt = jnp.transpose(vm[:, cols])
            vmtf_s[cols, rows] = vt
            vmt_s[cols, rows] = vt.astype(BF16)
        yield
        km = _dot(h, win_ref[:, base + D_HALF:base + 2 * D_HALF]) * (HEAD_DIM ** -0.5)
        km_s[rows, :] = km.astype(BF16)
        yield
        om_s[rows, :] = _dot(h, win_ref[:, base + 3 * D_HALF:base + 4 * D_HALF])
        g_s[rows, :] = _dot(h, wg_ref[...]) + gb_ref[...]
        yield

    yield from _each(project, t // ROW_CHUNK, static)

    kj = lax.broadcasted_iota(jnp.int32, (3 * CHUNK, 2 * CHUNK), 0)
    qi = lax.broadcasted_iota(jnp.int32, (3 * CHUNK, 2 * CHUNK), 1) % CHUNK
    band_ok = jnp.abs(kj - CHUNK - qi) <= WINDOW
    head_a = lax.broadcasted_iota(jnp.int32, (1, 2 * CHUNK), 1) < CHUNK

    n_keys = _att_keys(lc, latent)
    group = _att_group(latent)

    def attention():
        def attend(i):
            r0 = _aligned(i, CHUNK)
            rows = pl.ds(r0, CHUNK)
            if latent:
                key_pos = kj + (i - 1) * CHUNK
                mask = band_ok & (key_pos >= 0) & (key_pos < t)
                win = pl.ds(r0, 3 * CHUNK)
            for g0 in range(0, N_PAIRS, group):
                for gi in range(group):
                    p = g0 + gi
                    kv = p // (N_PAIRS // N_KV)
                    cols = slice(p * LANES, (p + 1) * LANES)
                    q2 = jnp.concatenate([qa_s[rows, cols], qb_s[rows, cols]], axis=0)
                    if latent:
                        s_s[gi, 0:lc, :] = _dot_nt(kc_s[kv], q2)
                        s_s[gi, lc:n_keys, :] = jnp.where(mask, _dot_nt(kk_s[kv, win, :], q2), -jnp.inf)
                    else:
                        s_s[gi] = _dot_nt(kk_s[kv], q2)
                yield
                maxes = []
                for gi in range(group):
                    p = g0 + gi
                    sink = jnp.where(head_a, sink_ref[2 * p:2 * p + 1, 0:1], sink_ref[2 * p + 1:2 * p + 2, 0:1])
                    mx = jnp.maximum(jnp.max(s_s[gi], axis=0, keepdims=True), sink)
                    p_s[gi] = jnp.exp(s_s[gi] - mx).astype(BF16)
                    maxes.append((sink, mx))
                yield
                for gi in range(group):
                    p = g0 + gi
                    kv = p // (N_PAIRS // N_KV)
                    cols = slice(p * LANES, (p + 1) * LANES)
                    if latent:
                        num = (_dot(vct_s[kv], p_s[gi, 0:lc, :]) + _dot(vvt_s[kv, :, win], p_s[gi, lc:n_keys, :]))
                    else:
                        num = _dot(vvt_s[kv], p_s[gi])
                    sink, mx = maxes[gi]
                    den = num[HEAD_DIM:HEAD_DIM + 1, :] + jnp.exp(sink - mx)
                    out = num[0:HEAD_DIM, :] * (1.0 / den)
                    pair = jnp.concatenate([out[:, 0:CHUNK], out[:, CHUNK:2 * CHUNK]], axis=0)
                    z_s[rows, cols] = jnp.transpose(pair).astype(BF16)
                yield

        return attend

    if latent:
        c_s[...] = cin_ref[...]
        n_s[...] = nin_ref[...]
        m_s[...] = min_ref[...]
    else:
        c_s[...] = jnp.zeros(c_s.shape, F32)
        n_s[...] = jnp.zeros(n_s.shape, F32)
        m_s[...] = jnp.zeros(m_s.shape, F32)

    see = (ti <= si, ti >= si)
    tri = tuple(m.astype(F32).astype(BF16) for m in see)
    last = (CHUNK - 1, 0)

    def mlstm(i):
        offs = (_aligned(i, CHUNK), _aligned(n_blocks - 1 - i, CHUNK))
        half = (slice(0, LANES), slice(LANES, 2 * LANES))
        a_rows, b_rows = [], []
        for d in range(2):
            gt = jnp.transpose(g_s[pl.ds(offs[d], CHUNK), :])
            lf = _log_sigmoid(gt[2 * N_HEADS:4 * N_HEADS, :])
            bcum = _dot_exact_lhs(lf, tri[d])[8 * d:8 * d + 8, :]
            a_rows.append(gt[8 * d:8 * d + 8, :] - bcum)
            b_rows.append(bcum)
        yield
        q_ns = []
        for d in range(2):
            rows = pl.ds(offs[d], CHUNK)
            for p in range(N_PAIRS):
                u = d * N_PAIRS + p
                cols = slice(p * LANES, (p + 1) * LANES)
                q2 = jnp.concatenate([qma_s[rows, cols], qmb_s[rows, cols]], axis=0)
                sr_s[u] = _dot_nt(km_s[rows, cols], q2)
                qc_s[u] = _dot_nt(c_s[u].astype(BF16), qm_s[rows, cols])
                n8 = jnp.broadcast_to(n_s[u:u + 1, :], (8, LANES)).astype(BF16)
                q_ns.append(_dot_nt(n8, q2)[0:1, :])
            yield
        stats = {}
        for d in range(2):
            for hd in range(N_HEADS):
                u, j = d * N_PAIRS + hd // 2, hd % 2
                mrow = d * N_HEADS + hd
                a_row = a_rows[d][hd:hd + 1, :]
                b_row = b_rows[d][hd:hd + 1, :]
                m_prev = m_s[mrow:mrow + 1, 0:1]
                a_col = jnp.transpose(jnp.broadcast_to(a_row, (CHUNK, CHUNK)))
                z_t = jnp.where(see[d], a_col, -jnp.inf)
                m_run = jnp.maximum(jnp.max(z_t, axis=0, keepdims=True), m_prev)
                s_t = sr_s[u, :, half[j]] * jnp.exp(z_t - m_run)
                st_s[u, :, half[j]] = s_t.astype(BF16)
                w_int = jnp.exp(m_prev - m_run)
                den = jnp.sum(s_t, axis=0, keepdims=True) + w_int * q_ns[u][:, half[j]]
                inv = 1.0 / jnp.maximum(jnp.abs(den), jnp.exp(-(b_row + m_run)))
                m_last = m_run[:, last[d]:last[d] + 1]
                stats[(u, j)] = (w_int, inv, jnp.exp(a_row - m_last), jnp.exp(m_prev - m_last))
                m_s[mrow:mrow + 1, :] = jnp.broadcast_to(b_row[:, last[d]:last[d] + 1] + m_last, (1, LANES))
            yield
        for d in range(2):
            rows = pl.ds(offs[d], CHUNK)
            ht_s = (hft_s, hbt_s)[d]
            for p in range(N_PAIRS):
                u = d * N_PAIRS + p
                cols = slice(p * LANES, (p + 1) * LANES)
                (w_a, inv_a, e_a, dec_a), (w_b, inv_b, e_b, dec_b) = stats[(u, 0)], stats[(u, 1)]
                kp = km_s[rows, cols]
                num2 = _dot(vmt_s[cols, rows], st_s[u])
                num = jnp.where(top, num2[:, half[0]], num2[:, half[1]])
                ht_s[cols, rows] = (num + jnp.where(top, w_a, w_b) * qc_s[u]) * jnp.where(top, inv_a, inv_b)
                vt_e = (vmtf_s[cols, rows] * jnp.where(top, e_a, e_b)).astype(BF16)
                c_s[u] = jnp.where(top, dec_a, dec_b) * c_s[u] + jnp.where(same_head, _dot(vt_e, kp), 0.0)
                e2 = jnp.concatenate([e_a, e_b, jnp.zeros((6, CHUNK), F32)], axis=0).astype(BF16)
                n_k = _dot(e2, kp)
                n_s[u:u + 1, :] = (jnp.where(left, dec_a, dec_b) * n_s[u:u + 1, :]
                                   + jnp.where(left, n_k[0:1, :], n_k[1:2, :]))
            yield

    attend = attention()
    yield from _each(lambda i: _alternate(attend(i), mlstm(i)), n_blocks, static, unroll=2)

    if not latent:
        for d, c_ref in enumerate((cf_ref, cb_ref)):
            for p in range(N_PAIRS):
                c_pair = jnp.transpose(c_s[d * N_PAIRS + p])
                c_ref[0, 2 * p] = c_pair[0:HEAD_DIM, 0:HEAD_DIM]
                c_ref[0, 2 * p + 1] = c_pair[HEAD_DIM:LANES, HEAD_DIM:LANES]
        no_ref[...] = n_s[...]
        mo_ref[...] = m_s[...]
        yield

    top_w = lax.broadcasted_iota(jnp.int32, (LANES, ROW_CHUNK), 0) < HEAD_DIM

    def finish(rc):
        r0 = _aligned(rc, ROW_CHUNK)
        rows = pl.ds(r0, ROW_CHUNK)
        for p in range(N_PAIRS):
            cols = slice(p * LANES, (p + 1) * LANES)
            hm = hft_s[cols, rows] + hbt_s[cols, rows]
            sq = hm * hm
            ms_a = jnp.sum(sq[0:HEAD_DIM], axis=0, keepdims=True)
            ms_b = jnp.sum(sq[HEAD_DIM:LANES], axis=0, keepdims=True)
            ms = jnp.where(top_w, ms_a, ms_b) * (1.0 / HEAD_DIM)
            y = jnp.transpose(hm * lax.rsqrt(ms + EPS)) * hn_ref[:, cols] * jax.nn.sigmoid(om_s[rows, cols])
            z_s[rows, D_HALF + p * LANES:D_HALF + (p + 1) * LANES] = y.astype(BF16)
            if p % 2:
                yield
        o = _dot(z_s[rows, :], wout_ref[...])
        o_ref[rows, :] = x_ref[rows, :] + gate1 * _rms(o, ng_ref[1:2, :])
        yield

    yield from _each(finish, t // ROW_CHUNK, static)


def _att_keys(lc, latent):
    return lc + 3 * CHUNK if latent else lc


def _att_group(latent):
    return 2 if latent else N_PAIRS


def _odd_scratch(t, lc, latent, nseq):
    pad = 2 * CHUNK if latent else 0
    att = (_att_group(latent), _att_keys(lc, latent), 2 * CHUNK)
    shapes = [
        pltpu.VMEM((t, D_HALF), BF16),
        pltpu.VMEM((t, D_HALF), BF16),
        pltpu.VMEM((N_KV, t + pad, LANES), BF16),
        pltpu.VMEM((N_KV, LANES, t + pad), BF16),
        pltpu.VMEM((t, D_HALF), BF16),
        pltpu.VMEM((t, D_HALF), BF16),
        pltpu.VMEM((t, D_HALF), BF16),
        pltpu.VMEM((t, D_HALF), BF16),
        pltpu.VMEM((D_HALF, t), BF16),
        pltpu.VMEM((D_HALF, t), F32),
        pltpu.VMEM((t, D_HALF), F32),
        pltpu.VMEM((t, LANES), F32),
        pltpu.VMEM((t, D_MODEL), BF16),
        pltpu.VMEM((D_HALF, t), F32),
        pltpu.VMEM((D_HALF, t), F32),
        pltpu.VMEM((2 * N_PAIRS, CHUNK, 2 * LANES), F32),
        pltpu.VMEM((2 * N_PAIRS, CHUNK, 2 * LANES), BF16),
        pltpu.VMEM((2 * N_PAIRS, LANES, CHUNK), F32),
        pltpu.VMEM((2 * N_PAIRS, LANES, LANES), F32),
        pltpu.VMEM((2 * N_PAIRS, LANES), F32),
        pltpu.VMEM((2 * N_HEADS, LANES), F32),
        pltpu.VMEM(att, F32),
        pltpu.VMEM(att, BF16),
    ]
    if latent:
        shapes += [pltpu.VMEM((N_KV, lc, LANES), BF16), pltpu.VMEM((N_KV, LANES, lc), BF16)]
    return [pltpu.VMEM((nseq,) + tuple(sh.shape), sh.dtype) for sh in shapes]


def _odd_common_specs(t, layer, cond_base, cond_stride, nseq):
    assert cond_stride == 0 or nseq == 1
    j = layer // 2
    return [
        _per_seq((t, D_MODEL), nseq),
        _mod_spec(layer, lambda i: cond_base + cond_stride * i),
        _layer_spec((4, D_MODEL), layer),
        _layer_spec((D_MODEL, D_IN_ODD), j),
        _const_spec((D_MODEL, LANES)),
        _const_spec((1, LANES)),
        _const_spec((N_HEADS, LANES)),
        _layer_spec((1, D_HALF), j),
        _layer_spec((D_MODEL, D_MODEL), j),
    ]


def _per_seq(shape, nseq):
    return pl.BlockSpec((nseq,) + shape, lambda i: (i,) + (0,) * len(shape))


def _odd_context(x, modv, layer, ng, w_main, w_gate, gate_bias, sink_b, hnorm, wout):
    b, t, _ = x.shape
    nseq = ODD_CTX_SEQS
    assert b % nseq == 0
    kern = functools.partial(_odd_kernel, t=t, lc=t, latent=False, nseq=nseq)
    per_seq = functools.partial(_per_seq, nseq=nseq)
    return pl.pallas_call(
        kern,
        grid=(b // nseq,),
        in_specs=_odd_common_specs(t, layer, 0, 0, nseq),
        out_specs=[per_seq((t, D_MODEL)),
                   per_seq((1, N_KV, HEAD_DIM, t)), per_seq((1, N_KV, HEAD_DIM, t)),
                   per_seq((1, N_HEADS, HEAD_DIM, HEAD_DIM)), per_seq((1, N_HEADS, HEAD_DIM, HEAD_DIM)),
                   per_seq((2 * N_PAIRS, LANES)), per_seq((2 * N_HEADS, LANES))],
        out_shape=[jax.ShapeDtypeStruct((b, t, D_MODEL), F32),
                   jax.ShapeDtypeStruct((b, 1, N_KV, HEAD_DIM, t), F32),
                   jax.ShapeDtypeStruct((b, 1, N_KV, HEAD_DIM, t), F32),
                   jax.ShapeDtypeStruct((b, 1, N_HEADS, HEAD_DIM, HEAD_DIM), F32),
                   jax.ShapeDtypeStruct((b, 1, N_HEADS, HEAD_DIM, HEAD_DIM), F32),
                   jax.ShapeDtypeStruct((b, 2 * N_PAIRS, LANES), F32),
                   jax.ShapeDtypeStruct((b, 2 * N_HEADS, LANES), F32)],
        scratch_shapes=_odd_scratch(t, t, False, nseq),
        compiler_params=_params(1),
        name="odd_mixer_context",
    )(x, modv, ng, w_main, w_gate, gate_bias, sink_b, hnorm, wout)


def _odd_latent(x, modv, layer, ng, w_main, w_gate, gate_bias, sink_b, hnorm, wout, cos_t, sin_t, kc, vc,
                c_in, n_in, m_in):
    b, t, _ = x.shape
    lc = kc.shape[2]
    kern = functools.partial(_odd_kernel, t=t, lc=lc, latent=True, nseq=1)
    per_seq = functools.partial(_per_seq, nseq=1)
    return pl.pallas_call(
        kern,
        grid=(b,),
        in_specs=_odd_common_specs(t, layer, 1, 1, 1) + [
            _const_spec((t, LANES)), _const_spec((t, LANES)),
            per_seq((N_KV, lc, LANES)), per_seq((N_KV, lc, LANES)),
            per_seq((2 * N_PAIRS, LANES, LANES)), per_seq((2 * N_PAIRS, LANES)), per_seq((2 * N_HEADS, LANES)),
        ],
        out_specs=per_seq((t, D_MODEL)),
        out_shape=jax.ShapeDtypeStruct((b, t, D_MODEL), F32),
        scratch_shapes=_odd_scratch(t, lc, True, 1),
        compiler_params=_params(1),
        name="odd_mixer_latent",
    )(x, modv, ng, w_main, w_gate, gate_bias, sink_b, hnorm, wout, cos_t, sin_t, kc, vc, c_in, n_in, m_in)


def _rope_tables(t):
    rows = t // GRID_W
    row = jnp.broadcast_to(jnp.arange(rows)[:, None], (rows, GRID_W)).reshape(t).astype(F32)
    col = jnp.broadcast_to(jnp.arange(GRID_W)[None, :], (rows, GRID_W)).reshape(t).astype(F32)
    n_freq = HEAD_DIM // 4
    inv_freq = ROPE_BASE ** (-jnp.arange(n_freq, dtype=F32) / n_freq)
    ang = jnp.concatenate([row[:, None] * inv_freq, col[:, None] * inv_freq], axis=-1)
    cos, sin = jnp.cos(ang), jnp.sin(ang)
    cos_l = jnp.tile(cos, (1, LANES // cos.shape[1]))
    sin_l = jnp.tile(jnp.concatenate([-sin, sin], axis=-1), (1, LANES // HEAD_DIM))
    return cos_l, sin_l


def _pair_blockdiag(c):
    b = c.shape[0]
    c = c.reshape(b, N_PAIRS, 2, HEAD_DIM, HEAD_DIM)
    z = jnp.zeros_like(c[:, :, 0])
    top = jnp.concatenate([c[:, :, 0], z], axis=-1)
    bot = jnp.concatenate([z, c[:, :, 1]], axis=-1)
    return jnp.concatenate([top, bot], axis=-2)


def _lane_bcast(v):
    return jnp.broadcast_to(v[..., None], v.shape + (LANES,))


def kernel(x_prompt, x_sample, c, cache_k, cache_v, state_c_fwd, state_n_fwd, state_m_fwd, state_c_bwd, state_n_bwd, state_m_bwd, c_ctx, mod_w, mod_b, norm_g, mlp_w1, mlp_w2, even_in_w, conv_a_w, conv_a_b, ln_a_g, ln_a_b, conv_b_w, even_out_w, odd_in_w, attn_sink, gate_b, hnorm_g, odd_out_w):
    n_dec = x_sample.shape[0]
    n_ctx = x_prompt.shape[0]
    cond = jnp.concatenate([c_ctx[None, :], c, jnp.zeros((COND_ROWS - 1 - n_dec, D_MODEL), F32)], axis=0)
    modv = _modulation(cond, mod_w, mod_b)

    yp, ys = x_prompt, x_sample
    w1, w2 = mlp_w1, mlp_w2

    ev = (norm_g, even_in_w.astype(BF16), conv_a_w, conv_a_b[:, None, :], ln_a_g[:, None, :], ln_a_b[:, None, :],
          conv_b_w, even_out_w.astype(BF16))
    yp = _even_layer(yp, modv, 0, 0, 0, *ev)
    ys = _even_layer(ys, modv, 0, 1, 1, *ev)
    yp = _mlp_layer(yp, modv, 0, 0, 0, norm_g, w1, w2)
    ys = _mlp_layer(ys, modv, 0, 1, 1, norm_g, w1, w2)

    order = jnp.array([0, 2, 1, 3])
    d_main = D_IN_ODD - 4 * N_HEADS
    wg = odd_in_w[0][:, d_main:].reshape(D_MODEL, 4, N_HEADS)[:, order, :].reshape(D_MODEL, 4 * N_HEADS)
    w_gate = jnp.pad(wg, ((0, 0), (0, LANES - 4 * N_HEADS))).astype(BF16)
    gate_bias = jnp.pad(gate_b[0][order, :].reshape(1, 4 * N_HEADS), ((0, 0), (0, LANES - 4 * N_HEADS)))
    sink_b = _lane_bcast(attn_sink[0])
    odd = (1, norm_g, odd_in_w.astype(BF16), w_gate, gate_bias, sink_b, hnorm_g[:, None, :], odd_out_w.astype(BF16))

    op, k_t, v_t, c_f, c_b, n_new, m_new = _odd_context(yp, modv, *odd)
    new_k, new_v = jnp.swapaxes(k_t, -1, -2), jnp.swapaxes(v_t, -1, -2)

    t_dec = x_sample.shape[1]
    cos_t, sin_t = _rope_tables(t_dec)
    kc = jnp.concatenate([cache_k[:, 0], cache_k[:, 0]], axis=-1)
    vc = jnp.concatenate([cache_v[:, 0], cache_v[:, 0]], axis=-1)
    c_in = jnp.concatenate([_pair_blockdiag(jnp.swapaxes(state_c_fwd[:, 0], -1, -2)),
                            _pair_blockdiag(jnp.swapaxes(state_c_bwd[:, 0], -1, -2))], axis=1)
    n_in = jnp.concatenate([state_n_fwd[:, 0].reshape(n_dec, N_PAIRS, LANES),
                            state_n_bwd[:, 0].reshape(n_dec, N_PAIRS, LANES)], axis=1)
    m_in = _lane_bcast(jnp.concatenate([state_m_fwd[:, 0], state_m_bwd[:, 0]], axis=1))
    os_ = _odd_latent(ys, modv, *odd, cos_t, sin_t, kc, vc, c_in, n_in, m_in)

    yp = _mlp_layer(op, modv, 1, 0, 0, norm_g, w1, w2)
    ys = _mlp_layer(os_, modv, 1, 1, 1, norm_g, w1, w2)

    n_f = n_new[:, :N_PAIRS].reshape(n_ctx, N_HEADS, HEAD_DIM)[:, None]
    n_b = n_new[:, N_PAIRS:].reshape(n_ctx, N_HEADS, HEAD_DIM)[:, None]
    m_f = m_new[:, :N_HEADS, 0][:, None]
    m_b = m_new[:, N_HEADS:, 0][:, None]
    return (yp, ys, new_k, new_v, c_f, n_f, m_f, c_b, n_b, m_b)
```

```python
import functools

import jax
import jax.numpy as jnp
from jax import lax
from jax.experimental import pallas as pl
from jax.experimental.pallas import tpu as pltpu

F32 = jnp.float32
BF16 = jnp.bfloat16

D_MODEL = 1024
D_FF = 4 * D_MODEL
EPS = 1e-6
D_HALF = D_MODEL // 2
CONF_WIDTH = 31
CONF_HALO = 16
HEAD_DIM = 64
N_HEADS = 8
N_PAIRS = N_HEADS // 2
N_KV = 2
LANES = 128
CHUNK = 128
WINDOW = 128
GRID_W = 64
ROPE_BASE = 10000.0
ATT_SCALE = HEAD_DIM ** -0.5
D_IN_ODD = D_HALF + 2 * N_KV * HEAD_DIM + 4 * D_HALF + 4 * N_HEADS
ROW_CHUNK = 256
MLP_ROWS = 512
ODD_CTX_SEQS = 2
ODD_STAGGER = 6
EVEN_CONV_ROWS = 64
EVEN_CHUNKS = 2
EVEN_STAGGER = 3
COND_ROWS = 8
VMEM_LIMIT = 56 * 1024 * 1024


def _dot(a, b):
    return jnp.dot(a, b, preferred_element_type=F32)


def _dot_nt(a, b):
    return lax.dot_general(a, b, (((1,), (1,)), ((), ())), preferred_element_type=F32)


def _split3(x):
    hi = x.astype(BF16)
    r1 = x - hi.astype(F32)
    mid = r1.astype(BF16)
    lo = (r1 - mid.astype(F32)).astype(BF16)
    return hi, mid, lo


def _dot_exact_lhs(x, b01):
    hi, mid, lo = _split3(x)
    return _dot(hi, b01) + _dot(mid, b01) + _dot(lo, b01)


def _rms(x, g):
    return x * lax.rsqrt(jnp.mean(x * x, axis=-1, keepdims=True) + EPS) * g


def _norm_mod(x, g, shift, scale):
    return _rms(x, g) * (1.0 + scale) + shift


def _params(n_grid):
    return pltpu.CompilerParams(dimension_semantics=("arbitrary",) * n_grid, vmem_limit_bytes=VMEM_LIMIT)


def _const_spec(shape):
    zeros = (0,) * len(shape)
    return pl.BlockSpec(shape, lambda *_: zeros, pipeline_mode=pl.Buffered(1))


def _layer_spec(shape, layer):
    index = (layer,) + (0,) * len(shape)
    return pl.BlockSpec((None,) + shape, lambda *_: index, pipeline_mode=pl.Buffered(1))


def _mod_spec(layer, cond_of):
    return pl.BlockSpec((None, None, 6, D_MODEL), lambda *idx: (layer, cond_of(*idx), 0, 0))


def _mod_kernel(cond_ref, w_ref, b_ref, o_ref):
    s = jax.nn.silu(cond_ref[...]).astype(BF16)
    o_ref[...] = _dot(s, w_ref[...].astype(BF16)) + b_ref[...]


def _modulation(cond, mod_w, mod_b):
    depth = mod_w.shape[0]
    n_out = mod_w.shape[2]
    tn = 2 * D_MODEL
    out = pl.pallas_call(
        _mod_kernel,
        grid=(depth, n_out // tn),
        in_specs=[
            pl.BlockSpec((COND_ROWS, D_MODEL), lambda l, j: (0, 0)),
            pl.BlockSpec((None, D_MODEL, tn), lambda l, j: (l, 0, j)),
            pl.BlockSpec((None, 1, tn), lambda l, j: (l, 0, j)),
        ],
        out_specs=pl.BlockSpec((None, COND_ROWS, tn), lambda l, j: (l, 0, j)),
        out_shape=jax.ShapeDtypeStruct((depth, COND_ROWS, n_out), F32),
        compiler_params=_params(2),
        name="modulation",
    )(cond, mod_w, mod_b.reshape(depth, 1, n_out))
    return out.reshape(depth, COND_ROWS, 6, D_MODEL)


def _run_staggered(programs, stagger):
    programs = list(programs)
    live, rounds = [], 0
    while programs or live:
        if programs and rounds % stagger == 0:
            live.append(programs.pop(0))
        for g in list(live):
            try:
                next(g)
            except StopIteration:
                live.remove(g)
        rounds += 1


def _even_kernel(*refs, n_chunks, nsub):
    halos, (xc_ref, mod_ref, ng_ref, win_ref, caw_ref, cab_ref, lng_ref, lnb_ref, cbw_ref, wout_ref, o_ref,
            apad, cpad, bgs, zs) = refs[:2 * nsub], refs[2 * nsub:]
    shared = (mod_ref, ng_ref, win_ref, caw_ref, cab_ref, lng_ref, lnb_ref, cbw_ref, wout_ref)
    _run_staggered(
        (_even_chunk(pl.program_id(0) * nsub + k, halos[2 * k], xc_ref.at[k], halos[2 * k + 1], *shared,
                     o_ref.at[k], apad.at[k], cpad.at[k], bgs.at[k], zs.at[k], n_chunks=n_chunks)
         for k in range(nsub)), EVEN_STAGGER)


def _even_chunk(g, xp_ref, xc_ref, xn_ref, mod_ref, ng_ref, win_ref, caw_ref, cab_ref, lng_ref, lnb_ref,
                cbw_ref, wout_ref, o_ref, apad, cpad, bgs, zs, *, n_chunks):
    c = g % n_chunks if n_chunks > 1 else 0
    rows = ROW_CHUNK + 2 * CONF_HALO
    shift1, scale1, gate1 = mod_ref[0:1, :], mod_ref[1:2, :], mod_ref[2:3, :]
    own = slice(CONF_HALO, CONF_HALO + ROW_CHUNK)
    if n_chunks == 1:
        h = _norm_mod(xc_ref[...], ng_ref[0:1, :], shift1, scale1).astype(BF16)
        h_own = h
        for pad_ref in (apad, cpad):
            pad_ref[0:CONF_HALO, :] = jnp.zeros((CONF_HALO, D_HALF), F32)
            pad_ref[CONF_HALO + ROW_CHUNK:rows, :] = jnp.zeros((CONF_HALO, D_HALF), F32)
        keep = lambda v: v
        span = own
    else:
        xh = jnp.concatenate([xp_ref[...], xc_ref[...], xn_ref[...]], axis=0)
        h = _norm_mod(xh, ng_ref[0:1, :], shift1, scale1).astype(BF16)
        h_own = h[own]
        ri = lax.broadcasted_iota(jnp.int32, (rows, D_HALF), 0)
        lo = jnp.where(c == 0, CONF_HALO, 0)
        hi = jnp.where(c == n_chunks - 1, CONF_HALO + ROW_CHUNK, rows)
        inside = (ri >= lo) & (ri < hi)
        keep = lambda v: jnp.where(inside, v, 0.0)
        span = slice(0, rows)
    a = _dot(h, win_ref[:, 0:D_HALF]) * jax.nn.sigmoid(_dot(h, win_ref[:, D_HALF:2 * D_HALF]))
    apad[span, :] = keep(a)
    yield
    cx = _dot(h, win_ref[:, 3 * D_HALF:4 * D_HALF]) * _dot(h, win_ref[:, 4 * D_HALF:5 * D_HALF])
    cpad[span, :] = keep(cx)
    yield
    bgs[...] = _dot(h_own, win_ref[:, 2 * D_HALF:3 * D_HALF])
    yield

    sub = EVEN_CONV_ROWS
    tile = 8
    for j in range(ROW_CHUNK // sub):
        r0 = j * sub
        groups = []
        for cg in range(D_HALF // LANES):
            cols = slice(cg * LANES, (cg + 1) * LANES)
            acc = None
            for r in range(tile):
                part = None
                for m in range(-(-(CONF_WIDTH + 1) // tile)):
                    o = tile * m + r
                    if 1 <= o <= CONF_WIDTH:
                        term = caw_ref[o - 1:o, cols] * apad[r0 + tile * m:r0 + tile * m + sub + tile, cols]
                        part = term if part is None else part + term
                shifted = part[r:r + sub, :]
                acc = shifted if acc is None else acc + shifted
            groups.append(acc)
        acc = jnp.concatenate(groups, axis=1) + cab_ref[...]
        mu = jnp.mean(acc, axis=-1, keepdims=True)
        dlt = acc - mu
        var = jnp.mean(dlt * dlt, axis=-1, keepdims=True)
        a_out = jax.nn.silu(dlt * lax.rsqrt(var + EPS) * lng_ref[...] + lnb_ref[...])
        zs[r0:r0 + sub, 0:D_HALF] = a_out.astype(BF16)
        base = r0 + CONF_HALO - 1
        sc = (cbw_ref[0:1, :] * cpad[base:base + sub, :]
              + cbw_ref[1:2, :] * cpad[base + 1:base + 1 + sub, :]
              + cbw_ref[2:3, :] * cpad[base + 2:base + 2 + sub, :])
        zs[r0:r0 + sub, D_HALF:D_MODEL] = (bgs[r0:r0 + sub, :] * sc).astype(BF16)
        yield

    o = _dot(zs[...], wout_ref[...])
    o_ref[...] = xc_ref[...] + gate1 * _rms(o, ng_ref[1:2, :])
    yield


def _even_layer(x, modv, layer, cond_base, cond_stride, ng, win, caw, cab, lng, lnb, cbw, wout):
    j = layer // 2
    b, t, _ = x.shape
    n_chunks = t // ROW_CHUNK
    nsub = EVEN_CHUNKS
    assert (b * n_chunks) % nsub == 0 and (cond_stride == 0 or n_chunks % nsub == 0)
    hpc = ROW_CHUNK // CONF_HALO
    n_halo_blocks = b * t // CONF_HALO
    rows = ROW_CHUNK + 2 * CONF_HALO
    halo_specs = []
    for k in range(nsub):
        halo_specs += [
            pl.BlockSpec((None, CONF_HALO, D_MODEL),
                         lambda i, k=k: (jnp.maximum((i * nsub + k) * hpc - 1, 0), 0, 0)),
            pl.BlockSpec((None, CONF_HALO, D_MODEL),
                         lambda i, k=k: (jnp.minimum((i * nsub + k + 1) * hpc, n_halo_blocks - 1), 0, 0)),
        ]
    x_halo = x.reshape(n_halo_blocks, CONF_HALO, D_MODEL)
    x_chunks = x.reshape(b * n_chunks, ROW_CHUNK, D_MODEL)
    kern = functools.partial(_even_kernel, n_chunks=n_chunks, nsub=nsub)
    out = pl.pallas_call(
        kern,
        grid=(b * n_chunks // nsub,),
        in_specs=halo_specs + [
            pl.BlockSpec((nsub, ROW_CHUNK, D_MODEL), lambda i: (i, 0, 0)),
            _mod_spec(layer, lambda i: cond_base + cond_stride * ((i * nsub) // n_chunks)),
            _layer_spec((4, D_MODEL), layer),
            _layer_spec((D_MODEL, 5 * D_HALF), j),
            _layer_spec((CONF_WIDTH, D_HALF), j),
            _layer_spec((1, D_HALF), j),
            _layer_spec((1, D_HALF), j),
            _layer_spec((1, D_HALF), j),
            _layer_spec((3, D_HALF), j),
            _layer_spec((D_MODEL, D_MODEL), j),
        ],
        out_specs=pl.BlockSpec((nsub, ROW_CHUNK, D_MODEL), lambda i: (i, 0, 0)),
        out_shape=jax.ShapeDtypeStruct(x_chunks.shape, F32),
        scratch_shapes=[
            pltpu.VMEM((nsub, rows, D_HALF), F32),
            pltpu.VMEM((nsub, rows, D_HALF), F32),
            pltpu.VMEM((nsub, ROW_CHUNK, D_HALF), F32),
            pltpu.VMEM((nsub, ROW_CHUNK, D_MODEL), BF16),
        ],
        compiler_params=_params(1),
        name="even_mixer",
    )(*([x_halo] * (2 * nsub)), x_chunks, modv, ng, win, caw, cab, lng, lnb, cbw, wout)
    return out.reshape(b, t, D_MODEL)


def _mlp_kernel(x_ref, mod_ref, ng_ref, w1_hbm, w2_hbm, o_ref, w1_s, w2_s, stage, sem, *, layer):
    n_units = 2 * (D_FF // D_MODEL)
    half = D_MODEL // 2

    def span(u):
        return slice((u // 2) * D_MODEL, (u // 2 + 1) * D_MODEL)

    def halves(u):
        out = []
        for hh in range(2):
            cols = pl.ds(hh * half, half)
            if u % 2 == 0:
                src, dst = w1_hbm.at[layer, :, span(u)].at[:, cols], w1_s.at[:, span(u)].at[:, cols]
            else:
                src, dst = w2_hbm.at[layer, span(u), :].at[:, cols], w2_s.at[span(u), :].at[:, cols]
            out.append((src, dst, (2 * u + hh) % 4))
        return out

    def copies(u):
        return [pltpu.make_async_copy(src, stage.at[slot], sem.at[slot]) for src, _, slot in halves(u)]

    def start_unit(u):
        for cp in copies(u):
            cp.start()

    def wait_unit(u):
        for cp in copies(u):
            cp.wait()

    def cast_unit(u):
        for _, dst, slot in halves(u):
            dst[...] = stage[slot].astype(BF16)

    def tile(load):
        x = x_ref[...]
        h = _norm_mod(x, ng_ref[2:3, :], mod_ref[3:4, :], mod_ref[4:5, :]).astype(BF16)
        acc = hid = None
        for u in range(n_units):
            if load and u + 1 < n_units:
                wait_unit(u + 1)
                cast_unit(u + 1)
            if u % 2 == 0:
                hid = jnp.square(jnp.maximum(_dot(h, w1_s[:, span(u)]), 0.0)).astype(BF16)
            else:
                part = _dot(hid, w2_s[span(u), :])
                acc = part if acc is None else acc + part
            if load and u + 3 < n_units:
                start_unit(u + 3)
        o_ref[...] = x + mod_ref[5:6, :] * _rms(acc, ng_ref[3:4, :])

    first = pl.program_id(0) == 0

    @pl.when(first)
    def _():
        start_unit(0)
        start_unit(1)
        wait_unit(0)
        cast_unit(0)
        start_unit(2)
        tile(load=True)

    @pl.when(jnp.logical_not(first))
    def _():
        tile(load=False)


def _mlp_layer(x, modv, layer, cond_base, cond_stride, ng, w1, w2):
    b, t, _ = x.shape
    tm = MLP_ROWS
    assert (b * t) % tm == 0 and (cond_stride == 0 or t % tm == 0)
    tiles_per_seq = max(t // tm, 1)
    x2 = x.reshape(b * t, D_MODEL)
    out = pl.pallas_call(
        functools.partial(_mlp_kernel, layer=layer),
        grid=(b * t // tm,),
        in_specs=[
            pl.BlockSpec((tm, D_MODEL), lambda i: (i, 0)),
            _mod_spec(layer, lambda i: cond_base + cond_stride * (i // tiles_per_seq)),
            _layer_spec((4, D_MODEL), layer),
            pl.BlockSpec(memory_space=pl.ANY),
            pl.BlockSpec(memory_space=pl.ANY),
        ],
        out_specs=pl.BlockSpec((tm, D_MODEL), lambda i: (i, 0)),
        out_shape=jax.ShapeDtypeStruct(x2.shape, F32),
        scratch_shapes=[
            pltpu.VMEM((D_MODEL, D_FF), BF16),
            pltpu.VMEM((D_FF, D_MODEL), BF16),
            pltpu.VMEM((4, D_MODEL, D_MODEL // 2), F32),
            pltpu.SemaphoreType.DMA((4,)),
        ],
        compiler_params=_params(1),
        name="mlp",
    )(x2, modv, ng, w1, w2)
    return out.reshape(b, t, D_MODEL)


def _log_sigmoid(x):
    return jnp.minimum(x, 0.0) - jnp.log(1.0 + jnp.exp(-jnp.abs(x)))


def _rot_half(x, first_half):
    return jnp.where(first_half, pltpu.roll(x, 96, axis=1), pltpu.roll(x, 32, axis=1))


def _values_and_ones(v):
    vt = jnp.transpose(v)
    row = lax.broadcasted_iota(jnp.int32, vt.shape, 0)
    return jnp.where(row < HEAD_DIM, vt, 1.0)


def _aligned(i, m):
    return i * m if isinstance(i, int) else pl.multiple_of(i * m, m)


def _drain(pieces):
    for _ in pieces:
        pass


def _each(body, n, static, unroll=1):
    if static:
        for i in range(n):
            yield from body(i)
    else:
        lax.fori_loop(0, n, lambda i, c: (_drain(body(i)), c)[1], 0, unroll=unroll)


def _alternate(*programs):
    live = list(programs)
    while live:
        for g in list(live):
            try:
                next(g)
            except StopIteration:
                live.remove(g)
            else:
                yield


def _odd_kernel(*refs, t, lc, latent, nseq):
    shared = set(range(1, 11 if latent else 9))
    _run_staggered((_odd_seq(*[r if k in shared else r.at[sq] for k, r in enumerate(refs)],
                             t=t, lc=lc, latent=latent) for sq in range(nseq)), ODD_STAGGER)


def _odd_seq(*refs, t, lc, latent):
    if latent:
        (x_ref, mod_ref, ng_ref, win_ref, wg_ref, gb_ref, sink_ref, hn_ref, wout_ref,
         cos_ref, sin_ref, kc_ref, vc_ref, cin_ref, nin_ref, min_ref,
         o_ref,
         qa_s, qb_s, kk_s, vvt_s, qma_s, qmb_s, qm_s, km_s, vmt_s, vmtf_s, om_s, g_s, z_s, hft_s, hbt_s,
         sr_s, st_s, qc_s, c_s, n_s, m_s, s_s, p_s, kc_s, vct_s) = refs
    else:
        (x_ref, mod_ref, ng_ref, win_ref, wg_ref, gb_ref, sink_ref, hn_ref, wout_ref,
         o_ref, ko_ref, vo_ref, cf_ref, cb_ref, no_ref, mo_ref,
         qa_s, qb_s, kk_s, vvt_s, qma_s, qmb_s, qm_s, km_s, vmt_s, vmtf_s, om_s, g_s, z_s, hft_s, hbt_s,
         sr_s, st_s, qc_s, c_s, n_s, m_s, s_s, p_s) = refs

    static = not latent
    n_blocks = t // CHUNK
    pad = CHUNK if latent else 0
    shift1, scale1, gate1 = mod_ref[0:1, :], mod_ref[1:2, :], mod_ref[2:3, :]
    lane = lax.broadcasted_iota(jnp.int32, (1, LANES), 1)
    left = lane < HEAD_DIM
    first_half = (lane % HEAD_DIM) < (HEAD_DIM // 2)
    ti = lax.broadcasted_iota(jnp.int32, (CHUNK, CHUNK), 0)
    si = lax.broadcasted_iota(jnp.int32, (CHUNK, CHUNK), 1)
    top = ti < HEAD_DIM
    same_head = top == (si < HEAD_DIM)

    if latent:
        for kv in range(N_KV):
            kk_s[kv, 0:CHUNK, :] = jnp.zeros((CHUNK, LANES), BF16)
            kk_s[kv, CHUNK + t:2 * CHUNK + t, :] = jnp.zeros((CHUNK, LANES), BF16)
            vvt_s[kv, :, 0:CHUNK] = jnp.zeros((LANES, CHUNK), BF16)
            vvt_s[kv, :, CHUNK + t:2 * CHUNK + t] = jnp.zeros((LANES, CHUNK), BF16)
            kc_s[kv] = kc_ref[kv].astype(BF16)
            vct_s[kv] = _values_and_ones(vc_ref[kv]).astype(BF16)

    def project(rc):
        r0 = _aligned(rc, ROW_CHUNK)
        rows = pl.ds(r0, ROW_CHUNK)
        krows = pl.ds(r0 + pad, ROW_CHUNK)
        h = _norm_mod(x_ref[rows, :], ng_ref[0:1, :], shift1, scale1).astype(BF16)
        if latent:
            cs, sn = cos_ref[rows, :], sin_ref[rows, :]
        q = _dot(h, win_ref[:, 0:D_HALF])
        for p in range(N_PAIRS):
            cols = slice(p * LANES, (p + 1) * LANES)
            qp = q[:, cols]
            if latent:
                qp = qp * cs + _rot_half(qp, first_half) * sn
            qp = qp * ATT_SCALE
            qa_s[rows, cols] = jnp.where(left, qp, 0.0).astype(BF16)
            qb_s[rows, cols] = jnp.where(left, 0.0, qp).astype(BF16)
        yield
        kv2 = _dot(h, win_ref[:, D_HALF:D_HALF + 2 * LANES])
        ka, va = kv2[:, 0:LANES], kv2[:, LANES:2 * LANES]
        if latent:
            ka = ka * cs + _rot_half(ka, first_half) * sn
        kr = pltpu.roll(ka, HEAD_DIM, axis=1)
        vr = pltpu.roll(va, HEAD_DIM, axis=1)
        if not latent:
            ka_t, va_t = jnp.transpose(ka), jnp.transpose(va)
            for kv in range(N_KV):
                ko_ref[0, kv, :, rows] = ka_t[kv * HEAD_DIM:(kv + 1) * HEAD_DIM, :]
                vo_ref[0, kv, :, rows] = va_t[kv * HEAD_DIM:(kv + 1) * HEAD_DIM, :]
        kk_s[0, krows, :] = jnp.where(left, ka, kr).astype(BF16)
        kk_s[1, krows, :] = jnp.where(left, kr, ka).astype(BF16)
        vvt_s[0, :, krows] = _values_and_ones(va).astype(BF16)
        vvt_s[1, :, krows] = _values_and_ones(vr).astype(BF16)
        yield
        base = D_HALF + 2 * LANES
        qm = _dot(h, win_ref[:, base:base + D_HALF])
        qm_s[rows, :] = qm.astype(BF16)
        for p in range(N_PAIRS):
            cols = slice(p * LANES, (p + 1) * LANES)
            qma_s[rows, cols] = jnp.where(left, qm[:, cols], 0.0).astype(BF16)
            qmb_s[rows, cols] = jnp.where(left, 0.0, qm[:, cols]).astype(BF16)
        yield
        vm = _dot(h, win_ref[:, base + 2 * D_HALF:base + 3 * D_HALF])
        for p in range(N_PAIRS):
            cols = slice(p * LANES, (p + 1) * LANES)
            vt = jnp.transpose(vm[:, cols])
            vmtf_s[cols, rows] = vt
            vmt_s[cols, rows] = vt.astype(BF16)
        yield
        km = _dot(h, win_ref[:, base + D_HALF:base + 2 * D_HALF]) * (HEAD_DIM ** -0.5)
        km_s[rows, :] = km.astype(BF16)
        yield
        om_s[rows, :] = _dot(h, win_ref[:, base + 3 * D_HALF:base + 4 * D_HALF])
        g_s[rows, :] = _dot(h, wg_ref[...]) + gb_ref[...]
        yield

    yield from _each(project, t // ROW_CHUNK, static)

    kj = lax.broadcasted_iota(jnp.int32, (3 * CHUNK, 2 * CHUNK), 0)
    qi = lax.broadcasted_iota(jnp.int32, (3 * CHUNK, 2 * CHUNK), 1) % CHUNK
    band_ok = jnp.abs(kj - CHUNK - qi) <= WINDOW
    head_a = lax.broadcasted_iota(jnp.int32, (1, 2 * CHUNK), 1) < CHUNK

    n_keys = _att_keys(lc, latent)
    group = _att_group(latent)

    def attention():
        def attend(i):
            r0 = _aligned(i, CHUNK)
            rows = pl.ds(r0, CHUNK)
            if latent:
                key_pos = kj + (i - 1) * CHUNK
                mask = band_ok & (key_pos >= 0) & (key_pos < t)
                win = pl.ds(r0, 3 * CHUNK)
            for g0 in range(0, N_PAIRS, group):
                for gi in range(group):
                    p = g0 + gi
                    kv = p // (N_PAIRS // N_KV)
                    cols = slice(p * LANES, (p + 1) * LANES)
                    q2 = jnp.concatenate([qa_s[rows, cols], qb_s[rows, cols]], axis=0)
                    if latent:
                        s_s[gi, 0:lc, :] = _dot_nt(kc_s[kv], q2)
                        s_s[gi, lc:n_keys, :] = jnp.where(mask, _dot_nt(kk_s[kv, win, :], q2), -jnp.inf)
                    else:
                        s_s[gi] = _dot_nt(kk_s[kv], q2)
                yield
                maxes = []
                for gi in range(group):
                    p = g0 + gi
                    sink = jnp.where(head_a, sink_ref[2 * p:2 * p + 1, 0:1], sink_ref[2 * p + 1:2 * p + 2, 0:1])
                    mx = jnp.maximum(jnp.max(s_s[gi], axis=0, keepdims=True), sink)
                    p_s[gi] = jnp.exp(s_s[gi] - mx).astype(BF16)
                    maxes.append((sink, mx))
                yield
                for gi in range(group):
                    p = g0 + gi
                    kv = p // (N_PAIRS // N_KV)
                    cols = slice(p * LANES, (p + 1) * LANES)
                    if latent:
                        num = (_dot(vct_s[kv], p_s[gi, 0:lc, :]) + _dot(vvt_s[kv, :, win], p_s[gi, lc:n_keys, :]))
                    else:
                        num = _dot(vvt_s[kv], p_s[gi])
                    sink, mx = maxes[gi]
                    den = num[HEAD_DIM:HEAD_DIM + 1, :] + jnp.exp(sink - mx)
                    out = num[0:HEAD_DIM, :] * (1.0 / den)
                    pair = jnp.concatenate([out[:, 0:CHUNK], out[:, CHUNK:2 * CHUNK]], axis=0)
                    z_s[rows, cols] = jnp.transpose(pair).astype(BF16)
                yield

        return attend

    if latent:
        c_s[...] = cin_ref[...]
        n_s[...] = nin_ref[...]
        m_s[...] = min_ref[...]
    else:
        c_s[...] = jnp.zeros(c_s.shape, F32)
        n_s[...] = jnp.zeros(n_s.shape, F32)
        m_s[...] = jnp.zeros(m_s.shape, F32)

    see = (ti <= si, ti >= si)
    tri = tuple(m.astype(F32).astype(BF16) for m in see)
    last = (CHUNK - 1, 0)

    def mlstm(i):
        offs = (_aligned(i, CHUNK), _aligned(n_blocks - 1 - i, CHUNK))
        half = (slice(0, LANES), slice(LANES, 2 * LANES))
        a_rows, b_rows = [], []
        for d in range(2):
            gt = jnp.transpose(g_s[pl.ds(offs[d], CHUNK), :])
            lf = _log_sigmoid(gt[2 * N_HEADS:4 * N_HEADS, :])
            bcum = _dot_exact_lhs(lf, tri[d])[8 * d:8 * d + 8, :]
            a_rows.append(gt[8 * d:8 * d + 8, :] - bcum)
            b_rows.append(bcum)
        yield
        q_ns = []
        for d in range(2):
            rows = pl.ds(offs[d], CHUNK)
            for p in range(N_PAIRS):
                u = d * N_PAIRS + p
                cols = slice(p * LANES, (p + 1) * LANES)
                q2 = jnp.concatenate([qma_s[rows, cols], qmb_s[rows, cols]], axis=0)
                sr_s[u] = _dot_nt(km_s[rows, cols], q2)
                qc_s[u] = _dot_nt(c_s[u].astype(BF16), qm_s[rows, cols])
                n8 = jnp.broadcast_to(n_s[u:u + 1, :], (8, LANES)).astype(BF16)
                q_ns.append(_dot_nt(n8, q2)[0:1, :])
            yield
        stats = {}
        for d in range(2):
            for hd in range(N_HEADS):
                u, j = d * N_PAIRS + hd // 2, hd % 2
                mrow = d * N_HEADS + hd
                a_row = a_rows[d][hd:hd + 1, :]
                b_row = b_rows[d][hd:hd + 1, :]
                m_prev = m_s[mrow:mrow + 1, 0:1]
                a_col = jnp.transpose(jnp.broadcast_to(a_row, (CHUNK, CHUNK)))
                z_t = jnp.where(see[d], a_col, -jnp.inf)
                m_run = jnp.maximum(jnp.max(z_t, axis=0, keepdims=True), m_prev)
                s_t = sr_s[u, :, half[j]] * jnp.exp(z_t - m_run)
                st_s[u, :, half[j]] = s_t.astype(BF16)
                w_int = jnp.exp(m_prev - m_run)
                den = jnp.sum(s_t, axis=0, keepdims=True) + w_int * q_ns[u][:, half[j]]
                inv = 1.0 / jnp.maximum(jnp.abs(den), jnp.exp(-(b_row + m_run)))
                m_last = m_run[:, last[d]:last[d] + 1]
                stats[(u, j)] = (w_int, inv, jnp.exp(a_row - m_last), jnp.exp(m_prev - m_last))
                m_s[mrow:mrow + 1, :] = jnp.broadcast_to(b_row[:, last[d]:last[d] + 1] + m_last, (1, LANES))
            yield
        for d in range(2):
            rows = pl.ds(offs[d], CHUNK)
            ht_s = (hft_s, hbt_s)[d]
            for p in range(N_PAIRS):
                u = d * N_PAIRS + p
                cols = slice(p * LANES, (p + 1) * LANES)
                (w_a, inv_a, e_a, dec_a), (w_b, inv_b, e_b, dec_b) = stats[(u, 0)], stats[(u, 1)]
                kp = km_s[rows, cols]
                num2 = _dot(vmt_s[cols, rows], st_s[u])
                num = jnp.where(top, num2[:, half[0]], num2[:, half[1]])
                ht_s[cols, rows] = (num + jnp.where(top, w_a, w_b) * qc_s[u]) * jnp.where(top, inv_a, inv_b)
                vt_e = (vmtf_s[cols, rows] * jnp.where(top, e_a, e_b)).astype(BF16)
                c_s[u] = jnp.where(top, dec_a, dec_b) * c_s[u] + jnp.where(same_head, _dot(vt_e, kp), 0.0)
                e2 = jnp.concatenate([e_a, e_b, jnp.zeros((6, CHUNK), F32)], axis=0).astype(BF16)
                n_k = _dot(e2, kp)
                n_s[u:u + 1, :] = (jnp.where(left, dec_a, dec_b) * n_s[u:u + 1, :]
                                   + jnp.where(left, n_k[0:1, :], n_k[1:2, :]))
            yield

    attend = attention()
    yield from _each(lambda i: _alternate(attend(i), mlstm(i)), n_blocks, static, unroll=2)

    if not latent:
        for d, c_ref in enumerate((cf_ref, cb_ref)):
            for p in range(N_PAIRS):
                c_pair = jnp.transpose(c_s[d * N_PAIRS + p])
                c_ref[0, 2 * p] = c_pair[0:HEAD_DIM, 0:HEAD_DIM]
                c_ref[0, 2 * p + 1] = c_pair[HEAD_DIM:LANES, HEAD_DIM:LANES]
        no_ref[...] = n_s[...]
        mo_ref[...] = m_s[...]
        yield

    top_w = lax.broadcasted_iota(jnp.int32, (LANES, ROW_CHUNK), 0) < HEAD_DIM

    def finish(rc):
        r0 = _aligned(rc, ROW_CHUNK)
        rows = pl.ds(r0, ROW_CHUNK)
        for p in range(N_PAIRS):
            cols = slice(p * LANES, (p + 1) * LANES)
            hm = hft_s[cols, rows] + hbt_s[cols, rows]
            sq = hm * hm
            ms_a = jnp.sum(sq[0:HEAD_DIM], axis=0, keepdims=True)
            ms_b = jnp.sum(sq[HEAD_DIM:LANES], axis=0, keepdims=True)
            ms = jnp.where(top_w, ms_a, ms_b) * (1.0 / HEAD_DIM)
            y = jnp.transpose(hm * lax.rsqrt(ms + EPS)) * hn_ref[:, cols] * jax.nn.sigmoid(om_s[rows, cols])
            z_s[rows, D_HALF + p * LANES:D_HALF + (p + 1) * LANES] = y.astype(BF16)
            if p % 2:
                yield
        o = _dot(z_s[rows, :], wout_ref[...])
        o_ref[rows, :] = x_ref[rows, :] + gate1 * _rms(o, ng_ref[1:2, :])
        yield

    yield from _each(finish, t // ROW_CHUNK, static)


def _att_keys(lc, latent):
    return lc + 3 * CHUNK if latent else lc


def _att_group(latent):
    return 2 if latent else N_PAIRS


def _odd_scratch(t, lc, latent, nseq):
    pad = 2 * CHUNK if latent else 0
    att = (_att_group(latent), _att_keys(lc, latent), 2 * CHUNK)
    shapes = [
        pltpu.VMEM((t, D_HALF), BF16),
        pltpu.VMEM((t, D_HALF), BF16),
        pltpu.VMEM((N_KV, t + pad, LANES), BF16),
        pltpu.VMEM((N_KV, LANES, t + pad), BF16),
        pltpu.VMEM((t, D_HALF), BF16),
        pltpu.VMEM((t, D_HALF), BF16),
        pltpu.VMEM((t, D_HALF), BF16),
        pltpu.VMEM((t, D_HALF), BF16),
        pltpu.VMEM((D_HALF, t), BF16),
        pltpu.VMEM((D_HALF, t), F32),
        pltpu.VMEM((t, D_HALF), F32),
        pltpu.VMEM((t, LANES), F32),
        pltpu.VMEM((t, D_MODEL), BF16),
        pltpu.VMEM((D_HALF, t), F32),
        pltpu.VMEM((D_HALF, t), F32),
        pltpu.VMEM((2 * N_PAIRS, CHUNK, 2 * LANES), F32),
        pltpu.VMEM((2 * N_PAIRS, CHUNK, 2 * LANES), BF16),
        pltpu.VMEM((2 * N_PAIRS, LANES, CHUNK), F32),
        pltpu.VMEM((2 * N_PAIRS, LANES, LANES), F32),
        pltpu.VMEM((2 * N_PAIRS, LANES), F32),
        pltpu.VMEM((2 * N_HEADS, LANES), F32),
        pltpu.VMEM(att, F32),
        pltpu.VMEM(att, BF16),
    ]
    if latent:
        shapes += [pltpu.VMEM((N_KV, lc, LANES), BF16), pltpu.VMEM((N_KV, LANES, lc), BF16)]
    return [pltpu.VMEM((nseq,) + tuple(sh.shape), sh.dtype) for sh in shapes]


def _odd_common_specs(t, layer, cond_base, cond_stride, nseq):
    assert cond_stride == 0 or nseq == 1
    j = layer // 2
    return [
        _per_seq((t, D_MODEL), nseq),
        _mod_spec(layer, lambda i: cond_base + cond_stride * i),
        _layer_spec((4, D_MODEL), layer),
        _layer_spec((D_MODEL, D_IN_ODD), j),
        _const_spec((D_MODEL, LANES)),
        _const_spec((1, LANES)),
        _const_spec((N_HEADS, LANES)),
        _layer_spec((1, D_HALF), j),
        _layer_spec((D_MODEL, D_MODEL), j),
    ]


def _per_seq(shape, nseq):
    return pl.BlockSpec((nseq,) + shape, lambda i: (i,) + (0,) * len(shape))


def _odd_context(x, modv, layer, ng, w_main, w_gate, gate_bias, sink_b, hnorm, wout):
    b, t, _ = x.shape
    nseq = ODD_CTX_SEQS
    assert b % nseq == 0
    kern = functools.partial(_odd_kernel, t=t, lc=t, latent=False, nseq=nseq)
    per_seq = functools.partial(_per_seq, nseq=nseq)
    return pl.pallas_call(
        kern,
        grid=(b // nseq,),
        in_specs=_odd_common_specs(t, layer, 0, 0, nseq),
        out_specs=[per_seq((t, D_MODEL)),
                   per_seq((1, N_KV, HEAD_DIM, t)), per_seq((1, N_KV, HEAD_DIM, t)),
                   per_seq((1, N_HEADS, HEAD_DIM, HEAD_DIM)), per_seq((1, N_HEADS, HEAD_DIM, HEAD_DIM)),
                   per_seq((2 * N_PAIRS, LANES)), per_seq((2 * N_HEADS, LANES))],
        out_shape=[jax.ShapeDtypeStruct((b, t, D_MODEL), F32),
                   jax.ShapeDtypeStruct((b, 1, N_KV, HEAD_DIM, t), F32),
                   jax.ShapeDtypeStruct((b, 1, N_KV, HEAD_DIM, t), F32),
                   jax.ShapeDtypeStruct((b, 1, N_HEADS, HEAD_DIM, HEAD_DIM), F32),
                   jax.ShapeDtypeStruct((b, 1, N_HEADS, HEAD_DIM, HEAD_DIM), F32),
                   jax.ShapeDtypeStruct((b, 2 * N_PAIRS, LANES), F32),
                   jax.ShapeDtypeStruct((b, 2 * N_HEADS, LANES), F32)],
        scratch_shapes=_odd_scratch(t, t, False, nseq),
        compiler_params=_params(1),
        name="odd_mixer_context",
    )(x, modv, ng, w_main, w_gate, gate_bias, sink_b, hnorm, wout)


def _odd_latent(x, modv, layer, ng, w_main, w_gate, gate_bias, sink_b, hnorm, wout, cos_t, sin_t, kc, vc,
                c_in, n_in, m_in):
    b, t, _ = x.shape
    lc = kc.shape[2]
    kern = functools.partial(_odd_kernel, t=t, lc=lc, latent=True, nseq=1)
    per_seq = functools.partial(_per_seq, nseq=1)
    return pl.pallas_call(
        kern,
        grid=(b,),
        in_specs=_odd_common_specs(t, layer, 1, 1, 1) + [
            _const_spec((t, LANES)), _const_spec((t, LANES)),
            per_seq((N_KV, lc, LANES)), per_seq((N_KV, lc, LANES)),
            per_seq((2 * N_PAIRS, LANES, LANES)), per_seq((2 * N_PAIRS, LANES)), per_seq((2 * N_HEADS, LANES)),
        ],
        out_specs=per_seq((t, D_MODEL)),
        out_shape=jax.ShapeDtypeStruct((b, t, D_MODEL), F32),
        scratch_shapes=_odd_scratch(t, lc, True, 1),
        compiler_params=_params(1),
        name="odd_mixer_latent",
    )(x, modv, ng, w_main, w_gate, gate_bias, sink_b, hnorm, wout, cos_t, sin_t, kc, vc, c_in, n_in, m_in)


def _rope_tables(t):
    rows = t // GRID_W
    row = jnp.broadcast_to(jnp.arange(rows)[:, None], (rows, GRID_W)).reshape(t).astype(F32)
    col = jnp.broadcast_to(jnp.arange(GRID_W)[None, :], (rows, GRID_W)).reshape(t).astype(F32)
    n_freq = HEAD_DIM // 4
    inv_freq = ROPE_BASE ** (-jnp.arange(n_freq, dtype=F32) / n_freq)
    ang = jnp.concatenate([row[:, None] * inv_freq, col[:, None] * inv_freq], axis=-1)
    cos, sin = jnp.cos(ang), jnp.sin(ang)
    cos_l = jnp.tile(cos, (1, LANES // cos.shape[1]))
    sin_l = jnp.tile(jnp.concatenate([-sin, sin], axis=-1), (1, LANES // HEAD_DIM))
    return cos_l, sin_l


def _pair_blockdiag(c):
    b = c.shape[0]
    c = c.reshape(b, N_PAIRS, 2, HEAD_DIM, HEAD_DIM)
    z = jnp.zeros_like(c[:, :, 0])
    top = jnp.concatenate([c[:, :, 0], z], axis=-1)
    bot = jnp.concatenate([z, c[:, :, 1]], axis=-1)
    return jnp.concatenate([top, bot], axis=-2)


def _lane_bcast(v):
    return jnp.broadcast_to(v[..., None], v.shape + (LANES,))


def kernel(x_prompt, x_sample, c, cache_k, cache_v, state_c_fwd, state_n_fwd, state_m_fwd, state_c_bwd, state_n_bwd, state_m_bwd, c_ctx, mod_w, mod_b, norm_g, mlp_w1, mlp_w2, even_in_w, conv_a_w, conv_a_b, ln_a_g, ln_a_b, conv_b_w, even_out_w, odd_in_w, attn_sink, gate_b, hnorm_g, odd_out_w):
    n_dec = x_sample.shape[0]
    n_ctx = x_prompt.shape[0]
    cond = jnp.concatenate([c_ctx[None, :], c, jnp.zeros((COND_ROWS - 1 - n_dec, D_MODEL), F32)], axis=0)
    modv = _modulation(cond, mod_w, mod_b)

    yp, ys = x_prompt, x_sample
    w1, w2 = mlp_w1, mlp_w2

    ev = (norm_g, even_in_w.astype(BF16), conv_a_w, conv_a_b[:, None, :], ln_a_g[:, None, :], ln_a_b[:, None, :],
          conv_b_w, even_out_w.astype(BF16))
    yp = _even_layer(yp, modv, 0, 0, 0, *ev)
    ys = _even_layer(ys, modv, 0, 1, 1, *ev)
    yp = _mlp_layer(yp, modv, 0, 0, 0, norm_g, w1, w2)
    ys = _mlp_layer(ys, modv, 0, 1, 1, norm_g, w1, w2)

    order = jnp.array([0, 2, 1, 3])
    d_main = D_IN_ODD - 4 * N_HEADS
    wg = odd_in_w[0][:, d_main:].reshape(D_MODEL, 4, N_HEADS)[:, order, :].reshape(D_MODEL, 4 * N_HEADS)
    w_gate = jnp.pad(wg, ((0, 0), (0, LANES - 4 * N_HEADS))).astype(BF16)
    gate_bias = jnp.pad(gate_b[0][order, :].reshape(1, 4 * N_HEADS), ((0, 0), (0, LANES - 4 * N_HEADS)))
    sink_b = _lane_bcast(attn_sink[0])
    odd = (1, norm_g, odd_in_w.astype(BF16), w_gate, gate_bias, sink_b, hnorm_g[:, None, :], odd_out_w.astype(BF16))

    op, k_t, v_t, c_f, c_b, n_new, m_new = _odd_context(yp, modv, *odd)
    new_k, new_v = jnp.swapaxes(k_t, -1, -2), jnp.swapaxes(v_t, -1, -2)

    t_dec = x_sample.shape[1]
    cos_t, sin_t = _rope_tables(t_dec)
    kc = jnp.concatenate([cache_k[:, 0], cache_k[:, 0]], axis=-1)
    vc = jnp.concatenate([cache_v[:, 0], cache_v[:, 0]], axis=-1)
    c_in = jnp.concatenate([_pair_blockdiag(jnp.swapaxes(state_c_fwd[:, 0], -1, -2)),
                            _pair_blockdiag(jnp.swapaxes(state_c_bwd[:, 0], -1, -2))], axis=1)
    n_in = jnp.concatenate([state_n_fwd[:, 0].reshape(n_dec, N_PAIRS, LANES),
                            state_n_bwd[:, 0].reshape(n_dec, N_PAIRS, LANES)], axis=1)
    m_in = _lane_bcast(jnp.concatenate([state_m_fwd[:, 0], state_m_bwd[:, 0]], axis=1))
    os_ = _odd_latent(ys, modv, *odd, cos_t, sin_t, kc, vc, c_in, n_in, m_in)

    yp = _mlp_layer(op, modv, 1, 0, 0, norm_g, w1, w2)
    ys = _mlp_layer(os_, modv, 1, 1, 1, norm_g, w1, w2)

    n_f = n_new[:, :N_PAIRS].reshape(n_ctx, N_HEADS, HEAD_DIM)[:, None]
    n_b = n_new[:, N_PAIRS:].reshape(n_ctx, N_HEADS, HEAD_DIM)[:, None]
    m_f = m_new[:, :N_HEADS, 0][:, None]
    m_b = m_new[:, N_HEADS:, 0][:, None]
    return (yp, ys, new_k, new_v, c_f, n_f, m_f, c_b, n_b, m_b)
```

```python
import functools

import jax
import jax.numpy as jnp
from jax import lax
from jax.experimental import pallas as pl
from jax.experimental.pallas import tpu as pltpu

F32 = jnp.float32
BF16 = jnp.bfloat16

D_MODEL = 1024
D_FF = 4 * D_MODEL
EPS = 1e-6
D_HALF = D_MODEL // 2
CONF_WIDTH = 31
CONF_HALO = 16
HEAD_DIM = 64
N_HEADS = 8
N_PAIRS = N_HEADS // 2
N_KV = 2
LANES = 128
CHUNK = 128
WINDOW = 128
GRID_W = 64
ROPE_BASE = 10000.0
ATT_SCALE = HEAD_DIM ** -0.5
D_IN_ODD = D_HALF + 2 * N_KV * HEAD_DIM + 4 * D_HALF + 4 * N_HEADS
ROW_CHUNK = 256
MLP_ROWS = 512
ODD_CTX_SEQS = 2
ODD_STAGGER = 6
EVEN_CONV_ROWS = 64
EVEN_CHUNKS = 2
EVEN_STAGGER = 3
COND_ROWS = 8
VMEM_LIMIT = 56 * 1024 * 1024


def _dot(a, b):
    return jnp.dot(a, b, preferred_element_type=F32)


def _dot_nt(a, b):
    return lax.dot_general(a, b, (((1,), (1,)), ((), ())), preferred_element_type=F32)


def _split3(x):
    hi = x.astype(BF16)
    r1 = x - hi.astype(F32)
    mid = r1.astype(BF16)
    lo = (r1 - mid.astype(F32)).astype(BF16)
    return hi, mid, lo


def _dot_exact_lhs(x, b01):
    hi, mid, lo = _split3(x)
    return _dot(hi, b01) + _dot(mid, b01) + _dot(lo, b01)


def _rms(x, g):
    return x * lax.rsqrt(jnp.mean(x * x, axis=-1, keepdims=True) + EPS) * g


def _norm_mod(x, g, shift, scale):
    return _rms(x, g) * (1.0 + scale) + shift


def _params(n_grid):
    return pltpu.CompilerParams(dimension_semantics=("arbitrary",) * n_grid, vmem_limit_bytes=VMEM_LIMIT)


def _const_spec(shape):
    zeros = (0,) * len(shape)
    return pl.BlockSpec(shape, lambda *_: zeros, pipeline_mode=pl.Buffered(1))


def _layer_spec(shape, layer):
    index = (layer,) + (0,) * len(shape)
    return pl.BlockSpec((None,) + shape, lambda *_: index, pipeline_mode=pl.Buffered(1))


def _mod_spec(layer, cond_of):
    return pl.BlockSpec((None, None, 6, D_MODEL), lambda *idx: (layer, cond_of(*idx), 0, 0))


def _mod_kernel(cond_ref, w_ref, b_ref, o_ref):
    s = jax.nn.silu(cond_ref[...]).astype(BF16)
    o_ref[...] = _dot(s, w_ref[...].astype(BF16)) + b_ref[...]


def _modulation(cond, mod_w, mod_b):
    depth = mod_w.shape[0]
    n_out = mod_w.shape[2]
    tn = 2 * D_MODEL
    out = pl.pallas_call(
        _mod_kernel,
        grid=(depth, n_out // tn),
        in_specs=[
            pl.BlockSpec((COND_ROWS, D_MODEL), lambda l, j: (0, 0)),
            pl.BlockSpec((None, D_MODEL, tn), lambda l, j: (l, 0, j)),
            pl.BlockSpec((None, 1, tn), lambda l, j: (l, 0, j)),
        ],
        out_specs=pl.BlockSpec((None, COND_ROWS, tn), lambda l, j: (l, 0, j)),
        out_shape=jax.ShapeDtypeStruct((depth, COND_ROWS, n_out), F32),
        compiler_params=_params(2),
        name="modulation",
    )(cond, mod_w, mod_b.reshape(depth, 1, n_out))
    return out.reshape(depth, COND_ROWS, 6, D_MODEL)


def _run_staggered(programs, stagger):
    programs = list(programs)
    live, rounds = [], 0
    while programs or live:
        if programs and rounds % stagger == 0:
            live.append(programs.pop(0))
        for g in list(live):
            try:
                next(g)
            except StopIteration:
                live.remove(g)
        rounds += 1


def _even_kernel(*refs, n_chunks, nsub):
    halos, (xc_ref, mod_ref, ng_ref, win_ref, caw_ref, cab_ref, lng_ref, lnb_ref, cbw_ref, wout_ref, o_ref,
            apad, cpad, bgs, zs) = refs[:2 * nsub], refs[2 * nsub:]
    shared = (mod_ref, ng_ref, win_ref, caw_ref, cab_ref, lng_ref, lnb_ref, cbw_ref, wout_ref)
    _run_staggered(
        (_even_chunk(pl.program_id(0) * nsub + k, halos[2 * k], xc_ref.at[k], halos[2 * k + 1], *shared,
                     o_ref.at[k], apad.at[k], cpad.at[k], bgs.at[k], zs.at[k], n_chunks=n_chunks)
         for k in range(nsub)), EVEN_STAGGER)


def _even_chunk(g, xp_ref, xc_ref, xn_ref, mod_ref, ng_ref, win_ref, caw_ref, cab_ref, lng_ref, lnb_ref,
                cbw_ref, wout_ref, o_ref, apad, cpad, bgs, zs, *, n_chunks):
    c = g % n_chunks if n_chunks > 1 else 0
    rows = ROW_CHUNK + 2 * CONF_HALO
    shift1, scale1, gate1 = mod_ref[0:1, :], mod_ref[1:2, :], mod_ref[2:3, :]
    own = slice(CONF_HALO, CONF_HALO + ROW_CHUNK)
    if n_chunks == 1:
        h = _norm_mod(xc_ref[...], ng_ref[0:1, :], shift1, scale1).astype(BF16)
        h_own = h
        for pad_ref in (apad, cpad):
            pad_ref[0:CONF_HALO, :] = jnp.zeros((CONF_HALO, D_HALF), F32)
            pad_ref[CONF_HALO + ROW_CHUNK:rows, :] = jnp.zeros((CONF_HALO, D_HALF), F32)
        keep = lambda v: v
        span = own
    else:
        xh = jnp.concatenate([xp_ref[...], xc_ref[...], xn_ref[...]], axis=0)
        h = _norm_mod(xh, ng_ref[0:1, :], shift1, scale1).astype(BF16)
        h_own = h[own]
        ri = lax.broadcasted_iota(jnp.int32, (rows, D_HALF), 0)
        lo = jnp.where(c == 0, CONF_HALO, 0)
        hi = jnp.where(c == n_chunks - 1, CONF_HALO + ROW_CHUNK, rows)
        inside = (ri >= lo) & (ri < hi)
        keep = lambda v: jnp.where(inside, v, 0.0)
        span = slice(0, rows)
    a = _dot(h, win_ref[:, 0:D_HALF]) * jax.nn.sigmoid(_dot(h, win_ref[:, D_HALF:2 * D_HALF]))
    apad[span, :] = keep(a)
    yield
    cx = _dot(h, win_ref[:, 3 * D_HALF:4 * D_HALF]) * _dot(h, win_ref[:, 4 * D_HALF:5 * D_HALF])
    cpad[span, :] = keep(cx)
    yield
    bgs[...] = _dot(h_own, win_ref[:, 2 * D_HALF:3 * D_HALF])
    yield

    sub = EVEN_CONV_ROWS
    tile = 8
    for j in range(ROW_CHUNK // sub):
        r0 = j * sub
        groups = []
        for cg in range(D_HALF // LANES):
            cols = slice(cg * LANES, (cg + 1) * LANES)
            acc = None
            for r in range(tile):
                part = None
                for m in range(-(-(CONF_WIDTH + 1) // tile)):
                    o = tile * m + r
                    if 1 <= o <= CONF_WIDTH:
                        term = caw_ref[o - 1:o, cols] * apad[r0 + tile * m:r0 + tile * m + sub + tile, cols]
                        part = term if part is None else part + term
                shifted = part[r:r + sub, :]
                acc = shifted if acc is None else acc + shifted
            groups.append(acc)
        acc = jnp.concatenate(groups, axis=1) + cab_ref[...]
        mu = jnp.mean(acc, axis=-1, keepdims=True)
        dlt = acc - mu
        var = jnp.mean(dlt * dlt, axis=-1, keepdims=True)
        a_out = jax.nn.silu(dlt * lax.rsqrt(var + EPS) * lng_ref[...] + lnb_ref[...])
        zs[r0:r0 + sub, 0:D_HALF] = a_out.astype(BF16)
        base = r0 + CONF_HALO - 1
        sc = (cbw_ref[0:1, :] * cpad[base:base + sub, :]
              + cbw_ref[1:2, :] * cpad[base + 1:base + 1 + sub, :]
              + cbw_ref[2:3, :] * cpad[base + 2:base + 2 + sub, :])
        zs[r0:r0 + sub, D_HALF:D_MODEL] = (bgs[r0:r0 + sub, :] * sc).astype(BF16)
        yield

    o = _dot(zs[...], wout_ref[...])
    o_ref[...] = xc_ref[...] + gate1 * _rms(o, ng_ref[1:2, :])
    yield


def _even_layer(x, modv, layer, cond_base, cond_stride, ng, win, caw, cab, lng, lnb, cbw, wout):
    j = layer // 2
    b, t, _ = x.shape
    n_chunks = t // ROW_CHUNK
    nsub = EVEN_CHUNKS
    assert (b * n_chunks) % nsub == 0 and (cond_stride == 0 or n_chunks % nsub == 0)
    hpc = ROW_CHUNK // CONF_HALO
    n_halo_blocks = b * t // CONF_HALO
    rows = ROW_CHUNK + 2 * CONF_HALO
    halo_specs = []
    for k in range(nsub):
        halo_specs += [
            pl.BlockSpec((None, CONF_HALO, D_MODEL),
                         lambda i, k=k: (jnp.maximum((i * nsub + k) * hpc - 1, 0), 0, 0)),
            pl.BlockSpec((None, CONF_HALO, D_MODEL),
                         lambda i, k=k: (jnp.minimum((i * nsub + k + 1) * hpc, n_halo_blocks - 1), 0, 0)),
        ]
    x_halo = x.reshape(n_halo_blocks, CONF_HALO, D_MODEL)
    x_chunks = x.reshape(b * n_chunks, ROW_CHUNK, D_MODEL)
    kern = functools.partial(_even_kernel, n_chunks=n_chunks, nsub=nsub)
    out = pl.pallas_call(
        kern,
        grid=(b * n_chunks // nsub,),
        in_specs=halo_specs + [
            pl.BlockSpec((nsub, ROW_CHUNK, D_MODEL), lambda i: (i, 0, 0)),
            _mod_spec(layer, lambda i: cond_base + cond_stride * ((i * nsub) // n_chunks)),
            _layer_spec((4, D_MODEL), layer),
            _layer_spec((D_MODEL, 5 * D_HALF), j),
            _layer_spec((CONF_WIDTH, D_HALF), j),
            _layer_spec((1, D_HALF), j),
            _layer_spec((1, D_HALF), j),
            _layer_spec((1, D_HALF), j),
            _layer_spec((3, D_HALF), j),
            _layer_spec((D_MODEL, D_MODEL), j),
        ],
        out_specs=pl.BlockSpec((nsub, ROW_CHUNK, D_MODEL), lambda i: (i, 0, 0)),
        out_shape=jax.ShapeDtypeStruct(x_chunks.shape, F32),
        scratch_shapes=[
            pltpu.VMEM((nsub, rows, D_HALF), F32),
            pltpu.VMEM((nsub, rows, D_HALF), F32),
            pltpu.VMEM((nsub, ROW_CHUNK, D_HALF), F32),
            pltpu.VMEM((nsub, ROW_CHUNK, D_MODEL), BF16),
        ],
        compiler_params=_params(1),
        name="even_mixer",
    )(*([x_halo] * (2 * nsub)), x_chunks, modv, ng, win, caw, cab, lng, lnb, cbw, wout)
    return out.reshape(b, t, D_MODEL)


def _mlp_kernel(xp_ref, xs_ref, mod_ref, ng_ref, w1_hbm, w2_hbm, op_ref, os_ref, w1_s, w2_s, stage, sem, *,
                layer, n_ctx_tiles):
    n_blocks = D_FF // D_MODEL

    def block_copy(k):
        c, slot = k // 2, k % 2
        span = pl.ds(c * D_MODEL, D_MODEL)
        src = w1_hbm.at[layer, :, span] if k % 2 == 0 else w2_hbm.at[layer, span, :]
        return pltpu.make_async_copy(src, stage.at[slot], sem.at[slot])

    def fetch(k, dst):
        block_copy(k).wait()
        dst[...] = stage[k % 2].astype(BF16)
        if k + 2 < 2 * n_blocks:
            block_copy(k + 2).start()

    def tile(x_ref, o_ref, load=False):
        x = x_ref[...]
        h = _norm_mod(x, ng_ref[2:3, :], mod_ref[3:4, :], mod_ref[4:5, :]).astype(BF16)
        acc = None
        for c in range(n_blocks):
            cols = slice(c * D_MODEL, (c + 1) * D_MODEL)
            if load:
                fetch(2 * c, w1_s.at[:, cols])
            hid = jnp.square(jnp.maximum(_dot(h, w1_s[:, cols]), 0.0)).astype(BF16)
            if load:
                fetch(2 * c + 1, w2_s.at[cols, :])
            part = _dot(hid, w2_s[cols, :])
            acc = part if acc is None else acc + part
        o_ref[...] = x + mod_ref[5:6, :] * _rms(acc, ng_ref[3:4, :])

    i = pl.program_id(0)

    @pl.when(i == 0)
    def _():
        block_copy(0).start()
        block_copy(1).start()
        tile(xp_ref, op_ref, load=True)

    @pl.when((i > 0) & (i < n_ctx_tiles))
    def _():
        tile(xp_ref, op_ref)

    @pl.when(i >= n_ctx_tiles)
    def _():
        tile(xs_ref, os_ref)


def _mlp_layer(xp, xs, modv, layer, ng, w1, w2):
    tm = MLP_ROWS
    xp2, xs2 = xp.reshape(-1, D_MODEL), xs.reshape(-1, D_MODEL)
    t_dec = xs.shape[1]
    assert xp2.shape[0] % tm == 0 and t_dec % tm == 0
    n_p, n_s, per_seq = xp2.shape[0] // tm, xs2.shape[0] // tm, t_dec // tm
    ctx_tile = lambda i: (jnp.minimum(i, n_p - 1), 0)
    lat_tile = lambda i: (jnp.maximum(i - n_p, 0), 0)
    yp, ys = pl.pallas_call(
        functools.partial(_mlp_kernel, layer=layer, n_ctx_tiles=n_p),
        grid=(n_p + n_s,),
        in_specs=[
            pl.BlockSpec((tm, D_MODEL), ctx_tile),
            pl.BlockSpec((tm, D_MODEL), lat_tile),
            _mod_spec(layer, lambda i: jnp.where(i < n_p, 0, 1 + jnp.maximum(i - n_p, 0) // per_seq)),
            _layer_spec((4, D_MODEL), layer),
            pl.BlockSpec(memory_space=pl.ANY),
            pl.BlockSpec(memory_space=pl.ANY),
        ],
        out_specs=[pl.BlockSpec((tm, D_MODEL), ctx_tile), pl.BlockSpec((tm, D_MODEL), lat_tile)],
        out_shape=[jax.ShapeDtypeStruct(xp2.shape, F32), jax.ShapeDtypeStruct(xs2.shape, F32)],
        scratch_shapes=[
            pltpu.VMEM((D_MODEL, D_FF), BF16),
            pltpu.VMEM((D_FF, D_MODEL), BF16),
            pltpu.VMEM((2, D_MODEL, D_MODEL), F32),
            pltpu.SemaphoreType.DMA((2,)),
        ],
        compiler_params=_params(1),
        name="mlp",
    )(xp2, xs2, modv, ng, w1, w2)
    return yp.reshape(xp.shape), ys.reshape(xs.shape)


def _log_sigmoid(x):
    return jnp.minimum(x, 0.0) - jnp.log(1.0 + jnp.exp(-jnp.abs(x)))


def _rot_half(x, first_half):
    return jnp.where(first_half, pltpu.roll(x, 96, axis=1), pltpu.roll(x, 32, axis=1))


def _values_and_ones(v):
    vt = jnp.transpose(v)
    row = lax.broadcasted_iota(jnp.int32, vt.shape, 0)
    return jnp.where(row < HEAD_DIM, vt, 1.0)


def _aligned(i, m):
    return i * m if isinstance(i, int) else pl.multiple_of(i * m, m)


def _drain(pieces):
    for _ in pieces:
        pass


def _each(body, n, static, unroll=1):
    if static:
        for i in range(n):
            yield from body(i)
    else:
        lax.fori_loop(0, n, lambda i, c: (_drain(body(i)), c)[1], 0, unroll=unroll)


def _alternate(*programs):
    live = list(programs)
    while live:
        for g in list(live):
            try:
                next(g)
            except StopIteration:
                live.remove(g)
            else:
                yield


def _odd_kernel(*refs, t, lc, latent, nseq):
    shared = set(range(1, 11 if latent else 9))
    _run_staggered((_odd_seq(*[r if k in shared else r.at[sq] for k, r in enumerate(refs)],
                             t=t, lc=lc, latent=latent) for sq in range(nseq)), ODD_STAGGER)


def _odd_seq(*refs, t, lc, latent):
    if latent:
        (x_ref, mod_ref, ng_ref, win_ref, wg_ref, gb_ref, sink_ref, hn_ref, wout_ref,
         cos_ref, sin_ref, kc_ref, vc_ref, cin_ref, nin_ref, min_ref,
         o_ref,
         qa_s, qb_s, kk_s, vvt_s, qma_s, qmb_s, qm_s, km_s, vmt_s, vmtf_s, om_s, g_s, z_s, hft_s, hbt_s,
         sr_s, st_s, qc_s, c_s, n_s, m_s, s_s, p_s, kc_s, vct_s) = refs
    else:
        (x_ref, mod_ref, ng_ref, win_ref, wg_ref, gb_ref, sink_ref, hn_ref, wout_ref,
         o_ref, ko_ref, vo_ref, cf_ref, cb_ref, no_ref, mo_ref,
         qa_s, qb_s, kk_s, vvt_s, qma_s, qmb_s, qm_s, km_s, vmt_s, vmtf_s, om_s, g_s, z_s, hft_s, hbt_s,
         sr_s, st_s, qc_s, c_s, n_s, m_s, s_s, p_s) = refs

    static = not latent
    n_blocks = t // CHUNK
    pad = CHUNK if latent else 0
    shift1, scale1, gate1 = mod_ref[0:1, :], mod_ref[1:2, :], mod_ref[2:3, :]
    lane = lax.broadcasted_iota(jnp.int32, (1, LANES), 1)
    left = lane < HEAD_DIM
    first_half = (lane % HEAD_DIM) < (HEAD_DIM // 2)
    ti = lax.broadcasted_iota(jnp.int32, (CHUNK, CHUNK), 0)
    si = lax.broadcasted_iota(jnp.int32, (CHUNK, CHUNK), 1)
    top = ti < HEAD_DIM
    same_head = top == (si < HEAD_DIM)

    if latent:
        for kv in range(N_KV):
            kk_s[kv, 0:CHUNK, :] = jnp.zeros((CHUNK, LANES), BF16)
            kk_s[kv, CHUNK + t:2 * CHUNK + t, :] = jnp.zeros((CHUNK, LANES), BF16)
            vvt_s[kv, :, 0:CHUNK] = jnp.zeros((LANES, CHUNK), BF16)
            vvt_s[kv, :, CHUNK + t:2 * CHUNK + t] = jnp.zeros((LANES, CHUNK), BF16)
            kc_s[kv] = kc_ref[kv].astype(BF16)
            vct_s[kv] = _values_and_ones(vc_ref[kv]).astype(BF16)

    def project(rc):
        r0 = _aligned(rc, ROW_CHUNK)
        rows = pl.ds(r0, ROW_CHUNK)
        krows = pl.ds(r0 + pad, ROW_CHUNK)
        h = _norm_mod(x_ref[rows, :], ng_ref[0:1, :], shift1, scale1).astype(BF16)
        if latent:
            cs, sn = cos_ref[rows, :], sin_ref[rows, :]
        q = _dot(h, win_ref[:, 0:D_HALF])
        for p in range(N_PAIRS):
            cols = slice(p * LANES, (p + 1) * LANES)
            qp = q[:, cols]
            if latent:
                qp = qp * cs + _rot_half(qp, first_half) * sn
            qp = qp * ATT_SCALE
            qa_s[rows, cols] = jnp.where(left, qp, 0.0).astype(BF16)
            qb_s[rows, cols] = jnp.where(left, 0.0, qp).astype(BF16)
        yield
        kv2 = _dot(h, win_ref[:, D_HALF:D_HALF + 2 * LANES])
        ka, va = kv2[:, 0:LANES], kv2[:, LANES:2 * LANES]
        if latent:
            ka = ka * cs + _rot_half(ka, first_half) * sn
        kr = pltpu.roll(ka, HEAD_DIM, axis=1)
        vr = pltpu.roll(va, HEAD_DIM, axis=1)
        if not latent:
            ka_t, va_t = jnp.transpose(ka), jnp.transpose(va)
            for kv in range(N_KV):
                ko_ref[0, kv, :, rows] = ka_t[kv * HEAD_DIM:(kv + 1) * HEAD_DIM, :]
                vo_ref[0, kv, :, rows] = va_t[kv * HEAD_DIM:(kv + 1) * HEAD_DIM, :]
        kk_s[0, krows, :] = jnp.where(left, ka, kr).astype(BF16)
        kk_s[1, krows, :] = jnp.where(left, kr, ka).astype(BF16)
        vvt_s[0, :, krows] = _values_and_ones(va).astype(BF16)
        vvt_s[1, :, krows] = _values_and_ones(vr).astype(BF16)
        yield
        base = D_HALF + 2 * LANES
        qm = _dot(h, win_ref[:, base:base + D_HALF])
        qm_s[rows, :] = qm.astype(BF16)
        for p in range(N_PAIRS):
            cols = slice(p * LANES, (p + 1) * LANES)
            qma_s[rows, cols] = jnp.where(left, qm[:, cols], 0.0).astype(BF16)
            qmb_s[rows, cols] = jnp.where(left, 0.0, qm[:, cols]).astype(BF16)
        yield
        vm = _dot(h, win_ref[:, base + 2 * D_HALF:base + 3 * D_HALF])
        for p in range(N_PAIRS):
            cols = slice(p * LANES, (p + 1) * LANES)
            vt = jnp.transpose(vm[:, cols])
            vmtf_s[cols, rows] = vt
            vmt_s[cols, rows] = vt.astype(BF16)
        yield
        km = _dot(h, win_ref[:, base + D_HALF:base + 2 * D_HALF]) * (HEAD_DIM ** -0.5)
        km_s[rows, :] = km.astype(BF16)
        yield
        om_s[rows, :] = _dot(h, win_ref[:, base + 3 * D_HALF:base + 4 * D_HALF])
        g_s[rows, :] = _dot(h, wg_ref[...]) + gb_ref[...]
        yield

    yield from _each(project, t // ROW_CHUNK, static)

    kj = lax.broadcasted_iota(jnp.int32, (3 * CHUNK, 2 * CHUNK), 0)
    qi = lax.broadcasted_iota(jnp.int32, (3 * CHUNK, 2 * CHUNK), 1) % CHUNK
    band_ok = jnp.abs(kj - CHUNK - qi) <= WINDOW
    head_a = lax.broadcasted_iota(jnp.int32, (1, 2 * CHUNK), 1) < CHUNK

    n_keys = _att_keys(lc, latent)
    group = _att_group(latent)

    def attention():
        def attend(i):
            r0 = _aligned(i, CHUNK)
            rows = pl.ds(r0, CHUNK)
            if latent:
                key_pos = kj + (i - 1) * CHUNK
                mask = band_ok & (key_pos >= 0) & (key_pos < t)
                win = pl.ds(r0, 3 * CHUNK)
            for g0 in range(0, N_PAIRS, group):
                for gi in range(group):
                    p = g0 + gi
                    kv = p // (N_PAIRS // N_KV)
                    cols = slice(p * LANES, (p + 1) * LANES)
                    q2 = jnp.concatenate([qa_s[rows, cols], qb_s[rows, cols]], axis=0)
                    if latent:
                        s_s[gi, 0:lc, :] = _dot_nt(kc_s[kv], q2)
                        s_s[gi, lc:n_keys, :] = jnp.where(mask, _dot_nt(kk_s[kv, win, :], q2), -jnp.inf)
                    else:
                        s_s[gi] = _dot_nt(kk_s[kv], q2)
                yield
                maxes = []
                for gi in range(group):
                    p = g0 + gi
                    sink = jnp.where(head_a, sink_ref[2 * p:2 * p + 1, 0:1], sink_ref[2 * p + 1:2 * p + 2, 0:1])
                    mx = jnp.maximum(jnp.max(s_s[gi], axis=0, keepdims=True), sink)
                    p_s[gi] = jnp.exp(s_s[gi] - mx).astype(BF16)
                    maxes.append((sink, mx))
                yield
                for gi in range(group):
                    p = g0 + gi
                    kv = p // (N_PAIRS // N_KV)
                    cols = slice(p * LANES, (p + 1) * LANES)
                    if latent:
                        num = (_dot(vct_s[kv], p_s[gi, 0:lc, :]) + _dot(vvt_s[kv, :, win], p_s[gi, lc:n_keys, :]))
                    else:
                        num = _dot(vvt_s[kv], p_s[gi])
                    sink, mx = maxes[gi]
                    den = num[HEAD_DIM:HEAD_DIM + 1, :] + jnp.exp(sink - mx)
                    out = num[0:HEAD_DIM, :] * (1.0 / den)
                    pair = jnp.concatenate([out[:, 0:CHUNK], out[:, CHUNK:2 * CHUNK]], axis=0)
                    z_s[rows, cols] = jnp.transpose(pair).astype(BF16)
                yield

        return attend

    if latent:
        c_s[...] = cin_ref[...]
        n_s[...] = nin_ref[...]
        m_s[...] = min_ref[...]
    else:
        c_s[...] = jnp.zeros(c_s.shape, F32)
        n_s[...] = jnp.zeros(n_s.shape, F32)
        m_s[...] = jnp.zeros(m_s.shape, F32)

    see = (ti <= si, ti >= si)
    tri = tuple(m.astype(F32).astype(BF16) for m in see)
    last = (CHUNK - 1, 0)

    def mlstm(i):
        offs = (_aligned(i, CHUNK), _aligned(n_blocks - 1 - i, CHUNK))
        half = (slice(0, LANES), slice(LANES, 2 * LANES))
        a_rows, b_rows = [], []
        for d in range(2):
            gt = jnp.transpose(g_s[pl.ds(offs[d], CHUNK), :])
            lf = _log_sigmoid(gt[2 * N_HEADS:4 * N_HEADS, :])
            bcum = _dot_exact_lhs(lf, tri[d])[8 * d:8 * d + 8, :]
            a_rows.append(gt[8 * d:8 * d + 8, :] - bcum)
            b_rows.append(bcum)
        yield
        q_ns = []
        for d in range(2):
            rows = pl.ds(offs[d], CHUNK)
            for p in range(N_PAIRS):
                u = d * N_PAIRS + p
                cols = slice(p * LANES, (p + 1) * LANES)
                q2 = jnp.concatenate([qma_s[rows, cols], qmb_s[rows, cols]], axis=0)
                sr_s[u] = _dot_nt(km_s[rows, cols], q2)
                qc_s[u] = _dot_nt(c_s[u].astype(BF16), qm_s[rows, cols])
                n8 = jnp.broadcast_to(n_s[u:u + 1, :], (8, LANES)).astype(BF16)
                q_ns.append(_dot_nt(n8, q2)[0:1, :])
            yield
        stats = {}
        for d in range(2):
            for hd in range(N_HEADS):
                u, j = d * N_PAIRS + hd // 2, hd % 2
                mrow = d * N_HEADS + hd
                a_row = a_rows[d][hd:hd + 1, :]
                b_row = b_rows[d][hd:hd + 1, :]
                m_prev = m_s[mrow:mrow + 1, 0:1]
                a_col = jnp.transpose(jnp.broadcast_to(a_row, (CHUNK, CHUNK)))
                z_t = jnp.where(see[d], a_col, -jnp.inf)
                m_run = jnp.maximum(jnp.max(z_t, axis=0, keepdims=True), m_prev)
                s_t = sr_s[u, :, half[j]] * jnp.exp(z_t - m_run)
                st_s[u, :, half[j]] = s_t.astype(BF16)
                w_int = jnp.exp(m_prev - m_run)
                den = jnp.sum(s_t, axis=0, keepdims=True) + w_int * q_ns[u][:, half[j]]
                inv = 1.0 / jnp.maximum(jnp.abs(den), jnp.exp(-(b_row + m_run)))
                m_last = m_run[:, last[d]:last[d] + 1]
                stats[(u, j)] = (w_int, inv, jnp.exp(a_row - m_last), jnp.exp(m_prev - m_last))
                m_s[mrow:mrow + 1, :] = jnp.broadcast_to(b_row[:, last[d]:last[d] + 1] + m_last, (1, LANES))
            yield
        for d in range(2):
            rows = pl.ds(offs[d], CHUNK)
            ht_s = (hft_s, hbt_s)[d]
            for p in range(N_PAIRS):
                u = d * N_PAIRS + p
                cols = slice(p * LANES, (p + 1) * LANES)
                (w_a, inv_a, e_a, dec_a), (w_b, inv_b, e_b, dec_b) = stats[(u, 0)], stats[(u, 1)]
                kp = km_s[rows, cols]
                num2 = _dot(vmt_s[cols, rows], st_s[u])
                num = jnp.where(top, num2[:, half[0]], num2[:, half[1]])
                ht_s[cols, rows] = (num + jnp.where(top, w_a, w_b) * qc_s[u]) * jnp.where(top, inv_a, inv_b)
                vt_e = (vmtf_s[cols, rows] * jnp.where(top, e_a, e_b)).astype(BF16)
                c_s[u] = jnp.where(top, dec_a, dec_b) * c_s[u] + jnp.where(same_head, _dot(vt_e, kp), 0.0)
                e2 = jnp.concatenate([e_a, e_b, jnp.zeros((6, CHUNK), F32)], axis=0).astype(BF16)
                n_k = _dot(e2, kp)
                n_s[u:u + 1, :] = (jnp.where(left, dec_a, dec_b) * n_s[u:u + 1, :]
                                   + jnp.where(left, n_k[0:1, :], n_k[1:2, :]))
            yield

    attend = attention()
    yield from _each(lambda i: _alternate(attend(i), mlstm(i)), n_blocks, static, unroll=2)

    if not latent:
        for d, c_ref in enumerate((cf_ref, cb_ref)):
            for p in range(N_PAIRS):
                c_pair = jnp.transpose(c_s[d * N_PAIRS + p])
                c_ref[0, 2 * p] = c_pair[0:HEAD_DIM, 0:HEAD_DIM]
                c_ref[0, 2 * p + 1] = c_pair[HEAD_DIM:LANES, HEAD_DIM:LANES]
        no_ref[...] = n_s[...]
        mo_ref[...] = m_s[...]
        yield

    top_w = lax.broadcasted_iota(jnp.int32, (LANES, ROW_CHUNK), 0) < HEAD_DIM

    def finish(rc):
        r0 = _aligned(rc, ROW_CHUNK)
        rows = pl.ds(r0, ROW_CHUNK)
        for p in range(N_PAIRS):
            cols = slice(p * LANES, (p + 1) * LANES)
            hm = hft_s[cols, rows] + hbt_s[cols, rows]
            sq = hm * hm
            ms_a = jnp.sum(sq[0:HEAD_DIM], axis=0, keepdims=True)
            ms_b = jnp.sum(sq[HEAD_DIM:LANES], axis=0, keepdims=True)
            ms = jnp.where(top_w, ms_a, ms_b) * (1.0 / HEAD_DIM)
            y = jnp.transpose(hm * lax.rsqrt(ms + EPS)) * hn_ref[:, cols] * jax.nn.sigmoid(om_s[rows, cols])
            z_s[rows, D_HALF + p * LANES:D_HALF + (p + 1) * LANES] = y.astype(BF16)
            if p % 2:
                yield
        o = _dot(z_s[rows, :], wout_ref[...])
        o_ref[rows, :] = x_ref[rows, :] + gate1 * _rms(o, ng_ref[1:2, :])
        yield

    yield from _each(finish, t // ROW_CHUNK, static)


def _att_keys(lc, latent):
    return lc + 3 * CHUNK if latent else lc


def _att_group(latent):
    return 2 if latent else N_PAIRS


def _odd_scratch(t, lc, latent, nseq):
    pad = 2 * CHUNK if latent else 0
    att = (_att_group(latent), _att_keys(lc, latent), 2 * CHUNK)
    shapes = [
        pltpu.VMEM((t, D_HALF), BF16),
        pltpu.VMEM((t, D_HALF), BF16),
        pltpu.VMEM((N_KV, t + pad, LANES), BF16),
        pltpu.VMEM((N_KV, LANES, t + pad), BF16),
        pltpu.VMEM((t, D_HALF), BF16),
        pltpu.VMEM((t, D_HALF), BF16),
        pltpu.VMEM((t, D_HALF), BF16),
        pltpu.VMEM((t, D_HALF), BF16),
        pltpu.VMEM((D_HALF, t), BF16),
        pltpu.VMEM((D_HALF, t), F32),
        pltpu.VMEM((t, D_HALF), F32),
        pltpu.VMEM((t, LANES), F32),
        pltpu.VMEM((t, D_MODEL), BF16),
        pltpu.VMEM((D_HALF, t), F32),
        pltpu.VMEM((D_HALF, t), F32),
        pltpu.VMEM((2 * N_PAIRS, CHUNK, 2 * LANES), F32),
        pltpu.VMEM((2 * N_PAIRS, CHUNK, 2 * LANES), BF16),
        pltpu.VMEM((2 * N_PAIRS, LANES, CHUNK), F32),
        pltpu.VMEM((2 * N_PAIRS, LANES, LANES), F32),
        pltpu.VMEM((2 * N_PAIRS, LANES), F32),
        pltpu.VMEM((2 * N_HEADS, LANES), F32),
        pltpu.VMEM(att, F32),
        pltpu.VMEM(att, BF16),
    ]
    if latent:
        shapes += [pltpu.VMEM((N_KV, lc, LANES), BF16), pltpu.VMEM((N_KV, LANES, lc), BF16)]
    return [pltpu.VMEM((nseq,) + tuple(sh.shape), sh.dtype) for sh in shapes]


def _odd_common_specs(t, layer, cond_base, cond_stride, nseq):
    assert cond_stride == 0 or nseq == 1
    j = layer // 2
    return [
        _per_seq((t, D_MODEL), nseq),
        _mod_spec(layer, lambda i: cond_base + cond_stride * i),
        _layer_spec((4, D_MODEL), layer),
        _layer_spec((D_MODEL, D_IN_ODD), j),
        _const_spec((D_MODEL, LANES)),
        _const_spec((1, LANES)),
        _const_spec((N_HEADS, LANES)),
        _layer_spec((1, D_HALF), j),
        _layer_spec((D_MODEL, D_MODEL), j),
    ]


def _per_seq(shape, nseq):
    return pl.BlockSpec((nseq,) + shape, lambda i: (i,) + (0,) * len(shape))


def _odd_context(x, modv, layer, ng, w_main, w_gate, gate_bias, sink_b, hnorm, wout):
    b, t, _ = x.shape
    nseq = ODD_CTX_SEQS
    assert b % nseq == 0
    kern = functools.partial(_odd_kernel, t=t, lc=t, latent=False, nseq=nseq)
    per_seq = functools.partial(_per_seq, nseq=nseq)
    return pl.pallas_call(
        kern,
        grid=(b // nseq,),
        in_specs=_odd_common_specs(t, layer, 0, 0, nseq),
        out_specs=[per_seq((t, D_MODEL)),
                   per_seq((1, N_KV, HEAD_DIM, t)), per_seq((1, N_KV, HEAD_DIM, t)),
                   per_seq((1, N_HEADS, HEAD_DIM, HEAD_DIM)), per_seq((1, N_HEADS, HEAD_DIM, HEAD_DIM)),
                   per_seq((2 * N_PAIRS, LANES)), per_seq((2 * N_HEADS, LANES))],
        out_shape=[jax.ShapeDtypeStruct((b, t, D_MODEL), F32),
                   jax.ShapeDtypeStruct((b, 1, N_KV, HEAD_DIM, t), F32),
                   jax.ShapeDtypeStruct((b, 1, N_KV, HEAD_DIM, t), F32),
                   jax.ShapeDtypeStruct((b, 1, N_HEADS, HEAD_DIM, HEAD_DIM), F32),
                   jax.ShapeDtypeStruct((b, 1, N_HEADS, HEAD_DIM, HEAD_DIM), F32),
                   jax.ShapeDtypeStruct((b, 2 * N_PAIRS, LANES), F32),
                   jax.ShapeDtypeStruct((b, 2 * N_HEADS, LANES), F32)],
        scratch_shapes=_odd_scratch(t, t, False, nseq),
        compiler_params=_params(1),
        name="odd_mixer_context",
    )(x, modv, ng, w_main, w_gate, gate_bias, sink_b, hnorm, wout)


def _odd_latent(x, modv, layer, ng, w_main, w_gate, gate_bias, sink_b, hnorm, wout, cos_t, sin_t, kc, vc,
                c_in, n_in, m_in):
    b, t, _ = x.shape
    lc = kc.shape[2]
    kern = functools.partial(_odd_kernel, t=t, lc=lc, latent=True, nseq=1)
    per_seq = functools.partial(_per_seq, nseq=1)
    return pl.pallas_call(
        kern,
        grid=(b,),
        in_specs=_odd_common_specs(t, layer, 1, 1, 1) + [
            _const_spec((t, LANES)), _const_spec((t, LANES)),
            per_seq((N_KV, lc, LANES)), per_seq((N_KV, lc, LANES)),
            per_seq((2 * N_PAIRS, LANES, LANES)), per_seq((2 * N_PAIRS, LANES)), per_seq((2 * N_HEADS, LANES)),
        ],
        out_specs=per_seq((t, D_MODEL)),
        out_shape=jax.ShapeDtypeStruct((b, t, D_MODEL), F32),
        scratch_shapes=_odd_scratch(t, lc, True, 1),
        compiler_params=_params(1),
        name="odd_mixer_latent",
    )(x, modv, ng, w_main, w_gate, gate_bias, sink_b, hnorm, wout, cos_t, sin_t, kc, vc, c_in, n_in, m_in)


def _rope_tables(t):
    rows = t // GRID_W
    row = jnp.broadcast_to(jnp.arange(rows)[:, None], (rows, GRID_W)).reshape(t).astype(F32)
    col = jnp.broadcast_to(jnp.arange(GRID_W)[None, :], (rows, GRID_W)).reshape(t).astype(F32)
    n_freq = HEAD_DIM // 4
    inv_freq = ROPE_BASE ** (-jnp.arange(n_freq, dtype=F32) / n_freq)
    ang = jnp.concatenate([row[:, None] * inv_freq, col[:, None] * inv_freq], axis=-1)
    cos, sin = jnp.cos(ang), jnp.sin(ang)
    cos_l = jnp.tile(cos, (1, LANES // cos.shape[1]))
    sin_l = jnp.tile(jnp.concatenate([-sin, sin], axis=-1), (1, LANES // HEAD_DIM))
    return cos_l, sin_l


def _pair_blockdiag(c):
    b = c.shape[0]
    c = c.reshape(b, N_PAIRS, 2, HEAD_DIM, HEAD_DIM)
    z = jnp.zeros_like(c[:, :, 0])
    top = jnp.concatenate([c[:, :, 0], z], axis=-1)
    bot = jnp.concatenate([z, c[:, :, 1]], axis=-1)
    return jnp.concatenate([top, bot], axis=-2)


def _lane_bcast(v):
    return jnp.broadcast_to(v[..., None], v.shape + (LANES,))


def kernel(x_prompt, x_sample, c, cache_k, cache_v, state_c_fwd, state_n_fwd, state_m_fwd, state_c_bwd, state_n_bwd, state_m_bwd, c_ctx, mod_w, mod_b, norm_g, mlp_w1, mlp_w2, even_in_w, conv_a_w, conv_a_b, ln_a_g, ln_a_b, conv_b_w, even_out_w, odd_in_w, attn_sink, gate_b, hnorm_g, odd_out_w):
    n_dec = x_sample.shape[0]
    n_ctx = x_prompt.shape[0]
    cond = jnp.concatenate([c_ctx[None, :], c, jnp.zeros((COND_ROWS - 1 - n_dec, D_MODEL), F32)], axis=0)
    modv = _modulation(cond, mod_w, mod_b)

    yp, ys = x_prompt, x_sample
    w1, w2 = mlp_w1, mlp_w2

    ev = (norm_g, even_in_w.astype(BF16), conv_a_w, conv_a_b[:, None, :], ln_a_g[:, None, :], ln_a_b[:, None, :],
          conv_b_w, even_out_w.astype(BF16))
    yp = _even_layer(yp, modv, 0, 0, 0, *ev)
    ys = _even_layer(ys, modv, 0, 1, 1, *ev)
    yp, ys = _mlp_layer(yp, ys, modv, 0, norm_g, w1, w2)

    order = jnp.array([0, 2, 1, 3])
    d_main = D_IN_ODD - 4 * N_HEADS
    wg = odd_in_w[0][:, d_main:].reshape(D_MODEL, 4, N_HEADS)[:, order, :].reshape(D_MODEL, 4 * N_HEADS)
    w_gate = jnp.pad(wg, ((0, 0), (0, LANES - 4 * N_HEADS))).astype(BF16)
    gate_bias = jnp.pad(gate_b[0][order, :].reshape(1, 4 * N_HEADS), ((0, 0), (0, LANES - 4 * N_HEADS)))
    sink_b = _lane_bcast(attn_sink[0])
    odd = (1, norm_g, odd_in_w.astype(BF16), w_gate, gate_bias, sink_b, hnorm_g[:, None, :], odd_out_w.astype(BF16))

    op, k_t, v_t, c_f, c_b, n_new, m_new = _odd_context(yp, modv, *odd)
    new_k, new_v = jnp.swapaxes(k_t, -1, -2), jnp.swapaxes(v_t, -1, -2)

    t_dec = x_sample.shape[1]
    cos_t, sin_t = _rope_tables(t_dec)
    kc = jnp.concatenate([cache_k[:, 0], cache_k[:, 0]], axis=-1)
    vc = jnp.concatenate([cache_v[:, 0], cache_v[:, 0]], axis=-1)
    c_in = jnp.concatenate([_pair_blockdiag(jnp.swapaxes(state_c_fwd[:, 0], -1, -2)),
                            _pair_blockdiag(jnp.swapaxes(state_c_bwd[:, 0], -1, -2))], axis=1)
    n_in = jnp.concatenate([state_n_fwd[:, 0].reshape(n_dec, N_PAIRS, LANES),
                            state_n_bwd[:, 0].reshape(n_dec, N_PAIRS, LANES)], axis=1)
    m_in = _lane_bcast(jnp.concatenate([state_m_fwd[:, 0], state_m_bwd[:, 0]], axis=1))
    os_ = _odd_latent(ys, modv, *odd, cos_t, sin_t, kc, vc, c_in, n_in, m_in)

    yp, ys = _mlp_layer(op, os_, modv, 1, norm_g, w1, w2)

    n_f = n_new[:, :N_PAIRS].reshape(n_ctx, N_HEADS, HEAD_DIM)[:, None]
    n_b = n_new[:, N_PAIRS:].reshape(n_ctx, N_HEADS, HEAD_DIM)[:, None]
    m_f = m_new[:, :N_HEADS, 0][:, None]
    m_b = m_new[:, N_HEADS:, 0][:, None]
    return (yp, ys, new_k, new_v, c_f, n_f, m_f, c_b, n_b, m_b)
```

```python
import functools

import jax
import jax.numpy as jnp
from jax import lax
from jax.experimental import pallas as pl
from jax.experimental.pallas import tpu as pltpu

F32 = jnp.float32
BF16 = jnp.bfloat16

D_MODEL = 1024
D_FF = 4 * D_MODEL
EPS = 1e-6
D_HALF = D_MODEL // 2
CONF_WIDTH = 31
CONF_HALO = 16
HEAD_DIM = 64
N_HEADS = 8
N_PAIRS = N_HEADS // 2
N_KV = 2
LANES = 128
CHUNK = 128
WINDOW = 128
GRID_W = 64
ROPE_BASE = 10000.0
ATT_SCALE = HEAD_DIM ** -0.5
D_IN_ODD = D_HALF + 2 * N_KV * HEAD_DIM + 4 * D_HALF + 4 * N_HEADS
ROW_CHUNK = 256
MLP_ROWS = 512
ODD_CTX_SEQS = 2
ODD_STAGGER = 6
EVEN_CONV_ROWS = 64
EVEN_CHUNKS = 2
EVEN_STAGGER = 3
COND_ROWS = 8
VMEM_LIMIT = 56 * 1024 * 1024


def _dot(a, b):
    return jnp.dot(a, b, preferred_element_type=F32)


def _dot_nt(a, b):
    return lax.dot_general(a, b, (((1,), (1,)), ((), ())), preferred_element_type=F32)


def _split3(x):
    hi = x.astype(BF16)
    r1 = x - hi.astype(F32)
    mid = r1.astype(BF16)
    lo = (r1 - mid.astype(F32)).astype(BF16)
    return hi, mid, lo


def _dot_exact_lhs(x, b01):
    hi, mid, lo = _split3(x)
    return _dot(hi, b01) + _dot(mid, b01) + _dot(lo, b01)


def _rms(x, g):
    return x * lax.rsqrt(jnp.mean(x * x, axis=-1, keepdims=True) + EPS) * g


def _norm_mod(x, g, shift, scale):
    return _rms(x, g) * (1.0 + scale) + shift


def _params(n_grid):
    return pltpu.CompilerParams(dimension_semantics=("arbitrary",) * n_grid, vmem_limit_bytes=VMEM_LIMIT)


def _const_spec(shape):
    zeros = (0,) * len(shape)
    return pl.BlockSpec(shape, lambda *_: zeros, pipeline_mode=pl.Buffered(1))


def _layer_spec(shape, layer):
    index = (layer,) + (0,) * len(shape)
    return pl.BlockSpec((None,) + shape, lambda *_: index, pipeline_mode=pl.Buffered(1))


def _mod_spec(layer, cond_of):
    return pl.BlockSpec((None, None, 6, D_MODEL), lambda *idx: (layer, cond_of(*idx), 0, 0))


def _mod_kernel(cond_ref, w_ref, b_ref, o_ref):
    s = jax.nn.silu(cond_ref[...]).astype(BF16)
    o_ref[...] = _dot(s, w_ref[...].astype(BF16)) + b_ref[...]


def _modulation(cond, mod_w, mod_b):
    depth = mod_w.shape[0]
    n_out = mod_w.shape[2]
    tn = 2 * D_MODEL
    out = pl.pallas_call(
        _mod_kernel,
        grid=(depth, n_out // tn),
        in_specs=[
            pl.BlockSpec((COND_ROWS, D_MODEL), lambda l, j: (0, 0)),
            pl.BlockSpec((None, D_MODEL, tn), lambda l, j: (l, 0, j)),
            pl.BlockSpec((None, 1, tn), lambda l, j: (l, 0, j)),
        ],
        out_specs=pl.BlockSpec((None, COND_ROWS, tn), lambda l, j: (l, 0, j)),
        out_shape=jax.ShapeDtypeStruct((depth, COND_ROWS, n_out), F32),
        compiler_params=_params(2),
        name="modulation",
    )(cond, mod_w, mod_b.reshape(depth, 1, n_out))
    return out.reshape(depth, COND_ROWS, 6, D_MODEL)


def _run_staggered(programs, stagger):
    programs = list(programs)
    live, rounds = [], 0
    while programs or live:
        if programs and rounds % stagger == 0:
            live.append(programs.pop(0))
        for g in list(live):
            try:
                next(g)
            except StopIteration:
                live.remove(g)
        rounds += 1


def _even_kernel(*refs, n_chunks, nsub):
    halos, (xc_ref, mod_ref, ng_ref, win_ref, caw_ref, cab_ref, lng_ref, lnb_ref, cbw_ref, wout_ref, o_ref,
            apad, cpad, bgs, zs) = refs[:2 * nsub], refs[2 * nsub:]
    shared = (mod_ref, ng_ref, win_ref, caw_ref, cab_ref, lng_ref, lnb_ref, cbw_ref, wout_ref)
    _run_staggered(
        (_even_chunk(pl.program_id(0) * nsub + k, halos[2 * k], xc_ref.at[k], halos[2 * k + 1], *shared,
                     o_ref.at[k], apad.at[k], cpad.at[k], bgs.at[k], zs.at[k], n_chunks=n_chunks)
         for k in range(nsub)), EVEN_STAGGER)


def _even_chunk(g, xp_ref, xc_ref, xn_ref, mod_ref, ng_ref, win_ref, caw_ref, cab_ref, lng_ref, lnb_ref,
                cbw_ref, wout_ref, o_ref, apad, cpad, bgs, zs, *, n_chunks):
    c = g % n_chunks if n_chunks > 1 else 0
    rows = ROW_CHUNK + 2 * CONF_HALO
    shift1, scale1, gate1 = mod_ref[0:1, :], mod_ref[1:2, :], mod_ref[2:3, :]
    own = slice(CONF_HALO, CONF_HALO + ROW_CHUNK)
    if n_chunks == 1:
        h = _norm_mod(xc_ref[...], ng_ref[0:1, :], shift1, scale1).astype(BF16)
        h_own = h
        for pad_ref in (apad, cpad):
            pad_ref[0:CONF_HALO, :] = jnp.zeros((CONF_HALO, D_HALF), F32)
            pad_ref[CONF_HALO + ROW_CHUNK:rows, :] = jnp.zeros((CONF_HALO, D_HALF), F32)
        keep = lambda v: v
        span = own
    else:
        xh = jnp.concatenate([xp_ref[...], xc_ref[...], xn_ref[...]], axis=0)
        h = _norm_mod(xh, ng_ref[0:1, :], shift1, scale1).astype(BF16)
        h_own = h[own]
        ri = lax.broadcasted_iota(jnp.int32, (rows, D_HALF), 0)
        lo = jnp.where(c == 0, CONF_HALO, 0)
        hi = jnp.where(c == n_chunks - 1, CONF_HALO + ROW_CHUNK, rows)
        inside = (ri >= lo) & (ri < hi)
        keep = lambda v: jnp.where(inside, v, 0.0)
        span = slice(0, rows)
    a = _dot(h, win_ref[:, 0:D_HALF]) * jax.nn.sigmoid(_dot(h, win_ref[:, D_HALF:2 * D_HALF]))
    apad[span, :] = keep(a)
    yield
    cx = _dot(h, win_ref[:, 3 * D_HALF:4 * D_HALF]) * _dot(h, win_ref[:, 4 * D_HALF:5 * D_HALF])
    cpad[span, :] = keep(cx)
    yield
    bgs[...] = _dot(h_own, win_ref[:, 2 * D_HALF:3 * D_HALF])
    yield

    sub = EVEN_CONV_ROWS
    tile = 8
    for j in range(ROW_CHUNK // sub):
        r0 = j * sub
        groups = []
        for cg in range(D_HALF // LANES):
            cols = slice(cg * LANES, (cg + 1) * LANES)
            acc = None
            for r in range(tile):
                part = None
                for m in range(-(-(CONF_WIDTH + 1) // tile)):
                    o = tile * m + r
                    if 1 <= o <= CONF_WIDTH:
                        term = caw_ref[o - 1:o, cols] * apad[r0 + tile * m:r0 + tile * m + sub + tile, cols]
                        part = term if part is None else part + term
                shifted = part[r:r + sub, :]
                acc = shifted if acc is None else acc + shifted
            groups.append(acc)
        acc = jnp.concatenate(groups, axis=1) + cab_ref[...]
        mu = jnp.mean(acc, axis=-1, keepdims=True)
        dlt = acc - mu
        var = jnp.mean(dlt * dlt, axis=-1, keepdims=True)
        a_out = jax.nn.silu(dlt * lax.rsqrt(var + EPS) * lng_ref[...] + lnb_ref[...])
        zs[r0:r0 + sub, 0:D_HALF] = a_out.astype(BF16)
        base = r0 + CONF_HALO - 1
        sc = (cbw_ref[0:1, :] * cpad[base:base + sub, :]
              + cbw_ref[1:2, :] * cpad[base + 1:base + 1 + sub, :]
              + cbw_ref[2:3, :] * cpad[base + 2:base + 2 + sub, :])
        zs[r0:r0 + sub, D_HALF:D_MODEL] = (bgs[r0:r0 + sub, :] * sc).astype(BF16)
        yield

    o = _dot(zs[...], wout_ref[...])
    o_ref[...] = xc_ref[...] + gate1 * _rms(o, ng_ref[1:2, :])
    yield


def _even_layer(x, modv, layer, cond_base, cond_stride, ng, win, caw, cab, lng, lnb, cbw, wout):
    j = layer // 2
    b, t, _ = x.shape
    n_chunks = t // ROW_CHUNK
    nsub = EVEN_CHUNKS
    assert (b * n_chunks) % nsub == 0 and (cond_stride == 0 or n_chunks % nsub == 0)
    hpc = ROW_CHUNK // CONF_HALO
    n_halo_blocks = b * t // CONF_HALO
    rows = ROW_CHUNK + 2 * CONF_HALO
    halo_specs = []
    for k in range(nsub):
        halo_specs += [
            pl.BlockSpec((None, CONF_HALO, D_MODEL),
                         lambda i, k=k: (jnp.maximum((i * nsub + k) * hpc - 1, 0), 0, 0)),
            pl.BlockSpec((None, CONF_HALO, D_MODEL),
                         lambda i, k=k: (jnp.minimum((i * nsub + k + 1) * hpc, n_halo_blocks - 1), 0, 0)),
        ]
    x_halo = x.reshape(n_halo_blocks, CONF_HALO, D_MODEL)
    x_chunks = x.reshape(b * n_chunks, ROW_CHUNK, D_MODEL)
    kern = functools.partial(_even_kernel, n_chunks=n_chunks, nsub=nsub)
    out = pl.pallas_call(
        kern,
        grid=(b * n_chunks // nsub,),
        in_specs=halo_specs + [
            pl.BlockSpec((nsub, ROW_CHUNK, D_MODEL), lambda i: (i, 0, 0)),
            _mod_spec(layer, lambda i: cond_base + cond_stride * ((i * nsub) // n_chunks)),
            _layer_spec((4, D_MODEL), layer),
            _layer_spec((D_MODEL, 5 * D_HALF), j),
            _layer_spec((CONF_WIDTH, D_HALF), j),
            _layer_spec((1, D_HALF), j),
            _layer_spec((1, D_HALF), j),
            _layer_spec((1, D_HALF), j),
            _layer_spec((3, D_HALF), j),
            _layer_spec((D_MODEL, D_MODEL), j),
        ],
        out_specs=pl.BlockSpec((nsub, ROW_CHUNK, D_MODEL), lambda i: (i, 0, 0)),
        out_shape=jax.ShapeDtypeStruct(x_chunks.shape, F32),
        scratch_shapes=[
            pltpu.VMEM((nsub, rows, D_HALF), F32),
            pltpu.VMEM((nsub, rows, D_HALF), F32),
            pltpu.VMEM((nsub, ROW_CHUNK, D_HALF), F32),
            pltpu.VMEM((nsub, ROW_CHUNK, D_MODEL), BF16),
        ],
        compiler_params=_params(1),
        name="even_mixer",
    )(*([x_halo] * (2 * nsub)), x_chunks, modv, ng, win, caw, cab, lng, lnb, cbw, wout)
    return out.reshape(b, t, D_MODEL)


def _mlp_kernel(xp_ref, xs_ref, mod_ref, ng_ref, w1_hbm, w2_hbm, op_ref, os_ref, w1_s, w2_s, stage, sem, *,
                layer, n_ctx_tiles):
    n_blocks = D_FF // D_MODEL

    def block_copy(k):
        c, slot = k // 2, k % 2
        span = pl.ds(c * D_MODEL, D_MODEL)
        src = w1_hbm.at[layer, :, span] if k % 2 == 0 else w2_hbm.at[layer, span, :]
        return pltpu.make_async_copy(src, stage.at[slot], sem.at[slot])

    def fetch(k, dst):
        block_copy(k).wait()
        dst[...] = stage[k % 2].astype(BF16)
        if k + 2 < 2 * n_blocks:
            block_copy(k + 2).start()

    def tile(x_ref, o_ref, load=False):
        x = x_ref[...]
        h = _norm_mod(x, ng_ref[2:3, :], mod_ref[3:4, :], mod_ref[4:5, :]).astype(BF16)
        acc = None
        for c in range(n_blocks):
            cols = slice(c * D_MODEL, (c + 1) * D_MODEL)
            if load:
                fetch(2 * c, w1_s.at[:, cols])
            hid = jnp.square(jnp.maximum(_dot(h, w1_s[:, cols]), 0.0)).astype(BF16)
            if load:
                fetch(2 * c + 1, w2_s.at[cols, :])
            part = _dot(hid, w2_s[cols, :])
            acc = part if acc is None else acc + part
        o_ref[...] = x + mod_ref[5:6, :] * _rms(acc, ng_ref[3:4, :])

    i = pl.program_id(0)

    @pl.when(i == 0)
    def _():
        block_copy(0).start()
        block_copy(1).start()
        tile(xp_ref, op_ref, load=True)

    @pl.when((i > 0) & (i < n_ctx_tiles))
    def _():
        tile(xp_ref, op_ref)

    @pl.when(i >= n_ctx_tiles)
    def _():
        tile(xs_ref, os_ref)


def _mlp_layer(xp, xs, modv, layer, ng, w1, w2):
    tm = MLP_ROWS
    xp2, xs2 = xp.reshape(-1, D_MODEL), xs.reshape(-1, D_MODEL)
    t_dec = xs.shape[1]
    assert xp2.shape[0] % tm == 0 and t_dec % tm == 0
    n_p, n_s, per_seq = xp2.shape[0] // tm, xs2.shape[0] // tm, t_dec // tm
    ctx_tile = lambda i: (jnp.minimum(i, n_p - 1), 0)
    lat_tile = lambda i: (jnp.maximum(i - n_p, 0), 0)
    yp, ys = pl.pallas_call(
        functools.partial(_mlp_kernel, layer=layer, n_ctx_tiles=n_p),
        grid=(n_p + n_s,),
        in_specs=[
            pl.BlockSpec((tm, D_MODEL), ctx_tile),
            pl.BlockSpec((tm, D_MODEL), lat_tile),
            _mod_spec(layer, lambda i: jnp.where(i < n_p, 0, 1 + jnp.maximum(i - n_p, 0) // per_seq)),
            _layer_spec((4, D_MODEL), layer),
            pl.BlockSpec(memory_space=pl.ANY),
            pl.BlockSpec(memory_space=pl.ANY),
        ],
        out_specs=[pl.BlockSpec((tm, D_MODEL), ctx_tile), pl.BlockSpec((tm, D_MODEL), lat_tile)],
        out_shape=[jax.ShapeDtypeStruct(xp2.shape, F32), jax.ShapeDtypeStruct(xs2.shape, F32)],
        scratch_shapes=[
            pltpu.VMEM((D_MODEL, D_FF), BF16),
            pltpu.VMEM((D_FF, D_MODEL), BF16),
            pltpu.VMEM((2, D_MODEL, D_MODEL), F32),
            pltpu.SemaphoreType.DMA((2,)),
        ],
        compiler_params=_params(1),
        name="mlp",
    )(xp2, xs2, modv, ng, w1, w2)
    return yp.reshape(xp.shape), ys.reshape(xs.shape)


def _log_sigmoid(x):
    return jnp.minimum(x, 0.0) - jnp.log(1.0 + jnp.exp(-jnp.abs(x)))


def _rot_half(x, first_half):
    return jnp.where(first_half, pltpu.roll(x, 96, axis=1), pltpu.roll(x, 32, axis=1))


def _values_and_ones(v):
    vt = jnp.transpose(v)
    row = lax.broadcasted_iota(jnp.int32, vt.shape, 0)
    return jnp.where(row < HEAD_DIM, vt, 1.0)


def _aligned(i, m):
    return i * m if isinstance(i, int) else pl.multiple_of(i * m, m)


def _drain(pieces):
    for _ in pieces:
        pass


def _each(body, n, static, unroll=1):
    if static:
        for i in range(n):
            yield from body(i)
    else:
        lax.fori_loop(0, n, lambda i, c: (_drain(body(i)), c)[1], 0, unroll=unroll)


def _alternate(*programs):
    live = list(programs)
    while live:
        for g in list(live):
            try:
                next(g)
            except StopIteration:
                live.remove(g)
            else:
                yield


def _odd_kernel(*refs, t, lc, latent, nseq):
    shared = set(range(1, 11 if latent else 9))
    _run_staggered((_odd_seq(*[r if k in shared else r.at[sq] for k, r in enumerate(refs)],
                             t=t, lc=lc, latent=latent) for sq in range(nseq)), ODD_STAGGER)


def _odd_seq(*refs, t, lc, latent):
    if latent:
        (x_ref, mod_ref, ng_ref, win_ref, wg_ref, gb_ref, sink_ref, hn_ref, wout_ref,
         cos_ref, sin_ref, kc_ref, vc_ref, cin_ref, nin_ref, min_ref,
         o_ref,
         qa_s, qb_s, kk_s, vvt_s, qma_s, qmb_s, qm_s, km_s, vmt_s, vmtf_s, om_s, g_s, z_s, hft_s, hbt_s,
         sr_s, st_s, qc_s, c_s, n_s, m_s, s_s, p_s, kc_s, vct_s) = refs
    else:
        (x_ref, mod_ref, ng_ref, win_ref, wg_ref, gb_ref, sink_ref, hn_ref, wout_ref,
         o_ref, ko_ref, vo_ref, cf_ref, cb_ref, no_ref, mo_ref,
         qa_s, qb_s, kk_s, vvt_s, qma_s, qmb_s, qm_s, km_s, vmt_s, vmtf_s, om_s, g_s, z_s, hft_s, hbt_s,
         sr_s, st_s, qc_s, c_s, n_s, m_s, s_s, p_s) = refs

    static = not latent
    n_blocks = t // CHUNK
    pad = CHUNK if latent else 0
    shift1, scale1, gate1 = mod_ref[0:1, :], mod_ref[1:2, :], mod_ref[2:3, :]
    lane = lax.broadcasted_iota(jnp.int32, (1, LANES), 1)
    left = lane < HEAD_DIM
    first_half = (lane % HEAD_DIM) < (HEAD_DIM // 2)
    ti = lax.broadcasted_iota(jnp.int32, (CHUNK, CHUNK), 0)
    si = lax.broadcasted_iota(jnp.int32, (CHUNK, CHUNK), 1)
    top = ti < HEAD_DIM
    same_head = top == (si < HEAD_DIM)

    if latent:
        for kv in range(N_KV):
            kk_s[kv, 0:CHUNK, :] = jnp.zeros((CHUNK, LANES), BF16)
            kk_s[kv, CHUNK + t:2 * CHUNK + t, :] = jnp.zeros((CHUNK, LANES), BF16)
            vvt_s[kv, :, 0:CHUNK] = jnp.zeros((LANES, CHUNK), BF16)
            vvt_s[kv, :, CHUNK + t:2 * CHUNK + t] = jnp.zeros((LANES, CHUNK), BF16)
            kc_s[kv] = kc_ref[kv].astype(BF16)
            vct_s[kv] = _values_and_ones(vc_ref[kv]).astype(BF16)

    def project(rc):
        r0 = _aligned(rc, ROW_CHUNK)
        rows = pl.ds(r0, ROW_CHUNK)
        krows = pl.ds(r0 + pad, ROW_CHUNK)
        h = _norm_mod(x_ref[rows, :], ng_ref[0:1, :], shift1, scale1).astype(BF16)
        if latent:
            cs, sn = cos_ref[rows, :], sin_ref[rows, :]
        q = _dot(h, win_ref[:, 0:D_HALF])
        for p in range(N_PAIRS):
            cols = slice(p * LANES, (p + 1) * LANES)
            qp = q[:, cols]
            if latent:
                qp = qp * cs + _rot_half(qp, first_half) * sn
            qp = qp * ATT_SCALE
            qa_s[rows, cols] = jnp.where(left, qp, 0.0).astype(BF16)
            qb_s[rows, cols] = jnp.where(left, 0.0, qp).astype(BF16)
        yield
        kv2 = _dot(h, win_ref[:, D_HALF:D_HALF + 2 * LANES])
        ka, va = kv2[:, 0:LANES], kv2[:, LANES:2 * LANES]
        if latent:
            ka = ka * cs + _rot_half(ka, first_half) * sn
        kr = pltpu.roll(ka, HEAD_DIM, axis=1)
        vr = pltpu.roll(va, HEAD_DIM, axis=1)
        if not latent:
            ka_t, va_t = jnp.transpose(ka), jnp.transpose(va)
            for kv in range(N_KV):
                ko_ref[0, kv, :, rows] = ka_t[kv * HEAD_DIM:(kv + 1) * HEAD_DIM, :]
                vo_ref[0, kv, :, rows] = va_t[kv * HEAD_DIM:(kv + 1) * HEAD_DIM, :]
        kk_s[0, krows, :] = jnp.where(left, ka, kr).astype(BF16)
        kk_s[1, krows, :] = jnp.where(left, kr, ka).astype(BF16)
        vvt_s[0, :, krows] = _values_and_ones(va).astype(BF16)
        vvt_s[1, :, krows] = _values_and_ones(vr).astype(BF16)
        yield
        base = D_HALF + 2 * LANES
        qm = _dot(h, win_ref[:, base:base + D_HALF])
        qm_s[rows, :] = qm.astype(BF16)
        for p in range(N_PAIRS):
            cols = slice(p * LANES, (p + 1) * LANES)
            qma_s[rows, cols] = jnp.where(left, qm[:, cols], 0.0).astype(BF16)
            qmb_s[rows, cols] = jnp.where(left, 0.0, qm[:, cols]).astype(BF16)
        yield
        vm = _dot(h, win_ref[:, base + 2 * D_HALF:base + 3 * D_HALF])
        for p in range(N_PAIRS):
            cols = slice(p * LANES, (p + 1) * LANES)
            vt = jnp.transpose(vm[:, cols])
            vmtf_s[cols, rows] = vt
            vmt_s[cols, rows] = vt.astype(BF16)
        yield
        km = _dot(h, win_ref[:, base + D_HALF:base + 2 * D_HALF]) * (HEAD_DIM ** -0.5)
        km_s[rows, :] = km.astype(BF16)
        yield
        om_s[rows, :] = _dot(h, win_ref[:, base + 3 * D_HALF:base + 4 * D_HALF])
        g_s[rows, :] = _dot(h, wg_ref[...]) + gb_ref[...]
        yield

    yield from _each(project, t // ROW_CHUNK, static)

    kj = lax.broadcasted_iota(jnp.int32, (3 * CHUNK, 2 * CHUNK), 0)
    qi = lax.broadcasted_iota(jnp.int32, (3 * CHUNK, 2 * CHUNK), 1) % CHUNK
    band_ok = jnp.abs(kj - CHUNK - qi) <= WINDOW
    head_a = lax.broadcasted_iota(jnp.int32, (1, 2 * CHUNK), 1) < CHUNK

    n_keys = _att_keys(lc, latent)
    group = _att_group(latent)

    def attention():
        def attend(i):
            r0 = _aligned(i, CHUNK)
            rows = pl.ds(r0, CHUNK)
            if latent:
                key_pos = kj + (i - 1) * CHUNK
                mask = band_ok & (key_pos >= 0) & (key_pos < t)
                win = pl.ds(r0, 3 * CHUNK)
            for g0 in range(0, N_PAIRS, group):
                for gi in range(group):
                    p = g0 + gi
                    kv = p // (N_PAIRS // N_KV)
                    cols = slice(p * LANES, (p + 1) * LANES)
                    q2 = jnp.concatenate([qa_s[rows, cols], qb_s[rows, cols]], axis=0)
                    if latent:
                        s_s[gi, 0:lc, :] = _dot_nt(kc_s[kv], q2)
                        s_s[gi, lc:n_keys, :] = jnp.where(mask, _dot_nt(kk_s[kv, win, :], q2), -jnp.inf)
                    else:
                        s_s[gi] = _dot_nt(kk_s[kv], q2)
                yield
                maxes = []
                for gi in range(group):
                    p = g0 + gi
                    sink = jnp.where(head_a, sink_ref[2 * p:2 * p + 1, 0:1], sink_ref[2 * p + 1:2 * p + 2, 0:1])
                    mx = jnp.maximum(jnp.max(s_s[gi], axis=0, keepdims=True), sink)
                    p_s[gi] = jnp.exp(s_s[gi] - mx).astype(BF16)
                    maxes.append((sink, mx))
                yield
                for gi in range(group):
                    p = g0 + gi
                    kv = p // (N_PAIRS // N_KV)
                    cols = slice(p * LANES, (p + 1) * LANES)
                    if latent:
                        num = (_dot(vct_s[kv], p_s[gi, 0:lc, :]) + _dot(vvt_s[kv, :, win], p_s[gi, lc:n_keys, :]))
                    else:
                        num = _dot(vvt_s[kv], p_s[gi])
                    sink, mx = maxes[gi]
                    den = num[HEAD_DIM:HEAD_DIM + 1, :] + jnp.exp(sink - mx)
                    out = num[0:HEAD_DIM, :] * (1.0 / den)
                    pair = jnp.concatenate([out[:, 0:CHUNK], out[:, CHUNK:2 * CHUNK]], axis=0)
                    z_s[rows, cols] = jnp.transpose(pair).astype(BF16)
                yield

        return attend

    if latent:
        c_s[...] = cin_ref[...]
        n_s[...] = nin_ref[...]
        m_s[...] = min_ref[...]
    else:
        c_s[...] = jnp.zeros(c_s.shape, F32)
        n_s[...] = jnp.zeros(n_s.shape, F32)
        m_s[...] = jnp.zeros(m_s.shape, F32)

    see = (ti <= si, ti >= si)
    tri = tuple(m.astype(F32).astype(BF16) for m in see)
    last = (CHUNK - 1, 0)

    def mlstm(i):
        offs = (_aligned(i, CHUNK), _aligned(n_blocks - 1 - i, CHUNK))
        half = (slice(0, LANES), slice(LANES, 2 * LANES))
        a_rows, b_rows = [], []
        for d in range(2):
            gt = jnp.transpose(g_s[pl.ds(offs[d], CHUNK), :])
            lf = _log_sigmoid(gt[2 * N_HEADS:4 * N_HEADS, :])
            bcum = _dot_exact_lhs(lf, tri[d])[8 * d:8 * d + 8, :]
            a_rows.append(gt[8 * d:8 * d + 8, :] - bcum)
            b_rows.append(bcum)
        yield
        q_ns = []
        for d in range(2):
            rows = pl.ds(offs[d], CHUNK)
            for p in range(N_PAIRS):
                u = d * N_PAIRS + p
                cols = slice(p * LANES, (p + 1) * LANES)
                q2 = jnp.concatenate([qma_s[rows, cols], qmb_s[rows, cols]], axis=0)
                sr_s[u] = _dot_nt(km_s[rows, cols], q2)
                qc_s[u] = _dot_nt(c_s[u].astype(BF16), qm_s[rows, cols])
                n8 = jnp.broadcast_to(n_s[u:u + 1, :], (8, LANES)).astype(BF16)
                q_ns.append(_dot_nt(n8, q2)[0:1, :])
            yield
        stats = {}
        for d in range(2):
            for hd in range(N_HEADS):
                u, j = d * N_PAIRS + hd // 2, hd % 2
                mrow = d * N_HEADS + hd
                a_row = a_rows[d][hd:hd + 1, :]
                b_row = b_rows[d][hd:hd + 1, :]
                m_prev = m_s[mrow:mrow + 1, 0:1]
                a_col = jnp.transpose(jnp.broadcast_to(a_row, (CHUNK, CHUNK)))
                z_t = jnp.where(see[d], a_col, -jnp.inf)
                m_run = jnp.maximum(jnp.max(z_t, axis=0, keepdims=True), m_prev)
                s_t = sr_s[u, :, half[j]] * jnp.exp(z_t - m_run)
                st_s[u, :, half[j]] = s_t.astype(BF16)
                w_int = jnp.exp(m_prev - m_run)
                den = jnp.sum(s_t, axis=0, keepdims=True) + w_int * q_ns[u][:, half[j]]
                inv = 1.0 / jnp.maximum(jnp.abs(den), jnp.exp(-(b_row + m_run)))
                m_last = m_run[:, last[d]:last[d] + 1]
                stats[(u, j)] = (w_int, inv, jnp.exp(a_row - m_last), jnp.exp(m_prev - m_last))
                m_s[mrow:mrow + 1, :] = jnp.broadcast_to(b_row[:, last[d]:last[d] + 1] + m_last, (1, LANES))
            yield
        for d in range(2):
            rows = pl.ds(offs[d], CHUNK)
            ht_s = (hft_s, hbt_s)[d]
            for p in range(N_PAIRS):
                u = d * N_PAIRS + p
                cols = slice(p * LANES, (p + 1) * LANES)
                (w_a, inv_a, e_a, dec_a), (w_b, inv_b, e_b, dec_b) = stats[(u, 0)], stats[(u, 1)]
                kp = km_s[rows, cols]
                num2 = _dot(vmt_s[cols, rows], st_s[u])
                num = jnp.where(top, num2[:, half[0]], num2[:, half[1]])
                ht_s[cols, rows] = (num + jnp.where(top, w_a, w_b) * qc_s[u]) * jnp.where(top, inv_a, inv_b)
                vt_e = (vmtf_s[cols, rows] * jnp.where(top, e_a, e_b)).astype(BF16)
                c_s[u] = jnp.where(top, dec_a, dec_b) * c_s[u] + jnp.where(same_head, _dot(vt_e, kp), 0.0)
                e2 = jnp.concatenate([e_a, e_b, jnp.zeros((6, CHUNK), F32)], axis=0).astype(BF16)
                n_k = _dot(e2, kp)
                n_s[u:u + 1, :] = (jnp.where(left, dec_a, dec_b) * n_s[u:u + 1, :]
                                   + jnp.where(left, n_k[0:1, :], n_k[1:2, :]))
            yield

    attend = attention()
    yield from _each(lambda i: _alternate(attend(i), mlstm(i)), n_blocks, static, unroll=2)

    if not latent:
        for d, c_ref in enumerate((cf_ref, cb_ref)):
            for p in range(N_PAIRS):
                c_pair = jnp.transpose(c_s[d * N_PAIRS + p])
                c_ref[0, 2 * p] = c_pair[0:HEAD_DIM, 0:HEAD_DIM]
                c_ref[0, 2 * p + 1] = c_pair[HEAD_DIM:LANES, HEAD_DIM:LANES]
        no_ref[...] = n_s[...]
        mo_ref[...] = m_s[...]
        yield

    top_w = lax.broadcasted_iota(jnp.int32, (LANES, ROW_CHUNK), 0) < HEAD_DIM

    def finish(rc):
        r0 = _aligned(rc, ROW_CHUNK)
        rows = pl.ds(r0, ROW_CHUNK)
        for p in range(N_PAIRS):
            cols = slice(p * LANES, (p + 1) * LANES)
            hm = hft_s[cols, rows] + hbt_s[cols, rows]
            sq = hm * hm
            ms_a = jnp.sum(sq[0:HEAD_DIM], axis=0, keepdims=True)
            ms_b = jnp.sum(sq[HEAD_DIM:LANES], axis=0, keepdims=True)
            ms = jnp.where(top_w, ms_a, ms_b) * (1.0 / HEAD_DIM)
            y = jnp.transpose(hm * lax.rsqrt(ms + EPS)) * hn_ref[:, cols] * jax.nn.sigmoid(om_s[rows, cols])
            z_s[rows, D_HALF + p * LANES:D_HALF + (p + 1) * LANES] = y.astype(BF16)
            if p % 2:
                yield
        o = _dot(z_s[rows, :], wout_ref[...])
        o_ref[rows, :] = x_ref[rows, :] + gate1 * _rms(o, ng_ref[1:2, :])
        yield

    yield from _each(finish, t // ROW_CHUNK, static)


def _att_keys(lc, latent):
    return lc + 3 * CHUNK if latent else lc


def _att_group(latent):
    return 2 if latent else N_PAIRS


def _odd_scratch(t, lc, latent, nseq):
    pad = 2 * CHUNK if latent else 0
    att = (_att_group(latent), _att_keys(lc, latent), 2 * CHUNK)
    shapes = [
        pltpu.VMEM((t, D_HALF), BF16),
        pltpu.VMEM((t, D_HALF), BF16),
        pltpu.VMEM((N_KV, t + pad, LANES), BF16),
        pltpu.VMEM((N_KV, LANES, t + pad), BF16),
        pltpu.VMEM((t, D_HALF), BF16),
        pltpu.VMEM((t, D_HALF), BF16),
        pltpu.VMEM((t, D_HALF), BF16),
        pltpu.VMEM((t, D_HALF), BF16),
        pltpu.VMEM((D_HALF, t), BF16),
        pltpu.VMEM((D_HALF, t), F32),
        pltpu.VMEM((t, D_HALF), F32),
        pltpu.VMEM((t, LANES), F32),
        pltpu.VMEM((t, D_MODEL), BF16),
        pltpu.VMEM((D_HALF, t), F32),
        pltpu.VMEM((D_HALF, t), F32),
        pltpu.VMEM((2 * N_PAIRS, CHUNK, 2 * LANES), F32),
        pltpu.VMEM((2 * N_PAIRS, CHUNK, 2 * LANES), BF16),
        pltpu.VMEM((2 * N_PAIRS, LANES, CHUNK), F32),
        pltpu.VMEM((2 * N_PAIRS, LANES, LANES), F32),
        pltpu.VMEM((2 * N_PAIRS, LANES), F32),
        pltpu.VMEM((2 * N_HEADS, LANES), F32),
        pltpu.VMEM(att, F32),
        pltpu.VMEM(att, BF16),
    ]
    if latent:
        shapes += [pltpu.VMEM((N_KV, lc, LANES), BF16), pltpu.VMEM((N_KV, LANES, lc), BF16)]
    return [pltpu.VMEM((nseq,) + tuple(sh.shape), sh.dtype) for sh in shapes]


def _odd_common_specs(t, layer, cond_base, cond_stride, nseq):
    assert cond_stride == 0 or nseq == 1
    j = layer // 2
    return [
        _per_seq((t, D_MODEL), nseq),
        _mod_spec(layer, lambda i: cond_base + cond_stride * i),
        _layer_spec((4, D_MODEL), layer),
        _layer_spec((D_MODEL, D_IN_ODD - 4 * N_HEADS), j),
        _const_spec((D_MODEL, LANES)),
        _const_spec((1, LANES)),
        _const_spec((N_HEADS, LANES)),
        _layer_spec((1, D_HALF), j),
        _layer_spec((D_MODEL, D_MODEL), j),
    ]


def _per_seq(shape, nseq):
    return pl.BlockSpec((nseq,) + shape, lambda i: (i,) + (0,) * len(shape))


def _odd_context(x, modv, layer, ng, w_main, w_gate, gate_bias, sink_b, hnorm, wout):
    b, t, _ = x.shape
    nseq = ODD_CTX_SEQS
    assert b % nseq == 0
    kern = functools.partial(_odd_kernel, t=t, lc=t, latent=False, nseq=nseq)
    per_seq = functools.partial(_per_seq, nseq=nseq)
    return pl.pallas_call(
        kern,
        grid=(b // nseq,),
        in_specs=_odd_common_specs(t, layer, 0, 0, nseq),
        out_specs=[per_seq((t, D_MODEL)),
                   per_seq((1, N_KV, HEAD_DIM, t)), per_seq((1, N_KV, HEAD_DIM, t)),
                   per_seq((1, N_HEADS, HEAD_DIM, HEAD_DIM)), per_seq((1, N_HEADS, HEAD_DIM, HEAD_DIM)),
                   per_seq((2 * N_PAIRS, LANES)), per_seq((2 * N_HEADS, LANES))],
        out_shape=[jax.ShapeDtypeStruct((b, t, D_MODEL), F32),
                   jax.ShapeDtypeStruct((b, 1, N_KV, HEAD_DIM, t), F32),
                   jax.ShapeDtypeStruct((b, 1, N_KV, HEAD_DIM, t), F32),
                   jax.ShapeDtypeStruct((b, 1, N_HEADS, HEAD_DIM, HEAD_DIM), F32),
                   jax.ShapeDtypeStruct((b, 1, N_HEADS, HEAD_DIM, HEAD_DIM), F32),
                   jax.ShapeDtypeStruct((b, 2 * N_PAIRS, LANES), F32),
                   jax.ShapeDtypeStruct((b, 2 * N_HEADS, LANES), F32)],
        scratch_shapes=_odd_scratch(t, t, False, nseq),
        compiler_params=_params(1),
        name="odd_mixer_context",
    )(x, modv, ng, w_main, w_gate, gate_bias, sink_b, hnorm, wout)


def _odd_latent(x, modv, layer, ng, w_main, w_gate, gate_bias, sink_b, hnorm, wout, cos_t, sin_t, kc, vc,
                c_in, n_in, m_in):
    b, t, _ = x.shape
    lc = kc.shape[2]
    kern = functools.partial(_odd_kernel, t=t, lc=lc, latent=True, nseq=1)
    per_seq = functools.partial(_per_seq, nseq=1)
    return pl.pallas_call(
        kern,
        grid=(b,),
        in_specs=_odd_common_specs(t, layer, 1, 1, 1) + [
            _const_spec((t, LANES)), _const_spec((t, LANES)),
            per_seq((N_KV, lc, LANES)), per_seq((N_KV, lc, LANES)),
            per_seq((2 * N_PAIRS, LANES, LANES)), per_seq((2 * N_PAIRS, LANES)), per_seq((2 * N_HEADS, LANES)),
        ],
        out_specs=per_seq((t, D_MODEL)),
        out_shape=jax.ShapeDtypeStruct((b, t, D_MODEL), F32),
        scratch_shapes=_odd_scratch(t, lc, True, 1),
        compiler_params=_params(1),
        name="odd_mixer_latent",
    )(x, modv, ng, w_main, w_gate, gate_bias, sink_b, hnorm, wout, cos_t, sin_t, kc, vc, c_in, n_in, m_in)


def _rope_tables(t):
    rows = t // GRID_W
    row = jnp.broadcast_to(jnp.arange(rows)[:, None], (rows, GRID_W)).reshape(t).astype(F32)
    col = jnp.broadcast_to(jnp.arange(GRID_W)[None, :], (rows, GRID_W)).reshape(t).astype(F32)
    n_freq = HEAD_DIM // 4
    inv_freq = ROPE_BASE ** (-jnp.arange(n_freq, dtype=F32) / n_freq)
    ang = jnp.concatenate([row[:, None] * inv_freq, col[:, None] * inv_freq], axis=-1)
    cos, sin = jnp.cos(ang), jnp.sin(ang)
    cos_l = jnp.tile(cos, (1, LANES // cos.shape[1]))
    sin_l = jnp.tile(jnp.concatenate([-sin, sin], axis=-1), (1, LANES // HEAD_DIM))
    return cos_l, sin_l


def _pair_blockdiag(c):
    b = c.shape[0]
    c = c.reshape(b, N_PAIRS, 2, HEAD_DIM, HEAD_DIM)
    z = jnp.zeros_like(c[:, :, 0])
    top = jnp.concatenate([c[:, :, 0], z], axis=-1)
    bot = jnp.concatenate([z, c[:, :, 1]], axis=-1)
    return jnp.concatenate([top, bot], axis=-2)


def _lane_bcast(v):
    return jnp.broadcast_to(v[..., None], v.shape + (LANES,))


def kernel(x_prompt, x_sample, c, cache_k, cache_v, state_c_fwd, state_n_fwd, state_m_fwd, state_c_bwd, state_n_bwd, state_m_bwd, c_ctx, mod_w, mod_b, norm_g, mlp_w1, mlp_w2, even_in_w, conv_a_w, conv_a_b, ln_a_g, ln_a_b, conv_b_w, even_out_w, odd_in_w, attn_sink, gate_b, hnorm_g, odd_out_w):
    n_dec = x_sample.shape[0]
    n_ctx = x_prompt.shape[0]
    cond = jnp.concatenate([c_ctx[None, :], c, jnp.zeros((COND_ROWS - 1 - n_dec, D_MODEL), F32)], axis=0)
    modv = _modulation(cond, mod_w, mod_b)

    yp, ys = x_prompt, x_sample
    w1, w2 = mlp_w1, mlp_w2

    ev = (norm_g, even_in_w.astype(BF16), conv_a_w, conv_a_b[:, None, :], ln_a_g[:, None, :], ln_a_b[:, None, :],
          conv_b_w, even_out_w.astype(BF16))
    yp = _even_layer(yp, modv, 0, 0, 0, *ev)
    ys = _even_layer(ys, modv, 0, 1, 1, *ev)
    yp, ys = _mlp_layer(yp, ys, modv, 0, norm_g, w1, w2)

    order = jnp.array([0, 2, 1, 3])
    d_main = D_IN_ODD - 4 * N_HEADS
    wg = odd_in_w[0][:, d_main:].reshape(D_MODEL, 4, N_HEADS)[:, order, :].reshape(D_MODEL, 4 * N_HEADS)
    w_gate = jnp.pad(wg, ((0, 0), (0, LANES - 4 * N_HEADS))).astype(BF16)
    gate_bias = jnp.pad(gate_b[0][order, :].reshape(1, 4 * N_HEADS), ((0, 0), (0, LANES - 4 * N_HEADS)))
    sink_b = _lane_bcast(attn_sink[0])
    odd = (1, norm_g, odd_in_w[:, :, :d_main].astype(BF16), w_gate, gate_bias, sink_b, hnorm_g[:, None, :], odd_out_w.astype(BF16))

    op, k_t, v_t, c_f, c_b, n_new, m_new = _odd_context(yp, modv, *odd)
    new_k, new_v = jnp.swapaxes(k_t, -1, -2), jnp.swapaxes(v_t, -1, -2)

    t_dec = x_sample.shape[1]
    cos_t, sin_t = _rope_tables(t_dec)
    kc = jnp.concatenate([cache_k[:, 0], cache_k[:, 0]], axis=-1)
    vc = jnp.concatenate([cache_v[:, 0], cache_v[:, 0]], axis=-1)
    c_in = jnp.concatenate([_pair_blockdiag(jnp.swapaxes(state_c_fwd[:, 0], -1, -2)),
                            _pair_blockdiag(jnp.swapaxes(state_c_bwd[:, 0], -1, -2))], axis=1)
    n_in = jnp.concatenate([state_n_fwd[:, 0].reshape(n_dec, N_PAIRS, LANES),
                            state_n_bwd[:, 0].reshape(n_dec, N_PAIRS, LANES)], axis=1)
    m_in = _lane_bcast(jnp.concatenate([state_m_fwd[:, 0], state_m_bwd[:, 0]], axis=1))
    os_ = _odd_latent(ys, modv, *odd, cos_t, sin_t, kc, vc, c_in, n_in, m_in)

    yp, ys = _mlp_layer(op, os_, modv, 1, norm_g, w1, w2)

    n_f = n_new[:, :N_PAIRS].reshape(n_ctx, N_HEADS, HEAD_DIM)[:, None]
    n_b = n_new[:, N_PAIRS:].reshape(n_ctx, N_HEADS, HEAD_DIM)[:, None]
    m_f = m_new[:, :N_HEADS, 0][:, None]
    m_b = m_new[:, N_HEADS:, 0][:, None]
    return (yp, ys, new_k, new_v, c_f, n_f, m_f, c_b, n_b, m_b)
```

```python
import functools

import jax
import jax.numpy as jnp
from jax import lax
from jax.experimental import pallas as pl
from jax.experimental.pallas import tpu as pltpu

F32 = jnp.float32
BF16 = jnp.bfloat16

D_MODEL = 1024
D_FF = 4 * D_MODEL
EPS = 1e-6
D_HALF = D_MODEL // 2
CONF_WIDTH = 31
CONF_HALO = 16
HEAD_DIM = 64
N_HEADS = 8
N_PAIRS = N_HEADS // 2
N_KV = 2
LANES = 128
CHUNK = 128
WINDOW = 128
GRID_W = 64
ROPE_BASE = 10000.0
ATT_SCALE = HEAD_DIM ** -0.5
D_IN_ODD = D_HALF + 2 * N_KV * HEAD_DIM + 4 * D_HALF + 4 * N_HEADS
ROW_CHUNK = 256
MLP_ROWS = 512
MLP_STAGGER = 3
ODD_CTX_SEQS = 2
ODD_STAGGER = 6
EVEN_CONV_ROWS = 64
EVEN_CHUNKS = 2
EVEN_STAGGER = 3
COND_ROWS = 8
VMEM_LIMIT = 56 * 1024 * 1024


def _dot(a, b):
    return jnp.dot(a, b, preferred_element_type=F32)


def _dot_nt(a, b):
    return lax.dot_general(a, b, (((1,), (1,)), ((), ())), preferred_element_type=F32)


def _split3(x):
    hi = x.astype(BF16)
    r1 = x - hi.astype(F32)
    mid = r1.astype(BF16)
    lo = (r1 - mid.astype(F32)).astype(BF16)
    return hi, mid, lo


def _dot_exact_lhs(x, b01):
    hi, mid, lo = _split3(x)
    return _dot(hi, b01) + _dot(mid, b01) + _dot(lo, b01)


def _rms(x, g):
    return x * lax.rsqrt(jnp.mean(x * x, axis=-1, keepdims=True) + EPS) * g


def _norm_mod(x, g, shift, scale):
    return _rms(x, g) * (1.0 + scale) + shift


def _params(n_grid):
    return pltpu.CompilerParams(dimension_semantics=("arbitrary",) * n_grid, vmem_limit_bytes=VMEM_LIMIT)


def _const_spec(shape):
    zeros = (0,) * len(shape)
    return pl.BlockSpec(shape, lambda *_: zeros, pipeline_mode=pl.Buffered(1))


def _layer_spec(shape, layer):
    index = (layer,) + (0,) * len(shape)
    return pl.BlockSpec((None,) + shape, lambda *_: index, pipeline_mode=pl.Buffered(1))


def _mod_spec(layer, cond_of):
    return pl.BlockSpec((None, None, 6, D_MODEL), lambda *idx: (layer, cond_of(*idx), 0, 0))


def _mod_kernel(cond_ref, w_ref, b_ref, o_ref):
    s = jax.nn.silu(cond_ref[...]).astype(BF16)
    o_ref[...] = _dot(s, w_ref[...].astype(BF16)) + b_ref[...]


def _modulation(cond, mod_w, mod_b):
    depth = mod_w.shape[0]
    n_out = mod_w.shape[2]
    tn = 2 * D_MODEL
    out = pl.pallas_call(
        _mod_kernel,
        grid=(depth, n_out // tn),
        in_specs=[
            pl.BlockSpec((COND_ROWS, D_MODEL), lambda l, j: (0, 0)),
            pl.BlockSpec((None, D_MODEL, tn), lambda l, j: (l, 0, j)),
            pl.BlockSpec((None, 1, tn), lambda l, j: (l, 0, j)),
        ],
        out_specs=pl.BlockSpec((None, COND_ROWS, tn), lambda l, j: (l, 0, j)),
        out_shape=jax.ShapeDtypeStruct((depth, COND_ROWS, n_out), F32),
        compiler_params=_params(2),
        name="modulation",
    )(cond, mod_w, mod_b.reshape(depth, 1, n_out))
    return out.reshape(depth, COND_ROWS, 6, D_MODEL)


def _run_staggered(programs, stagger):
    programs = list(programs)
    live, rounds = [], 0
    while programs or live:
        if programs and rounds % stagger == 0:
            live.append(programs.pop(0))
        for g in list(live):
            try:
                next(g)
            except StopIteration:
                live.remove(g)
        rounds += 1


def _even_kernel(*refs, n_chunks, nsub):
    halos, (xc_ref, mod_ref, ng_ref, win_ref, caw_ref, cab_ref, lng_ref, lnb_ref, cbw_ref, wout_ref, o_ref,
            apad, cpad, bgs, zs) = refs[:2 * nsub], refs[2 * nsub:]
    shared = (mod_ref, ng_ref, win_ref, caw_ref, cab_ref, lng_ref, lnb_ref, cbw_ref, wout_ref)
    _run_staggered(
        (_even_chunk(pl.program_id(0) * nsub + k, halos[2 * k], xc_ref.at[k], halos[2 * k + 1], *shared,
                     o_ref.at[k], apad.at[k], cpad.at[k], bgs.at[k], zs.at[k], n_chunks=n_chunks)
         for k in range(nsub)), EVEN_STAGGER)


def _even_chunk(g, xp_ref, xc_ref, xn_ref, mod_ref, ng_ref, win_ref, caw_ref, cab_ref, lng_ref, lnb_ref,
                cbw_ref, wout_ref, o_ref, apad, cpad, bgs, zs, *, n_chunks):
    c = g % n_chunks if n_chunks > 1 else 0
    rows = ROW_CHUNK + 2 * CONF_HALO
    shift1, scale1, gate1 = mod_ref[0:1, :], mod_ref[1:2, :], mod_ref[2:3, :]
    own = slice(CONF_HALO, CONF_HALO + ROW_CHUNK)
    if n_chunks == 1:
        h = _norm_mod(xc_ref[...], ng_ref[0:1, :], shift1, scale1).astype(BF16)
        h_own = h
        for pad_ref in (apad, cpad):
            pad_ref[0:CONF_HALO, :] = jnp.zeros((CONF_HALO, D_HALF), F32)
            pad_ref[CONF_HALO + ROW_CHUNK:rows, :] = jnp.zeros((CONF_HALO, D_HALF), F32)
        keep = lambda v: v
        span = own
    else:
        xh = jnp.concatenate([xp_ref[...], xc_ref[...], xn_ref[...]], axis=0)
        h = _norm_mod(xh, ng_ref[0:1, :], shift1, scale1).astype(BF16)
        h_own = h[own]
        ri = lax.broadcasted_iota(jnp.int32, (rows, D_HALF), 0)
        lo = jnp.where(c == 0, CONF_HALO, 0)
        hi = jnp.where(c == n_chunks - 1, CONF_HALO + ROW_CHUNK, rows)
        inside = (ri >= lo) & (ri < hi)
        keep = lambda v: jnp.where(inside, v, 0.0)
        span = slice(0, rows)
    a = _dot(h, win_ref[:, 0:D_HALF]) * jax.nn.sigmoid(_dot(h, win_ref[:, D_HALF:2 * D_HALF]))
    apad[span, :] = keep(a)
    yield
    cx = _dot(h, win_ref[:, 3 * D_HALF:4 * D_HALF]) * _dot(h, win_ref[:, 4 * D_HALF:5 * D_HALF])
    cpad[span, :] = keep(cx)
    yield
    bgs[...] = _dot(h_own, win_ref[:, 2 * D_HALF:3 * D_HALF])
    yield

    sub = EVEN_CONV_ROWS
    tile = 8
    for j in range(ROW_CHUNK // sub):
        r0 = j * sub
        groups = []
        for cg in range(D_HALF // LANES):
            cols = slice(cg * LANES, (cg + 1) * LANES)
            acc = None
            for r in range(tile):
                part = None
                for m in range(-(-(CONF_WIDTH + 1) // tile)):
                    o = tile * m + r
                    if 1 <= o <= CONF_WIDTH:
                        term = caw_ref[o - 1:o, cols] * apad[r0 + tile * m:r0 + tile * m + sub + tile, cols]
                        part = term if part is None else part + term
                shifted = part[r:r + sub, :]
                acc = shifted if acc is None else acc + shifted
            groups.append(acc)
        acc = jnp.concatenate(groups, axis=1) + cab_ref[...]
        mu = jnp.mean(acc, axis=-1, keepdims=True)
        dlt = acc - mu
        var = jnp.mean(dlt * dlt, axis=-1, keepdims=True)
        a_out = jax.nn.silu(dlt * lax.rsqrt(var + EPS) * lng_ref[...] + lnb_ref[...])
        zs[r0:r0 + sub, 0:D_HALF] = a_out.astype(BF16)
        base = r0 + CONF_HALO - 1
        sc = (cbw_ref[0:1, :] * cpad[base:base + sub, :]
              + cbw_ref[1:2, :] * cpad[base + 1:base + 1 + sub, :]
              + cbw_ref[2:3, :] * cpad[base + 2:base + 2 + sub, :])
        zs[r0:r0 + sub, D_HALF:D_MODEL] = (bgs[r0:r0 + sub, :] * sc).astype(BF16)
        yield

    o = _dot(zs[...], wout_ref[...])
    o_ref[...] = xc_ref[...] + gate1 * _rms(o, ng_ref[1:2, :])
    yield


def _even_layer(x, modv, layer, cond_base, cond_stride, ng, win, caw, cab, lng, lnb, cbw, wout):
    j = layer // 2
    b, t, _ = x.shape
    n_chunks = t // ROW_CHUNK
    nsub = EVEN_CHUNKS
    assert (b * n_chunks) % nsub == 0 and (cond_stride == 0 or n_chunks % nsub == 0)
    hpc = ROW_CHUNK // CONF_HALO
    n_halo_blocks = b * t // CONF_HALO
    rows = ROW_CHUNK + 2 * CONF_HALO
    halo_specs = []
    for k in range(nsub):
        halo_specs += [
            pl.BlockSpec((None, CONF_HALO, D_MODEL),
                         lambda i, k=k: (jnp.maximum((i * nsub + k) * hpc - 1, 0), 0, 0)),
            pl.BlockSpec((None, CONF_HALO, D_MODEL),
                         lambda i, k=k: (jnp.minimum((i * nsub + k + 1) * hpc, n_halo_blocks - 1), 0, 0)),
        ]
    x_halo = x.reshape(n_halo_blocks, CONF_HALO, D_MODEL)
    x_chunks = x.reshape(b * n_chunks, ROW_CHUNK, D_MODEL)
    kern = functools.partial(_even_kernel, n_chunks=n_chunks, nsub=nsub)
    out = pl.pallas_call(
        kern,
        grid=(b * n_chunks // nsub,),
        in_specs=halo_specs + [
            pl.BlockSpec((nsub, ROW_CHUNK, D_MODEL), lambda i: (i, 0, 0)),
            _mod_spec(layer, lambda i: cond_base + cond_stride * ((i * nsub) // n_chunks)),
            _layer_spec((4, D_MODEL), layer),
            _layer_spec((D_MODEL, 5 * D_HALF), j),
            _layer_spec((CONF_WIDTH, D_HALF), j),
            _layer_spec((1, D_HALF), j),
            _layer_spec((1, D_HALF), j),
            _layer_spec((1, D_HALF), j),
            _layer_spec((3, D_HALF), j),
            _layer_spec((D_MODEL, D_MODEL), j),
        ],
        out_specs=pl.BlockSpec((nsub, ROW_CHUNK, D_MODEL), lambda i: (i, 0, 0)),
        out_shape=jax.ShapeDtypeStruct(x_chunks.shape, F32),
        scratch_shapes=[
            pltpu.VMEM((nsub, rows, D_HALF), F32),
            pltpu.VMEM((nsub, rows, D_HALF), F32),
            pltpu.VMEM((nsub, ROW_CHUNK, D_HALF), F32),
            pltpu.VMEM((nsub, ROW_CHUNK, D_MODEL), BF16),
        ],
        compiler_params=_params(1),
        name="even_mixer",
    )(*([x_halo] * (2 * nsub)), x_chunks, modv, ng, win, caw, cab, lng, lnb, cbw, wout)
    return out.reshape(b, t, D_MODEL)


def _mlp_kernel(xp_ref, xs_ref, mod_ref, ng_ref, w1_hbm, w2_hbm, op_ref, os_ref, w1_s, w2_s, stage, sem, *,
                layer, n_ctx_tiles):
    n_blocks = D_FF // D_MODEL

    def block_copy(k):
        c, slot = k // 2, k % 2
        span = pl.ds(c * D_MODEL, D_MODEL)
        src = w1_hbm.at[layer, :, span] if k % 2 == 0 else w2_hbm.at[layer, span, :]
        return pltpu.make_async_copy(src, stage.at[slot], sem.at[slot])

    def fetch(k, dst):
        block_copy(k).wait()
        dst[...] = stage[k % 2].astype(BF16)
        if k + 2 < 2 * n_blocks:
            block_copy(k + 2).start()

    def rows_program(x_ref, o_ref, rows, load):
        x = x_ref[rows, :]
        h = _norm_mod(x, ng_ref[2:3, :], mod_ref[3:4, :], mod_ref[4:5, :]).astype(BF16)
        yield
        acc = None
        for c in range(n_blocks):
            cols = slice(c * D_MODEL, (c + 1) * D_MODEL)
            if load:
                fetch(2 * c, w1_s.at[:, cols])
            hid = jnp.square(jnp.maximum(_dot(h, w1_s[:, cols]), 0.0)).astype(BF16)
            yield
            if load:
                fetch(2 * c + 1, w2_s.at[cols, :])
            part = _dot(hid, w2_s[cols, :])
            acc = part if acc is None else acc + part
            yield
        o_ref[rows, :] = x + mod_ref[5:6, :] * _rms(acc, ng_ref[3:4, :])
        yield

    def tile(x_ref, o_ref, load=False):
        n_rows = x_ref.shape[0]
        if load:
            _drain(rows_program(x_ref, o_ref, slice(0, n_rows), True))
        else:
            _run_staggered((rows_program(x_ref, o_ref, slice(r0, r0 + n_rows // 2), False)
                            for r0 in (0, n_rows // 2)), MLP_STAGGER)

    i = pl.program_id(0)

    @pl.when(i == 0)
    def _():
        block_copy(0).start()
        block_copy(1).start()
        tile(xp_ref, op_ref, load=True)

    @pl.when((i > 0) & (i < n_ctx_tiles))
    def _():
        tile(xp_ref, op_ref)

    @pl.when(i >= n_ctx_tiles)
    def _():
        tile(xs_ref, os_ref)


def _mlp_layer(xp, xs, modv, layer, ng, w1, w2):
    tm = MLP_ROWS
    xp2, xs2 = xp.reshape(-1, D_MODEL), xs.reshape(-1, D_MODEL)
    t_dec = xs.shape[1]
    assert xp2.shape[0] % tm == 0 and t_dec % tm == 0
    n_p, n_s, per_seq = xp2.shape[0] // tm, xs2.shape[0] // tm, t_dec // tm
    ctx_tile = lambda i: (jnp.minimum(i, n_p - 1), 0)
    lat_tile = lambda i: (jnp.maximum(i - n_p, 0), 0)
    yp, ys = pl.pallas_call(
        functools.partial(_mlp_kernel, layer=layer, n_ctx_tiles=n_p),
        grid=(n_p + n_s,),
        in_specs=[
            pl.BlockSpec((tm, D_MODEL), ctx_tile),
            pl.BlockSpec((tm, D_MODEL), lat_tile),
            _mod_spec(layer, lambda i: jnp.where(i < n_p, 0, 1 + jnp.maximum(i - n_p, 0) // per_seq)),
            _layer_spec((4, D_MODEL), layer),
            pl.BlockSpec(memory_space=pl.ANY),
            pl.BlockSpec(memory_space=pl.ANY),
        ],
        out_specs=[pl.BlockSpec((tm, D_MODEL), ctx_tile), pl.BlockSpec((tm, D_MODEL), lat_tile)],
        out_shape=[jax.ShapeDtypeStruct(xp2.shape, F32), jax.ShapeDtypeStruct(xs2.shape, F32)],
        scratch_shapes=[
            pltpu.VMEM((D_MODEL, D_FF), BF16),
            pltpu.VMEM((D_FF, D_MODEL), BF16),
            pltpu.VMEM((2, D_MODEL, D_MODEL), F32),
            pltpu.SemaphoreType.DMA((2,)),
        ],
        compiler_params=_params(1),
        name="mlp",
    )(xp2, xs2, modv, ng, w1, w2)
    return yp.reshape(xp.shape), ys.reshape(xs.shape)


def _log_sigmoid(x):
    return jnp.minimum(x, 0.0) - jnp.log(1.0 + jnp.exp(-jnp.abs(x)))


def _rot_half(x, first_half):
    return jnp.where(first_half, pltpu.roll(x, 96, axis=1), pltpu.roll(x, 32, axis=1))


def _values_and_ones(v):
    vt = jnp.transpose(v)
    row = lax.broadcasted_iota(jnp.int32, vt.shape, 0)
    return jnp.where(row < HEAD_DIM, vt, 1.0)


def _aligned(i, m):
    return i * m if isinstance(i, int) else pl.multiple_of(i * m, m)


def _drain(pieces):
    for _ in pieces:
        pass


def _each(body, n, static, unroll=1):
    if static:
        for i in range(n):
            yield from body(i)
    else:
        lax.fori_loop(0, n, lambda i, c: (_drain(body(i)), c)[1], 0, unroll=unroll)


def _alternate(*programs):
    live = list(programs)
    while live:
        for g in list(live):
            try:
                next(g)
            except StopIteration:
                live.remove(g)
            else:
                yield


def _odd_kernel(*refs, t, lc, latent, nseq):
    shared = set(range(1, 11 if latent else 9))
    _run_staggered((_odd_seq(*[r if k in shared else r.at[sq] for k, r in enumerate(refs)],
                             t=t, lc=lc, latent=latent) for sq in range(nseq)), ODD_STAGGER)


def _odd_seq(*refs, t, lc, latent):
    if latent:
        (x_ref, mod_ref, ng_ref, win_ref, wg_ref, gb_ref, sink_ref, hn_ref, wout_ref,
         cos_ref, sin_ref, kc_ref, vc_ref, cin_ref, nin_ref, min_ref,
         o_ref,
         qa_s, qb_s, kk_s, vvt_s, qma_s, qmb_s, qm_s, km_s, vmt_s, vmtf_s, om_s, g_s, z_s, hft_s, hbt_s,
         sr_s, st_s, qc_s, c_s, n_s, m_s, s_s, p_s, kc_s, vct_s) = refs
    else:
        (x_ref, mod_ref, ng_ref, win_ref, wg_ref, gb_ref, sink_ref, hn_ref, wout_ref,
         o_ref, ko_ref, vo_ref, cf_ref, cb_ref, no_ref, mo_ref,
         qa_s, qb_s, kk_s, vvt_s, qma_s, qmb_s, qm_s, km_s, vmt_s, vmtf_s, om_s, g_s, z_s, hft_s, hbt_s,
         sr_s, st_s, qc_s, c_s, n_s, m_s, s_s, p_s) = refs

    static = not latent
    n_blocks = t // CHUNK
    pad = CHUNK if latent else 0
    shift1, scale1, gate1 = mod_ref[0:1, :], mod_ref[1:2, :], mod_ref[2:3, :]
    lane = lax.broadcasted_iota(jnp.int32, (1, LANES), 1)
    left = lane < HEAD_DIM
    first_half = (lane % HEAD_DIM) < (HEAD_DIM // 2)
    ti = lax.broadcasted_iota(jnp.int32, (CHUNK, CHUNK), 0)
    si = lax.broadcasted_iota(jnp.int32, (CHUNK, CHUNK), 1)
    top = ti < HEAD_DIM
    same_head = top == (si < HEAD_DIM)

    if latent:
        for kv in range(N_KV):
            kk_s[kv, 0:CHUNK, :] = jnp.zeros((CHUNK, LANES), BF16)
            kk_s[kv, CHUNK + t:2 * CHUNK + t, :] = jnp.zeros((CHUNK, LANES), BF16)
            vvt_s[kv, :, 0:CHUNK] = jnp.zeros((LANES, CHUNK), BF16)
            vvt_s[kv, :, CHUNK + t:2 * CHUNK + t] = jnp.zeros((LANES, CHUNK), BF16)
            kc_s[kv] = kc_ref[kv].astype(BF16)
            vct_s[kv] = _values_and_ones(vc_ref[kv]).astype(BF16)

    def project(rc):
        r0 = _aligned(rc, ROW_CHUNK)
        rows = pl.ds(r0, ROW_CHUNK)
        krows = pl.ds(r0 + pad, ROW_CHUNK)
        h = _norm_mod(x_ref[rows, :], ng_ref[0:1, :], shift1, scale1).astype(BF16)
        if latent:
            cs, sn = cos_ref[rows, :], sin_ref[rows, :]
        q = _dot(h, win_ref[:, 0:D_HALF])
        for p in range(N_PAIRS):
            cols = slice(p * LANES, (p + 1) * LANES)
            qp = q[:, cols]
            if latent:
                qp = qp * cs + _rot_half(qp, first_half) * sn
            qp = qp * ATT_SCALE
            qa_s[rows, cols] = jnp.where(left, qp, 0.0).astype(BF16)
            qb_s[rows, cols] = jnp.where(left, 0.0, qp).astype(BF16)
        yield
        kv2 = _dot(h, win_ref[:, D_HALF:D_HALF + 2 * LANES])
        ka, va = kv2[:, 0:LANES], kv2[:, LANES:2 * LANES]
        if latent:
            ka = ka * cs + _rot_half(ka, first_half) * sn
        kr = pltpu.roll(ka, HEAD_DIM, axis=1)
        vr = pltpu.roll(va, HEAD_DIM, axis=1)
        if not latent:
            ka_t, va_t = jnp.transpose(ka), jnp.transpose(va)
            for kv in range(N_KV):
                ko_ref[0, kv, :, rows] = ka_t[kv * HEAD_DIM:(kv + 1) * HEAD_DIM, :]
                vo_ref[0, kv, :, rows] = va_t[kv * HEAD_DIM:(kv + 1) * HEAD_DIM, :]
        kk_s[0, krows, :] = jnp.where(left, ka, kr).astype(BF16)
        kk_s[1, krows, :] = jnp.where(left, kr, ka).astype(BF16)
        vvt_s[0, :, krows] = _values_and_ones(va).astype(BF16)
        vvt_s[1, :, krows] = _values_and_ones(vr).astype(BF16)
        yield
        base = D_HALF + 2 * LANES
        qm = _dot(h, win_ref[:, base:base + D_HALF])
        qm_s[rows, :] = qm.astype(BF16)
        for p in range(N_PAIRS):
            cols = slice(p * LANES, (p + 1) * LANES)
            qma_s[rows, cols] = jnp.where(left, qm[:, cols], 0.0).astype(BF16)
            qmb_s[rows, cols] = jnp.where(left, 0.0, qm[:, cols]).astype(BF16)
        yield
        vm = _dot(h, win_ref[:, base + 2 * D_HALF:base + 3 * D_HALF])
        for p in range(N_PAIRS):
            cols = slice(p * LANES, (p + 1) * LANES)
            vt = jnp.transpose(vm[:, cols])
            vmtf_s[cols, rows] = vt
            vmt_s[cols, rows] = vt.astype(BF16)
        yield
        km = _dot(h, win_ref[:, base + D_HALF:base + 2 * D_HALF]) * (HEAD_DIM ** -0.5)
        km_s[rows, :] = km.astype(BF16)
        yield
        om_s[rows, :] = _dot(h, win_ref[:, base + 3 * D_HALF:base + 4 * D_HALF])
        g_s[rows, :] = _dot(h, wg_ref[...]) + gb_ref[...]
        yield

    yield from _each(project, t // ROW_CHUNK, static)

    kj = lax.broadcasted_iota(jnp.int32, (3 * CHUNK, 2 * CHUNK), 0)
    qi = lax.broadcasted_iota(jnp.int32, (3 * CHUNK, 2 * CHUNK), 1) % CHUNK
    band_ok = jnp.abs(kj - CHUNK - qi) <= WINDOW
    head_a = lax.broadcasted_iota(jnp.int32, (1, 2 * CHUNK), 1) < CHUNK

    n_keys = _att_keys(lc, latent)
    group = _att_group(latent)

    def attention():
        def attend(i):
            r0 = _aligned(i, CHUNK)
            rows = pl.ds(r0, CHUNK)
            if latent:
                key_pos = kj + (i - 1) * CHUNK
                mask = band_ok & (key_pos >= 0) & (key_pos < t)
                win = pl.ds(r0, 3 * CHUNK)
            for g0 in range(0, N_PAIRS, group):
                for gi in range(group):
                    p = g0 + gi
                    kv = p // (N_PAIRS // N_KV)
                    cols = slice(p * LANES, (p + 1) * LANES)
                    q2 = jnp.concatenate([qa_s[rows, cols], qb_s[rows, cols]], axis=0)
                    if latent:
                        s_s[gi, 0:lc, :] = _dot_nt(kc_s[kv], q2)
                        s_s[gi, lc:n_keys, :] = jnp.where(mask, _dot_nt(kk_s[kv, win, :], q2), -jnp.inf)
                    else:
                        s_s[gi] = _dot_nt(kk_s[kv], q2)
                yield
                maxes = []
                for gi in range(group):
                    p = g0 + gi
                    sink = jnp.where(head_a, sink_ref[2 * p:2 * p + 1, 0:1], sink_ref[2 * p + 1:2 * p + 2, 0:1])
                    mx = jnp.maximum(jnp.max(s_s[gi], axis=0, keepdims=True), sink)
                    p_s[gi] = jnp.exp(s_s[gi] - mx).astype(BF16)
                    maxes.append((sink, mx))
                yield
                for gi in range(group):
                    p = g0 + gi
                    kv = p // (N_PAIRS // N_KV)
                    cols = slice(p * LANES, (p + 1) * LANES)
                    if latent:
                        num = (_dot(vct_s[kv], p_s[gi, 0:lc, :]) + _dot(vvt_s[kv, :, win], p_s[gi, lc:n_keys, :]))
                    else:
                        num = _dot(vvt_s[kv], p_s[gi])
                    sink, mx = maxes[gi]
                    den = num[HEAD_DIM:HEAD_DIM + 1, :] + jnp.exp(sink - mx)
                    out = num[0:HEAD_DIM, :] * (1.0 / den)
                    pair = jnp.concatenate([out[:, 0:CHUNK], out[:, CHUNK:2 * CHUNK]], axis=0)
                    z_s[rows, cols] = jnp.transpose(pair).astype(BF16)
                yield

        return attend

    if latent:
        c_s[...] = cin_ref[...]
        n_s[...] = nin_ref[...]
        m_s[...] = min_ref[...]
    else:
        c_s[...] = jnp.zeros(c_s.shape, F32)
        n_s[...] = jnp.zeros(n_s.shape, F32)
        m_s[...] = jnp.zeros(m_s.shape, F32)

    see = (ti <= si, ti >= si)
    tri = tuple(m.astype(F32).astype(BF16) for m in see)
    last = (CHUNK - 1, 0)

    def mlstm(i):
        offs = (_aligned(i, CHUNK), _aligned(n_blocks - 1 - i, CHUNK))
        half = (slice(0, LANES), slice(LANES, 2 * LANES))
        a_rows, b_rows = [], []
        for d in range(2):
            gt = jnp.transpose(g_s[pl.ds(offs[d], CHUNK), :])
            lf = _log_sigmoid(gt[2 * N_HEADS:4 * N_HEADS, :])
            bcum = _dot_exact_lhs(lf, tri[d])[8 * d:8 * d + 8, :]
            a_rows.append(gt[8 * d:8 * d + 8, :] - bcum)
            b_rows.append(bcum)
        yield
        q_ns = []
        for d in range(2):
            rows = pl.ds(offs[d], CHUNK)
            for p in range(N_PAIRS):
                u = d * N_PAIRS + p
                cols = slice(p * LANES, (p + 1) * LANES)
                q2 = jnp.concatenate([qma_s[rows, cols], qmb_s[rows, cols]], axis=0)
                sr_s[u] = _dot_nt(km_s[rows, cols], q2)
                qc_s[u] = _dot_nt(c_s[u].astype(BF16), qm_s[rows, cols])
                n8 = jnp.broadcast_to(n_s[u:u + 1, :], (8, LANES)).astype(BF16)
                q_ns.append(_dot_nt(n8, q2)[0:1, :])
            yield
        stats = {}
        for d in range(2):
            for hd in range(N_HEADS):
                u, j = d * N_PAIRS + hd // 2, hd % 2
                mrow = d * N_HEADS + hd
                a_row = a_rows[d][hd:hd + 1, :]
                b_row = b_rows[d][hd:hd + 1, :]
                m_prev = m_s[mrow:mrow + 1, 0:1]
                a_col = jnp.transpose(jnp.broadcast_to(a_row, (CHUNK, CHUNK)))
                z_t = jnp.where(see[d], a_col, -jnp.inf)
                m_run = jnp.maximum(jnp.max(z_t, axis=0, keepdims=True), m_prev)
                s_t = sr_s[u, :, half[j]] * jnp.exp(z_t - m_run)
                st_s[u, :, half[j]] = s_t.astype(BF16)
                w_int = jnp.exp(m_prev - m_run)
                den = jnp.sum(s_t, axis=0, keepdims=True) + w_int * q_ns[u][:, half[j]]
                inv = 1.0 / jnp.maximum(jnp.abs(den), jnp.exp(-(b_row + m_run)))
                m_last = m_run[:, last[d]:last[d] + 1]
                stats[(u, j)] = (w_int, inv, jnp.exp(a_row - m_last), jnp.exp(m_prev - m_last))
                m_s[mrow:mrow + 1, :] = jnp.broadcast_to(b_row[:, last[d]:last[d] + 1] + m_last, (1, LANES))
            yield
        for d in range(2):
            rows = pl.ds(offs[d], CHUNK)
            ht_s = (hft_s, hbt_s)[d]
            for p in range(N_PAIRS):
                u = d * N_PAIRS + p
                cols = slice(p * LANES, (p + 1) * LANES)
                (w_a, inv_a, e_a, dec_a), (w_b, inv_b, e_b, dec_b) = stats[(u, 0)], stats[(u, 1)]
                kp = km_s[rows, cols]
                num2 = _dot(vmt_s[cols, rows], st_s[u])
                num = jnp.where(top, num2[:, half[0]], num2[:, half[1]])
                ht_s[cols, rows] = (num + jnp.where(top, w_a, w_b) * qc_s[u]) * jnp.where(top, inv_a, inv_b)
                vt_e = (vmtf_s[cols, rows] * jnp.where(top, e_a, e_b)).astype(BF16)
                c_s[u] = jnp.where(top, dec_a, dec_b) * c_s[u] + jnp.where(same_head, _dot(vt_e, kp), 0.0)
                e2 = jnp.concatenate([e_a, e_b, jnp.zeros((6, CHUNK), F32)], axis=0).astype(BF16)
                n_k = _dot(e2, kp)
                n_s[u:u + 1, :] = (jnp.where(left, dec_a, dec_b) * n_s[u:u + 1, :]
                                   + jnp.where(left, n_k[0:1, :], n_k[1:2, :]))
            yield

    attend = attention()
    yield from _each(lambda i: _alternate(attend(i), mlstm(i)), n_blocks, static, unroll=2)

    if not latent:
        for d, c_ref in enumerate((cf_ref, cb_ref)):
            for p in range(N_PAIRS):
                c_pair = jnp.transpose(c_s[d * N_PAIRS + p])
                c_ref[0, 2 * p] = c_pair[0:HEAD_DIM, 0:HEAD_DIM]
                c_ref[0, 2 * p + 1] = c_pair[HEAD_DIM:LANES, HEAD_DIM:LANES]
        no_ref[...] = n_s[...]
        mo_ref[...] = m_s[...]
        yield

    top_w = lax.broadcasted_iota(jnp.int32, (LANES, ROW_CHUNK), 0) < HEAD_DIM

    def finish(rc):
        r0 = _aligned(rc, ROW_CHUNK)
        rows = pl.ds(r0, ROW_CHUNK)
        for p in range(N_PAIRS):
            cols = slice(p * LANES, (p + 1) * LANES)
            hm = hft_s[cols, rows] + hbt_s[cols, rows]
            sq = hm * hm
            ms_a = jnp.sum(sq[0:HEAD_DIM], axis=0, keepdims=True)
            ms_b = jnp.sum(sq[HEAD_DIM:LANES], axis=0, keepdims=True)
            ms = jnp.where(top_w, ms_a, ms_b) * (1.0 / HEAD_DIM)
            y = jnp.transpose(hm * lax.rsqrt(ms + EPS)) * hn_ref[:, cols] * jax.nn.sigmoid(om_s[rows, cols])
            z_s[rows, D_HALF + p * LANES:D_HALF + (p + 1) * LANES] = y.astype(BF16)
            if p % 2:
                yield
        o = _dot(z_s[rows, :], wout_ref[...])
        o_ref[rows, :] = x_ref[rows, :] + gate1 * _rms(o, ng_ref[1:2, :])
        yield

    yield from _each(finish, t // ROW_CHUNK, static)


def _att_keys(lc, latent):
    return lc + 3 * CHUNK if latent else lc


def _att_group(latent):
    return 2 if latent else N_PAIRS


def _odd_scratch(t, lc, latent, nseq):
    pad = 2 * CHUNK if latent else 0
    att = (_att_group(latent), _att_keys(lc, latent), 2 * CHUNK)
    shapes = [
        pltpu.VMEM((t, D_HALF), BF16),
        pltpu.VMEM((t, D_HALF), BF16),
        pltpu.VMEM((N_KV, t + pad, LANES), BF16),
        pltpu.VMEM((N_KV, LANES, t + pad), BF16),
        pltpu.VMEM((t, D_HALF), BF16),
        pltpu.VMEM((t, D_HALF), BF16),
        pltpu.VMEM((t, D_HALF), BF16),
        pltpu.VMEM((t, D_HALF), BF16),
        pltpu.VMEM((D_HALF, t), BF16),
        pltpu.VMEM((D_HALF, t), F32),
        pltpu.VMEM((t, D_HALF), F32),
        pltpu.VMEM((t, LANES), F32),
        pltpu.VMEM((t, D_MODEL), BF16),
        pltpu.VMEM((D_HALF, t), F32),
        pltpu.VMEM((D_HALF, t), F32),
        pltpu.VMEM((2 * N_PAIRS, CHUNK, 2 * LANES), F32),
        pltpu.VMEM((2 * N_PAIRS, CHUNK, 2 * LANES), BF16),
        pltpu.VMEM((2 * N_PAIRS, LANES, CHUNK), F32),
        pltpu.VMEM((2 * N_PAIRS, LANES, LANES), F32),
        pltpu.VMEM((2 * N_PAIRS, LANES), F32),
        pltpu.VMEM((2 * N_HEADS, LANES), F32),
        pltpu.VMEM(att, F32),
        pltpu.VMEM(att, BF16),
    ]
    if latent:
        shapes += [pltpu.VMEM((N_KV, lc, LANES), BF16), pltpu.VMEM((N_KV, LANES, lc), BF16)]
    return [pltpu.VMEM((nseq,) + tuple(sh.shape), sh.dtype) for sh in shapes]


def _odd_common_specs(t, layer, cond_base, cond_stride, nseq):
    assert cond_stride == 0 or nseq == 1
    j = layer // 2
    return [
        _per_seq((t, D_MODEL), nseq),
        _mod_spec(layer, lambda i: cond_base + cond_stride * i),
        _layer_spec((4, D_MODEL), layer),
        _layer_spec((D_MODEL, D_IN_ODD), j),
        _const_spec((D_MODEL, LANES)),
        _const_spec((1, LANES)),
        _const_spec((N_HEADS, LANES)),
        _layer_spec((1, D_HALF), j),
        _layer_spec((D_MODEL, D_MODEL), j),
    ]


def _per_seq(shape, nseq):
    return pl.BlockSpec((nseq,) + shape, lambda i: (i,) + (0,) * len(shape))


def _odd_context(x, modv, layer, ng, w_main, w_gate, gate_bias, sink_b, hnorm, wout):
    b, t, _ = x.shape
    nseq = ODD_CTX_SEQS
    assert b % nseq == 0
    kern = functools.partial(_odd_kernel, t=t, lc=t, latent=False, nseq=nseq)
    per_seq = functools.partial(_per_seq, nseq=nseq)
    return pl.pallas_call(
        kern,
        grid=(b // nseq,),
        in_specs=_odd_common_specs(t, layer, 0, 0, nseq),
        out_specs=[per_seq((t, D_MODEL)),
                   per_seq((1, N_KV, HEAD_DIM, t)), per_seq((1, N_KV, HEAD_DIM, t)),
                   per_seq((1, N_HEADS, HEAD_DIM, HEAD_DIM)), per_seq((1, N_HEADS, HEAD_DIM, HEAD_DIM)),
                   per_seq((2 * N_PAIRS, LANES)), per_seq((2 * N_HEADS, LANES))],
        out_shape=[jax.ShapeDtypeStruct((b, t, D_MODEL), F32),
                   jax.ShapeDtypeStruct((b, 1, N_KV, HEAD_DIM, t), F32),
                   jax.ShapeDtypeStruct((b, 1, N_KV, HEAD_DIM, t), F32),
                   jax.ShapeDtypeStruct((b, 1, N_HEADS, HEAD_DIM, HEAD_DIM), F32),
                   jax.ShapeDtypeStruct((b, 1, N_HEADS, HEAD_DIM, HEAD_DIM), F32),
                   jax.ShapeDtypeStruct((b, 2 * N_PAIRS, LANES), F32),
                   jax.ShapeDtypeStruct((b, 2 * N_HEADS, LANES), F32)],
        scratch_shapes=_odd_scratch(t, t, False, nseq),
        compiler_params=_params(1),
        name="odd_mixer_context",
    )(x, modv, ng, w_main, w_gate, gate_bias, sink_b, hnorm, wout)


def _odd_latent(x, modv, layer, ng, w_main, w_gate, gate_bias, sink_b, hnorm, wout, cos_t, sin_t, kc, vc,
                c_in, n_in, m_in):
    b, t, _ = x.shape
    lc = kc.shape[2]
    kern = functools.partial(_odd_kernel, t=t, lc=lc, latent=True, nseq=1)
    per_seq = functools.partial(_per_seq, nseq=1)
    return pl.pallas_call(
        kern,
        grid=(b,),
        in_specs=_odd_common_specs(t, layer, 1, 1, 1) + [
            _const_spec((t, LANES)), _const_spec((t, LANES)),
            per_seq((N_KV, lc, LANES)), per_seq((N_KV, lc, LANES)),
            per_seq((2 * N_PAIRS, LANES, LANES)), per_seq((2 * N_PAIRS, LANES)), per_seq((2 * N_HEADS, LANES)),
        ],
        out_specs=per_seq((t, D_MODEL)),
        out_shape=jax.ShapeDtypeStruct((b, t, D_MODEL), F32),
        scratch_shapes=_odd_scratch(t, lc, True, 1),
        compiler_params=_params(1),
        name="odd_mixer_latent",
    )(x, modv, ng, w_main, w_gate, gate_bias, sink_b, hnorm, wout, cos_t, sin_t, kc, vc, c_in, n_in, m_in)


def _rope_tables(t):
    rows = t // GRID_W
    row = jnp.broadcast_to(jnp.arange(rows)[:, None], (rows, GRID_W)).reshape(t).astype(F32)
    col = jnp.broadcast_to(jnp.arange(GRID_W)[None, :], (rows, GRID_W)).reshape(t).astype(F32)
    n_freq = HEAD_DIM // 4
    inv_freq = ROPE_BASE ** (-jnp.arange(n_freq, dtype=F32) / n_freq)
    ang = jnp.concatenate([row[:, None] * inv_freq, col[:, None] * inv_freq], axis=-1)
    cos, sin = jnp.cos(ang), jnp.sin(ang)
    cos_l = jnp.tile(cos, (1, LANES // cos.shape[1]))
    sin_l = jnp.tile(jnp.concatenate([-sin, sin], axis=-1), (1, LANES // HEAD_DIM))
    return cos_l, sin_l


def _pair_blockdiag(c):
    b = c.shape[0]
    c = c.reshape(b, N_PAIRS, 2, HEAD_DIM, HEAD_DIM)
    z = jnp.zeros_like(c[:, :, 0])
    top = jnp.concatenate([c[:, :, 0], z], axis=-1)
    bot = jnp.concatenate([z, c[:, :, 1]], axis=-1)
    return jnp.concatenate([top, bot], axis=-2)


def _lane_bcast(v):
    return jnp.broadcast_to(v[..., None], v.shape + (LANES,))


def kernel(x_prompt, x_sample, c, cache_k, cache_v, state_c_fwd, state_n_fwd, state_m_fwd, state_c_bwd, state_n_bwd, state_m_bwd, c_ctx, mod_w, mod_b, norm_g, mlp_w1, mlp_w2, even_in_w, conv_a_w, conv_a_b, ln_a_g, ln_a_b, conv_b_w, even_out_w, odd_in_w, attn_sink, gate_b, hnorm_g, odd_out_w):
    n_dec = x_sample.shape[0]
    n_ctx = x_prompt.shape[0]
    cond = jnp.concatenate([c_ctx[None, :], c, jnp.zeros((COND_ROWS - 1 - n_dec, D_MODEL), F32)], axis=0)
    modv = _modulation(cond, mod_w, mod_b)

    yp, ys = x_prompt, x_sample
    w1, w2 = mlp_w1, mlp_w2

    ev = (norm_g, even_in_w.astype(BF16), conv_a_w, conv_a_b[:, None, :], ln_a_g[:, None, :], ln_a_b[:, None, :],
          conv_b_w, even_out_w.astype(BF16))
    yp = _even_layer(yp, modv, 0, 0, 0, *ev)
    ys = _even_layer(ys, modv, 0, 1, 1, *ev)
    yp, ys = _mlp_layer(yp, ys, modv, 0, norm_g, w1, w2)

    order = jnp.array([0, 2, 1, 3])
    d_main = D_IN_ODD - 4 * N_HEADS
    wg = odd_in_w[0][:, d_main:].reshape(D_MODEL, 4, N_HEADS)[:, order, :].reshape(D_MODEL, 4 * N_HEADS)
    w_gate = jnp.pad(wg, ((0, 0), (0, LANES - 4 * N_HEADS))).astype(BF16)
    gate_bias = jnp.pad(gate_b[0][order, :].reshape(1, 4 * N_HEADS), ((0, 0), (0, LANES - 4 * N_HEADS)))
    sink_b = _lane_bcast(attn_sink[0])
    odd = (1, norm_g, odd_in_w.astype(BF16), w_gate, gate_bias, sink_b, hnorm_g[:, None, :], odd_out_w.astype(BF16))

    op, k_t, v_t, c_f, c_b, n_new, m_new = _odd_context(yp, modv, *odd)
    new_k, new_v = jnp.swapaxes(k_t, -1, -2), jnp.swapaxes(v_t, -1, -2)

    t_dec = x_sample.shape[1]
    cos_t, sin_t = _rope_tables(t_dec)
    kc = jnp.concatenate([cache_k[:, 0], cache_k[:, 0]], axis=-1)
    vc = jnp.concatenate([cache_v[:, 0], cache_v[:, 0]], axis=-1)
    c_in = jnp.concatenate([_pair_blockdiag(jnp.swapaxes(state_c_fwd[:, 0], -1, -2)),
                            _pair_blockdiag(jnp.swapaxes(state_c_bwd[:, 0], -1, -2))], axis=1)
    n_in = jnp.concatenate([state_n_fwd[:, 0].reshape(n_dec, N_PAIRS, LANES),
                            state_n_bwd[:, 0].reshape(n_dec, N_PAIRS, LANES)], axis=1)
    m_in = _lane_bcast(jnp.concatenate([state_m_fwd[:, 0], state_m_bwd[:, 0]], axis=1))
    os_ = _odd_latent(ys, modv, *odd, cos_t, sin_t, kc, vc, c_in, n_in, m_in)

    yp, ys = _mlp_layer(op, os_, modv, 1, norm_g, w1, w2)

    n_f = n_new[:, :N_PAIRS].reshape(n_ctx, N_HEADS, HEAD_DIM)[:, None]
    n_b = n_new[:, N_PAIRS:].reshape(n_ctx, N_HEADS, HEAD_DIM)[:, None]
    m_f = m_new[:, :N_HEADS, 0][:, None]
    m_b = m_new[:, N_HEADS:, 0][:, None]
    return (yp, ys, new_k, new_v, c_f, n_f, m_f, c_b, n_b, m_b)
```

```python
import functools

import jax
import jax.numpy as jnp
from jax import lax
from jax.experimental import pallas as pl
from jax.experimental.pallas import tpu as pltpu

F32 = jnp.float32
BF16 = jnp.bfloat16

D_MODEL = 1024
D_FF = 4 * D_MODEL
EPS = 1e-6
D_HALF = D_MODEL // 2
CONF_WIDTH = 31
CONF_HALO = 16
HEAD_DIM = 64
N_HEADS = 8
N_PAIRS = N_HEADS // 2
N_KV = 2
LANES = 128
CHUNK = 128
WINDOW = 128
GRID_W = 64
ROPE_BASE = 10000.0
ATT_SCALE = HEAD_DIM ** -0.5
D_IN_ODD = D_HALF + 2 * N_KV * HEAD_DIM + 4 * D_HALF + 4 * N_HEADS
ROW_CHUNK = 256
MLP_ROWS = 512
MLP_STAGGER = 3
ODD_CTX_SEQS = 2
ODD_STAGGER = 6
EVEN_CONV_ROWS = 64
EVEN_CHUNKS = 2
EVEN_STAGGER = 3
COND_ROWS = 8
VMEM_LIMIT = 56 * 1024 * 1024


def _dot(a, b):
    return jnp.dot(a, b, preferred_element_type=F32)


def _dot_nt(a, b):
    return lax.dot_general(a, b, (((1,), (1,)), ((), ())), preferred_element_type=F32)


def _split3(x):
    hi = x.astype(BF16)
    r1 = x - hi.astype(F32)
    mid = r1.astype(BF16)
    lo = (r1 - mid.astype(F32)).astype(BF16)
    return hi, mid, lo


def _dot_exact_lhs(x, b01):
    hi, mid, lo = _split3(x)
    return _dot(hi, b01) + _dot(mid, b01) + _dot(lo, b01)


def _rms(x, g):
    return x * lax.rsqrt(jnp.mean(x * x, axis=-1, keepdims=True) + EPS) * g


def _norm_mod(x, g, shift, scale):
    return _rms(x, g) * (1.0 + scale) + shift


def _params(n_grid):
    return pltpu.CompilerParams(dimension_semantics=("arbitrary",) * n_grid, vmem_limit_bytes=VMEM_LIMIT)


def _const_spec(shape):
    zeros = (0,) * len(shape)
    return pl.BlockSpec(shape, lambda *_: zeros, pipeline_mode=pl.Buffered(1))


def _layer_spec(shape, layer):
    index = (layer,) + (0,) * len(shape)
    return pl.BlockSpec((None,) + shape, lambda *_: index, pipeline_mode=pl.Buffered(1))


def _mod_spec(layer, cond_of):
    return pl.BlockSpec((None, None, 6, D_MODEL), lambda *idx: (layer, cond_of(*idx), 0, 0))


def _mod_kernel(cond_ref, w_ref, b_ref, o_ref):
    s = jax.nn.silu(cond_ref[...]).astype(BF16)
    o_ref[...] = _dot(s, w_ref[...].astype(BF16)) + b_ref[...]


def _modulation(cond, mod_w, mod_b):
    depth = mod_w.shape[0]
    n_out = mod_w.shape[2]
    tn = 2 * D_MODEL
    out = pl.pallas_call(
        _mod_kernel,
        grid=(depth, n_out // tn),
        in_specs=[
            pl.BlockSpec((COND_ROWS, D_MODEL), lambda l, j: (0, 0)),
            pl.BlockSpec((None, D_MODEL, tn), lambda l, j: (l, 0, j)),
            pl.BlockSpec((None, 1, tn), lambda l, j: (l, 0, j)),
        ],
        out_specs=pl.BlockSpec((None, COND_ROWS, tn), lambda l, j: (l, 0, j)),
        out_shape=jax.ShapeDtypeStruct((depth, COND_ROWS, n_out), F32),
        compiler_params=_params(2),
        name="modulation",
    )(cond, mod_w, mod_b.reshape(depth, 1, n_out))
    return out.reshape(depth, COND_ROWS, 6, D_MODEL)


def _run_staggered(programs, stagger):
    programs = list(programs)
    live, rounds = [], 0
    while programs or live:
        if programs and rounds % stagger == 0:
            live.append(programs.pop(0))
        for g in list(live):
            try:
                next(g)
            except StopIteration:
                live.remove(g)
        rounds += 1


def _even_kernel(*refs, nsub, n_ctx_steps, ctx_chunks, lat_chunks):
    halos_p, halos_s, (xp_ref, xs_ref, mod_ref, ng_ref, win_ref, caw_ref, cab_ref, lng_ref, lnb_ref, cbw_ref,
                       wout_ref, op_ref, os_ref, apad, cpad, bgs, zs) = (
        refs[:2 * nsub], refs[2 * nsub:4 * nsub], refs[4 * nsub:])
    shared = (mod_ref, ng_ref, win_ref, caw_ref, cab_ref, lng_ref, lnb_ref, cbw_ref, wout_ref)
    i = pl.program_id(0)

    def run(halos, x_ref, o_ref, n_chunks, first_chunk):
        _run_staggered(
            (_even_chunk(first_chunk + k, halos[2 * k], x_ref.at[k], halos[2 * k + 1], *shared,
                         o_ref.at[k], apad.at[k], cpad.at[k], bgs.at[k], zs.at[k], n_chunks=n_chunks)
             for k in range(nsub)), EVEN_STAGGER)

    @pl.when(i < n_ctx_steps)
    def _():
        run(halos_p, xp_ref, op_ref, ctx_chunks, i * nsub)

    @pl.when(i >= n_ctx_steps)
    def _():
        run(halos_s, xs_ref, os_ref, lat_chunks, (i - n_ctx_steps) * nsub)


def _even_chunk(g, xp_ref, xc_ref, xn_ref, mod_ref, ng_ref, win_ref, caw_ref, cab_ref, lng_ref, lnb_ref,
                cbw_ref, wout_ref, o_ref, apad, cpad, bgs, zs, *, n_chunks):
    c = g % n_chunks if n_chunks > 1 else 0
    rows = ROW_CHUNK + 2 * CONF_HALO
    shift1, scale1, gate1 = mod_ref[0:1, :], mod_ref[1:2, :], mod_ref[2:3, :]
    own = slice(CONF_HALO, CONF_HALO + ROW_CHUNK)
    if n_chunks == 1:
        h = _norm_mod(xc_ref[...], ng_ref[0:1, :], shift1, scale1).astype(BF16)
        h_own = h
        for pad_ref in (apad, cpad):
            pad_ref[0:CONF_HALO, :] = jnp.zeros((CONF_HALO, D_HALF), F32)
            pad_ref[CONF_HALO + ROW_CHUNK:rows, :] = jnp.zeros((CONF_HALO, D_HALF), F32)
        keep = lambda v: v
        span = own
    else:
        xh = jnp.concatenate([xp_ref[...], xc_ref[...], xn_ref[...]], axis=0)
        h = _norm_mod(xh, ng_ref[0:1, :], shift1, scale1).astype(BF16)
        h_own = h[own]
        ri = lax.broadcasted_iota(jnp.int32, (rows, D_HALF), 0)
        lo = jnp.where(c == 0, CONF_HALO, 0)
        hi = jnp.where(c == n_chunks - 1, CONF_HALO + ROW_CHUNK, rows)
        inside = (ri >= lo) & (ri < hi)
        keep = lambda v: jnp.where(inside, v, 0.0)
        span = slice(0, rows)
    a = _dot(h, win_ref[:, 0:D_HALF]) * jax.nn.sigmoid(_dot(h, win_ref[:, D_HALF:2 * D_HALF]))
    apad[span, :] = keep(a)
    yield
    cx = _dot(h, win_ref[:, 3 * D_HALF:4 * D_HALF]) * _dot(h, win_ref[:, 4 * D_HALF:5 * D_HALF])
    cpad[span, :] = keep(cx)
    yield
    bgs[...] = _dot(h_own, win_ref[:, 2 * D_HALF:3 * D_HALF])
    yield

    sub = EVEN_CONV_ROWS
    tile = 8
    for j in range(ROW_CHUNK // sub):
        r0 = j * sub
        groups = []
        for cg in range(D_HALF // LANES):
            cols = slice(cg * LANES, (cg + 1) * LANES)
            acc = None
            for r in range(tile):
                part = None
                for m in range(-(-(CONF_WIDTH + 1) // tile)):
                    o = tile * m + r
                    if 1 <= o <= CONF_WIDTH:
                        term = caw_ref[o - 1:o, cols] * apad[r0 + tile * m:r0 + tile * m + sub + tile, cols]
                        part = term if part is None else part + term
                shifted = part[r:r + sub, :]
                acc = shifted if acc is None else acc + shifted
            groups.append(acc)
        acc = jnp.concatenate(groups, axis=1) + cab_ref[...]
        mu = jnp.mean(acc, axis=-1, keepdims=True)
        dlt = acc - mu
        var = jnp.mean(dlt * dlt, axis=-1, keepdims=True)
        a_out = jax.nn.silu(dlt * lax.rsqrt(var + EPS) * lng_ref[...] + lnb_ref[...])
        zs[r0:r0 + sub, 0:D_HALF] = a_out.astype(BF16)
        base = r0 + CONF_HALO - 1
        sc = (cbw_ref[0:1, :] * cpad[base:base + sub, :]
              + cbw_ref[1:2, :] * cpad[base + 1:base + 1 + sub, :]
              + cbw_ref[2:3, :] * cpad[base + 2:base + 2 + sub, :])
        zs[r0:r0 + sub, D_HALF:D_MODEL] = (bgs[r0:r0 + sub, :] * sc).astype(BF16)
        yield

    o = _dot(zs[...], wout_ref[...])
    o_ref[...] = xc_ref[...] + gate1 * _rms(o, ng_ref[1:2, :])
    yield


def _even_layer(xp, xs, modv, layer, ng, win, caw, cab, lng, lnb, cbw, wout):
    j = layer // 2
    nsub = EVEN_CHUNKS
    hpc = ROW_CHUNK // CONF_HALO
    rows = ROW_CHUNK + 2 * CONF_HALO
    ctx_chunks, lat_chunks = xp.shape[1] // ROW_CHUNK, xs.shape[1] // ROW_CHUNK
    groups = []
    for x in (xp, xs):
        n = x.shape[0] * x.shape[1] // ROW_CHUNK
        assert n % nsub == 0
        groups.append((x.reshape(n * hpc, CONF_HALO, D_MODEL), x.reshape(n, ROW_CHUNK, D_MODEL), n // nsub))
    assert lat_chunks % nsub == 0
    n_p = groups[0][2]
    steps = (lambda i: jnp.minimum(i, n_p - 1), lambda i: jnp.maximum(i - n_p, 0))

    halo_specs, halo_args = [], []
    for (x_halo, _, _), step in zip(groups, steps):
        last = x_halo.shape[0] - 1
        for k in range(nsub):
            halo_specs += [
                pl.BlockSpec((None, CONF_HALO, D_MODEL),
                             lambda i, k=k, step=step: (jnp.maximum((step(i) * nsub + k) * hpc - 1, 0), 0, 0)),
                pl.BlockSpec((None, CONF_HALO, D_MODEL),
                             lambda i, k=k, step=step, last=last:
                             (jnp.minimum((step(i) * nsub + k + 1) * hpc, last), 0, 0)),
            ]
            halo_args += [x_halo, x_halo]
    chunk_specs = [pl.BlockSpec((nsub, ROW_CHUNK, D_MODEL), lambda i, step=step: (step(i), 0, 0)) for step in steps]
    kern = functools.partial(_even_kernel, nsub=nsub, n_ctx_steps=n_p, ctx_chunks=ctx_chunks,
                             lat_chunks=lat_chunks)
    yp, ys = pl.pallas_call(
        kern,
        grid=(n_p + groups[1][2],),
        in_specs=halo_specs + chunk_specs + [
            _mod_spec(layer, lambda i: jnp.where(i < n_p, 0, 1 + (steps[1](i) * nsub) // lat_chunks)),
            _layer_spec((4, D_MODEL), layer),
            _layer_spec((D_MODEL, 5 * D_HALF), j),
            _layer_spec((CONF_WIDTH, D_HALF), j),
            _layer_spec((1, D_HALF), j),
            _layer_spec((1, D_HALF), j),
            _layer_spec((1, D_HALF), j),
            _layer_spec((3, D_HALF), j),
            _layer_spec((D_MODEL, D_MODEL), j),
        ],
        out_specs=chunk_specs,
        out_shape=[jax.ShapeDtypeStruct(g[1].shape, F32) for g in groups],
        scratch_shapes=[
            pltpu.VMEM((nsub, rows, D_HALF), F32),
            pltpu.VMEM((nsub, rows, D_HALF), F32),
            pltpu.VMEM((nsub, ROW_CHUNK, D_HALF), F32),
            pltpu.VMEM((nsub, ROW_CHUNK, D_MODEL), BF16),
        ],
        compiler_params=_params(1),
        name="even_mixer",
    )(*halo_args, groups[0][1], groups[1][1], modv, ng, win, caw, cab, lng, lnb, cbw, wout)
    return yp.reshape(xp.shape), ys.reshape(xs.shape)


def _mlp_kernel(xp_ref, xs_ref, mod_ref, ng_ref, w1_hbm, w2_hbm, op_ref, os_ref, w1_s, w2_s, stage, sem, *,
                layer, n_ctx_tiles):
    n_blocks = D_FF // D_MODEL

    def block_copy(k):
        c, slot = k // 2, k % 2
        span = pl.ds(c * D_MODEL, D_MODEL)
        src = w1_hbm.at[layer, :, span] if k % 2 == 0 else w2_hbm.at[layer, span, :]
        return pltpu.make_async_copy(src, stage.at[slot], sem.at[slot])

    def fetch(k, dst):
        block_copy(k).wait()
        dst[...] = stage[k % 2].astype(BF16)
        if k + 2 < 2 * n_blocks:
            block_copy(k + 2).start()

    def rows_program(x_ref, o_ref, rows, load):
        x = x_ref[rows, :]
        h = _norm_mod(x, ng_ref[2:3, :], mod_ref[3:4, :], mod_ref[4:5, :]).astype(BF16)
        yield
        acc = None
        for c in range(n_blocks):
            cols = slice(c * D_MODEL, (c + 1) * D_MODEL)
            if load:
                fetch(2 * c, w1_s.at[:, cols])
            hid = jnp.square(jnp.maximum(_dot(h, w1_s[:, cols]), 0.0)).astype(BF16)
            yield
            if load:
                fetch(2 * c + 1, w2_s.at[cols, :])
            part = _dot(hid, w2_s[cols, :])
            acc = part if acc is None else acc + part
            yield
        o_ref[rows, :] = x + mod_ref[5:6, :] * _rms(acc, ng_ref[3:4, :])
        yield

    def tile(x_ref, o_ref, load=False):
        n_rows = x_ref.shape[0]
        if load:
            _drain(rows_program(x_ref, o_ref, slice(0, n_rows), True))
        else:
            _run_staggered((rows_program(x_ref, o_ref, slice(r0, r0 + n_rows // 2), False)
                            for r0 in (0, n_rows // 2)), MLP_STAGGER)

    i = pl.program_id(0)

    @pl.when(i == 0)
    def _():
        block_copy(0).start()
        block_copy(1).start()
        tile(xp_ref, op_ref, load=True)

    @pl.when((i > 0) & (i < n_ctx_tiles))
    def _():
        tile(xp_ref, op_ref)

    @pl.when(i >= n_ctx_tiles)
    def _():
        tile(xs_ref, os_ref)


def _mlp_layer(xp, xs, modv, layer, ng, w1, w2):
    tm = MLP_ROWS
    xp2, xs2 = xp.reshape(-1, D_MODEL), xs.reshape(-1, D_MODEL)
    t_dec = xs.shape[1]
    assert xp2.shape[0] % tm == 0 and t_dec % tm == 0
    n_p, n_s, per_seq = xp2.shape[0] // tm, xs2.shape[0] // tm, t_dec // tm
    ctx_tile = lambda i: (jnp.minimum(i, n_p - 1), 0)
    lat_tile = lambda i: (jnp.maximum(i - n_p, 0), 0)
    yp, ys = pl.pallas_call(
        functools.partial(_mlp_kernel, layer=layer, n_ctx_tiles=n_p),
        grid=(n_p + n_s,),
        in_specs=[
            pl.BlockSpec((tm, D_MODEL), ctx_tile),
            pl.BlockSpec((tm, D_MODEL), lat_tile),
            _mod_spec(layer, lambda i: jnp.where(i < n_p, 0, 1 + jnp.maximum(i - n_p, 0) // per_seq)),
            _layer_spec((4, D_MODEL), layer),
            pl.BlockSpec(memory_space=pl.ANY),
            pl.BlockSpec(memory_space=pl.ANY),
        ],
        out_specs=[pl.BlockSpec((tm, D_MODEL), ctx_tile), pl.BlockSpec((tm, D_MODEL), lat_tile)],
        out_shape=[jax.ShapeDtypeStruct(xp2.shape, F32), jax.ShapeDtypeStruct(xs2.shape, F32)],
        scratch_shapes=[
            pltpu.VMEM((D_MODEL, D_FF), BF16),
            pltpu.VMEM((D_FF, D_MODEL), BF16),
            pltpu.VMEM((2, D_MODEL, D_MODEL), F32),
            pltpu.SemaphoreType.DMA((2,)),
        ],
        compiler_params=_params(1),
        name="mlp",
    )(xp2, xs2, modv, ng, w1, w2)
    return yp.reshape(xp.shape), ys.reshape(xs.shape)


def _log_sigmoid(x):
    return jnp.minimum(x, 0.0) - jnp.log(1.0 + jnp.exp(-jnp.abs(x)))


def _rot_half(x, first_half):
    return jnp.where(first_half, pltpu.roll(x, 96, axis=1), pltpu.roll(x, 32, axis=1))


def _values_and_ones(v):
    vt = jnp.transpose(v)
    row = lax.broadcasted_iota(jnp.int32, vt.shape, 0)
    return jnp.where(row < HEAD_DIM, vt, 1.0)


def _aligned(i, m):
    return i * m if isinstance(i, int) else pl.multiple_of(i * m, m)


def _drain(pieces):
    for _ in pieces:
        pass


def _each(body, n, static, unroll=1):
    if static:
        for i in range(n):
            yield from body(i)
    else:
        lax.fori_loop(0, n, lambda i, c: (_drain(body(i)), c)[1], 0, unroll=unroll)


def _alternate(*programs):
    live = list(programs)
    while live:
        for g in list(live):
            try:
                next(g)
            except StopIteration:
                live.remove(g)
            else:
                yield


def _odd_kernel(*refs, t, lc, latent, nseq):
    shared = set(range(1, 11 if latent else 9))
    _run_staggered((_odd_seq(*[r if k in shared else r.at[sq] for k, r in enumerate(refs)],
                             t=t, lc=lc, latent=latent) for sq in range(nseq)), ODD_STAGGER)


def _odd_seq(*refs, t, lc, latent):
    if latent:
        (x_ref, mod_ref, ng_ref, win_ref, wg_ref, gb_ref, sink_ref, hn_ref, wout_ref,
         cos_ref, sin_ref, kc_ref, vc_ref, cin_ref, nin_ref, min_ref,
         o_ref,
         qa_s, qb_s, kk_s, vvt_s, qma_s, qmb_s, qm_s, km_s, vmt_s, vmtf_s, om_s, g_s, z_s, hft_s, hbt_s,
         sr_s, st_s, qc_s, c_s, n_s, m_s, s_s, p_s, kc_s, vct_s) = refs
    else:
        (x_ref, mod_ref, ng_ref, win_ref, wg_ref, gb_ref, sink_ref, hn_ref, wout_ref,
         o_ref, ko_ref, vo_ref, cf_ref, cb_ref, no_ref, mo_ref,
         qa_s, qb_s, kk_s, vvt_s, qma_s, qmb_s, qm_s, km_s, vmt_s, vmtf_s, om_s, g_s, z_s, hft_s, hbt_s,
         sr_s, st_s, qc_s, c_s, n_s, m_s, s_s, p_s) = refs

    static = not latent
    n_blocks = t // CHUNK
    pad = CHUNK if latent else 0
    shift1, scale1, gate1 = mod_ref[0:1, :], mod_ref[1:2, :], mod_ref[2:3, :]
    lane = lax.broadcasted_iota(jnp.int32, (1, LANES), 1)
    left = lane < HEAD_DIM
    first_half = (lane % HEAD_DIM) < (HEAD_DIM // 2)
    ti = lax.broadcasted_iota(jnp.int32, (CHUNK, CHUNK), 0)
    si = lax.broadcasted_iota(jnp.int32, (CHUNK, CHUNK), 1)
    top = ti < HEAD_DIM
    same_head = top == (si < HEAD_DIM)

    if latent:
        for kv in range(N_KV):
            kk_s[kv, 0:CHUNK, :] = jnp.zeros((CHUNK, LANES), BF16)
            kk_s[kv, CHUNK + t:2 * CHUNK + t, :] = jnp.zeros((CHUNK, LANES), BF16)
            vvt_s[kv, :, 0:CHUNK] = jnp.zeros((LANES, CHUNK), BF16)
            vvt_s[kv, :, CHUNK + t:2 * CHUNK + t] = jnp.zeros((LANES, CHUNK), BF16)
            kc_s[kv] = kc_ref[kv].astype(BF16)
            vct_s[kv] = _values_and_ones(vc_ref[kv]).astype(BF16)

    def project(rc):
        r0 = _aligned(rc, ROW_CHUNK)
        rows = pl.ds(r0, ROW_CHUNK)
        krows = pl.ds(r0 + pad, ROW_CHUNK)
        h = _norm_mod(x_ref[rows, :], ng_ref[0:1, :], shift1, scale1).astype(BF16)
        if latent:
            cs, sn = cos_ref[rows, :], sin_ref[rows, :]
        q = _dot(h, win_ref[:, 0:D_HALF])
        for p in range(N_PAIRS):
            cols = slice(p * LANES, (p + 1) * LANES)
            qp = q[:, cols]
            if latent:
                qp = qp * cs + _rot_half(qp, first_half) * sn
            qp = qp * ATT_SCALE
            qa_s[rows, cols] = jnp.where(left, qp, 0.0).astype(BF16)
            qb_s[rows, cols] = jnp.where(left, 0.0, qp).astype(BF16)
        yield
        kv2 = _dot(h, win_ref[:, D_HALF:D_HALF + 2 * LANES])
        ka, va = kv2[:, 0:LANES], kv2[:, LANES:2 * LANES]
        if latent:
            ka = ka * cs + _rot_half(ka, first_half) * sn
        kr = pltpu.roll(ka, HEAD_DIM, axis=1)
        vr = pltpu.roll(va, HEAD_DIM, axis=1)
        if not latent:
            ka_t, va_t = jnp.transpose(ka), jnp.transpose(va)
            for kv in range(N_KV):
                ko_ref[0, kv, :, rows] = ka_t[kv * HEAD_DIM:(kv + 1) * HEAD_DIM, :]
                vo_ref[0, kv, :, rows] = va_t[kv * HEAD_DIM:(kv + 1) * HEAD_DIM, :]
        kk_s[0, krows, :] = jnp.where(left, ka, kr).astype(BF16)
        kk_s[1, krows, :] = jnp.where(left, kr, ka).astype(BF16)
        vvt_s[0, :, krows] = _values_and_ones(va).astype(BF16)
        vvt_s[1, :, krows] = _values_and_ones(vr).astype(BF16)
        yield
        base = D_HALF + 2 * LANES
        qm = _dot(h, win_ref[:, base:base + D_HALF])
        qm_s[rows, :] = qm.astype(BF16)
        for p in range(N_PAIRS):
            cols = slice(p * LANES, (p + 1) * LANES)
            qma_s[rows, cols] = jnp.where(left, qm[:, cols], 0.0).astype(BF16)
            qmb_s[rows, cols] = jnp.where(left, 0.0, qm[:, cols]).astype(BF16)
        yield
        vm = _dot(h, win_ref[:, base + 2 * D_HALF:base + 3 * D_HALF])
        for p in range(N_PAIRS):
            cols = slice(p * LANES, (p + 1) * LANES)
            vt = jnp.transpose(vm[:, cols])
            vmtf_s[cols, rows] = vt
            vmt_s[cols, rows] = vt.astype(BF16)
        yield
        km = _dot(h, win_ref[:, base + D_HALF:base + 2 * D_HALF]) * (HEAD_DIM ** -0.5)
        km_s[rows, :] = km.astype(BF16)
        yield
        om_s[rows, :] = _dot(h, win_ref[:, base + 3 * D_HALF:base + 4 * D_HALF])
        g_s[rows, :] = _dot(h, wg_ref[...]) + gb_ref[...]
        yield

    yield from _each(project, t // ROW_CHUNK, static)

    kj = lax.broadcasted_iota(jnp.int32, (3 * CHUNK, 2 * CHUNK), 0)
    qi = lax.broadcasted_iota(jnp.int32, (3 * CHUNK, 2 * CHUNK), 1) % CHUNK
    band_ok = jnp.abs(kj - CHUNK - qi) <= WINDOW
    head_a = lax.broadcasted_iota(jnp.int32, (1, 2 * CHUNK), 1) < CHUNK

    n_keys = _att_keys(lc, latent)
    group = _att_group(latent)

    def attention():
        def attend(i):
            r0 = _aligned(i, CHUNK)
            rows = pl.ds(r0, CHUNK)
            if latent:
                key_pos = kj + (i - 1) * CHUNK
                mask = band_ok & (key_pos >= 0) & (key_pos < t)
                win = pl.ds(r0, 3 * CHUNK)
            for g0 in range(0, N_PAIRS, group):
                for gi in range(group):
                    p = g0 + gi
                    kv = p // (N_PAIRS // N_KV)
                    cols = slice(p * LANES, (p + 1) * LANES)
                    q2 = jnp.concatenate([qa_s[rows, cols], qb_s[rows, cols]], axis=0)
                    if latent:
                        s_s[gi, 0:lc, :] = _dot_nt(kc_s[kv], q2)
                        s_s[gi, lc:n_keys, :] = jnp.where(mask, _dot_nt(kk_s[kv, win, :], q2), -jnp.inf)
                    else:
                        s_s[gi] = _dot_nt(kk_s[kv], q2)
                yield
                maxes = []
                for gi in range(group):
                    p = g0 + gi
                    sink = jnp.where(head_a, sink_ref[2 * p:2 * p + 1, 0:1], sink_ref[2 * p + 1:2 * p + 2, 0:1])
                    mx = jnp.maximum(jnp.max(s_s[gi], axis=0, keepdims=True), sink)
                    p_s[gi] = jnp.exp(s_s[gi] - mx).astype(BF16)
                    maxes.append((sink, mx))
                yield
                for gi in range(group):
                    p = g0 + gi
                    kv = p // (N_PAIRS // N_KV)
                    cols = slice(p * LANES, (p + 1) * LANES)
                    if latent:
                        num = (_dot(vct_s[kv], p_s[gi, 0:lc, :]) + _dot(vvt_s[kv, :, win], p_s[gi, lc:n_keys, :]))
                    else:
                        num = _dot(vvt_s[kv], p_s[gi])
                    sink, mx = maxes[gi]
                    den = num[HEAD_DIM:HEAD_DIM + 1, :] + jnp.exp(sink - mx)
                    out = num[0:HEAD_DIM, :] * (1.0 / den)
                    pair = jnp.concatenate([out[:, 0:CHUNK], out[:, CHUNK:2 * CHUNK]], axis=0)
                    z_s[rows, cols] = jnp.transpose(pair).astype(BF16)
                yield

        return attend

    if latent:
        c_s[...] = cin_ref[...]
        n_s[...] = nin_ref[...]
        m_s[...] = min_ref[...]
    else:
        c_s[...] = jnp.zeros(c_s.shape, F32)
        n_s[...] = jnp.zeros(n_s.shape, F32)
        m_s[...] = jnp.zeros(m_s.shape, F32)

    see = (ti <= si, ti >= si)
    tri = tuple(m.astype(F32).astype(BF16) for m in see)
    last = (CHUNK - 1, 0)

    def mlstm(i):
        offs = (_aligned(i, CHUNK), _aligned(n_blocks - 1 - i, CHUNK))
        half = (slice(0, LANES), slice(LANES, 2 * LANES))
        a_rows, b_rows = [], []
        for d in range(2):
            gt = jnp.transpose(g_s[pl.ds(offs[d], CHUNK), :])
            lf = _log_sigmoid(gt[2 * N_HEADS:4 * N_HEADS, :])
            bcum = _dot_exact_lhs(lf, tri[d])[8 * d:8 * d + 8, :]
            a_rows.append(gt[8 * d:8 * d + 8, :] - bcum)
            b_rows.append(bcum)
        yield
        q_ns = []
        for d in range(2):
            rows = pl.ds(offs[d], CHUNK)
            for p in range(N_PAIRS):
                u = d * N_PAIRS + p
                cols = slice(p * LANES, (p + 1) * LANES)
                q2 = jnp.concatenate([qma_s[rows, cols], qmb_s[rows, cols]], axis=0)
                sr_s[u] = _dot_nt(km_s[rows, cols], q2)
                qc_s[u] = _dot_nt(c_s[u].astype(BF16), qm_s[rows, cols])
                n8 = jnp.broadcast_to(n_s[u:u + 1, :], (8, LANES)).astype(BF16)
                q_ns.append(_dot_nt(n8, q2)[0:1, :])
            yield
        stats = {}
        for d in range(2):
            for hd in range(N_HEADS):
                u, j = d * N_PAIRS + hd // 2, hd % 2
                mrow = d * N_HEADS + hd
                a_row = a_rows[d][hd:hd + 1, :]
                b_row = b_rows[d][hd:hd + 1, :]
                m_prev = m_s[mrow:mrow + 1, 0:1]
                a_col = jnp.transpose(jnp.broadcast_to(a_row, (CHUNK, CHUNK)))
                z_t = jnp.where(see[d], a_col, -jnp.inf)
                m_run = jnp.maximum(jnp.max(z_t, axis=0, keepdims=True), m_prev)
                s_t = sr_s[u, :, half[j]] * jnp.exp(z_t - m_run)
                st_s[u, :, half[j]] = s_t.astype(BF16)
                w_int = jnp.exp(m_prev - m_run)
                den = jnp.sum(s_t, axis=0, keepdims=True) + w_int * q_ns[u][:, half[j]]
                inv = 1.0 / jnp.maximum(jnp.abs(den), jnp.exp(-(b_row + m_run)))
                m_last = m_run[:, last[d]:last[d] + 1]
                stats[(u, j)] = (w_int, inv, jnp.exp(a_row - m_last), jnp.exp(m_prev - m_last))
                m_s[mrow:mrow + 1, :] = jnp.broadcast_to(b_row[:, last[d]:last[d] + 1] + m_last, (1, LANES))
            yield
        for d in range(2):
            rows = pl.ds(offs[d], CHUNK)
            ht_s = (hft_s, hbt_s)[d]
            for p in range(N_PAIRS):
                u = d * N_PAIRS + p
                cols = slice(p * LANES, (p + 1) * LANES)
                (w_a, inv_a, e_a, dec_a), (w_b, inv_b, e_b, dec_b) = stats[(u, 0)], stats[(u, 1)]
                kp = km_s[rows, cols]
                num2 = _dot(vmt_s[cols, rows], st_s[u])
                num = jnp.where(top, num2[:, half[0]], num2[:, half[1]])
                ht_s[cols, rows] = (num + jnp.where(top, w_a, w_b) * qc_s[u]) * jnp.where(top, inv_a, inv_b)
                vt_e = (vmtf_s[cols, rows] * jnp.where(top, e_a, e_b)).astype(BF16)
                c_s[u] = jnp.where(top, dec_a, dec_b) * c_s[u] + jnp.where(same_head, _dot(vt_e, kp), 0.0)
                e2 = jnp.concatenate([e_a, e_b, jnp.zeros((6, CHUNK), F32)], axis=0).astype(BF16)
                n_k = _dot(e2, kp)
                n_s[u:u + 1, :] = (jnp.where(left, dec_a, dec_b) * n_s[u:u + 1, :]
                                   + jnp.where(left, n_k[0:1, :], n_k[1:2, :]))
            yield

    attend = attention()
    yield from _each(lambda i: _alternate(attend(i), mlstm(i)), n_blocks, static, unroll=2)

    if not latent:
        for d, c_ref in enumerate((cf_ref, cb_ref)):
            for p in range(N_PAIRS):
                c_pair = jnp.transpose(c_s[d * N_PAIRS + p])
                c_ref[0, 2 * p] = c_pair[0:HEAD_DIM, 0:HEAD_DIM]
                c_ref[0, 2 * p + 1] = c_pair[HEAD_DIM:LANES, HEAD_DIM:LANES]
        no_ref[...] = n_s[...]
        mo_ref[...] = m_s[...]
        yield

    top_w = lax.broadcasted_iota(jnp.int32, (LANES, ROW_CHUNK), 0) < HEAD_DIM

    def finish(rc):
        r0 = _aligned(rc, ROW_CHUNK)
        rows = pl.ds(r0, ROW_CHUNK)
        for p in range(N_PAIRS):
            cols = slice(p * LANES, (p + 1) * LANES)
            hm = hft_s[cols, rows] + hbt_s[cols, rows]
            sq = hm * hm
            ms_a = jnp.sum(sq[0:HEAD_DIM], axis=0, keepdims=True)
            ms_b = jnp.sum(sq[HEAD_DIM:LANES], axis=0, keepdims=True)
            ms = jnp.where(top_w, ms_a, ms_b) * (1.0 / HEAD_DIM)
            y = jnp.transpose(hm * lax.rsqrt(ms + EPS)) * hn_ref[:, cols] * jax.nn.sigmoid(om_s[rows, cols])
            z_s[rows, D_HALF + p * LANES:D_HALF + (p + 1) * LANES] = y.astype(BF16)
            if p % 2:
                yield
        o = _dot(z_s[rows, :], wout_ref[...])
        o_ref[rows, :] = x_ref[rows, :] + gate1 * _rms(o, ng_ref[1:2, :])
        yield

    yield from _each(finish, t // ROW_CHUNK, static)


def _att_keys(lc, latent):
    return lc + 3 * CHUNK if latent else lc


def _att_group(latent):
    return 2 if latent else N_PAIRS


def _odd_scratch(t, lc, latent, nseq):
    pad = 2 * CHUNK if latent else 0
    att = (_att_group(latent), _att_keys(lc, latent), 2 * CHUNK)
    shapes = [
        pltpu.VMEM((t, D_HALF), BF16),
        pltpu.VMEM((t, D_HALF), BF16),
        pltpu.VMEM((N_KV, t + pad, LANES), BF16),
        pltpu.VMEM((N_KV, LANES, t + pad), BF16),
        pltpu.VMEM((t, D_HALF), BF16),
        pltpu.VMEM((t, D_HALF), BF16),
        pltpu.VMEM((t, D_HALF), BF16),
        pltpu.VMEM((t, D_HALF), BF16),
        pltpu.VMEM((D_HALF, t), BF16),
        pltpu.VMEM((D_HALF, t), F32),
        pltpu.VMEM((t, D_HALF), F32),
        pltpu.VMEM((t, LANES), F32),
        pltpu.VMEM((t, D_MODEL), BF16),
        pltpu.VMEM((D_HALF, t), F32),
        pltpu.VMEM((D_HALF, t), F32),
        pltpu.VMEM((2 * N_PAIRS, CHUNK, 2 * LANES), F32),
        pltpu.VMEM((2 * N_PAIRS, CHUNK, 2 * LANES), BF16),
        pltpu.VMEM((2 * N_PAIRS, LANES, CHUNK), F32),
        pltpu.VMEM((2 * N_PAIRS, LANES, LANES), F32),
        pltpu.VMEM((2 * N_PAIRS, LANES), F32),
        pltpu.VMEM((2 * N_HEADS, LANES), F32),
        pltpu.VMEM(att, F32),
        pltpu.VMEM(att, BF16),
    ]
    if latent:
        shapes += [pltpu.VMEM((N_KV, lc, LANES), BF16), pltpu.VMEM((N_KV, LANES, lc), BF16)]
    return [pltpu.VMEM((nseq,) + tuple(sh.shape), sh.dtype) for sh in shapes]


def _odd_common_specs(t, layer, cond_base, cond_stride, nseq):
    assert cond_stride == 0 or nseq == 1
    j = layer // 2
    return [
        _per_seq((t, D_MODEL), nseq),
        _mod_spec(layer, lambda i: cond_base + cond_stride * i),
        _layer_spec((4, D_MODEL), layer),
        _layer_spec((D_MODEL, D_IN_ODD), j),
        _const_spec((D_MODEL, LANES)),
        _const_spec((1, LANES)),
        _const_spec((N_HEADS, LANES)),
        _layer_spec((1, D_HALF), j),
        _layer_spec((D_MODEL, D_MODEL), j),
    ]


def _per_seq(shape, nseq):
    return pl.BlockSpec((nseq,) + shape, lambda i: (i,) + (0,) * len(shape))


def _odd_context(x, modv, layer, ng, w_main, w_gate, gate_bias, sink_b, hnorm, wout):
    b, t, _ = x.shape
    nseq = ODD_CTX_SEQS
    assert b % nseq == 0
    kern = functools.partial(_odd_kernel, t=t, lc=t, latent=False, nseq=nseq)
    per_seq = functools.partial(_per_seq, nseq=nseq)
    return pl.pallas_call(
        kern,
        grid=(b // nseq,),
        in_specs=_odd_common_specs(t, layer, 0, 0, nseq),
        out_specs=[per_seq((t, D_MODEL)),
                   per_seq((1, N_KV, HEAD_DIM, t)), per_seq((1, N_KV, HEAD_DIM, t)),
                   per_seq((1, N_HEADS, HEAD_DIM, HEAD_DIM)), per_seq((1, N_HEADS, HEAD_DIM, HEAD_DIM)),
                   per_seq((2 * N_PAIRS, LANES)), per_seq((2 * N_HEADS, LANES))],
        out_shape=[jax.ShapeDtypeStruct((b, t, D_MODEL), F32),
                   jax.ShapeDtypeStruct((b, 1, N_KV, HEAD_DIM, t), F32),
                   jax.ShapeDtypeStruct((b, 1, N_KV, HEAD_DIM, t), F32),
                   jax.ShapeDtypeStruct((b, 1, N_HEADS, HEAD_DIM, HEAD_DIM), F32),
                   jax.ShapeDtypeStruct((b, 1, N_HEADS, HEAD_DIM, HEAD_DIM), F32),
                   jax.ShapeDtypeStruct((b, 2 * N_PAIRS, LANES), F32),
                   jax.ShapeDtypeStruct((b, 2 * N_HEADS, LANES), F32)],
        scratch_shapes=_odd_scratch(t, t, False, nseq),
        compiler_params=_params(1),
        name="odd_mixer_context",
    )(x, modv, ng, w_main, w_gate, gate_bias, sink_b, hnorm, wout)


def _odd_latent(x, modv, layer, ng, w_main, w_gate, gate_bias, sink_b, hnorm, wout, cos_t, sin_t, kc, vc,
                c_in, n_in, m_in):
    b, t, _ = x.shape
    lc = kc.shape[2]
    kern = functools.partial(_odd_kernel, t=t, lc=lc, latent=True, nseq=1)
    per_seq = functools.partial(_per_seq, nseq=1)
    return pl.pallas_call(
        kern,
        grid=(b,),
        in_specs=_odd_common_specs(t, layer, 1, 1, 1) + [
            _const_spec((t, LANES)), _const_spec((t, LANES)),
            per_seq((N_KV, lc, LANES)), per_seq((N_KV, lc, LANES)),
            per_seq((2 * N_PAIRS, LANES, LANES)), per_seq((2 * N_PAIRS, LANES)), per_seq((2 * N_HEADS, LANES)),
        ],
        out_specs=per_seq((t, D_MODEL)),
        out_shape=jax.ShapeDtypeStruct((b, t, D_MODEL), F32),
        scratch_shapes=_odd_scratch(t, lc, True, 1),
        compiler_params=_params(1),
        name="odd_mixer_latent",
    )(x, modv, ng, w_main, w_gate, gate_bias, sink_b, hnorm, wout, cos_t, sin_t, kc, vc, c_in, n_in, m_in)


def _rope_tables(t):
    rows = t // GRID_W
    row = jnp.broadcast_to(jnp.arange(rows)[:, None], (rows, GRID_W)).reshape(t).astype(F32)
    col = jnp.broadcast_to(jnp.arange(GRID_W)[None, :], (rows, GRID_W)).reshape(t).astype(F32)
    n_freq = HEAD_DIM // 4
    inv_freq = ROPE_BASE ** (-jnp.arange(n_freq, dtype=F32) / n_freq)
    ang = jnp.concatenate([row[:, None] * inv_freq, col[:, None] * inv_freq], axis=-1)
    cos, sin = jnp.cos(ang), jnp.sin(ang)
    cos_l = jnp.tile(cos, (1, LANES // cos.shape[1]))
    sin_l = jnp.tile(jnp.concatenate([-sin, sin], axis=-1), (1, LANES // HEAD_DIM))
    return cos_l, sin_l


def _pair_blockdiag(c):
    b = c.shape[0]
    c = c.reshape(b, N_PAIRS, 2, HEAD_DIM, HEAD_DIM)
    z = jnp.zeros_like(c[:, :, 0])
    top = jnp.concatenate([c[:, :, 0], z], axis=-1)
    bot = jnp.concatenate([z, c[:, :, 1]], axis=-1)
    return jnp.concatenate([top, bot], axis=-2)


def _lane_bcast(v):
    return jnp.broadcast_to(v[..., None], v.shape + (LANES,))


def kernel(x_prompt, x_sample, c, cache_k, cache_v, state_c_fwd, state_n_fwd, state_m_fwd, state_c_bwd, state_n_bwd, state_m_bwd, c_ctx, mod_w, mod_b, norm_g, mlp_w1, mlp_w2, even_in_w, conv_a_w, conv_a_b, ln_a_g, ln_a_b, conv_b_w, even_out_w, odd_in_w, attn_sink, gate_b, hnorm_g, odd_out_w):
    n_dec = x_sample.shape[0]
    n_ctx = x_prompt.shape[0]
    cond = jnp.concatenate([c_ctx[None, :], c, jnp.zeros((COND_ROWS - 1 - n_dec, D_MODEL), F32)], axis=0)
    modv = _modulation(cond, mod_w, mod_b)

    yp, ys = x_prompt, x_sample
    w1, w2 = mlp_w1, mlp_w2

    ev = (norm_g, even_in_w.astype(BF16), conv_a_w, conv_a_b[:, None, :], ln_a_g[:, None, :], ln_a_b[:, None, :],
          conv_b_w, even_out_w.astype(BF16))
    yp, ys = _even_layer(yp, ys, modv, 0, *ev)
    yp, ys = _mlp_layer(yp, ys, modv, 0, norm_g, w1, w2)

    order = jnp.array([0, 2, 1, 3])
    d_main = D_IN_ODD - 4 * N_HEADS
    wg = odd_in_w[0][:, d_main:].reshape(D_MODEL, 4, N_HEADS)[:, order, :].reshape(D_MODEL, 4 * N_HEADS)
    w_gate = jnp.pad(wg, ((0, 0), (0, LANES - 4 * N_HEADS))).astype(BF16)
    gate_bias = jnp.pad(gate_b[0][order, :].reshape(1, 4 * N_HEADS), ((0, 0), (0, LANES - 4 * N_HEADS)))
    sink_b = _lane_bcast(attn_sink[0])
    odd = (1, norm_g, odd_in_w.astype(BF16), w_gate, gate_bias, sink_b, hnorm_g[:, None, :], odd_out_w.astype(BF16))

    op, k_t, v_t, c_f, c_b, n_new, m_new = _odd_context(yp, modv, *odd)
    new_k, new_v = jnp.swapaxes(k_t, -1, -2), jnp.swapaxes(v_t, -1, -2)

    t_dec = x_sample.shape[1]
    cos_t, sin_t = _rope_tables(t_dec)
    kc = jnp.concatenate([cache_k[:, 0], cache_k[:, 0]], axis=-1)
    vc = jnp.concatenate([cache_v[:, 0], cache_v[:, 0]], axis=-1)
    c_in = jnp.concatenate([_pair_blockdiag(jnp.swapaxes(state_c_fwd[:, 0], -1, -2)),
                            _pair_blockdiag(jnp.swapaxes(state_c_bwd[:, 0], -1, -2))], axis=1)
    n_in = jnp.concatenate([state_n_fwd[:, 0].reshape(n_dec, N_PAIRS, LANES),
                            state_n_bwd[:, 0].reshape(n_dec, N_PAIRS, LANES)], axis=1)
    m_in = _lane_bcast(jnp.concatenate([state_m_fwd[:, 0], state_m_bwd[:, 0]], axis=1))
    os_ = _odd_latent(ys, modv, *odd, cos_t, sin_t, kc, vc, c_in, n_in, m_in)

    yp, ys = _mlp_layer(op, os_, modv, 1, norm_g, w1, w2)

    n_f = n_new[:, :N_PAIRS].reshape(n_ctx, N_HEADS, HEAD_DIM)[:, None]
    n_b = n_new[:, N_PAIRS:].reshape(n_ctx, N_HEADS, HEAD_DIM)[:, None]
    m_f = m_new[:, :N_HEADS, 0][:, None]
    m_b = m_new[:, N_HEADS:, 0][:, None]
    return (yp, ys, new_k, new_v, c_f, n_f, m_f, c_b, n_b, m_b)
```

```python
import functools

import jax
import jax.numpy as jnp
from jax import lax
from jax.experimental import pallas as pl
from jax.experimental.pallas import tpu as pltpu

F32 = jnp.float32
BF16 = jnp.bfloat16

D_MODEL = 1024
D_FF = 4 * D_MODEL
EPS = 1e-6
D_HALF = D_MODEL // 2
CONF_WIDTH = 31
CONF_HALO = 16
HEAD_DIM = 64
N_HEADS = 8
N_PAIRS = N_HEADS // 2
N_KV = 2
LANES = 128
CHUNK = 128
WINDOW = 128
GRID_W = 64
ROPE_BASE = 10000.0
ATT_SCALE = HEAD_DIM ** -0.5
D_IN_ODD = D_HALF + 2 * N_KV * HEAD_DIM + 4 * D_HALF + 4 * N_HEADS
ROW_CHUNK = 256
MLP_ROWS = 512
MLP_STAGGER = 3
ODD_CTX_SEQS = 2
ODD_STAGGER = 6
EVEN_CONV_ROWS = 64
EVEN_CHUNKS = 2
EVEN_STAGGER = 3
COND_ROWS = 8
VMEM_LIMIT = 56 * 1024 * 1024


def _dot(a, b):
    return jnp.dot(a, b, preferred_element_type=F32)


def _dot_nt(a, b):
    return lax.dot_general(a, b, (((1,), (1,)), ((), ())), preferred_element_type=F32)


def _split3(x):
    hi = x.astype(BF16)
    r1 = x - hi.astype(F32)
    mid = r1.astype(BF16)
    lo = (r1 - mid.astype(F32)).astype(BF16)
    return hi, mid, lo


def _dot_exact_lhs(x, b01):
    hi, mid, lo = _split3(x)
    return _dot(hi, b01) + _dot(mid, b01) + _dot(lo, b01)


def _rms(x, g):
    return x * lax.rsqrt(jnp.mean(x * x, axis=-1, keepdims=True) + EPS) * g


def _norm_mod(x, g, shift, scale):
    return _rms(x, g) * (1.0 + scale) + shift


def _params(n_grid):
    return pltpu.CompilerParams(dimension_semantics=("arbitrary",) * n_grid, vmem_limit_bytes=VMEM_LIMIT)


def _const_spec(shape):
    zeros = (0,) * len(shape)
    return pl.BlockSpec(shape, lambda *_: zeros, pipeline_mode=pl.Buffered(1))


def _layer_spec(shape, layer):
    index = (layer,) + (0,) * len(shape)
    return pl.BlockSpec((None,) + shape, lambda *_: index, pipeline_mode=pl.Buffered(1))


def _mod_spec(layer, cond_of):
    return pl.BlockSpec((None, None, 6, D_MODEL), lambda *idx: (layer, cond_of(*idx), 0, 0))


def _mod_kernel(cond_ref, w_ref, b_ref, o_ref):
    s = jax.nn.silu(cond_ref[...]).astype(BF16)
    o_ref[...] = _dot(s, w_ref[...].astype(BF16)) + b_ref[...]


def _modulation(cond, mod_w, mod_b):
    depth = mod_w.shape[0]
    n_out = mod_w.shape[2]
    tn = 2 * D_MODEL
    out = pl.pallas_call(
        _mod_kernel,
        grid=(depth, n_out // tn),
        in_specs=[
            pl.BlockSpec((COND_ROWS, D_MODEL), lambda l, j: (0, 0)),
            pl.BlockSpec((None, D_MODEL, tn), lambda l, j: (l, 0, j)),
            pl.BlockSpec((None, 1, tn), lambda l, j: (l, 0, j)),
        ],
        out_specs=pl.BlockSpec((None, COND_ROWS, tn), lambda l, j: (l, 0, j)),
        out_shape=jax.ShapeDtypeStruct((depth, COND_ROWS, n_out), F32),
        compiler_params=_params(2),
        name="modulation",
    )(cond, mod_w, mod_b.reshape(depth, 1, n_out))
    return out.reshape(depth, COND_ROWS, 6, D_MODEL)


def _run_staggered(programs, stagger):
    programs = list(programs)
    live, rounds = [], 0
    while programs or live:
        if programs and rounds % stagger == 0:
            live.append(programs.pop(0))
        for g in list(live):
            try:
                next(g)
            except StopIteration:
                live.remove(g)
        rounds += 1


def _even_kernel(*refs, n_chunks, nsub):
    halos, (xc_ref, mod_ref, ng_ref, win_ref, caw_ref, cab_ref, lng_ref, lnb_ref, cbw_ref, wout_ref, o_ref,
            apad, cpad, bgs, zs) = refs[:2 * nsub], refs[2 * nsub:]
    shared = (mod_ref, ng_ref, win_ref, caw_ref, cab_ref, lng_ref, lnb_ref, cbw_ref, wout_ref)
    _run_staggered(
        (_even_chunk(pl.program_id(0) * nsub + k, halos[2 * k], xc_ref.at[k], halos[2 * k + 1], *shared,
                     o_ref.at[k], apad.at[k], cpad.at[k], bgs.at[k], zs.at[k], n_chunks=n_chunks)
         for k in range(nsub)), EVEN_STAGGER)


def _even_chunk(g, xp_ref, xc_ref, xn_ref, mod_ref, ng_ref, win_ref, caw_ref, cab_ref, lng_ref, lnb_ref,
                cbw_ref, wout_ref, o_ref, apad, cpad, bgs, zs, *, n_chunks):
    c = g % n_chunks if n_chunks > 1 else 0
    rows = ROW_CHUNK + 2 * CONF_HALO
    shift1, scale1, gate1 = mod_ref[0:1, :], mod_ref[1:2, :], mod_ref[2:3, :]
    own = slice(CONF_HALO, CONF_HALO + ROW_CHUNK)
    if n_chunks == 1:
        h = _norm_mod(xc_ref[...], ng_ref[0:1, :], shift1, scale1).astype(BF16)
        h_own = h
        for pad_ref in (apad, cpad):
            pad_ref[0:CONF_HALO, :] = jnp.zeros((CONF_HALO, D_HALF), F32)
            pad_ref[CONF_HALO + ROW_CHUNK:rows, :] = jnp.zeros((CONF_HALO, D_HALF), F32)
        keep = lambda v: v
        span = own
    else:
        xh = jnp.concatenate([xp_ref[...], xc_ref[...], xn_ref[...]], axis=0)
        h = _norm_mod(xh, ng_ref[0:1, :], shift1, scale1).astype(BF16)
        h_own = h[own]
        ri = lax.broadcasted_iota(jnp.int32, (rows, D_HALF), 0)
        lo = jnp.where(c == 0, CONF_HALO, 0)
        hi = jnp.where(c == n_chunks - 1, CONF_HALO + ROW_CHUNK, rows)
        inside = (ri >= lo) & (ri < hi)
        keep = lambda v: jnp.where(inside, v, 0.0)
        span = slice(0, rows)
    a = _dot(h, win_ref[:, 0:D_HALF]) * jax.nn.sigmoid(_dot(h, win_ref[:, D_HALF:2 * D_HALF]))
    apad[span, :] = keep(a)
    yield
    cx = _dot(h, win_ref[:, 3 * D_HALF:4 * D_HALF]) * _dot(h, win_ref[:, 4 * D_HALF:5 * D_HALF])
    cpad[span, :] = keep(cx)
    yield
    bgs[...] = _dot(h_own, win_ref[:, 2 * D_HALF:3 * D_HALF])
    yield

    sub = EVEN_CONV_ROWS
    tile = 8
    for j in range(ROW_CHUNK // sub):
        r0 = j * sub
        groups = []
        for cg in range(D_HALF // LANES):
            cols = slice(cg * LANES, (cg + 1) * LANES)
            acc = None
            for r in range(tile):
                part = None
                for m in range(-(-(CONF_WIDTH + 1) // tile)):
                    o = tile * m + r
                    if 1 <= o <= CONF_WIDTH:
                        term = caw_ref[o - 1:o, cols] * apad[r0 + tile * m:r0 + tile * m + sub + tile, cols]
                        part = term if part is None else part + term
                shifted = part[r:r + sub, :]
                acc = shifted if acc is None else acc + shifted
            groups.append(acc)
        acc = jnp.concatenate(groups, axis=1) + cab_ref[...]
        mu = jnp.mean(acc, axis=-1, keepdims=True)
        dlt = acc - mu
        var = jnp.mean(dlt * dlt, axis=-1, keepdims=True)
        a_out = jax.nn.silu(dlt * lax.rsqrt(var + EPS) * lng_ref[...] + lnb_ref[...])
        zs[r0:r0 + sub, 0:D_HALF] = a_out.astype(BF16)
        base = r0 + CONF_HALO - 1
        sc = (cbw_ref[0:1, :] * cpad[base:base + sub, :]
              + cbw_ref[1:2, :] * cpad[base + 1:base + 1 + sub, :]
              + cbw_ref[2:3, :] * cpad[base + 2:base + 2 + sub, :])
        zs[r0:r0 + sub, D_HALF:D_MODEL] = (bgs[r0:r0 + sub, :] * sc).astype(BF16)
        yield

    o = _dot(zs[...], wout_ref[...])
    o_ref[...] = xc_ref[...] + gate1 * _rms(o, ng_ref[1:2, :])
    yield


def _even_layer(x, modv, layer, cond_base, cond_stride, ng, win, caw, cab, lng, lnb, cbw, wout):
    j = layer // 2
    b, t, _ = x.shape
    n_chunks = t // ROW_CHUNK
    nsub = EVEN_CHUNKS
    assert (b * n_chunks) % nsub == 0 and (cond_stride == 0 or n_chunks % nsub == 0)
    hpc = ROW_CHUNK // CONF_HALO
    n_halo_blocks = b * t // CONF_HALO
    rows = ROW_CHUNK + 2 * CONF_HALO
    halo_specs = []
    for k in range(nsub):
        halo_specs += [
            pl.BlockSpec((None, CONF_HALO, D_MODEL),
                         lambda i, k=k: (jnp.maximum((i * nsub + k) * hpc - 1, 0), 0, 0)),
            pl.BlockSpec((None, CONF_HALO, D_MODEL),
                         lambda i, k=k: (jnp.minimum((i * nsub + k + 1) * hpc, n_halo_blocks - 1), 0, 0)),
        ]
    x_halo = x.reshape(n_halo_blocks, CONF_HALO, D_MODEL)
    x_chunks = x.reshape(b * n_chunks, ROW_CHUNK, D_MODEL)
    kern = functools.partial(_even_kernel, n_chunks=n_chunks, nsub=nsub)
    out = pl.pallas_call(
        kern,
        grid=(b * n_chunks // nsub,),
        in_specs=halo_specs + [
            pl.BlockSpec((nsub, ROW_CHUNK, D_MODEL), lambda i: (i, 0, 0)),
            _mod_spec(layer, lambda i: cond_base + cond_stride * ((i * nsub) // n_chunks)),
            _layer_spec((4, D_MODEL), layer),
            _layer_spec((D_MODEL, 5 * D_HALF), j),
            _layer_spec((CONF_WIDTH, D_HALF), j),
            _layer_spec((1, D_HALF), j),
            _layer_spec((1, D_HALF), j),
            _layer_spec((1, D_HALF), j),
            _layer_spec((3, D_HALF), j),
            _layer_spec((D_MODEL, D_MODEL), j),
        ],
        out_specs=pl.BlockSpec((nsub, ROW_CHUNK, D_MODEL), lambda i: (i, 0, 0)),
        out_shape=jax.ShapeDtypeStruct(x_chunks.shape, F32),
        scratch_shapes=[
            pltpu.VMEM((nsub, rows, D_HALF), F32),
            pltpu.VMEM((nsub, rows, D_HALF), F32),
            pltpu.VMEM((nsub, ROW_CHUNK, D_HALF), F32),
            pltpu.VMEM((nsub, ROW_CHUNK, D_MODEL), BF16),
        ],
        compiler_params=_params(1),
        name="even_mixer",
    )(*([x_halo] * (2 * nsub)), x_chunks, modv, ng, win, caw, cab, lng, lnb, cbw, wout)
    return out.reshape(b, t, D_MODEL)


def _mlp_kernel(xp_ref, xs_ref, mod_ref, ng_ref, w1_hbm, w2_hbm, op_ref, os_ref, w1_s, w2_s, stage, sem, *,
                layer, n_ctx_tiles):
    n_blocks = D_FF // D_MODEL

    def block_copy(k):
        c, slot = k // 2, k % 2
        span = pl.ds(c * D_MODEL, D_MODEL)
        src = w1_hbm.at[layer, :, span] if k % 2 == 0 else w2_hbm.at[layer, span, :]
        return pltpu.make_async_copy(src, stage.at[slot], sem.at[slot])

    def fetch(k, dst):
        block_copy(k).wait()
        dst[...] = stage[k % 2].astype(BF16)
        if k + 2 < 2 * n_blocks:
            block_copy(k + 2).start()

    def rows_program(x_ref, o_ref, rows, load):
        x = x_ref[rows, :]
        h = _norm_mod(x, ng_ref[2:3, :], mod_ref[3:4, :], mod_ref[4:5, :]).astype(BF16)
        yield
        acc = None
        for c in range(n_blocks):
            cols = slice(c * D_MODEL, (c + 1) * D_MODEL)
            if load:
                fetch(2 * c, w1_s.at[:, cols])
            hid = jnp.square(jnp.maximum(_dot(h, w1_s[:, cols]), 0.0)).astype(BF16)
            yield
            if load:
                fetch(2 * c + 1, w2_s.at[cols, :])
            part = _dot(hid, w2_s[cols, :])
            acc = part if acc is None else acc + part
            yield
        o_ref[rows, :] = x + mod_ref[5:6, :] * _rms(acc, ng_ref[3:4, :])
        yield

    def tile(x_ref, o_ref, load=False):
        n_rows = x_ref.shape[0]
        if load:
            _drain(rows_program(x_ref, o_ref, slice(0, n_rows), True))
        else:
            _run_staggered((rows_program(x_ref, o_ref, slice(r0, r0 + n_rows // 2), False)
                            for r0 in (0, n_rows // 2)), MLP_STAGGER)

    i = pl.program_id(0)

    @pl.when(i == 0)
    def _():
        block_copy(0).start()
        block_copy(1).start()
        tile(xp_ref, op_ref, load=True)

    @pl.when((i > 0) & (i < n_ctx_tiles))
    def _():
        tile(xp_ref, op_ref)

    @pl.when(i >= n_ctx_tiles)
    def _():
        tile(xs_ref, os_ref)


def _mlp_layer(xp, xs, modv, layer, ng, w1, w2):
    tm = MLP_ROWS
    xp2, xs2 = xp.reshape(-1, D_MODEL), xs.reshape(-1, D_MODEL)
    t_dec = xs.shape[1]
    assert xp2.shape[0] % tm == 0 and t_dec % tm == 0
    n_p, n_s, per_seq = xp2.shape[0] // tm, xs2.shape[0] // tm, t_dec // tm
    ctx_tile = lambda i: (jnp.minimum(i, n_p - 1), 0)
    lat_tile = lambda i: (jnp.maximum(i - n_p, 0), 0)
    yp, ys = pl.pallas_call(
        functools.partial(_mlp_kernel, layer=layer, n_ctx_tiles=n_p),
        grid=(n_p + n_s,),
        in_specs=[
            pl.BlockSpec((tm, D_MODEL), ctx_tile),
            pl.BlockSpec((tm, D_MODEL), lat_tile),
            _mod_spec(layer, lambda i: jnp.where(i < n_p, 0, 1 + jnp.maximum(i - n_p, 0) // per_seq)),
            _layer_spec((4, D_MODEL), layer),
            pl.BlockSpec(memory_space=pl.ANY),
            pl.BlockSpec(memory_space=pl.ANY),
        ],
        out_specs=[pl.BlockSpec((tm, D_MODEL), ctx_tile), pl.BlockSpec((tm, D_MODEL), lat_tile)],
        out_shape=[jax.ShapeDtypeStruct(xp2.shape, F32), jax.ShapeDtypeStruct(xs2.shape, F32)],
        scratch_shapes=[
            pltpu.VMEM((D_MODEL, D_FF), BF16),
            pltpu.VMEM((D_FF, D_MODEL), BF16),
            pltpu.VMEM((2, D_MODEL, D_MODEL), F32),
            pltpu.SemaphoreType.DMA((2,)),
        ],
        compiler_params=_params(1),
        name="mlp",
    )(xp2, xs2, modv, ng, w1, w2)
    return yp.reshape(xp.shape), ys.reshape(xs.shape)


def _log_sigmoid(x):
    return jnp.minimum(x, 0.0) - jnp.log(1.0 + jnp.exp(-jnp.abs(x)))


def _rot_half(x, first_half):
    return jnp.where(first_half, pltpu.roll(x, 96, axis=1), pltpu.roll(x, 32, axis=1))


def _values_and_ones(v):
    vt = jnp.transpose(v)
    row = lax.broadcasted_iota(jnp.int32, vt.shape, 0)
    return jnp.where(row < HEAD_DIM, vt, 1.0)


def _aligned(i, m):
    return i * m if isinstance(i, int) else pl.multiple_of(i * m, m)


def _drain(pieces):
    for _ in pieces:
        pass


def _each(body, n, static, unroll=1):
    if static:
        for i in range(n):
            yield from body(i)
    else:
        lax.fori_loop(0, n, lambda i, c: (_drain(body(i)), c)[1], 0, unroll=unroll)


def _alternate(*programs):
    live = list(programs)
    while live:
        for g in list(live):
            try:
                next(g)
            except StopIteration:
                live.remove(g)
            else:
                yield


def _odd_kernel(*refs, t, lc, latent, nseq):
    shared = set(range(1, 11 if latent else 9))
    joint = not latent and nseq > 1 and t == ROW_CHUNK
    programs = [_odd_seq(*[r if k in shared else r.at[sq] for k, r in enumerate(refs)],
                         t=t, lc=lc, latent=latent, joint=joint) for sq in range(nseq)]
    if joint:
        parts = [next(g) for g in programs]
        h_all = jnp.concatenate([proj_h(0) for proj_h, _ in parts], axis=0)
        for idx, (weight, _) in enumerate(parts[0][1]):
            y = _dot(h_all, weight())
            for sq, (_, plan) in enumerate(parts):
                plan[idx][1](y[sq * ROW_CHUNK:(sq + 1) * ROW_CHUNK], 0)
    _run_staggered(programs, ODD_STAGGER)


def _odd_seq(*refs, t, lc, latent, joint):
    if latent:
        (x_ref, mod_ref, ng_ref, win_ref, wg_ref, gb_ref, sink_ref, hn_ref, wout_ref,
         cos_ref, sin_ref, kc_ref, vc_ref, cin_ref, nin_ref, min_ref,
         o_ref,
         qa_s, qb_s, kk_s, vvt_s, qma_s, qmb_s, qm_s, km_s, vmt_s, vmtf_s, om_s, g_s, z_s, hft_s, hbt_s,
         sr_s, st_s, qc_s, c_s, n_s, m_s, s_s, p_s, kc_s, vct_s) = refs
    else:
        (x_ref, mod_ref, ng_ref, win_ref, wg_ref, gb_ref, sink_ref, hn_ref, wout_ref,
         o_ref, ko_ref, vo_ref, cf_ref, cb_ref, no_ref, mo_ref,
         qa_s, qb_s, kk_s, vvt_s, qma_s, qmb_s, qm_s, km_s, vmt_s, vmtf_s, om_s, g_s, z_s, hft_s, hbt_s,
         sr_s, st_s, qc_s, c_s, n_s, m_s, s_s, p_s) = refs

    static = not latent
    n_blocks = t // CHUNK
    pad = CHUNK if latent else 0
    shift1, scale1, gate1 = mod_ref[0:1, :], mod_ref[1:2, :], mod_ref[2:3, :]
    lane = lax.broadcasted_iota(jnp.int32, (1, LANES), 1)
    left = lane < HEAD_DIM
    first_half = (lane % HEAD_DIM) < (HEAD_DIM // 2)
    ti = lax.broadcasted_iota(jnp.int32, (CHUNK, CHUNK), 0)
    si = lax.broadcasted_iota(jnp.int32, (CHUNK, CHUNK), 1)
    top = ti < HEAD_DIM
    same_head = top == (si < HEAD_DIM)

    if latent:
        for kv in range(N_KV):
            kk_s[kv, 0:CHUNK, :] = jnp.zeros((CHUNK, LANES), BF16)
            kk_s[kv, CHUNK + t:2 * CHUNK + t, :] = jnp.zeros((CHUNK, LANES), BF16)
            vvt_s[kv, :, 0:CHUNK] = jnp.zeros((LANES, CHUNK), BF16)
            vvt_s[kv, :, CHUNK + t:2 * CHUNK + t] = jnp.zeros((LANES, CHUNK), BF16)
            kc_s[kv] = kc_ref[kv].astype(BF16)
            vct_s[kv] = _values_and_ones(vc_ref[kv]).astype(BF16)

    base = D_HALF + 2 * LANES

    def proj_rows(rc):
        r0 = _aligned(rc, ROW_CHUNK)
        return pl.ds(r0, ROW_CHUNK), pl.ds(r0 + pad, ROW_CHUNK)

    def proj_h(rc):
        rows, _ = proj_rows(rc)
        return _norm_mod(x_ref[rows, :], ng_ref[0:1, :], shift1, scale1).astype(BF16)

    def rope(v, rows):
        return v * cos_ref[rows, :] + _rot_half(v, first_half) * sin_ref[rows, :]

    def use_q(q, rc):
        rows, _ = proj_rows(rc)
        for p in range(N_PAIRS):
            cols = slice(p * LANES, (p + 1) * LANES)
            qp = q[:, cols]
            if latent:
                qp = rope(qp, rows)
            qp = qp * ATT_SCALE
            qa_s[rows, cols] = jnp.where(left, qp, 0.0).astype(BF16)
            qb_s[rows, cols] = jnp.where(left, 0.0, qp).astype(BF16)

    def use_kv(kv2, rc):
        rows, krows = proj_rows(rc)
        ka, va = kv2[:, 0:LANES], kv2[:, LANES:2 * LANES]
        if latent:
            ka = rope(ka, rows)
        kr = pltpu.roll(ka, HEAD_DIM, axis=1)
        vr = pltpu.roll(va, HEAD_DIM, axis=1)
        if not latent:
            ka_t, va_t = jnp.transpose(ka), jnp.transpose(va)
            for kv in range(N_KV):
                ko_ref[0, kv, :, rows] = ka_t[kv * HEAD_DIM:(kv + 1) * HEAD_DIM, :]
                vo_ref[0, kv, :, rows] = va_t[kv * HEAD_DIM:(kv + 1) * HEAD_DIM, :]
        kk_s[0, krows, :] = jnp.where(left, ka, kr).astype(BF16)
        kk_s[1, krows, :] = jnp.where(left, kr, ka).astype(BF16)
        vvt_s[0, :, krows] = _values_and_ones(va).astype(BF16)
        vvt_s[1, :, krows] = _values_and_ones(vr).astype(BF16)

    def use_qm(qm, rc):
        rows, _ = proj_rows(rc)
        qm_s[rows, :] = qm.astype(BF16)
        for p in range(N_PAIRS):
            cols = slice(p * LANES, (p + 1) * LANES)
            qma_s[rows, cols] = jnp.where(left, qm[:, cols], 0.0).astype(BF16)
            qmb_s[rows, cols] = jnp.where(left, 0.0, qm[:, cols]).astype(BF16)

    def use_vm(vm, rc):
        rows, _ = proj_rows(rc)
        for p in range(N_PAIRS):
            cols = slice(p * LANES, (p + 1) * LANES)
            vt = jnp.transpose(vm[:, cols])
            vmtf_s[cols, rows] = vt
            vmt_s[cols, rows] = vt.astype(BF16)

    def use_km(km, rc):
        km_s[proj_rows(rc)[0], :] = (km * (HEAD_DIM ** -0.5)).astype(BF16)

    def use_om(om, rc):
        om_s[proj_rows(rc)[0], :] = om

    def use_g(g, rc):
        g_s[proj_rows(rc)[0], :] = g + gb_ref[...]

    def columns(start, width):
        return lambda: win_ref[:, start:start + width]

    proj_plan = [(columns(0, D_HALF), use_q), (columns(D_HALF, 2 * LANES), use_kv),
                 (columns(base, D_HALF), use_qm), (columns(base + 2 * D_HALF, D_HALF), use_vm),
                 (columns(base + D_HALF, D_HALF), use_km), (columns(base + 3 * D_HALF, D_HALF), use_om),
                 (lambda: wg_ref[...], use_g)]

    def project(rc):
        h = proj_h(rc)
        for weight, use in proj_plan:
            use(_dot(h, weight()), rc)
            yield

    if joint:
        yield proj_h, proj_plan
    else:
        yield from _each(project, t // ROW_CHUNK, static)

    kj = lax.broadcasted_iota(jnp.int32, (3 * CHUNK, 2 * CHUNK), 0)
    qi = lax.broadcasted_iota(jnp.int32, (3 * CHUNK, 2 * CHUNK), 1) % CHUNK
    band_ok = jnp.abs(kj - CHUNK - qi) <= WINDOW
    head_a = lax.broadcasted_iota(jnp.int32, (1, 2 * CHUNK), 1) < CHUNK

    n_keys = _att_keys(lc, latent)
    group = _att_group(latent)

    def attention():
        def attend(i):
            r0 = _aligned(i, CHUNK)
            rows = pl.ds(r0, CHUNK)
            if latent:
                key_pos = kj + (i - 1) * CHUNK
                mask = band_ok & (key_pos >= 0) & (key_pos < t)
                win = pl.ds(r0, 3 * CHUNK)
            for g0 in range(0, N_PAIRS, group):
                for gi in range(group):
                    p = g0 + gi
                    kv = p // (N_PAIRS // N_KV)
                    cols = slice(p * LANES, (p + 1) * LANES)
                    q2 = jnp.concatenate([qa_s[rows, cols], qb_s[rows, cols]], axis=0)
                    if latent:
                        s_s[gi, 0:lc, :] = _dot_nt(kc_s[kv], q2)
                        s_s[gi, lc:n_keys, :] = jnp.where(mask, _dot_nt(kk_s[kv, win, :], q2), -jnp.inf)
                    else:
                        s_s[gi] = _dot_nt(kk_s[kv], q2)
                yield
                maxes = []
                for gi in range(group):
                    p = g0 + gi
                    sink = jnp.where(head_a, sink_ref[2 * p:2 * p + 1, 0:1], sink_ref[2 * p + 1:2 * p + 2, 0:1])
                    mx = jnp.maximum(jnp.max(s_s[gi], axis=0, keepdims=True), sink)
                    p_s[gi] = jnp.exp(s_s[gi] - mx).astype(BF16)
                    maxes.append((sink, mx))
                yield
                for gi in range(group):
                    p = g0 + gi
                    kv = p // (N_PAIRS // N_KV)
                    cols = slice(p * LANES, (p + 1) * LANES)
                    if latent:
                        num = (_dot(vct_s[kv], p_s[gi, 0:lc, :]) + _dot(vvt_s[kv, :, win], p_s[gi, lc:n_keys, :]))
                    else:
                        num = _dot(vvt_s[kv], p_s[gi])
                    sink, mx = maxes[gi]
                    den = num[HEAD_DIM:HEAD_DIM + 1, :] + jnp.exp(sink - mx)
                    out = num[0:HEAD_DIM, :] * (1.0 / den)
                    pair = jnp.concatenate([out[:, 0:CHUNK], out[:, CHUNK:2 * CHUNK]], axis=0)
                    z_s[rows, cols] = jnp.transpose(pair).astype(BF16)
                yield

        return attend

    if latent:
        c_s[...] = cin_ref[...]
        n_s[...] = nin_ref[...]
        m_s[...] = min_ref[...]
    else:
        c_s[...] = jnp.zeros(c_s.shape, F32)
        n_s[...] = jnp.zeros(n_s.shape, F32)
        m_s[...] = jnp.zeros(m_s.shape, F32)

    see = (ti <= si, ti >= si)
    tri = tuple(m.astype(F32).astype(BF16) for m in see)
    last = (CHUNK - 1, 0)

    def mlstm(i):
        offs = (_aligned(i, CHUNK), _aligned(n_blocks - 1 - i, CHUNK))
        half = (slice(0, LANES), slice(LANES, 2 * LANES))
        a_rows, b_rows = [], []
        for d in range(2):
            gt = jnp.transpose(g_s[pl.ds(offs[d], CHUNK), :])
            lf = _log_sigmoid(gt[2 * N_HEADS:4 * N_HEADS, :])
            bcum = _dot_exact_lhs(lf, tri[d])[8 * d:8 * d + 8, :]
            a_rows.append(gt[8 * d:8 * d + 8, :] - bcum)
            b_rows.append(bcum)
        yield
        q_ns = []
        for d in range(2):
            rows = pl.ds(offs[d], CHUNK)
            for p in range(N_PAIRS):
                u = d * N_PAIRS + p
                cols = slice(p * LANES, (p + 1) * LANES)
                q2 = jnp.concatenate([qma_s[rows, cols], qmb_s[rows, cols]], axis=0)
                sr_s[u] = _dot_nt(km_s[rows, cols], q2)
                qc_s[u] = _dot_nt(c_s[u].astype(BF16), qm_s[rows, cols])
                n8 = jnp.broadcast_to(n_s[u:u + 1, :], (8, LANES)).astype(BF16)
                q_ns.append(_dot_nt(n8, q2)[0:1, :])
            yield
        stats = {}
        for d in range(2):
            for hd in range(N_HEADS):
                u, j = d * N_PAIRS + hd // 2, hd % 2
                mrow = d * N_HEADS + hd
                a_row = a_rows[d][hd:hd + 1, :]
                b_row = b_rows[d][hd:hd + 1, :]
                m_prev = m_s[mrow:mrow + 1, 0:1]
                a_col = jnp.transpose(jnp.broadcast_to(a_row, (CHUNK, CHUNK)))
                z_t = jnp.where(see[d], a_col, -jnp.inf)
                m_run = jnp.maximum(jnp.max(z_t, axis=0, keepdims=True), m_prev)
                s_t = sr_s[u, :, half[j]] * jnp.exp(z_t - m_run)
                st_s[u, :, half[j]] = s_t.astype(BF16)
                w_int = jnp.exp(m_prev - m_run)
                den = jnp.sum(s_t, axis=0, keepdims=True) + w_int * q_ns[u][:, half[j]]
                inv = 1.0 / jnp.maximum(jnp.abs(den), jnp.exp(-(b_row + m_run)))
                m_last = m_run[:, last[d]:last[d] + 1]
                stats[(u, j)] = (w_int, inv, jnp.exp(a_row - m_last), jnp.exp(m_prev - m_last))
                m_s[mrow:mrow + 1, :] = jnp.broadcast_to(b_row[:, last[d]:last[d] + 1] + m_last, (1, LANES))
            yield
        for d in range(2):
            rows = pl.ds(offs[d], CHUNK)
            ht_s = (hft_s, hbt_s)[d]
            for p in range(N_PAIRS):
                u = d * N_PAIRS + p
                cols = slice(p * LANES, (p + 1) * LANES)
                (w_a, inv_a, e_a, dec_a), (w_b, inv_b, e_b, dec_b) = stats[(u, 0)], stats[(u, 1)]
                kp = km_s[rows, cols]
                num2 = _dot(vmt_s[cols, rows], st_s[u])
                num = jnp.where(top, num2[:, half[0]], num2[:, half[1]])
                ht_s[cols, rows] = (num + jnp.where(top, w_a, w_b) * qc_s[u]) * jnp.where(top, inv_a, inv_b)
                vt_e = (vmtf_s[cols, rows] * jnp.where(top, e_a, e_b)).astype(BF16)
                c_s[u] = jnp.where(top, dec_a, dec_b) * c_s[u] + jnp.where(same_head, _dot(vt_e, kp), 0.0)
                e2 = jnp.concatenate([e_a, e_b, jnp.zeros((6, CHUNK), F32)], axis=0).astype(BF16)
                n_k = _dot(e2, kp)
                n_s[u:u + 1, :] = (jnp.where(left, dec_a, dec_b) * n_s[u:u + 1, :]
                                   + jnp.where(left, n_k[0:1, :], n_k[1:2, :]))
            yield

    attend = attention()
    yield from _each(lambda i: _alternate(attend(i), mlstm(i)), n_blocks, static, unroll=2)

    if not latent:
        for d, c_ref in enumerate((cf_ref, cb_ref)):
            for p in range(N_PAIRS):
                c_pair = jnp.transpose(c_s[d * N_PAIRS + p])
                c_ref[0, 2 * p] = c_pair[0:HEAD_DIM, 0:HEAD_DIM]
                c_ref[0, 2 * p + 1] = c_pair[HEAD_DIM:LANES, HEAD_DIM:LANES]
        no_ref[...] = n_s[...]
        mo_ref[...] = m_s[...]
        yield

    top_w = lax.broadcasted_iota(jnp.int32, (LANES, ROW_CHUNK), 0) < HEAD_DIM

    def finish(rc):
        r0 = _aligned(rc, ROW_CHUNK)
        rows = pl.ds(r0, ROW_CHUNK)
        for p in range(N_PAIRS):
            cols = slice(p * LANES, (p + 1) * LANES)
            hm = hft_s[cols, rows] + hbt_s[cols, rows]
            sq = hm * hm
            ms_a = jnp.sum(sq[0:HEAD_DIM], axis=0, keepdims=True)
            ms_b = jnp.sum(sq[HEAD_DIM:LANES], axis=0, keepdims=True)
            ms = jnp.where(top_w, ms_a, ms_b) * (1.0 / HEAD_DIM)
            y = jnp.transpose(hm * lax.rsqrt(ms + EPS)) * hn_ref[:, cols] * jax.nn.sigmoid(om_s[rows, cols])
            z_s[rows, D_HALF + p * LANES:D_HALF + (p + 1) * LANES] = y.astype(BF16)
            if p % 2:
                yield
        o = _dot(z_s[rows, :], wout_ref[...])
        o_ref[rows, :] = x_ref[rows, :] + gate1 * _rms(o, ng_ref[1:2, :])
        yield

    yield from _each(finish, t // ROW_CHUNK, static)


def _att_keys(lc, latent):
    return lc + 3 * CHUNK if latent else lc


def _att_group(latent):
    return 2 if latent else N_PAIRS


def _odd_scratch(t, lc, latent, nseq):
    pad = 2 * CHUNK if latent else 0
    att = (_att_group(latent), _att_keys(lc, latent), 2 * CHUNK)
    shapes = [
        pltpu.VMEM((t, D_HALF), BF16),
        pltpu.VMEM((t, D_HALF), BF16),
        pltpu.VMEM((N_KV, t + pad, LANES), BF16),
        pltpu.VMEM((N_KV, LANES, t + pad), BF16),
        pltpu.VMEM((t, D_HALF), BF16),
        pltpu.VMEM((t, D_HALF), BF16),
        pltpu.VMEM((t, D_HALF), BF16),
        pltpu.VMEM((t, D_HALF), BF16),
        pltpu.VMEM((D_HALF, t), BF16),
        pltpu.VMEM((D_HALF, t), F32),
        pltpu.VMEM((t, D_HALF), F32),
        pltpu.VMEM((t, LANES), F32),
        pltpu.VMEM((t, D_MODEL), BF16),
        pltpu.VMEM((D_HALF, t), F32),
        pltpu.VMEM((D_HALF, t), F32),
        pltpu.VMEM((2 * N_PAIRS, CHUNK, 2 * LANES), F32),
        pltpu.VMEM((2 * N_PAIRS, CHUNK, 2 * LANES), BF16),
        pltpu.VMEM((2 * N_PAIRS, LANES, CHUNK), F32),
        pltpu.VMEM((2 * N_PAIRS, LANES, LANES), F32),
        pltpu.VMEM((2 * N_PAIRS, LANES), F32),
        pltpu.VMEM((2 * N_HEADS, LANES), F32),
        pltpu.VMEM(att, F32),
        pltpu.VMEM(att, BF16),
    ]
    if latent:
        shapes += [pltpu.VMEM((N_KV, lc, LANES), BF16), pltpu.VMEM((N_KV, LANES, lc), BF16)]
    return [pltpu.VMEM((nseq,) + tuple(sh.shape), sh.dtype) for sh in shapes]


def _odd_common_specs(t, layer, cond_base, cond_stride, nseq):
    assert cond_stride == 0 or nseq == 1
    j = layer // 2
    return [
        _per_seq((t, D_MODEL), nseq),
        _mod_spec(layer, lambda i: cond_base + cond_stride * i),
        _layer_spec((4, D_MODEL), layer),
        _layer_spec((D_MODEL, D_IN_ODD), j),
        _const_spec((D_MODEL, LANES)),
        _const_spec((1, LANES)),
        _const_spec((N_HEADS, LANES)),
        _layer_spec((1, D_HALF), j),
        _layer_spec((D_MODEL, D_MODEL), j),
    ]


def _per_seq(shape, nseq):
    return pl.BlockSpec((nseq,) + shape, lambda i: (i,) + (0,) * len(shape))


def _odd_context(x, modv, layer, ng, w_main, w_gate, gate_bias, sink_b, hnorm, wout):
    b, t, _ = x.shape
    nseq = ODD_CTX_SEQS
    assert b % nseq == 0
    kern = functools.partial(_odd_kernel, t=t, lc=t, latent=False, nseq=nseq)
    per_seq = functools.partial(_per_seq, nseq=nseq)
    return pl.pallas_call(
        kern,
        grid=(b // nseq,),
        in_specs=_odd_common_specs(t, layer, 0, 0, nseq),
        out_specs=[per_seq((t, D_MODEL)),
                   per_seq((1, N_KV, HEAD_DIM, t)), per_seq((1, N_KV, HEAD_DIM, t)),
                   per_seq((1, N_HEADS, HEAD_DIM, HEAD_DIM)), per_seq((1, N_HEADS, HEAD_DIM, HEAD_DIM)),
                   per_seq((2 * N_PAIRS, LANES)), per_seq((2 * N_HEADS, LANES))],
        out_shape=[jax.ShapeDtypeStruct((b, t, D_MODEL), F32),
                   jax.ShapeDtypeStruct((b, 1, N_KV, HEAD_DIM, t), F32),
                   jax.ShapeDtypeStruct((b, 1, N_KV, HEAD_DIM, t), F32),
                   jax.ShapeDtypeStruct((b, 1, N_HEADS, HEAD_DIM, HEAD_DIM), F32),
                   jax.ShapeDtypeStruct((b, 1, N_HEADS, HEAD_DIM, HEAD_DIM), F32),
                   jax.ShapeDtypeStruct((b, 2 * N_PAIRS, LANES), F32),
                   jax.ShapeDtypeStruct((b, 2 * N_HEADS, LANES), F32)],
        scratch_shapes=_odd_scratch(t, t, False, nseq),
        compiler_params=_params(1),
        name="odd_mixer_context",
    )(x, modv, ng, w_main, w_gate, gate_bias, sink_b, hnorm, wout)


def _odd_latent(x, modv, layer, ng, w_main, w_gate, gate_bias, sink_b, hnorm, wout, cos_t, sin_t, kc, vc,
                c_in, n_in, m_in):
    b, t, _ = x.shape
    lc = kc.shape[2]
    kern = functools.partial(_odd_kernel, t=t, lc=lc, latent=True, nseq=1)
    per_seq = functools.partial(_per_seq, nseq=1)
    return pl.pallas_call(
        kern,
        grid=(b,),
        in_specs=_odd_common_specs(t, layer, 1, 1, 1) + [
            _const_spec((t, LANES)), _const_spec((t, LANES)),
            per_seq((N_KV, lc, LANES)), per_seq((N_KV, lc, LANES)),
            per_seq((2 * N_PAIRS, LANES, LANES)), per_seq((2 * N_PAIRS, LANES)), per_seq((2 * N_HEADS, LANES)),
        ],
        out_specs=per_seq((t, D_MODEL)),
        out_shape=jax.ShapeDtypeStruct((b, t, D_MODEL), F32),
        scratch_shapes=_odd_scratch(t, lc, True, 1),
        compiler_params=_params(1),
        name="odd_mixer_latent",
    )(x, modv, ng, w_main, w_gate, gate_bias, sink_b, hnorm, wout, cos_t, sin_t, kc, vc, c_in, n_in, m_in)


def _rope_tables(t):
    rows = t // GRID_W
    row = jnp.broadcast_to(jnp.arange(rows)[:, None], (rows, GRID_W)).reshape(t).astype(F32)
    col = jnp.broadcast_to(jnp.arange(GRID_W)[None, :], (rows, GRID_W)).reshape(t).astype(F32)
    n_freq = HEAD_DIM // 4
    inv_freq = ROPE_BASE ** (-jnp.arange(n_freq, dtype=F32) / n_freq)
    ang = jnp.concatenate([row[:, None] * inv_freq, col[:, None] * inv_freq], axis=-1)
    cos, sin = jnp.cos(ang), jnp.sin(ang)
    cos_l = jnp.tile(cos, (1, LANES // cos.shape[1]))
    sin_l = jnp.tile(jnp.concatenate([-sin, sin], axis=-1), (1, LANES // HEAD_DIM))
    return cos_l, sin_l


def _pair_blockdiag(c):
    b = c.shape[0]
    c = c.reshape(b, N_PAIRS, 2, HEAD_DIM, HEAD_DIM)
    z = jnp.zeros_like(c[:, :, 0])
    top = jnp.concatenate([c[:, :, 0], z], axis=-1)
    bot = jnp.concatenate([z, c[:, :, 1]], axis=-1)
    return jnp.concatenate([top, bot], axis=-2)


def _lane_bcast(v):
    return jnp.broadcast_to(v[..., None], v.shape + (LANES,))


def kernel(x_prompt, x_sample, c, cache_k, cache_v, state_c_fwd, state_n_fwd, state_m_fwd, state_c_bwd, state_n_bwd, state_m_bwd, c_ctx, mod_w, mod_b, norm_g, mlp_w1, mlp_w2, even_in_w, conv_a_w, conv_a_b, ln_a_g, ln_a_b, conv_b_w, even_out_w, odd_in_w, attn_sink, gate_b, hnorm_g, odd_out_w):
    n_dec = x_sample.shape[0]
    n_ctx = x_prompt.shape[0]
    cond = jnp.concatenate([c_ctx[None, :], c, jnp.zeros((COND_ROWS - 1 - n_dec, D_MODEL), F32)], axis=0)
    modv = _modulation(cond, mod_w, mod_b)

    yp, ys = x_prompt, x_sample
    w1, w2 = mlp_w1, mlp_w2

    ev = (norm_g, even_in_w.astype(BF16), conv_a_w, conv_a_b[:, None, :], ln_a_g[:, None, :], ln_a_b[:, None, :],
          conv_b_w, even_out_w.astype(BF16))
    yp = _even_layer(yp, modv, 0, 0, 0, *ev)
    ys = _even_layer(ys, modv, 0, 1, 1, *ev)
    yp, ys = _mlp_layer(yp, ys, modv, 0, norm_g, w1, w2)

    order = jnp.array([0, 2, 1, 3])
    d_main = D_IN_ODD - 4 * N_HEADS
    wg = odd_in_w[0][:, d_main:].reshape(D_MODEL, 4, N_HEADS)[:, order, :].reshape(D_MODEL, 4 * N_HEADS)
    w_gate = jnp.pad(wg, ((0, 0), (0, LANES - 4 * N_HEADS))).astype(BF16)
    gate_bias = jnp.pad(gate_b[0][order, :].reshape(1, 4 * N_HEADS), ((0, 0), (0, LANES - 4 * N_HEADS)))
    sink_b = _lane_bcast(attn_sink[0])
    odd = (1, norm_g, odd_in_w.astype(BF16), w_gate, gate_bias, sink_b, hnorm_g[:, None, :], odd_out_w.astype(BF16))

    op, k_t, v_t, c_f, c_b, n_new, m_new = _odd_context(yp, modv, *odd)
    new_k, new_v = jnp.swapaxes(k_t, -1, -2), jnp.swapaxes(v_t, -1, -2)

    t_dec = x_sample.shape[1]
    cos_t, sin_t = _rope_tables(t_dec)
    kc = jnp.concatenate([cache_k[:, 0], cache_k[:, 0]], axis=-1)
    vc = jnp.concatenate([cache_v[:, 0], cache_v[:, 0]], axis=-1)
    c_in = jnp.concatenate([_pair_blockdiag(jnp.swapaxes(state_c_fwd[:, 0], -1, -2)),
                            _pair_blockdiag(jnp.swapaxes(state_c_bwd[:, 0], -1, -2))], axis=1)
    n_in = jnp.concatenate([state_n_fwd[:, 0].reshape(n_dec, N_PAIRS, LANES),
                            state_n_bwd[:, 0].reshape(n_dec, N_PAIRS, LANES)], axis=1)
    m_in = _lane_bcast(jnp.concatenate([state_m_fwd[:, 0], state_m_bwd[:, 0]], axis=1))
    os_ = _odd_latent(ys, modv, *odd, cos_t, sin_t, kc, vc, c_in, n_in, m_in)

    yp, ys = _mlp_layer(op, os_, modv, 1, norm_g, w1, w2)

    n_f = n_new[:, :N_PAIRS].reshape(n_ctx, N_HEADS, HEAD_DIM)[:, None]
    n_b = n_new[:, N_PAIRS:].reshape(n_ctx, N_HEADS, HEAD_DIM)[:, None]
    m_f = m_new[:, :N_HEADS, 0][:, None]
    m_b = m_new[:, N_HEADS:, 0][:, None]
    return (yp, ys, new_k, new_v, c_f, n_f, m_f, c_b, n_b, m_b)
```

```python
import functools

import jax
import jax.numpy as jnp
from jax import lax
from jax.experimental import pallas as pl
from jax.experimental.pallas import tpu as pltpu

F32 = jnp.float32
BF16 = jnp.bfloat16

D_MODEL = 1024
D_FF = 4 * D_MODEL
EPS = 1e-6
D_HALF = D_MODEL // 2
CONF_WIDTH = 31
CONF_HALO = 16
HEAD_DIM = 64
N_HEADS = 8
N_PAIRS = N_HEADS // 2
N_KV = 2
LANES = 128
CHUNK = 128
WINDOW = 128
GRID_W = 64
ROPE_BASE = 10000.0
ATT_SCALE = HEAD_DIM ** -0.5
D_IN_ODD = D_HALF + 2 * N_KV * HEAD_DIM + 4 * D_HALF + 4 * N_HEADS
ROW_CHUNK = 256
MLP_ROWS = 512
MLP_STAGGER = 3
ODD_ROWS = 512
ODD_CTX_SEQS = 2
ODD_STAGGER = 6
EVEN_CONV_ROWS = 64
EVEN_CHUNKS = 2
EVEN_STAGGER = 3
COND_ROWS = 8
VMEM_LIMIT = 56 * 1024 * 1024


def _dot(a, b):
    return jnp.dot(a, b, preferred_element_type=F32)


def _dot_nt(a, b):
    return lax.dot_general(a, b, (((1,), (1,)), ((), ())), preferred_element_type=F32)


def _split3(x):
    hi = x.astype(BF16)
    r1 = x - hi.astype(F32)
    mid = r1.astype(BF16)
    lo = (r1 - mid.astype(F32)).astype(BF16)
    return hi, mid, lo


def _dot_exact_lhs(x, b01):
    hi, mid, lo = _split3(x)
    return _dot(hi, b01) + _dot(mid, b01) + _dot(lo, b01)


def _rms(x, g):
    return x * lax.rsqrt(jnp.mean(x * x, axis=-1, keepdims=True) + EPS) * g


def _norm_mod(x, g, shift, scale):
    return _rms(x, g) * (1.0 + scale) + shift


def _params(n_grid):
    return pltpu.CompilerParams(dimension_semantics=("arbitrary",) * n_grid, vmem_limit_bytes=VMEM_LIMIT)


def _const_spec(shape):
    zeros = (0,) * len(shape)
    return pl.BlockSpec(shape, lambda *_: zeros, pipeline_mode=pl.Buffered(1))


def _layer_spec(shape, layer):
    index = (layer,) + (0,) * len(shape)
    return pl.BlockSpec((None,) + shape, lambda *_: index, pipeline_mode=pl.Buffered(1))


def _mod_spec(layer, cond_of):
    return pl.BlockSpec((None, None, 6, D_MODEL), lambda *idx: (layer, cond_of(*idx), 0, 0))


def _mod_kernel(cond_ref, w_ref, b_ref, o_ref):
    s = jax.nn.silu(cond_ref[...]).astype(BF16)
    o_ref[...] = _dot(s, w_ref[...].astype(BF16)) + b_ref[...]


def _modulation(cond, mod_w, mod_b):
    depth = mod_w.shape[0]
    n_out = mod_w.shape[2]
    tn = 2 * D_MODEL
    out = pl.pallas_call(
        _mod_kernel,
        grid=(depth, n_out // tn),
        in_specs=[
            pl.BlockSpec((COND_ROWS, D_MODEL), lambda l, j: (0, 0)),
            pl.BlockSpec((None, D_MODEL, tn), lambda l, j: (l, 0, j)),
            pl.BlockSpec((None, 1, tn), lambda l, j: (l, 0, j)),
        ],
        out_specs=pl.BlockSpec((None, COND_ROWS, tn), lambda l, j: (l, 0, j)),
        out_shape=jax.ShapeDtypeStruct((depth, COND_ROWS, n_out), F32),
        compiler_params=_params(2),
        name="modulation",
    )(cond, mod_w, mod_b.reshape(depth, 1, n_out))
    return out.reshape(depth, COND_ROWS, 6, D_MODEL)


def _run_staggered(programs, stagger):
    programs = list(programs)
    live, rounds = [], 0
    while programs or live:
        if programs and rounds % stagger == 0:
            live.append(programs.pop(0))
        for g in list(live):
            try:
                next(g)
            except StopIteration:
                live.remove(g)
        rounds += 1


def _even_kernel(*refs, n_chunks, nsub):
    halos, (xc_ref, mod_ref, ng_ref, win_ref, caw_ref, cab_ref, lng_ref, lnb_ref, cbw_ref, wout_ref, o_ref,
            apad, cpad, bgs, zs) = refs[:2 * nsub], refs[2 * nsub:]
    shared = (mod_ref, ng_ref, win_ref, caw_ref, cab_ref, lng_ref, lnb_ref, cbw_ref, wout_ref)
    _run_staggered(
        (_even_chunk(pl.program_id(0) * nsub + k, halos[2 * k], xc_ref.at[k], halos[2 * k + 1], *shared,
                     o_ref.at[k], apad.at[k], cpad.at[k], bgs.at[k], zs.at[k], n_chunks=n_chunks)
         for k in range(nsub)), EVEN_STAGGER)


def _even_chunk(g, xp_ref, xc_ref, xn_ref, mod_ref, ng_ref, win_ref, caw_ref, cab_ref, lng_ref, lnb_ref,
                cbw_ref, wout_ref, o_ref, apad, cpad, bgs, zs, *, n_chunks):
    c = g % n_chunks if n_chunks > 1 else 0
    rows = ROW_CHUNK + 2 * CONF_HALO
    shift1, scale1, gate1 = mod_ref[0:1, :], mod_ref[1:2, :], mod_ref[2:3, :]
    own = slice(CONF_HALO, CONF_HALO + ROW_CHUNK)
    if n_chunks == 1:
        h = _norm_mod(xc_ref[...], ng_ref[0:1, :], shift1, scale1).astype(BF16)
        h_own = h
        for pad_ref in (apad, cpad):
            pad_ref[0:CONF_HALO, :] = jnp.zeros((CONF_HALO, D_HALF), F32)
            pad_ref[CONF_HALO + ROW_CHUNK:rows, :] = jnp.zeros((CONF_HALO, D_HALF), F32)
        keep = lambda v: v
        span = own
    else:
        xh = jnp.concatenate([xp_ref[...], xc_ref[...], xn_ref[...]], axis=0)
        h = _norm_mod(xh, ng_ref[0:1, :], shift1, scale1).astype(BF16)
        h_own = h[own]
        ri = lax.broadcasted_iota(jnp.int32, (rows, D_HALF), 0)
        lo = jnp.where(c == 0, CONF_HALO, 0)
        hi = jnp.where(c == n_chunks - 1, CONF_HALO + ROW_CHUNK, rows)
        inside = (ri >= lo) & (ri < hi)
        keep = lambda v: jnp.where(inside, v, 0.0)
        span = slice(0, rows)
    a = _dot(h, win_ref[:, 0:D_HALF]) * jax.nn.sigmoid(_dot(h, win_ref[:, D_HALF:2 * D_HALF]))
    apad[span, :] = keep(a)
    yield
    cx = _dot(h, win_ref[:, 3 * D_HALF:4 * D_HALF]) * _dot(h, win_ref[:, 4 * D_HALF:5 * D_HALF])
    cpad[span, :] = keep(cx)
    yield
    bgs[...] = _dot(h_own, win_ref[:, 2 * D_HALF:3 * D_HALF])
    yield

    sub = EVEN_CONV_ROWS
    tile = 8
    for j in range(ROW_CHUNK // sub):
        r0 = j * sub
        groups = []
        for cg in range(D_HALF // LANES):
            cols = slice(cg * LANES, (cg + 1) * LANES)
            acc = None
            for r in range(tile):
                part = None
                for m in range(-(-(CONF_WIDTH + 1) // tile)):
                    o = tile * m + r
                    if 1 <= o <= CONF_WIDTH:
                        term = caw_ref[o - 1:o, cols] * apad[r0 + tile * m:r0 + tile * m + sub + tile, cols]
                        part = term if part is None else part + term
                shifted = part[r:r + sub, :]
                acc = shifted if acc is None else acc + shifted
            groups.append(acc)
        acc = jnp.concatenate(groups, axis=1) + cab_ref[...]
        mu = jnp.mean(acc, axis=-1, keepdims=True)
        dlt = acc - mu
        var = jnp.mean(dlt * dlt, axis=-1, keepdims=True)
        a_out = jax.nn.silu(dlt * lax.rsqrt(var + EPS) * lng_ref[...] + lnb_ref[...])
        zs[r0:r0 + sub, 0:D_HALF] = a_out.astype(BF16)
        base = r0 + CONF_HALO - 1
        sc = (cbw_ref[0:1, :] * cpad[base:base + sub, :]
              + cbw_ref[1:2, :] * cpad[base + 1:base + 1 + sub, :]
              + cbw_ref[2:3, :] * cpad[base + 2:base + 2 + sub, :])
        zs[r0:r0 + sub, D_HALF:D_MODEL] = (bgs[r0:r0 + sub, :] * sc).astype(BF16)
        yield

    o = _dot(zs[...], wout_ref[...])
    o_ref[...] = xc_ref[...] + gate1 * _rms(o, ng_ref[1:2, :])
    yield


def _even_layer(x, modv, layer, cond_base, cond_stride, ng, win, caw, cab, lng, lnb, cbw, wout):
    j = layer // 2
    b, t, _ = x.shape
    n_chunks = t // ROW_CHUNK
    nsub = EVEN_CHUNKS
    assert (b * n_chunks) % nsub == 0 and (cond_stride == 0 or n_chunks % nsub == 0)
    hpc = ROW_CHUNK // CONF_HALO
    n_halo_blocks = b * t // CONF_HALO
    rows = ROW_CHUNK + 2 * CONF_HALO
    halo_specs = []
    for k in range(nsub):
        halo_specs += [
            pl.BlockSpec((None, CONF_HALO, D_MODEL),
                         lambda i, k=k: (jnp.maximum((i * nsub + k) * hpc - 1, 0), 0, 0)),
            pl.BlockSpec((None, CONF_HALO, D_MODEL),
                         lambda i, k=k: (jnp.minimum((i * nsub + k + 1) * hpc, n_halo_blocks - 1), 0, 0)),
        ]
    x_halo = x.reshape(n_halo_blocks, CONF_HALO, D_MODEL)
    x_chunks = x.reshape(b * n_chunks, ROW_CHUNK, D_MODEL)
    kern = functools.partial(_even_kernel, n_chunks=n_chunks, nsub=nsub)
    out = pl.pallas_call(
        kern,
        grid=(b * n_chunks // nsub,),
        in_specs=halo_specs + [
            pl.BlockSpec((nsub, ROW_CHUNK, D_MODEL), lambda i: (i, 0, 0)),
            _mod_spec(layer, lambda i: cond_base + cond_stride * ((i * nsub) // n_chunks)),
            _layer_spec((4, D_MODEL), layer),
            _layer_spec((D_MODEL, 5 * D_HALF), j),
            _layer_spec((CONF_WIDTH, D_HALF), j),
            _layer_spec((1, D_HALF), j),
            _layer_spec((1, D_HALF), j),
            _layer_spec((1, D_HALF), j),
            _layer_spec((3, D_HALF), j),
            _layer_spec((D_MODEL, D_MODEL), j),
        ],
        out_specs=pl.BlockSpec((nsub, ROW_CHUNK, D_MODEL), lambda i: (i, 0, 0)),
        out_shape=jax.ShapeDtypeStruct(x_chunks.shape, F32),
        scratch_shapes=[
            pltpu.VMEM((nsub, rows, D_HALF), F32),
            pltpu.VMEM((nsub, rows, D_HALF), F32),
            pltpu.VMEM((nsub, ROW_CHUNK, D_HALF), F32),
            pltpu.VMEM((nsub, ROW_CHUNK, D_MODEL), BF16),
        ],
        compiler_params=_params(1),
        name="even_mixer",
    )(*([x_halo] * (2 * nsub)), x_chunks, modv, ng, win, caw, cab, lng, lnb, cbw, wout)
    return out.reshape(b, t, D_MODEL)


def _mlp_kernel(xp_ref, xs_ref, mod_ref, ng_ref, w1_hbm, w2_hbm, op_ref, os_ref, w1_s, w2_s, stage, sem, *,
                layer, n_ctx_tiles):
    n_blocks = D_FF // D_MODEL

    def block_copy(k):
        c, slot = k // 2, k % 2
        span = pl.ds(c * D_MODEL, D_MODEL)
        src = w1_hbm.at[layer, :, span] if k % 2 == 0 else w2_hbm.at[layer, span, :]
        return pltpu.make_async_copy(src, stage.at[slot], sem.at[slot])

    def fetch(k, dst):
        block_copy(k).wait()
        dst[...] = stage[k % 2].astype(BF16)
        if k + 2 < 2 * n_blocks:
            block_copy(k + 2).start()

    def rows_program(x_ref, o_ref, rows, load):
        x = x_ref[rows, :]
        h = _norm_mod(x, ng_ref[2:3, :], mod_ref[3:4, :], mod_ref[4:5, :]).astype(BF16)
        yield
        acc = None
        for c in range(n_blocks):
            cols = slice(c * D_MODEL, (c + 1) * D_MODEL)
            if load:
                fetch(2 * c, w1_s.at[:, cols])
            hid = jnp.square(jnp.maximum(_dot(h, w1_s[:, cols]), 0.0)).astype(BF16)
            yield
            if load:
                fetch(2 * c + 1, w2_s.at[cols, :])
            part = _dot(hid, w2_s[cols, :])
            acc = part if acc is None else acc + part
            yield
        o_ref[rows, :] = x + mod_ref[5:6, :] * _rms(acc, ng_ref[3:4, :])
        yield

    def tile(x_ref, o_ref, load=False):
        n_rows = x_ref.shape[0]
        if load:
            _drain(rows_program(x_ref, o_ref, slice(0, n_rows), True))
        else:
            _run_staggered((rows_program(x_ref, o_ref, slice(r0, r0 + n_rows // 2), False)
                            for r0 in (0, n_rows // 2)), MLP_STAGGER)

    i = pl.program_id(0)

    @pl.when(i == 0)
    def _():
        block_copy(0).start()
        block_copy(1).start()
        tile(xp_ref, op_ref, load=True)

    @pl.when((i > 0) & (i < n_ctx_tiles))
    def _():
        tile(xp_ref, op_ref)

    @pl.when(i >= n_ctx_tiles)
    def _():
        tile(xs_ref, os_ref)


def _mlp_layer(xp, xs, modv, layer, ng, w1, w2):
    tm = MLP_ROWS
    xp2, xs2 = xp.reshape(-1, D_MODEL), xs.reshape(-1, D_MODEL)
    t_dec = xs.shape[1]
    assert xp2.shape[0] % tm == 0 and t_dec % tm == 0
    n_p, n_s, per_seq = xp2.shape[0] // tm, xs2.shape[0] // tm, t_dec // tm
    ctx_tile = lambda i: (jnp.minimum(i, n_p - 1), 0)
    lat_tile = lambda i: (jnp.maximum(i - n_p, 0), 0)
    yp, ys = pl.pallas_call(
        functools.partial(_mlp_kernel, layer=layer, n_ctx_tiles=n_p),
        grid=(n_p + n_s,),
        in_specs=[
            pl.BlockSpec((tm, D_MODEL), ctx_tile),
            pl.BlockSpec((tm, D_MODEL), lat_tile),
            _mod_spec(layer, lambda i: jnp.where(i < n_p, 0, 1 + jnp.maximum(i - n_p, 0) // per_seq)),
            _layer_spec((4, D_MODEL), layer),
            pl.BlockSpec(memory_space=pl.ANY),
            pl.BlockSpec(memory_space=pl.ANY),
        ],
        out_specs=[pl.BlockSpec((tm, D_MODEL), ctx_tile), pl.BlockSpec((tm, D_MODEL), lat_tile)],
        out_shape=[jax.ShapeDtypeStruct(xp2.shape, F32), jax.ShapeDtypeStruct(xs2.shape, F32)],
        scratch_shapes=[
            pltpu.VMEM((D_MODEL, D_FF), BF16),
            pltpu.VMEM((D_FF, D_MODEL), BF16),
            pltpu.VMEM((2, D_MODEL, D_MODEL), F32),
            pltpu.SemaphoreType.DMA((2,)),
        ],
        compiler_params=_params(1),
        name="mlp",
    )(xp2, xs2, modv, ng, w1, w2)
    return yp.reshape(xp.shape), ys.reshape(xs.shape)


def _log_sigmoid(x):
    return jnp.minimum(x, 0.0) - jnp.log(1.0 + jnp.exp(-jnp.abs(x)))


def _rot_half(x, first_half):
    return jnp.where(first_half, pltpu.roll(x, 96, axis=1), pltpu.roll(x, 32, axis=1))


def _values_and_ones(v):
    vt = jnp.transpose(v)
    row = lax.broadcasted_iota(jnp.int32, vt.shape, 0)
    return jnp.where(row < HEAD_DIM, vt, 1.0)


def _aligned(i, m):
    return i * m if isinstance(i, int) else pl.multiple_of(i * m, m)


def _drain(pieces):
    for _ in pieces:
        pass


def _each(body, n, static, unroll=1):
    if static:
        for i in range(n):
            yield from body(i)
    else:
        lax.fori_loop(0, n, lambda i, c: (_drain(body(i)), c)[1], 0, unroll=unroll)


def _alternate(*programs):
    live = list(programs)
    while live:
        for g in list(live):
            try:
                next(g)
            except StopIteration:
                live.remove(g)
            else:
                yield


def _odd_kernel(*refs, t, lc, latent, nseq):
    shared = set(range(1, 11 if latent else 9))
    n_rows = min(t, ODD_ROWS)
    joint = not latent and nseq > 1 and t == n_rows
    finals = []
    programs = [_odd_seq(*[r if k in shared else r.at[sq] for k, r in enumerate(refs)],
                         t=t, lc=lc, latent=latent, joint=joint, finals=finals) for sq in range(nseq)]
    if joint:
        parts = [next(g) for g in programs]
        h_all = jnp.concatenate([proj_h(0) for proj_h, _ in parts], axis=0)
        for idx, (weight, _) in enumerate(parts[0][1]):
            y = _dot(h_all, weight())
            for sq, (_, plan) in enumerate(parts):
                plan[idx][1](y[sq * n_rows:(sq + 1) * n_rows], 0)
    _run_staggered(programs, ODD_STAGGER)
    if joint:
        o = _dot(jnp.concatenate([z() for z, _ in finals], axis=0), refs[8][...])
        for sq, (_, residual) in enumerate(finals):
            residual(o[sq * n_rows:(sq + 1) * n_rows])


def _odd_seq(*refs, t, lc, latent, joint, finals):
    if latent:
        (x_ref, mod_ref, ng_ref, win_ref, wg_ref, gb_ref, sink_ref, hn_ref, wout_ref,
         cos_ref, sin_ref, kc_ref, vc_ref, cin_ref, nin_ref, min_ref,
         o_ref,
         qa_s, qb_s, kk_s, vvt_s, qma_s, qmb_s, qm_s, km_s, vmt_s, vmtf_s, om_s, g_s, z_s, hft_s, hbt_s,
         sr_s, st_s, qc_s, c_s, n_s, m_s, s_s, p_s, kc_s, vct_s) = refs
    else:
        (x_ref, mod_ref, ng_ref, win_ref, wg_ref, gb_ref, sink_ref, hn_ref, wout_ref,
         o_ref, ko_ref, vo_ref, cf_ref, cb_ref, no_ref, mo_ref,
         qa_s, qb_s, kk_s, vvt_s, qma_s, qmb_s, qm_s, km_s, vmt_s, vmtf_s, om_s, g_s, z_s, hft_s, hbt_s,
         sr_s, st_s, qc_s, c_s, n_s, m_s, s_s, p_s) = refs

    static = not latent
    n_blocks = t // CHUNK
    pad = CHUNK if latent else 0
    shift1, scale1, gate1 = mod_ref[0:1, :], mod_ref[1:2, :], mod_ref[2:3, :]
    lane = lax.broadcasted_iota(jnp.int32, (1, LANES), 1)
    left = lane < HEAD_DIM
    first_half = (lane % HEAD_DIM) < (HEAD_DIM // 2)
    ti = lax.broadcasted_iota(jnp.int32, (CHUNK, CHUNK), 0)
    si = lax.broadcasted_iota(jnp.int32, (CHUNK, CHUNK), 1)
    top = ti < HEAD_DIM
    same_head = top == (si < HEAD_DIM)

    n_rows = min(t, ODD_ROWS)
    if latent:
        for kv in range(N_KV):
            kk_s[kv, 0:CHUNK, :] = jnp.zeros((CHUNK, LANES), BF16)
            kk_s[kv, CHUNK + t:2 * CHUNK + t, :] = jnp.zeros((CHUNK, LANES), BF16)
            vvt_s[kv, :, 0:CHUNK] = jnp.zeros((LANES, CHUNK), BF16)
            vvt_s[kv, :, CHUNK + t:2 * CHUNK + t] = jnp.zeros((LANES, CHUNK), BF16)
            kc_s[kv] = kc_ref[kv].astype(BF16)
            vct_s[kv] = _values_and_ones(vc_ref[kv]).astype(BF16)

    base = D_HALF + 2 * LANES

    def proj_rows(rc):
        r0 = _aligned(rc, n_rows)
        return pl.ds(r0, n_rows), pl.ds(r0 + pad, n_rows)

    def proj_h(rc):
        rows, _ = proj_rows(rc)
        return _norm_mod(x_ref[rows, :], ng_ref[0:1, :], shift1, scale1).astype(BF16)

    def rope(v, rows):
        return v * cos_ref[rows, :] + _rot_half(v, first_half) * sin_ref[rows, :]

    def use_q(q, rc):
        rows, _ = proj_rows(rc)
        for p in range(N_PAIRS):
            cols = slice(p * LANES, (p + 1) * LANES)
            qp = q[:, cols]
            if latent:
                qp = rope(qp, rows)
            qp = qp * ATT_SCALE
            qa_s[rows, cols] = jnp.where(left, qp, 0.0).astype(BF16)
            qb_s[rows, cols] = jnp.where(left, 0.0, qp).astype(BF16)

    def use_kv(kv2, rc):
        rows, krows = proj_rows(rc)
        ka, va = kv2[:, 0:LANES], kv2[:, LANES:2 * LANES]
        if latent:
            ka = rope(ka, rows)
        kr = pltpu.roll(ka, HEAD_DIM, axis=1)
        vr = pltpu.roll(va, HEAD_DIM, axis=1)
        if not latent:
            ka_t, va_t = jnp.transpose(ka), jnp.transpose(va)
            for kv in range(N_KV):
                ko_ref[0, kv, :, rows] = ka_t[kv * HEAD_DIM:(kv + 1) * HEAD_DIM, :]
                vo_ref[0, kv, :, rows] = va_t[kv * HEAD_DIM:(kv + 1) * HEAD_DIM, :]
        kk_s[0, krows, :] = jnp.where(left, ka, kr).astype(BF16)
        kk_s[1, krows, :] = jnp.where(left, kr, ka).astype(BF16)
        vvt_s[0, :, krows] = _values_and_ones(va).astype(BF16)
        vvt_s[1, :, krows] = _values_and_ones(vr).astype(BF16)

    def use_qm(qm, rc):
        rows, _ = proj_rows(rc)
        qm_s[rows, :] = qm.astype(BF16)
        for p in range(N_PAIRS):
            cols = slice(p * LANES, (p + 1) * LANES)
            qma_s[rows, cols] = jnp.where(left, qm[:, cols], 0.0).astype(BF16)
            qmb_s[rows, cols] = jnp.where(left, 0.0, qm[:, cols]).astype(BF16)

    def use_vm(vm, rc):
        rows, _ = proj_rows(rc)
        for p in range(N_PAIRS):
            cols = slice(p * LANES, (p + 1) * LANES)
            vt = jnp.transpose(vm[:, cols])
            vmtf_s[cols, rows] = vt
            vmt_s[cols, rows] = vt.astype(BF16)

    def use_km(km, rc):
        km_s[proj_rows(rc)[0], :] = (km * (HEAD_DIM ** -0.5)).astype(BF16)

    def use_om(om, rc):
        om_s[proj_rows(rc)[0], :] = om

    def use_g(g, rc):
        g_s[proj_rows(rc)[0], :] = g + gb_ref[...]

    def columns(start, width):
        return lambda: win_ref[:, start:start + width]

    proj_plan = [(columns(0, D_HALF), use_q), (columns(D_HALF, 2 * LANES), use_kv),
                 (columns(base, D_HALF), use_qm), (columns(base + 2 * D_HALF, D_HALF), use_vm),
                 (columns(base + D_HALF, D_HALF), use_km), (columns(base + 3 * D_HALF, D_HALF), use_om),
                 (lambda: wg_ref[...], use_g)]

    def project(rc):
        h = proj_h(rc)
        for weight, use in proj_plan:
            use(_dot(h, weight()), rc)
            yield

    if joint:
        yield proj_h, proj_plan
    else:
        yield from _each(project, t // n_rows, static)

    kj = lax.broadcasted_iota(jnp.int32, (3 * CHUNK, 2 * CHUNK), 0)
    qi = lax.broadcasted_iota(jnp.int32, (3 * CHUNK, 2 * CHUNK), 1) % CHUNK
    band_ok = jnp.abs(kj - CHUNK - qi) <= WINDOW
    head_a = lax.broadcasted_iota(jnp.int32, (1, 2 * CHUNK), 1) < CHUNK

    n_keys = _att_keys(lc, latent)
    group = _att_group(latent)

    def attention():
        def attend(i):
            r0 = _aligned(i, CHUNK)
            rows = pl.ds(r0, CHUNK)
            if latent:
                key_pos = kj + (i - 1) * CHUNK
                mask = band_ok & (key_pos >= 0) & (key_pos < t)
                win = pl.ds(r0, 3 * CHUNK)
            for g0 in range(0, N_PAIRS, group):
                for gi in range(group):
                    p = g0 + gi
                    kv = p // (N_PAIRS // N_KV)
                    cols = slice(p * LANES, (p + 1) * LANES)
                    q2 = jnp.concatenate([qa_s[rows, cols], qb_s[rows, cols]], axis=0)
                    if latent:
                        s_s[gi, 0:lc, :] = _dot_nt(kc_s[kv], q2)
                        s_s[gi, lc:n_keys, :] = jnp.where(mask, _dot_nt(kk_s[kv, win, :], q2), -jnp.inf)
                    else:
                        s_s[gi] = _dot_nt(kk_s[kv], q2)
                yield
                maxes = []
                for gi in range(group):
                    p = g0 + gi
                    sink = jnp.where(head_a, sink_ref[2 * p:2 * p + 1, 0:1], sink_ref[2 * p + 1:2 * p + 2, 0:1])
                    mx = jnp.maximum(jnp.max(s_s[gi], axis=0, keepdims=True), sink)
                    p_s[gi] = jnp.exp(s_s[gi] - mx).astype(BF16)
                    maxes.append((sink, mx))
                yield
                for gi in range(group):
                    p = g0 + gi
                    kv = p // (N_PAIRS // N_KV)
                    cols = slice(p * LANES, (p + 1) * LANES)
                    if latent:
                        num = (_dot(vct_s[kv], p_s[gi, 0:lc, :]) + _dot(vvt_s[kv, :, win], p_s[gi, lc:n_keys, :]))
                    else:
                        num = _dot(vvt_s[kv], p_s[gi])
                    sink, mx = maxes[gi]
                    den = num[HEAD_DIM:HEAD_DIM + 1, :] + jnp.exp(sink - mx)
                    out = num[0:HEAD_DIM, :] * (1.0 / den)
                    pair = jnp.concatenate([out[:, 0:CHUNK], out[:, CHUNK:2 * CHUNK]], axis=0)
                    z_s[rows, cols] = jnp.transpose(pair).astype(BF16)
                yield

        return attend

    if latent:
        c_s[...] = cin_ref[...]
        n_s[...] = nin_ref[...]
        m_s[...] = min_ref[...]
    else:
        c_s[...] = jnp.zeros(c_s.shape, F32)
        n_s[...] = jnp.zeros(n_s.shape, F32)
        m_s[...] = jnp.zeros(m_s.shape, F32)

    see = (ti <= si, ti >= si)
    tri = tuple(m.astype(F32).astype(BF16) for m in see)
    last = (CHUNK - 1, 0)

    def mlstm(i):
        offs = (_aligned(i, CHUNK), _aligned(n_blocks - 1 - i, CHUNK))
        half = (slice(0, LANES), slice(LANES, 2 * LANES))
        a_rows, b_rows = [], []
        for d in range(2):
            gt = jnp.transpose(g_s[pl.ds(offs[d], CHUNK), :])
            lf = _log_sigmoid(gt[2 * N_HEADS:4 * N_HEADS, :])
            bcum = _dot_exact_lhs(lf, tri[d])[8 * d:8 * d + 8, :]
            a_rows.append(gt[8 * d:8 * d + 8, :] - bcum)
            b_rows.append(bcum)
        yield
        q_ns = []
        for d in range(2):
            rows = pl.ds(offs[d], CHUNK)
            for p in range(N_PAIRS):
                u = d * N_PAIRS + p
                cols = slice(p * LANES, (p + 1) * LANES)
                q2 = jnp.concatenate([qma_s[rows, cols], qmb_s[rows, cols]], axis=0)
                sr_s[u] = _dot_nt(km_s[rows, cols], q2)
                qc_s[u] = _dot_nt(c_s[u].astype(BF16), qm_s[rows, cols])
                n8 = jnp.broadcast_to(n_s[u:u + 1, :], (8, LANES)).astype(BF16)
                q_ns.append(_dot_nt(n8, q2)[0:1, :])
            yield
        stats = {}
        for d in range(2):
            for hd in range(N_HEADS):
                u, j = d * N_PAIRS + hd // 2, hd % 2
                mrow = d * N_HEADS + hd
                a_row = a_rows[d][hd:hd + 1, :]
                b_row = b_rows[d][hd:hd + 1, :]
                m_prev = m_s[mrow:mrow + 1, 0:1]
                a_col = jnp.transpose(jnp.broadcast_to(a_row, (CHUNK, CHUNK)))
                z_t = jnp.where(see[d], a_col, -jnp.inf)
                m_run = jnp.maximum(jnp.max(z_t, axis=0, keepdims=True), m_prev)
                s_t = sr_s[u, :, half[j]] * jnp.exp(z_t - m_run)
                st_s[u, :, half[j]] = s_t.astype(BF16)
                w_int = jnp.exp(m_prev - m_run)
                den = jnp.sum(s_t, axis=0, keepdims=True) + w_int * q_ns[u][:, half[j]]
                inv = 1.0 / jnp.maximum(jnp.abs(den), jnp.exp(-(b_row + m_run)))
                m_last = m_run[:, last[d]:last[d] + 1]
                stats[(u, j)] = (w_int, inv, jnp.exp(a_row - m_last), jnp.exp(m_prev - m_last))
                m_s[mrow:mrow + 1, :] = jnp.broadcast_to(b_row[:, last[d]:last[d] + 1] + m_last, (1, LANES))
            yield
        for d in range(2):
            rows = pl.ds(offs[d], CHUNK)
            ht_s = (hft_s, hbt_s)[d]
            for p in range(N_PAIRS):
                u = d * N_PAIRS + p
                cols = slice(p * LANES, (p + 1) * LANES)
                (w_a, inv_a, e_a, dec_a), (w_b, inv_b, e_b, dec_b) = stats[(u, 0)], stats[(u, 1)]
                kp = km_s[rows, cols]
                num2 = _dot(vmt_s[cols, rows], st_s[u])
                num = jnp.where(top, num2[:, half[0]], num2[:, half[1]])
                ht_s[cols, rows] = (num + jnp.where(top, w_a, w_b) * qc_s[u]) * jnp.where(top, inv_a, inv_b)
                vt_e = (vmtf_s[cols, rows] * jnp.where(top, e_a, e_b)).astype(BF16)
                c_s[u] = jnp.where(top, dec_a, dec_b) * c_s[u] + jnp.where(same_head, _dot(vt_e, kp), 0.0)
                e2 = jnp.concatenate([e_a, e_b, jnp.zeros((6, CHUNK), F32)], axis=0).astype(BF16)
                n_k = _dot(e2, kp)
                n_s[u:u + 1, :] = (jnp.where(left, dec_a, dec_b) * n_s[u:u + 1, :]
                                   + jnp.where(left, n_k[0:1, :], n_k[1:2, :]))
            yield

    attend = attention()
    yield from _each(lambda i: _alternate(attend(i), mlstm(i)), n_blocks, static, unroll=2)

    if not latent:
        for d, c_ref in enumerate((cf_ref, cb_ref)):
            for p in range(N_PAIRS):
                c_pair = jnp.transpose(c_s[d * N_PAIRS + p])
                c_ref[0, 2 * p] = c_pair[0:HEAD_DIM, 0:HEAD_DIM]
                c_ref[0, 2 * p + 1] = c_pair[HEAD_DIM:LANES, HEAD_DIM:LANES]
        no_ref[...] = n_s[...]
        mo_ref[...] = m_s[...]
        yield

    top_w = lax.broadcasted_iota(jnp.int32, (LANES, n_rows), 0) < HEAD_DIM

    def finish(rc):
        r0 = _aligned(rc, n_rows)
        rows = pl.ds(r0, n_rows)
        for p in range(N_PAIRS):
            cols = slice(p * LANES, (p + 1) * LANES)
            hm = hft_s[cols, rows] + hbt_s[cols, rows]
            sq = hm * hm
            ms_a = jnp.sum(sq[0:HEAD_DIM], axis=0, keepdims=True)
            ms_b = jnp.sum(sq[HEAD_DIM:LANES], axis=0, keepdims=True)
            ms = jnp.where(top_w, ms_a, ms_b) * (1.0 / HEAD_DIM)
            y = jnp.transpose(hm * lax.rsqrt(ms + EPS)) * hn_ref[:, cols] * jax.nn.sigmoid(om_s[rows, cols])
            z_s[rows, D_HALF + p * LANES:D_HALF + (p + 1) * LANES] = y.astype(BF16)
            if p % 2:
                yield
        def residual(o):
            o_ref[rows, :] = x_ref[rows, :] + gate1 * _rms(o, ng_ref[1:2, :])

        if joint:
            finals.append((lambda: z_s[rows, :], residual))
        else:
            residual(_dot(z_s[rows, :], wout_ref[...]))
        yield

    yield from _each(finish, t // n_rows, static)


def _att_keys(lc, latent):
    return lc + 3 * CHUNK if latent else lc


def _att_group(latent):
    return 2 if latent else N_PAIRS


def _odd_scratch(t, lc, latent, nseq):
    pad = 2 * CHUNK if latent else 0
    att = (_att_group(latent), _att_keys(lc, latent), 2 * CHUNK)
    shapes = [
        pltpu.VMEM((t, D_HALF), BF16),
        pltpu.VMEM((t, D_HALF), BF16),
        pltpu.VMEM((N_KV, t + pad, LANES), BF16),
        pltpu.VMEM((N_KV, LANES, t + pad), BF16),
        pltpu.VMEM((t, D_HALF), BF16),
        pltpu.VMEM((t, D_HALF), BF16),
        pltpu.VMEM((t, D_HALF), BF16),
        pltpu.VMEM((t, D_HALF), BF16),
        pltpu.VMEM((D_HALF, t), BF16),
        pltpu.VMEM((D_HALF, t), F32),
        pltpu.VMEM((t, D_HALF), F32),
        pltpu.VMEM((t, LANES), F32),
        pltpu.VMEM((t, D_MODEL), BF16),
        pltpu.VMEM((D_HALF, t), F32),
        pltpu.VMEM((D_HALF, t), F32),
        pltpu.VMEM((2 * N_PAIRS, CHUNK, 2 * LANES), F32),
        pltpu.VMEM((2 * N_PAIRS, CHUNK, 2 * LANES), BF16),
        pltpu.VMEM((2 * N_PAIRS, LANES, CHUNK), F32),
        pltpu.VMEM((2 * N_PAIRS, LANES, LANES), F32),
        pltpu.VMEM((2 * N_PAIRS, LANES), F32),
        pltpu.VMEM((2 * N_HEADS, LANES), F32),
        pltpu.VMEM(att, F32),
        pltpu.VMEM(att, BF16),
    ]
    if latent:
        shapes += [pltpu.VMEM((N_KV, lc, LANES), BF16), pltpu.VMEM((N_KV, LANES, lc), BF16)]
    return [pltpu.VMEM((nseq,) + tuple(sh.shape), sh.dtype) for sh in shapes]


def _odd_common_specs(t, layer, cond_base, cond_stride, nseq):
    assert cond_stride == 0 or nseq == 1
    j = layer // 2
    return [
        _per_seq((t, D_MODEL), nseq),
        _mod_spec(layer, lambda i: cond_base + cond_stride * i),
        _layer_spec((4, D_MODEL), layer),
        _layer_spec((D_MODEL, D_IN_ODD), j),
        _const_spec((D_MODEL, LANES)),
        _const_spec((1, LANES)),
        _const_spec((N_HEADS, LANES)),
        _layer_spec((1, D_HALF), j),
        _layer_spec((D_MODEL, D_MODEL), j),
    ]


def _per_seq(shape, nseq):
    return pl.BlockSpec((nseq,) + shape, lambda i: (i,) + (0,) * len(shape))


def _odd_context(x, modv, layer, ng, w_main, w_gate, gate_bias, sink_b, hnorm, wout):
    b, t, _ = x.shape
    nseq = ODD_CTX_SEQS
    assert b % nseq == 0
    kern = functools.partial(_odd_kernel, t=t, lc=t, latent=False, nseq=nseq)
    per_seq = functools.partial(_per_seq, nseq=nseq)
    return pl.pallas_call(
        kern,
        grid=(b // nseq,),
        in_specs=_odd_common_specs(t, layer, 0, 0, nseq),
        out_specs=[per_seq((t, D_MODEL)),
                   per_seq((1, N_KV, HEAD_DIM, t)), per_seq((1, N_KV, HEAD_DIM, t)),
                   per_seq((1, N_HEADS, HEAD_DIM, HEAD_DIM)), per_seq((1, N_HEADS, HEAD_DIM, HEAD_DIM)),
                   per_seq((2 * N_PAIRS, LANES)), per_seq((2 * N_HEADS, LANES))],
        out_shape=[jax.ShapeDtypeStruct((b, t, D_MODEL), F32),
                   jax.ShapeDtypeStruct((b, 1, N_KV, HEAD_DIM, t), F32),
                   jax.ShapeDtypeStruct((b, 1, N_KV, HEAD_DIM, t), F32),
                   jax.ShapeDtypeStruct((b, 1, N_HEADS, HEAD_DIM, HEAD_DIM), F32),
                   jax.ShapeDtypeStruct((b, 1, N_HEADS, HEAD_DIM, HEAD_DIM), F32),
                   jax.ShapeDtypeStruct((b, 2 * N_PAIRS, LANES), F32),
                   jax.ShapeDtypeStruct((b, 2 * N_HEADS, LANES), F32)],
        scratch_shapes=_odd_scratch(t, t, False, nseq),
        compiler_params=_params(1),
        name="odd_mixer_context",
    )(x, modv, ng, w_main, w_gate, gate_bias, sink_b, hnorm, wout)


def _odd_latent(x, modv, layer, ng, w_main, w_gate, gate_bias, sink_b, hnorm, wout, cos_t, sin_t, kc, vc,
                c_in, n_in, m_in):
    b, t, _ = x.shape
    lc = kc.shape[2]
    kern = functools.partial(_odd_kernel, t=t, lc=lc, latent=True, nseq=1)
    per_seq = functools.partial(_per_seq, nseq=1)
    return pl.pallas_call(
        kern,
        grid=(b,),
        in_specs=_odd_common_specs(t, layer, 1, 1, 1) + [
            _const_spec((t, LANES)), _const_spec((t, LANES)),
            per_seq((N_KV, lc, LANES)), per_seq((N_KV, lc, LANES)),
            per_seq((2 * N_PAIRS, LANES, LANES)), per_seq((2 * N_PAIRS, LANES)), per_seq((2 * N_HEADS, LANES)),
        ],
        out_specs=per_seq((t, D_MODEL)),
        out_shape=jax.ShapeDtypeStruct((b, t, D_MODEL), F32),
        scratch_shapes=_odd_scratch(t, lc, True, 1),
        compiler_params=_params(1),
        name="odd_mixer_latent",
    )(x, modv, ng, w_main, w_gate, gate_bias, sink_b, hnorm, wout, cos_t, sin_t, kc, vc, c_in, n_in, m_in)


def _rope_tables(t):
    rows = t // GRID_W
    row = jnp.broadcast_to(jnp.arange(rows)[:, None], (rows, GRID_W)).reshape(t).astype(F32)
    col = jnp.broadcast_to(jnp.arange(GRID_W)[None, :], (rows, GRID_W)).reshape(t).astype(F32)
    n_freq = HEAD_DIM // 4
    inv_freq = ROPE_BASE ** (-jnp.arange(n_freq, dtype=F32) / n_freq)
    ang = jnp.concatenate([row[:, None] * inv_freq, col[:, None] * inv_freq], axis=-1)
    cos, sin = jnp.cos(ang), jnp.sin(ang)
    cos_l = jnp.tile(cos, (1, LANES // cos.shape[1]))
    sin_l = jnp.tile(jnp.concatenate([-sin, sin], axis=-1), (1, LANES // HEAD_DIM))
    return cos_l, sin_l


def _pair_blockdiag(c):
    b = c.shape[0]
    c = c.reshape(b, N_PAIRS, 2, HEAD_DIM, HEAD_DIM)
    z = jnp.zeros_like(c[:, :, 0])
    top = jnp.concatenate([c[:, :, 0], z], axis=-1)
    bot = jnp.concatenate([z, c[:, :, 1]], axis=-1)
    return jnp.concatenate([top, bot], axis=-2)


def _lane_bcast(v):
    return jnp.broadcast_to(v[..., None], v.shape + (LANES,))


def kernel(x_prompt, x_sample, c, cache_k, cache_v, state_c_fwd, state_n_fwd, state_m_fwd, state_c_bwd, state_n_bwd, state_m_bwd, c_ctx, mod_w, mod_b, norm_g, mlp_w1, mlp_w2, even_in_w, conv_a_w, conv_a_b, ln_a_g, ln_a_b, conv_b_w, even_out_w, odd_in_w, attn_sink, gate_b, hnorm_g, odd_out_w):
    n_dec = x_sample.shape[0]
    n_ctx = x_prompt.shape[0]
    cond = jnp.concatenate([c_ctx[None, :], c, jnp.zeros((COND_ROWS - 1 - n_dec, D_MODEL), F32)], axis=0)
    modv = _modulation(cond, mod_w, mod_b)

    yp, ys = x_prompt, x_sample
    w1, w2 = mlp_w1, mlp_w2

    ev = (norm_g, even_in_w.astype(BF16), conv_a_w, conv_a_b[:, None, :], ln_a_g[:, None, :], ln_a_b[:, None, :],
          conv_b_w, even_out_w.astype(BF16))
    yp = _even_layer(yp, modv, 0, 0, 0, *ev)
    ys = _even_layer(ys, modv, 0, 1, 1, *ev)
    yp, ys = _mlp_layer(yp, ys, modv, 0, norm_g, w1, w2)

    order = jnp.array([0, 2, 1, 3])
    d_main = D_IN_ODD - 4 * N_HEADS
    wg = odd_in_w[0][:, d_main:].reshape(D_MODEL, 4, N_HEADS)[:, order, :].reshape(D_MODEL, 4 * N_HEADS)
    w_gate = jnp.pad(wg, ((0, 0), (0, LANES - 4 * N_HEADS))).astype(BF16)
    gate_bias = jnp.pad(gate_b[0][order, :].reshape(1, 4 * N_HEADS), ((0, 0), (0, LANES - 4 * N_HEADS)))
    sink_b = _lane_bcast(attn_sink[0])
    odd = (1, norm_g, odd_in_w.astype(BF16), w_gate, gate_bias, sink_b, hnorm_g[:, None, :], odd_out_w.astype(BF16))

    op, k_t, v_t, c_f, c_b, n_new, m_new = _odd_context(yp, modv, *odd)
    new_k, new_v = jnp.swapaxes(k_t, -1, -2), jnp.swapaxes(v_t, -1, -2)

    t_dec = x_sample.shape[1]
    cos_t, sin_t = _rope_tables(t_dec)
    kc = jnp.concatenate([cache_k[:, 0], cache_k[:, 0]], axis=-1)
    vc = jnp.concatenate([cache_v[:, 0], cache_v[:, 0]], axis=-1)
    c_in = jnp.concatenate([_pair_blockdiag(jnp.swapaxes(state_c_fwd[:, 0], -1, -2)),
                            _pair_blockdiag(jnp.swapaxes(state_c_bwd[:, 0], -1, -2))], axis=1)
    n_in = jnp.concatenate([state_n_fwd[:, 0].reshape(n_dec, N_PAIRS, LANES),
                            state_n_bwd[:, 0].reshape(n_dec, N_PAIRS, LANES)], axis=1)
    m_in = _lane_bcast(jnp.concatenate([state_m_fwd[:, 0], state_m_bwd[:, 0]], axis=1))
    os_ = _odd_latent(ys, modv, *odd, cos_t, sin_t, kc, vc, c_in, n_in, m_in)

    yp, ys = _mlp_layer(op, os_, modv, 1, norm_g, w1, w2)

    n_f = n_new[:, :N_PAIRS].reshape(n_ctx, N_HEADS, HEAD_DIM)[:, None]
    n_b = n_new[:, N_PAIRS:].reshape(n_ctx, N_HEADS, HEAD_DIM)[:, None]
    m_f = m_new[:, :N_HEADS, 0][:, None]
    m_b = m_new[:, N_HEADS:, 0][:, None]
    return (yp, ys, new_k, new_v, c_f, n_f, m_f, c_b, n_b, m_b)
```

```python
import functools

import jax
import jax.numpy as jnp
from jax import lax
from jax.experimental import pallas as pl
from jax.experimental.pallas import tpu as pltpu

F32 = jnp.float32
BF16 = jnp.bfloat16

D_MODEL = 1024
D_FF = 4 * D_MODEL
EPS = 1e-6
D_HALF = D_MODEL // 2
CONF_WIDTH = 31
CONF_HALO = 16
HEAD_DIM = 64
N_HEADS = 8
N_PAIRS = N_HEADS // 2
N_KV = 2
LANES = 128
CHUNK = 128
WINDOW = 128
GRID_W = 64
ROPE_BASE = 10000.0
ATT_SCALE = HEAD_DIM ** -0.5
D_IN_ODD = D_HALF + 2 * N_KV * HEAD_DIM + 4 * D_HALF + 4 * N_HEADS
ROW_CHUNK = 256
MLP_ROWS = 512
MLP_STAGGER = 3
ODD_ROWS = 512
ODD_CTX_SEQS = 2
ODD_STAGGER = 6
EVEN_CONV_ROWS = 64
EVEN_CHUNKS = 2
EVEN_STAGGER = 3
COND_ROWS = 8
VMEM_LIMIT = 56 * 1024 * 1024


def _dot(a, b):
    return jnp.dot(a, b, preferred_element_type=F32)


def _dot_nt(a, b):
    return lax.dot_general(a, b, (((1,), (1,)), ((), ())), preferred_element_type=F32)


def _split3(x):
    hi = x.astype(BF16)
    r1 = x - hi.astype(F32)
    mid = r1.astype(BF16)
    lo = (r1 - mid.astype(F32)).astype(BF16)
    return hi, mid, lo


def _dot_exact_lhs(x, b01):
    hi, mid, lo = _split3(x)
    return _dot(hi, b01) + _dot(mid, b01) + _dot(lo, b01)


def _rms(x, g):
    return x * lax.rsqrt(jnp.mean(x * x, axis=-1, keepdims=True) + EPS) * g


def _norm_mod(x, g, shift, scale):
    return _rms(x, g) * (1.0 + scale) + shift


def _params(n_grid):
    return pltpu.CompilerParams(dimension_semantics=("arbitrary",) * n_grid, vmem_limit_bytes=VMEM_LIMIT)


def _const_spec(shape):
    zeros = (0,) * len(shape)
    return pl.BlockSpec(shape, lambda *_: zeros, pipeline_mode=pl.Buffered(1))


def _layer_spec(shape, layer):
    index = (layer,) + (0,) * len(shape)
    return pl.BlockSpec((None,) + shape, lambda *_: index, pipeline_mode=pl.Buffered(1))


def _mod_spec(layer, cond_of):
    return pl.BlockSpec((None, None, 6, D_MODEL), lambda *idx: (layer, cond_of(*idx), 0, 0))


def _mod_kernel(cond_ref, w_ref, b_ref, o_ref):
    s = jax.nn.silu(cond_ref[...]).astype(BF16)
    o_ref[...] = _dot(s, w_ref[...].astype(BF16)) + b_ref[...]


def _modulation(cond, mod_w, mod_b):
    depth = mod_w.shape[0]
    n_out = mod_w.shape[2]
    tn = 2 * D_MODEL
    out = pl.pallas_call(
        _mod_kernel,
        grid=(depth, n_out // tn),
        in_specs=[
            pl.BlockSpec((COND_ROWS, D_MODEL), lambda l, j: (0, 0)),
            pl.BlockSpec((None, D_MODEL, tn), lambda l, j: (l, 0, j)),
            pl.BlockSpec((None, 1, tn), lambda l, j: (l, 0, j)),
        ],
        out_specs=pl.BlockSpec((None, COND_ROWS, tn), lambda l, j: (l, 0, j)),
        out_shape=jax.ShapeDtypeStruct((depth, COND_ROWS, n_out), F32),
        compiler_params=_params(2),
        name="modulation",
    )(cond, mod_w, mod_b.reshape(depth, 1, n_out))
    return out.reshape(depth, COND_ROWS, 6, D_MODEL)


def _run_staggered(programs, stagger):
    programs = list(programs)
    live, rounds = [], 0
    while programs or live:
        if programs and rounds % stagger == 0:
            live.append(programs.pop(0))
        for g in list(live):
            try:
                next(g)
            except StopIteration:
                live.remove(g)
        rounds += 1


def _even_kernel(*refs, n_chunks, nsub):
    halos, (xc_ref, mod_ref, ng_ref, win_ref, caw_ref, cab_ref, lng_ref, lnb_ref, cbw_ref, wout_ref, o_ref,
            apad, cpad, bgs, zs) = refs[:2 * nsub], refs[2 * nsub:]
    shared = (mod_ref, ng_ref, caw_ref, cab_ref, lng_ref, lnb_ref, cbw_ref)
    finals = []
    chunks = [_even_chunk(pl.program_id(0) * nsub + k, halos[2 * k], xc_ref.at[k], halos[2 * k + 1], *shared,
                          o_ref.at[k], apad.at[k], cpad.at[k], bgs.at[k], zs.at[k], n_chunks=n_chunks,
                          finals=finals) for k in range(nsub)]
    h_all = jnp.concatenate([next(g) for g in chunks], axis=0)
    n = h_all.shape[0] // nsub
    ys = [_dot(h_all, win_ref[:, k * D_HALF:(k + 1) * D_HALF]) for k in range(5)]
    for idx, g in enumerate(chunks):
        g.send([y[idx * n:(idx + 1) * n] for y in ys])
    _run_staggered(chunks, EVEN_STAGGER)
    o = _dot(jnp.concatenate([z() for z, _ in finals], axis=0), wout_ref[...])
    for idx, (_, residual) in enumerate(finals):
        residual(o[idx * ROW_CHUNK:(idx + 1) * ROW_CHUNK])


def _even_chunk(g, xp_ref, xc_ref, xn_ref, mod_ref, ng_ref, caw_ref, cab_ref, lng_ref, lnb_ref,
                cbw_ref, o_ref, apad, cpad, bgs, zs, *, n_chunks, finals):
    c = g % n_chunks if n_chunks > 1 else 0
    rows = ROW_CHUNK + 2 * CONF_HALO
    shift1, scale1, gate1 = mod_ref[0:1, :], mod_ref[1:2, :], mod_ref[2:3, :]
    own = slice(CONF_HALO, CONF_HALO + ROW_CHUNK)
    if n_chunks == 1:
        h = _norm_mod(xc_ref[...], ng_ref[0:1, :], shift1, scale1).astype(BF16)
        for pad_ref in (apad, cpad):
            pad_ref[0:CONF_HALO, :] = jnp.zeros((CONF_HALO, D_HALF), F32)
            pad_ref[CONF_HALO + ROW_CHUNK:rows, :] = jnp.zeros((CONF_HALO, D_HALF), F32)
        keep = lambda v: v
        span = own
    else:
        xh = jnp.concatenate([xp_ref[...], xc_ref[...], xn_ref[...]], axis=0)
        h = _norm_mod(xh, ng_ref[0:1, :], shift1, scale1).astype(BF16)
        ri = lax.broadcasted_iota(jnp.int32, (rows, D_HALF), 0)
        lo = jnp.where(c == 0, CONF_HALO, 0)
        hi = jnp.where(c == n_chunks - 1, CONF_HALO + ROW_CHUNK, rows)
        inside = (ri >= lo) & (ri < hi)
        keep = lambda v: jnp.where(inside, v, 0.0)
        span = slice(0, rows)
    a_val, a_gate, b_gate, c_gate, b_x = yield h
    apad[span, :] = keep(a_val * jax.nn.sigmoid(a_gate))
    cpad[span, :] = keep(c_gate * b_x)
    bgs[...] = b_gate if n_chunks == 1 else b_gate[own]
    yield

    sub = EVEN_CONV_ROWS
    tile = 8
    for j in range(ROW_CHUNK // sub):
        r0 = j * sub
        groups = []
        for cg in range(D_HALF // LANES):
            cols = slice(cg * LANES, (cg + 1) * LANES)
            acc = None
            for r in range(tile):
                part = None
                for m in range(-(-(CONF_WIDTH + 1) // tile)):
                    o = tile * m + r
                    if 1 <= o <= CONF_WIDTH:
                        term = caw_ref[o - 1:o, cols] * apad[r0 + tile * m:r0 + tile * m + sub + tile, cols]
                        part = term if part is None else part + term
                shifted = part[r:r + sub, :]
                acc = shifted if acc is None else acc + shifted
            groups.append(acc)
        acc = jnp.concatenate(groups, axis=1) + cab_ref[...]
        mu = jnp.mean(acc, axis=-1, keepdims=True)
        dlt = acc - mu
        var = jnp.mean(dlt * dlt, axis=-1, keepdims=True)
        a_out = jax.nn.silu(dlt * lax.rsqrt(var + EPS) * lng_ref[...] + lnb_ref[...])
        zs[r0:r0 + sub, 0:D_HALF] = a_out.astype(BF16)
        base = r0 + CONF_HALO - 1
        sc = (cbw_ref[0:1, :] * cpad[base:base + sub, :]
              + cbw_ref[1:2, :] * cpad[base + 1:base + 1 + sub, :]
              + cbw_ref[2:3, :] * cpad[base + 2:base + 2 + sub, :])
        zs[r0:r0 + sub, D_HALF:D_MODEL] = (bgs[r0:r0 + sub, :] * sc).astype(BF16)
        yield

    def residual(o):
        o_ref[...] = xc_ref[...] + gate1 * _rms(o, ng_ref[1:2, :])

    finals.append((lambda: zs[...], residual))


def _even_layer(x, modv, layer, cond_base, cond_stride, ng, win, caw, cab, lng, lnb, cbw, wout):
    j = layer // 2
    b, t, _ = x.shape
    n_chunks = t // ROW_CHUNK
    nsub = EVEN_CHUNKS
    assert (b * n_chunks) % nsub == 0 and (cond_stride == 0 or n_chunks % nsub == 0)
    hpc = ROW_CHUNK // CONF_HALO
    n_halo_blocks = b * t // CONF_HALO
    rows = ROW_CHUNK + 2 * CONF_HALO
    halo_specs = []
    for k in range(nsub):
        halo_specs += [
            pl.BlockSpec((None, CONF_HALO, D_MODEL),
                         lambda i, k=k: (jnp.maximum((i * nsub + k) * hpc - 1, 0), 0, 0)),
            pl.BlockSpec((None, CONF_HALO, D_MODEL),
                         lambda i, k=k: (jnp.minimum((i * nsub + k + 1) * hpc, n_halo_blocks - 1), 0, 0)),
        ]
    x_halo = x.reshape(n_halo_blocks, CONF_HALO, D_MODEL)
    x_chunks = x.reshape(b * n_chunks, ROW_CHUNK, D_MODEL)
    kern = functools.partial(_even_kernel, n_chunks=n_chunks, nsub=nsub)
    out = pl.pallas_call(
        kern,
        grid=(b * n_chunks // nsub,),
        in_specs=halo_specs + [
            pl.BlockSpec((nsub, ROW_CHUNK, D_MODEL), lambda i: (i, 0, 0)),
            _mod_spec(layer, lambda i: cond_base + cond_stride * ((i * nsub) // n_chunks)),
            _layer_spec((4, D_MODEL), layer),
            _layer_spec((D_MODEL, 5 * D_HALF), j),
            _layer_spec((CONF_WIDTH, D_HALF), j),
            _layer_spec((1, D_HALF), j),
            _layer_spec((1, D_HALF), j),
            _layer_spec((1, D_HALF), j),
            _layer_spec((3, D_HALF), j),
            _layer_spec((D_MODEL, D_MODEL), j),
        ],
        out_specs=pl.BlockSpec((nsub, ROW_CHUNK, D_MODEL), lambda i: (i, 0, 0)),
        out_shape=jax.ShapeDtypeStruct(x_chunks.shape, F32),
        scratch_shapes=[
            pltpu.VMEM((nsub, rows, D_HALF), F32),
            pltpu.VMEM((nsub, rows, D_HALF), F32),
            pltpu.VMEM((nsub, ROW_CHUNK, D_HALF), F32),
            pltpu.VMEM((nsub, ROW_CHUNK, D_MODEL), BF16),
        ],
        compiler_params=_params(1),
        name="even_mixer",
    )(*([x_halo] * (2 * nsub)), x_chunks, modv, ng, win, caw, cab, lng, lnb, cbw, wout)
    return out.reshape(b, t, D_MODEL)


def _mlp_kernel(xp_ref, xs_ref, mod_ref, ng_ref, w1_hbm, w2_hbm, op_ref, os_ref, w1_s, w2_s, stage, sem, *,
                layer, n_ctx_tiles):
    n_blocks = D_FF // D_MODEL

    def block_copy(k):
        c, slot = k // 2, k % 2
        span = pl.ds(c * D_MODEL, D_MODEL)
        src = w1_hbm.at[layer, :, span] if k % 2 == 0 else w2_hbm.at[layer, span, :]
        return pltpu.make_async_copy(src, stage.at[slot], sem.at[slot])

    def fetch(k, dst):
        block_copy(k).wait()
        dst[...] = stage[k % 2].astype(BF16)
        if k + 2 < 2 * n_blocks:
            block_copy(k + 2).start()

    def rows_program(x_ref, o_ref, rows, load):
        x = x_ref[rows, :]
        h = _norm_mod(x, ng_ref[2:3, :], mod_ref[3:4, :], mod_ref[4:5, :]).astype(BF16)
        yield
        acc = None
        for c in range(n_blocks):
            cols = slice(c * D_MODEL, (c + 1) * D_MODEL)
            if load:
                fetch(2 * c, w1_s.at[:, cols])
            hid = jnp.square(jnp.maximum(_dot(h, w1_s[:, cols]), 0.0)).astype(BF16)
            yield
            if load:
                fetch(2 * c + 1, w2_s.at[cols, :])
            part = _dot(hid, w2_s[cols, :])
            acc = part if acc is None else acc + part
            yield
        o_ref[rows, :] = x + mod_ref[5:6, :] * _rms(acc, ng_ref[3:4, :])
        yield

    def tile(x_ref, o_ref, load=False):
        n_rows = x_ref.shape[0]
        if load:
            _drain(rows_program(x_ref, o_ref, slice(0, n_rows), True))
        else:
            _run_staggered((rows_program(x_ref, o_ref, slice(r0, r0 + n_rows // 2), False)
                            for r0 in (0, n_rows // 2)), MLP_STAGGER)

    i = pl.program_id(0)

    @pl.when(i == 0)
    def _():
        block_copy(0).start()
        block_copy(1).start()
        tile(xp_ref, op_ref, load=True)

    @pl.when((i > 0) & (i < n_ctx_tiles))
    def _():
        tile(xp_ref, op_ref)

    @pl.when(i >= n_ctx_tiles)
    def _():
        tile(xs_ref, os_ref)


def _mlp_layer(xp, xs, modv, layer, ng, w1, w2):
    tm = MLP_ROWS
    xp2, xs2 = xp.reshape(-1, D_MODEL), xs.reshape(-1, D_MODEL)
    t_dec = xs.shape[1]
    assert xp2.shape[0] % tm == 0 and t_dec % tm == 0
    n_p, n_s, per_seq = xp2.shape[0] // tm, xs2.shape[0] // tm, t_dec // tm
    ctx_tile = lambda i: (jnp.minimum(i, n_p - 1), 0)
    lat_tile = lambda i: (jnp.maximum(i - n_p, 0), 0)
    yp, ys = pl.pallas_call(
        functools.partial(_mlp_kernel, layer=layer, n_ctx_tiles=n_p),
        grid=(n_p + n_s,),
        in_specs=[
            pl.BlockSpec((tm, D_MODEL), ctx_tile),
            pl.BlockSpec((tm, D_MODEL), lat_tile),
            _mod_spec(layer, lambda i: jnp.where(i < n_p, 0, 1 + jnp.maximum(i - n_p, 0) // per_seq)),
            _layer_spec((4, D_MODEL), layer),
            pl.BlockSpec(memory_space=pl.ANY),
            pl.BlockSpec(memory_space=pl.ANY),
        ],
        out_specs=[pl.BlockSpec((tm, D_MODEL), ctx_tile), pl.BlockSpec((tm, D_MODEL), lat_tile)],
        out_shape=[jax.ShapeDtypeStruct(xp2.shape, F32), jax.ShapeDtypeStruct(xs2.shape, F32)],
        scratch_shapes=[
            pltpu.VMEM((D_MODEL, D_FF), BF16),
            pltpu.VMEM((D_FF, D_MODEL), BF16),
            pltpu.VMEM((2, D_MODEL, D_MODEL), F32),
            pltpu.SemaphoreType.DMA((2,)),
        ],
        compiler_params=_params(1),
        name="mlp",
    )(xp2, xs2, modv, ng, w1, w2)
    return yp.reshape(xp.shape), ys.reshape(xs.shape)


def _log_sigmoid(x):
    return jnp.minimum(x, 0.0) - jnp.log(1.0 + jnp.exp(-jnp.abs(x)))


def _rot_half(x, first_half):
    return jnp.where(first_half, pltpu.roll(x, 96, axis=1), pltpu.roll(x, 32, axis=1))


def _values_and_ones(v):
    vt = jnp.transpose(v)
    row = lax.broadcasted_iota(jnp.int32, vt.shape, 0)
    return jnp.where(row < HEAD_DIM, vt, 1.0)


def _aligned(i, m):
    return i * m if isinstance(i, int) else pl.multiple_of(i * m, m)


def _drain(pieces):
    for _ in pieces:
        pass


def _each(body, n, static, unroll=1):
    if static:
        for i in range(n):
            yield from body(i)
    else:
        lax.fori_loop(0, n, lambda i, c: (_drain(body(i)), c)[1], 0, unroll=unroll)


def _alternate(*programs):
    live = list(programs)
    while live:
        for g in list(live):
            try:
                next(g)
            except StopIteration:
                live.remove(g)
            else:
                yield


def _odd_kernel(*refs, t, lc, latent, nseq):
    shared = set(range(1, 11 if latent else 9))
    n_rows = min(t, ODD_ROWS)
    joint = not latent and nseq > 1 and t == n_rows
    finals = []
    programs = [_odd_seq(*[r if k in shared else r.at[sq] for k, r in enumerate(refs)],
                         t=t, lc=lc, latent=latent, joint=joint, finals=finals) for sq in range(nseq)]
    if joint:
        parts = [next(g) for g in programs]
        h_all = jnp.concatenate([proj_h(0) for proj_h, _ in parts], axis=0)
        for idx, (weight, _) in enumerate(parts[0][1]):
            y = _dot(h_all, weight())
            for sq, (_, plan) in enumerate(parts):
                plan[idx][1](y[sq * n_rows:(sq + 1) * n_rows], 0)
    _run_staggered(programs, ODD_STAGGER)
    if joint:
        o = _dot(jnp.concatenate([z() for z, _ in finals], axis=0), refs[8][...])
        for sq, (_, residual) in enumerate(finals):
            residual(o[sq * n_rows:(sq + 1) * n_rows])


def _odd_seq(*refs, t, lc, latent, joint, finals):
    if latent:
        (x_ref, mod_ref, ng_ref, win_ref, wg_ref, gb_ref, sink_ref, hn_ref, wout_ref,
         cos_ref, sin_ref, kc_ref, vc_ref, cin_ref, nin_ref, min_ref,
         o_ref,
         qa_s, qb_s, kk_s, vvt_s, qma_s, qmb_s, qm_s, km_s, vmt_s, vmtf_s, om_s, g_s, z_s, hft_s, hbt_s,
         sr_s, st_s, qc_s, c_s, n_s, m_s, s_s, p_s, kc_s, vct_s) = refs
    else:
        (x_ref, mod_ref, ng_ref, win_ref, wg_ref, gb_ref, sink_ref, hn_ref, wout_ref,
         o_ref, ko_ref, vo_ref, cf_ref, cb_ref, no_ref, mo_ref,
         qa_s, qb_s, kk_s, vvt_s, qma_s, qmb_s, qm_s, km_s, vmt_s, vmtf_s, om_s, g_s, z_s, hft_s, hbt_s,
         sr_s, st_s, qc_s, c_s, n_s, m_s, s_s, p_s) = refs

    static = not latent
    n_blocks = t // CHUNK
    pad = CHUNK if latent else 0
    shift1, scale1, gate1 = mod_ref[0:1, :], mod_ref[1:2, :], mod_ref[2:3, :]
    lane = lax.broadcasted_iota(jnp.int32, (1, LANES), 1)
    left = lane < HEAD_DIM
    first_half = (lane % HEAD_DIM) < (HEAD_DIM // 2)
    ti = lax.broadcasted_iota(jnp.int32, (CHUNK, CHUNK), 0)
    si = lax.broadcasted_iota(jnp.int32, (CHUNK, CHUNK), 1)
    top = ti < HEAD_DIM
    same_head = top == (si < HEAD_DIM)

    n_rows = min(t, ODD_ROWS)
    if latent:
        for kv in range(N_KV):
            kk_s[kv, 0:CHUNK, :] = jnp.zeros((CHUNK, LANES), BF16)
            kk_s[kv, CHUNK + t:2 * CHUNK + t, :] = jnp.zeros((CHUNK, LANES), BF16)
            vvt_s[kv, :, 0:CHUNK] = jnp.zeros((LANES, CHUNK), BF16)
            vvt_s[kv, :, CHUNK + t:2 * CHUNK + t] = jnp.zeros((LANES, CHUNK), BF16)
            kc_s[kv] = kc_ref[kv].astype(BF16)
            vct_s[kv] = _values_and_ones(vc_ref[kv]).astype(BF16)

    base = D_HALF + 2 * LANES

    def proj_rows(rc):
        r0 = _aligned(rc, n_rows)
        return pl.ds(r0, n_rows), pl.ds(r0 + pad, n_rows)

    def proj_h(rc):
        rows, _ = proj_rows(rc)
        return _norm_mod(x_ref[rows, :], ng_ref[0:1, :], shift1, scale1).astype(BF16)

    def rope(v, rows):
        return v * cos_ref[rows, :] + _rot_half(v, first_half) * sin_ref[rows, :]

    def use_q(q, rc):
        rows, _ = proj_rows(rc)
        for p in range(N_PAIRS):
            cols = slice(p * LANES, (p + 1) * LANES)
            qp = q[:, cols]
            if latent:
                qp = rope(qp, rows)
            qp = qp * ATT_SCALE
            qa_s[rows, cols] = jnp.where(left, qp, 0.0).astype(BF16)
            qb_s[rows, cols] = jnp.where(left, 0.0, qp).astype(BF16)

    def use_kv(kv2, rc):
        rows, krows = proj_rows(rc)
        ka, va = kv2[:, 0:LANES], kv2[:, LANES:2 * LANES]
        if latent:
            ka = rope(ka, rows)
        kr = pltpu.roll(ka, HEAD_DIM, axis=1)
        vr = pltpu.roll(va, HEAD_DIM, axis=1)
        if not latent:
            ka_t, va_t = jnp.transpose(ka), jnp.transpose(va)
            for kv in range(N_KV):
                ko_ref[0, kv, :, rows] = ka_t[kv * HEAD_DIM:(kv + 1) * HEAD_DIM, :]
                vo_ref[0, kv, :, rows] = va_t[kv * HEAD_DIM:(kv + 1) * HEAD_DIM, :]
        kk_s[0, krows, :] = jnp.where(left, ka, kr).astype(BF16)
        kk_s[1, krows, :] = jnp.where(left, kr, ka).astype(BF16)
        vvt_s[0, :, krows] = _values_and_ones(va).astype(BF16)
        vvt_s[1, :, krows] = _values_and_ones(vr).astype(BF16)

    def use_qm(qm, rc):
        rows, _ = proj_rows(rc)
        qm_s[rows, :] = qm.astype(BF16)
        for p in range(N_PAIRS):
            cols = slice(p * LANES, (p + 1) * LANES)
            qma_s[rows, cols] = jnp.where(left, qm[:, cols], 0.0).astype(BF16)
            qmb_s[rows, cols] = jnp.where(left, 0.0, qm[:, cols]).astype(BF16)

    def use_vm(vm, rc):
        rows, _ = proj_rows(rc)
        for p in range(N_PAIRS):
            cols = slice(p * LANES, (p + 1) * LANES)
            vt = jnp.transpose(vm[:, cols])
            vmtf_s[cols, rows] = vt
            vmt_s[cols, rows] = vt.astype(BF16)

    def use_km(km, rc):
        km_s[proj_rows(rc)[0], :] = (km * (HEAD_DIM ** -0.5)).astype(BF16)

    def use_om(om, rc):
        om_s[proj_rows(rc)[0], :] = om

    def use_g(g, rc):
        g_s[proj_rows(rc)[0], :] = g + gb_ref[...]

    def columns(start, width):
        return lambda: win_ref[:, start:start + width]

    proj_plan = [(columns(0, D_HALF), use_q), (columns(D_HALF, 2 * LANES), use_kv),
                 (columns(base, D_HALF), use_qm), (columns(base + 2 * D_HALF, D_HALF), use_vm),
                 (columns(base + D_HALF, D_HALF), use_km), (columns(base + 3 * D_HALF, D_HALF), use_om),
                 (lambda: wg_ref[...], use_g)]

    def project(rc):
        h = proj_h(rc)
        for weight, use in proj_plan:
            use(_dot(h, weight()), rc)
            yield

    if joint:
        yield proj_h, proj_plan
    else:
        yield from _each(project, t // n_rows, static)

    kj = lax.broadcasted_iota(jnp.int32, (3 * CHUNK, 2 * CHUNK), 0)
    qi = lax.broadcasted_iota(jnp.int32, (3 * CHUNK, 2 * CHUNK), 1) % CHUNK
    band_ok = jnp.abs(kj - CHUNK - qi) <= WINDOW
    head_a = lax.broadcasted_iota(jnp.int32, (1, 2 * CHUNK), 1) < CHUNK

    n_keys = _att_keys(lc, latent)
    group = _att_group(latent)

    def attention():
        def attend(i):
            r0 = _aligned(i, CHUNK)
            rows = pl.ds(r0, CHUNK)
            if latent:
                key_pos = kj + (i - 1) * CHUNK
                mask = band_ok & (key_pos >= 0) & (key_pos < t)
                win = pl.ds(r0, 3 * CHUNK)
            for g0 in range(0, N_PAIRS, group):
                for gi in range(group):
                    p = g0 + gi
                    kv = p // (N_PAIRS // N_KV)
                    cols = slice(p * LANES, (p + 1) * LANES)
                    q2 = jnp.concatenate([qa_s[rows, cols], qb_s[rows, cols]], axis=0)
                    if latent:
                        s_s[gi, 0:lc, :] = _dot_nt(kc_s[kv], q2)
                        s_s[gi, lc:n_keys, :] = jnp.where(mask, _dot_nt(kk_s[kv, win, :], q2), -jnp.inf)
                    else:
                        s_s[gi] = _dot_nt(kk_s[kv], q2)
                yield
                maxes = []
                for gi in range(group):
                    p = g0 + gi
                    sink = jnp.where(head_a, sink_ref[2 * p:2 * p + 1, 0:1], sink_ref[2 * p + 1:2 * p + 2, 0:1])
                    mx = jnp.maximum(jnp.max(s_s[gi], axis=0, keepdims=True), sink)
                    p_s[gi] = jnp.exp(s_s[gi] - mx).astype(BF16)
                    maxes.append((sink, mx))
                yield
                for gi in range(group):
                    p = g0 + gi
                    kv = p // (N_PAIRS // N_KV)
                    cols = slice(p * LANES, (p + 1) * LANES)
                    if latent:
                        num = (_dot(vct_s[kv], p_s[gi, 0:lc, :]) + _dot(vvt_s[kv, :, win], p_s[gi, lc:n_keys, :]))
                    else:
                        num = _dot(vvt_s[kv], p_s[gi])
                    sink, mx = maxes[gi]
                    den = num[HEAD_DIM:HEAD_DIM + 1, :] + jnp.exp(sink - mx)
                    out = num[0:HEAD_DIM, :] * (1.0 / den)
                    pair = jnp.concatenate([out[:, 0:CHUNK], out[:, CHUNK:2 * CHUNK]], axis=0)
                    z_s[rows, cols] = jnp.transpose(pair).astype(BF16)
                yield

        return attend

    if latent:
        c_s[...] = cin_ref[...]
        n_s[...] = nin_ref[...]
        m_s[...] = min_ref[...]
    else:
        c_s[...] = jnp.zeros(c_s.shape, F32)
        n_s[...] = jnp.zeros(n_s.shape, F32)
        m_s[...] = jnp.zeros(m_s.shape, F32)

    see = (ti <= si, ti >= si)
    tri = tuple(m.astype(F32).astype(BF16) for m in see)
    last = (CHUNK - 1, 0)

    def mlstm(i):
        offs = (_aligned(i, CHUNK), _aligned(n_blocks - 1 - i, CHUNK))
        half = (slice(0, LANES), slice(LANES, 2 * LANES))
        a_rows, b_rows = [], []
        for d in range(2):
            gt = jnp.transpose(g_s[pl.ds(offs[d], CHUNK), :])
            lf = _log_sigmoid(gt[2 * N_HEADS:4 * N_HEADS, :])
            bcum = _dot_exact_lhs(lf, tri[d])[8 * d:8 * d + 8, :]
            a_rows.append(gt[8 * d:8 * d + 8, :] - bcum)
            b_rows.append(bcum)
        yield
        q_ns = []
        for d in range(2):
            rows = pl.ds(offs[d], CHUNK)
            for p in range(N_PAIRS):
                u = d * N_PAIRS + p
                cols = slice(p * LANES, (p + 1) * LANES)
                q2 = jnp.concatenate([qma_s[rows, cols], qmb_s[rows, cols]], axis=0)
                sr_s[u] = _dot_nt(km_s[rows, cols], q2)
                qc_s[u] = _dot_nt(c_s[u].astype(BF16), qm_s[rows, cols])
                n8 = jnp.broadcast_to(n_s[u:u + 1, :], (8, LANES)).astype(BF16)
                q_ns.append(_dot_nt(n8, q2)[0:1, :])
            yield
        stats = {}
        for d in range(2):
            for hd in range(N_HEADS):
                u, j = d * N_PAIRS + hd // 2, hd % 2
                mrow = d * N_HEADS + hd
                a_row = a_rows[d][hd:hd + 1, :]
                b_row = b_rows[d][hd:hd + 1, :]
                m_prev = m_s[mrow:mrow + 1, 0:1]
                a_col = jnp.transpose(jnp.broadcast_to(a_row, (CHUNK, CHUNK)))
                z_t = jnp.where(see[d], a_col, -jnp.inf)
                m_run = jnp.maximum(jnp.max(z_t, axis=0, keepdims=True), m_prev)
                s_t = sr_s[u, :, half[j]] * jnp.exp(z_t - m_run)
                st_s[u, :, half[j]] = s_t.astype(BF16)
                w_int = jnp.exp(m_prev - m_run)
                den = jnp.sum(s_t, axis=0, keepdims=True) + w_int * q_ns[u][:, half[j]]
                inv = 1.0 / jnp.maximum(jnp.abs(den), jnp.exp(-(b_row + m_run)))
                m_last = m_run[:, last[d]:last[d] + 1]
                stats[(u, j)] = (w_int, inv, jnp.exp(a_row - m_last), jnp.exp(m_prev - m_last))
                m_s[mrow:mrow + 1, :] = jnp.broadcast_to(b_row[:, last[d]:last[d] + 1] + m_last, (1, LANES))
            yield
        for d in range(2):
            rows = pl.ds(offs[d], CHUNK)
            ht_s = (hft_s, hbt_s)[d]
            for p in range(N_PAIRS):
                u = d * N_PAIRS + p
                cols = slice(p * LANES, (p + 1) * LANES)
                (w_a, inv_a, e_a, dec_a), (w_b, inv_b, e_b, dec_b) = stats[(u, 0)], stats[(u, 1)]
                kp = km_s[rows, cols]
                num2 = _dot(vmt_s[cols, rows], st_s[u])
                num = jnp.where(top, num2[:, half[0]], num2[:, half[1]])
                ht_s[cols, rows] = (num + jnp.where(top, w_a, w_b) * qc_s[u]) * jnp.where(top, inv_a, inv_b)
                vt_e = (vmtf_s[cols, rows] * jnp.where(top, e_a, e_b)).astype(BF16)
                c_s[u] = jnp.where(top, dec_a, dec_b) * c_s[u] + jnp.where(same_head, _dot(vt_e, kp), 0.0)
                e2 = jnp.concatenate([e_a, e_b, jnp.zeros((6, CHUNK), F32)], axis=0).astype(BF16)
                n_k = _dot(e2, kp)
                n_s[u:u + 1, :] = (jnp.where(left, dec_a, dec_b) * n_s[u:u + 1, :]
                                   + jnp.where(left, n_k[0:1, :], n_k[1:2, :]))
            yield

    attend = attention()
    yield from _each(lambda i: _alternate(attend(i), mlstm(i)), n_blocks, static, unroll=2)

    if not latent:
        for d, c_ref in enumerate((cf_ref, cb_ref)):
            for p in range(N_PAIRS):
                c_pair = jnp.transpose(c_s[d * N_PAIRS + p])
                c_ref[0, 2 * p] = c_pair[0:HEAD_DIM, 0:HEAD_DIM]
                c_ref[0, 2 * p + 1] = c_pair[HEAD_DIM:LANES, HEAD_DIM:LANES]
        no_ref[...] = n_s[...]
        mo_ref[...] = m_s[...]
        yield

    top_w = lax.broadcasted_iota(jnp.int32, (LANES, n_rows), 0) < HEAD_DIM

    def finish(rc):
        r0 = _aligned(rc, n_rows)
        rows = pl.ds(r0, n_rows)
        for p in range(N_PAIRS):
            cols = slice(p * LANES, (p + 1) * LANES)
            hm = hft_s[cols, rows] + hbt_s[cols, rows]
            sq = hm * hm
            ms_a = jnp.sum(sq[0:HEAD_DIM], axis=0, keepdims=True)
            ms_b = jnp.sum(sq[HEAD_DIM:LANES], axis=0, keepdims=True)
            ms = jnp.where(top_w, ms_a, ms_b) * (1.0 / HEAD_DIM)
            y = jnp.transpose(hm * lax.rsqrt(ms + EPS)) * hn_ref[:, cols] * jax.nn.sigmoid(om_s[rows, cols])
            z_s[rows, D_HALF + p * LANES:D_HALF + (p + 1) * LANES] = y.astype(BF16)
            if p % 2:
                yield
        def residual(o):
            o_ref[rows, :] = x_ref[rows, :] + gate1 * _rms(o, ng_ref[1:2, :])

        if joint:
            finals.append((lambda: z_s[rows, :], residual))
        else:
            residual(_dot(z_s[rows, :], wout_ref[...]))
        yield

    yield from _each(finish, t // n_rows, static)


def _att_keys(lc, latent):
    return lc + 3 * CHUNK if latent else lc


def _att_group(latent):
    return 2 if latent else N_PAIRS


def _odd_scratch(t, lc, latent, nseq):
    pad = 2 * CHUNK if latent else 0
    att = (_att_group(latent), _att_keys(lc, latent), 2 * CHUNK)
    shapes = [
        pltpu.VMEM((t, D_HALF), BF16),
        pltpu.VMEM((t, D_HALF), BF16),
        pltpu.VMEM((N_KV, t + pad, LANES), BF16),
        pltpu.VMEM((N_KV, LANES, t + pad), BF16),
        pltpu.VMEM((t, D_HALF), BF16),
        pltpu.VMEM((t, D_HALF), BF16),
        pltpu.VMEM((t, D_HALF), BF16),
        pltpu.VMEM((t, D_HALF), BF16),
        pltpu.VMEM((D_HALF, t), BF16),
        pltpu.VMEM((D_HALF, t), F32),
        pltpu.VMEM((t, D_HALF), F32),
        pltpu.VMEM((t, LANES), F32),
        pltpu.VMEM((t, D_MODEL), BF16),
        pltpu.VMEM((D_HALF, t), F32),
        pltpu.VMEM((D_HALF, t), F32),
        pltpu.VMEM((2 * N_PAIRS, CHUNK, 2 * LANES), F32),
        pltpu.VMEM((2 * N_PAIRS, CHUNK, 2 * LANES), BF16),
        pltpu.VMEM((2 * N_PAIRS, LANES, CHUNK), F32),
        pltpu.VMEM((2 * N_PAIRS, LANES, LANES), F32),
        pltpu.VMEM((2 * N_PAIRS, LANES), F32),
        pltpu.VMEM((2 * N_HEADS, LANES), F32),
        pltpu.VMEM(att, F32),
        pltpu.VMEM(att, BF16),
    ]
    if latent:
        shapes += [pltpu.VMEM((N_KV, lc, LANES), BF16), pltpu.VMEM((N_KV, LANES, lc), BF16)]
    return [pltpu.VMEM((nseq,) + tuple(sh.shape), sh.dtype) for sh in shapes]


def _odd_common_specs(t, layer, cond_base, cond_stride, nseq):
    assert cond_stride == 0 or nseq == 1
    j = layer // 2
    return [
        _per_seq((t, D_MODEL), nseq),
        _mod_spec(layer, lambda i: cond_base + cond_stride * i),
        _layer_spec((4, D_MODEL), layer),
        _layer_spec((D_MODEL, D_IN_ODD), j),
        _const_spec((D_MODEL, LANES)),
        _const_spec((1, LANES)),
        _const_spec((N_HEADS, LANES)),
        _layer_spec((1, D_HALF), j),
        _layer_spec((D_MODEL, D_MODEL), j),
    ]


def _per_seq(shape, nseq):
    return pl.BlockSpec((nseq,) + shape, lambda i: (i,) + (0,) * len(shape))


def _odd_context(x, modv, layer, ng, w_main, w_gate, gate_bias, sink_b, hnorm, wout):
    b, t, _ = x.shape
    nseq = ODD_CTX_SEQS
    assert b % nseq == 0
    kern = functools.partial(_odd_kernel, t=t, lc=t, latent=False, nseq=nseq)
    per_seq = functools.partial(_per_seq, nseq=nseq)
    return pl.pallas_call(
        kern,
        grid=(b // nseq,),
        in_specs=_odd_common_specs(t, layer, 0, 0, nseq),
        out_specs=[per_seq((t, D_MODEL)),
                   per_seq((1, N_KV, HEAD_DIM, t)), per_seq((1, N_KV, HEAD_DIM, t)),
                   per_seq((1, N_HEADS, HEAD_DIM, HEAD_DIM)), per_seq((1, N_HEADS, HEAD_DIM, HEAD_DIM)),
                   per_seq((2 * N_PAIRS, LANES)), per_seq((2 * N_HEADS, LANES))],
        out_shape=[jax.ShapeDtypeStruct((b, t, D_MODEL), F32),
                   jax.ShapeDtypeStruct((b, 1, N_KV, HEAD_DIM, t), F32),
                   jax.ShapeDtypeStruct((b, 1, N_KV, HEAD_DIM, t), F32),
                   jax.ShapeDtypeStruct((b, 1, N_HEADS, HEAD_DIM, HEAD_DIM), F32),
                   jax.ShapeDtypeStruct((b, 1, N_HEADS, HEAD_DIM, HEAD_DIM), F32),
                   jax.ShapeDtypeStruct((b, 2 * N_PAIRS, LANES), F32),
                   jax.ShapeDtypeStruct((b, 2 * N_HEADS, LANES), F32)],
        scratch_shapes=_odd_scratch(t, t, False, nseq),
        compiler_params=_params(1),
        name="odd_mixer_context",
    )(x, modv, ng, w_main, w_gate, gate_bias, sink_b, hnorm, wout)


def _odd_latent(x, modv, layer, ng, w_main, w_gate, gate_bias, sink_b, hnorm, wout, cos_t, sin_t, kc, vc,
                c_in, n_in, m_in):
    b, t, _ = x.shape
    lc = kc.shape[2]
    kern = functools.partial(_odd_kernel, t=t, lc=lc, latent=True, nseq=1)
    per_seq = functools.partial(_per_seq, nseq=1)
    return pl.pallas_call(
        kern,
        grid=(b,),
        in_specs=_odd_common_specs(t, layer, 1, 1, 1) + [
            _const_spec((t, LANES)), _const_spec((t, LANES)),
            per_seq((N_KV, lc, LANES)), per_seq((N_KV, lc, LANES)),
            per_seq((2 * N_PAIRS, LANES, LANES)), per_seq((2 * N_PAIRS, LANES)), per_seq((2 * N_HEADS, LANES)),
        ],
        out_specs=per_seq((t, D_MODEL)),
        out_shape=jax.ShapeDtypeStruct((b, t, D_MODEL), F32),
        scratch_shapes=_odd_scratch(t, lc, True, 1),
        compiler_params=_params(1),
        name="odd_mixer_latent",
    )(x, modv, ng, w_main, w_gate, gate_bias, sink_b, hnorm, wout, cos_t, sin_t, kc, vc, c_in, n_in, m_in)


def _rope_tables(t):
    rows = t // GRID_W
    row = jnp.broadcast_to(jnp.arange(rows)[:, None], (rows, GRID_W)).reshape(t).astype(F32)
    col = jnp.broadcast_to(jnp.arange(GRID_W)[None, :], (rows, GRID_W)).reshape(t).astype(F32)
    n_freq = HEAD_DIM // 4
    inv_freq = ROPE_BASE ** (-jnp.arange(n_freq, dtype=F32) / n_freq)
    ang = jnp.concatenate([row[:, None] * inv_freq, col[:, None] * inv_freq], axis=-1)
    cos, sin = jnp.cos(ang), jnp.sin(ang)
    cos_l = jnp.tile(cos, (1, LANES // cos.shape[1]))
    sin_l = jnp.tile(jnp.concatenate([-sin, sin], axis=-1), (1, LANES // HEAD_DIM))
    return cos_l, sin_l


def _pair_blockdiag(c):
    b = c.shape[0]
    c = c.reshape(b, N_PAIRS, 2, HEAD_DIM, HEAD_DIM)
    z = jnp.zeros_like(c[:, :, 0])
    top = jnp.concatenate([c[:, :, 0], z], axis=-1)
    bot = jnp.concatenate([z, c[:, :, 1]], axis=-1)
    return jnp.concatenate([top, bot], axis=-2)


def _lane_bcast(v):
    return jnp.broadcast_to(v[..., None], v.shape + (LANES,))


def kernel(x_prompt, x_sample, c, cache_k, cache_v, state_c_fwd, state_n_fwd, state_m_fwd, state_c_bwd, state_n_bwd, state_m_bwd, c_ctx, mod_w, mod_b, norm_g, mlp_w1, mlp_w2, even_in_w, conv_a_w, conv_a_b, ln_a_g, ln_a_b, conv_b_w, even_out_w, odd_in_w, attn_sink, gate_b, hnorm_g, odd_out_w):
    n_dec = x_sample.shape[0]
    n_ctx = x_prompt.shape[0]
    cond = jnp.concatenate([c_ctx[None, :], c, jnp.zeros((COND_ROWS - 1 - n_dec, D_MODEL), F32)], axis=0)
    modv = _modulation(cond, mod_w, mod_b)

    yp, ys = x_prompt, x_sample
    w1, w2 = mlp_w1, mlp_w2

    ev = (norm_g, even_in_w.astype(BF16), conv_a_w, conv_a_b[:, None, :], ln_a_g[:, None, :], ln_a_b[:, None, :],
          conv_b_w, even_out_w.astype(BF16))
    yp = _even_layer(yp, modv, 0, 0, 0, *ev)
    ys = _even_layer(ys, modv, 0, 1, 1, *ev)
    yp, ys = _mlp_layer(yp, ys, modv, 0, norm_g, w1, w2)

    order = jnp.array([0, 2, 1, 3])
    d_main = D_IN_ODD - 4 * N_HEADS
    wg = odd_in_w[0][:, d_main:].reshape(D_MODEL, 4, N_HEADS)[:, order, :].reshape(D_MODEL, 4 * N_HEADS)
    w_gate = jnp.pad(wg, ((0, 0), (0, LANES - 4 * N_HEADS))).astype(BF16)
    gate_bias = jnp.pad(gate_b[0][order, :].reshape(1, 4 * N_HEADS), ((0, 0), (0, LANES - 4 * N_HEADS)))
    sink_b = _lane_bcast(attn_sink[0])
    odd = (1, norm_g, odd_in_w.astype(BF16), w_gate, gate_bias, sink_b, hnorm_g[:, None, :], odd_out_w.astype(BF16))

    op, k_t, v_t, c_f, c_b, n_new, m_new = _odd_context(yp, modv, *odd)
    new_k, new_v = jnp.swapaxes(k_t, -1, -2), jnp.swapaxes(v_t, -1, -2)

    t_dec = x_sample.shape[1]
    cos_t, sin_t = _rope_tables(t_dec)
    kc = jnp.concatenate([cache_k[:, 0], cache_k[:, 0]], axis=-1)
    vc = jnp.concatenate([cache_v[:, 0], cache_v[:, 0]], axis=-1)
    c_in = jnp.concatenate([_pair_blockdiag(jnp.swapaxes(state_c_fwd[:, 0], -1, -2)),
                            _pair_blockdiag(jnp.swapaxes(state_c_bwd[:, 0], -1, -2))], axis=1)
    n_in = jnp.concatenate([state_n_fwd[:, 0].reshape(n_dec, N_PAIRS, LANES),
                            state_n_bwd[:, 0].reshape(n_dec, N_PAIRS, LANES)], axis=1)
    m_in = _lane_bcast(jnp.concatenate([state_m_fwd[:, 0], state_m_bwd[:, 0]], axis=1))
    os_ = _odd_latent(ys, modv, *odd, cos_t, sin_t, kc, vc, c_in, n_in, m_in)

    yp, ys = _mlp_layer(op, os_, modv, 1, norm_g, w1, w2)

    n_f = n_new[:, :N_PAIRS].reshape(n_ctx, N_HEADS, HEAD_DIM)[:, None]
    n_b = n_new[:, N_PAIRS:].reshape(n_ctx, N_HEADS, HEAD_DIM)[:, None]
    m_f = m_new[:, :N_HEADS, 0][:, None]
    m_b = m_new[:, N_HEADS:, 0][:, None]
    return (yp, ys, new_k, new_v, c_f, n_f, m_f, c_b, n_b, m_b)
```

```python
import functools

import jax
import jax.numpy as jnp
from jax import lax
from jax.experimental import pallas as pl
from jax.experimental.pallas import tpu as pltpu

F32 = jnp.float32
BF16 = jnp.bfloat16

D_MODEL = 1024
D_FF = 4 * D_MODEL
EPS = 1e-6
D_HALF = D_MODEL // 2
CONF_WIDTH = 31
CONF_HALO = 16
HEAD_DIM = 64
N_HEADS = 8
N_PAIRS = N_HEADS // 2
N_KV = 2
LANES = 128
CHUNK = 128
WINDOW = 128
GRID_W = 64
ROPE_BASE = 10000.0
ATT_SCALE = HEAD_DIM ** -0.5
D_IN_ODD = D_HALF + 2 * N_KV * HEAD_DIM + 4 * D_HALF + 4 * N_HEADS
ROW_CHUNK = 256
MLP_ROWS = 512
MLP_STAGGER = 3
ODD_ROWS = 512
ODD_CTX_SEQS = 4
ODD_STAGGER = 6
EVEN_CONV_ROWS = 64
EVEN_CHUNKS = 2
EVEN_STAGGER = 3
COND_ROWS = 8
VMEM_LIMIT = 56 * 1024 * 1024


def _dot(a, b):
    return jnp.dot(a, b, preferred_element_type=F32)


def _dot_nt(a, b):
    return lax.dot_general(a, b, (((1,), (1,)), ((), ())), preferred_element_type=F32)


def _split3(x):
    hi = x.astype(BF16)
    r1 = x - hi.astype(F32)
    mid = r1.astype(BF16)
    lo = (r1 - mid.astype(F32)).astype(BF16)
    return hi, mid, lo


def _dot_exact_lhs(x, b01):
    hi, mid, lo = _split3(x)
    return _dot(hi, b01) + _dot(mid, b01) + _dot(lo, b01)


def _rms(x, g):
    return x * lax.rsqrt(jnp.mean(x * x, axis=-1, keepdims=True) + EPS) * g


def _norm_mod(x, g, shift, scale):
    return _rms(x, g) * (1.0 + scale) + shift


def _params(n_grid):
    return pltpu.CompilerParams(dimension_semantics=("arbitrary",) * n_grid, vmem_limit_bytes=VMEM_LIMIT)


def _const_spec(shape):
    zeros = (0,) * len(shape)
    return pl.BlockSpec(shape, lambda *_: zeros, pipeline_mode=pl.Buffered(1))


def _layer_spec(shape, layer):
    index = (layer,) + (0,) * len(shape)
    return pl.BlockSpec((None,) + shape, lambda *_: index, pipeline_mode=pl.Buffered(1))


def _mod_spec(layer, cond_of):
    return pl.BlockSpec((None, None, 6, D_MODEL), lambda *idx: (layer, cond_of(*idx), 0, 0))


def _mod_kernel(cond_ref, w_ref, b_ref, o_ref):
    s = jax.nn.silu(cond_ref[...]).astype(BF16)
    o_ref[...] = _dot(s, w_ref[...].astype(BF16)) + b_ref[...]


def _modulation(cond, mod_w, mod_b):
    depth = mod_w.shape[0]
    n_out = mod_w.shape[2]
    tn = 2 * D_MODEL
    out = pl.pallas_call(
        _mod_kernel,
        grid=(depth, n_out // tn),
        in_specs=[
            pl.BlockSpec((COND_ROWS, D_MODEL), lambda l, j: (0, 0)),
            pl.BlockSpec((None, D_MODEL, tn), lambda l, j: (l, 0, j)),
            pl.BlockSpec((None, 1, tn), lambda l, j: (l, 0, j)),
        ],
        out_specs=pl.BlockSpec((None, COND_ROWS, tn), lambda l, j: (l, 0, j)),
        out_shape=jax.ShapeDtypeStruct((depth, COND_ROWS, n_out), F32),
        compiler_params=_params(2),
        name="modulation",
    )(cond, mod_w, mod_b.reshape(depth, 1, n_out))
    return out.reshape(depth, COND_ROWS, 6, D_MODEL)


def _run_staggered(programs, stagger):
    programs = list(programs)
    live, rounds = [], 0
    while programs or live:
        if programs and rounds % stagger == 0:
            live.append(programs.pop(0))
        for g in list(live):
            try:
                next(g)
            except StopIteration:
                live.remove(g)
        rounds += 1


def _even_kernel(*refs, n_chunks, nsub):
    halos, (xc_ref, mod_ref, ng_ref, win_ref, caw_ref, cab_ref, lng_ref, lnb_ref, cbw_ref, wout_ref, o_ref,
            apad, cpad, bgs, zs) = refs[:2 * nsub], refs[2 * nsub:]
    shared = (mod_ref, ng_ref, win_ref, caw_ref, cab_ref, lng_ref, lnb_ref, cbw_ref, wout_ref)
    _run_staggered(
        (_even_chunk(pl.program_id(0) * nsub + k, halos[2 * k], xc_ref.at[k], halos[2 * k + 1], *shared,
                     o_ref.at[k], apad.at[k], cpad.at[k], bgs.at[k], zs.at[k], n_chunks=n_chunks)
         for k in range(nsub)), EVEN_STAGGER)


def _even_chunk(g, xp_ref, xc_ref, xn_ref, mod_ref, ng_ref, win_ref, caw_ref, cab_ref, lng_ref, lnb_ref,
                cbw_ref, wout_ref, o_ref, apad, cpad, bgs, zs, *, n_chunks):
    c = g % n_chunks if n_chunks > 1 else 0
    rows = ROW_CHUNK + 2 * CONF_HALO
    shift1, scale1, gate1 = mod_ref[0:1, :], mod_ref[1:2, :], mod_ref[2:3, :]
    own = slice(CONF_HALO, CONF_HALO + ROW_CHUNK)
    if n_chunks == 1:
        h = _norm_mod(xc_ref[...], ng_ref[0:1, :], shift1, scale1).astype(BF16)
        h_own = h
        for pad_ref in (apad, cpad):
            pad_ref[0:CONF_HALO, :] = jnp.zeros((CONF_HALO, D_HALF), F32)
            pad_ref[CONF_HALO + ROW_CHUNK:rows, :] = jnp.zeros((CONF_HALO, D_HALF), F32)
        keep = lambda v: v
        span = own
    else:
        xh = jnp.concatenate([xp_ref[...], xc_ref[...], xn_ref[...]], axis=0)
        h = _norm_mod(xh, ng_ref[0:1, :], shift1, scale1).astype(BF16)
        h_own = h[own]
        ri = lax.broadcasted_iota(jnp.int32, (rows, D_HALF), 0)
        lo = jnp.where(c == 0, CONF_HALO, 0)
        hi = jnp.where(c == n_chunks - 1, CONF_HALO + ROW_CHUNK, rows)
        inside = (ri >= lo) & (ri < hi)
        keep = lambda v: jnp.where(inside, v, 0.0)
        span = slice(0, rows)
    a = _dot(h, win_ref[:, 0:D_HALF]) * jax.nn.sigmoid(_dot(h, win_ref[:, D_HALF:2 * D_HALF]))
    apad[span, :] = keep(a)
    yield
    cx = _dot(h, win_ref[:, 3 * D_HALF:4 * D_HALF]) * _dot(h, win_ref[:, 4 * D_HALF:5 * D_HALF])
    cpad[span, :] = keep(cx)
    yield
    bgs[...] = _dot(h_own, win_ref[:, 2 * D_HALF:3 * D_HALF])
    yield

    sub = EVEN_CONV_ROWS
    tile = 8
    for j in range(ROW_CHUNK // sub):
        r0 = j * sub
        groups = []
        for cg in range(D_HALF // LANES):
            cols = slice(cg * LANES, (cg + 1) * LANES)
            acc = None
            for r in range(tile):
                part = None
                for m in range(-(-(CONF_WIDTH + 1) // tile)):
                    o = tile * m + r
                    if 1 <= o <= CONF_WIDTH:
                        term = caw_ref[o - 1:o, cols] * apad[r0 + tile * m:r0 + tile * m + sub + tile, cols]
                        part = term if part is None else part + term
                shifted = part[r:r + sub, :]
                acc = shifted if acc is None else acc + shifted
            groups.append(acc)
        acc = jnp.concatenate(groups, axis=1) + cab_ref[...]
        mu = jnp.mean(acc, axis=-1, keepdims=True)
        dlt = acc - mu
        var = jnp.mean(dlt * dlt, axis=-1, keepdims=True)
        a_out = jax.nn.silu(dlt * lax.rsqrt(var + EPS) * lng_ref[...] + lnb_ref[...])
        zs[r0:r0 + sub, 0:D_HALF] = a_out.astype(BF16)
        base = r0 + CONF_HALO - 1
        sc = (cbw_ref[0:1, :] * cpad[base:base + sub, :]
              + cbw_ref[1:2, :] * cpad[base + 1:base + 1 + sub, :]
              + cbw_ref[2:3, :] * cpad[base + 2:base + 2 + sub, :])
        zs[r0:r0 + sub, D_HALF:D_MODEL] = (bgs[r0:r0 + sub, :] * sc).astype(BF16)
        yield

    o = _dot(zs[...], wout_ref[...])
    o_ref[...] = xc_ref[...] + gate1 * _rms(o, ng_ref[1:2, :])
    yield


def _even_layer(x, modv, layer, cond_base, cond_stride, ng, win, caw, cab, lng, lnb, cbw, wout):
    j = layer // 2
    b, t, _ = x.shape
    n_chunks = t // ROW_CHUNK
    nsub = EVEN_CHUNKS
    assert (b * n_chunks) % nsub == 0 and (cond_stride == 0 or n_chunks % nsub == 0)
    hpc = ROW_CHUNK // CONF_HALO
    n_halo_blocks = b * t // CONF_HALO
    rows = ROW_CHUNK + 2 * CONF_HALO
    halo_specs = []
    for k in range(nsub):
        halo_specs += [
            pl.BlockSpec((None, CONF_HALO, D_MODEL),
                         lambda i, k=k: (jnp.maximum((i * nsub + k) * hpc - 1, 0), 0, 0)),
            pl.BlockSpec((None, CONF_HALO, D_MODEL),
                         lambda i, k=k: (jnp.minimum((i * nsub + k + 1) * hpc, n_halo_blocks - 1), 0, 0)),
        ]
    x_halo = x.reshape(n_halo_blocks, CONF_HALO, D_MODEL)
    x_chunks = x.reshape(b * n_chunks, ROW_CHUNK, D_MODEL)
    kern = functools.partial(_even_kernel, n_chunks=n_chunks, nsub=nsub)
    out = pl.pallas_call(
        kern,
        grid=(b * n_chunks // nsub,),
        in_specs=halo_specs + [
            pl.BlockSpec((nsub, ROW_CHUNK, D_MODEL), lambda i: (i, 0, 0)),
            _mod_spec(layer, lambda i: cond_base + cond_stride * ((i * nsub) // n_chunks)),
            _layer_spec((4, D_MODEL), layer),
            _layer_spec((D_MODEL, 5 * D_HALF), j),
            _layer_spec((CONF_WIDTH, D_HALF), j),
            _layer_spec((1, D_HALF), j),
            _layer_spec((1, D_HALF), j),
            _layer_spec((1, D_HALF), j),
            _layer_spec((3, D_HALF), j),
            _layer_spec((D_MODEL, D_MODEL), j),
        ],
        out_specs=pl.BlockSpec((nsub, ROW_CHUNK, D_MODEL), lambda i: (i, 0, 0)),
        out_shape=jax.ShapeDtypeStruct(x_chunks.shape, F32),
        scratch_shapes=[
            pltpu.VMEM((nsub, rows, D_HALF), F32),
            pltpu.VMEM((nsub, rows, D_HALF), F32),
            pltpu.VMEM((nsub, ROW_CHUNK, D_HALF), F32),
            pltpu.VMEM((nsub, ROW_CHUNK, D_MODEL), BF16),
        ],
        compiler_params=_params(1),
        name="even_mixer",
    )(*([x_halo] * (2 * nsub)), x_chunks, modv, ng, win, caw, cab, lng, lnb, cbw, wout)
    return out.reshape(b, t, D_MODEL)


def _mlp_kernel(xp_ref, xs_ref, mod_ref, ng_ref, w1_hbm, w2_hbm, op_ref, os_ref, w1_s, w2_s, stage, sem, *,
                layer, n_ctx_tiles):
    n_blocks = D_FF // D_MODEL

    def block_copy(k):
        c, slot = k // 2, k % 2
        span = pl.ds(c * D_MODEL, D_MODEL)
        src = w1_hbm.at[layer, :, span] if k % 2 == 0 else w2_hbm.at[layer, span, :]
        return pltpu.make_async_copy(src, stage.at[slot], sem.at[slot])

    def fetch(k, dst):
        block_copy(k).wait()
        dst[...] = stage[k % 2].astype(BF16)
        if k + 2 < 2 * n_blocks:
            block_copy(k + 2).start()

    def rows_program(x_ref, o_ref, rows, load):
        x = x_ref[rows, :]
        h = _norm_mod(x, ng_ref[2:3, :], mod_ref[3:4, :], mod_ref[4:5, :]).astype(BF16)
        yield
        acc = None
        for c in range(n_blocks):
            cols = slice(c * D_MODEL, (c + 1) * D_MODEL)
            if load:
                fetch(2 * c, w1_s.at[:, cols])
            hid = jnp.square(jnp.maximum(_dot(h, w1_s[:, cols]), 0.0)).astype(BF16)
            yield
            if load:
                fetch(2 * c + 1, w2_s.at[cols, :])
            part = _dot(hid, w2_s[cols, :])
            acc = part if acc is None else acc + part
            yield
        o_ref[rows, :] = x + mod_ref[5:6, :] * _rms(acc, ng_ref[3:4, :])
        yield

    def tile(x_ref, o_ref, load=False):
        n_rows = x_ref.shape[0]
        if load:
            _drain(rows_program(x_ref, o_ref, slice(0, n_rows), True))
        else:
            _run_staggered((rows_program(x_ref, o_ref, slice(r0, r0 + n_rows // 2), False)
                            for r0 in (0, n_rows // 2)), MLP_STAGGER)

    i = pl.program_id(0)

    @pl.when(i == 0)
    def _():
        block_copy(0).start()
        block_copy(1).start()
        tile(xp_ref, op_ref, load=True)

    @pl.when((i > 0) & (i < n_ctx_tiles))
    def _():
        tile(xp_ref, op_ref)

    @pl.when(i >= n_ctx_tiles)
    def _():
        tile(xs_ref, os_ref)


def _mlp_layer(xp, xs, modv, layer, ng, w1, w2):
    tm = MLP_ROWS
    xp2, xs2 = xp.reshape(-1, D_MODEL), xs.reshape(-1, D_MODEL)
    t_dec = xs.shape[1]
    assert xp2.shape[0] % tm == 0 and t_dec % tm == 0
    n_p, n_s, per_seq = xp2.shape[0] // tm, xs2.shape[0] // tm, t_dec // tm
    ctx_tile = lambda i: (jnp.minimum(i, n_p - 1), 0)
    lat_tile = lambda i: (jnp.maximum(i - n_p, 0), 0)
    yp, ys = pl.pallas_call(
        functools.partial(_mlp_kernel, layer=layer, n_ctx_tiles=n_p),
        grid=(n_p + n_s,),
        in_specs=[
            pl.BlockSpec((tm, D_MODEL), ctx_tile),
            pl.BlockSpec((tm, D_MODEL), lat_tile),
            _mod_spec(layer, lambda i: jnp.where(i < n_p, 0, 1 + jnp.maximum(i - n_p, 0) // per_seq)),
            _layer_spec((4, D_MODEL), layer),
            pl.BlockSpec(memory_space=pl.ANY),
            pl.BlockSpec(memory_space=pl.ANY),
        ],
        out_specs=[pl.BlockSpec((tm, D_MODEL), ctx_tile), pl.BlockSpec((tm, D_MODEL), lat_tile)],
        out_shape=[jax.ShapeDtypeStruct(xp2.shape, F32), jax.ShapeDtypeStruct(xs2.shape, F32)],
        scratch_shapes=[
            pltpu.VMEM((D_MODEL, D_FF), BF16),
            pltpu.VMEM((D_FF, D_MODEL), BF16),
            pltpu.VMEM((2, D_MODEL, D_MODEL), F32),
            pltpu.SemaphoreType.DMA((2,)),
        ],
        compiler_params=_params(1),
        name="mlp",
    )(xp2, xs2, modv, ng, w1, w2)
    return yp.reshape(xp.shape), ys.reshape(xs.shape)


def _log_sigmoid(x):
    return jnp.minimum(x, 0.0) - jnp.log(1.0 + jnp.exp(-jnp.abs(x)))


def _rot_half(x, first_half):
    return jnp.where(first_half, pltpu.roll(x, 96, axis=1), pltpu.roll(x, 32, axis=1))


def _values_and_ones(v):
    vt = jnp.transpose(v)
    row = lax.broadcasted_iota(jnp.int32, vt.shape, 0)
    return jnp.where(row < HEAD_DIM, vt, 1.0)


def _aligned(i, m):
    return i * m if isinstance(i, int) else pl.multiple_of(i * m, m)


def _drain(pieces):
    for _ in pieces:
        pass


def _each(body, n, static, unroll=1):
    if static:
        for i in range(n):
            yield from body(i)
    else:
        lax.fori_loop(0, n, lambda i, c: (_drain(body(i)), c)[1], 0, unroll=unroll)


def _alternate(*programs):
    live = list(programs)
    while live:
        for g in list(live):
            try:
                next(g)
            except StopIteration:
                live.remove(g)
            else:
                yield


def _odd_kernel(*refs, t, lc, latent, nseq):
    shared = set(range(1, 11 if latent else 9))
    n_rows = min(t, ODD_ROWS)
    joint = not latent and nseq > 1 and t == n_rows
    finals = []
    programs = [_odd_seq(*[r if k in shared else r.at[sq] for k, r in enumerate(refs)],
                         t=t, lc=lc, latent=latent, joint=joint, finals=finals) for sq in range(nseq)]
    if joint:
        parts = [next(g) for g in programs]
        h_all = jnp.concatenate([proj_h(0) for proj_h, _ in parts], axis=0)
        for idx, (weight, _) in enumerate(parts[0][1]):
            y = _dot(h_all, weight())
            for sq, (_, plan) in enumerate(parts):
                plan[idx][1](y[sq * n_rows:(sq + 1) * n_rows], 0)
    _run_staggered(programs, ODD_STAGGER)
    if joint:
        o = _dot(jnp.concatenate([z() for z, _ in finals], axis=0), refs[8][...])
        for sq, (_, residual) in enumerate(finals):
            residual(o[sq * n_rows:(sq + 1) * n_rows])


def _odd_seq(*refs, t, lc, latent, joint, finals):
    if latent:
        (x_ref, mod_ref, ng_ref, win_ref, wg_ref, gb_ref, sink_ref, hn_ref, wout_ref,
         cos_ref, sin_ref, kc_ref, vc_ref, cin_ref, nin_ref, min_ref,
         o_ref,
         qa_s, qb_s, kk_s, vvt_s, qma_s, qmb_s, qm_s, km_s, vmt_s, vmtf_s, om_s, g_s, z_s, hft_s, hbt_s,
         sr_s, st_s, qc_s, c_s, n_s, m_s, s_s, p_s, kc_s, vct_s) = refs
    else:
        (x_ref, mod_ref, ng_ref, win_ref, wg_ref, gb_ref, sink_ref, hn_ref, wout_ref,
         o_ref, ko_ref, vo_ref, cf_ref, cb_ref, no_ref, mo_ref,
         qa_s, qb_s, kk_s, vvt_s, qma_s, qmb_s, qm_s, km_s, vmt_s, vmtf_s, om_s, g_s, z_s, hft_s, hbt_s,
         sr_s, st_s, qc_s, c_s, n_s, m_s, s_s, p_s) = refs

    static = not latent
    n_blocks = t // CHUNK
    pad = CHUNK if latent else 0
    shift1, scale1, gate1 = mod_ref[0:1, :], mod_ref[1:2, :], mod_ref[2:3, :]
    lane = lax.broadcasted_iota(jnp.int32, (1, LANES), 1)
    left = lane < HEAD_DIM
    first_half = (lane % HEAD_DIM) < (HEAD_DIM // 2)
    ti = lax.broadcasted_iota(jnp.int32, (CHUNK, CHUNK), 0)
    si = lax.broadcasted_iota(jnp.int32, (CHUNK, CHUNK), 1)
    top = ti < HEAD_DIM
    same_head = top == (si < HEAD_DIM)

    n_rows = min(t, ODD_ROWS)
    if latent:
        for kv in range(N_KV):
            kk_s[kv, 0:CHUNK, :] = jnp.zeros((CHUNK, LANES), BF16)
            kk_s[kv, CHUNK + t:2 * CHUNK + t, :] = jnp.zeros((CHUNK, LANES), BF16)
            vvt_s[kv, :, 0:CHUNK] = jnp.zeros((LANES, CHUNK), BF16)
            vvt_s[kv, :, CHUNK + t:2 * CHUNK + t] = jnp.zeros((LANES, CHUNK), BF16)
            kc_s[kv] = kc_ref[kv].astype(BF16)
            vct_s[kv] = _values_and_ones(vc_ref[kv]).astype(BF16)

    base = D_HALF + 2 * LANES

    def proj_rows(rc):
        r0 = _aligned(rc, n_rows)
        return pl.ds(r0, n_rows), pl.ds(r0 + pad, n_rows)

    def proj_h(rc):
        rows, _ = proj_rows(rc)
        return _norm_mod(x_ref[rows, :], ng_ref[0:1, :], shift1, scale1).astype(BF16)

    def rope(v, rows):
        return v * cos_ref[rows, :] + _rot_half(v, first_half) * sin_ref[rows, :]

    def use_q(q, rc):
        rows, _ = proj_rows(rc)
        for p in range(N_PAIRS):
            cols = slice(p * LANES, (p + 1) * LANES)
            qp = q[:, cols]
            if latent:
                qp = rope(qp, rows)
            qp = qp * ATT_SCALE
            qa_s[rows, cols] = jnp.where(left, qp, 0.0).astype(BF16)
            qb_s[rows, cols] = jnp.where(left, 0.0, qp).astype(BF16)

    def use_kv(kv2, rc):
        rows, krows = proj_rows(rc)
        ka, va = kv2[:, 0:LANES], kv2[:, LANES:2 * LANES]
        if latent:
            ka = rope(ka, rows)
        kr = pltpu.roll(ka, HEAD_DIM, axis=1)
        vr = pltpu.roll(va, HEAD_DIM, axis=1)
        if not latent:
            ka_t, va_t = jnp.transpose(ka), jnp.transpose(va)
            for kv in range(N_KV):
                ko_ref[0, kv, :, rows] = ka_t[kv * HEAD_DIM:(kv + 1) * HEAD_DIM, :]
                vo_ref[0, kv, :, rows] = va_t[kv * HEAD_DIM:(kv + 1) * HEAD_DIM, :]
        kk_s[0, krows, :] = jnp.where(left, ka, kr).astype(BF16)
        kk_s[1, krows, :] = jnp.where(left, kr, ka).astype(BF16)
        vvt_s[0, :, krows] = _values_and_ones(va).astype(BF16)
        vvt_s[1, :, krows] = _values_and_ones(vr).astype(BF16)

    def use_qm(qm, rc):
        rows, _ = proj_rows(rc)
        qm_s[rows, :] = qm.astype(BF16)
        for p in range(N_PAIRS):
            cols = slice(p * LANES, (p + 1) * LANES)
            qma_s[rows, cols] = jnp.where(left, qm[:, cols], 0.0).astype(BF16)
            qmb_s[rows, cols] = jnp.where(left, 0.0, qm[:, cols]).astype(BF16)

    def use_vm(vm, rc):
        rows, _ = proj_rows(rc)
        for p in range(N_PAIRS):
            cols = slice(p * LANES, (p + 1) * LANES)
            vt = jnp.transpose(vm[:, cols])
            vmtf_s[cols, rows] = vt
            vmt_s[cols, rows] = vt.astype(BF16)

    def use_km(km, rc):
        km_s[proj_rows(rc)[0], :] = (km * (HEAD_DIM ** -0.5)).astype(BF16)

    def use_om(om, rc):
        om_s[proj_rows(rc)[0], :] = om

    def use_g(g, rc):
        g_s[proj_rows(rc)[0], :] = g + gb_ref[...]

    def columns(start, width):
        return lambda: win_ref[:, start:start + width]

    proj_plan = [(columns(0, D_HALF), use_q), (columns(D_HALF, 2 * LANES), use_kv),
                 (columns(base, D_HALF), use_qm), (columns(base + 2 * D_HALF, D_HALF), use_vm),
                 (columns(base + D_HALF, D_HALF), use_km), (columns(base + 3 * D_HALF, D_HALF), use_om),
                 (lambda: wg_ref[...], use_g)]

    def project(rc):
        h = proj_h(rc)
        for weight, use in proj_plan:
            use(_dot(h, weight()), rc)
            yield

    if joint:
        yield proj_h, proj_plan
    else:
        yield from _each(project, t // n_rows, static)

    kj = lax.broadcasted_iota(jnp.int32, (3 * CHUNK, 2 * CHUNK), 0)
    qi = lax.broadcasted_iota(jnp.int32, (3 * CHUNK, 2 * CHUNK), 1) % CHUNK
    band_ok = jnp.abs(kj - CHUNK - qi) <= WINDOW
    head_a = lax.broadcasted_iota(jnp.int32, (1, 2 * CHUNK), 1) < CHUNK

    n_keys = _att_keys(lc, latent)
    group = _att_group(latent)

    def attention():
        def attend(i):
            r0 = _aligned(i, CHUNK)
            rows = pl.ds(r0, CHUNK)
            if latent:
                key_pos = kj + (i - 1) * CHUNK
                mask = band_ok & (key_pos >= 0) & (key_pos < t)
                win = pl.ds(r0, 3 * CHUNK)
            for g0 in range(0, N_PAIRS, group):
                for gi in range(group):
                    p = g0 + gi
                    kv = p // (N_PAIRS // N_KV)
                    cols = slice(p * LANES, (p + 1) * LANES)
                    q2 = jnp.concatenate([qa_s[rows, cols], qb_s[rows, cols]], axis=0)
                    if latent:
                        s_s[gi, 0:lc, :] = _dot_nt(kc_s[kv], q2)
                        s_s[gi, lc:n_keys, :] = jnp.where(mask, _dot_nt(kk_s[kv, win, :], q2), -jnp.inf)
                    else:
                        s_s[gi] = _dot_nt(kk_s[kv], q2)
                yield
                maxes = []
                for gi in range(group):
                    p = g0 + gi
                    sink = jnp.where(head_a, sink_ref[2 * p:2 * p + 1, 0:1], sink_ref[2 * p + 1:2 * p + 2, 0:1])
                    mx = jnp.maximum(jnp.max(s_s[gi], axis=0, keepdims=True), sink)
                    p_s[gi] = jnp.exp(s_s[gi] - mx).astype(BF16)
                    maxes.append((sink, mx))
                yield
                for gi in range(group):
                    p = g0 + gi
                    kv = p // (N_PAIRS // N_KV)
                    cols = slice(p * LANES, (p + 1) * LANES)
                    if latent:
                        num = (_dot(vct_s[kv], p_s[gi, 0:lc, :]) + _dot(vvt_s[kv, :, win], p_s[gi, lc:n_keys, :]))
                    else:
                        num = _dot(vvt_s[kv], p_s[gi])
                    sink, mx = maxes[gi]
                    den = num[HEAD_DIM:HEAD_DIM + 1, :] + jnp.exp(sink - mx)
                    out = num[0:HEAD_DIM, :] * (1.0 / den)
                    pair = jnp.concatenate([out[:, 0:CHUNK], out[:, CHUNK:2 * CHUNK]], axis=0)
                    z_s[rows, cols] = jnp.transpose(pair).astype(BF16)
                yield

        return attend

    if latent:
        c_s[...] = cin_ref[...]
        n_s[...] = nin_ref[...]
        m_s[...] = min_ref[...]
    else:
        c_s[...] = jnp.zeros(c_s.shape, F32)
        n_s[...] = jnp.zeros(n_s.shape, F32)
        m_s[...] = jnp.zeros(m_s.shape, F32)

    see = (ti <= si, ti >= si)
    tri = tuple(m.astype(F32).astype(BF16) for m in see)
    last = (CHUNK - 1, 0)

    def mlstm(i):
        offs = (_aligned(i, CHUNK), _aligned(n_blocks - 1 - i, CHUNK))
        half = (slice(0, LANES), slice(LANES, 2 * LANES))
        a_rows, b_rows = [], []
        for d in range(2):
            gt = jnp.transpose(g_s[pl.ds(offs[d], CHUNK), :])
            lf = _log_sigmoid(gt[2 * N_HEADS:4 * N_HEADS, :])
            bcum = _dot_exact_lhs(lf, tri[d])[8 * d:8 * d + 8, :]
            a_rows.append(gt[8 * d:8 * d + 8, :] - bcum)
            b_rows.append(bcum)
        yield
        q_ns = []
        for d in range(2):
            rows = pl.ds(offs[d], CHUNK)
            for p in range(N_PAIRS):
                u = d * N_PAIRS + p
                cols = slice(p * LANES, (p + 1) * LANES)
                q2 = jnp.concatenate([qma_s[rows, cols], qmb_s[rows, cols]], axis=0)
                sr_s[u] = _dot_nt(km_s[rows, cols], q2)
                qc_s[u] = _dot_nt(c_s[u].astype(BF16), qm_s[rows, cols])
                n8 = jnp.broadcast_to(n_s[u:u + 1, :], (8, LANES)).astype(BF16)
                q_ns.append(_dot_nt(n8, q2)[0:1, :])
            yield
        stats = {}
        for d in range(2):
            for hd in range(N_HEADS):
                u, j = d * N_PAIRS + hd // 2, hd % 2
                mrow = d * N_HEADS + hd
                a_row = a_rows[d][hd:hd + 1, :]
                b_row = b_rows[d][hd:hd + 1, :]
                m_prev = m_s[mrow:mrow + 1, 0:1]
                a_col = jnp.transpose(jnp.broadcast_to(a_row, (CHUNK, CHUNK)))
                z_t = jnp.where(see[d], a_col, -jnp.inf)
                m_run = jnp.maximum(jnp.max(z_t, axis=0, keepdims=True), m_prev)
                s_t = sr_s[u, :, half[j]] * jnp.exp(z_t - m_run)
                st_s[u, :, half[j]] = s_t.astype(BF16)
                w_int = jnp.exp(m_prev - m_run)
                den = jnp.sum(s_t, axis=0, keepdims=True) + w_int * q_ns[u][:, half[j]]
                inv = 1.0 / jnp.maximum(jnp.abs(den), jnp.exp(-(b_row + m_run)))
                m_last = m_run[:, last[d]:last[d] + 1]
                stats[(u, j)] = (w_int, inv, jnp.exp(a_row - m_last), jnp.exp(m_prev - m_last))
                m_s[mrow:mrow + 1, :] = jnp.broadcast_to(b_row[:, last[d]:last[d] + 1] + m_last, (1, LANES))
            yield
        for d in range(2):
            rows = pl.ds(offs[d], CHUNK)
            ht_s = (hft_s, hbt_s)[d]
            for p in range(N_PAIRS):
                u = d * N_PAIRS + p
                cols = slice(p * LANES, (p + 1) * LANES)
                (w_a, inv_a, e_a, dec_a), (w_b, inv_b, e_b, dec_b) = stats[(u, 0)], stats[(u, 1)]
                kp = km_s[rows, cols]
                num2 = _dot(vmt_s[cols, rows], st_s[u])
                num = jnp.where(top, num2[:, half[0]], num2[:, half[1]])
                ht_s[cols, rows] = (num + jnp.where(top, w_a, w_b) * qc_s[u]) * jnp.where(top, inv_a, inv_b)
                vt_e = (vmtf_s[cols, rows] * jnp.where(top, e_a, e_b)).astype(BF16)
                c_s[u] = jnp.where(top, dec_a, dec_b) * c_s[u] + jnp.where(same_head, _dot(vt_e, kp), 0.0)
                e2 = jnp.concatenate([e_a, e_b, jnp.zeros((6, CHUNK), F32)], axis=0).astype(BF16)
                n_k = _dot(e2, kp)
                n_s[u:u + 1, :] = (jnp.where(left, dec_a, dec_b) * n_s[u:u + 1, :]
                                   + jnp.where(left, n_k[0:1, :], n_k[1:2, :]))
            yield

    attend = attention()
    yield from _each(lambda i: _alternate(attend(i), mlstm(i)), n_blocks, static, unroll=2)

    if not latent:
        for d, c_ref in enumerate((cf_ref, cb_ref)):
            for p in range(N_PAIRS):
                c_pair = jnp.transpose(c_s[d * N_PAIRS + p])
                c_ref[0, 2 * p] = c_pair[0:HEAD_DIM, 0:HEAD_DIM]
                c_ref[0, 2 * p + 1] = c_pair[HEAD_DIM:LANES, HEAD_DIM:LANES]
        no_ref[...] = n_s[...]
        mo_ref[...] = m_s[...]
        yield

    top_w = lax.broadcasted_iota(jnp.int32, (LANES, n_rows), 0) < HEAD_DIM

    def finish(rc):
        r0 = _aligned(rc, n_rows)
        rows = pl.ds(r0, n_rows)
        for p in range(N_PAIRS):
            cols = slice(p * LANES, (p + 1) * LANES)
            hm = hft_s[cols, rows] + hbt_s[cols, rows]
            sq = hm * hm
            ms_a = jnp.sum(sq[0:HEAD_DIM], axis=0, keepdims=True)
            ms_b = jnp.sum(sq[HEAD_DIM:LANES], axis=0, keepdims=True)
            ms = jnp.where(top_w, ms_a, ms_b) * (1.0 / HEAD_DIM)
            y = jnp.transpose(hm * lax.rsqrt(ms + EPS)) * hn_ref[:, cols] * jax.nn.sigmoid(om_s[rows, cols])
            z_s[rows, D_HALF + p * LANES:D_HALF + (p + 1) * LANES] = y.astype(BF16)
            if p % 2:
                yield
        def residual(o):
            o_ref[rows, :] = x_ref[rows, :] + gate1 * _rms(o, ng_ref[1:2, :])

        if joint:
            finals.append((lambda: z_s[rows, :], residual))
        else:
            residual(_dot(z_s[rows, :], wout_ref[...]))
        yield

    yield from _each(finish, t // n_rows, static)


def _att_keys(lc, latent):
    return lc + 3 * CHUNK if latent else lc


def _att_group(latent):
    return 2 if latent else N_PAIRS


def _odd_scratch(t, lc, latent, nseq):
    pad = 2 * CHUNK if latent else 0
    att = (_att_group(latent), _att_keys(lc, latent), 2 * CHUNK)
    shapes = [
        pltpu.VMEM((t, D_HALF), BF16),
        pltpu.VMEM((t, D_HALF), BF16),
        pltpu.VMEM((N_KV, t + pad, LANES), BF16),
        pltpu.VMEM((N_KV, LANES, t + pad), BF16),
        pltpu.VMEM((t, D_HALF), BF16),
        pltpu.VMEM((t, D_HALF), BF16),
        pltpu.VMEM((t, D_HALF), BF16),
        pltpu.VMEM((t, D_HALF), BF16),
        pltpu.VMEM((D_HALF, t), BF16),
        pltpu.VMEM((D_HALF, t), F32),
        pltpu.VMEM((t, D_HALF), F32),
        pltpu.VMEM((t, LANES), F32),
        pltpu.VMEM((t, D_MODEL), BF16),
        pltpu.VMEM((D_HALF, t), F32),
        pltpu.VMEM((D_HALF, t), F32),
        pltpu.VMEM((2 * N_PAIRS, CHUNK, 2 * LANES), F32),
        pltpu.VMEM((2 * N_PAIRS, CHUNK, 2 * LANES), BF16),
        pltpu.VMEM((2 * N_PAIRS, LANES, CHUNK), F32),
        pltpu.VMEM((2 * N_PAIRS, LANES, LANES), F32),
        pltpu.VMEM((2 * N_PAIRS, LANES), F32),
        pltpu.VMEM((2 * N_HEADS, LANES), F32),
        pltpu.VMEM(att, F32),
        pltpu.VMEM(att, BF16),
    ]
    if latent:
        shapes += [pltpu.VMEM((N_KV, lc, LANES), BF16), pltpu.VMEM((N_KV, LANES, lc), BF16)]
    return [pltpu.VMEM((nseq,) + tuple(sh.shape), sh.dtype) for sh in shapes]


def _odd_common_specs(t, layer, cond_base, cond_stride, nseq):
    assert cond_stride == 0 or nseq == 1
    j = layer // 2
    return [
        _per_seq((t, D_MODEL), nseq),
        _mod_spec(layer, lambda i: cond_base + cond_stride * i),
        _layer_spec((4, D_MODEL), layer),
        _layer_spec((D_MODEL, D_IN_ODD), j),
        _const_spec((D_MODEL, LANES)),
        _const_spec((1, LANES)),
        _const_spec((N_HEADS, LANES)),
        _layer_spec((1, D_HALF), j),
        _layer_spec((D_MODEL, D_MODEL), j),
    ]


def _per_seq(shape, nseq):
    return pl.BlockSpec((nseq,) + shape, lambda i: (i,) + (0,) * len(shape))


def _odd_context(x, modv, layer, ng, w_main, w_gate, gate_bias, sink_b, hnorm, wout):
    b, t, _ = x.shape
    nseq = ODD_CTX_SEQS
    assert b % nseq == 0
    kern = functools.partial(_odd_kernel, t=t, lc=t, latent=False, nseq=nseq)
    per_seq = functools.partial(_per_seq, nseq=nseq)
    return pl.pallas_call(
        kern,
        grid=(b // nseq,),
        in_specs=_odd_common_specs(t, layer, 0, 0, nseq),
        out_specs=[per_seq((t, D_MODEL)),
                   per_seq((1, N_KV, HEAD_DIM, t)), per_seq((1, N_KV, HEAD_DIM, t)),
                   per_seq((1, N_HEADS, HEAD_DIM, HEAD_DIM)), per_seq((1, N_HEADS, HEAD_DIM, HEAD_DIM)),
                   per_seq((2 * N_PAIRS, LANES)), per_seq((2 * N_HEADS, LANES))],
        out_shape=[jax.ShapeDtypeStruct((b, t, D_MODEL), F32),
                   jax.ShapeDtypeStruct((b, 1, N_KV, HEAD_DIM, t), F32),
                   jax.ShapeDtypeStruct((b, 1, N_KV, HEAD_DIM, t), F32),
                   jax.ShapeDtypeStruct((b, 1, N_HEADS, HEAD_DIM, HEAD_DIM), F32),
                   jax.ShapeDtypeStruct((b, 1, N_HEADS, HEAD_DIM, HEAD_DIM), F32),
                   jax.ShapeDtypeStruct((b, 2 * N_PAIRS, LANES), F32),
                   jax.ShapeDtypeStruct((b, 2 * N_HEADS, LANES), F32)],
        scratch_shapes=_odd_scratch(t, t, False, nseq),
        compiler_params=_params(1),
        name="odd_mixer_context",
    )(x, modv, ng, w_main, w_gate, gate_bias, sink_b, hnorm, wout)


def _odd_latent(x, modv, layer, ng, w_main, w_gate, gate_bias, sink_b, hnorm, wout, cos_t, sin_t, kc, vc,
                c_in, n_in, m_in):
    b, t, _ = x.shape
    lc = kc.shape[2]
    kern = functools.partial(_odd_kernel, t=t, lc=lc, latent=True, nseq=1)
    per_seq = functools.partial(_per_seq, nseq=1)
    return pl.pallas_call(
        kern,
        grid=(b,),
        in_specs=_odd_common_specs(t, layer, 1, 1, 1) + [
            _const_spec((t, LANES)), _const_spec((t, LANES)),
            per_seq((N_KV, lc, LANES)), per_seq((N_KV, lc, LANES)),
            per_seq((2 * N_PAIRS, LANES, LANES)), per_seq((2 * N_PAIRS, LANES)), per_seq((2 * N_HEADS, LANES)),
        ],
        out_specs=per_seq((t, D_MODEL)),
        out_shape=jax.ShapeDtypeStruct((b, t, D_MODEL), F32),
        scratch_shapes=_odd_scratch(t, lc, True, 1),
        compiler_params=_params(1),
        name="odd_mixer_latent",
    )(x, modv, ng, w_main, w_gate, gate_bias, sink_b, hnorm, wout, cos_t, sin_t, kc, vc, c_in, n_in, m_in)


def _rope_tables(t):
    rows = t // GRID_W
    row = jnp.broadcast_to(jnp.arange(rows)[:, None], (rows, GRID_W)).reshape(t).astype(F32)
    col = jnp.broadcast_to(jnp.arange(GRID_W)[None, :], (rows, GRID_W)).reshape(t).astype(F32)
    n_freq = HEAD_DIM // 4
    inv_freq = ROPE_BASE ** (-jnp.arange(n_freq, dtype=F32) / n_freq)
    ang = jnp.concatenate([row[:, None] * inv_freq, col[:, None] * inv_freq], axis=-1)
    cos, sin = jnp.cos(ang), jnp.sin(ang)
    cos_l = jnp.tile(cos, (1, LANES // cos.shape[1]))
    sin_l = jnp.tile(jnp.concatenate([-sin, sin], axis=-1), (1, LANES // HEAD_DIM))
    return cos_l, sin_l


def _pair_blockdiag(c):
    b = c.shape[0]
    c = c.reshape(b, N_PAIRS, 2, HEAD_DIM, HEAD_DIM)
    z = jnp.zeros_like(c[:, :, 0])
    top = jnp.concatenate([c[:, :, 0], z], axis=-1)
    bot = jnp.concatenate([z, c[:, :, 1]], axis=-1)
    return jnp.concatenate([top, bot], axis=-2)


def _lane_bcast(v):
    return jnp.broadcast_to(v[..., None], v.shape + (LANES,))


def kernel(x_prompt, x_sample, c, cache_k, cache_v, state_c_fwd, state_n_fwd, state_m_fwd, state_c_bwd, state_n_bwd, state_m_bwd, c_ctx, mod_w, mod_b, norm_g, mlp_w1, mlp_w2, even_in_w, conv_a_w, conv_a_b, ln_a_g, ln_a_b, conv_b_w, even_out_w, odd_in_w, attn_sink, gate_b, hnorm_g, odd_out_w):
    n_dec = x_sample.shape[0]
    n_ctx = x_prompt.shape[0]
    cond = jnp.concatenate([c_ctx[None, :], c, jnp.zeros((COND_ROWS - 1 - n_dec, D_MODEL), F32)], axis=0)
    modv = _modulation(cond, mod_w, mod_b)

    yp, ys = x_prompt, x_sample
    w1, w2 = mlp_w1, mlp_w2

    ev = (norm_g, even_in_w.astype(BF16), conv_a_w, conv_a_b[:, None, :], ln_a_g[:, None, :], ln_a_b[:, None, :],
          conv_b_w, even_out_w.astype(BF16))
    yp = _even_layer(yp, modv, 0, 0, 0, *ev)
    ys = _even_layer(ys, modv, 0, 1, 1, *ev)
    yp, ys = _mlp_layer(yp, ys, modv, 0, norm_g, w1, w2)

    order = jnp.array([0, 2, 1, 3])
    d_main = D_IN_ODD - 4 * N_HEADS
    wg = odd_in_w[0][:, d_main:].reshape(D_MODEL, 4, N_HEADS)[:, order, :].reshape(D_MODEL, 4 * N_HEADS)
    w_gate = jnp.pad(wg, ((0, 0), (0, LANES - 4 * N_HEADS))).astype(BF16)
    gate_bias = jnp.pad(gate_b[0][order, :].reshape(1, 4 * N_HEADS), ((0, 0), (0, LANES - 4 * N_HEADS)))
    sink_b = _lane_bcast(attn_sink[0])
    odd = (1, norm_g, odd_in_w.astype(BF16), w_gate, gate_bias, sink_b, hnorm_g[:, None, :], odd_out_w.astype(BF16))

    op, k_t, v_t, c_f, c_b, n_new, m_new = _odd_context(yp, modv, *odd)
    new_k, new_v = jnp.swapaxes(k_t, -1, -2), jnp.swapaxes(v_t, -1, -2)

    t_dec = x_sample.shape[1]
    cos_t, sin_t = _rope_tables(t_dec)
    kc = jnp.concatenate([cache_k[:, 0], cache_k[:, 0]], axis=-1)
    vc = jnp.concatenate([cache_v[:, 0], cache_v[:, 0]], axis=-1)
    c_in = jnp.concatenate([_pair_blockdiag(jnp.swapaxes(state_c_fwd[:, 0], -1, -2)),
                            _pair_blockdiag(jnp.swapaxes(state_c_bwd[:, 0], -1, -2))], axis=1)
    n_in = jnp.concatenate([state_n_fwd[:, 0].reshape(n_dec, N_PAIRS, LANES),
                            state_n_bwd[:, 0].reshape(n_dec, N_PAIRS, LANES)], axis=1)
    m_in = _lane_bcast(jnp.concatenate([state_m_fwd[:, 0], state_m_bwd[:, 0]], axis=1))
    os_ = _odd_latent(ys, modv, *odd, cos_t, sin_t, kc, vc, c_in, n_in, m_in)

    yp, ys = _mlp_layer(op, os_, modv, 1, norm_g, w1, w2)

    n_f = n_new[:, :N_PAIRS].reshape(n_ctx, N_HEADS, HEAD_DIM)[:, None]
    n_b = n_new[:, N_PAIRS:].reshape(n_ctx, N_HEADS, HEAD_DIM)[:, None]
    m_f = m_new[:, :N_HEADS, 0][:, None]
    m_b = m_new[:, N_HEADS:, 0][:, None]
    return (yp, ys, new_k, new_v, c_f, n_f, m_f, c_b, n_b, m_b)
```

```python
import functools

import jax
import jax.numpy as jnp
from jax import lax
from jax.experimental import pallas as pl
from jax.experimental.pallas import tpu as pltpu

F32 = jnp.float32
BF16 = jnp.bfloat16

D_MODEL = 1024
D_FF = 4 * D_MODEL
EPS = 1e-6
D_HALF = D_MODEL // 2
CONF_WIDTH = 31
CONF_HALO = 16
HEAD_DIM = 64
N_HEADS = 8
N_PAIRS = N_HEADS // 2
N_KV = 2
LANES = 128
CHUNK = 128
WINDOW = 128
GRID_W = 64
ROPE_BASE = 10000.0
ATT_SCALE = HEAD_DIM ** -0.5
D_IN_ODD = D_HALF + 2 * N_KV * HEAD_DIM + 4 * D_HALF + 4 * N_HEADS
ROW_CHUNK = 256
MLP_ROWS = 512
MLP_STAGGER = 3
ODD_ROWS = 512
ODD_CTX_SEQS = 2
ODD_STAGGER = 6
EVEN_CONV_ROWS = 64
EVEN_CHUNKS = 2
EVEN_STAGGER = 3
COND_ROWS = 8
VMEM_LIMIT = 56 * 1024 * 1024


def _dot(a, b):
    return jnp.dot(a, b, preferred_element_type=F32)


def _dot_nt(a, b):
    return lax.dot_general(a, b, (((1,), (1,)), ((), ())), preferred_element_type=F32)


def _split3(x):
    hi = x.astype(BF16)
    r1 = x - hi.astype(F32)
    mid = r1.astype(BF16)
    lo = (r1 - mid.astype(F32)).astype(BF16)
    return hi, mid, lo


def _dot_exact_lhs(x, b01):
    hi, mid, lo = _split3(x)
    return _dot(hi, b01) + _dot(mid, b01) + _dot(lo, b01)


def _rms(x, g):
    return x * lax.rsqrt(jnp.mean(x * x, axis=-1, keepdims=True) + EPS) * g


def _norm_mod(x, g, shift, scale):
    return _rms(x, g) * (1.0 + scale) + shift


def _params(n_grid):
    return pltpu.CompilerParams(dimension_semantics=("arbitrary",) * n_grid, vmem_limit_bytes=VMEM_LIMIT)


def _const_spec(shape):
    zeros = (0,) * len(shape)
    return pl.BlockSpec(shape, lambda *_: zeros, pipeline_mode=pl.Buffered(1))


def _layer_spec(shape, layer):
    index = (layer,) + (0,) * len(shape)
    return pl.BlockSpec((None,) + shape, lambda *_: index, pipeline_mode=pl.Buffered(1))


def _mod_spec(layer, cond_of):
    return pl.BlockSpec((None, None, 6, D_MODEL), lambda *idx: (layer, cond_of(*idx), 0, 0))


def _mod_kernel(cond_ref, w_ref, b_ref, o_ref):
    s = jax.nn.silu(cond_ref[...]).astype(BF16)
    o_ref[...] = _dot(s, w_ref[...].astype(BF16)) + b_ref[...]


def _modulation(cond, mod_w, mod_b):
    depth = mod_w.shape[0]
    n_out = mod_w.shape[2]
    tn = 2 * D_MODEL
    out = pl.pallas_call(
        _mod_kernel,
        grid=(depth, n_out // tn),
        in_specs=[
            pl.BlockSpec((COND_ROWS, D_MODEL), lambda l, j: (0, 0)),
            pl.BlockSpec((None, D_MODEL, tn), lambda l, j: (l, 0, j)),
            pl.BlockSpec((None, 1, tn), lambda l, j: (l, 0, j)),
        ],
        out_specs=pl.BlockSpec((None, COND_ROWS, tn), lambda l, j: (l, 0, j)),
        out_shape=jax.ShapeDtypeStruct((depth, COND_ROWS, n_out), F32),
        compiler_params=_params(2),
        name="modulation",
    )(cond, mod_w, mod_b.reshape(depth, 1, n_out))
    return out.reshape(depth, COND_ROWS, 6, D_MODEL)


def _run_staggered(programs, stagger):
    programs = list(programs)
    live, rounds = [], 0
    while programs or live:
        if programs and rounds % stagger == 0:
            live.append(programs.pop(0))
        for g in list(live):
            try:
                next(g)
            except StopIteration:
                live.remove(g)
        rounds += 1


def _even_kernel(*refs, n_chunks, nsub):
    halos, (xc_ref, mod_ref, ng_ref, win_ref, caw_ref, cab_ref, lng_ref, lnb_ref, cbw_ref, wout_ref, o_ref,
            apad, cpad, bgs, zs) = refs[:2 * nsub], refs[2 * nsub:]
    shared = (mod_ref, ng_ref, win_ref, caw_ref, cab_ref, lng_ref, lnb_ref, cbw_ref, wout_ref)
    _run_staggered(
        (_even_chunk(pl.program_id(0) * nsub + k, halos[2 * k], xc_ref.at[k], halos[2 * k + 1], *shared,
                     o_ref.at[k], apad.at[k], cpad.at[k], bgs.at[k], zs.at[k], n_chunks=n_chunks)
         for k in range(nsub)), EVEN_STAGGER)


def _even_chunk(g, xp_ref, xc_ref, xn_ref, mod_ref, ng_ref, win_ref, caw_ref, cab_ref, lng_ref, lnb_ref,
                cbw_ref, wout_ref, o_ref, apad, cpad, bgs, zs, *, n_chunks):
    c = g % n_chunks if n_chunks > 1 else 0
    rows = ROW_CHUNK + 2 * CONF_HALO
    shift1, scale1, gate1 = mod_ref[0:1, :], mod_ref[1:2, :], mod_ref[2:3, :]
    own = slice(CONF_HALO, CONF_HALO + ROW_CHUNK)
    if n_chunks == 1:
        h = _norm_mod(xc_ref[...], ng_ref[0:1, :], shift1, scale1).astype(BF16)
        h_own = h
        for pad_ref in (apad, cpad):
            pad_ref[0:CONF_HALO, :] = jnp.zeros((CONF_HALO, D_HALF), F32)
            pad_ref[CONF_HALO + ROW_CHUNK:rows, :] = jnp.zeros((CONF_HALO, D_HALF), F32)
        keep = lambda v: v
        span = own
    else:
        xh = jnp.concatenate([xp_ref[...], xc_ref[...], xn_ref[...]], axis=0)
        h = _norm_mod(xh, ng_ref[0:1, :], shift1, scale1).astype(BF16)
        h_own = h[own]
        ri = lax.broadcasted_iota(jnp.int32, (rows, D_HALF), 0)
        lo = jnp.where(c == 0, CONF_HALO, 0)
        hi = jnp.where(c == n_chunks - 1, CONF_HALO + ROW_CHUNK, rows)
        inside = (ri >= lo) & (ri < hi)
        keep = lambda v: jnp.where(inside, v, 0.0)
        span = slice(0, rows)
    a = _dot(h, win_ref[:, 0:D_HALF]) * jax.nn.sigmoid(_dot(h, win_ref[:, D_HALF:2 * D_HALF]))
    apad[span, :] = keep(a)
    yield
    cx = _dot(h, win_ref[:, 3 * D_HALF:4 * D_HALF]) * _dot(h, win_ref[:, 4 * D_HALF:5 * D_HALF])
    cpad[span, :] = keep(cx)
    yield
    bgs[...] = _dot(h_own, win_ref[:, 2 * D_HALF:3 * D_HALF])
    yield

    sub = EVEN_CONV_ROWS
    tile = 8
    for j in range(ROW_CHUNK // sub):
        r0 = j * sub
        groups = []
        for cg in range(D_HALF // LANES):
            cols = slice(cg * LANES, (cg + 1) * LANES)
            acc = None
            for r in range(tile):
                part = None
                for m in range(-(-(CONF_WIDTH + 1) // tile)):
                    o = tile * m + r
                    if 1 <= o <= CONF_WIDTH:
                        term = caw_ref[o - 1:o, cols] * apad[r0 + tile * m:r0 + tile * m + sub + tile, cols]
                        part = term if part is None else part + term
                shifted = part[r:r + sub, :]
                acc = shifted if acc is None else acc + shifted
            groups.append(acc)
        acc = jnp.concatenate(groups, axis=1) + cab_ref[...]
        mu = jnp.mean(acc, axis=-1, keepdims=True)
        dlt = acc - mu
        var = jnp.mean(dlt * dlt, axis=-1, keepdims=True)
        a_out = jax.nn.silu(dlt * lax.rsqrt(var + EPS) * lng_ref[...] + lnb_ref[...])
        zs[r0:r0 + sub, 0:D_HALF] = a_out.astype(BF16)
        base = r0 + CONF_HALO - 1
        sc = (cbw_ref[0:1, :] * cpad[base:base + sub, :]
              + cbw_ref[1:2, :] * cpad[base + 1:base + 1 + sub, :]
              + cbw_ref[2:3, :] * cpad[base + 2:base + 2 + sub, :])
        zs[r0:r0 + sub, D_HALF:D_MODEL] = (bgs[r0:r0 + sub, :] * sc).astype(BF16)
        yield

    o = _dot(zs[...], wout_ref[...])
    o_ref[...] = xc_ref[...] + gate1 * _rms(o, ng_ref[1:2, :])
    yield


def _even_layer(x, modv, layer, cond_base, cond_stride, ng, win, caw, cab, lng, lnb, cbw, wout):
    j = layer // 2
    b, t, _ = x.shape
    n_chunks = t // ROW_CHUNK
    nsub = EVEN_CHUNKS
    assert (b * n_chunks) % nsub == 0 and (cond_stride == 0 or n_chunks % nsub == 0)
    hpc = ROW_CHUNK // CONF_HALO
    n_halo_blocks = b * t // CONF_HALO
    rows = ROW_CHUNK + 2 * CONF_HALO
    halo_specs = []
    for k in range(nsub):
        halo_specs += [
            pl.BlockSpec((None, CONF_HALO, D_MODEL),
                         lambda i, k=k: (jnp.maximum((i * nsub + k) * hpc - 1, 0), 0, 0)),
            pl.BlockSpec((None, CONF_HALO, D_MODEL),
                         lambda i, k=k: (jnp.minimum((i * nsub + k + 1) * hpc, n_halo_blocks - 1), 0, 0)),
        ]
    x_halo = x.reshape(n_halo_blocks, CONF_HALO, D_MODEL)
    x_chunks = x.reshape(b * n_chunks, ROW_CHUNK, D_MODEL)
    kern = functools.partial(_even_kernel, n_chunks=n_chunks, nsub=nsub)
    out = pl.pallas_call(
        kern,
        grid=(b * n_chunks // nsub,),
        in_specs=halo_specs + [
            pl.BlockSpec((nsub, ROW_CHUNK, D_MODEL), lambda i: (i, 0, 0)),
            _mod_spec(layer, lambda i: cond_base + cond_stride * ((i * nsub) // n_chunks)),
            _layer_spec((4, D_MODEL), layer),
            _layer_spec((D_MODEL, 5 * D_HALF), j),
            _layer_spec((CONF_WIDTH, D_HALF), j),
            _layer_spec((1, D_HALF), j),
            _layer_spec((1, D_HALF), j),
            _layer_spec((1, D_HALF), j),
            _layer_spec((3, D_HALF), j),
            _layer_spec((D_MODEL, D_MODEL), j),
        ],
        out_specs=pl.BlockSpec((nsub, ROW_CHUNK, D_MODEL), lambda i: (i, 0, 0)),
        out_shape=jax.ShapeDtypeStruct(x_chunks.shape, F32),
        scratch_shapes=[
            pltpu.VMEM((nsub, rows, D_HALF), F32),
            pltpu.VMEM((nsub, rows, D_HALF), F32),
            pltpu.VMEM((nsub, ROW_CHUNK, D_HALF), F32),
            pltpu.VMEM((nsub, ROW_CHUNK, D_MODEL), BF16),
        ],
        compiler_params=_params(1),
        name="even_mixer",
    )(*([x_halo] * (2 * nsub)), x_chunks, modv, ng, win, caw, cab, lng, lnb, cbw, wout)
    return out.reshape(b, t, D_MODEL)


def _mlp_kernel(xp_ref, xs_ref, mod_ref, ng_ref, w1_hbm, w2_hbm, op_ref, os_ref, w1_s, w2_s, stage, sem, y_s, *,
                layer, n_ctx_tiles):
    n_blocks = D_FF // D_MODEL

    def block_copy(k):
        c, slot = k // 2, k % 2
        span = pl.ds(c * D_MODEL, D_MODEL)
        src = w1_hbm.at[layer, :, span] if k % 2 == 0 else w2_hbm.at[layer, span, :]
        return pltpu.make_async_copy(src, stage.at[slot], sem.at[slot])

    def fetch(k, dst):
        block_copy(k).wait()
        dst[...] = stage[k % 2].astype(BF16)
        if k + 2 < 2 * n_blocks:
            block_copy(k + 2).start()

    def rows_program(read, o_ref, rows, load):
        x = read(rows)
        h = _norm_mod(x, ng_ref[2:3, :], mod_ref[3:4, :], mod_ref[4:5, :]).astype(BF16)
        yield
        acc = None
        for c in range(n_blocks):
            cols = slice(c * D_MODEL, (c + 1) * D_MODEL)
            if load:
                fetch(2 * c, w1_s.at[:, cols])
            hid = jnp.square(jnp.maximum(_dot(h, w1_s[:, cols]), 0.0)).astype(BF16)
            yield
            if load:
                fetch(2 * c + 1, w2_s.at[cols, :])
            part = _dot(hid, w2_s[cols, :])
            acc = part if acc is None else acc + part
            yield
        o_ref[rows, :] = x + mod_ref[5:6, :] * _rms(acc, ng_ref[3:4, :])
        yield

    i = pl.program_id(0)
    is_ctx = i < n_ctx_tiles
    n_rows = xp_ref.shape[0]

    @pl.when(i == 0)
    def _():
        block_copy(0).start()
        block_copy(1).start()
        _drain(rows_program(lambda rows: xp_ref[rows, :], op_ref, slice(0, n_rows), True))

    @pl.when(i > 0)
    def _():
        read = lambda rows: jnp.where(is_ctx, xp_ref[rows, :], xs_ref[rows, :])
        _run_staggered((rows_program(read, y_s, slice(r0, r0 + n_rows // 2), False)
                        for r0 in (0, n_rows // 2)), MLP_STAGGER)

    @pl.when((i > 0) & is_ctx)
    def _():
        op_ref[...] = y_s[...]

    @pl.when(jnp.logical_not(is_ctx))
    def _():
        os_ref[...] = y_s[...]


def _mlp_layer(xp, xs, modv, layer, ng, w1, w2):
    tm = MLP_ROWS
    xp2, xs2 = xp.reshape(-1, D_MODEL), xs.reshape(-1, D_MODEL)
    t_dec = xs.shape[1]
    assert xp2.shape[0] % tm == 0 and t_dec % tm == 0
    n_p, n_s, per_seq = xp2.shape[0] // tm, xs2.shape[0] // tm, t_dec // tm
    ctx_tile = lambda i: (jnp.minimum(i, n_p - 1), 0)
    lat_tile = lambda i: (jnp.maximum(i - n_p, 0), 0)
    yp, ys = pl.pallas_call(
        functools.partial(_mlp_kernel, layer=layer, n_ctx_tiles=n_p),
        grid=(n_p + n_s,),
        in_specs=[
            pl.BlockSpec((tm, D_MODEL), ctx_tile),
            pl.BlockSpec((tm, D_MODEL), lat_tile),
            _mod_spec(layer, lambda i: jnp.where(i < n_p, 0, 1 + jnp.maximum(i - n_p, 0) // per_seq)),
            _layer_spec((4, D_MODEL), layer),
            pl.BlockSpec(memory_space=pl.ANY),
            pl.BlockSpec(memory_space=pl.ANY),
        ],
        out_specs=[pl.BlockSpec((tm, D_MODEL), ctx_tile), pl.BlockSpec((tm, D_MODEL), lat_tile)],
        out_shape=[jax.ShapeDtypeStruct(xp2.shape, F32), jax.ShapeDtypeStruct(xs2.shape, F32)],
        scratch_shapes=[
            pltpu.VMEM((D_MODEL, D_FF), BF16),
            pltpu.VMEM((D_FF, D_MODEL), BF16),
            pltpu.VMEM((2, D_MODEL, D_MODEL), F32),
            pltpu.SemaphoreType.DMA((2,)),
            pltpu.VMEM((tm, D_MODEL), F32),
        ],
        compiler_params=_params(1),
        name="mlp",
    )(xp2, xs2, modv, ng, w1, w2)
    return yp.reshape(xp.shape), ys.reshape(xs.shape)


def _log_sigmoid(x):
    return jnp.minimum(x, 0.0) - jnp.log(1.0 + jnp.exp(-jnp.abs(x)))


def _rot_half(x, first_half):
    return jnp.where(first_half, pltpu.roll(x, 96, axis=1), pltpu.roll(x, 32, axis=1))


def _values_and_ones(v):
    vt = jnp.transpose(v)
    row = lax.broadcasted_iota(jnp.int32, vt.shape, 0)
    return jnp.where(row < HEAD_DIM, vt, 1.0)


def _aligned(i, m):
    return i * m if isinstance(i, int) else pl.multiple_of(i * m, m)


def _drain(pieces):
    for _ in pieces:
        pass


def _each(body, n, static, unroll=1):
    if static:
        for i in range(n):
            yield from body(i)
    else:
        lax.fori_loop(0, n, lambda i, c: (_drain(body(i)), c)[1], 0, unroll=unroll)


def _alternate(*programs):
    live = list(programs)
    while live:
        for g in list(live):
            try:
                next(g)
            except StopIteration:
                live.remove(g)
            else:
                yield


def _odd_kernel(*refs, t, lc, latent, nseq):
    shared = set(range(1, 11 if latent else 9))
    n_rows = min(t, ODD_ROWS)
    joint = not latent and nseq > 1 and t == n_rows
    finals = []
    programs = [_odd_seq(*[r if k in shared else r.at[sq] for k, r in enumerate(refs)],
                         t=t, lc=lc, latent=latent, joint=joint, finals=finals) for sq in range(nseq)]
    if joint:
        parts = [next(g) for g in programs]
        h_all = jnp.concatenate([proj_h(0) for proj_h, _ in parts], axis=0)
        for idx, (weight, _) in enumerate(parts[0][1]):
            y = _dot(h_all, weight())
            for sq, (_, plan) in enumerate(parts):
                plan[idx][1](y[sq * n_rows:(sq + 1) * n_rows], 0)
    _run_staggered(programs, ODD_STAGGER)
    if joint:
        o = _dot(jnp.concatenate([z() for z, _ in finals], axis=0), refs[8][...])
        for sq, (_, residual) in enumerate(finals):
            residual(o[sq * n_rows:(sq + 1) * n_rows])


def _odd_seq(*refs, t, lc, latent, joint, finals):
    if latent:
        (x_ref, mod_ref, ng_ref, win_ref, wg_ref, gb_ref, sink_ref, hn_ref, wout_ref,
         cos_ref, sin_ref, kc_ref, vc_ref, cin_ref, nin_ref, min_ref,
         o_ref,
         qa_s, qb_s, kk_s, vvt_s, qma_s, qmb_s, qm_s, km_s, vmt_s, vmtf_s, om_s, g_s, z_s, hft_s, hbt_s,
         sr_s, st_s, qc_s, c_s, n_s, m_s, s_s, p_s, kc_s, vct_s) = refs
    else:
        (x_ref, mod_ref, ng_ref, win_ref, wg_ref, gb_ref, sink_ref, hn_ref, wout_ref,
         o_ref, ko_ref, vo_ref, cf_ref, cb_ref, no_ref, mo_ref,
         qa_s, qb_s, kk_s, vvt_s, qma_s, qmb_s, qm_s, km_s, vmt_s, vmtf_s, om_s, g_s, z_s, hft_s, hbt_s,
         sr_s, st_s, qc_s, c_s, n_s, m_s, s_s, p_s) = refs

    static = not latent
    n_blocks = t // CHUNK
    pad = CHUNK if latent else 0
    shift1, scale1, gate1 = mod_ref[0:1, :], mod_ref[1:2, :], mod_ref[2:3, :]
    lane = lax.broadcasted_iota(jnp.int32, (1, LANES), 1)
    left = lane < HEAD_DIM
    first_half = (lane % HEAD_DIM) < (HEAD_DIM // 2)
    ti = lax.broadcasted_iota(jnp.int32, (CHUNK, CHUNK), 0)
    si = lax.broadcasted_iota(jnp.int32, (CHUNK, CHUNK), 1)
    top = ti < HEAD_DIM
    same_head = top == (si < HEAD_DIM)

    n_rows = min(t, ODD_ROWS)
    if latent:
        for kv in range(N_KV):
            kk_s[kv, 0:CHUNK, :] = jnp.zeros((CHUNK, LANES), BF16)
            kk_s[kv, CHUNK + t:2 * CHUNK + t, :] = jnp.zeros((CHUNK, LANES), BF16)
            vvt_s[kv, :, 0:CHUNK] = jnp.zeros((LANES, CHUNK), BF16)
            vvt_s[kv, :, CHUNK + t:2 * CHUNK + t] = jnp.zeros((LANES, CHUNK), BF16)
            kc_s[kv] = kc_ref[kv].astype(BF16)
            vct_s[kv] = _values_and_ones(vc_ref[kv]).astype(BF16)

    base = D_HALF + 2 * LANES

    def proj_rows(rc):
        r0 = _aligned(rc, n_rows)
        return pl.ds(r0, n_rows), pl.ds(r0 + pad, n_rows)

    def proj_h(rc):
        rows, _ = proj_rows(rc)
        return _norm_mod(x_ref[rows, :], ng_ref[0:1, :], shift1, scale1).astype(BF16)

    def rope(v, rows):
        return v * cos_ref[rows, :] + _rot_half(v, first_half) * sin_ref[rows, :]

    def use_q(q, rc):
        rows, _ = proj_rows(rc)
        for p in range(N_PAIRS):
            cols = slice(p * LANES, (p + 1) * LANES)
            qp = q[:, cols]
            if latent:
                qp = rope(qp, rows)
            qp = qp * ATT_SCALE
            qa_s[rows, cols] = jnp.where(left, qp, 0.0).astype(BF16)
            qb_s[rows, cols] = jnp.where(left, 0.0, qp).astype(BF16)

    def use_kv(kv2, rc):
        rows, krows = proj_rows(rc)
        ka, va = kv2[:, 0:LANES], kv2[:, LANES:2 * LANES]
        if latent:
            ka = rope(ka, rows)
        kr = pltpu.roll(ka, HEAD_DIM, axis=1)
        vr = pltpu.roll(va, HEAD_DIM, axis=1)
        if not latent:
            ka_t, va_t = jnp.transpose(ka), jnp.transpose(va)
            for kv in range(N_KV):
                ko_ref[0, kv, :, rows] = ka_t[kv * HEAD_DIM:(kv + 1) * HEAD_DIM, :]
                vo_ref[0, kv, :, rows] = va_t[kv * HEAD_DIM:(kv + 1) * HEAD_DIM, :]
        kk_s[0, krows, :] = jnp.where(left, ka, kr).astype(BF16)
        kk_s[1, krows, :] = jnp.where(left, kr, ka).astype(BF16)
        vvt_s[0, :, krows] = _values_and_ones(va).astype(BF16)
        vvt_s[1, :, krows] = _values_and_ones(vr).astype(BF16)

    def use_qm(qm, rc):
        rows, _ = proj_rows(rc)
        qm_s[rows, :] = qm.astype(BF16)
        for p in range(N_PAIRS):
            cols = slice(p * LANES, (p + 1) * LANES)
            qma_s[rows, cols] = jnp.where(left, qm[:, cols], 0.0).astype(BF16)
            qmb_s[rows, cols] = jnp.where(left, 0.0, qm[:, cols]).astype(BF16)

    def use_vm(vm, rc):
        rows, _ = proj_rows(rc)
        for p in range(N_PAIRS):
            cols = slice(p * LANES, (p + 1) * LANES)
            vt = jnp.transpose(vm[:, cols])
            vmtf_s[cols, rows] = vt
            vmt_s[cols, rows] = vt.astype(BF16)

    def use_km(km, rc):
        km_s[proj_rows(rc)[0], :] = (km * (HEAD_DIM ** -0.5)).astype(BF16)

    def use_om(om, rc):
        om_s[proj_rows(rc)[0], :] = om

    def use_g(g, rc):
        g_s[proj_rows(rc)[0], :] = g + gb_ref[...]

    def columns(start, width):
        return lambda: win_ref[:, start:start + width]

    proj_plan = [(columns(0, D_HALF), use_q), (columns(D_HALF, 2 * LANES), use_kv),
                 (columns(base, D_HALF), use_qm), (columns(base + 2 * D_HALF, D_HALF), use_vm),
                 (columns(base + D_HALF, D_HALF), use_km), (columns(base + 3 * D_HALF, D_HALF), use_om),
                 (lambda: wg_ref[...], use_g)]

    def project(rc):
        h = proj_h(rc)
        for weight, use in proj_plan:
            use(_dot(h, weight()), rc)
            yield

    if joint:
        yield proj_h, proj_plan
    else:
        yield from _each(project, t // n_rows, static)

    kj = lax.broadcasted_iota(jnp.int32, (3 * CHUNK, 2 * CHUNK), 0)
    qi = lax.broadcasted_iota(jnp.int32, (3 * CHUNK, 2 * CHUNK), 1) % CHUNK
    band_ok = jnp.abs(kj - CHUNK - qi) <= WINDOW
    head_a = lax.broadcasted_iota(jnp.int32, (1, 2 * CHUNK), 1) < CHUNK

    n_keys = _att_keys(lc, latent)
    group = _att_group(latent)

    def attention():
        def attend(i):
            r0 = _aligned(i, CHUNK)
            rows = pl.ds(r0, CHUNK)
            if latent:
                key_pos = kj + (i - 1) * CHUNK
                mask = band_ok & (key_pos >= 0) & (key_pos < t)
                win = pl.ds(r0, 3 * CHUNK)
            for g0 in range(0, N_PAIRS, group):
                for gi in range(group):
                    p = g0 + gi
                    kv = p // (N_PAIRS // N_KV)
                    cols = slice(p * LANES, (p + 1) * LANES)
                    q2 = jnp.concatenate([qa_s[rows, cols], qb_s[rows, cols]], axis=0)
                    if latent:
                        s_s[gi, 0:lc, :] = _dot_nt(kc_s[kv], q2)
                        s_s[gi, lc:n_keys, :] = jnp.where(mask, _dot_nt(kk_s[kv, win, :], q2), -jnp.inf)
                    else:
                        s_s[gi] = _dot_nt(kk_s[kv], q2)
                yield
                maxes = []
                for gi in range(group):
                    p = g0 + gi
                    sink = jnp.where(head_a, sink_ref[2 * p:2 * p + 1, 0:1], sink_ref[2 * p + 1:2 * p + 2, 0:1])
                    mx = jnp.maximum(jnp.max(s_s[gi], axis=0, keepdims=True), sink)
                    p_s[gi] = jnp.exp(s_s[gi] - mx).astype(BF16)
                    maxes.append((sink, mx))
                yield
                for gi in range(group):
                    p = g0 + gi
                    kv = p // (N_PAIRS // N_KV)
                    cols = slice(p * LANES, (p + 1) * LANES)
                    if latent:
                        num = (_dot(vct_s[kv], p_s[gi, 0:lc, :]) + _dot(vvt_s[kv, :, win], p_s[gi, lc:n_keys, :]))
                    else:
                        num = _dot(vvt_s[kv], p_s[gi])
                    sink, mx = maxes[gi]
                    den = num[HEAD_DIM:HEAD_DIM + 1, :] + jnp.exp(sink - mx)
                    out = num[0:HEAD_DIM, :] * (1.0 / den)
                    pair = jnp.concatenate([out[:, 0:CHUNK], out[:, CHUNK:2 * CHUNK]], axis=0)
                    z_s[rows, cols] = jnp.transpose(pair).astype(BF16)
                yield

        return attend

    if latent:
        c_s[...] = cin_ref[...]
        n_s[...] = nin_ref[...]
        m_s[...] = min_ref[...]
    else:
        c_s[...] = jnp.zeros(c_s.shape, F32)
        n_s[...] = jnp.zeros(n_s.shape, F32)
        m_s[...] = jnp.zeros(m_s.shape, F32)

    see = (ti <= si, ti >= si)
    tri = tuple(m.astype(F32).astype(BF16) for m in see)
    last = (CHUNK - 1, 0)

    def mlstm(i):
        offs = (_aligned(i, CHUNK), _aligned(n_blocks - 1 - i, CHUNK))
        half = (slice(0, LANES), slice(LANES, 2 * LANES))
        a_rows, b_rows = [], []
        for d in range(2):
            gt = jnp.transpose(g_s[pl.ds(offs[d], CHUNK), :])
            lf = _log_sigmoid(gt[2 * N_HEADS:4 * N_HEADS, :])
            bcum = _dot_exact_lhs(lf, tri[d])[8 * d:8 * d + 8, :]
            a_rows.append(gt[8 * d:8 * d + 8, :] - bcum)
            b_rows.append(bcum)
        yield
        q_ns = []
        for d in range(2):
            rows = pl.ds(offs[d], CHUNK)
            for p in range(N_PAIRS):
                u = d * N_PAIRS + p
                cols = slice(p * LANES, (p + 1) * LANES)
                q2 = jnp.concatenate([qma_s[rows, cols], qmb_s[rows, cols]], axis=0)
                sr_s[u] = _dot_nt(km_s[rows, cols], q2)
                qc_s[u] = _dot_nt(c_s[u].astype(BF16), qm_s[rows, cols])
                n8 = jnp.broadcast_to(n_s[u:u + 1, :], (8, LANES)).astype(BF16)
                q_ns.append(_dot_nt(n8, q2)[0:1, :])
            yield
        stats = {}
        for d in range(2):
            for hd in range(N_HEADS):
                u, j = d * N_PAIRS + hd // 2, hd % 2
                mrow = d * N_HEADS + hd
                a_row = a_rows[d][hd:hd + 1, :]
                b_row = b_rows[d][hd:hd + 1, :]
                m_prev = m_s[mrow:mrow + 1, 0:1]
                a_col = jnp.transpose(jnp.broadcast_to(a_row, (CHUNK, CHUNK)))
                z_t = jnp.where(see[d], a_col, -jnp.inf)
                m_run = jnp.maximum(jnp.max(z_t, axis=0, keepdims=True), m_prev)
                s_t = sr_s[u, :, half[j]] * jnp.exp(z_t - m_run)
                st_s[u, :, half[j]] = s_t.astype(BF16)
                w_int = jnp.exp(m_prev - m_run)
                den = jnp.sum(s_t, axis=0, keepdims=True) + w_int * q_ns[u][:, half[j]]
                inv = 1.0 / jnp.maximum(jnp.abs(den), jnp.exp(-(b_row + m_run)))
                m_last = m_run[:, last[d]:last[d] + 1]
                stats[(u, j)] = (w_int, inv, jnp.exp(a_row - m_last), jnp.exp(m_prev - m_last))
                m_s[mrow:mrow + 1, :] = jnp.broadcast_to(b_row[:, last[d]:last[d] + 1] + m_last, (1, LANES))
            yield
        for d in range(2):
            rows = pl.ds(offs[d], CHUNK)
            ht_s = (hft_s, hbt_s)[d]
            for p in range(N_PAIRS):
                u = d * N_PAIRS + p
                cols = slice(p * LANES, (p + 1) * LANES)
                (w_a, inv_a, e_a, dec_a), (w_b, inv_b, e_b, dec_b) = stats[(u, 0)], stats[(u, 1)]
                kp = km_s[rows, cols]
                num2 = _dot(vmt_s[cols, rows], st_s[u])
                num = jnp.where(top, num2[:, half[0]], num2[:, half[1]])
                ht_s[cols, rows] = (num + jnp.where(top, w_a, w_b) * qc_s[u]) * jnp.where(top, inv_a, inv_b)
                vt_e = (vmtf_s[cols, rows] * jnp.where(top, e_a, e_b)).astype(BF16)
                c_s[u] = jnp.where(top, dec_a, dec_b) * c_s[u] + jnp.where(same_head, _dot(vt_e, kp), 0.0)
                e2 = jnp.concatenate([e_a, e_b, jnp.zeros((6, CHUNK), F32)], axis=0).astype(BF16)
                n_k = _dot(e2, kp)
                n_s[u:u + 1, :] = (jnp.where(left, dec_a, dec_b) * n_s[u:u + 1, :]
                                   + jnp.where(left, n_k[0:1, :], n_k[1:2, :]))
            yield

    attend = attention()
    yield from _each(lambda i: _alternate(attend(i), mlstm(i)), n_blocks, static, unroll=2)

    if not latent:
        for d, c_ref in enumerate((cf_ref, cb_ref)):
            for p in range(N_PAIRS):
                c_pair = jnp.transpose(c_s[d * N_PAIRS + p])
                c_ref[0, 2 * p] = c_pair[0:HEAD_DIM, 0:HEAD_DIM]
                c_ref[0, 2 * p + 1] = c_pair[HEAD_DIM:LANES, HEAD_DIM:LANES]
        no_ref[...] = n_s[...]
        mo_ref[...] = m_s[...]
        yield

    top_w = lax.broadcasted_iota(jnp.int32, (LANES, n_rows), 0) < HEAD_DIM

    def finish(rc):
        r0 = _aligned(rc, n_rows)
        rows = pl.ds(r0, n_rows)
        for p in range(N_PAIRS):
            cols = slice(p * LANES, (p + 1) * LANES)
            hm = hft_s[cols, rows] + hbt_s[cols, rows]
            sq = hm * hm
            ms_a = jnp.sum(sq[0:HEAD_DIM], axis=0, keepdims=True)
            ms_b = jnp.sum(sq[HEAD_DIM:LANES], axis=0, keepdims=True)
            ms = jnp.where(top_w, ms_a, ms_b) * (1.0 / HEAD_DIM)
            y = jnp.transpose(hm * lax.rsqrt(ms + EPS)) * hn_ref[:, cols] * jax.nn.sigmoid(om_s[rows, cols])
            z_s[rows, D_HALF + p * LANES:D_HALF + (p + 1) * LANES] = y.astype(BF16)
            if p % 2:
                yield
        def residual(o):
            o_ref[rows, :] = x_ref[rows, :] + gate1 * _rms(o, ng_ref[1:2, :])

        if joint:
            finals.append((lambda: z_s[rows, :], residual))
        else:
            residual(_dot(z_s[rows, :], wout_ref[...]))
        yield

    yield from _each(finish, t // n_rows, static)


def _att_keys(lc, latent):
    return lc + 3 * CHUNK if latent else lc


def _att_group(latent):
    return 2 if latent else N_PAIRS


def _odd_scratch(t, lc, latent, nseq):
    pad = 2 * CHUNK if latent else 0
    att = (_att_group(latent), _att_keys(lc, latent), 2 * CHUNK)
    shapes = [
        pltpu.VMEM((t, D_HALF), BF16),
        pltpu.VMEM((t, D_HALF), BF16),
        pltpu.VMEM((N_KV, t + pad, LANES), BF16),
        pltpu.VMEM((N_KV, LANES, t + pad), BF16),
        pltpu.VMEM((t, D_HALF), BF16),
        pltpu.VMEM((t, D_HALF), BF16),
        pltpu.VMEM((t, D_HALF), BF16),
        pltpu.VMEM((t, D_HALF), BF16),
        pltpu.VMEM((D_HALF, t), BF16),
        pltpu.VMEM((D_HALF, t), F32),
        pltpu.VMEM((t, D_HALF), F32),
        pltpu.VMEM((t, LANES), F32),
        pltpu.VMEM((t, D_MODEL), BF16),
        pltpu.VMEM((D_HALF, t), F32),
        pltpu.VMEM((D_HALF, t), F32),
        pltpu.VMEM((2 * N_PAIRS, CHUNK, 2 * LANES), F32),
        pltpu.VMEM((2 * N_PAIRS, CHUNK, 2 * LANES), BF16),
        pltpu.VMEM((2 * N_PAIRS, LANES, CHUNK), F32),
        pltpu.VMEM((2 * N_PAIRS, LANES, LANES), F32),
        pltpu.VMEM((2 * N_PAIRS, LANES), F32),
        pltpu.VMEM((2 * N_HEADS, LANES), F32),
        pltpu.VMEM(att, F32),
        pltpu.VMEM(att, BF16),
    ]
    if latent:
        shapes += [pltpu.VMEM((N_KV, lc, LANES), BF16), pltpu.VMEM((N_KV, LANES, lc), BF16)]
    return [pltpu.VMEM((nseq,) + tuple(sh.shape), sh.dtype) for sh in shapes]


def _odd_common_specs(t, layer, cond_base, cond_stride, nseq):
    assert cond_stride == 0 or nseq == 1
    j = layer // 2
    return [
        _per_seq((t, D_MODEL), nseq),
        _mod_spec(layer, lambda i: cond_base + cond_stride * i),
        _layer_spec((4, D_MODEL), layer),
        _layer_spec((D_MODEL, D_IN_ODD), j),
        _const_spec((D_MODEL, LANES)),
        _const_spec((1, LANES)),
        _const_spec((N_HEADS, LANES)),
        _layer_spec((1, D_HALF), j),
        _layer_spec((D_MODEL, D_MODEL), j),
    ]


def _per_seq(shape, nseq):
    return pl.BlockSpec((nseq,) + shape, lambda i: (i,) + (0,) * len(shape))


def _odd_context(x, modv, layer, ng, w_main, w_gate, gate_bias, sink_b, hnorm, wout):
    b, t, _ = x.shape
    nseq = ODD_CTX_SEQS
    assert b % nseq == 0
    kern = functools.partial(_odd_kernel, t=t, lc=t, latent=False, nseq=nseq)
    per_seq = functools.partial(_per_seq, nseq=nseq)
    return pl.pallas_call(
        kern,
        grid=(b // nseq,),
        in_specs=_odd_common_specs(t, layer, 0, 0, nseq),
        out_specs=[per_seq((t, D_MODEL)),
                   per_seq((1, N_KV, HEAD_DIM, t)), per_seq((1, N_KV, HEAD_DIM, t)),
                   per_seq((1, N_HEADS, HEAD_DIM, HEAD_DIM)), per_seq((1, N_HEADS, HEAD_DIM, HEAD_DIM)),
                   per_seq((2 * N_PAIRS, LANES)), per_seq((2 * N_HEADS, LANES))],
        out_shape=[jax.ShapeDtypeStruct((b, t, D_MODEL), F32),
                   jax.ShapeDtypeStruct((b, 1, N_KV, HEAD_DIM, t), F32),
                   jax.ShapeDtypeStruct((b, 1, N_KV, HEAD_DIM, t), F32),
                   jax.ShapeDtypeStruct((b, 1, N_HEADS, HEAD_DIM, HEAD_DIM), F32),
                   jax.ShapeDtypeStruct((b, 1, N_HEADS, HEAD_DIM, HEAD_DIM), F32),
                   jax.ShapeDtypeStruct((b, 2 * N_PAIRS, LANES), F32),
                   jax.ShapeDtypeStruct((b, 2 * N_HEADS, LANES), F32)],
        scratch_shapes=_odd_scratch(t, t, False, nseq),
        compiler_params=_params(1),
        name="odd_mixer_context",
    )(x, modv, ng, w_main, w_gate, gate_bias, sink_b, hnorm, wout)


def _odd_latent(x, modv, layer, ng, w_main, w_gate, gate_bias, sink_b, hnorm, wout, cos_t, sin_t, kc, vc,
                c_in, n_in, m_in):
    b, t, _ = x.shape
    lc = kc.shape[2]
    kern = functools.partial(_odd_kernel, t=t, lc=lc, latent=True, nseq=1)
    per_seq = functools.partial(_per_seq, nseq=1)
    return pl.pallas_call(
        kern,
        grid=(b,),
        in_specs=_odd_common_specs(t, layer, 1, 1, 1) + [
            _const_spec((t, LANES)), _const_spec((t, LANES)),
            per_seq((N_KV, lc, LANES)), per_seq((N_KV, lc, LANES)),
            per_seq((2 * N_PAIRS, LANES, LANES)), per_seq((2 * N_PAIRS, LANES)), per_seq((2 * N_HEADS, LANES)),
        ],
        out_specs=per_seq((t, D_MODEL)),
        out_shape=jax.ShapeDtypeStruct((b, t, D_MODEL), F32),
        scratch_shapes=_odd_scratch(t, lc, True, 1),
        compiler_params=_params(1),
        name="odd_mixer_latent",
    )(x, modv, ng, w_main, w_gate, gate_bias, sink_b, hnorm, wout, cos_t, sin_t, kc, vc, c_in, n_in, m_in)


def _rope_tables(t):
    rows = t // GRID_W
    row = jnp.broadcast_to(jnp.arange(rows)[:, None], (rows, GRID_W)).reshape(t).astype(F32)
    col = jnp.broadcast_to(jnp.arange(GRID_W)[None, :], (rows, GRID_W)).reshape(t).astype(F32)
    n_freq = HEAD_DIM // 4
    inv_freq = ROPE_BASE ** (-jnp.arange(n_freq, dtype=F32) / n_freq)
    ang = jnp.concatenate([row[:, None] * inv_freq, col[:, None] * inv_freq], axis=-1)
    cos, sin = jnp.cos(ang), jnp.sin(ang)
    cos_l = jnp.tile(cos, (1, LANES // cos.shape[1]))
    sin_l = jnp.tile(jnp.concatenate([-sin, sin], axis=-1), (1, LANES // HEAD_DIM))
    return cos_l, sin_l


def _pair_blockdiag(c):
    b = c.shape[0]
    c = c.reshape(b, N_PAIRS, 2, HEAD_DIM, HEAD_DIM)
    z = jnp.zeros_like(c[:, :, 0])
    top = jnp.concatenate([c[:, :, 0], z], axis=-1)
    bot = jnp.concatenate([z, c[:, :, 1]], axis=-1)
    return jnp.concatenate([top, bot], axis=-2)


def _lane_bcast(v):
    return jnp.broadcast_to(v[..., None], v.shape + (LANES,))


def kernel(x_prompt, x_sample, c, cache_k, cache_v, state_c_fwd, state_n_fwd, state_m_fwd, state_c_bwd, state_n_bwd, state_m_bwd, c_ctx, mod_w, mod_b, norm_g, mlp_w1, mlp_w2, even_in_w, conv_a_w, conv_a_b, ln_a_g, ln_a_b, conv_b_w, even_out_w, odd_in_w, attn_sink, gate_b, hnorm_g, odd_out_w):
    n_dec = x_sample.shape[0]
    n_ctx = x_prompt.shape[0]
    cond = jnp.concatenate([c_ctx[None, :], c, jnp.zeros((COND_ROWS - 1 - n_dec, D_MODEL), F32)], axis=0)
    modv = _modulation(cond, mod_w, mod_b)

    yp, ys = x_prompt, x_sample
    w1, w2 = mlp_w1, mlp_w2

    ev = (norm_g, even_in_w.astype(BF16), conv_a_w, conv_a_b[:, None, :], ln_a_g[:, None, :], ln_a_b[:, None, :],
          conv_b_w, even_out_w.astype(BF16))
    yp = _even_layer(yp, modv, 0, 0, 0, *ev)
    ys = _even_layer(ys, modv, 0, 1, 1, *ev)
    yp, ys = _mlp_layer(yp, ys, modv, 0, norm_g, w1, w2)

    order = jnp.array([0, 2, 1, 3])
    d_main = D_IN_ODD - 4 * N_HEADS
    wg = odd_in_w[0][:, d_main:].reshape(D_MODEL, 4, N_HEADS)[:, order, :].reshape(D_MODEL, 4 * N_HEADS)
    w_gate = jnp.pad(wg, ((0, 0), (0, LANES - 4 * N_HEADS))).astype(BF16)
    gate_bias = jnp.pad(gate_b[0][order, :].reshape(1, 4 * N_HEADS), ((0, 0), (0, LANES - 4 * N_HEADS)))
    sink_b = _lane_bcast(attn_sink[0])
    odd = (1, norm_g, odd_in_w.astype(BF16), w_gate, gate_bias, sink_b, hnorm_g[:, None, :], odd_out_w.astype(BF16))

    op, k_t, v_t, c_f, c_b, n_new, m_new = _odd_context(yp, modv, *odd)
    new_k, new_v = jnp.swapaxes(k_t, -1, -2), jnp.swapaxes(v_t, -1, -2)

    t_dec = x_sample.shape[1]
    cos_t, sin_t = _rope_tables(t_dec)
    kc = jnp.concatenate([cache_k[:, 0], cache_k[:, 0]], axis=-1)
    vc = jnp.concatenate([cache_v[:, 0], cache_v[:, 0]], axis=-1)
    c_in = jnp.concatenate([_pair_blockdiag(jnp.swapaxes(state_c_fwd[:, 0], -1, -2)),
                            _pair_blockdiag(jnp.swapaxes(state_c_bwd[:, 0], -1, -2))], axis=1)
    n_in = jnp.concatenate([state_n_fwd[:, 0].reshape(n_dec, N_PAIRS, LANES),
                            state_n_bwd[:, 0].reshape(n_dec, N_PAIRS, LANES)], axis=1)
    m_in = _lane_bcast(jnp.concatenate([state_m_fwd[:, 0], state_m_bwd[:, 0]], axis=1))
    os_ = _odd_latent(ys, modv, *odd, cos_t, sin_t, kc, vc, c_in, n_in, m_in)

    yp, ys = _mlp_layer(op, os_, modv, 1, norm_g, w1, w2)

    n_f = n_new[:, :N_PAIRS].reshape(n_ctx, N_HEADS, HEAD_DIM)[:, None]
    n_b = n_new[:, N_PAIRS:].reshape(n_ctx, N_HEADS, HEAD_DIM)[:, None]
    m_f = m_new[:, :N_HEADS, 0][:, None]
    m_b = m_new[:, N_HEADS:, 0][:, None]
    return (yp, ys, new_k, new_v, c_f, n_f, m_f, c_b, n_b, m_b)
```

```python
import functools

import jax
import jax.numpy as jnp
from jax import lax
from jax.experimental import pallas as pl
from jax.experimental.pallas import tpu as pltpu

F32 = jnp.float32
BF16 = jnp.bfloat16

D_MODEL = 1024
D_FF = 4 * D_MODEL
EPS = 1e-6
D_HALF = D_MODEL // 2
CONF_WIDTH = 31
CONF_HALO = 16
HEAD_DIM = 64
N_HEADS = 8
N_PAIRS = N_HEADS // 2
N_KV = 2
LANES = 128
CHUNK = 128
WINDOW = 128
GRID_W = 64
ROPE_BASE = 10000.0
ATT_SCALE = HEAD_DIM ** -0.5
D_IN_ODD = D_HALF + 2 * N_KV * HEAD_DIM + 4 * D_HALF + 4 * N_HEADS
ROW_CHUNK = 256
MLP_ROWS = 512
MLP_STAGGER = 3
ODD_ROWS = 512
ODD_CTX_SEQS = 2
ODD_STAGGER = 6
EVEN_CONV_ROWS = 64
EVEN_CHUNKS = 2
EVEN_STAGGER = 3
COND_ROWS = 8
VMEM_LIMIT = 56 * 1024 * 1024


def _dot(a, b):
    return jnp.dot(a, b, preferred_element_type=F32)


def _dot_nt(a, b):
    return lax.dot_general(a, b, (((1,), (1,)), ((), ())), preferred_element_type=F32)


def _split3(x):
    hi = x.astype(BF16)
    r1 = x - hi.astype(F32)
    mid = r1.astype(BF16)
    lo = (r1 - mid.astype(F32)).astype(BF16)
    return hi, mid, lo


def _dot_exact_lhs(x, b01):
    hi, mid, lo = _split3(x)
    return _dot(hi, b01) + _dot(mid, b01) + _dot(lo, b01)


def _rms(x, g):
    return x * lax.rsqrt(jnp.mean(x * x, axis=-1, keepdims=True) + EPS) * g


def _norm_mod(x, g, shift, scale):
    return _rms(x, g) * (1.0 + scale) + shift


def _params(n_grid):
    return pltpu.CompilerParams(dimension_semantics=("arbitrary",) * n_grid, vmem_limit_bytes=VMEM_LIMIT)


def _const_spec(shape):
    zeros = (0,) * len(shape)
    return pl.BlockSpec(shape, lambda *_: zeros, pipeline_mode=pl.Buffered(1))


def _layer_spec(shape, layer):
    index = (layer,) + (0,) * len(shape)
    return pl.BlockSpec((None,) + shape, lambda *_: index, pipeline_mode=pl.Buffered(1))


def _mod_spec(layer, cond_of):
    return pl.BlockSpec((None, None, 6, D_MODEL), lambda *idx: (layer, cond_of(*idx), 0, 0))


def _mod_kernel(cond_ref, w_ref, b_ref, o_ref):
    s = jax.nn.silu(cond_ref[...]).astype(BF16)
    o_ref[...] = _dot(s, w_ref[...].astype(BF16)) + b_ref[...]


def _modulation(cond, mod_w, mod_b):
    depth = mod_w.shape[0]
    n_out = mod_w.shape[2]
    tn = 2 * D_MODEL
    out = pl.pallas_call(
        _mod_kernel,
        grid=(depth, n_out // tn),
        in_specs=[
            pl.BlockSpec((COND_ROWS, D_MODEL), lambda l, j: (0, 0)),
            pl.BlockSpec((None, D_MODEL, tn), lambda l, j: (l, 0, j)),
            pl.BlockSpec((None, 1, tn), lambda l, j: (l, 0, j)),
        ],
        out_specs=pl.BlockSpec((None, COND_ROWS, tn), lambda l, j: (l, 0, j)),
        out_shape=jax.ShapeDtypeStruct((depth, COND_ROWS, n_out), F32),
        compiler_params=_params(2),
        name="modulation",
    )(cond, mod_w, mod_b.reshape(depth, 1, n_out))
    return out.reshape(depth, COND_ROWS, 6, D_MODEL)


def _run_staggered(programs, stagger):
    programs = list(programs)
    live, rounds = [], 0
    while programs or live:
        if programs and rounds % stagger == 0:
            live.append(programs.pop(0))
        for g in list(live):
            try:
                next(g)
            except StopIteration:
                live.remove(g)
        rounds += 1


def _even_kernel(*refs, n_chunks, nsub):
    halos, (xc_ref, mod_ref, ng_ref, win_ref, caw_ref, cab_ref, lng_ref, lnb_ref, cbw_ref, wout_ref, o_ref,
            apad, cpad, bgs, zs) = refs[:2 * nsub], refs[2 * nsub:]
    shared = (mod_ref, ng_ref, win_ref, caw_ref, cab_ref, lng_ref, lnb_ref, cbw_ref, wout_ref)
    _run_staggered(
        (_even_chunk(pl.program_id(0) * nsub + k, halos[2 * k], xc_ref.at[k], halos[2 * k + 1], *shared,
                     o_ref.at[k], apad.at[k], cpad.at[k], bgs.at[k], zs.at[k], n_chunks=n_chunks)
         for k in range(nsub)), EVEN_STAGGER)


def _even_chunk(g, xp_ref, xc_ref, xn_ref, mod_ref, ng_ref, win_ref, caw_ref, cab_ref, lng_ref, lnb_ref,
                cbw_ref, wout_ref, o_ref, apad, cpad, bgs, zs, *, n_chunks):
    c = g % n_chunks if n_chunks > 1 else 0
    rows = ROW_CHUNK + 2 * CONF_HALO
    shift1, scale1, gate1 = mod_ref[0:1, :], mod_ref[1:2, :], mod_ref[2:3, :]
    own = slice(CONF_HALO, CONF_HALO + ROW_CHUNK)
    if n_chunks == 1:
        h = _norm_mod(xc_ref[...], ng_ref[0:1, :], shift1, scale1).astype(BF16)
        h_own = h
        for pad_ref in (apad, cpad):
            pad_ref[0:CONF_HALO, :] = jnp.zeros((CONF_HALO, D_HALF), F32)
            pad_ref[CONF_HALO + ROW_CHUNK:rows, :] = jnp.zeros((CONF_HALO, D_HALF), F32)
        keep = lambda v: v
        span = own
    else:
        xh = jnp.concatenate([xp_ref[...], xc_ref[...], xn_ref[...]], axis=0)
        h = _norm_mod(xh, ng_ref[0:1, :], shift1, scale1).astype(BF16)
        h_own = h[own]
        ri = lax.broadcasted_iota(jnp.int32, (rows, D_HALF), 0)
        lo = jnp.where(c == 0, CONF_HALO, 0)
        hi = jnp.where(c == n_chunks - 1, CONF_HALO + ROW_CHUNK, rows)
        inside = (ri >= lo) & (ri < hi)
        keep = lambda v: jnp.where(inside, v, 0.0)
        span = slice(0, rows)
    a = _dot(h, win_ref[:, 0:D_HALF]) * jax.nn.sigmoid(_dot(h, win_ref[:, D_HALF:2 * D_HALF]))
    apad[span, :] = keep(a)
    yield
    cx = _dot(h, win_ref[:, 3 * D_HALF:4 * D_HALF]) * _dot(h, win_ref[:, 4 * D_HALF:5 * D_HALF])
    cpad[span, :] = keep(cx)
    yield
    bgs[...] = _dot(h_own, win_ref[:, 2 * D_HALF:3 * D_HALF])
    yield

    sub = EVEN_CONV_ROWS
    tile = 8
    for j in range(ROW_CHUNK // sub):
        r0 = j * sub
        groups = []
        for cg in range(D_HALF // LANES):
            cols = slice(cg * LANES, (cg + 1) * LANES)
            acc = None
            for r in range(tile):
                part = None
                for m in range(-(-(CONF_WIDTH + 1) // tile)):
                    o = tile * m + r
                    if 1 <= o <= CONF_WIDTH:
                        term = caw_ref[o - 1:o, cols] * apad[r0 + tile * m:r0 + tile * m + sub + tile, cols]
                        part = term if part is None else part + term
                shifted = part[r:r + sub, :]
                acc = shifted if acc is None else acc + shifted
            groups.append(acc)
        acc = jnp.concatenate(groups, axis=1) + cab_ref[...]
        mu = jnp.mean(acc, axis=-1, keepdims=True)
        dlt = acc - mu
        var = jnp.mean(dlt * dlt, axis=-1, keepdims=True)
        a_out = jax.nn.silu(dlt * lax.rsqrt(var + EPS) * lng_ref[...] + lnb_ref[...])
        zs[r0:r0 + sub, 0:D_HALF] = a_out.astype(BF16)
        base = r0 + CONF_HALO - 1
        sc = (cbw_ref[0:1, :] * cpad[base:base + sub, :]
              + cbw_ref[1:2, :] * cpad[base + 1:base + 1 + sub, :]
              + cbw_ref[2:3, :] * cpad[base + 2:base + 2 + sub, :])
        zs[r0:r0 + sub, D_HALF:D_MODEL] = (bgs[r0:r0 + sub, :] * sc).astype(BF16)
        yield

    o = _dot(zs[...], wout_ref[...])
    o_ref[...] = xc_ref[...] + gate1 * _rms(o, ng_ref[1:2, :])
    yield


def _even_layer(x, modv, layer, cond_base, cond_stride, ng, win, caw, cab, lng, lnb, cbw, wout):
    j = layer // 2
    b, t, _ = x.shape
    n_chunks = t // ROW_CHUNK
    nsub = EVEN_CHUNKS
    assert (b * n_chunks) % nsub == 0 and (cond_stride == 0 or n_chunks % nsub == 0)
    hpc = ROW_CHUNK // CONF_HALO
    n_halo_blocks = b * t // CONF_HALO
    rows = ROW_CHUNK + 2 * CONF_HALO
    halo_specs = []
    for k in range(nsub):
        halo_specs += [
            pl.BlockSpec((None, CONF_HALO, D_MODEL),
                         lambda i, k=k: (jnp.maximum((i * nsub + k) * hpc - 1, 0), 0, 0)),
            pl.BlockSpec((None, CONF_HALO, D_MODEL),
                         lambda i, k=k: (jnp.minimum((i * nsub + k + 1) * hpc, n_halo_blocks - 1), 0, 0)),
        ]
    x_halo = x.reshape(n_halo_blocks, CONF_HALO, D_MODEL)
    x_chunks = x.reshape(b * n_chunks, ROW_CHUNK, D_MODEL)
    kern = functools.partial(_even_kernel, n_chunks=n_chunks, nsub=nsub)
    out = pl.pallas_call(
        kern,
        grid=(b * n_chunks // nsub,),
        in_specs=halo_specs + [
            pl.BlockSpec((nsub, ROW_CHUNK, D_MODEL), lambda i: (i, 0, 0)),
            _mod_spec(layer, lambda i: cond_base + cond_stride * ((i * nsub) // n_chunks)),
            _layer_spec((4, D_MODEL), layer),
            _layer_spec((D_MODEL, 5 * D_HALF), j),
            _layer_spec((CONF_WIDTH, D_HALF), j),
            _layer_spec((1, D_HALF), j),
            _layer_spec((1, D_HALF), j),
            _layer_spec((1, D_HALF), j),
            _layer_spec((3, D_HALF), j),
            _layer_spec((D_MODEL, D_MODEL), j),
        ],
        out_specs=pl.BlockSpec((nsub, ROW_CHUNK, D_MODEL), lambda i: (i, 0, 0)),
        out_shape=jax.ShapeDtypeStruct(x_chunks.shape, F32),
        scratch_shapes=[
            pltpu.VMEM((nsub, rows, D_HALF), F32),
            pltpu.VMEM((nsub, rows, D_HALF), F32),
            pltpu.VMEM((nsub, ROW_CHUNK, D_HALF), F32),
            pltpu.VMEM((nsub, ROW_CHUNK, D_MODEL), BF16),
        ],
        compiler_params=_params(1),
        name="even_mixer",
    )(*([x_halo] * (2 * nsub)), x_chunks, modv, ng, win, caw, cab, lng, lnb, cbw, wout)
    return out.reshape(b, t, D_MODEL)


def _mlp_kernel(xp_ref, xs_ref, mod_ref, ng_ref, w1_hbm, w2_hbm, op_ref, os_ref, w1_s, w2_s, stage, sem, *,
                layer, n_ctx_tiles):
    n_blocks = D_FF // D_MODEL

    def block_copy(k):
        c, slot = k // 2, k % 2
        span = pl.ds(c * D_MODEL, D_MODEL)
        src = w1_hbm.at[layer, :, span] if k % 2 == 0 else w2_hbm.at[layer, span, :]
        return pltpu.make_async_copy(src, stage.at[slot], sem.at[slot])

    def fetch(k, dst):
        block_copy(k).wait()
        dst[...] = stage[k % 2].astype(BF16)
        if k + 2 < 2 * n_blocks:
            block_copy(k + 2).start()

    def rows_program(x_ref, o_ref, rows, load):
        x = x_ref[rows, :]
        h = _norm_mod(x, ng_ref[2:3, :], mod_ref[3:4, :], mod_ref[4:5, :]).astype(BF16)
        yield
        acc = None
        for c in range(n_blocks):
            cols = slice(c * D_MODEL, (c + 1) * D_MODEL)
            if load:
                fetch(2 * c, w1_s.at[:, cols])
            hid = jnp.square(jnp.maximum(_dot(h, w1_s[:, cols]), 0.0)).astype(BF16)
            yield
            if load:
                fetch(2 * c + 1, w2_s.at[cols, :])
            part = _dot(hid, w2_s[cols, :])
            acc = part if acc is None else acc + part
            yield
        o_ref[rows, :] = x + mod_ref[5:6, :] * _rms(acc, ng_ref[3:4, :])
        yield

    def tile(x_ref, o_ref, load=False):
        n_rows = x_ref.shape[0]
        if load:
            _drain(rows_program(x_ref, o_ref, slice(0, n_rows), True))
        else:
            _run_staggered((rows_program(x_ref, o_ref, slice(r0, r0 + n_rows // 2), False)
                            for r0 in (0, n_rows // 2)), MLP_STAGGER)

    i = pl.program_id(0)

    @pl.when(i == 0)
    def _():
        block_copy(0).start()
        block_copy(1).start()
        tile(xp_ref, op_ref, load=True)

    @pl.when((i > 0) & (i < n_ctx_tiles))
    def _():
        tile(xp_ref, op_ref)

    @pl.when(i >= n_ctx_tiles)
    def _():
        tile(xs_ref, os_ref)


def _mlp_layer(xp, xs, modv, layer, ng, w1, w2):
    tm = MLP_ROWS
    xp2, xs2 = xp.reshape(-1, D_MODEL), xs.reshape(-1, D_MODEL)
    t_dec = xs.shape[1]
    assert xp2.shape[0] % tm == 0 and t_dec % tm == 0
    n_p, n_s, per_seq = xp2.shape[0] // tm, xs2.shape[0] // tm, t_dec // tm
    ctx_tile = lambda i: (jnp.minimum(i, n_p - 1), 0)
    lat_tile = lambda i: (jnp.maximum(i - n_p, 0), 0)
    yp, ys = pl.pallas_call(
        functools.partial(_mlp_kernel, layer=layer, n_ctx_tiles=n_p),
        grid=(n_p + n_s,),
        in_specs=[
            pl.BlockSpec((tm, D_MODEL), ctx_tile),
            pl.BlockSpec((tm, D_MODEL), lat_tile),
            _mod_spec(layer, lambda i: jnp.where(i < n_p, 0, 1 + jnp.maximum(i - n_p, 0) // per_seq)),
            _layer_spec((4, D_MODEL), layer),
            pl.BlockSpec(memory_space=pl.ANY),
            pl.BlockSpec(memory_space=pl.ANY),
        ],
        out_specs=[pl.BlockSpec((tm, D_MODEL), ctx_tile), pl.BlockSpec((tm, D_MODEL), lat_tile)],
        out_shape=[jax.ShapeDtypeStruct(xp2.shape, F32), jax.ShapeDtypeStruct(xs2.shape, F32)],
        scratch_shapes=[
            pltpu.VMEM((D_MODEL, D_FF), BF16),
            pltpu.VMEM((D_FF, D_MODEL), BF16),
            pltpu.VMEM((2, D_MODEL, D_MODEL), F32),
            pltpu.SemaphoreType.DMA((2,)),
        ],
        compiler_params=_params(1),
        name="mlp",
    )(xp2, xs2, modv, ng, w1, w2)
    return yp.reshape(xp.shape), ys.reshape(xs.shape)


def _log_sigmoid(x):
    return jnp.minimum(x, 0.0) - jnp.log(1.0 + jnp.exp(-jnp.abs(x)))


def _rot_half(x, first_half):
    return jnp.where(first_half, pltpu.roll(x, 96, axis=1), pltpu.roll(x, 32, axis=1))


def _values_and_ones(v):
    vt = jnp.transpose(v)
    row = lax.broadcasted_iota(jnp.int32, vt.shape, 0)
    return jnp.where(row < HEAD_DIM, vt, 1.0)


def _aligned(i, m):
    return i * m if isinstance(i, int) else pl.multiple_of(i * m, m)


def _drain(pieces):
    for _ in pieces:
        pass


def _each(body, n, static, unroll=1):
    if static:
        for i in range(n):
            yield from body(i)
    else:
        lax.fori_loop(0, n, lambda i, c: (_drain(body(i)), c)[1], 0, unroll=unroll)


def _alternate(*programs):
    live = list(programs)
    while live:
        for g in list(live):
            try:
                next(g)
            except StopIteration:
                live.remove(g)
            else:
                yield


def _odd_kernel(*refs, t, lc, latent, nseq):
    shared = set(range(1, 11 if latent else 9))
    n_rows = min(t, ODD_ROWS)
    joint = not latent and nseq > 1 and t == n_rows
    finals = []
    programs = [_odd_seq(*[r if k in shared else r.at[sq] for k, r in enumerate(refs)],
                         t=t, lc=lc, latent=latent, joint=joint, finals=finals) for sq in range(nseq)]
    if joint:
        parts = [next(g) for g in programs]
        h_all = jnp.concatenate([proj_h(0) for proj_h, _ in parts], axis=0)
        for idx, (weight, _) in enumerate(parts[0][1]):
            y = _dot(h_all, weight())
            for sq, (_, plan) in enumerate(parts):
                plan[idx][1](y[sq * n_rows:(sq + 1) * n_rows], 0)
    _run_staggered(programs, ODD_STAGGER)
    if joint:
        o = _dot(jnp.concatenate([z() for z, _ in finals], axis=0), refs[8][...])
        for sq, (_, residual) in enumerate(finals):
            residual(o[sq * n_rows:(sq + 1) * n_rows])


def _odd_seq(*refs, t, lc, latent, joint, finals):
    if latent:
        (x_ref, mod_ref, ng_ref, win_ref, wg_ref, gb_ref, sink_ref, hn_ref, wout_ref,
         cos_ref, sin_ref, kc_ref, vc_ref, cin_ref, nin_ref, min_ref,
         o_ref,
         qa_s, qb_s, kk_s, vvt_s, qma_s, qmb_s, qm_s, km_s, vmt_s, vmtf_s, om_s, g_s, z_s, hft_s, hbt_s,
         sr_s, st_s, qc_s, c_s, n_s, m_s, s_s, p_s, kc_s, vct_s) = refs
    else:
        (x_ref, mod_ref, ng_ref, win_ref, wg_ref, gb_ref, sink_ref, hn_ref, wout_ref,
         o_ref, ko_ref, vo_ref, cf_ref, cb_ref, no_ref, mo_ref,
         qa_s, qb_s, kk_s, vvt_s, qma_s, qmb_s, qm_s, km_s, vmt_s, vmtf_s, om_s, g_s, z_s, hft_s, hbt_s,
         sr_s, st_s, qc_s, c_s, n_s, m_s, s_s, p_s) = refs

    static = not latent
    n_blocks = t // CHUNK
    pad = CHUNK if latent else 0
    shift1, scale1, gate1 = mod_ref[0:1, :], mod_ref[1:2, :], mod_ref[2:3, :]
    lane = lax.broadcasted_iota(jnp.int32, (1, LANES), 1)
    left = lane < HEAD_DIM
    first_half = (lane % HEAD_DIM) < (HEAD_DIM // 2)
    ti = lax.broadcasted_iota(jnp.int32, (CHUNK, CHUNK), 0)
    si = lax.broadcasted_iota(jnp.int32, (CHUNK, CHUNK), 1)
    top = ti < HEAD_DIM
    same_head = top == (si < HEAD_DIM)

    n_rows = min(t, ODD_ROWS)
    if latent:
        for kv in range(N_KV):
            kk_s[kv, 0:CHUNK, :] = jnp.zeros((CHUNK, LANES), BF16)
            kk_s[kv, CHUNK + t:2 * CHUNK + t, :] = jnp.zeros((CHUNK, LANES), BF16)
            vvt_s[kv, :, 0:CHUNK] = jnp.zeros((LANES, CHUNK), BF16)
            vvt_s[kv, :, CHUNK + t:2 * CHUNK + t] = jnp.zeros((LANES, CHUNK), BF16)
            kc_s[kv] = kc_ref[kv].astype(BF16)
            vct_s[kv] = _values_and_ones(vc_ref[kv]).astype(BF16)

    base = D_HALF + 2 * LANES

    def proj_rows(rc):
        r0 = _aligned(rc, n_rows)
        return pl.ds(r0, n_rows), pl.ds(r0 + pad, n_rows)

    def proj_h(rc):
        rows, _ = proj_rows(rc)
        return _norm_mod(x_ref[rows, :], ng_ref[0:1, :], shift1, scale1).astype(BF16)

    def rope(v, rows):
        return v * cos_ref[rows, :] + _rot_half(v, first_half) * sin_ref[rows, :]

    def use_q(q, rc):
        rows, _ = proj_rows(rc)
        for p in range(N_PAIRS):
            cols = slice(p * LANES, (p + 1) * LANES)
            qp = q[:, cols]
            if latent:
                qp = rope(qp, rows)
            qp = qp * ATT_SCALE
            qa_s[rows, cols] = jnp.where(left, qp, 0.0).astype(BF16)
            qb_s[rows, cols] = jnp.where(left, 0.0, qp).astype(BF16)

    def use_kv(kv2, rc):
        rows, krows = proj_rows(rc)
        ka, va = kv2[:, 0:LANES], kv2[:, LANES:2 * LANES]
        if latent:
            ka = rope(ka, rows)
        kr = pltpu.roll(ka, HEAD_DIM, axis=1)
        vr = pltpu.roll(va, HEAD_DIM, axis=1)
        if not latent:
            ka_t, va_t = jnp.transpose(ka), jnp.transpose(va)
            for kv in range(N_KV):
                ko_ref[0, kv, :, rows] = ka_t[kv * HEAD_DIM:(kv + 1) * HEAD_DIM, :]
                vo_ref[0, kv, :, rows] = va_t[kv * HEAD_DIM:(kv + 1) * HEAD_DIM, :]
        kk_s[0, krows, :] = jnp.where(left, ka, kr).astype(BF16)
        kk_s[1, krows, :] = jnp.where(left, kr, ka).astype(BF16)
        vvt_s[0, :, krows] = _values_and_ones(va).astype(BF16)
        vvt_s[1, :, krows] = _values_and_ones(vr).astype(BF16)

    def use_qm(qm, rc):
        rows, _ = proj_rows(rc)
        qm_s[rows, :] = qm.astype(BF16)
        for p in range(N_PAIRS):
            cols = slice(p * LANES, (p + 1) * LANES)
            qma_s[rows, cols] = jnp.where(left, qm[:, cols], 0.0).astype(BF16)
            qmb_s[rows, cols] = jnp.where(left, 0.0, qm[:, cols]).astype(BF16)

    def use_vm(vm, rc):
        rows, _ = proj_rows(rc)
        for p in range(N_PAIRS):
            cols = slice(p * LANES, (p + 1) * LANES)
            vt = jnp.transpose(vm[:, cols])
            vmtf_s[cols, rows] = vt
            vmt_s[cols, rows] = vt.astype(BF16)

    def use_km(km, rc):
        km_s[proj_rows(rc)[0], :] = (km * (HEAD_DIM ** -0.5)).astype(BF16)

    def use_om(om, rc):
        om_s[proj_rows(rc)[0], :] = om

    def use_g(g, rc):
        g_s[proj_rows(rc)[0], :] = g + gb_ref[...]

    def columns(start, width):
        return lambda: win_ref[:, start:start + width]

    proj_plan = [(columns(0, D_HALF), use_q), (columns(D_HALF, 2 * LANES), use_kv),
                 (columns(base, D_HALF), use_qm), (columns(base + 2 * D_HALF, D_HALF), use_vm),
                 (columns(base + D_HALF, D_HALF), use_km), (columns(base + 3 * D_HALF, D_HALF), use_om),
                 (lambda: wg_ref[...], use_g)]

    def project(rc):
        h = proj_h(rc)
        for weight, use in proj_plan:
            use(_dot(h, weight()), rc)
            yield

    if joint:
        yield proj_h, proj_plan
    else:
        yield from _each(project, t // n_rows, static)

    kj = lax.broadcasted_iota(jnp.int32, (3 * CHUNK, 2 * CHUNK), 0)
    qi = lax.broadcasted_iota(jnp.int32, (3 * CHUNK, 2 * CHUNK), 1) % CHUNK
    band_ok = jnp.abs(kj - CHUNK - qi) <= WINDOW
    head_a = lax.broadcasted_iota(jnp.int32, (1, 2 * CHUNK), 1) < CHUNK

    n_keys = _att_keys(lc, latent)
    group = _att_group(latent)

    def attention():
        def attend(i):
            r0 = _aligned(i, CHUNK)
            rows = pl.ds(r0, CHUNK)
            if latent:
                key_pos = kj + (i - 1) * CHUNK
                mask = band_ok & (key_pos >= 0) & (key_pos < t)
                win = pl.ds(r0, 3 * CHUNK)
            for g0 in range(0, N_PAIRS, group):
                for gi in range(group):
                    p = g0 + gi
                    kv = p // (N_PAIRS // N_KV)
                    cols = slice(p * LANES, (p + 1) * LANES)
                    q2 = jnp.concatenate([qa_s[rows, cols], qb_s[rows, cols]], axis=0)
                    if latent:
                        s_s[gi, 0:lc, :] = _dot_nt(kc_s[kv], q2)
                        s_s[gi, lc:n_keys, :] = jnp.where(mask, _dot_nt(kk_s[kv, win, :], q2), -jnp.inf)
                    else:
                        s_s[gi] = _dot_nt(kk_s[kv], q2)
                yield
                maxes = []
                for gi in range(group):
                    p = g0 + gi
                    sink = jnp.where(head_a, sink_ref[2 * p:2 * p + 1, 0:1], sink_ref[2 * p + 1:2 * p + 2, 0:1])
                    mx = jnp.maximum(jnp.max(s_s[gi], axis=0, keepdims=True), sink)
                    p_s[gi] = jnp.exp(s_s[gi] - mx).astype(BF16)
                    maxes.append((sink, mx))
                yield
                for gi in range(group):
                    p = g0 + gi
                    kv = p // (N_PAIRS // N_KV)
                    cols = slice(p * LANES, (p + 1) * LANES)
                    if latent:
                        num = (_dot(vct_s[kv], p_s[gi, 0:lc, :]) + _dot(vvt_s[kv, :, win], p_s[gi, lc:n_keys, :]))
                    else:
                        num = _dot(vvt_s[kv], p_s[gi])
                    sink, mx = maxes[gi]
                    den = num[HEAD_DIM:HEAD_DIM + 1, :] + jnp.exp(sink - mx)
                    out = num[0:HEAD_DIM, :] * (1.0 / den)
                    pair = jnp.concatenate([out[:, 0:CHUNK], out[:, CHUNK:2 * CHUNK]], axis=0)
                    z_s[rows, cols] = jnp.transpose(pair).astype(BF16)
                yield

        return attend

    if latent:
        c_s[...] = cin_ref[...]
        n_s[...] = nin_ref[...]
        m_s[...] = min_ref[...]
    else:
        c_s[...] = jnp.zeros(c_s.shape, F32)
        n_s[...] = jnp.zeros(n_s.shape, F32)
        m_s[...] = jnp.zeros(m_s.shape, F32)

    see = (ti <= si, ti >= si)
    tri = tuple(m.astype(F32).astype(BF16) for m in see)
    last = (CHUNK - 1, 0)

    def mlstm(i):
        offs = (_aligned(i, CHUNK), _aligned(n_blocks - 1 - i, CHUNK))
        half = (slice(0, LANES), slice(LANES, 2 * LANES))
        a_rows, b_rows = [], []
        for d in range(2):
            gt = jnp.transpose(g_s[pl.ds(offs[d], CHUNK), :])
            lf = _log_sigmoid(gt[2 * N_HEADS:4 * N_HEADS, :])
            bcum = _dot_exact_lhs(lf, tri[d])[8 * d:8 * d + 8, :]
            a_rows.append(gt[8 * d:8 * d + 8, :] - bcum)
            b_rows.append(bcum)
        yield
        q_ns = []
        for d in range(2):
            rows = pl.ds(offs[d], CHUNK)
            for p in range(N_PAIRS):
                u = d * N_PAIRS + p
                cols = slice(p * LANES, (p + 1) * LANES)
                q2 = jnp.concatenate([qma_s[rows, cols], qmb_s[rows, cols]], axis=0)
                sr_s[u] = _dot_nt(km_s[rows, cols], q2)
                qc_s[u] = _dot_nt(c_s[u].astype(BF16), qm_s[rows, cols])
                n8 = jnp.broadcast_to(n_s[u:u + 1, :], (8, LANES)).astype(BF16)
                q_ns.append(_dot_nt(n8, q2)[0:1, :])
            yield
        stats = {}
        for d in range(2):
            for hd in range(N_HEADS):
                u, j = d * N_PAIRS + hd // 2, hd % 2
                mrow = d * N_HEADS + hd
                a_row = a_rows[d][hd:hd + 1, :]
                b_row = b_rows[d][hd:hd + 1, :]
                m_prev = m_s[mrow:mrow + 1, 0:1]
                a_col = jnp.transpose(jnp.broadcast_to(a_row, (CHUNK, CHUNK)))
                z_t = jnp.where(see[d], a_col, -jnp.inf)
                m_run = jnp.maximum(jnp.max(z_t, axis=0, keepdims=True), m_prev)
                s_t = sr_s[u, :, half[j]] * jnp.exp(z_t - m_run)
                st_s[u, :, half[j]] = s_t.astype(BF16)
                w_int = jnp.exp(m_prev - m_run)
                den = jnp.sum(s_t, axis=0, keepdims=True) + w_int * q_ns[u][:, half[j]]
                inv = 1.0 / jnp.maximum(jnp.abs(den), jnp.exp(-(b_row + m_run)))
                m_last = m_run[:, last[d]:last[d] + 1]
                stats[(u, j)] = (w_int, inv, jnp.exp(a_row - m_last), jnp.exp(m_prev - m_last))
                m_s[mrow:mrow + 1, :] = jnp.broadcast_to(b_row[:, last[d]:last[d] + 1] + m_last, (1, LANES))
            yield
        for d in range(2):
            rows = pl.ds(offs[d], CHUNK)
            ht_s = (hft_s, hbt_s)[d]
            for p in range(N_PAIRS):
                u = d * N_PAIRS + p
                cols = slice(p * LANES, (p + 1) * LANES)
                (w_a, inv_a, e_a, dec_a), (w_b, inv_b, e_b, dec_b) = stats[(u, 0)], stats[(u, 1)]
                kp = km_s[rows, cols]
                num2 = _dot(vmt_s[cols, rows], st_s[u])
                num = jnp.where(top, num2[:, half[0]], num2[:, half[1]])
                ht_s[cols, rows] = (num + jnp.where(top, w_a, w_b) * qc_s[u]) * jnp.where(top, inv_a, inv_b)
                vt_e = (vmtf_s[cols, rows] * jnp.where(top, e_a, e_b)).astype(BF16)
                c_s[u] = jnp.where(top, dec_a, dec_b) * c_s[u] + jnp.where(same_head, _dot(vt_e, kp), 0.0)
                e2 = jnp.concatenate([e_a, e_b, jnp.zeros((6, CHUNK), F32)], axis=0).astype(BF16)
                n_k = _dot(e2, kp)
                n_s[u:u + 1, :] = (jnp.where(left, dec_a, dec_b) * n_s[u:u + 1, :]
                                   + jnp.where(left, n_k[0:1, :], n_k[1:2, :]))
            yield

    attend = attention()
    yield from _each(lambda i: _alternate(attend(i), mlstm(i)), n_blocks, static, unroll=2)

    if not latent:
        for d, c_ref in enumerate((cf_ref, cb_ref)):
            for p in range(N_PAIRS):
                c_pair = jnp.transpose(c_s[d * N_PAIRS + p])
                c_ref[0, 2 * p] = c_pair[0:HEAD_DIM, 0:HEAD_DIM]
                c_ref[0, 2 * p + 1] = c_pair[HEAD_DIM:LANES, HEAD_DIM:LANES]
        no_ref[...] = n_s[...]
        mo_ref[...] = m_s[...]
        yield

    top_w = lax.broadcasted_iota(jnp.int32, (LANES, n_rows), 0) < HEAD_DIM

    def finish(rc):
        r0 = _aligned(rc, n_rows)
        rows = pl.ds(r0, n_rows)
        for p in range(N_PAIRS):
            cols = slice(p * LANES, (p + 1) * LANES)
            hm = hft_s[cols, rows] + hbt_s[cols, rows]
            sq = hm * hm
            ms_a = jnp.sum(sq[0:HEAD_DIM], axis=0, keepdims=True)
            ms_b = jnp.sum(sq[HEAD_DIM:LANES], axis=0, keepdims=True)
            ms = jnp.where(top_w, ms_a, ms_b) * (1.0 / HEAD_DIM)
            y = jnp.transpose(hm * lax.rsqrt(ms + EPS)) * hn_ref[:, cols] * jax.nn.sigmoid(om_s[rows, cols])
            z_s[rows, D_HALF + p * LANES:D_HALF + (p + 1) * LANES] = y.astype(BF16)
            if p % 2:
                yield
        def residual(o):
            o_ref[rows, :] = x_ref[rows, :] + gate1 * _rms(o, ng_ref[1:2, :])

        if joint:
            finals.append((lambda: z_s[rows, :], residual))
        else:
            residual(_dot(z_s[rows, :], wout_ref[...]))
        yield

    yield from _each(finish, t // n_rows, static)


def _att_keys(lc, latent):
    return lc + 3 * CHUNK if latent else lc


def _att_group(latent):
    return N_PAIRS


def _odd_scratch(t, lc, latent, nseq):
    pad = 2 * CHUNK if latent else 0
    att = (_att_group(latent), _att_keys(lc, latent), 2 * CHUNK)
    shapes = [
        pltpu.VMEM((t, D_HALF), BF16),
        pltpu.VMEM((t, D_HALF), BF16),
        pltpu.VMEM((N_KV, t + pad, LANES), BF16),
        pltpu.VMEM((N_KV, LANES, t + pad), BF16),
        pltpu.VMEM((t, D_HALF), BF16),
        pltpu.VMEM((t, D_HALF), BF16),
        pltpu.VMEM((t, D_HALF), BF16),
        pltpu.VMEM((t, D_HALF), BF16),
        pltpu.VMEM((D_HALF, t), BF16),
        pltpu.VMEM((D_HALF, t), F32),
        pltpu.VMEM((t, D_HALF), F32),
        pltpu.VMEM((t, LANES), F32),
        pltpu.VMEM((t, D_MODEL), BF16),
        pltpu.VMEM((D_HALF, t), F32),
        pltpu.VMEM((D_HALF, t), F32),
        pltpu.VMEM((2 * N_PAIRS, CHUNK, 2 * LANES), F32),
        pltpu.VMEM((2 * N_PAIRS, CHUNK, 2 * LANES), BF16),
        pltpu.VMEM((2 * N_PAIRS, LANES, CHUNK), F32),
        pltpu.VMEM((2 * N_PAIRS, LANES, LANES), F32),
        pltpu.VMEM((2 * N_PAIRS, LANES), F32),
        pltpu.VMEM((2 * N_HEADS, LANES), F32),
        pltpu.VMEM(att, F32),
        pltpu.VMEM(att, BF16),
    ]
    if latent:
        shapes += [pltpu.VMEM((N_KV, lc, LANES), BF16), pltpu.VMEM((N_KV, LANES, lc), BF16)]
    return [pltpu.VMEM((nseq,) + tuple(sh.shape), sh.dtype) for sh in shapes]


def _odd_common_specs(t, layer, cond_base, cond_stride, nseq):
    assert cond_stride == 0 or nseq == 1
    j = layer // 2
    return [
        _per_seq((t, D_MODEL), nseq),
        _mod_spec(layer, lambda i: cond_base + cond_stride * i),
        _layer_spec((4, D_MODEL), layer),
        _layer_spec((D_MODEL, D_IN_ODD), j),
        _const_spec((D_MODEL, LANES)),
        _const_spec((1, LANES)),
        _const_spec((N_HEADS, LANES)),
        _layer_spec((1, D_HALF), j),
        _layer_spec((D_MODEL, D_MODEL), j),
    ]


def _per_seq(shape, nseq):
    return pl.BlockSpec((nseq,) + shape, lambda i: (i,) + (0,) * len(shape))


def _odd_context(x, modv, layer, ng, w_main, w_gate, gate_bias, sink_b, hnorm, wout):
    b, t, _ = x.shape
    nseq = ODD_CTX_SEQS
    assert b % nseq == 0
    kern = functools.partial(_odd_kernel, t=t, lc=t, latent=False, nseq=nseq)
    per_seq = functools.partial(_per_seq, nseq=nseq)
    return pl.pallas_call(
        kern,
        grid=(b // nseq,),
        in_specs=_odd_common_specs(t, layer, 0, 0, nseq),
        out_specs=[per_seq((t, D_MODEL)),
                   per_seq((1, N_KV, HEAD_DIM, t)), per_seq((1, N_KV, HEAD_DIM, t)),
                   per_seq((1, N_HEADS, HEAD_DIM, HEAD_DIM)), per_seq((1, N_HEADS, HEAD_DIM, HEAD_DIM)),
                   per_seq((2 * N_PAIRS, LANES)), per_seq((2 * N_HEADS, LANES))],
        out_shape=[jax.ShapeDtypeStruct((b, t, D_MODEL), F32),
                   jax.ShapeDtypeStruct((b, 1, N_KV, HEAD_DIM, t), F32),
                   jax.ShapeDtypeStruct((b, 1, N_KV, HEAD_DIM, t), F32),
                   jax.ShapeDtypeStruct((b, 1, N_HEADS, HEAD_DIM, HEAD_DIM), F32),
                   jax.ShapeDtypeStruct((b, 1, N_HEADS, HEAD_DIM, HEAD_DIM), F32),
                   jax.ShapeDtypeStruct((b, 2 * N_PAIRS, LANES), F32),
                   jax.ShapeDtypeStruct((b, 2 * N_HEADS, LANES), F32)],
        scratch_shapes=_odd_scratch(t, t, False, nseq),
        compiler_params=_params(1),
        name="odd_mixer_context",
    )(x, modv, ng, w_main, w_gate, gate_bias, sink_b, hnorm, wout)


def _odd_latent(x, modv, layer, ng, w_main, w_gate, gate_bias, sink_b, hnorm, wout, cos_t, sin_t, kc, vc,
                c_in, n_in, m_in):
    b, t, _ = x.shape
    lc = kc.shape[2]
    kern = functools.partial(_odd_kernel, t=t, lc=lc, latent=True, nseq=1)
    per_seq = functools.partial(_per_seq, nseq=1)
    return pl.pallas_call(
        kern,
        grid=(b,),
        in_specs=_odd_common_specs(t, layer, 1, 1, 1) + [
            _const_spec((t, LANES)), _const_spec((t, LANES)),
            per_seq((N_KV, lc, LANES)), per_seq((N_KV, lc, LANES)),
            per_seq((2 * N_PAIRS, LANES, LANES)), per_seq((2 * N_PAIRS, LANES)), per_seq((2 * N_HEADS, LANES)),
        ],
        out_specs=per_seq((t, D_MODEL)),
        out_shape=jax.ShapeDtypeStruct((b, t, D_MODEL), F32),
        scratch_shapes=_odd_scratch(t, lc, True, 1),
        compiler_params=_params(1),
        name="odd_mixer_latent",
    )(x, modv, ng, w_main, w_gate, gate_bias, sink_b, hnorm, wout, cos_t, sin_t, kc, vc, c_in, n_in, m_in)


def _rope_tables(t):
    rows = t // GRID_W
    row = jnp.broadcast_to(jnp.arange(rows)[:, None], (rows, GRID_W)).reshape(t).astype(F32)
    col = jnp.broadcast_to(jnp.arange(GRID_W)[None, :], (rows, GRID_W)).reshape(t).astype(F32)
    n_freq = HEAD_DIM // 4
    inv_freq = ROPE_BASE ** (-jnp.arange(n_freq, dtype=F32) / n_freq)
    ang = jnp.concatenate([row[:, None] * inv_freq, col[:, None] * inv_freq], axis=-1)
    cos, sin = jnp.cos(ang), jnp.sin(ang)
    cos_l = jnp.tile(cos, (1, LANES // cos.shape[1]))
    sin_l = jnp.tile(jnp.concatenate([-sin, sin], axis=-1), (1, LANES // HEAD_DIM))
    return cos_l, sin_l


def _pair_blockdiag(c):
    b = c.shape[0]
    c = c.reshape(b, N_PAIRS, 2, HEAD_DIM, HEAD_DIM)
    z = jnp.zeros_like(c[:, :, 0])
    top = jnp.concatenate([c[:, :, 0], z], axis=-1)
    bot = jnp.concatenate([z, c[:, :, 1]], axis=-1)
    return jnp.concatenate([top, bot], axis=-2)


def _lane_bcast(v):
    return jnp.broadcast_to(v[..., None], v.shape + (LANES,))


def kernel(x_prompt, x_sample, c, cache_k, cache_v, state_c_fwd, state_n_fwd, state_m_fwd, state_c_bwd, state_n_bwd, state_m_bwd, c_ctx, mod_w, mod_b, norm_g, mlp_w1, mlp_w2, even_in_w, conv_a_w, conv_a_b, ln_a_g, ln_a_b, conv_b_w, even_out_w, odd_in_w, attn_sink, gate_b, hnorm_g, odd_out_w):
    n_dec = x_sample.shape[0]
    n_ctx = x_prompt.shape[0]
    cond = jnp.concatenate([c_ctx[None, :], c, jnp.zeros((COND_ROWS - 1 - n_dec, D_MODEL), F32)], axis=0)
    modv = _modulation(cond, mod_w, mod_b)

    yp, ys = x_prompt, x_sample
    w1, w2 = mlp_w1, mlp_w2

    ev = (norm_g, even_in_w.astype(BF16), conv_a_w, conv_a_b[:, None, :], ln_a_g[:, None, :], ln_a_b[:, None, :],
          conv_b_w, even_out_w.astype(BF16))
    yp = _even_layer(yp, modv, 0, 0, 0, *ev)
    ys = _even_layer(ys, modv, 0, 1, 1, *ev)
    yp, ys = _mlp_layer(yp, ys, modv, 0, norm_g, w1, w2)

    order = jnp.array([0, 2, 1, 3])
    d_main = D_IN_ODD - 4 * N_HEADS
    wg = odd_in_w[0][:, d_main:].reshape(D_MODEL, 4, N_HEADS)[:, order, :].reshape(D_MODEL, 4 * N_HEADS)
    w_gate = jnp.pad(wg, ((0, 0), (0, LANES - 4 * N_HEADS))).astype(BF16)
    gate_bias = jnp.pad(gate_b[0][order, :].reshape(1, 4 * N_HEADS), ((0, 0), (0, LANES - 4 * N_HEADS)))
    sink_b = _lane_bcast(attn_sink[0])
    odd = (1, norm_g, odd_in_w.astype(BF16), w_gate, gate_bias, sink_b, hnorm_g[:, None, :], odd_out_w.astype(BF16))

    op, k_t, v_t, c_f, c_b, n_new, m_new = _odd_context(yp, modv, *odd)
    new_k, new_v = jnp.swapaxes(k_t, -1, -2), jnp.swapaxes(v_t, -1, -2)

    t_dec = x_sample.shape[1]
    cos_t, sin_t = _rope_tables(t_dec)
    kc = jnp.concatenate([cache_k[:, 0], cache_k[:, 0]], axis=-1)
    vc = jnp.concatenate([cache_v[:, 0], cache_v[:, 0]], axis=-1)
    c_in = jnp.concatenate([_pair_blockdiag(jnp.swapaxes(state_c_fwd[:, 0], -1, -2)),
                            _pair_blockdiag(jnp.swapaxes(state_c_bwd[:, 0], -1, -2))], axis=1)
    n_in = jnp.concatenate([state_n_fwd[:, 0].reshape(n_dec, N_PAIRS, LANES),
                            state_n_bwd[:, 0].reshape(n_dec, N_PAIRS, LANES)], axis=1)
    m_in = _lane_bcast(jnp.concatenate([state_m_fwd[:, 0], state_m_bwd[:, 0]], axis=1))
    os_ = _odd_latent(ys, modv, *odd, cos_t, sin_t, kc, vc, c_in, n_in, m_in)

    yp, ys = _mlp_layer(op, os_, modv, 1, norm_g, w1, w2)

    n_f = n_new[:, :N_PAIRS].reshape(n_ctx, N_HEADS, HEAD_DIM)[:, None]
    n_b = n_new[:, N_PAIRS:].reshape(n_ctx, N_HEADS, HEAD_DIM)[:, None]
    m_f = m_new[:, :N_HEADS, 0][:, None]
    m_b = m_new[:, N_HEADS:, 0][:, None]
    return (yp, ys, new_k, new_v, c_f, n_f, m_f, c_b, n_b, m_b)
```

```python
import functools

import jax
import jax.numpy as jnp
from jax import lax
from jax.experimental import pallas as pl
from jax.experimental.pallas import tpu as pltpu

F32 = jnp.float32
BF16 = jnp.bfloat16

D_MODEL = 1024
D_FF = 4 * D_MODEL
EPS = 1e-6
D_HALF = D_MODEL // 2
CONF_WIDTH = 31
CONF_HALO = 16
HEAD_DIM = 64
N_HEADS = 8
N_PAIRS = N_HEADS // 2
N_KV = 2
LANES = 128
CHUNK = 128
WINDOW = 128
GRID_W = 64
ROPE_BASE = 10000.0
ATT_SCALE = HEAD_DIM ** -0.5
D_IN_ODD = D_HALF + 2 * N_KV * HEAD_DIM + 4 * D_HALF + 4 * N_HEADS
ROW_CHUNK = 256
MLP_ROWS = 512
MLP_STAGGER = 3
ODD_ROWS = 512
ODD_CTX_SEQS = 2
ODD_STAGGER = 3
EVEN_CONV_ROWS = 64
EVEN_CHUNKS = 2
EVEN_STAGGER = 3
COND_ROWS = 8
VMEM_LIMIT = 56 * 1024 * 1024


def _dot(a, b):
    return jnp.dot(a, b, preferred_element_type=F32)


def _dot_nt(a, b):
    return lax.dot_general(a, b, (((1,), (1,)), ((), ())), preferred_element_type=F32)


def _split3(x):
    hi = x.astype(BF16)
    r1 = x - hi.astype(F32)
    mid = r1.astype(BF16)
    lo = (r1 - mid.astype(F32)).astype(BF16)
    return hi, mid, lo


def _dot_exact_lhs(x, b01):
    hi, mid, lo = _split3(x)
    return _dot(hi, b01) + _dot(mid, b01) + _dot(lo, b01)


def _rms(x, g):
    return x * lax.rsqrt(jnp.mean(x * x, axis=-1, keepdims=True) + EPS) * g


def _norm_mod(x, g, shift, scale):
    return _rms(x, g) * (1.0 + scale) + shift


def _params(n_grid):
    return pltpu.CompilerParams(dimension_semantics=("arbitrary",) * n_grid, vmem_limit_bytes=VMEM_LIMIT)


def _const_spec(shape):
    zeros = (0,) * len(shape)
    return pl.BlockSpec(shape, lambda *_: zeros, pipeline_mode=pl.Buffered(1))


def _layer_spec(shape, layer):
    index = (layer,) + (0,) * len(shape)
    return pl.BlockSpec((None,) + shape, lambda *_: index, pipeline_mode=pl.Buffered(1))


def _mod_spec(layer, cond_of):
    return pl.BlockSpec((None, None, 6, D_MODEL), lambda *idx: (layer, cond_of(*idx), 0, 0))


def _mod_kernel(cond_ref, w_ref, b_ref, o_ref):
    s = jax.nn.silu(cond_ref[...]).astype(BF16)
    o_ref[...] = _dot(s, w_ref[...].astype(BF16)) + b_ref[...]


def _modulation(cond, mod_w, mod_b):
    depth = mod_w.shape[0]
    n_out = mod_w.shape[2]
    tn = 2 * D_MODEL
    out = pl.pallas_call(
        _mod_kernel,
        grid=(depth, n_out // tn),
        in_specs=[
            pl.BlockSpec((COND_ROWS, D_MODEL), lambda l, j: (0, 0)),
            pl.BlockSpec((None, D_MODEL, tn), lambda l, j: (l, 0, j)),
            pl.BlockSpec((None, 1, tn), lambda l, j: (l, 0, j)),
        ],
        out_specs=pl.BlockSpec((None, COND_ROWS, tn), lambda l, j: (l, 0, j)),
        out_shape=jax.ShapeDtypeStruct((depth, COND_ROWS, n_out), F32),
        compiler_params=_params(2),
        name="modulation",
    )(cond, mod_w, mod_b.reshape(depth, 1, n_out))
    return out.reshape(depth, COND_ROWS, 6, D_MODEL)


def _run_staggered(programs, stagger):
    programs = list(programs)
    live, rounds = [], 0
    while programs or live:
        if programs and rounds % stagger == 0:
            live.append(programs.pop(0))
        for g in list(live):
            try:
                next(g)
            except StopIteration:
                live.remove(g)
        rounds += 1


def _even_kernel(*refs, n_chunks, nsub):
    halos, (xc_ref, mod_ref, ng_ref, win_ref, caw_ref, cab_ref, lng_ref, lnb_ref, cbw_ref, wout_ref, o_ref,
            apad, cpad, bgs, zs) = refs[:2 * nsub], refs[2 * nsub:]
    shared = (mod_ref, ng_ref, win_ref, caw_ref, cab_ref, lng_ref, lnb_ref, cbw_ref, wout_ref)
    _run_staggered(
        (_even_chunk(pl.program_id(0) * nsub + k, halos[2 * k], xc_ref.at[k], halos[2 * k + 1], *shared,
                     o_ref.at[k], apad.at[k], cpad.at[k], bgs.at[k], zs.at[k], n_chunks=n_chunks)
         for k in range(nsub)), EVEN_STAGGER)


def _even_chunk(g, xp_ref, xc_ref, xn_ref, mod_ref, ng_ref, win_ref, caw_ref, cab_ref, lng_ref, lnb_ref,
                cbw_ref, wout_ref, o_ref, apad, cpad, bgs, zs, *, n_chunks):
    c = g % n_chunks if n_chunks > 1 else 0
    rows = ROW_CHUNK + 2 * CONF_HALO
    shift1, scale1, gate1 = mod_ref[0:1, :], mod_ref[1:2, :], mod_ref[2:3, :]
    own = slice(CONF_HALO, CONF_HALO + ROW_CHUNK)
    if n_chunks == 1:
        h = _norm_mod(xc_ref[...], ng_ref[0:1, :], shift1, scale1).astype(BF16)
        h_own = h
        for pad_ref in (apad, cpad):
            pad_ref[0:CONF_HALO, :] = jnp.zeros((CONF_HALO, D_HALF), F32)
            pad_ref[CONF_HALO + ROW_CHUNK:rows, :] = jnp.zeros((CONF_HALO, D_HALF), F32)
        keep = lambda v: v
        span = own
    else:
        xh = jnp.concatenate([xp_ref[...], xc_ref[...], xn_ref[...]], axis=0)
        h = _norm_mod(xh, ng_ref[0:1, :], shift1, scale1).astype(BF16)
        h_own = h[own]
        ri = lax.broadcasted_iota(jnp.int32, (rows, D_HALF), 0)
        lo = jnp.where(c == 0, CONF_HALO, 0)
        hi = jnp.where(c == n_chunks - 1, CONF_HALO + ROW_CHUNK, rows)
        inside = (ri >= lo) & (ri < hi)
        keep = lambda v: jnp.where(inside, v, 0.0)
        span = slice(0, rows)
    a = _dot(h, win_ref[:, 0:D_HALF]) * jax.nn.sigmoid(_dot(h, win_ref[:, D_HALF:2 * D_HALF]))
    apad[span, :] = keep(a)
    yield
    cx = _dot(h, win_ref[:, 3 * D_HALF:4 * D_HALF]) * _dot(h, win_ref[:, 4 * D_HALF:5 * D_HALF])
    cpad[span, :] = keep(cx)
    yield
    bgs[...] = _dot(h_own, win_ref[:, 2 * D_HALF:3 * D_HALF])
    yield

    sub = EVEN_CONV_ROWS
    tile = 8
    for j in range(ROW_CHUNK // sub):
        r0 = j * sub
        groups = []
        for cg in range(D_HALF // LANES):
            cols = slice(cg * LANES, (cg + 1) * LANES)
            acc = None
            for r in range(tile):
                part = None
                for m in range(-(-(CONF_WIDTH + 1) // tile)):
                    o = tile * m + r
                    if 1 <= o <= CONF_WIDTH:
                        term = caw_ref[o - 1:o, cols] * apad[r0 + tile * m:r0 + tile * m + sub + tile, cols]
                        part = term if part is None else part + term
                shifted = part[r:r + sub, :]
                acc = shifted if acc is None else acc + shifted
            groups.append(acc)
        acc = jnp.concatenate(groups, axis=1) + cab_ref[...]
        mu = jnp.mean(acc, axis=-1, keepdims=True)
        dlt = acc - mu
        var = jnp.mean(dlt * dlt, axis=-1, keepdims=True)
        a_out = jax.nn.silu(dlt * lax.rsqrt(var + EPS) * lng_ref[...] + lnb_ref[...])
        zs[r0:r0 + sub, 0:D_HALF] = a_out.astype(BF16)
        base = r0 + CONF_HALO - 1
        sc = (cbw_ref[0:1, :] * cpad[base:base + sub, :]
              + cbw_ref[1:2, :] * cpad[base + 1:base + 1 + sub, :]
              + cbw_ref[2:3, :] * cpad[base + 2:base + 2 + sub, :])
        zs[r0:r0 + sub, D_HALF:D_MODEL] = (bgs[r0:r0 + sub, :] * sc).astype(BF16)
        yield

    o = _dot(zs[...], wout_ref[...])
    o_ref[...] = xc_ref[...] + gate1 * _rms(o, ng_ref[1:2, :])
    yield


def _even_layer(x, modv, layer, cond_base, cond_stride, ng, win, caw, cab, lng, lnb, cbw, wout):
    j = layer // 2
    b, t, _ = x.shape
    n_chunks = t // ROW_CHUNK
    nsub = EVEN_CHUNKS
    assert (b * n_chunks) % nsub == 0 and (cond_stride == 0 or n_chunks % nsub == 0)
    hpc = ROW_CHUNK // CONF_HALO
    n_halo_blocks = b * t // CONF_HALO
    rows = ROW_CHUNK + 2 * CONF_HALO
    halo_specs = []
    for k in range(nsub):
        halo_specs += [
            pl.BlockSpec((None, CONF_HALO, D_MODEL),
                         lambda i, k=k: (jnp.maximum((i * nsub + k) * hpc - 1, 0), 0, 0)),
            pl.BlockSpec((None, CONF_HALO, D_MODEL),
                         lambda i, k=k: (jnp.minimum((i * nsub + k + 1) * hpc, n_halo_blocks - 1), 0, 0)),
        ]
    x_halo = x.reshape(n_halo_blocks, CONF_HALO, D_MODEL)
    x_chunks = x.reshape(b * n_chunks, ROW_CHUNK, D_MODEL)
    kern = functools.partial(_even_kernel, n_chunks=n_chunks, nsub=nsub)
    out = pl.pallas_call(
        kern,
        grid=(b * n_chunks // nsub,),
        in_specs=halo_specs + [
            pl.BlockSpec((nsub, ROW_CHUNK, D_MODEL), lambda i: (i, 0, 0)),
            _mod_spec(layer, lambda i: cond_base + cond_stride * ((i * nsub) // n_chunks)),
            _layer_spec((4, D_MODEL), layer),
            _layer_spec((D_MODEL, 5 * D_HALF), j),
            _layer_spec((CONF_WIDTH, D_HALF), j),
            _layer_spec((1, D_HALF), j),
            _layer_spec((1, D_HALF), j),
            _layer_spec((1, D_HALF), j),
            _layer_spec((3, D_HALF), j),
            _layer_spec((D_MODEL, D_MODEL), j),
        ],
        out_specs=pl.BlockSpec((nsub, ROW_CHUNK, D_MODEL), lambda i: (i, 0, 0)),
        out_shape=jax.ShapeDtypeStruct(x_chunks.shape, F32),
        scratch_shapes=[
            pltpu.VMEM((nsub, rows, D_HALF), F32),
            pltpu.VMEM((nsub, rows, D_HALF), F32),
            pltpu.VMEM((nsub, ROW_CHUNK, D_HALF), F32),
            pltpu.VMEM((nsub, ROW_CHUNK, D_MODEL), BF16),
        ],
        compiler_params=_params(1),
        name="even_mixer",
    )(*([x_halo] * (2 * nsub)), x_chunks, modv, ng, win, caw, cab, lng, lnb, cbw, wout)
    return out.reshape(b, t, D_MODEL)


def _mlp_kernel(xp_ref, xs_ref, mod_ref, ng_ref, w1_hbm, w2_hbm, op_ref, os_ref, w1_s, w2_s, stage, sem, *,
                layer, n_ctx_tiles):
    n_blocks = D_FF // D_MODEL

    def block_copy(k):
        c, slot = k // 2, k % 2
        span = pl.ds(c * D_MODEL, D_MODEL)
        src = w1_hbm.at[layer, :, span] if k % 2 == 0 else w2_hbm.at[layer, span, :]
        return pltpu.make_async_copy(src, stage.at[slot], sem.at[slot])

    def fetch(k, dst):
        block_copy(k).wait()
        dst[...] = stage[k % 2].astype(BF16)
        if k + 2 < 2 * n_blocks:
            block_copy(k + 2).start()

    def rows_program(x_ref, o_ref, rows, load):
        x = x_ref[rows, :]
        h = _norm_mod(x, ng_ref[2:3, :], mod_ref[3:4, :], mod_ref[4:5, :]).astype(BF16)
        yield
        acc = None
        for c in range(n_blocks):
            cols = slice(c * D_MODEL, (c + 1) * D_MODEL)
            if load:
                fetch(2 * c, w1_s.at[:, cols])
            hid = jnp.square(jnp.maximum(_dot(h, w1_s[:, cols]), 0.0)).astype(BF16)
            yield
            if load:
                fetch(2 * c + 1, w2_s.at[cols, :])
            part = _dot(hid, w2_s[cols, :])
            acc = part if acc is None else acc + part
            yield
        o_ref[rows, :] = x + mod_ref[5:6, :] * _rms(acc, ng_ref[3:4, :])
        yield

    def tile(x_ref, o_ref, load=False):
        n_rows = x_ref.shape[0]
        if load:
            _drain(rows_program(x_ref, o_ref, slice(0, n_rows), True))
        else:
            _run_staggered((rows_program(x_ref, o_ref, slice(r0, r0 + n_rows // 2), False)
                            for r0 in (0, n_rows // 2)), MLP_STAGGER)

    i = pl.program_id(0)

    @pl.when(i == 0)
    def _():
        block_copy(0).start()
        block_copy(1).start()
        tile(xp_ref, op_ref, load=True)

    @pl.when((i > 0) & (i < n_ctx_tiles))
    def _():
        tile(xp_ref, op_ref)

    @pl.when(i >= n_ctx_tiles)
    def _():
        tile(xs_ref, os_ref)


def _mlp_layer(xp, xs, modv, layer, ng, w1, w2):
    tm = MLP_ROWS
    xp2, xs2 = xp.reshape(-1, D_MODEL), xs.reshape(-1, D_MODEL)
    t_dec = xs.shape[1]
    assert xp2.shape[0] % tm == 0 and t_dec % tm == 0
    n_p, n_s, per_seq = xp2.shape[0] // tm, xs2.shape[0] // tm, t_dec // tm
    ctx_tile = lambda i: (jnp.minimum(i, n_p - 1), 0)
    lat_tile = lambda i: (jnp.maximum(i - n_p, 0), 0)
    yp, ys = pl.pallas_call(
        functools.partial(_mlp_kernel, layer=layer, n_ctx_tiles=n_p),
        grid=(n_p + n_s,),
        in_specs=[
            pl.BlockSpec((tm, D_MODEL), ctx_tile),
            pl.BlockSpec((tm, D_MODEL), lat_tile),
            _mod_spec(layer, lambda i: jnp.where(i < n_p, 0, 1 + jnp.maximum(i - n_p, 0) // per_seq)),
            _layer_spec((4, D_MODEL), layer),
            pl.BlockSpec(memory_space=pl.ANY),
            pl.BlockSpec(memory_space=pl.ANY),
        ],
        out_specs=[pl.BlockSpec((tm, D_MODEL), ctx_tile), pl.BlockSpec((tm, D_MODEL), lat_tile)],
        out_shape=[jax.ShapeDtypeStruct(xp2.shape, F32), jax.ShapeDtypeStruct(xs2.shape, F32)],
        scratch_shapes=[
            pltpu.VMEM((D_MODEL, D_FF), BF16),
            pltpu.VMEM((D_FF, D_MODEL), BF16),
            pltpu.VMEM((2, D_MODEL, D_MODEL), F32),
            pltpu.SemaphoreType.DMA((2,)),
        ],
        compiler_params=_params(1),
        name="mlp",
    )(xp2, xs2, modv, ng, w1, w2)
    return yp.reshape(xp.shape), ys.reshape(xs.shape)


def _log_sigmoid(x):
    return jnp.minimum(x, 0.0) - jnp.log(1.0 + jnp.exp(-jnp.abs(x)))


def _rot_half(x, first_half):
    return jnp.where(first_half, pltpu.roll(x, 96, axis=1), pltpu.roll(x, 32, axis=1))


def _values_and_ones(v):
    vt = jnp.transpose(v)
    row = lax.broadcasted_iota(jnp.int32, vt.shape, 0)
    return jnp.where(row < HEAD_DIM, vt, 1.0)


def _aligned(i, m):
    return i * m if isinstance(i, int) else pl.multiple_of(i * m, m)


def _drain(pieces):
    for _ in pieces:
        pass


def _each(body, n, static, unroll=1):
    if static:
        for i in range(n):
            yield from body(i)
    else:
        lax.fori_loop(0, n, lambda i, c: (_drain(body(i)), c)[1], 0, unroll=unroll)


def _alternate(*programs):
    live = list(programs)
    while live:
        for g in list(live):
            try:
                next(g)
            except StopIteration:
                live.remove(g)
            else:
                yield


def _odd_kernel(*refs, t, lc, latent, nseq):
    shared = set(range(1, 11 if latent else 9))
    n_rows = min(t, ODD_ROWS)
    joint = not latent and nseq > 1 and t == n_rows
    finals = []
    programs = [_odd_seq(*[r if k in shared else r.at[sq] for k, r in enumerate(refs)],
                         t=t, lc=lc, latent=latent, joint=joint, finals=finals) for sq in range(nseq)]
    if joint:
        parts = [next(g) for g in programs]
        h_all = jnp.concatenate([proj_h(0) for proj_h, _ in parts], axis=0)
        for idx, (weight, _) in enumerate(parts[0][1]):
            y = _dot(h_all, weight())
            for sq, (_, plan) in enumerate(parts):
                plan[idx][1](y[sq * n_rows:(sq + 1) * n_rows], 0)
    _run_staggered(programs, ODD_STAGGER)
    if joint:
        o = _dot(jnp.concatenate([z() for z, _ in finals], axis=0), refs[8][...])
        for sq, (_, residual) in enumerate(finals):
            residual(o[sq * n_rows:(sq + 1) * n_rows])


def _odd_seq(*refs, t, lc, latent, joint, finals):
    if latent:
        (x_ref, mod_ref, ng_ref, win_ref, wg_ref, gb_ref, sink_ref, hn_ref, wout_ref,
         cos_ref, sin_ref, kc_ref, vc_ref, cin_ref, nin_ref, min_ref,
         o_ref,
         qa_s, qb_s, kk_s, vvt_s, qma_s, qmb_s, qm_s, km_s, vmt_s, vmtf_s, om_s, g_s, z_s, hft_s, hbt_s,
         sr_s, st_s, qc_s, c_s, n_s, m_s, s_s, p_s, kc_s, vct_s) = refs
    else:
        (x_ref, mod_ref, ng_ref, win_ref, wg_ref, gb_ref, sink_ref, hn_ref, wout_ref,
         o_ref, ko_ref, vo_ref, cf_ref, cb_ref, no_ref, mo_ref,
         qa_s, qb_s, kk_s, vvt_s, qma_s, qmb_s, qm_s, km_s, vmt_s, vmtf_s, om_s, g_s, z_s, hft_s, hbt_s,
         sr_s, st_s, qc_s, c_s, n_s, m_s, s_s, p_s) = refs

    static = not latent
    n_blocks = t // CHUNK
    pad = CHUNK if latent else 0
    shift1, scale1, gate1 = mod_ref[0:1, :], mod_ref[1:2, :], mod_ref[2:3, :]
    lane = lax.broadcasted_iota(jnp.int32, (1, LANES), 1)
    left = lane < HEAD_DIM
    first_half = (lane % HEAD_DIM) < (HEAD_DIM // 2)
    ti = lax.broadcasted_iota(jnp.int32, (CHUNK, CHUNK), 0)
    si = lax.broadcasted_iota(jnp.int32, (CHUNK, CHUNK), 1)
    top = ti < HEAD_DIM
    same_head = top == (si < HEAD_DIM)

    n_rows = min(t, ODD_ROWS)
    if latent:
        for kv in range(N_KV):
            kk_s[kv, 0:CHUNK, :] = jnp.zeros((CHUNK, LANES), BF16)
            kk_s[kv, CHUNK + t:2 * CHUNK + t, :] = jnp.zeros((CHUNK, LANES), BF16)
            vvt_s[kv, :, 0:CHUNK] = jnp.zeros((LANES, CHUNK), BF16)
            vvt_s[kv, :, CHUNK + t:2 * CHUNK + t] = jnp.zeros((LANES, CHUNK), BF16)
            kc_s[kv] = kc_ref[kv].astype(BF16)
            vct_s[kv] = _values_and_ones(vc_ref[kv]).astype(BF16)

    base = D_HALF + 2 * LANES

    def proj_rows(rc):
        r0 = _aligned(rc, n_rows)
        return pl.ds(r0, n_rows), pl.ds(r0 + pad, n_rows)

    def proj_h(rc):
        rows, _ = proj_rows(rc)
        return _norm_mod(x_ref[rows, :], ng_ref[0:1, :], shift1, scale1).astype(BF16)

    def rope(v, rows):
        return v * cos_ref[rows, :] + _rot_half(v, first_half) * sin_ref[rows, :]

    def use_q(q, rc):
        rows, _ = proj_rows(rc)
        for p in range(N_PAIRS):
            cols = slice(p * LANES, (p + 1) * LANES)
            qp = q[:, cols]
            if latent:
                qp = rope(qp, rows)
            qp = qp * ATT_SCALE
            qa_s[rows, cols] = jnp.where(left, qp, 0.0).astype(BF16)
            qb_s[rows, cols] = jnp.where(left, 0.0, qp).astype(BF16)

    def use_kv(kv2, rc):
        rows, krows = proj_rows(rc)
        ka, va = kv2[:, 0:LANES], kv2[:, LANES:2 * LANES]
        if latent:
            ka = rope(ka, rows)
        kr = pltpu.roll(ka, HEAD_DIM, axis=1)
        vr = pltpu.roll(va, HEAD_DIM, axis=1)
        if not latent:
            ka_t, va_t = jnp.transpose(ka), jnp.transpose(va)
            for kv in range(N_KV):
                ko_ref[0, kv, :, rows] = ka_t[kv * HEAD_DIM:(kv + 1) * HEAD_DIM, :]
                vo_ref[0, kv, :, rows] = va_t[kv * HEAD_DIM:(kv + 1) * HEAD_DIM, :]
        kk_s[0, krows, :] = jnp.where(left, ka, kr).astype(BF16)
        kk_s[1, krows, :] = jnp.where(left, kr, ka).astype(BF16)
        vvt_s[0, :, krows] = _values_and_ones(va).astype(BF16)
        vvt_s[1, :, krows] = _values_and_ones(vr).astype(BF16)

    def use_qm(qm, rc):
        rows, _ = proj_rows(rc)
        qm_s[rows, :] = qm.astype(BF16)
        for p in range(N_PAIRS):
            cols = slice(p * LANES, (p + 1) * LANES)
            qma_s[rows, cols] = jnp.where(left, qm[:, cols], 0.0).astype(BF16)
            qmb_s[rows, cols] = jnp.where(left, 0.0, qm[:, cols]).astype(BF16)

    def use_vm(vm, rc):
        rows, _ = proj_rows(rc)
        for p in range(N_PAIRS):
            cols = slice(p * LANES, (p + 1) * LANES)
            vt = jnp.transpose(vm[:, cols])
            vmtf_s[cols, rows] = vt
            vmt_s[cols, rows] = vt.astype(BF16)

    def use_km(km, rc):
        km_s[proj_rows(rc)[0], :] = (km * (HEAD_DIM ** -0.5)).astype(BF16)

    def use_om(om, rc):
        om_s[proj_rows(rc)[0], :] = om

    def use_g(g, rc):
        g_s[proj_rows(rc)[0], :] = g + gb_ref[...]

    def columns(start, width):
        return lambda: win_ref[:, start:start + width]

    proj_plan = [(columns(0, D_HALF), use_q), (columns(D_HALF, 2 * LANES), use_kv),
                 (columns(base, D_HALF), use_qm), (columns(base + 2 * D_HALF, D_HALF), use_vm),
                 (columns(base + D_HALF, D_HALF), use_km), (columns(base + 3 * D_HALF, D_HALF), use_om),
                 (lambda: wg_ref[...], use_g)]

    def project(rc):
        h = proj_h(rc)
        for weight, use in proj_plan:
            use(_dot(h, weight()), rc)
            yield

    if joint:
        yield proj_h, proj_plan
    else:
        yield from _each(project, t // n_rows, static)

    kj = lax.broadcasted_iota(jnp.int32, (3 * CHUNK, 2 * CHUNK), 0)
    qi = lax.broadcasted_iota(jnp.int32, (3 * CHUNK, 2 * CHUNK), 1) % CHUNK
    band_ok = jnp.abs(kj - CHUNK - qi) <= WINDOW
    head_a = lax.broadcasted_iota(jnp.int32, (1, 2 * CHUNK), 1) < CHUNK

    n_keys = _att_keys(lc, latent)
    group = _att_group(latent)

    def attention():
        def attend(i):
            r0 = _aligned(i, CHUNK)
            rows = pl.ds(r0, CHUNK)
            if latent:
                key_pos = kj + (i - 1) * CHUNK
                mask = band_ok & (key_pos >= 0) & (key_pos < t)
                win = pl.ds(r0, 3 * CHUNK)
            for g0 in range(0, N_PAIRS, group):
                for gi in range(group):
                    p = g0 + gi
                    kv = p // (N_PAIRS // N_KV)
                    cols = slice(p * LANES, (p + 1) * LANES)
                    q2 = jnp.concatenate([qa_s[rows, cols], qb_s[rows, cols]], axis=0)
                    if latent:
                        s_s[gi, 0:lc, :] = _dot_nt(kc_s[kv], q2)
                        s_s[gi, lc:n_keys, :] = jnp.where(mask, _dot_nt(kk_s[kv, win, :], q2), -jnp.inf)
                    else:
                        s_s[gi] = _dot_nt(kk_s[kv], q2)
                yield
                maxes = []
                for gi in range(group):
                    p = g0 + gi
                    sink = jnp.where(head_a, sink_ref[2 * p:2 * p + 1, 0:1], sink_ref[2 * p + 1:2 * p + 2, 0:1])
                    mx = jnp.maximum(jnp.max(s_s[gi], axis=0, keepdims=True), sink)
                    p_s[gi] = jnp.exp(s_s[gi] - mx).astype(BF16)
                    maxes.append((sink, mx))
                yield
                for gi in range(group):
                    p = g0 + gi
                    kv = p // (N_PAIRS // N_KV)
                    cols = slice(p * LANES, (p + 1) * LANES)
                    if latent:
                        num = (_dot(vct_s[kv], p_s[gi, 0:lc, :]) + _dot(vvt_s[kv, :, win], p_s[gi, lc:n_keys, :]))
                    else:
                        num = _dot(vvt_s[kv], p_s[gi])
                    sink, mx = maxes[gi]
                    den = num[HEAD_DIM:HEAD_DIM + 1, :] + jnp.exp(sink - mx)
                    out = num[0:HEAD_DIM, :] * (1.0 / den)
                    pair = jnp.concatenate([out[:, 0:CHUNK], out[:, CHUNK:2 * CHUNK]], axis=0)
                    z_s[rows, cols] = jnp.transpose(pair).astype(BF16)
                yield

        return attend

    if latent:
        c_s[...] = cin_ref[...]
        n_s[...] = nin_ref[...]
        m_s[...] = min_ref[...]
    else:
        c_s[...] = jnp.zeros(c_s.shape, F32)
        n_s[...] = jnp.zeros(n_s.shape, F32)
        m_s[...] = jnp.zeros(m_s.shape, F32)

    see = (ti <= si, ti >= si)
    tri = tuple(m.astype(F32).astype(BF16) for m in see)
    last = (CHUNK - 1, 0)

    def mlstm(i):
        offs = (_aligned(i, CHUNK), _aligned(n_blocks - 1 - i, CHUNK))
        half = (slice(0, LANES), slice(LANES, 2 * LANES))
        a_rows, b_rows = [], []
        for d in range(2):
            gt = jnp.transpose(g_s[pl.ds(offs[d], CHUNK), :])
            lf = _log_sigmoid(gt[2 * N_HEADS:4 * N_HEADS, :])
            bcum = _dot_exact_lhs(lf, tri[d])[8 * d:8 * d + 8, :]
            a_rows.append(gt[8 * d:8 * d + 8, :] - bcum)
            b_rows.append(bcum)
        yield
        q_ns = []
        for d in range(2):
            rows = pl.ds(offs[d], CHUNK)
            for p in range(N_PAIRS):
                u = d * N_PAIRS + p
                cols = slice(p * LANES, (p + 1) * LANES)
                q2 = jnp.concatenate([qma_s[rows, cols], qmb_s[rows, cols]], axis=0)
                sr_s[u] = _dot_nt(km_s[rows, cols], q2)
                qc_s[u] = _dot_nt(c_s[u].astype(BF16), qm_s[rows, cols])
                n8 = jnp.broadcast_to(n_s[u:u + 1, :], (8, LANES)).astype(BF16)
                q_ns.append(_dot_nt(n8, q2)[0:1, :])
            yield
        stats = {}
        for d in range(2):
            for hd in range(N_HEADS):
                u, j = d * N_PAIRS + hd // 2, hd % 2
                mrow = d * N_HEADS + hd
                a_row = a_rows[d][hd:hd + 1, :]
                b_row = b_rows[d][hd:hd + 1, :]
                m_prev = m_s[mrow:mrow + 1, 0:1]
                a_col = jnp.transpose(jnp.broadcast_to(a_row, (CHUNK, CHUNK)))
                z_t = jnp.where(see[d], a_col, -jnp.inf)
                m_run = jnp.maximum(jnp.max(z_t, axis=0, keepdims=True), m_prev)
                s_t = sr_s[u, :, half[j]] * jnp.exp(z_t - m_run)
                st_s[u, :, half[j]] = s_t.astype(BF16)
                w_int = jnp.exp(m_prev - m_run)
                den = jnp.sum(s_t, axis=0, keepdims=True) + w_int * q_ns[u][:, half[j]]
                inv = 1.0 / jnp.maximum(jnp.abs(den), jnp.exp(-(b_row + m_run)))
                m_last = m_run[:, last[d]:last[d] + 1]
                stats[(u, j)] = (w_int, inv, jnp.exp(a_row - m_last), jnp.exp(m_prev - m_last))
                m_s[mrow:mrow + 1, :] = jnp.broadcast_to(b_row[:, last[d]:last[d] + 1] + m_last, (1, LANES))
            yield
        for d in range(2):
            rows = pl.ds(offs[d], CHUNK)
            ht_s = (hft_s, hbt_s)[d]
            for p in range(N_PAIRS):
                u = d * N_PAIRS + p
                cols = slice(p * LANES, (p + 1) * LANES)
                (w_a, inv_a, e_a, dec_a), (w_b, inv_b, e_b, dec_b) = stats[(u, 0)], stats[(u, 1)]
                kp = km_s[rows, cols]
                num2 = _dot(vmt_s[cols, rows], st_s[u])
                num = jnp.where(top, num2[:, half[0]], num2[:, half[1]])
                ht_s[cols, rows] = (num + jnp.where(top, w_a, w_b) * qc_s[u]) * jnp.where(top, inv_a, inv_b)
                vt_e = (vmtf_s[cols, rows] * jnp.where(top, e_a, e_b)).astype(BF16)
                c_s[u] = jnp.where(top, dec_a, dec_b) * c_s[u] + jnp.where(same_head, _dot(vt_e, kp), 0.0)
                e2 = jnp.concatenate([e_a, e_b, jnp.zeros((6, CHUNK), F32)], axis=0).astype(BF16)
                n_k = _dot(e2, kp)
                n_s[u:u + 1, :] = (jnp.where(left, dec_a, dec_b) * n_s[u:u + 1, :]
                                   + jnp.where(left, n_k[0:1, :], n_k[1:2, :]))
            yield

    attend = attention()
    yield from _each(lambda i: _alternate(attend(i), mlstm(i)), n_blocks, static, unroll=2)

    if not latent:
        for d, c_ref in enumerate((cf_ref, cb_ref)):
            for p in range(N_PAIRS):
                c_pair = jnp.transpose(c_s[d * N_PAIRS + p])
                c_ref[0, 2 * p] = c_pair[0:HEAD_DIM, 0:HEAD_DIM]
                c_ref[0, 2 * p + 1] = c_pair[HEAD_DIM:LANES, HEAD_DIM:LANES]
        no_ref[...] = n_s[...]
        mo_ref[...] = m_s[...]
        yield

    top_w = lax.broadcasted_iota(jnp.int32, (LANES, n_rows), 0) < HEAD_DIM

    def finish(rc):
        r0 = _aligned(rc, n_rows)
        rows = pl.ds(r0, n_rows)
        for p in range(N_PAIRS):
            cols = slice(p * LANES, (p + 1) * LANES)
            hm = hft_s[cols, rows] + hbt_s[cols, rows]
            sq = hm * hm
            ms_a = jnp.sum(sq[0:HEAD_DIM], axis=0, keepdims=True)
            ms_b = jnp.sum(sq[HEAD_DIM:LANES], axis=0, keepdims=True)
            ms = jnp.where(top_w, ms_a, ms_b) * (1.0 / HEAD_DIM)
            y = jnp.transpose(hm * lax.rsqrt(ms + EPS)) * hn_ref[:, cols] * jax.nn.sigmoid(om_s[rows, cols])
            z_s[rows, D_HALF + p * LANES:D_HALF + (p + 1) * LANES] = y.astype(BF16)
            if p % 2:
                yield
        def residual(o):
            o_ref[rows, :] = x_ref[rows, :] + gate1 * _rms(o, ng_ref[1:2, :])

        if joint:
            finals.append((lambda: z_s[rows, :], residual))
        else:
            residual(_dot(z_s[rows, :], wout_ref[...]))
        yield

    yield from _each(finish, t // n_rows, static)


def _att_keys(lc, latent):
    return lc + 3 * CHUNK if latent else lc


def _att_group(latent):
    return N_PAIRS


def _odd_scratch(t, lc, latent, nseq):
    pad = 2 * CHUNK if latent else 0
    att = (_att_group(latent), _att_keys(lc, latent), 2 * CHUNK)
    shapes = [
        pltpu.VMEM((t, D_HALF), BF16),
        pltpu.VMEM((t, D_HALF), BF16),
        pltpu.VMEM((N_KV, t + pad, LANES), BF16),
        pltpu.VMEM((N_KV, LANES, t + pad), BF16),
        pltpu.VMEM((t, D_HALF), BF16),
        pltpu.VMEM((t, D_HALF), BF16),
        pltpu.VMEM((t, D_HALF), BF16),
        pltpu.VMEM((t, D_HALF), BF16),
        pltpu.VMEM((D_HALF, t), BF16),
        pltpu.VMEM((D_HALF, t), F32),
        pltpu.VMEM((t, D_HALF), F32),
        pltpu.VMEM((t, LANES), F32),
        pltpu.VMEM((t, D_MODEL), BF16),
        pltpu.VMEM((D_HALF, t), F32),
        pltpu.VMEM((D_HALF, t), F32),
        pltpu.VMEM((2 * N_PAIRS, CHUNK, 2 * LANES), F32),
        pltpu.VMEM((2 * N_PAIRS, CHUNK, 2 * LANES), BF16),
        pltpu.VMEM((2 * N_PAIRS, LANES, CHUNK), F32),
        pltpu.VMEM((2 * N_PAIRS, LANES, LANES), F32),
        pltpu.VMEM((2 * N_PAIRS, LANES), F32),
        pltpu.VMEM((2 * N_HEADS, LANES), F32),
        pltpu.VMEM(att, F32),
        pltpu.VMEM(att, BF16),
    ]
    if latent:
        shapes += [pltpu.VMEM((N_KV, lc, LANES), BF16), pltpu.VMEM((N_KV, LANES, lc), BF16)]
    return [pltpu.VMEM((nseq,) + tuple(sh.shape), sh.dtype) for sh in shapes]


def _odd_common_specs(t, layer, cond_base, cond_stride, nseq):
    assert cond_stride == 0 or nseq == 1
    j = layer // 2
    return [
        _per_seq((t, D_MODEL), nseq),
        _mod_spec(layer, lambda i: cond_base + cond_stride * i),
        _layer_spec((4, D_MODEL), layer),
        _layer_spec((D_MODEL, D_IN_ODD), j),
        _const_spec((D_MODEL, LANES)),
        _const_spec((1, LANES)),
        _const_spec((N_HEADS, LANES)),
        _layer_spec((1, D_HALF), j),
        _layer_spec((D_MODEL, D_MODEL), j),
    ]


def _per_seq(shape, nseq):
    return pl.BlockSpec((nseq,) + shape, lambda i: (i,) + (0,) * len(shape))


def _odd_context(x, modv, layer, ng, w_main, w_gate, gate_bias, sink_b, hnorm, wout):
    b, t, _ = x.shape
    nseq = ODD_CTX_SEQS
    assert b % nseq == 0
    kern = functools.partial(_odd_kernel, t=t, lc=t, latent=False, nseq=nseq)
    per_seq = functools.partial(_per_seq, nseq=nseq)
    return pl.pallas_call(
        kern,
        grid=(b // nseq,),
        in_specs=_odd_common_specs(t, layer, 0, 0, nseq),
        out_specs=[per_seq((t, D_MODEL)),
                   per_seq((1, N_KV, HEAD_DIM, t)), per_seq((1, N_KV, HEAD_DIM, t)),
                   per_seq((1, N_HEADS, HEAD_DIM, HEAD_DIM)), per_seq((1, N_HEADS, HEAD_DIM, HEAD_DIM)),
                   per_seq((2 * N_PAIRS, LANES)), per_seq((2 * N_HEADS, LANES))],
        out_shape=[jax.ShapeDtypeStruct((b, t, D_MODEL), F32),
                   jax.ShapeDtypeStruct((b, 1, N_KV, HEAD_DIM, t), F32),
                   jax.ShapeDtypeStruct((b, 1, N_KV, HEAD_DIM, t), F32),
                   jax.ShapeDtypeStruct((b, 1, N_HEADS, HEAD_DIM, HEAD_DIM), F32),
                   jax.ShapeDtypeStruct((b, 1, N_HEADS, HEAD_DIM, HEAD_DIM), F32),
                   jax.ShapeDtypeStruct((b, 2 * N_PAIRS, LANES), F32),
                   jax.ShapeDtypeStruct((b, 2 * N_HEADS, LANES), F32)],
        scratch_shapes=_odd_scratch(t, t, False, nseq),
        compiler_params=_params(1),
        name="odd_mixer_context",
    )(x, modv, ng, w_main, w_gate, gate_bias, sink_b, hnorm, wout)


def _odd_latent(x, modv, layer, ng, w_main, w_gate, gate_bias, sink_b, hnorm, wout, cos_t, sin_t, kc, vc,
                c_in, n_in, m_in):
    b, t, _ = x.shape
    lc = kc.shape[2]
    kern = functools.partial(_odd_kernel, t=t, lc=lc, latent=True, nseq=1)
    per_seq = functools.partial(_per_seq, nseq=1)
    return pl.pallas_call(
        kern,
        grid=(b,),
        in_specs=_odd_common_specs(t, layer, 1, 1, 1) + [
            _const_spec((t, LANES)), _const_spec((t, LANES)),
            per_seq((N_KV, lc, LANES)), per_seq((N_KV, lc, LANES)),
            per_seq((2 * N_PAIRS, LANES, LANES)), per_seq((2 * N_PAIRS, LANES)), per_seq((2 * N_HEADS, LANES)),
        ],
        out_specs=per_seq((t, D_MODEL)),
        out_shape=jax.ShapeDtypeStruct((b, t, D_MODEL), F32),
        scratch_shapes=_odd_scratch(t, lc, True, 1),
        compiler_params=_params(1),
        name="odd_mixer_latent",
    )(x, modv, ng, w_main, w_gate, gate_bias, sink_b, hnorm, wout, cos_t, sin_t, kc, vc, c_in, n_in, m_in)


def _rope_tables(t):
    rows = t // GRID_W
    row = jnp.broadcast_to(jnp.arange(rows)[:, None], (rows, GRID_W)).reshape(t).astype(F32)
    col = jnp.broadcast_to(jnp.arange(GRID_W)[None, :], (rows, GRID_W)).reshape(t).astype(F32)
    n_freq = HEAD_DIM // 4
    inv_freq = ROPE_BASE ** (-jnp.arange(n_freq, dtype=F32) / n_freq)
    ang = jnp.concatenate([row[:, None] * inv_freq, col[:, None] * inv_freq], axis=-1)
    cos, sin = jnp.cos(ang), jnp.sin(ang)
    cos_l = jnp.tile(cos, (1, LANES // cos.shape[1]))
    sin_l = jnp.tile(jnp.concatenate([-sin, sin], axis=-1), (1, LANES // HEAD_DIM))
    return cos_l, sin_l


def _pair_blockdiag(c):
    b = c.shape[0]
    c = c.reshape(b, N_PAIRS, 2, HEAD_DIM, HEAD_DIM)
    z = jnp.zeros_like(c[:, :, 0])
    top = jnp.concatenate([c[:, :, 0], z], axis=-1)
    bot = jnp.concatenate([z, c[:, :, 1]], axis=-1)
    return jnp.concatenate([top, bot], axis=-2)


def _lane_bcast(v):
    return jnp.broadcast_to(v[..., None], v.shape + (LANES,))


def kernel(x_prompt, x_sample, c, cache_k, cache_v, state_c_fwd, state_n_fwd, state_m_fwd, state_c_bwd, state_n_bwd, state_m_bwd, c_ctx, mod_w, mod_b, norm_g, mlp_w1, mlp_w2, even_in_w, conv_a_w, conv_a_b, ln_a_g, ln_a_b, conv_b_w, even_out_w, odd_in_w, attn_sink, gate_b, hnorm_g, odd_out_w):
    n_dec = x_sample.shape[0]
    n_ctx = x_prompt.shape[0]
    cond = jnp.concatenate([c_ctx[None, :], c, jnp.zeros((COND_ROWS - 1 - n_dec, D_MODEL), F32)], axis=0)
    modv = _modulation(cond, mod_w, mod_b)

    yp, ys = x_prompt, x_sample
    w1, w2 = mlp_w1, mlp_w2

    ev = (norm_g, even_in_w.astype(BF16), conv_a_w, conv_a_b[:, None, :], ln_a_g[:, None, :], ln_a_b[:, None, :],
          conv_b_w, even_out_w.astype(BF16))
    yp = _even_layer(yp, modv, 0, 0, 0, *ev)
    ys = _even_layer(ys, modv, 0, 1, 1, *ev)
    yp, ys = _mlp_layer(yp, ys, modv, 0, norm_g, w1, w2)

    order = jnp.array([0, 2, 1, 3])
    d_main = D_IN_ODD - 4 * N_HEADS
    wg = odd_in_w[0][:, d_main:].reshape(D_MODEL, 4, N_HEADS)[:, order, :].reshape(D_MODEL, 4 * N_HEADS)
    w_gate = jnp.pad(wg, ((0, 0), (0, LANES - 4 * N_HEADS))).astype(BF16)
    gate_bias = jnp.pad(gate_b[0][order, :].reshape(1, 4 * N_HEADS), ((0, 0), (0, LANES - 4 * N_HEADS)))
    sink_b = _lane_bcast(attn_sink[0])
    odd = (1, norm_g, odd_in_w.astype(BF16), w_gate, gate_bias, sink_b, hnorm_g[:, None, :], odd_out_w.astype(BF16))

    op, k_t, v_t, c_f, c_b, n_new, m_new = _odd_context(yp, modv, *odd)
    new_k, new_v = jnp.swapaxes(k_t, -1, -2), jnp.swapaxes(v_t, -1, -2)

    t_dec = x_sample.shape[1]
    cos_t, sin_t = _rope_tables(t_dec)
    kc = jnp.concatenate([cache_k[:, 0], cache_k[:, 0]], axis=-1)
    vc = jnp.concatenate([cache_v[:, 0], cache_v[:, 0]], axis=-1)
    c_in = jnp.concatenate([_pair_blockdiag(jnp.swapaxes(state_c_fwd[:, 0], -1, -2)),
                            _pair_blockdiag(jnp.swapaxes(state_c_bwd[:, 0], -1, -2))], axis=1)
    n_in = jnp.concatenate([state_n_fwd[:, 0].reshape(n_dec, N_PAIRS, LANES),
                            state_n_bwd[:, 0].reshape(n_dec, N_PAIRS, LANES)], axis=1)
    m_in = _lane_bcast(jnp.concatenate([state_m_fwd[:, 0], state_m_bwd[:, 0]], axis=1))
    os_ = _odd_latent(ys, modv, *odd, cos_t, sin_t, kc, vc, c_in, n_in, m_in)

    yp, ys = _mlp_layer(op, os_, modv, 1, norm_g, w1, w2)

    n_f = n_new[:, :N_PAIRS].reshape(n_ctx, N_HEADS, HEAD_DIM)[:, None]
    n_b = n_new[:, N_PAIRS:].reshape(n_ctx, N_HEADS, HEAD_DIM)[:, None]
    m_f = m_new[:, :N_HEADS, 0][:, None]
    m_b = m_new[:, N_HEADS:, 0][:, None]
    return (yp, ys, new_k, new_v, c_f, n_f, m_f, c_b, n_b, m_b)
```

```python
import functools

import jax
import jax.numpy as jnp
from jax import lax
from jax.experimental import pallas as pl
from jax.experimental.pallas import tpu as pltpu

F32 = jnp.float32
BF16 = jnp.bfloat16

D_MODEL = 1024
D_FF = 4 * D_MODEL
EPS = 1e-6
D_HALF = D_MODEL // 2
CONF_WIDTH = 31
CONF_HALO = 16
HEAD_DIM = 64
N_HEADS = 8
N_PAIRS = N_HEADS // 2
N_KV = 2
LANES = 128
CHUNK = 128
WINDOW = 128
GRID_W = 64
ROPE_BASE = 10000.0
ATT_SCALE = HEAD_DIM ** -0.5
D_IN_ODD = D_HALF + 2 * N_KV * HEAD_DIM + 4 * D_HALF + 4 * N_HEADS
ROW_CHUNK = 256
MLP_ROWS = 512
MLP_STAGGER = 3
ATT_GROUP = N_PAIRS
ODD_ROWS = 512
ODD_CTX_SEQS = 2
ODD_STAGGER = 6
EVEN_CONV_ROWS = 64
EVEN_CHUNKS = 2
EVEN_STAGGER = 3
COND_ROWS = 8
VMEM_LIMIT = 56 * 1024 * 1024


def _dot(a, b):
    return jnp.dot(a, b, preferred_element_type=F32)


def _dot_nt(a, b):
    return lax.dot_general(a, b, (((1,), (1,)), ((), ())), preferred_element_type=F32)


def _split3(x):
    hi = x.astype(BF16)
    r1 = x - hi.astype(F32)
    mid = r1.astype(BF16)
    lo = (r1 - mid.astype(F32)).astype(BF16)
    return hi, mid, lo


def _dot_exact_lhs(x, b01):
    hi, mid, lo = _split3(x)
    return _dot(hi, b01) + _dot(mid, b01) + _dot(lo, b01)


def _rms(x, g):
    return x * lax.rsqrt(jnp.mean(x * x, axis=-1, keepdims=True) + EPS) * g


def _norm_mod(x, g, shift, scale):
    return _rms(x, g) * (1.0 + scale) + shift


def _params(n_grid):
    return pltpu.CompilerParams(dimension_semantics=("arbitrary",) * n_grid, vmem_limit_bytes=VMEM_LIMIT)


def _const_spec(shape):
    zeros = (0,) * len(shape)
    return pl.BlockSpec(shape, lambda *_: zeros, pipeline_mode=pl.Buffered(1))


def _layer_spec(shape, layer):
    index = (layer,) + (0,) * len(shape)
    return pl.BlockSpec((None,) + shape, lambda *_: index, pipeline_mode=pl.Buffered(1))


def _mod_spec(layer, cond_of):
    return pl.BlockSpec((None, None, 6, D_MODEL), lambda *idx: (layer, cond_of(*idx), 0, 0))


def _mod_kernel(cond_ref, w_ref, b_ref, o_ref):
    s = jax.nn.silu(cond_ref[...]).astype(BF16)
    o_ref[...] = _dot(s, w_ref[...].astype(BF16)) + b_ref[...]


def _modulation(cond, mod_w, mod_b):
    depth = mod_w.shape[0]
    n_out = mod_w.shape[2]
    tn = 2 * D_MODEL
    out = pl.pallas_call(
        _mod_kernel,
        grid=(depth, n_out // tn),
        in_specs=[
            pl.BlockSpec((COND_ROWS, D_MODEL), lambda l, j: (0, 0)),
            pl.BlockSpec((None, D_MODEL, tn), lambda l, j: (l, 0, j)),
            pl.BlockSpec((None, 1, tn), lambda l, j: (l, 0, j)),
        ],
        out_specs=pl.BlockSpec((None, COND_ROWS, tn), lambda l, j: (l, 0, j)),
        out_shape=jax.ShapeDtypeStruct((depth, COND_ROWS, n_out), F32),
        compiler_params=_params(2),
        name="modulation",
    )(cond, mod_w, mod_b.reshape(depth, 1, n_out))
    return out.reshape(depth, COND_ROWS, 6, D_MODEL)


def _run_staggered(programs, stagger):
    programs = list(programs)
    live, rounds = [], 0
    while programs or live:
        if programs and rounds % stagger == 0:
            live.append(programs.pop(0))
        for g in list(live):
            try:
                next(g)
            except StopIteration:
                live.remove(g)
        rounds += 1


def _even_kernel(*refs, n_chunks, nsub):
    halos, (xc_ref, mod_ref, ng_ref, win_ref, caw_ref, cab_ref, lng_ref, lnb_ref, cbw_ref, wout_ref, o_ref,
            apad, cpad, bgs, zs) = refs[:2 * nsub], refs[2 * nsub:]
    shared = (mod_ref, ng_ref, win_ref, caw_ref, cab_ref, lng_ref, lnb_ref, cbw_ref, wout_ref)
    _run_staggered(
        (_even_chunk(pl.program_id(0) * nsub + k, halos[2 * k], xc_ref.at[k], halos[2 * k + 1], *shared,
                     o_ref.at[k], apad.at[k], cpad.at[k], bgs.at[k], zs.at[k], n_chunks=n_chunks)
         for k in range(nsub)), EVEN_STAGGER)


def _even_chunk(g, xp_ref, xc_ref, xn_ref, mod_ref, ng_ref, win_ref, caw_ref, cab_ref, lng_ref, lnb_ref,
                cbw_ref, wout_ref, o_ref, apad, cpad, bgs, zs, *, n_chunks):
    c = g % n_chunks if n_chunks > 1 else 0
    rows = ROW_CHUNK + 2 * CONF_HALO
    shift1, scale1, gate1 = mod_ref[0:1, :], mod_ref[1:2, :], mod_ref[2:3, :]
    own = slice(CONF_HALO, CONF_HALO + ROW_CHUNK)
    if n_chunks == 1:
        h = _norm_mod(xc_ref[...], ng_ref[0:1, :], shift1, scale1).astype(BF16)
        h_own = h
        for pad_ref in (apad, cpad):
            pad_ref[0:CONF_HALO, :] = jnp.zeros((CONF_HALO, D_HALF), F32)
            pad_ref[CONF_HALO + ROW_CHUNK:rows, :] = jnp.zeros((CONF_HALO, D_HALF), F32)
        keep = lambda v: v
        span = own
    else:
        xh = jnp.concatenate([xp_ref[...], xc_ref[...], xn_ref[...]], axis=0)
        h = _norm_mod(xh, ng_ref[0:1, :], shift1, scale1).astype(BF16)
        h_own = h[own]
        ri = lax.broadcasted_iota(jnp.int32, (rows, D_HALF), 0)
        lo = jnp.where(c == 0, CONF_HALO, 0)
        hi = jnp.where(c == n_chunks - 1, CONF_HALO + ROW_CHUNK, rows)
        inside = (ri >= lo) & (ri < hi)
        keep = lambda v: jnp.where(inside, v, 0.0)
        span = slice(0, rows)
    a = _dot(h, win_ref[:, 0:D_HALF]) * jax.nn.sigmoid(_dot(h, win_ref[:, D_HALF:2 * D_HALF]))
    apad[span, :] = keep(a)
    yield
    cx = _dot(h, win_ref[:, 3 * D_HALF:4 * D_HALF]) * _dot(h, win_ref[:, 4 * D_HALF:5 * D_HALF])
    cpad[span, :] = keep(cx)
    yield
    bgs[...] = _dot(h_own, win_ref[:, 2 * D_HALF:3 * D_HALF])
    yield

    sub = EVEN_CONV_ROWS
    tile = 8
    for j in range(ROW_CHUNK // sub):
        r0 = j * sub
        groups = []
        for cg in range(D_HALF // LANES):
            cols = slice(cg * LANES, (cg + 1) * LANES)
            acc = None
            for r in range(tile):
                part = None
                for m in range(-(-(CONF_WIDTH + 1) // tile)):
                    o = tile * m + r
                    if 1 <= o <= CONF_WIDTH:
                        term = caw_ref[o - 1:o, cols] * apad[r0 + tile * m:r0 + tile * m + sub + tile, cols]
                        part = term if part is None else part + term
                shifted = part[r:r + sub, :]
                acc = shifted if acc is None else acc + shifted
            groups.append(acc)
        acc = jnp.concatenate(groups, axis=1) + cab_ref[...]
        mu = jnp.mean(acc, axis=-1, keepdims=True)
        dlt = acc - mu
        var = jnp.mean(dlt * dlt, axis=-1, keepdims=True)
        a_out = jax.nn.silu(dlt * lax.rsqrt(var + EPS) * lng_ref[...] + lnb_ref[...])
        zs[r0:r0 + sub, 0:D_HALF] = a_out.astype(BF16)
        base = r0 + CONF_HALO - 1
        sc = (cbw_ref[0:1, :] * cpad[base:base + sub, :]
              + cbw_ref[1:2, :] * cpad[base + 1:base + 1 + sub, :]
              + cbw_ref[2:3, :] * cpad[base + 2:base + 2 + sub, :])
        zs[r0:r0 + sub, D_HALF:D_MODEL] = (bgs[r0:r0 + sub, :] * sc).astype(BF16)
        yield

    o = _dot(zs[...], wout_ref[...])
    o_ref[...] = xc_ref[...] + gate1 * _rms(o, ng_ref[1:2, :])
    yield


def _even_layer(x, modv, layer, cond_base, cond_stride, ng, win, caw, cab, lng, lnb, cbw, wout):
    j = layer // 2
    b, t, _ = x.shape
    n_chunks = t // ROW_CHUNK
    nsub = EVEN_CHUNKS
    assert (b * n_chunks) % nsub == 0 and (cond_stride == 0 or n_chunks % nsub == 0)
    hpc = ROW_CHUNK // CONF_HALO
    n_halo_blocks = b * t // CONF_HALO
    rows = ROW_CHUNK + 2 * CONF_HALO
    halo_specs = []
    for k in range(nsub):
        halo_specs += [
            pl.BlockSpec((None, CONF_HALO, D_MODEL),
                         lambda i, k=k: (jnp.maximum((i * nsub + k) * hpc - 1, 0), 0, 0)),
            pl.BlockSpec((None, CONF_HALO, D_MODEL),
                         lambda i, k=k: (jnp.minimum((i * nsub + k + 1) * hpc, n_halo_blocks - 1), 0, 0)),
        ]
    x_halo = x.reshape(n_halo_blocks, CONF_HALO, D_MODEL)
    x_chunks = x.reshape(b * n_chunks, ROW_CHUNK, D_MODEL)
    kern = functools.partial(_even_kernel, n_chunks=n_chunks, nsub=nsub)
    out = pl.pallas_call(
        kern,
        grid=(b * n_chunks // nsub,),
        in_specs=halo_specs + [
            pl.BlockSpec((nsub, ROW_CHUNK, D_MODEL), lambda i: (i, 0, 0)),
            _mod_spec(layer, lambda i: cond_base + cond_stride * ((i * nsub) // n_chunks)),
            _layer_spec((4, D_MODEL), layer),
            _layer_spec((D_MODEL, 5 * D_HALF), j),
            _layer_spec((CONF_WIDTH, D_HALF), j),
            _layer_spec((1, D_HALF), j),
            _layer_spec((1, D_HALF), j),
            _layer_spec((1, D_HALF), j),
            _layer_spec((3, D_HALF), j),
            _layer_spec((D_MODEL, D_MODEL), j),
        ],
        out_specs=pl.BlockSpec((nsub, ROW_CHUNK, D_MODEL), lambda i: (i, 0, 0)),
        out_shape=jax.ShapeDtypeStruct(x_chunks.shape, F32),
        scratch_shapes=[
            pltpu.VMEM((nsub, rows, D_HALF), F32),
            pltpu.VMEM((nsub, rows, D_HALF), F32),
            pltpu.VMEM((nsub, ROW_CHUNK, D_HALF), F32),
            pltpu.VMEM((nsub, ROW_CHUNK, D_MODEL), BF16),
        ],
        compiler_params=_params(1),
        name="even_mixer",
    )(*([x_halo] * (2 * nsub)), x_chunks, modv, ng, win, caw, cab, lng, lnb, cbw, wout)
    return out.reshape(b, t, D_MODEL)


def _mlp_kernel(xp_ref, xs_ref, mod_ref, ng_ref, w1_hbm, w2_hbm, op_ref, os_ref, w1_s, w2_s, stage, sem, *,
                layer, n_ctx_tiles):
    n_blocks = D_FF // D_MODEL

    def block_copy(k):
        c, slot = k // 2, k % 2
        span = pl.ds(c * D_MODEL, D_MODEL)
        src = w1_hbm.at[layer, :, span] if k % 2 == 0 else w2_hbm.at[layer, span, :]
        return pltpu.make_async_copy(src, stage.at[slot], sem.at[slot])

    def fetch(k, dst):
        block_copy(k).wait()
        dst[...] = stage[k % 2].astype(BF16)
        if k + 2 < 2 * n_blocks:
            block_copy(k + 2).start()

    def rows_program(x_ref, o_ref, rows, load):
        x = x_ref[rows, :]
        h = _norm_mod(x, ng_ref[2:3, :], mod_ref[3:4, :], mod_ref[4:5, :]).astype(BF16)
        yield
        acc = None
        for c in range(n_blocks):
            cols = slice(c * D_MODEL, (c + 1) * D_MODEL)
            if load:
                fetch(2 * c, w1_s.at[:, cols])
            hid = jnp.square(jnp.maximum(_dot(h, w1_s[:, cols]), 0.0)).astype(BF16)
            yield
            if load:
                fetch(2 * c + 1, w2_s.at[cols, :])
            part = _dot(hid, w2_s[cols, :])
            acc = part if acc is None else acc + part
            yield
        o_ref[rows, :] = x + mod_ref[5:6, :] * _rms(acc, ng_ref[3:4, :])
        yield

    def tile(x_ref, o_ref, load=False):
        n_rows = x_ref.shape[0]
        if load:
            _drain(rows_program(x_ref, o_ref, slice(0, n_rows), True))
        else:
            _run_staggered((rows_program(x_ref, o_ref, slice(r0, r0 + n_rows // 2), False)
                            for r0 in (0, n_rows // 2)), MLP_STAGGER)

    i = pl.program_id(0)

    @pl.when(i == 0)
    def _():
        block_copy(0).start()
        block_copy(1).start()
        tile(xp_ref, op_ref, load=True)

    @pl.when((i > 0) & (i < n_ctx_tiles))
    def _():
        tile(xp_ref, op_ref)

    @pl.when(i >= n_ctx_tiles)
    def _():
        tile(xs_ref, os_ref)


def _mlp_layer(xp, xs, modv, layer, ng, w1, w2):
    tm = MLP_ROWS
    xp2, xs2 = xp.reshape(-1, D_MODEL), xs.reshape(-1, D_MODEL)
    t_dec = xs.shape[1]
    assert xp2.shape[0] % tm == 0 and t_dec % tm == 0
    n_p, n_s, per_seq = xp2.shape[0] // tm, xs2.shape[0] // tm, t_dec // tm
    ctx_tile = lambda i: (jnp.minimum(i, n_p - 1), 0)
    lat_tile = lambda i: (jnp.maximum(i - n_p, 0), 0)
    yp, ys = pl.pallas_call(
        functools.partial(_mlp_kernel, layer=layer, n_ctx_tiles=n_p),
        grid=(n_p + n_s,),
        in_specs=[
            pl.BlockSpec((tm, D_MODEL), ctx_tile),
            pl.BlockSpec((tm, D_MODEL), lat_tile),
            _mod_spec(layer, lambda i: jnp.where(i < n_p, 0, 1 + jnp.maximum(i - n_p, 0) // per_seq)),
            _layer_spec((4, D_MODEL), layer),
            pl.BlockSpec(memory_space=pl.ANY),
            pl.BlockSpec(memory_space=pl.ANY),
        ],
        out_specs=[pl.BlockSpec((tm, D_MODEL), ctx_tile), pl.BlockSpec((tm, D_MODEL), lat_tile)],
        out_shape=[jax.ShapeDtypeStruct(xp2.shape, F32), jax.ShapeDtypeStruct(xs2.shape, F32)],
        scratch_shapes=[
            pltpu.VMEM((D_MODEL, D_FF), BF16),
            pltpu.VMEM((D_FF, D_MODEL), BF16),
            pltpu.VMEM((2, D_MODEL, D_MODEL), F32),
            pltpu.SemaphoreType.DMA((2,)),
        ],
        compiler_params=_params(1),
        name="mlp",
    )(xp2, xs2, modv, ng, w1, w2)
    return yp.reshape(xp.shape), ys.reshape(xs.shape)


def _log_sigmoid(x):
    return jnp.minimum(x, 0.0) - jnp.log(1.0 + jnp.exp(-jnp.abs(x)))


def _rot_half(x, first_half):
    return jnp.where(first_half, pltpu.roll(x, 96, axis=1), pltpu.roll(x, 32, axis=1))


def _values_and_ones(v):
    vt = jnp.transpose(v)
    row = lax.broadcasted_iota(jnp.int32, vt.shape, 0)
    return jnp.where(row < HEAD_DIM, vt, 1.0)


def _aligned(i, m):
    return i * m if isinstance(i, int) else pl.multiple_of(i * m, m)


def _drain(pieces):
    for _ in pieces:
        pass


def _each(body, n, static, unroll=1):
    if static:
        for i in range(n):
            yield from body(i)
    else:
        lax.fori_loop(0, n, lambda i, c: (_drain(body(i)), c)[1], 0, unroll=unroll)


def _alternate(*programs):
    live = list(programs)
    while live:
        for g in list(live):
            try:
                next(g)
            except StopIteration:
                live.remove(g)
            else:
                yield


def _odd_kernel(*refs, t, lc, latent, nseq):
    shared = set(range(1, 11 if latent else 9))
    n_rows = min(t, ODD_ROWS)
    joint = not latent and nseq > 1 and t == n_rows
    finals = []
    programs = [_odd_seq(*[r if k in shared else r.at[sq] for k, r in enumerate(refs)],
                         t=t, lc=lc, latent=latent, joint=joint, finals=finals) for sq in range(nseq)]
    if joint:
        parts = [next(g) for g in programs]
        h_all = jnp.concatenate([proj_h(0) for proj_h, _ in parts], axis=0)
        for idx, (weight, _) in enumerate(parts[0][1]):
            y = _dot(h_all, weight())
            for sq, (_, plan) in enumerate(parts):
                plan[idx][1](y[sq * n_rows:(sq + 1) * n_rows], 0)
    _run_staggered(programs, ODD_STAGGER)
    if joint:
        o = _dot(jnp.concatenate([z() for z, _ in finals], axis=0), refs[8][...])
        for sq, (_, residual) in enumerate(finals):
            residual(o[sq * n_rows:(sq + 1) * n_rows])


def _odd_seq(*refs, t, lc, latent, joint, finals):
    if latent:
        (x_ref, mod_ref, ng_ref, win_ref, wg_ref, gb_ref, sink_ref, hn_ref, wout_ref,
         cos_ref, sin_ref, kc_ref, vc_ref, cin_ref, nin_ref, min_ref,
         o_ref,
         qa_s, qb_s, kk_s, vvt_s, qma_s, qmb_s, qm_s, km_s, vmt_s, vmtf_s, om_s, g_s, z_s, hft_s, hbt_s,
         sr_s, st_s, qc_s, c_s, n_s, m_s, s_s, p_s, kc_s, vct_s) = refs
    else:
        (x_ref, mod_ref, ng_ref, win_ref, wg_ref, gb_ref, sink_ref, hn_ref, wout_ref,
         o_ref, ko_ref, vo_ref, cf_ref, cb_ref, no_ref, mo_ref,
         qa_s, qb_s, kk_s, vvt_s, qma_s, qmb_s, qm_s, km_s, vmt_s, vmtf_s, om_s, g_s, z_s, hft_s, hbt_s,
         sr_s, st_s, qc_s, c_s, n_s, m_s, s_s, p_s) = refs

    static = not latent
    n_blocks = t // CHUNK
    pad = CHUNK if latent else 0
    shift1, scale1, gate1 = mod_ref[0:1, :], mod_ref[1:2, :], mod_ref[2:3, :]
    lane = lax.broadcasted_iota(jnp.int32, (1, LANES), 1)
    left = lane < HEAD_DIM
    first_half = (lane % HEAD_DIM) < (HEAD_DIM // 2)
    ti = lax.broadcasted_iota(jnp.int32, (CHUNK, CHUNK), 0)
    si = lax.broadcasted_iota(jnp.int32, (CHUNK, CHUNK), 1)
    top = ti < HEAD_DIM
    same_head = top == (si < HEAD_DIM)

    n_rows = min(t, ODD_ROWS)
    if latent:
        for kv in range(N_KV):
            kk_s[kv, 0:CHUNK, :] = jnp.zeros((CHUNK, LANES), BF16)
            kk_s[kv, CHUNK + t:2 * CHUNK + t, :] = jnp.zeros((CHUNK, LANES), BF16)
            vvt_s[kv, :, 0:CHUNK] = jnp.zeros((LANES, CHUNK), BF16)
            vvt_s[kv, :, CHUNK + t:2 * CHUNK + t] = jnp.zeros((LANES, CHUNK), BF16)
            kc_s[kv] = kc_ref[kv].astype(BF16)
            vct_s[kv] = _values_and_ones(vc_ref[kv]).astype(BF16)

    base = D_HALF + 2 * LANES

    def proj_rows(rc):
        r0 = _aligned(rc, n_rows)
        return pl.ds(r0, n_rows), pl.ds(r0 + pad, n_rows)

    def proj_h(rc):
        rows, _ = proj_rows(rc)
        return _norm_mod(x_ref[rows, :], ng_ref[0:1, :], shift1, scale1).astype(BF16)

    def rope(v, rows):
        return v * cos_ref[rows, :] + _rot_half(v, first_half) * sin_ref[rows, :]

    def use_q(q, rc):
        rows, _ = proj_rows(rc)
        for p in range(N_PAIRS):
            cols = slice(p * LANES, (p + 1) * LANES)
            qp = q[:, cols]
            if latent:
                qp = rope(qp, rows)
            qp = qp * ATT_SCALE
            qa_s[rows, cols] = jnp.where(left, qp, 0.0).astype(BF16)
            qb_s[rows, cols] = jnp.where(left, 0.0, qp).astype(BF16)

    def use_kv(kv2, rc):
        rows, krows = proj_rows(rc)
        ka, va = kv2[:, 0:LANES], kv2[:, LANES:2 * LANES]
        if latent:
            ka = rope(ka, rows)
        kr = pltpu.roll(ka, HEAD_DIM, axis=1)
        vr = pltpu.roll(va, HEAD_DIM, axis=1)
        if not latent:
            ka_t, va_t = jnp.transpose(ka), jnp.transpose(va)
            for kv in range(N_KV):
                ko_ref[0, kv, :, rows] = ka_t[kv * HEAD_DIM:(kv + 1) * HEAD_DIM, :]
                vo_ref[0, kv, :, rows] = va_t[kv * HEAD_DIM:(kv + 1) * HEAD_DIM, :]
        kk_s[0, krows, :] = jnp.where(left, ka, kr).astype(BF16)
        kk_s[1, krows, :] = jnp.where(left, kr, ka).astype(BF16)
        vvt_s[0, :, krows] = _values_and_ones(va).astype(BF16)
        vvt_s[1, :, krows] = _values_and_ones(vr).astype(BF16)

    def use_qm(qm, rc):
        rows, _ = proj_rows(rc)
        qm_s[rows, :] = qm.astype(BF16)
        for p in range(N_PAIRS):
            cols = slice(p * LANES, (p + 1) * LANES)
            qma_s[rows, cols] = jnp.where(left, qm[:, cols], 0.0).astype(BF16)
            qmb_s[rows, cols] = jnp.where(left, 0.0, qm[:, cols]).astype(BF16)

    def use_vm(vm, rc):
        rows, _ = proj_rows(rc)
        for p in range(N_PAIRS):
            cols = slice(p * LANES, (p + 1) * LANES)
            vt = jnp.transpose(vm[:, cols])
            vmtf_s[cols, rows] = vt
            vmt_s[cols, rows] = vt.astype(BF16)

    def use_km(km, rc):
        km_s[proj_rows(rc)[0], :] = (km * (HEAD_DIM ** -0.5)).astype(BF16)

    def use_om(om, rc):
        om_s[proj_rows(rc)[0], :] = om

    def use_g(g, rc):
        g_s[proj_rows(rc)[0], :] = g + gb_ref[...]

    def columns(start, width):
        return lambda: win_ref[:, start:start + width]

    proj_plan = [(columns(0, D_HALF), use_q), (columns(D_HALF, 2 * LANES), use_kv),
                 (columns(base, D_HALF), use_qm), (columns(base + 2 * D_HALF, D_HALF), use_vm),
                 (columns(base + D_HALF, D_HALF), use_km), (columns(base + 3 * D_HALF, D_HALF), use_om),
                 (lambda: wg_ref[...], use_g)]

    def project(rc):
        h = proj_h(rc)
        for weight, use in proj_plan:
            use(_dot(h, weight()), rc)
            yield

    if joint:
        yield proj_h, proj_plan
    else:
        yield from _each(project, t // n_rows, static)

    kj = lax.broadcasted_iota(jnp.int32, (3 * CHUNK, 2 * CHUNK), 0)
    qi = lax.broadcasted_iota(jnp.int32, (3 * CHUNK, 2 * CHUNK), 1) % CHUNK
    band_ok = jnp.abs(kj - CHUNK - qi) <= WINDOW
    head_a = lax.broadcasted_iota(jnp.int32, (1, 2 * CHUNK), 1) < CHUNK

    n_keys = _att_keys(lc, latent)
    group = ATT_GROUP

    def attention():
        def attend(i):
            r0 = _aligned(i, CHUNK)
            rows = pl.ds(r0, CHUNK)
            if latent:
                key_pos = kj + (i - 1) * CHUNK
                mask = band_ok & (key_pos >= 0) & (key_pos < t)
                win = pl.ds(r0, 3 * CHUNK)
            for g0 in range(0, N_PAIRS, group):
                for gi in range(group):
                    p = g0 + gi
                    kv = p // (N_PAIRS // N_KV)
                    cols = slice(p * LANES, (p + 1) * LANES)
                    q2 = jnp.concatenate([qa_s[rows, cols], qb_s[rows, cols]], axis=0)
                    if latent:
                        s_s[gi, 0:lc, :] = _dot_nt(kc_s[kv], q2)
                        s_s[gi, lc:n_keys, :] = jnp.where(mask, _dot_nt(kk_s[kv, win, :], q2), -jnp.inf)
                    else:
                        s_s[gi] = _dot_nt(kk_s[kv], q2)
                yield
                maxes = []
                for gi in range(group):
                    p = g0 + gi
                    sink = jnp.where(head_a, sink_ref[2 * p:2 * p + 1, 0:1], sink_ref[2 * p + 1:2 * p + 2, 0:1])
                    mx = jnp.maximum(jnp.max(s_s[gi], axis=0, keepdims=True), sink)
                    p_s[gi] = jnp.exp(s_s[gi] - mx).astype(BF16)
                    maxes.append((sink, mx))
                yield
                for gi in range(group):
                    p = g0 + gi
                    kv = p // (N_PAIRS // N_KV)
                    cols = slice(p * LANES, (p + 1) * LANES)
                    if latent:
                        num = (_dot(vct_s[kv], p_s[gi, 0:lc, :]) + _dot(vvt_s[kv, :, win], p_s[gi, lc:n_keys, :]))
                    else:
                        num = _dot(vvt_s[kv], p_s[gi])
                    sink, mx = maxes[gi]
                    den = num[HEAD_DIM:HEAD_DIM + 1, :] + jnp.exp(sink - mx)
                    out = num[0:HEAD_DIM, :] * (1.0 / den)
                    pair = jnp.concatenate([out[:, 0:CHUNK], out[:, CHUNK:2 * CHUNK]], axis=0)
                    z_s[rows, cols] = jnp.transpose(pair).astype(BF16)
                yield

        return attend

    if latent:
        c_s[...] = cin_ref[...]
        n_s[...] = nin_ref[...]
        m_s[...] = min_ref[...]
    else:
        c_s[...] = jnp.zeros(c_s.shape, F32)
        n_s[...] = jnp.zeros(n_s.shape, F32)
        m_s[...] = jnp.zeros(m_s.shape, F32)

    see = (ti <= si, ti >= si)
    tri = tuple(m.astype(F32).astype(BF16) for m in see)
    last = (CHUNK - 1, 0)

    def mlstm(i):
        offs = (_aligned(i, CHUNK), _aligned(n_blocks - 1 - i, CHUNK))
        half = (slice(0, LANES), slice(LANES, 2 * LANES))
        a_rows, b_rows = [], []
        for d in range(2):
            gt = jnp.transpose(g_s[pl.ds(offs[d], CHUNK), :])
            lf = _log_sigmoid(gt[2 * N_HEADS:4 * N_HEADS, :])
            bcum = _dot_exact_lhs(lf, tri[d])[8 * d:8 * d + 8, :]
            a_rows.append(gt[8 * d:8 * d + 8, :] - bcum)
            b_rows.append(bcum)
        yield
        q_ns = []
        for d in range(2):
            rows = pl.ds(offs[d], CHUNK)
            for p in range(N_PAIRS):
                u = d * N_PAIRS + p
                cols = slice(p * LANES, (p + 1) * LANES)
                q2 = jnp.concatenate([qma_s[rows, cols], qmb_s[rows, cols]], axis=0)
                sr_s[u] = _dot_nt(km_s[rows, cols], q2)
                qc_s[u] = _dot_nt(c_s[u].astype(BF16), qm_s[rows, cols])
                n8 = jnp.broadcast_to(n_s[u:u + 1, :], (8, LANES)).astype(BF16)
                q_ns.append(_dot_nt(n8, q2)[0:1, :])
            yield
        stats = {}
        for d in range(2):
            for hd in range(N_HEADS):
                u, j = d * N_PAIRS + hd // 2, hd % 2
                mrow = d * N_HEADS + hd
                a_row = a_rows[d][hd:hd + 1, :]
                b_row = b_rows[d][hd:hd + 1, :]
                m_prev = m_s[mrow:mrow + 1, 0:1]
                a_col = jnp.transpose(jnp.broadcast_to(a_row, (CHUNK, CHUNK)))
                z_t = jnp.where(see[d], a_col, -jnp.inf)
                m_run = jnp.maximum(jnp.max(z_t, axis=0, keepdims=True), m_prev)
                s_t = sr_s[u, :, half[j]] * jnp.exp(z_t - m_run)
                st_s[u, :, half[j]] = s_t.astype(BF16)
                w_int = jnp.exp(m_prev - m_run)
                den = jnp.sum(s_t, axis=0, keepdims=True) + w_int * q_ns[u][:, half[j]]
                inv = 1.0 / jnp.maximum(jnp.abs(den), jnp.exp(-(b_row + m_run)))
                m_last = m_run[:, last[d]:last[d] + 1]
                stats[(u, j)] = (w_int, inv, jnp.exp(a_row - m_last), jnp.exp(m_prev - m_last))
                m_s[mrow:mrow + 1, :] = jnp.broadcast_to(b_row[:, last[d]:last[d] + 1] + m_last, (1, LANES))
            yield
        for d in range(2):
            rows = pl.ds(offs[d], CHUNK)
            ht_s = (hft_s, hbt_s)[d]
            for p in range(N_PAIRS):
                u = d * N_PAIRS + p
                cols = slice(p * LANES, (p + 1) * LANES)
                (w_a, inv_a, e_a, dec_a), (w_b, inv_b, e_b, dec_b) = stats[(u, 0)], stats[(u, 1)]
                kp = km_s[rows, cols]
                num2 = _dot(vmt_s[cols, rows], st_s[u])
                num = jnp.where(top, num2[:, half[0]], num2[:, half[1]])
                ht_s[cols, rows] = (num + jnp.where(top, w_a, w_b) * qc_s[u]) * jnp.where(top, inv_a, inv_b)
                vt_e = (vmtf_s[cols, rows] * jnp.where(top, e_a, e_b)).astype(BF16)
                c_s[u] = jnp.where(top, dec_a, dec_b) * c_s[u] + jnp.where(same_head, _dot(vt_e, kp), 0.0)
                e2 = jnp.concatenate([e_a, e_b, jnp.zeros((6, CHUNK), F32)], axis=0).astype(BF16)
                n_k = _dot(e2, kp)
                n_s[u:u + 1, :] = (jnp.where(left, dec_a, dec_b) * n_s[u:u + 1, :]
                                   + jnp.where(left, n_k[0:1, :], n_k[1:2, :]))
            yield

    attend = attention()
    yield from _each(lambda i: _alternate(attend(i), mlstm(i)), n_blocks, static, unroll=2)

    if not latent:
        for d, c_ref in enumerate((cf_ref, cb_ref)):
            for p in range(N_PAIRS):
                c_pair = jnp.transpose(c_s[d * N_PAIRS + p])
                c_ref[0, 2 * p] = c_pair[0:HEAD_DIM, 0:HEAD_DIM]
                c_ref[0, 2 * p + 1] = c_pair[HEAD_DIM:LANES, HEAD_DIM:LANES]
        no_ref[...] = n_s[...]
        mo_ref[...] = m_s[...]
        yield

    top_w = lax.broadcasted_iota(jnp.int32, (LANES, n_rows), 0) < HEAD_DIM

    def finish(rc):
        r0 = _aligned(rc, n_rows)
        rows = pl.ds(r0, n_rows)
        for p in range(N_PAIRS):
            cols = slice(p * LANES, (p + 1) * LANES)
            hm = hft_s[cols, rows] + hbt_s[cols, rows]
            sq = hm * hm
            ms_a = jnp.sum(sq[0:HEAD_DIM], axis=0, keepdims=True)
            ms_b = jnp.sum(sq[HEAD_DIM:LANES], axis=0, keepdims=True)
            ms = jnp.where(top_w, ms_a, ms_b) * (1.0 / HEAD_DIM)
            y = jnp.transpose(hm * lax.rsqrt(ms + EPS)) * hn_ref[:, cols] * jax.nn.sigmoid(om_s[rows, cols])
            z_s[rows, D_HALF + p * LANES:D_HALF + (p + 1) * LANES] = y.astype(BF16)
            if p % 2:
                yield
        def residual(o):
            o_ref[rows, :] = x_ref[rows, :] + gate1 * _rms(o, ng_ref[1:2, :])

        if joint:
            finals.append((lambda: z_s[rows, :], residual))
        else:
            residual(_dot(z_s[rows, :], wout_ref[...]))
        yield

    yield from _each(finish, t // n_rows, static)


def _att_keys(lc, latent):
    return lc + 3 * CHUNK if latent else lc


def _odd_scratch(t, lc, latent, nseq):
    pad = 2 * CHUNK if latent else 0
    att = (ATT_GROUP, _att_keys(lc, latent), 2 * CHUNK)
    shapes = [
        pltpu.VMEM((t, D_HALF), BF16),
        pltpu.VMEM((t, D_HALF), BF16),
        pltpu.VMEM((N_KV, t + pad, LANES), BF16),
        pltpu.VMEM((N_KV, LANES, t + pad), BF16),
        pltpu.VMEM((t, D_HALF), BF16),
        pltpu.VMEM((t, D_HALF), BF16),
        pltpu.VMEM((t, D_HALF), BF16),
        pltpu.VMEM((t, D_HALF), BF16),
        pltpu.VMEM((D_HALF, t), BF16),
        pltpu.VMEM((D_HALF, t), F32),
        pltpu.VMEM((t, D_HALF), F32),
        pltpu.VMEM((t, LANES), F32),
        pltpu.VMEM((t, D_MODEL), BF16),
        pltpu.VMEM((D_HALF, t), F32),
        pltpu.VMEM((D_HALF, t), F32),
        pltpu.VMEM((2 * N_PAIRS, CHUNK, 2 * LANES), F32),
        pltpu.VMEM((2 * N_PAIRS, CHUNK, 2 * LANES), BF16),
        pltpu.VMEM((2 * N_PAIRS, LANES, CHUNK), F32),
        pltpu.VMEM((2 * N_PAIRS, LANES, LANES), F32),
        pltpu.VMEM((2 * N_PAIRS, LANES), F32),
        pltpu.VMEM((2 * N_HEADS, LANES), F32),
        pltpu.VMEM(att, F32),
        pltpu.VMEM(att, BF16),
    ]
    if latent:
        shapes += [pltpu.VMEM((N_KV, lc, LANES), BF16), pltpu.VMEM((N_KV, LANES, lc), BF16)]
    return [pltpu.VMEM((nseq,) + tuple(sh.shape), sh.dtype) for sh in shapes]


def _odd_common_specs(t, layer, cond_base, cond_stride, nseq):
    assert cond_stride == 0 or nseq == 1
    j = layer // 2
    return [
        _per_seq((t, D_MODEL), nseq),
        _mod_spec(layer, lambda i: cond_base + cond_stride * i),
        _layer_spec((4, D_MODEL), layer),
        _layer_spec((D_MODEL, D_IN_ODD), j),
        _const_spec((D_MODEL, LANES)),
        _const_spec((1, LANES)),
        _const_spec((N_HEADS, LANES)),
        _layer_spec((1, D_HALF), j),
        _layer_spec((D_MODEL, D_MODEL), j),
    ]


def _per_seq(shape, nseq):
    return pl.BlockSpec((nseq,) + shape, lambda i: (i,) + (0,) * len(shape))


def _odd_context(x, modv, layer, ng, w_main, w_gate, gate_bias, sink_b, hnorm, wout):
    b, t, _ = x.shape
    nseq = ODD_CTX_SEQS
    assert b % nseq == 0
    kern = functools.partial(_odd_kernel, t=t, lc=t, latent=False, nseq=nseq)
    per_seq = functools.partial(_per_seq, nseq=nseq)
    return pl.pallas_call(
        kern,
        grid=(b // nseq,),
        in_specs=_odd_common_specs(t, layer, 0, 0, nseq),
        out_specs=[per_seq((t, D_MODEL)),
                   per_seq((1, N_KV, HEAD_DIM, t)), per_seq((1, N_KV, HEAD_DIM, t)),
                   per_seq((1, N_HEADS, HEAD_DIM, HEAD_DIM)), per_seq((1, N_HEADS, HEAD_DIM, HEAD_DIM)),
                   per_seq((2 * N_PAIRS, LANES)), per_seq((2 * N_HEADS, LANES))],
        out_shape=[jax.ShapeDtypeStruct((b, t, D_MODEL), F32),
                   jax.ShapeDtypeStruct((b, 1, N_KV, HEAD_DIM, t), F32),
                   jax.ShapeDtypeStruct((b, 1, N_KV, HEAD_DIM, t), F32),
                   jax.ShapeDtypeStruct((b, 1, N_HEADS, HEAD_DIM, HEAD_DIM), F32),
                   jax.ShapeDtypeStruct((b, 1, N_HEADS, HEAD_DIM, HEAD_DIM), F32),
                   jax.ShapeDtypeStruct((b, 2 * N_PAIRS, LANES), F32),
                   jax.ShapeDtypeStruct((b, 2 * N_HEADS, LANES), F32)],
        scratch_shapes=_odd_scratch(t, t, False, nseq),
        compiler_params=_params(1),
        name="odd_mixer_context",
    )(x, modv, ng, w_main, w_gate, gate_bias, sink_b, hnorm, wout)


def _odd_latent(x, modv, layer, ng, w_main, w_gate, gate_bias, sink_b, hnorm, wout, cos_t, sin_t, kc, vc,
                c_in, n_in, m_in):
    b, t, _ = x.shape
    lc = kc.shape[2]
    kern = functools.partial(_odd_kernel, t=t, lc=lc, latent=True, nseq=1)
    per_seq = functools.partial(_per_seq, nseq=1)
    return pl.pallas_call(
        kern,
        grid=(b,),
        in_specs=_odd_common_specs(t, layer, 1, 1, 1) + [
            _const_spec((t, LANES)), _const_spec((t, LANES)),
            per_seq((N_KV, lc, LANES)), per_seq((N_KV, lc, LANES)),
            per_seq((2 * N_PAIRS, LANES, LANES)), per_seq((2 * N_PAIRS, LANES)), per_seq((2 * N_HEADS, LANES)),
        ],
        out_specs=per_seq((t, D_MODEL)),
        out_shape=jax.ShapeDtypeStruct((b, t, D_MODEL), F32),
        scratch_shapes=_odd_scratch(t, lc, True, 1),
        compiler_params=_params(1),
        name="odd_mixer_latent",
    )(x, modv, ng, w_main, w_gate, gate_bias, sink_b, hnorm, wout, cos_t, sin_t, kc, vc, c_in, n_in, m_in)


def _rope_tables(t):
    rows = t // GRID_W
    row = jnp.broadcast_to(jnp.arange(rows)[:, None], (rows, GRID_W)).reshape(t).astype(F32)
    col = jnp.broadcast_to(jnp.arange(GRID_W)[None, :], (rows, GRID_W)).reshape(t).astype(F32)
    n_freq = HEAD_DIM // 4
    inv_freq = ROPE_BASE ** (-jnp.arange(n_freq, dtype=F32) / n_freq)
    ang = jnp.concatenate([row[:, None] * inv_freq, col[:, None] * inv_freq], axis=-1)
    cos, sin = jnp.cos(ang), jnp.sin(ang)
    cos_l = jnp.tile(cos, (1, LANES // cos.shape[1]))
    sin_l = jnp.tile(jnp.concatenate([-sin, sin], axis=-1), (1, LANES // HEAD_DIM))
    return cos_l, sin_l


def _pair_blockdiag(c):
    b = c.shape[0]
    c = c.reshape(b, N_PAIRS, 2, HEAD_DIM, HEAD_DIM)
    z = jnp.zeros_like(c[:, :, 0])
    top = jnp.concatenate([c[:, :, 0], z], axis=-1)
    bot = jnp.concatenate([z, c[:, :, 1]], axis=-1)
    return jnp.concatenate([top, bot], axis=-2)


def _lane_bcast(v):
    return jnp.broadcast_to(v[..., None], v.shape + (LANES,))


def kernel(x_prompt, x_sample, c, cache_k, cache_v, state_c_fwd, state_n_fwd, state_m_fwd, state_c_bwd, state_n_bwd, state_m_bwd, c_ctx, mod_w, mod_b, norm_g, mlp_w1, mlp_w2, even_in_w, conv_a_w, conv_a_b, ln_a_g, ln_a_b, conv_b_w, even_out_w, odd_in_w, attn_sink, gate_b, hnorm_g, odd_out_w):
    n_dec = x_sample.shape[0]
    n_ctx = x_prompt.shape[0]
    cond = jnp.concatenate([c_ctx[None, :], c, jnp.zeros((COND_ROWS - 1 - n_dec, D_MODEL), F32)], axis=0)
    modv = _modulation(cond, mod_w, mod_b)

    yp, ys = x_prompt, x_sample
    w1, w2 = mlp_w1, mlp_w2

    ev = (norm_g, even_in_w.astype(BF16), conv_a_w, conv_a_b[:, None, :], ln_a_g[:, None, :], ln_a_b[:, None, :],
          conv_b_w, even_out_w.astype(BF16))
    yp = _even_layer(yp, modv, 0, 0, 0, *ev)
    ys = _even_layer(ys, modv, 0, 1, 1, *ev)
    yp, ys = _mlp_layer(yp, ys, modv, 0, norm_g, w1, w2)

    order = jnp.array([0, 2, 1, 3])
    d_main = D_IN_ODD - 4 * N_HEADS
    wg = odd_in_w[0][:, d_main:].reshape(D_MODEL, 4, N_HEADS)[:, order, :].reshape(D_MODEL, 4 * N_HEADS)
    w_gate = jnp.pad(wg, ((0, 0), (0, LANES - 4 * N_HEADS))).astype(BF16)
    gate_bias = jnp.pad(gate_b[0][order, :].reshape(1, 4 * N_HEADS), ((0, 0), (0, LANES - 4 * N_HEADS)))
    sink_b = _lane_bcast(attn_sink[0])
    odd = (1, norm_g, odd_in_w.astype(BF16), w_gate, gate_bias, sink_b, hnorm_g[:, None, :], odd_out_w.astype(BF16))

    op, k_t, v_t, c_f, c_b, n_new, m_new = _odd_context(yp, modv, *odd)
    new_k, new_v = jnp.swapaxes(k_t, -1, -2), jnp.swapaxes(v_t, -1, -2)

    t_dec = x_sample.shape[1]
    cos_t, sin_t = _rope_tables(t_dec)
    kc = jnp.concatenate([cache_k[:, 0], cache_k[:, 0]], axis=-1)
    vc = jnp.concatenate([cache_v[:, 0], cache_v[:, 0]], axis=-1)
    c_in = jnp.concatenate([_pair_blockdiag(jnp.swapaxes(state_c_fwd[:, 0], -1, -2)),
                            _pair_blockdiag(jnp.swapaxes(state_c_bwd[:, 0], -1, -2))], axis=1)
    n_in = jnp.concatenate([state_n_fwd[:, 0].reshape(n_dec, N_PAIRS, LANES),
                            state_n_bwd[:, 0].reshape(n_dec, N_PAIRS, LANES)], axis=1)
    m_in = _lane_bcast(jnp.concatenate([state_m_fwd[:, 0], state_m_bwd[:, 0]], axis=1))
    os_ = _odd_latent(ys, modv, *odd, cos_t, sin_t, kc, vc, c_in, n_in, m_in)

    yp, ys = _mlp_layer(op, os_, modv, 1, norm_g, w1, w2)

    n_f = n_new[:, :N_PAIRS].reshape(n_ctx, N_HEADS, HEAD_DIM)[:, None]
    n_b = n_new[:, N_PAIRS:].reshape(n_ctx, N_HEADS, HEAD_DIM)[:, None]
    m_f = m_new[:, :N_HEADS, 0][:, None]
    m_b = m_new[:, N_HEADS:, 0][:, None]
    return (yp, ys, new_k, new_v, c_f, n_f, m_f, c_b, n_b, m_b)
```

```python
import functools

import jax
import jax.numpy as jnp
from jax import lax
from jax.experimental import pallas as pl
from jax.experimental.pallas import tpu as pltpu

F32 = jnp.float32
BF16 = jnp.bfloat16

D_MODEL = 1024
D_FF = 4 * D_MODEL
EPS = 1e-6
D_HALF = D_MODEL // 2
CONF_WIDTH = 31
CONF_HALO = 16
HEAD_DIM = 64
N_HEADS = 8
N_PAIRS = N_HEADS // 2
N_KV = 2
LANES = 128
CHUNK = 128
WINDOW = 128
GRID_W = 64
ROPE_BASE = 10000.0
ATT_SCALE = HEAD_DIM ** -0.5
D_IN_ODD = D_HALF + 2 * N_KV * HEAD_DIM + 4 * D_HALF + 4 * N_HEADS
ROW_CHUNK = 256
MLP_ROWS = 512
MLP_STAGGER = 3
ODD_ROWS = 512
ODD_CTX_SEQS = 2
ODD_STAGGER = 6
EVEN_CONV_ROWS = 64
EVEN_CHUNKS = 2
EVEN_STAGGER = 3
COND_ROWS = 8
VMEM_LIMIT = 56 * 1024 * 1024


def _dot(a, b):
    return jnp.dot(a, b, preferred_element_type=F32)


def _dot_nt(a, b):
    return lax.dot_general(a, b, (((1,), (1,)), ((), ())), preferred_element_type=F32)


def _split3(x):
    hi = x.astype(BF16)
    r1 = x - hi.astype(F32)
    mid = r1.astype(BF16)
    lo = (r1 - mid.astype(F32)).astype(BF16)
    return hi, mid, lo


def _dot_exact_lhs(x, b01):
    hi, mid, lo = _split3(x)
    return _dot(hi, b01) + _dot(mid, b01) + _dot(lo, b01)


def _rms(x, g):
    return x * lax.rsqrt(jnp.mean(x * x, axis=-1, keepdims=True) + EPS) * g


def _norm_mod(x, g, shift, scale):
    return _rms(x, g) * (1.0 + scale) + shift


def _params(n_grid):
    return pltpu.CompilerParams(dimension_semantics=("arbitrary",) * n_grid, vmem_limit_bytes=VMEM_LIMIT)


def _const_spec(shape):
    zeros = (0,) * len(shape)
    return pl.BlockSpec(shape, lambda *_: zeros, pipeline_mode=pl.Buffered(1))


def _layer_spec(shape, layer):
    index = (layer,) + (0,) * len(shape)
    return pl.BlockSpec((None,) + shape, lambda *_: index, pipeline_mode=pl.Buffered(1))


def _mod_spec(layer, cond_of):
    return pl.BlockSpec((None, None, 6, D_MODEL), lambda *idx: (layer, cond_of(*idx), 0, 0))


def _mod_kernel(cond_ref, w_ref, b_ref, o_ref):
    s = jax.nn.silu(cond_ref[...]).astype(BF16)
    o_ref[...] = _dot(s, w_ref[...].astype(BF16)) + b_ref[...]


def _modulation(cond, mod_w, mod_b):
    depth = mod_w.shape[0]
    n_out = mod_w.shape[2]
    tn = 2 * D_MODEL
    out = pl.pallas_call(
        _mod_kernel,
        grid=(depth, n_out // tn),
        in_specs=[
            pl.BlockSpec((COND_ROWS, D_MODEL), lambda l, j: (0, 0)),
            pl.BlockSpec((None, D_MODEL, tn), lambda l, j: (l, 0, j)),
            pl.BlockSpec((None, 1, tn), lambda l, j: (l, 0, j)),
        ],
        out_specs=pl.BlockSpec((None, COND_ROWS, tn), lambda l, j: (l, 0, j)),
        out_shape=jax.ShapeDtypeStruct((depth, COND_ROWS, n_out), F32),
        compiler_params=_params(2),
        name="modulation",
    )(cond, mod_w, mod_b.reshape(depth, 1, n_out))
    return out.reshape(depth, COND_ROWS, 6, D_MODEL)


def _run_staggered(programs, stagger):
    programs = list(programs)
    live, rounds = [], 0
    while programs or live:
        if programs and rounds % stagger == 0:
            live.append(programs.pop(0))
        for g in list(live):
            try:
                next(g)
            except StopIteration:
                live.remove(g)
        rounds += 1


def _even_kernel(*refs, n_chunks, nsub):
    halos, (xc_ref, mod_ref, ng_ref, win_ref, caw_ref, cab_ref, lng_ref, lnb_ref, cbw_ref, wout_ref, o_ref,
            apad, cpad, bgs, zs) = refs[:2 * nsub], refs[2 * nsub:]
    shared = (mod_ref, ng_ref, win_ref, caw_ref, cab_ref, lng_ref, lnb_ref, cbw_ref, wout_ref)
    _run_staggered(
        (_even_chunk(pl.program_id(0) * nsub + k, halos[2 * k], xc_ref.at[k], halos[2 * k + 1], *shared,
                     o_ref.at[k], apad.at[k], cpad.at[k], bgs.at[k], zs.at[k], n_chunks=n_chunks)
         for k in range(nsub)), EVEN_STAGGER)


def _even_chunk(g, xp_ref, xc_ref, xn_ref, mod_ref, ng_ref, win_ref, caw_ref, cab_ref, lng_ref, lnb_ref,
                cbw_ref, wout_ref, o_ref, apad, cpad, bgs, zs, *, n_chunks):
    c = g % n_chunks if n_chunks > 1 else 0
    rows = ROW_CHUNK + 2 * CONF_HALO
    shift1, scale1, gate1 = mod_ref[0:1, :], mod_ref[1:2, :], mod_ref[2:3, :]
    own = slice(CONF_HALO, CONF_HALO + ROW_CHUNK)
    if n_chunks == 1:
        h = _norm_mod(xc_ref[...], ng_ref[0:1, :], shift1, scale1).astype(BF16)
        h_own = h
        for pad_ref in (apad, cpad):
            pad_ref[0:CONF_HALO, :] = jnp.zeros((CONF_HALO, D_HALF), F32)
            pad_ref[CONF_HALO + ROW_CHUNK:rows, :] = jnp.zeros((CONF_HALO, D_HALF), F32)
        keep = lambda v: v
        span = own
    else:
        xh = jnp.concatenate([xp_ref[...], xc_ref[...], xn_ref[...]], axis=0)
        h = _norm_mod(xh, ng_ref[0:1, :], shift1, scale1).astype(BF16)
        h_own = h[own]
        ri = lax.broadcasted_iota(jnp.int32, (rows, D_HALF), 0)
        lo = jnp.where(c == 0, CONF_HALO, 0)
        hi = jnp.where(c == n_chunks - 1, CONF_HALO + ROW_CHUNK, rows)
        inside = (ri >= lo) & (ri < hi)
        keep = lambda v: jnp.where(inside, v, 0.0)
        span = slice(0, rows)
    a = _dot(h, win_ref[:, 0:D_HALF]) * jax.nn.sigmoid(_dot(h, win_ref[:, D_HALF:2 * D_HALF]))
    apad[span, :] = keep(a)
    yield
    cx = _dot(h, win_ref[:, 3 * D_HALF:4 * D_HALF]) * _dot(h, win_ref[:, 4 * D_HALF:5 * D_HALF])
    cpad[span, :] = keep(cx)
    yield
    bgs[...] = _dot(h_own, win_ref[:, 2 * D_HALF:3 * D_HALF])
    yield

    sub = EVEN_CONV_ROWS
    tile = 8
    for j in range(ROW_CHUNK // sub):
        r0 = j * sub
        groups = []
        for cg in range(D_HALF // LANES):
            cols = slice(cg * LANES, (cg + 1) * LANES)
            acc = None
            for r in range(tile):
                part = None
                for m in range(-(-(CONF_WIDTH + 1) // tile)):
                    o = tile * m + r
                    if 1 <= o <= CONF_WIDTH:
                        term = caw_ref[o - 1:o, cols] * apad[r0 + tile * m:r0 + tile * m + sub + tile, cols]
                        part = term if part is None else part + term
                shifted = part[r:r + sub, :]
                acc = shifted if acc is None else acc + shifted
            groups.append(acc)
        acc = jnp.concatenate(groups, axis=1) + cab_ref[...]
        mu = jnp.mean(acc, axis=-1, keepdims=True)
        dlt = acc - mu
        var = jnp.mean(dlt * dlt, axis=-1, keepdims=True)
        a_out = jax.nn.silu(dlt * lax.rsqrt(var + EPS) * lng_ref[...] + lnb_ref[...])
        zs[r0:r0 + sub, 0:D_HALF] = a_out.astype(BF16)
        base = r0 + CONF_HALO - 1
        sc = (cbw_ref[0:1, :] * cpad[base:base + sub, :]
              + cbw_ref[1:2, :] * cpad[base + 1:base + 1 + sub, :]
              + cbw_ref[2:3, :] * cpad[base + 2:base + 2 + sub, :])
        zs[r0:r0 + sub, D_HALF:D_MODEL] = (bgs[r0:r0 + sub, :] * sc).astype(BF16)
        yield

    o = _dot(zs[...], wout_ref[...])
    o_ref[...] = xc_ref[...] + gate1 * _rms(o, ng_ref[1:2, :])
    yield


def _even_layer(x, modv, layer, cond_base, cond_stride, ng, win, caw, cab, lng, lnb, cbw, wout):
    j = layer // 2
    b, t, _ = x.shape
    n_chunks = t // ROW_CHUNK
    nsub = EVEN_CHUNKS
    assert (b * n_chunks) % nsub == 0 and (cond_stride == 0 or n_chunks % nsub == 0)
    hpc = ROW_CHUNK // CONF_HALO
    n_halo_blocks = b * t // CONF_HALO
    rows = ROW_CHUNK + 2 * CONF_HALO
    halo_specs = []
    for k in range(nsub):
        halo_specs += [
            pl.BlockSpec((None, CONF_HALO, D_MODEL),
                         lambda i, k=k: (jnp.maximum((i * nsub + k) * hpc - 1, 0), 0, 0)),
            pl.BlockSpec((None, CONF_HALO, D_MODEL),
                         lambda i, k=k: (jnp.minimum((i * nsub + k + 1) * hpc, n_halo_blocks - 1), 0, 0)),
        ]
    x_halo = x.reshape(n_halo_blocks, CONF_HALO, D_MODEL)
    x_chunks = x.reshape(b * n_chunks, ROW_CHUNK, D_MODEL)
    kern = functools.partial(_even_kernel, n_chunks=n_chunks, nsub=nsub)
    out = pl.pallas_call(
        kern,
        grid=(b * n_chunks // nsub,),
        in_specs=halo_specs + [
            pl.BlockSpec((nsub, ROW_CHUNK, D_MODEL), lambda i: (i, 0, 0)),
            _mod_spec(layer, lambda i: cond_base + cond_stride * ((i * nsub) // n_chunks)),
            _layer_spec((4, D_MODEL), layer),
            _layer_spec((D_MODEL, 5 * D_HALF), j),
            _layer_spec((CONF_WIDTH, D_HALF), j),
            _layer_spec((1, D_HALF), j),
            _layer_spec((1, D_HALF), j),
            _layer_spec((1, D_HALF), j),
            _layer_spec((3, D_HALF), j),
            _layer_spec((D_MODEL, D_MODEL), j),
        ],
        out_specs=pl.BlockSpec((nsub, ROW_CHUNK, D_MODEL), lambda i: (i, 0, 0)),
        out_shape=jax.ShapeDtypeStruct(x_chunks.shape, F32),
        scratch_shapes=[
            pltpu.VMEM((nsub, rows, D_HALF), F32),
            pltpu.VMEM((nsub, rows, D_HALF), F32),
            pltpu.VMEM((nsub, ROW_CHUNK, D_HALF), F32),
            pltpu.VMEM((nsub, ROW_CHUNK, D_MODEL), BF16),
        ],
        compiler_params=_params(1),
        name="even_mixer",
    )(*([x_halo] * (2 * nsub)), x_chunks, modv, ng, win, caw, cab, lng, lnb, cbw, wout)
    return out.reshape(b, t, D_MODEL)


def _mlp_kernel(xp_ref, xs_ref, mod_ref, ng_ref, w1_hbm, w2_hbm, op_ref, os_ref, w1_s, w2_s, stage, sem, *,
                layer, n_ctx_tiles):
    n_blocks = D_FF // D_MODEL

    def block_copy(k):
        c, slot = k // 2, k % 2
        span = pl.ds(c * D_MODEL, D_MODEL)
        src = w1_hbm.at[layer, :, span] if k % 2 == 0 else w2_hbm.at[layer, span, :]
        return pltpu.make_async_copy(src, stage.at[slot], sem.at[slot])

    def fetch(k, dst):
        block_copy(k).wait()
        dst[...] = stage[k % 2].astype(BF16)
        if k + 2 < 2 * n_blocks:
            block_copy(k + 2).start()

    def rows_program(x_ref, o_ref, rows, load):
        x = x_ref[rows, :]
        h = _norm_mod(x, ng_ref[2:3, :], mod_ref[3:4, :], mod_ref[4:5, :]).astype(BF16)
        yield
        acc = None
        for c in range(n_blocks):
            cols = slice(c * D_MODEL, (c + 1) * D_MODEL)
            if load:
                fetch(2 * c, w1_s.at[:, cols])
            hid = jnp.square(jnp.maximum(_dot(h, w1_s[:, cols]), 0.0)).astype(BF16)
            yield
            if load:
                fetch(2 * c + 1, w2_s.at[cols, :])
            part = _dot(hid, w2_s[cols, :])
            acc = part if acc is None else acc + part
            yield
        o_ref[rows, :] = x + mod_ref[5:6, :] * _rms(acc, ng_ref[3:4, :])
        yield

    def tile(x_ref, o_ref, load=False):
        n_rows = x_ref.shape[0]
        if load:
            _drain(rows_program(x_ref, o_ref, slice(0, n_rows), True))
        else:
            _run_staggered((rows_program(x_ref, o_ref, slice(r0, r0 + n_rows // 2), False)
                            for r0 in (0, n_rows // 2)), MLP_STAGGER)

    i = pl.program_id(0)

    @pl.when(i == 0)
    def _():
        block_copy(0).start()
        block_copy(1).start()
        tile(xp_ref, op_ref, load=True)

    @pl.when((i > 0) & (i < n_ctx_tiles))
    def _():
        tile(xp_ref, op_ref)

    @pl.when(i >= n_ctx_tiles)
    def _():
        tile(xs_ref, os_ref)


def _mlp_layer(xp, xs, modv, layer, ng, w1, w2):
    tm = MLP_ROWS
    xp2, xs2 = xp.reshape(-1, D_MODEL), xs.reshape(-1, D_MODEL)
    t_dec = xs.shape[1]
    assert xp2.shape[0] % tm == 0 and t_dec % tm == 0
    n_p, n_s, per_seq = xp2.shape[0] // tm, xs2.shape[0] // tm, t_dec // tm
    ctx_tile = lambda i: (jnp.minimum(i, n_p - 1), 0)
    lat_tile = lambda i: (jnp.maximum(i - n_p, 0), 0)
    yp, ys = pl.pallas_call(
        functools.partial(_mlp_kernel, layer=layer, n_ctx_tiles=n_p),
        grid=(n_p + n_s,),
        in_specs=[
            pl.BlockSpec((tm, D_MODEL), ctx_tile),
            pl.BlockSpec((tm, D_MODEL), lat_tile),
            _mod_spec(layer, lambda i: jnp.where(i < n_p, 0, 1 + jnp.maximum(i - n_p, 0) // per_seq)),
            _layer_spec((4, D_MODEL), layer),
            pl.BlockSpec(memory_space=pl.ANY),
            pl.BlockSpec(memory_space=pl.ANY),
        ],
        out_specs=[pl.BlockSpec((tm, D_MODEL), ctx_tile), pl.BlockSpec((tm, D_MODEL), lat_tile)],
        out_shape=[jax.ShapeDtypeStruct(xp2.shape, F32), jax.ShapeDtypeStruct(xs2.shape, F32)],
        scratch_shapes=[
            pltpu.VMEM((D_MODEL, D_FF), BF16),
            pltpu.VMEM((D_FF, D_MODEL), BF16),
            pltpu.VMEM((2, D_MODEL, D_MODEL), F32),
            pltpu.SemaphoreType.DMA((2,)),
        ],
        compiler_params=_params(1),
        name="mlp",
    )(xp2, xs2, modv, ng, w1, w2)
    return yp.reshape(xp.shape), ys.reshape(xs.shape)


def _log_sigmoid(x):
    return jnp.minimum(x, 0.0) - jnp.log(1.0 + jnp.exp(-jnp.abs(x)))


def _rot_half(x, first_half):
    return jnp.where(first_half, pltpu.roll(x, 96, axis=1), pltpu.roll(x, 32, axis=1))


def _values_and_ones(v):
    vt = jnp.transpose(v)
    row = lax.broadcasted_iota(jnp.int32, vt.shape, 0)
    return jnp.where(row < HEAD_DIM, vt, 1.0)


def _aligned(i, m):
    return i * m if isinstance(i, int) else pl.multiple_of(i * m, m)


def _drain(pieces):
    for _ in pieces:
        pass


def _each(body, n, static, unroll=1):
    if static:
        for i in range(n):
            yield from body(i)
    else:
        lax.fori_loop(0, n, lambda i, c: (_drain(body(i)), c)[1], 0, unroll=unroll)


def _alternate(*programs):
    live = list(programs)
    while live:
        for g in list(live):
            try:
                next(g)
            except StopIteration:
                live.remove(g)
            else:
                yield


def _odd_kernel(*refs, t, lc, latent, nseq):
    shared = set(range(1, 11 if latent else 9))
    n_rows = min(t, ODD_ROWS)
    joint = not latent and nseq > 1 and t == n_rows
    finals = []
    programs = [_odd_seq(*[r if k in shared else r.at[sq] for k, r in enumerate(refs)],
                         t=t, lc=lc, latent=latent, joint=joint, finals=finals) for sq in range(nseq)]
    if joint:
        parts = [next(g) for g in programs]
        h_all = jnp.concatenate([proj_h(0) for proj_h, _ in parts], axis=0)
        for idx, (weight, _) in enumerate(parts[0][1]):
            y = _dot(h_all, weight())
            for sq, (_, plan) in enumerate(parts):
                plan[idx][1](y[sq * n_rows:(sq + 1) * n_rows], 0)
    _run_staggered(programs, ODD_STAGGER)
    if joint:
        o = _dot(jnp.concatenate([z() for z, _ in finals], axis=0), refs[8][...])
        for sq, (_, residual) in enumerate(finals):
            residual(o[sq * n_rows:(sq + 1) * n_rows])


def _odd_seq(*refs, t, lc, latent, joint, finals):
    if latent:
        (x_ref, mod_ref, ng_ref, win_ref, wg_ref, gb_ref, sink_ref, hn_ref, wout_ref,
         cos_ref, sin_ref, kc_ref, vc_ref, cin_ref, nin_ref, min_ref,
         o_ref,
         qa_s, qb_s, kk_s, vvt_s, qma_s, qmb_s, qm_s, km_s, vmt_s, vmtf_s, om_s, g_s, z_s, hft_s, hbt_s,
         sr_s, st_s, qc_s, c_s, n_s, m_s, s_s, p_s, kc_s, vct_s) = refs
    else:
        (x_ref, mod_ref, ng_ref, win_ref, wg_ref, gb_ref, sink_ref, hn_ref, wout_ref,
         o_ref, ko_ref, vo_ref, cf_ref, cb_ref, no_ref, mo_ref,
         qa_s, qb_s, kk_s, vvt_s, qma_s, qmb_s, qm_s, km_s, vmt_s, vmtf_s, om_s, g_s, z_s, hft_s, hbt_s,
         sr_s, st_s, qc_s, c_s, n_s, m_s, s_s, p_s) = refs

    static = not latent
    n_blocks = t // CHUNK
    pad = CHUNK if latent else 0
    shift1, scale1, gate1 = mod_ref[0:1, :], mod_ref[1:2, :], mod_ref[2:3, :]
    lane = lax.broadcasted_iota(jnp.int32, (1, LANES), 1)
    left = lane < HEAD_DIM
    first_half = (lane % HEAD_DIM) < (HEAD_DIM // 2)
    ti = lax.broadcasted_iota(jnp.int32, (CHUNK, CHUNK), 0)
    si = lax.broadcasted_iota(jnp.int32, (CHUNK, CHUNK), 1)
    top = ti < HEAD_DIM
    same_head = top == (si < HEAD_DIM)

    n_rows = min(t, ODD_ROWS)
    if latent:
        for kv in range(N_KV):
            kk_s[kv, 0:CHUNK, :] = jnp.zeros((CHUNK, LANES), BF16)
            kk_s[kv, CHUNK + t:2 * CHUNK + t, :] = jnp.zeros((CHUNK, LANES), BF16)
            vvt_s[kv, :, 0:CHUNK] = jnp.zeros((LANES, CHUNK), BF16)
            vvt_s[kv, :, CHUNK + t:2 * CHUNK + t] = jnp.zeros((LANES, CHUNK), BF16)
            kc_s[kv] = kc_ref[kv].astype(BF16)
            vct_s[kv] = _values_and_ones(vc_ref[kv]).astype(BF16)

    base = D_HALF + 2 * LANES

    def proj_rows(rc):
        r0 = _aligned(rc, n_rows)
        return pl.ds(r0, n_rows), pl.ds(r0 + pad, n_rows)

    def proj_h(rc):
        rows, _ = proj_rows(rc)
        return _norm_mod(x_ref[rows, :], ng_ref[0:1, :], shift1, scale1).astype(BF16)

    def rope(v, rows):
        return v * cos_ref[rows, :] + _rot_half(v, first_half) * sin_ref[rows, :]

    def use_q(q, rc):
        rows, _ = proj_rows(rc)
        for p in range(N_PAIRS):
            cols = slice(p * LANES, (p + 1) * LANES)
            qp = q[:, cols]
            if latent:
                qp = rope(qp, rows)
            qp = qp * ATT_SCALE
            qa_s[rows, cols] = jnp.where(left, qp, 0.0).astype(BF16)
            qb_s[rows, cols] = jnp.where(left, 0.0, qp).astype(BF16)

    def use_kv(kv2, rc):
        rows, krows = proj_rows(rc)
        ka, va = kv2[:, 0:LANES], kv2[:, LANES:2 * LANES]
        if latent:
            ka = rope(ka, rows)
        kr = pltpu.roll(ka, HEAD_DIM, axis=1)
        vr = pltpu.roll(va, HEAD_DIM, axis=1)
        if not latent:
            ka_t, va_t = jnp.transpose(ka), jnp.transpose(va)
            for kv in range(N_KV):
                ko_ref[0, kv, :, rows] = ka_t[kv * HEAD_DIM:(kv + 1) * HEAD_DIM, :]
                vo_ref[0, kv, :, rows] = va_t[kv * HEAD_DIM:(kv + 1) * HEAD_DIM, :]
        kk_s[0, krows, :] = jnp.where(left, ka, kr).astype(BF16)
        kk_s[1, krows, :] = jnp.where(left, kr, ka).astype(BF16)
        vvt_s[0, :, krows] = _values_and_ones(va).astype(BF16)
        vvt_s[1, :, krows] = _values_and_ones(vr).astype(BF16)

    def use_qm(qm, rc):
        rows, _ = proj_rows(rc)
        qm_s[rows, :] = qm.astype(BF16)
        for p in range(N_PAIRS):
            cols = slice(p * LANES, (p + 1) * LANES)
            qma_s[rows, cols] = jnp.where(left, qm[:, cols], 0.0).astype(BF16)
            qmb_s[rows, cols] = jnp.where(left, 0.0, qm[:, cols]).astype(BF16)

    def use_vm(vm, rc):
        rows, _ = proj_rows(rc)
        for p in range(N_PAIRS):
            cols = slice(p * LANES, (p + 1) * LANES)
            vt = jnp.transpose(vm[:, cols])
            vmtf_s[cols, rows] = vt
            vmt_s[cols, rows] = vt.astype(BF16)

    def use_km(km, rc):
        km_s[proj_rows(rc)[0], :] = (km * (HEAD_DIM ** -0.5)).astype(BF16)

    def use_om(om, rc):
        om_s[proj_rows(rc)[0], :] = om

    def use_g(g, rc):
        g_s[proj_rows(rc)[0], :] = g + gb_ref[...]

    def columns(start, width):
        return lambda: win_ref[:, start:start + width]

    proj_plan = [(columns(0, D_HALF), use_q), (columns(D_HALF, 2 * LANES), use_kv),
                 (columns(base, D_HALF), use_qm), (columns(base + 2 * D_HALF, D_HALF), use_vm),
                 (columns(base + D_HALF, D_HALF), use_km), (columns(base + 3 * D_HALF, D_HALF), use_om),
                 (lambda: wg_ref[...], use_g)]

    def project(rc):
        h = proj_h(rc)
        for weight, use in proj_plan:
            use(_dot(h, weight()), rc)
            yield

    if joint:
        yield proj_h, proj_plan
    else:
        yield from _each(project, t // n_rows, static)

    kj = lax.broadcasted_iota(jnp.int32, (3 * CHUNK, 2 * CHUNK), 0)
    qi = lax.broadcasted_iota(jnp.int32, (3 * CHUNK, 2 * CHUNK), 1) % CHUNK
    band_ok = jnp.abs(kj - CHUNK - qi) <= WINDOW
    head_a = lax.broadcasted_iota(jnp.int32, (1, 2 * CHUNK), 1) < CHUNK

    n_keys = _att_keys(lc, latent)
    group = _att_group(latent)

    def attention():
        def attend(i):
            r0 = _aligned(i, CHUNK)
            rows = pl.ds(r0, CHUNK)
            if latent:
                key_pos = kj + (i - 1) * CHUNK
                mask = band_ok & (key_pos >= 0) & (key_pos < t)
                win = pl.ds(r0, 3 * CHUNK)
            for g0 in range(0, N_PAIRS, group):
                for gi in range(group):
                    p = g0 + gi
                    kv = p // (N_PAIRS // N_KV)
                    cols = slice(p * LANES, (p + 1) * LANES)
                    q2 = jnp.concatenate([qa_s[rows, cols], qb_s[rows, cols]], axis=0)
                    if latent:
                        s_s[gi, 0:lc, :] = _dot_nt(kc_s[kv], q2)
                        s_s[gi, lc:n_keys, :] = jnp.where(mask, _dot_nt(kk_s[kv, win, :], q2), -jnp.inf)
                    else:
                        s_s[gi] = _dot_nt(kk_s[kv], q2)
                yield
                maxes = []
                for gi in range(group):
                    p = g0 + gi
                    sink = jnp.where(head_a, sink_ref[2 * p:2 * p + 1, 0:1], sink_ref[2 * p + 1:2 * p + 2, 0:1])
                    mx = jnp.maximum(jnp.max(s_s[gi], axis=0, keepdims=True), sink)
                    p_s[gi] = jnp.exp(s_s[gi] - mx).astype(BF16)
                    maxes.append((sink, mx))
                yield
                for gi in range(group):
                    p = g0 + gi
                    kv = p // (N_PAIRS // N_KV)
                    cols = slice(p * LANES, (p + 1) * LANES)
                    if latent:
                        num = (_dot(vct_s[kv], p_s[gi, 0:lc, :]) + _dot(vvt_s[kv, :, win], p_s[gi, lc:n_keys, :]))
                    else:
                        num = _dot(vvt_s[kv], p_s[gi])
                    sink, mx = maxes[gi]
                    den = num[HEAD_DIM:HEAD_DIM + 1, :] + jnp.exp(sink - mx)
                    out = num[0:HEAD_DIM, :] * (1.0 / den)
                    pair = jnp.concatenate([out[:, 0:CHUNK], out[:, CHUNK:2 * CHUNK]], axis=0)
                    z_s[rows, cols] = jnp.transpose(pair).astype(BF16)
                yield

        return attend

    if latent:
        c_s[...] = cin_ref[...]
        n_s[...] = nin_ref[...]
        m_s[...] = min_ref[...]
    else:
        c_s[...] = jnp.zeros(c_s.shape, F32)
        n_s[...] = jnp.zeros(n_s.shape, F32)
        m_s[...] = jnp.zeros(m_s.shape, F32)

    see = (ti <= si, ti >= si)
    tri = tuple(m.astype(F32).astype(BF16) for m in see)
    last = (CHUNK - 1, 0)

    def mlstm(i):
        offs = (_aligned(i, CHUNK), _aligned(n_blocks - 1 - i, CHUNK))
        half = (slice(0, LANES), slice(LANES, 2 * LANES))
        a_rows, b_rows = [], []
        for d in range(2):
            gt = jnp.transpose(g_s[pl.ds(offs[d], CHUNK), :])
            lf = _log_sigmoid(gt[2 * N_HEADS:4 * N_HEADS, :])
            bcum = _dot_exact_lhs(lf, tri[d])[8 * d:8 * d + 8, :]
            a_rows.append(gt[8 * d:8 * d + 8, :] - bcum)
            b_rows.append(bcum)
        yield
        q_ns = []
        for d in range(2):
            rows = pl.ds(offs[d], CHUNK)
            for p in range(N_PAIRS):
                u = d * N_PAIRS + p
                cols = slice(p * LANES, (p + 1) * LANES)
                q2 = jnp.concatenate([qma_s[rows, cols], qmb_s[rows, cols]], axis=0)
                sr_s[u] = _dot_nt(km_s[rows, cols], q2)
                qc_s[u] = _dot_nt(c_s[u].astype(BF16), qm_s[rows, cols])
                n8 = jnp.broadcast_to(n_s[u:u + 1, :], (8, LANES)).astype(BF16)
                q_ns.append(_dot_nt(n8, q2)[0:1, :])
            yield
        stats = {}
        for d in range(2):
            for hd in range(N_HEADS):
                u, j = d * N_PAIRS + hd // 2, hd % 2
                mrow = d * N_HEADS + hd
                a_row = a_rows[d][hd:hd + 1, :]
                b_row = b_rows[d][hd:hd + 1, :]
                m_prev = m_s[mrow:mrow + 1, 0:1]
                a_col = jnp.transpose(jnp.broadcast_to(a_row, (CHUNK, CHUNK)))
                z_t = jnp.where(see[d], a_col, -jnp.inf)
                m_run = jnp.maximum(jnp.max(z_t, axis=0, keepdims=True), m_prev)
                s_t = sr_s[u, :, half[j]] * jnp.exp(z_t - m_run)
                st_s[u, :, half[j]] = s_t.astype(BF16)
                w_int = jnp.exp(m_prev - m_run)
                den = jnp.sum(s_t, axis=0, keepdims=True) + w_int * q_ns[u][:, half[j]]
                inv = 1.0 / jnp.maximum(jnp.abs(den), jnp.exp(-(b_row + m_run)))
                m_last = m_run[:, last[d]:last[d] + 1]
                stats[(u, j)] = (w_int, inv, jnp.exp(a_row - m_last), jnp.exp(m_prev - m_last))
                m_s[mrow:mrow + 1, :] = jnp.broadcast_to(b_row[:, last[d]:last[d] + 1] + m_last, (1, LANES))
            yield
        for d in range(2):
            rows = pl.ds(offs[d], CHUNK)
            ht_s = (hft_s, hbt_s)[d]
            for p in range(N_PAIRS):
                u = d * N_PAIRS + p
                cols = slice(p * LANES, (p + 1) * LANES)
                (w_a, inv_a, e_a, dec_a), (w_b, inv_b, e_b, dec_b) = stats[(u, 0)], stats[(u, 1)]
                kp = km_s[rows, cols]
                num2 = _dot(vmt_s[cols, rows], st_s[u])
                num = jnp.where(top, num2[:, half[0]], num2[:, half[1]])
                ht_s[cols, rows] = (num + jnp.where(top, w_a, w_b) * qc_s[u]) * jnp.where(top, inv_a, inv_b)
                vt_e = (vmtf_s[cols, rows] * jnp.where(top, e_a, e_b)).astype(BF16)
                c_s[u] = jnp.where(top, dec_a, dec_b) * c_s[u] + jnp.where(same_head, _dot(vt_e, kp), 0.0)
                e2 = jnp.concatenate([e_a, e_b, jnp.zeros((6, CHUNK), F32)], axis=0).astype(BF16)
                n_k = _dot(e2, kp)
                n_s[u:u + 1, :] = (jnp.where(left, dec_a, dec_b) * n_s[u:u + 1, :]
                                   + jnp.where(left, n_k[0:1, :], n_k[1:2, :]))
            yield

    attend = attention()
    yield from _each(lambda i: _alternate(attend(i), mlstm(i)), n_blocks, static, unroll=4)

    if not latent:
        for d, c_ref in enumerate((cf_ref, cb_ref)):
            for p in range(N_PAIRS):
                c_pair = jnp.transpose(c_s[d * N_PAIRS + p])
                c_ref[0, 2 * p] = c_pair[0:HEAD_DIM, 0:HEAD_DIM]
                c_ref[0, 2 * p + 1] = c_pair[HEAD_DIM:LANES, HEAD_DIM:LANES]
        no_ref[...] = n_s[...]
        mo_ref[...] = m_s[...]
        yield

    top_w = lax.broadcasted_iota(jnp.int32, (LANES, n_rows), 0) < HEAD_DIM

    def finish(rc):
        r0 = _aligned(rc, n_rows)
        rows = pl.ds(r0, n_rows)
        for p in range(N_PAIRS):
            cols = slice(p * LANES, (p + 1) * LANES)
            hm = hft_s[cols, rows] + hbt_s[cols, rows]
            sq = hm * hm
            ms_a = jnp.sum(sq[0:HEAD_DIM], axis=0, keepdims=True)
            ms_b = jnp.sum(sq[HEAD_DIM:LANES], axis=0, keepdims=True)
            ms = jnp.where(top_w, ms_a, ms_b) * (1.0 / HEAD_DIM)
            y = jnp.transpose(hm * lax.rsqrt(ms + EPS)) * hn_ref[:, cols] * jax.nn.sigmoid(om_s[rows, cols])
            z_s[rows, D_HALF + p * LANES:D_HALF + (p + 1) * LANES] = y.astype(BF16)
            if p % 2:
                yield
        def residual(o):
            o_ref[rows, :] = x_ref[rows, :] + gate1 * _rms(o, ng_ref[1:2, :])

        if joint:
            finals.append((lambda: z_s[rows, :], residual))
        else:
            residual(_dot(z_s[rows, :], wout_ref[...]))
        yield

    yield from _each(finish, t // n_rows, static)


def _att_keys(lc, latent):
    return lc + 3 * CHUNK if latent else lc


def _att_group(latent):
    return N_PAIRS


def _odd_scratch(t, lc, latent, nseq):
    pad = 2 * CHUNK if latent else 0
    att = (_att_group(latent), _att_keys(lc, latent), 2 * CHUNK)
    shapes = [
        pltpu.VMEM((t, D_HALF), BF16),
        pltpu.VMEM((t, D_HALF), BF16),
        pltpu.VMEM((N_KV, t + pad, LANES), BF16),
        pltpu.VMEM((N_KV, LANES, t + pad), BF16),
        pltpu.VMEM((t, D_HALF), BF16),
        pltpu.VMEM((t, D_HALF), BF16),
        pltpu.VMEM((t, D_HALF), BF16),
        pltpu.VMEM((t, D_HALF), BF16),
        pltpu.VMEM((D_HALF, t), BF16),
        pltpu.VMEM((D_HALF, t), F32),
        pltpu.VMEM((t, D_HALF), F32),
        pltpu.VMEM((t, LANES), F32),
        pltpu.VMEM((t, D_MODEL), BF16),
        pltpu.VMEM((D_HALF, t), F32),
        pltpu.VMEM((D_HALF, t), F32),
        pltpu.VMEM((2 * N_PAIRS, CHUNK, 2 * LANES), F32),
        pltpu.VMEM((2 * N_PAIRS, CHUNK, 2 * LANES), BF16),
        pltpu.VMEM((2 * N_PAIRS, LANES, CHUNK), F32),
        pltpu.VMEM((2 * N_PAIRS, LANES, LANES), F32),
        pltpu.VMEM((2 * N_PAIRS, LANES), F32),
        pltpu.VMEM((2 * N_HEADS, LANES), F32),
        pltpu.VMEM(att, F32),
        pltpu.VMEM(att, BF16),
    ]
    if latent:
        shapes += [pltpu.VMEM((N_KV, lc, LANES), BF16), pltpu.VMEM((N_KV, LANES, lc), BF16)]
    return [pltpu.VMEM((nseq,) + tuple(sh.shape), sh.dtype) for sh in shapes]


def _odd_common_specs(t, layer, cond_base, cond_stride, nseq):
    assert cond_stride == 0 or nseq == 1
    j = layer // 2
    return [
        _per_seq((t, D_MODEL), nseq),
        _mod_spec(layer, lambda i: cond_base + cond_stride * i),
        _layer_spec((4, D_MODEL), layer),
        _layer_spec((D_MODEL, D_IN_ODD), j),
        _const_spec((D_MODEL, LANES)),
        _const_spec((1, LANES)),
        _const_spec((N_HEADS, LANES)),
        _layer_spec((1, D_HALF), j),
        _layer_spec((D_MODEL, D_MODEL), j),
    ]


def _per_seq(shape, nseq):
    return pl.BlockSpec((nseq,) + shape, lambda i: (i,) + (0,) * len(shape))


def _odd_context(x, modv, layer, ng, w_main, w_gate, gate_bias, sink_b, hnorm, wout):
    b, t, _ = x.shape
    nseq = ODD_CTX_SEQS
    assert b % nseq == 0
    kern = functools.partial(_odd_kernel, t=t, lc=t, latent=False, nseq=nseq)
    per_seq = functools.partial(_per_seq, nseq=nseq)
    return pl.pallas_call(
        kern,
        grid=(b // nseq,),
        in_specs=_odd_common_specs(t, layer, 0, 0, nseq),
        out_specs=[per_seq((t, D_MODEL)),
                   per_seq((1, N_KV, HEAD_DIM, t)), per_seq((1, N_KV, HEAD_DIM, t)),
                   per_seq((1, N_HEADS, HEAD_DIM, HEAD_DIM)), per_seq((1, N_HEADS, HEAD_DIM, HEAD_DIM)),
                   per_seq((2 * N_PAIRS, LANES)), per_seq((2 * N_HEADS, LANES))],
        out_shape=[jax.ShapeDtypeStruct((b, t, D_MODEL), F32),
                   jax.ShapeDtypeStruct((b, 1, N_KV, HEAD_DIM, t), F32),
                   jax.ShapeDtypeStruct((b, 1, N_KV, HEAD_DIM, t), F32),
                   jax.ShapeDtypeStruct((b, 1, N_HEADS, HEAD_DIM, HEAD_DIM), F32),
                   jax.ShapeDtypeStruct((b, 1, N_HEADS, HEAD_DIM, HEAD_DIM), F32),
                   jax.ShapeDtypeStruct((b, 2 * N_PAIRS, LANES), F32),
                   jax.ShapeDtypeStruct((b, 2 * N_HEADS, LANES), F32)],
        scratch_shapes=_odd_scratch(t, t, False, nseq),
        compiler_params=_params(1),
        name="odd_mixer_context",
    )(x, modv, ng, w_main, w_gate, gate_bias, sink_b, hnorm, wout)


def _odd_latent(x, modv, layer, ng, w_main, w_gate, gate_bias, sink_b, hnorm, wout, cos_t, sin_t, kc, vc,
                c_in, n_in, m_in):
    b, t, _ = x.shape
    lc = kc.shape[2]
    kern = functools.partial(_odd_kernel, t=t, lc=lc, latent=True, nseq=1)
    per_seq = functools.partial(_per_seq, nseq=1)
    return pl.pallas_call(
        kern,
        grid=(b,),
        in_specs=_odd_common_specs(t, layer, 1, 1, 1) + [
            _const_spec((t, LANES)), _const_spec((t, LANES)),
            per_seq((N_KV, lc, LANES)), per_seq((N_KV, lc, LANES)),
            per_seq((2 * N_PAIRS, LANES, LANES)), per_seq((2 * N_PAIRS, LANES)), per_seq((2 * N_HEADS, LANES)),
        ],
        out_specs=per_seq((t, D_MODEL)),
        out_shape=jax.ShapeDtypeStruct((b, t, D_MODEL), F32),
        scratch_shapes=_odd_scratch(t, lc, True, 1),
        compiler_params=_params(1),
        name="odd_mixer_latent",
    )(x, modv, ng, w_main, w_gate, gate_bias, sink_b, hnorm, wout, cos_t, sin_t, kc, vc, c_in, n_in, m_in)


def _rope_tables(t):
    rows = t // GRID_W
    row = jnp.broadcast_to(jnp.arange(rows)[:, None], (rows, GRID_W)).reshape(t).astype(F32)
    col = jnp.broadcast_to(jnp.arange(GRID_W)[None, :], (rows, GRID_W)).reshape(t).astype(F32)
    n_freq = HEAD_DIM // 4
    inv_freq = ROPE_BASE ** (-jnp.arange(n_freq, dtype=F32) / n_freq)
    ang = jnp.concatenate([row[:, None] * inv_freq, col[:, None] * inv_freq], axis=-1)
    cos, sin = jnp.cos(ang), jnp.sin(ang)
    cos_l = jnp.tile(cos, (1, LANES // cos.shape[1]))
    sin_l = jnp.tile(jnp.concatenate([-sin, sin], axis=-1), (1, LANES // HEAD_DIM))
    return cos_l, sin_l


def _pair_blockdiag(c):
    b = c.shape[0]
    c = c.reshape(b, N_PAIRS, 2, HEAD_DIM, HEAD_DIM)
    z = jnp.zeros_like(c[:, :, 0])
    top = jnp.concatenate([c[:, :, 0], z], axis=-1)
    bot = jnp.concatenate([z, c[:, :, 1]], axis=-1)
    return jnp.concatenate([top, bot], axis=-2)


def _lane_bcast(v):
    return jnp.broadcast_to(v[..., None], v.shape + (LANES,))


def kernel(x_prompt, x_sample, c, cache_k, cache_v, state_c_fwd, state_n_fwd, state_m_fwd, state_c_bwd, state_n_bwd, state_m_bwd, c_ctx, mod_w, mod_b, norm_g, mlp_w1, mlp_w2, even_in_w, conv_a_w, conv_a_b, ln_a_g, ln_a_b, conv_b_w, even_out_w, odd_in_w, attn_sink, gate_b, hnorm_g, odd_out_w):
    n_dec = x_sample.shape[0]
    n_ctx = x_prompt.shape[0]
    cond = jnp.concatenate([c_ctx[None, :], c, jnp.zeros((COND_ROWS - 1 - n_dec, D_MODEL), F32)], axis=0)
    modv = _modulation(cond, mod_w, mod_b)

    yp, ys = x_prompt, x_sample
    w1, w2 = mlp_w1, mlp_w2

    ev = (norm_g, even_in_w.astype(BF16), conv_a_w, conv_a_b[:, None, :], ln_a_g[:, None, :], ln_a_b[:, None, :],
          conv_b_w, even_out_w.astype(BF16))
    yp = _even_layer(yp, modv, 0, 0, 0, *ev)
    ys = _even_layer(ys, modv, 0, 1, 1, *ev)
    yp, ys = _mlp_layer(yp, ys, modv, 0, norm_g, w1, w2)

    order = jnp.array([0, 2, 1, 3])
    d_main = D_IN_ODD - 4 * N_HEADS
    wg = odd_in_w[0][:, d_main:].reshape(D_MODEL, 4, N_HEADS)[:, order, :].reshape(D_MODEL, 4 * N_HEADS)
    w_gate = jnp.pad(wg, ((0, 0), (0, LANES - 4 * N_HEADS))).astype(BF16)
    gate_bias = jnp.pad(gate_b[0][order, :].reshape(1, 4 * N_HEADS), ((0, 0), (0, LANES - 4 * N_HEADS)))
    sink_b = _lane_bcast(attn_sink[0])
    odd = (1, norm_g, odd_in_w.astype(BF16), w_gate, gate_bias, sink_b, hnorm_g[:, None, :], odd_out_w.astype(BF16))

    op, k_t, v_t, c_f, c_b, n_new, m_new = _odd_context(yp, modv, *odd)
    new_k, new_v = jnp.swapaxes(k_t, -1, -2), jnp.swapaxes(v_t, -1, -2)

    t_dec = x_sample.shape[1]
    cos_t, sin_t = _rope_tables(t_dec)
    kc = jnp.concatenate([cache_k[:, 0], cache_k[:, 0]], axis=-1)
    vc = jnp.concatenate([cache_v[:, 0], cache_v[:, 0]], axis=-1)
    c_in = jnp.concatenate([_pair_blockdiag(jnp.swapaxes(state_c_fwd[:, 0], -1, -2)),
                            _pair_blockdiag(jnp.swapaxes(state_c_bwd[:, 0], -1, -2))], axis=1)
    n_in = jnp.concatenate([state_n_fwd[:, 0].reshape(n_dec, N_PAIRS, LANES),
                            state_n_bwd[:, 0].reshape(n_dec, N_PAIRS, LANES)], axis=1)
    m_in = _lane_bcast(jnp.concatenate([state_m_fwd[:, 0], state_m_bwd[:, 0]], axis=1))
    os_ = _odd_latent(ys, modv, *odd, cos_t, sin_t, kc, vc, c_in, n_in, m_in)

    yp, ys = _mlp_layer(op, os_, modv, 1, norm_g, w1, w2)

    n_f = n_new[:, :N_PAIRS].reshape(n_ctx, N_HEADS, HEAD_DIM)[:, None]
    n_b = n_new[:, N_PAIRS:].reshape(n_ctx, N_HEADS, HEAD_DIM)[:, None]
    m_f = m_new[:, :N_HEADS, 0][:, None]
    m_b = m_new[:, N_HEADS:, 0][:, None]
    return (yp, ys, new_k, new_v, c_f, n_f, m_f, c_b, n_b, m_b)
```

```python
import functools

import jax
import jax.numpy as jnp
from jax import lax
from jax.experimental import pallas as pl
from jax.experimental.pallas import tpu as pltpu

F32 = jnp.float32
BF16 = jnp.bfloat16

D_MODEL = 1024
D_FF = 4 * D_MODEL
EPS = 1e-6
D_HALF = D_MODEL // 2
CONF_WIDTH = 31
CONF_HALO = 16
HEAD_DIM = 64
N_HEADS = 8
N_PAIRS = N_HEADS // 2
N_KV = 2
LANES = 128
CHUNK = 128
WINDOW = 128
GRID_W = 64
ROPE_BASE = 10000.0
ATT_SCALE = HEAD_DIM ** -0.5
D_IN_ODD = D_HALF + 2 * N_KV * HEAD_DIM + 4 * D_HALF + 4 * N_HEADS
ROW_CHUNK = 256
MLP_ROWS = 512
MLP_STAGGER = 3
ODD_ROWS = 512
ODD_CTX_SEQS = 2
ODD_STAGGER = 6
EVEN_CONV_ROWS = 64
EVEN_CHUNKS = 2
EVEN_STAGGER = 3
COND_ROWS = 8
VMEM_LIMIT = 56 * 1024 * 1024


def _dot(a, b):
    return jnp.dot(a, b, preferred_element_type=F32)


def _dot_nt(a, b):
    return lax.dot_general(a, b, (((1,), (1,)), ((), ())), preferred_element_type=F32)


def _split3(x):
    hi = x.astype(BF16)
    r1 = x - hi.astype(F32)
    mid = r1.astype(BF16)
    lo = (r1 - mid.astype(F32)).astype(BF16)
    return hi, mid, lo


def _dot_exact_lhs(x, b01):
    hi, mid, lo = _split3(x)
    return _dot(hi, b01) + _dot(mid, b01) + _dot(lo, b01)


def _rms(x, g):
    return x * lax.rsqrt(jnp.mean(x * x, axis=-1, keepdims=True) + EPS) * g


def _norm_mod(x, g, shift, scale):
    return _rms(x, g) * (1.0 + scale) + shift


def _params(n_grid):
    return pltpu.CompilerParams(dimension_semantics=("arbitrary",) * n_grid, vmem_limit_bytes=VMEM_LIMIT)


def _const_spec(shape):
    zeros = (0,) * len(shape)
    return pl.BlockSpec(shape, lambda *_: zeros, pipeline_mode=pl.Buffered(1))


def _layer_spec(shape, layer):
    index = (layer,) + (0,) * len(shape)
    return pl.BlockSpec((None,) + shape, lambda *_: index, pipeline_mode=pl.Buffered(1))


def _mod_spec(layer, cond_of):
    return pl.BlockSpec((None, None, 6, D_MODEL), lambda *idx: (layer, cond_of(*idx), 0, 0))


def _mod_kernel(cond_ref, w_ref, b_ref, o_ref):
    s = jax.nn.silu(cond_ref[...]).astype(BF16)
    o_ref[...] = _dot(s, w_ref[...].astype(BF16)) + b_ref[...]


def _modulation(cond, mod_w, mod_b):
    depth = mod_w.shape[0]
    n_out = mod_w.shape[2]
    tn = 2 * D_MODEL
    out = pl.pallas_call(
        _mod_kernel,
        grid=(depth, n_out // tn),
        in_specs=[
            pl.BlockSpec((COND_ROWS, D_MODEL), lambda l, j: (0, 0)),
            pl.BlockSpec((None, D_MODEL, tn), lambda l, j: (l, 0, j)),
            pl.BlockSpec((None, 1, tn), lambda l, j: (l, 0, j)),
        ],
        out_specs=pl.BlockSpec((None, COND_ROWS, tn), lambda l, j: (l, 0, j)),
        out_shape=jax.ShapeDtypeStruct((depth, COND_ROWS, n_out), F32),
        compiler_params=_params(2),
        name="modulation",
    )(cond, mod_w, mod_b.reshape(depth, 1, n_out))
    return out.reshape(depth, COND_ROWS, 6, D_MODEL)


def _run_staggered(programs, stagger):
    programs = list(programs)
    live, rounds = [], 0
    while programs or live:
        if programs and rounds % stagger == 0:
            live.append(programs.pop(0))
        for g in list(live):
            try:
                next(g)
            except StopIteration:
                live.remove(g)
        rounds += 1


def _even_kernel(*refs, n_chunks, nsub):
    halos, (xc_ref, mod_ref, ng_ref, win_ref, caw_ref, cab_ref, lng_ref, lnb_ref, cbw_ref, wout_ref, o_ref,
            apad, cpad, bgs, zs) = refs[:2 * nsub], refs[2 * nsub:]
    shared = (mod_ref, ng_ref, win_ref, caw_ref, cab_ref, lng_ref, lnb_ref, cbw_ref, wout_ref)
    _run_staggered(
        (_even_chunk(pl.program_id(0) * nsub + k, halos[2 * k], xc_ref.at[k], halos[2 * k + 1], *shared,
                     o_ref.at[k], apad.at[k], cpad.at[k], bgs.at[k], zs.at[k], n_chunks=n_chunks)
         for k in range(nsub)), EVEN_STAGGER)


def _even_chunk(g, xp_ref, xc_ref, xn_ref, mod_ref, ng_ref, win_ref, caw_ref, cab_ref, lng_ref, lnb_ref,
                cbw_ref, wout_ref, o_ref, apad, cpad, bgs, zs, *, n_chunks):
    c = g % n_chunks if n_chunks > 1 else 0
    rows = ROW_CHUNK + 2 * CONF_HALO
    shift1, scale1, gate1 = mod_ref[0:1, :], mod_ref[1:2, :], mod_ref[2:3, :]
    own = slice(CONF_HALO, CONF_HALO + ROW_CHUNK)
    if n_chunks == 1:
        h = _norm_mod(xc_ref[...], ng_ref[0:1, :], shift1, scale1).astype(BF16)
        h_own = h
        for pad_ref in (apad, cpad):
            pad_ref[0:CONF_HALO, :] = jnp.zeros((CONF_HALO, D_HALF), F32)
            pad_ref[CONF_HALO + ROW_CHUNK:rows, :] = jnp.zeros((CONF_HALO, D_HALF), F32)
        keep = lambda v: v
        span = own
    else:
        xh = jnp.concatenate([xp_ref[...], xc_ref[...], xn_ref[...]], axis=0)
        h = _norm_mod(xh, ng_ref[0:1, :], shift1, scale1).astype(BF16)
        h_own = h[own]
        ri = lax.broadcasted_iota(jnp.int32, (rows, D_HALF), 0)
        lo = jnp.where(c == 0, CONF_HALO, 0)
        hi = jnp.where(c == n_chunks - 1, CONF_HALO + ROW_CHUNK, rows)
        inside = (ri >= lo) & (ri < hi)
        keep = lambda v: jnp.where(inside, v, 0.0)
        span = slice(0, rows)
    a = _dot(h, win_ref[:, 0:D_HALF]) * jax.nn.sigmoid(_dot(h, win_ref[:, D_HALF:2 * D_HALF]))
    apad[span, :] = keep(a)
    yield
    cx = _dot(h, win_ref[:, 3 * D_HALF:4 * D_HALF]) * _dot(h, win_ref[:, 4 * D_HALF:5 * D_HALF])
    cpad[span, :] = keep(cx)
    yield
    bgs[...] = _dot(h_own, win_ref[:, 2 * D_HALF:3 * D_HALF])
    yield

    sub = EVEN_CONV_ROWS
    tile = 8
    for j in range(ROW_CHUNK // sub):
        r0 = j * sub
        groups = []
        for cg in range(D_HALF // LANES):
            cols = slice(cg * LANES, (cg + 1) * LANES)
            acc = None
            for r in range(tile):
                part = None
                for m in range(-(-(CONF_WIDTH + 1) // tile)):
                    o = tile * m + r
                    if 1 <= o <= CONF_WIDTH:
                        term = caw_ref[o - 1:o, cols] * apad[r0 + tile * m:r0 + tile * m + sub + tile, cols]
                        part = term if part is None else part + term
                shifted = part[r:r + sub, :]
                acc = shifted if acc is None else acc + shifted
            groups.append(acc)
        acc = jnp.concatenate(groups, axis=1) + cab_ref[...]
        mu = jnp.mean(acc, axis=-1, keepdims=True)
        dlt = acc - mu
        var = jnp.mean(dlt * dlt, axis=-1, keepdims=True)
        a_out = jax.nn.silu(dlt * lax.rsqrt(var + EPS) * lng_ref[...] + lnb_ref[...])
        zs[r0:r0 + sub, 0:D_HALF] = a_out.astype(BF16)
        base = r0 + CONF_HALO - 1
        sc = (cbw_ref[0:1, :] * cpad[base:base + sub, :]
              + cbw_ref[1:2, :] * cpad[base + 1:base + 1 + sub, :]
              + cbw_ref[2:3, :] * cpad[base + 2:base + 2 + sub, :])
        zs[r0:r0 + sub, D_HALF:D_MODEL] = (bgs[r0:r0 + sub, :] * sc).astype(BF16)
        yield

    o = _dot(zs[...], wout_ref[...])
    o_ref[...] = xc_ref[...] + gate1 * _rms(o, ng_ref[1:2, :])
    yield


def _even_layer(x, modv, layer, cond_base, cond_stride, ng, win, caw, cab, lng, lnb, cbw, wout):
    j = layer // 2
    b, t, _ = x.shape
    n_chunks = t // ROW_CHUNK
    nsub = EVEN_CHUNKS
    assert (b * n_chunks) % nsub == 0 and (cond_stride == 0 or n_chunks % nsub == 0)
    hpc = ROW_CHUNK // CONF_HALO
    n_halo_blocks = b * t // CONF_HALO
    rows = ROW_CHUNK + 2 * CONF_HALO
    halo_specs = []
    for k in range(nsub):
        halo_specs += [
            pl.BlockSpec((None, CONF_HALO, D_MODEL),
                         lambda i, k=k: (jnp.maximum((i * nsub + k) * hpc - 1, 0), 0, 0)),
            pl.BlockSpec((None, CONF_HALO, D_MODEL),
                         lambda i, k=k: (jnp.minimum((i * nsub + k + 1) * hpc, n_halo_blocks - 1), 0, 0)),
        ]
    x_halo = x.reshape(n_halo_blocks, CONF_HALO, D_MODEL)
    x_chunks = x.reshape(b * n_chunks, ROW_CHUNK, D_MODEL)
    kern = functools.partial(_even_kernel, n_chunks=n_chunks, nsub=nsub)
    out = pl.pallas_call(
        kern,
        grid=(b * n_chunks // nsub,),
        in_specs=halo_specs + [
            pl.BlockSpec((nsub, ROW_CHUNK, D_MODEL), lambda i: (i, 0, 0)),
            _mod_spec(layer, lambda i: cond_base + cond_stride * ((i * nsub) // n_chunks)),
            _layer_spec((4, D_MODEL), layer),
            _layer_spec((D_MODEL, 5 * D_HALF), j),
            _layer_spec((CONF_WIDTH, D_HALF), j),
            _layer_spec((1, D_HALF), j),
            _layer_spec((1, D_HALF), j),
            _layer_spec((1, D_HALF), j),
            _layer_spec((3, D_HALF), j),
            _layer_spec((D_MODEL, D_MODEL), j),
        ],
        out_specs=pl.BlockSpec((nsub, ROW_CHUNK, D_MODEL), lambda i: (i, 0, 0)),
        out_shape=jax.ShapeDtypeStruct(x_chunks.shape, F32),
        scratch_shapes=[
            pltpu.VMEM((nsub, rows, D_HALF), F32),
            pltpu.VMEM((nsub, rows, D_HALF), F32),
            pltpu.VMEM((nsub, ROW_CHUNK, D_HALF), F32),
            pltpu.VMEM((nsub, ROW_CHUNK, D_MODEL), BF16),
        ],
        compiler_params=_params(1),
        name="even_mixer",
    )(*([x_halo] * (2 * nsub)), x_chunks, modv, ng, win, caw, cab, lng, lnb, cbw, wout)
    return out.reshape(b, t, D_MODEL)


def _mlp_kernel(xp_ref, xs_ref, mod_ref, ng_ref, w1_hbm, w2_hbm, op_ref, os_ref, w1_s, w2_s, stage, sem, *,
                layer, n_ctx_tiles):
    n_blocks = D_FF // D_MODEL

    def block_copy(k):
        c, slot = k // 2, k % 2
        span = pl.ds(c * D_MODEL, D_MODEL)
        src = w1_hbm.at[layer, :, span] if k % 2 == 0 else w2_hbm.at[layer, span, :]
        return pltpu.make_async_copy(src, stage.at[slot], sem.at[slot])

    def fetch(k, dst):
        block_copy(k).wait()
        dst[...] = stage[k % 2].astype(BF16)
        if k + 2 < 2 * n_blocks:
            block_copy(k + 2).start(priority=k % 2)

    def rows_program(x_ref, o_ref, rows, load):
        x = x_ref[rows, :]
        h = _norm_mod(x, ng_ref[2:3, :], mod_ref[3:4, :], mod_ref[4:5, :]).astype(BF16)
        yield
        acc = None
        for c in range(n_blocks):
            cols = slice(c * D_MODEL, (c + 1) * D_MODEL)
            if load:
                fetch(2 * c, w1_s.at[:, cols])
            hid = jnp.square(jnp.maximum(_dot(h, w1_s[:, cols]), 0.0)).astype(BF16)
            yield
            if load:
                fetch(2 * c + 1, w2_s.at[cols, :])
            part = _dot(hid, w2_s[cols, :])
            acc = part if acc is None else acc + part
            yield
        o_ref[rows, :] = x + mod_ref[5:6, :] * _rms(acc, ng_ref[3:4, :])
        yield

    def tile(x_ref, o_ref, load=False):
        n_rows = x_ref.shape[0]
        if load:
            _drain(rows_program(x_ref, o_ref, slice(0, n_rows), True))
        else:
            _run_staggered((rows_program(x_ref, o_ref, slice(r0, r0 + n_rows // 2), False)
                            for r0 in (0, n_rows // 2)), MLP_STAGGER)

    i = pl.program_id(0)

    @pl.when(i == 0)
    def _():
        block_copy(0).start(priority=0)
        block_copy(1).start(priority=1)
        tile(xp_ref, op_ref, load=True)

    @pl.when((i > 0) & (i < n_ctx_tiles))
    def _():
        tile(xp_ref, op_ref)

    @pl.when(i >= n_ctx_tiles)
    def _():
        tile(xs_ref, os_ref)


def _mlp_layer(xp, xs, modv, layer, ng, w1, w2):
    tm = MLP_ROWS
    xp2, xs2 = xp.reshape(-1, D_MODEL), xs.reshape(-1, D_MODEL)
    t_dec = xs.shape[1]
    assert xp2.shape[0] % tm == 0 and t_dec % tm == 0
    n_p, n_s, per_seq = xp2.shape[0] // tm, xs2.shape[0] // tm, t_dec // tm
    ctx_tile = lambda i: (jnp.minimum(i, n_p - 1), 0)
    lat_tile = lambda i: (jnp.maximum(i - n_p, 0), 0)
    yp, ys = pl.pallas_call(
        functools.partial(_mlp_kernel, layer=layer, n_ctx_tiles=n_p),
        grid=(n_p + n_s,),
        in_specs=[
            pl.BlockSpec((tm, D_MODEL), ctx_tile),
            pl.BlockSpec((tm, D_MODEL), lat_tile),
            _mod_spec(layer, lambda i: jnp.where(i < n_p, 0, 1 + jnp.maximum(i - n_p, 0) // per_seq)),
            _layer_spec((4, D_MODEL), layer),
            pl.BlockSpec(memory_space=pl.ANY),
            pl.BlockSpec(memory_space=pl.ANY),
        ],
        out_specs=[pl.BlockSpec((tm, D_MODEL), ctx_tile), pl.BlockSpec((tm, D_MODEL), lat_tile)],
        out_shape=[jax.ShapeDtypeStruct(xp2.shape, F32), jax.ShapeDtypeStruct(xs2.shape, F32)],
        scratch_shapes=[
            pltpu.VMEM((D_MODEL, D_FF), BF16),
            pltpu.VMEM((D_FF, D_MODEL), BF16),
            pltpu.VMEM((2, D_MODEL, D_MODEL), F32),
            pltpu.SemaphoreType.DMA((2,)),
        ],
        compiler_params=_params(1),
        name="mlp",
    )(xp2, xs2, modv, ng, w1, w2)
    return yp.reshape(xp.shape), ys.reshape(xs.shape)


def _log_sigmoid(x):
    return jnp.minimum(x, 0.0) - jnp.log(1.0 + jnp.exp(-jnp.abs(x)))


def _rot_half(x, first_half):
    return jnp.where(first_half, pltpu.roll(x, 96, axis=1), pltpu.roll(x, 32, axis=1))


def _values_and_ones(v):
    vt = jnp.transpose(v)
    row = lax.broadcasted_iota(jnp.int32, vt.shape, 0)
    return jnp.where(row < HEAD_DIM, vt, 1.0)


def _aligned(i, m):
    return i * m if isinstance(i, int) else pl.multiple_of(i * m, m)


def _drain(pieces):
    for _ in pieces:
        pass


def _each(body, n, static, unroll=1):
    if static:
        for i in range(n):
            yield from body(i)
    else:
        lax.fori_loop(0, n, lambda i, c: (_drain(body(i)), c)[1], 0, unroll=unroll)


def _alternate(*programs):
    live = list(programs)
    while live:
        for g in list(live):
            try:
                next(g)
            except StopIteration:
                live.remove(g)
            else:
                yield


def _odd_kernel(*refs, t, lc, latent, nseq):
    shared = set(range(1, 11 if latent else 9))
    n_rows = min(t, ODD_ROWS)
    joint = not latent and nseq > 1 and t == n_rows
    finals = []
    programs = [_odd_seq(*[r if k in shared else r.at[sq] for k, r in enumerate(refs)],
                         t=t, lc=lc, latent=latent, joint=joint, finals=finals) for sq in range(nseq)]
    if joint:
        parts = [next(g) for g in programs]
        h_all = jnp.concatenate([proj_h(0) for proj_h, _ in parts], axis=0)
        for idx, (weight, _) in enumerate(parts[0][1]):
            y = _dot(h_all, weight())
            for sq, (_, plan) in enumerate(parts):
                plan[idx][1](y[sq * n_rows:(sq + 1) * n_rows], 0)
    _run_staggered(programs, ODD_STAGGER)
    if joint:
        o = _dot(jnp.concatenate([z() for z, _ in finals], axis=0), refs[8][...])
        for sq, (_, residual) in enumerate(finals):
            residual(o[sq * n_rows:(sq + 1) * n_rows])


def _odd_seq(*refs, t, lc, latent, joint, finals):
    if latent:
        (x_ref, mod_ref, ng_ref, win_ref, wg_ref, gb_ref, sink_ref, hn_ref, wout_ref,
         cos_ref, sin_ref, kc_ref, vc_ref, cin_ref, nin_ref, min_ref,
         o_ref,
         qa_s, qb_s, kk_s, vvt_s, qma_s, qmb_s, qm_s, km_s, vmt_s, vmtf_s, om_s, g_s, z_s, hft_s, hbt_s,
         sr_s, st_s, qc_s, c_s, n_s, m_s, s_s, p_s, kc_s, vct_s) = refs
    else:
        (x_ref, mod_ref, ng_ref, win_ref, wg_ref, gb_ref, sink_ref, hn_ref, wout_ref,
         o_ref, ko_ref, vo_ref, cf_ref, cb_ref, no_ref, mo_ref,
         qa_s, qb_s, kk_s, vvt_s, qma_s, qmb_s, qm_s, km_s, vmt_s, vmtf_s, om_s, g_s, z_s, hft_s, hbt_s,
         sr_s, st_s, qc_s, c_s, n_s, m_s, s_s, p_s) = refs

    static = not latent
    n_blocks = t // CHUNK
    pad = CHUNK if latent else 0
    shift1, scale1, gate1 = mod_ref[0:1, :], mod_ref[1:2, :], mod_ref[2:3, :]
    lane = lax.broadcasted_iota(jnp.int32, (1, LANES), 1)
    left = lane < HEAD_DIM
    first_half = (lane % HEAD_DIM) < (HEAD_DIM // 2)
    ti = lax.broadcasted_iota(jnp.int32, (CHUNK, CHUNK), 0)
    si = lax.broadcasted_iota(jnp.int32, (CHUNK, CHUNK), 1)
    top = ti < HEAD_DIM
    same_head = top == (si < HEAD_DIM)

    n_rows = min(t, ODD_ROWS)
    if latent:
        for kv in range(N_KV):
            kk_s[kv, 0:CHUNK, :] = jnp.zeros((CHUNK, LANES), BF16)
            kk_s[kv, CHUNK + t:2 * CHUNK + t, :] = jnp.zeros((CHUNK, LANES), BF16)
            vvt_s[kv, :, 0:CHUNK] = jnp.zeros((LANES, CHUNK), BF16)
            vvt_s[kv, :, CHUNK + t:2 * CHUNK + t] = jnp.zeros((LANES, CHUNK), BF16)
            kc_s[kv] = kc_ref[kv].astype(BF16)
            vct_s[kv] = _values_and_ones(vc_ref[kv]).astype(BF16)

    base = D_HALF + 2 * LANES

    def proj_rows(rc):
        r0 = _aligned(rc, n_rows)
        return pl.ds(r0, n_rows), pl.ds(r0 + pad, n_rows)

    def proj_h(rc):
        rows, _ = proj_rows(rc)
        return _norm_mod(x_ref[rows, :], ng_ref[0:1, :], shift1, scale1).astype(BF16)

    def rope(v, rows):
        return v * cos_ref[rows, :] + _rot_half(v, first_half) * sin_ref[rows, :]

    def use_q(q, rc):
        rows, _ = proj_rows(rc)
        for p in range(N_PAIRS):
            cols = slice(p * LANES, (p + 1) * LANES)
            qp = q[:, cols]
            if latent:
                qp = rope(qp, rows)
            qp = qp * ATT_SCALE
            qa_s[rows, cols] = jnp.where(left, qp, 0.0).astype(BF16)
            qb_s[rows, cols] = jnp.where(left, 0.0, qp).astype(BF16)

    def use_kv(kv2, rc):
        rows, krows = proj_rows(rc)
        ka, va = kv2[:, 0:LANES], kv2[:, LANES:2 * LANES]
        if latent:
            ka = rope(ka, rows)
        kr = pltpu.roll(ka, HEAD_DIM, axis=1)
        vr = pltpu.roll(va, HEAD_DIM, axis=1)
        if not latent:
            ka_t, va_t = jnp.transpose(ka), jnp.transpose(va)
            for kv in range(N_KV):
                ko_ref[0, kv, :, rows] = ka_t[kv * HEAD_DIM:(kv + 1) * HEAD_DIM, :]
                vo_ref[0, kv, :, rows] = va_t[kv * HEAD_DIM:(kv + 1) * HEAD_DIM, :]
        kk_s[0, krows, :] = jnp.where(left, ka, kr).astype(BF16)
        kk_s[1, krows, :] = jnp.where(left, kr, ka).astype(BF16)
        vvt_s[0, :, krows] = _values_and_ones(va).astype(BF16)
        vvt_s[1, :, krows] = _values_and_ones(vr).astype(BF16)

    def use_qm(qm, rc):
        rows, _ = proj_rows(rc)
        qm_s[rows, :] = qm.astype(BF16)
        for p in range(N_PAIRS):
            cols = slice(p * LANES, (p + 1) * LANES)
            qma_s[rows, cols] = jnp.where(left, qm[:, cols], 0.0).astype(BF16)
            qmb_s[rows, cols] = jnp.where(left, 0.0, qm[:, cols]).astype(BF16)

    def use_vm(vm, rc):
        rows, _ = proj_rows(rc)
        for p in range(N_PAIRS):
            cols = slice(p * LANES, (p + 1) * LANES)
            vt = jnp.transpose(vm[:, cols])
            vmtf_s[cols, rows] = vt
            vmt_s[cols, rows] = vt.astype(BF16)

    def use_km(km, rc):
        km_s[proj_rows(rc)[0], :] = (km * (HEAD_DIM ** -0.5)).astype(BF16)

    def use_om(om, rc):
        om_s[proj_rows(rc)[0], :] = om

    def use_g(g, rc):
        g_s[proj_rows(rc)[0], :] = g + gb_ref[...]

    def columns(start, width):
        return lambda: win_ref[:, start:start + width]

    proj_plan = [(columns(0, D_HALF), use_q), (columns(D_HALF, 2 * LANES), use_kv),
                 (columns(base, D_HALF), use_qm), (columns(base + 2 * D_HALF, D_HALF), use_vm),
                 (columns(base + D_HALF, D_HALF), use_km), (columns(base + 3 * D_HALF, D_HALF), use_om),
                 (lambda: wg_ref[...], use_g)]

    def project(rc):
        h = proj_h(rc)
        for weight, use in proj_plan:
            use(_dot(h, weight()), rc)
            yield

    if joint:
        yield proj_h, proj_plan
    else:
        yield from _each(project, t // n_rows, static)

    kj = lax.broadcasted_iota(jnp.int32, (3 * CHUNK, 2 * CHUNK), 0)
    qi = lax.broadcasted_iota(jnp.int32, (3 * CHUNK, 2 * CHUNK), 1) % CHUNK
    band_ok = jnp.abs(kj - CHUNK - qi) <= WINDOW
    head_a = lax.broadcasted_iota(jnp.int32, (1, 2 * CHUNK), 1) < CHUNK

    n_keys = _att_keys(lc, latent)
    group = _att_group(latent)

    def attention():
        def attend(i):
            r0 = _aligned(i, CHUNK)
            rows = pl.ds(r0, CHUNK)
            if latent:
                key_pos = kj + (i - 1) * CHUNK
                mask = band_ok & (key_pos >= 0) & (key_pos < t)
                win = pl.ds(r0, 3 * CHUNK)
            for g0 in range(0, N_PAIRS, group):
                for gi in range(group):
                    p = g0 + gi
                    kv = p // (N_PAIRS // N_KV)
                    cols = slice(p * LANES, (p + 1) * LANES)
                    q2 = jnp.concatenate([qa_s[rows, cols], qb_s[rows, cols]], axis=0)
                    if latent:
                        s_s[gi, 0:lc, :] = _dot_nt(kc_s[kv], q2)
                        s_s[gi, lc:n_keys, :] = jnp.where(mask, _dot_nt(kk_s[kv, win, :], q2), -jnp.inf)
                    else:
                        s_s[gi] = _dot_nt(kk_s[kv], q2)
                yield
                maxes = []
                for gi in range(group):
                    p = g0 + gi
                    sink = jnp.where(head_a, sink_ref[2 * p:2 * p + 1, 0:1], sink_ref[2 * p + 1:2 * p + 2, 0:1])
                    mx = jnp.maximum(jnp.max(s_s[gi], axis=0, keepdims=True), sink)
                    p_s[gi] = jnp.exp(s_s[gi] - mx).astype(BF16)
                    maxes.append((sink, mx))
                yield
                for gi in range(group):
                    p = g0 + gi
                    kv = p // (N_PAIRS // N_KV)
                    cols = slice(p * LANES, (p + 1) * LANES)
                    if latent:
                        num = (_dot(vct_s[kv], p_s[gi, 0:lc, :]) + _dot(vvt_s[kv, :, win], p_s[gi, lc:n_keys, :]))
                    else:
                        num = _dot(vvt_s[kv], p_s[gi])
                    sink, mx = maxes[gi]
                    den = num[HEAD_DIM:HEAD_DIM + 1, :] + jnp.exp(sink - mx)
                    out = num[0:HEAD_DIM, :] * (1.0 / den)
                    pair = jnp.concatenate([out[:, 0:CHUNK], out[:, CHUNK:2 * CHUNK]], axis=0)
                    z_s[rows, cols] = jnp.transpose(pair).astype(BF16)
                yield

        return attend

    if latent:
        c_s[...] = cin_ref[...]
        n_s[...] = nin_ref[...]
        m_s[...] = min_ref[...]
    else:
        c_s[...] = jnp.zeros(c_s.shape, F32)
        n_s[...] = jnp.zeros(n_s.shape, F32)
        m_s[...] = jnp.zeros(m_s.shape, F32)

    see = (ti <= si, ti >= si)
    tri = tuple(m.astype(F32).astype(BF16) for m in see)
    last = (CHUNK - 1, 0)

    def mlstm(i):
        offs = (_aligned(i, CHUNK), _aligned(n_blocks - 1 - i, CHUNK))
        half = (slice(0, LANES), slice(LANES, 2 * LANES))
        a_rows, b_rows = [], []
        for d in range(2):
            gt = jnp.transpose(g_s[pl.ds(offs[d], CHUNK), :])
            lf = _log_sigmoid(gt[2 * N_HEADS:4 * N_HEADS, :])
            bcum = _dot_exact_lhs(lf, tri[d])[8 * d:8 * d + 8, :]
            a_rows.append(gt[8 * d:8 * d + 8, :] - bcum)
            b_rows.append(bcum)
        yield
        q_ns = []
        for d in range(2):
            rows = pl.ds(offs[d], CHUNK)
            for p in range(N_PAIRS):
                u = d * N_PAIRS + p
                cols = slice(p * LANES, (p + 1) * LANES)
                q2 = jnp.concatenate([qma_s[rows, cols], qmb_s[rows, cols]], axis=0)
                sr_s[u] = _dot_nt(km_s[rows, cols], q2)
                qc_s[u] = _dot_nt(c_s[u].astype(BF16), qm_s[rows, cols])
                n8 = jnp.broadcast_to(n_s[u:u + 1, :], (8, LANES)).astype(BF16)
                q_ns.append(_dot_nt(n8, q2)[0:1, :])
            yield
        stats = {}
        for d in range(2):
            for hd in range(N_HEADS):
                u, j = d * N_PAIRS + hd // 2, hd % 2
                mrow = d * N_HEADS + hd
                a_row = a_rows[d][hd:hd + 1, :]
                b_row = b_rows[d][hd:hd + 1, :]
                m_prev = m_s[mrow:mrow + 1, 0:1]
                a_col = jnp.transpose(jnp.broadcast_to(a_row, (CHUNK, CHUNK)))
                z_t = jnp.where(see[d], a_col, -jnp.inf)
                m_run = jnp.maximum(jnp.max(z_t, axis=0, keepdims=True), m_prev)
                s_t = sr_s[u, :, half[j]] * jnp.exp(z_t - m_run)
                st_s[u, :, half[j]] = s_t.astype(BF16)
                w_int = jnp.exp(m_prev - m_run)
                den = jnp.sum(s_t, axis=0, keepdims=True) + w_int * q_ns[u][:, half[j]]
                inv = 1.0 / jnp.maximum(jnp.abs(den), jnp.exp(-(b_row + m_run)))
                m_last = m_run[:, last[d]:last[d] + 1]
                stats[(u, j)] = (w_int, inv, jnp.exp(a_row - m_last), jnp.exp(m_prev - m_last))
                m_s[mrow:mrow + 1, :] = jnp.broadcast_to(b_row[:, last[d]:last[d] + 1] + m_last, (1, LANES))
            yield
        for d in range(2):
            rows = pl.ds(offs[d], CHUNK)
            ht_s = (hft_s, hbt_s)[d]
            for p in range(N_PAIRS):
                u = d * N_PAIRS + p
                cols = slice(p * LANES, (p + 1) * LANES)
                (w_a, inv_a, e_a, dec_a), (w_b, inv_b, e_b, dec_b) = stats[(u, 0)], stats[(u, 1)]
                kp = km_s[rows, cols]
                num2 = _dot(vmt_s[cols, rows], st_s[u])
                num = jnp.where(top, num2[:, half[0]], num2[:, half[1]])
                ht_s[cols, rows] = (num + jnp.where(top, w_a, w_b) * qc_s[u]) * jnp.where(top, inv_a, inv_b)
                vt_e = (vmtf_s[cols, rows] * jnp.where(top, e_a, e_b)).astype(BF16)
                c_s[u] = jnp.where(top, dec_a, dec_b) * c_s[u] + jnp.where(same_head, _dot(vt_e, kp), 0.0)
                e2 = jnp.concatenate([e_a, e_b, jnp.zeros((6, CHUNK), F32)], axis=0).astype(BF16)
                n_k = _dot(e2, kp)
                n_s[u:u + 1, :] = (jnp.where(left, dec_a, dec_b) * n_s[u:u + 1, :]
                                   + jnp.where(left, n_k[0:1, :], n_k[1:2, :]))
            yield

    attend = attention()
    yield from _each(lambda i: _alternate(attend(i), mlstm(i)), n_blocks, static, unroll=2)

    if not latent:
        for d, c_ref in enumerate((cf_ref, cb_ref)):
            for p in range(N_PAIRS):
                c_pair = jnp.transpose(c_s[d * N_PAIRS + p])
                c_ref[0, 2 * p] = c_pair[0:HEAD_DIM, 0:HEAD_DIM]
                c_ref[0, 2 * p + 1] = c_pair[HEAD_DIM:LANES, HEAD_DIM:LANES]
        no_ref[...] = n_s[...]
        mo_ref[...] = m_s[...]
        yield

    top_w = lax.broadcasted_iota(jnp.int32, (LANES, n_rows), 0) < HEAD_DIM

    def finish(rc):
        r0 = _aligned(rc, n_rows)
        rows = pl.ds(r0, n_rows)
        for p in range(N_PAIRS):
            cols = slice(p * LANES, (p + 1) * LANES)
            hm = hft_s[cols, rows] + hbt_s[cols, rows]
            sq = hm * hm
            ms_a = jnp.sum(sq[0:HEAD_DIM], axis=0, keepdims=True)
            ms_b = jnp.sum(sq[HEAD_DIM:LANES], axis=0, keepdims=True)
            ms = jnp.where(top_w, ms_a, ms_b) * (1.0 / HEAD_DIM)
            y = jnp.transpose(hm * lax.rsqrt(ms + EPS)) * hn_ref[:, cols] * jax.nn.sigmoid(om_s[rows, cols])
            z_s[rows, D_HALF + p * LANES:D_HALF + (p + 1) * LANES] = y.astype(BF16)
            if p % 2:
                yield
        def residual(o):
            o_ref[rows, :] = x_ref[rows, :] + gate1 * _rms(o, ng_ref[1:2, :])

        if joint:
            finals.append((lambda: z_s[rows, :], residual))
        else:
            residual(_dot(z_s[rows, :], wout_ref[...]))
        yield

    yield from _each(finish, t // n_rows, static)


def _att_keys(lc, latent):
    return lc + 3 * CHUNK if latent else lc


def _att_group(latent):
    return N_PAIRS


def _odd_scratch(t, lc, latent, nseq):
    pad = 2 * CHUNK if latent else 0
    att = (_att_group(latent), _att_keys(lc, latent), 2 * CHUNK)
    shapes = [
        pltpu.VMEM((t, D_HALF), BF16),
        pltpu.VMEM((t, D_HALF), BF16),
        pltpu.VMEM((N_KV, t + pad, LANES), BF16),
        pltpu.VMEM((N_KV, LANES, t + pad), BF16),
        pltpu.VMEM((t, D_HALF), BF16),
        pltpu.VMEM((t, D_HALF), BF16),
        pltpu.VMEM((t, D_HALF), BF16),
        pltpu.VMEM((t, D_HALF), BF16),
        pltpu.VMEM((D_HALF, t), BF16),
        pltpu.VMEM((D_HALF, t), F32),
        pltpu.VMEM((t, D_HALF), F32),
        pltpu.VMEM((t, LANES), F32),
        pltpu.VMEM((t, D_MODEL), BF16),
        pltpu.VMEM((D_HALF, t), F32),
        pltpu.VMEM((D_HALF, t), F32),
        pltpu.VMEM((2 * N_PAIRS, CHUNK, 2 * LANES), F32),
        pltpu.VMEM((2 * N_PAIRS, CHUNK, 2 * LANES), BF16),
        pltpu.VMEM((2 * N_PAIRS, LANES, CHUNK), F32),
        pltpu.VMEM((2 * N_PAIRS, LANES, LANES), F32),
        pltpu.VMEM((2 * N_PAIRS, LANES), F32),
        pltpu.VMEM((2 * N_HEADS, LANES), F32),
        pltpu.VMEM(att, F32),
        pltpu.VMEM(att, BF16),
    ]
    if latent:
        shapes += [pltpu.VMEM((N_KV, lc, LANES), BF16), pltpu.VMEM((N_KV, LANES, lc), BF16)]
    return [pltpu.VMEM((nseq,) + tuple(sh.shape), sh.dtype) for sh in shapes]


def _odd_common_specs(t, layer, cond_base, cond_stride, nseq):
    assert cond_stride == 0 or nseq == 1
    j = layer // 2
    return [
        _per_seq((t, D_MODEL), nseq),
        _mod_spec(layer, lambda i: cond_base + cond_stride * i),
        _layer_spec((4, D_MODEL), layer),
        _layer_spec((D_MODEL, D_IN_ODD), j),
        _const_spec((D_MODEL, LANES)),
        _const_spec((1, LANES)),
        _const_spec((N_HEADS, LANES)),
        _layer_spec((1, D_HALF), j),
        _layer_spec((D_MODEL, D_MODEL), j),
    ]


def _per_seq(shape, nseq):
    return pl.BlockSpec((nseq,) + shape, lambda i: (i,) + (0,) * len(shape))


def _odd_context(x, modv, layer, ng, w_main, w_gate, gate_bias, sink_b, hnorm, wout):
    b, t, _ = x.shape
    nseq = ODD_CTX_SEQS
    assert b % nseq == 0
    kern = functools.partial(_odd_kernel, t=t, lc=t, latent=False, nseq=nseq)
    per_seq = functools.partial(_per_seq, nseq=nseq)
    return pl.pallas_call(
        kern,
        grid=(b // nseq,),
        in_specs=_odd_common_specs(t, layer, 0, 0, nseq),
        out_specs=[per_seq((t, D_MODEL)),
                   per_seq((1, N_KV, HEAD_DIM, t)), per_seq((1, N_KV, HEAD_DIM, t)),
                   per_seq((1, N_HEADS, HEAD_DIM, HEAD_DIM)), per_seq((1, N_HEADS, HEAD_DIM, HEAD_DIM)),
                   per_seq((2 * N_PAIRS, LANES)), per_seq((2 * N_HEADS, LANES))],
        out_shape=[jax.ShapeDtypeStruct((b, t, D_MODEL), F32),
                   jax.ShapeDtypeStruct((b, 1, N_KV, HEAD_DIM, t), F32),
                   jax.ShapeDtypeStruct((b, 1, N_KV, HEAD_DIM, t), F32),
                   jax.ShapeDtypeStruct((b, 1, N_HEADS, HEAD_DIM, HEAD_DIM), F32),
                   jax.ShapeDtypeStruct((b, 1, N_HEADS, HEAD_DIM, HEAD_DIM), F32),
                   jax.ShapeDtypeStruct((b, 2 * N_PAIRS, LANES), F32),
                   jax.ShapeDtypeStruct((b, 2 * N_HEADS, LANES), F32)],
        scratch_shapes=_odd_scratch(t, t, False, nseq),
        compiler_params=_params(1),
        name="odd_mixer_context",
    )(x, modv, ng, w_main, w_gate, gate_bias, sink_b, hnorm, wout)


def _odd_latent(x, modv, layer, ng, w_main, w_gate, gate_bias, sink_b, hnorm, wout, cos_t, sin_t, kc, vc,
                c_in, n_in, m_in):
    b, t, _ = x.shape
    lc = kc.shape[2]
    kern = functools.partial(_odd_kernel, t=t, lc=lc, latent=True, nseq=1)
    per_seq = functools.partial(_per_seq, nseq=1)
    return pl.pallas_call(
        kern,
        grid=(b,),
        in_specs=_odd_common_specs(t, layer, 1, 1, 1) + [
            _const_spec((t, LANES)), _const_spec((t, LANES)),
            per_seq((N_KV, lc, LANES)), per_seq((N_KV, lc, LANES)),
            per_seq((2 * N_PAIRS, LANES, LANES)), per_seq((2 * N_PAIRS, LANES)), per_seq((2 * N_HEADS, LANES)),
        ],
        out_specs=per_seq((t, D_MODEL)),
        out_shape=jax.ShapeDtypeStruct((b, t, D_MODEL), F32),
        scratch_shapes=_odd_scratch(t, lc, True, 1),
        compiler_params=_params(1),
        name="odd_mixer_latent",
    )(x, modv, ng, w_main, w_gate, gate_bias, sink_b, hnorm, wout, cos_t, sin_t, kc, vc, c_in, n_in, m_in)


def _rope_tables(t):
    rows = t // GRID_W
    row = jnp.broadcast_to(jnp.arange(rows)[:, None], (rows, GRID_W)).reshape(t).astype(F32)
    col = jnp.broadcast_to(jnp.arange(GRID_W)[None, :], (rows, GRID_W)).reshape(t).astype(F32)
    n_freq = HEAD_DIM // 4
    inv_freq = ROPE_BASE ** (-jnp.arange(n_freq, dtype=F32) / n_freq)
    ang = jnp.concatenate([row[:, None] * inv_freq, col[:, None] * inv_freq], axis=-1)
    cos, sin = jnp.cos(ang), jnp.sin(ang)
    cos_l = jnp.tile(cos, (1, LANES // cos.shape[1]))
    sin_l = jnp.tile(jnp.concatenate([-sin, sin], axis=-1), (1, LANES // HEAD_DIM))
    return cos_l, sin_l


def _pair_blockdiag(c):
    b = c.shape[0]
    c = c.reshape(b, N_PAIRS, 2, HEAD_DIM, HEAD_DIM)
    z = jnp.zeros_like(c[:, :, 0])
    top = jnp.concatenate([c[:, :, 0], z], axis=-1)
    bot = jnp.concatenate([z, c[:, :, 1]], axis=-1)
    return jnp.concatenate([top, bot], axis=-2)


def _lane_bcast(v):
    return jnp.broadcast_to(v[..., None], v.shape + (LANES,))


def kernel(x_prompt, x_sample, c, cache_k, cache_v, state_c_fwd, state_n_fwd, state_m_fwd, state_c_bwd, state_n_bwd, state_m_bwd, c_ctx, mod_w, mod_b, norm_g, mlp_w1, mlp_w2, even_in_w, conv_a_w, conv_a_b, ln_a_g, ln_a_b, conv_b_w, even_out_w, odd_in_w, attn_sink, gate_b, hnorm_g, odd_out_w):
    n_dec = x_sample.shape[0]
    n_ctx = x_prompt.shape[0]
    cond = jnp.concatenate([c_ctx[None, :], c, jnp.zeros((COND_ROWS - 1 - n_dec, D_MODEL), F32)], axis=0)
    modv = _modulation(cond, mod_w, mod_b)

    yp, ys = x_prompt, x_sample
    w1, w2 = mlp_w1, mlp_w2

    ev = (norm_g, even_in_w.astype(BF16), conv_a_w, conv_a_b[:, None, :], ln_a_g[:, None, :], ln_a_b[:, None, :],
          conv_b_w, even_out_w.astype(BF16))
    yp = _even_layer(yp, modv, 0, 0, 0, *ev)
    ys = _even_layer(ys, modv, 0, 1, 1, *ev)
    yp, ys = _mlp_layer(yp, ys, modv, 0, norm_g, w1, w2)

    order = jnp.array([0, 2, 1, 3])
    d_main = D_IN_ODD - 4 * N_HEADS
    wg = odd_in_w[0][:, d_main:].reshape(D_MODEL, 4, N_HEADS)[:, order, :].reshape(D_MODEL, 4 * N_HEADS)
    w_gate = jnp.pad(wg, ((0, 0), (0, LANES - 4 * N_HEADS))).astype(BF16)
    gate_bias = jnp.pad(gate_b[0][order, :].reshape(1, 4 * N_HEADS), ((0, 0), (0, LANES - 4 * N_HEADS)))
    sink_b = _lane_bcast(attn_sink[0])
    odd = (1, norm_g, odd_in_w.astype(BF16), w_gate, gate_bias, sink_b, hnorm_g[:, None, :], odd_out_w.astype(BF16))

    op, k_t, v_t, c_f, c_b, n_new, m_new = _odd_context(yp, modv, *odd)
    new_k, new_v = jnp.swapaxes(k_t, -1, -2), jnp.swapaxes(v_t, -1, -2)

    t_dec = x_sample.shape[1]
    cos_t, sin_t = _rope_tables(t_dec)
    kc = jnp.concatenate([cache_k[:, 0], cache_k[:, 0]], axis=-1)
    vc = jnp.concatenate([cache_v[:, 0], cache_v[:, 0]], axis=-1)
    c_in = jnp.concatenate([_pair_blockdiag(jnp.swapaxes(state_c_fwd[:, 0], -1, -2)),
                            _pair_blockdiag(jnp.swapaxes(state_c_bwd[:, 0], -1, -2))], axis=1)
    n_in = jnp.concatenate([state_n_fwd[:, 0].reshape(n_dec, N_PAIRS, LANES),
                            state_n_bwd[:, 0].reshape(n_dec, N_PAIRS, LANES)], axis=1)
    m_in = _lane_bcast(jnp.concatenate([state_m_fwd[:, 0], state_m_bwd[:, 0]], axis=1))
    os_ = _odd_latent(ys, modv, *odd, cos_t, sin_t, kc, vc, c_in, n_in, m_in)

    yp, ys = _mlp_layer(op, os_, modv, 1, norm_g, w1, w2)

    n_f = n_new[:, :N_PAIRS].reshape(n_ctx, N_HEADS, HEAD_DIM)[:, None]
    n_b = n_new[:, N_PAIRS:].reshape(n_ctx, N_HEADS, HEAD_DIM)[:, None]
    m_f = m_new[:, :N_HEADS, 0][:, None]
    m_b = m_new[:, N_HEADS:, 0][:, None]
    return (yp, ys, new_k, new_v, c_f, n_f, m_f, c_b, n_b, m_b)
```
